```python
import math
import jax, jax.numpy as jnp
from jax import lax
import numpy as np

D_MODEL = 1024
BATCH = 8
SEQ = 4096
DEPTH = 2

N_EVEN = (DEPTH + 1) // 2
N_ODD = DEPTH // 2

A_WIDTH = D_MODEL // 2
A_GROUPS = 4
A_GROUP_DIM = A_WIDTH // A_GROUPS
A_CHUNK = 128
B_WIDTH = D_MODEL // 2
B_HEADS = 4
B_HEAD_DIM = B_WIDTH // B_HEADS
B_CHUNK = 128
B_CONV = 4
EVEN_IN = 2 * A_WIDTH + 2 * B_WIDTH
EVEN_MIX = A_WIDTH + B_WIDTH
C_HEAD_DIM = 64
C_HEADS = D_MODEL // C_HEAD_DIM
C_KV_HEADS = 4
C_WINDOW = 128
C_BLOCK = 128
C_Q = C_HEADS * C_HEAD_DIM
C_KV = C_KV_HEADS * C_HEAD_DIM
ROPE_THETA = 10000.0
MEM_LEN = 256
X_HEADS = 4
X_HEAD_DIM = D_MODEL // X_HEADS
N_GROUPS = 4
EXPERTS_PER_GROUP = 8
N_EXPERTS = N_GROUPS * EXPERTS_PER_GROUP
TOP_K = 2
D_EXPERT = D_MODEL // 2
MOE_BLOCK = 128
DN_ALPHA = (2 * DEPTH) ** 0.25
DN_BETA = (8 * DEPTH) ** -0.25
LN_EPS = 1e-5

kernel_name = 'hybrid_gmlp_mlstm_swa_hmoe'


def layer_norm(x, g, b):
    xf = x.astype(jnp.float32)
    xc = xf - xf.mean(-1, keepdims=True)
    var = (xc * xc).mean(-1, keepdims=True)
    return xc * lax.rsqrt(var + LN_EPS) * g + b


def rotary_tables(seq):
    inv = ROPE_THETA ** (-jnp.arange(0, C_HEAD_DIM, 2, dtype=jnp.float32) / C_HEAD_DIM)
    ang = jnp.arange(seq, dtype=jnp.float32)[:, None] * inv[None, :]
    return jnp.cos(ang), jnp.sin(ang)


def apply_rotary(t, cos, sin):
    t = t.astype(jnp.float32)
    t1, t2 = jnp.split(t, 2, axis=-1)
    c = cos[None, :, None, :]
    s = sin[None, :, None, :]
    return jnp.concatenate([t1 * c - t2 * s, t2 * c + t1 * s], axis=-1)


def causal_depthwise_conv(x, w):
    return lax.conv_general_dilated(
        x, w.astype(x.dtype)[:, None, :], window_strides=(1,),
        padding=[(w.shape[0] - 1, 0)], dimension_numbers=('NWC', 'WIO', 'NWC'),
        feature_group_count=x.shape[-1])


def chunked_gmlp(u, v, ln_g, ln_b, w_s, b_s):
    bsz, seq, _ = u.shape
    nc = seq // A_CHUNK
    vn = layer_norm(v, ln_g, ln_b).reshape(bsz, nc, A_CHUNK, A_GROUPS, A_GROUP_DIM)
    causal = jnp.tril(jnp.ones((A_CHUNK, A_CHUNK), dtype=bool))
    w = jnp.where(causal[None], w_s, 0.0)
    mixed = jnp.einsum('gts,bcsgd->bctgd', w, vn) + b_s.T[None, None, :, :, None]
    return u * mixed.reshape(bsz, seq, A_WIDTH)


def mlstm_chunkwise(q, k, v, ig, lf):
    bsz, nh, seq, dh = q.shape
    L = B_CHUNK
    nc = seq // L

    def to_chunks(t):
        return jnp.moveaxis(t.reshape(bsz, nh, nc, L, *t.shape[3:]), 2, 0)

    causal = jnp.tril(jnp.ones((L, L), dtype=bool))

    def step(carry, inp):
        C, n, m = carry
        qc, kc, vc, igc, lfc = inp
        b = jnp.cumsum(lfc, axis=-1)
        dmat = jnp.where(causal, b[..., :, None] - b[..., None, :] + igc[..., None, :], -jnp.inf)
        inter = b + m[..., None]
        mt = jnp.maximum(dmat.max(-1), inter)
        w_intra = jnp.exp(dmat - mt[..., None])
        w_state = jnp.exp(inter - mt)
        s = jnp.einsum('bhtd,bhsd->bhts', qc, kc) * w_intra
        num = jnp.einsum('bhts,bhsd->bhtd', s, vc) + w_state[..., None] * jnp.einsum('bhvk,bhtk->bhtv', C, qc)
        nq = s.sum(-1) + w_state * jnp.einsum('bhk,bhtk->bht', n, qc)
        h = num / jnp.maximum(jnp.abs(nq), jnp.exp(-mt))[..., None]
        bl = b[..., -1]
        g = bl[..., None] - b + igc
        m_new = jnp.maximum(bl + m, g.max(-1))
        decay = jnp.exp(bl + m - m_new)
        wg = jnp.exp(g - m_new[..., None])
        C_new = decay[..., None, None] * C + jnp.einsum('bhs,bhsv,bhsk->bhvk', wg, vc, kc)
        n_new = decay[..., None] * n + jnp.einsum('bhs,bhsk->bhk', wg, kc)
        return (C_new, n_new, m_new), h

    init = (jnp.zeros((bsz, nh, dh, dh), jnp.float32),
            jnp.zeros((bsz, nh, dh), jnp.float32),
            jnp.zeros((bsz, nh), jnp.float32))
    _, hs = lax.scan(step, init, (to_chunks(q), to_chunks(k), to_chunks(v), to_chunks(ig), to_chunks(lf)))
    return jnp.moveaxis(hs, 0, 2).reshape(bsz, nh, seq, dh)


def mlstm_mixer(xm, z, conv_w, conv_b, wq, wk, wv, w_if, b_if, norm_w, skip):
    bsz, seq, _ = xm.shape
    H, dh = B_HEADS, B_HEAD_DIM
    xc = jax.nn.silu(causal_depthwise_conv(xm, conv_w) + conv_b)
    xc_h = xc.reshape(bsz, seq, H, dh)
    xm_h = xm.reshape(bsz, seq, H, dh)
    q = jnp.einsum('bshd,hde->bshe', xc_h, wq)
    k = jnp.einsum('bshd,hde->bshe', xc_h, wk)
    v = jnp.einsum('bshd,hde->bshe', xm_h, wv)
    gate_in = jnp.concatenate([q.reshape(bsz, seq, B_WIDTH), k.reshape(bsz, seq, B_WIDTH),
                               v.reshape(bsz, seq, B_WIDTH)], axis=-1)
    gates = (gate_in @ w_if + b_if).astype(jnp.float32)
    ig = jnp.transpose(gates[..., :H], (0, 2, 1))
    lf = jnp.transpose(jax.nn.log_sigmoid(gates[..., H:]), (0, 2, 1))

    def to_bhsd(t):
        return jnp.transpose(t, (0, 2, 1, 3)).astype(jnp.float32)

    h = mlstm_chunkwise(to_bhsd(q), to_bhsd(k) * dh ** -0.5, to_bhsd(v), ig, lf)
    h = jnp.transpose(h, (0, 2, 1, 3))
    hc = h - h.mean(-1, keepdims=True)
    hn = hc * lax.rsqrt((hc * hc).mean(-1, keepdims=True) + LN_EPS) * norm_w.reshape(H, dh)
    return (hn.reshape(bsz, seq, B_WIDTH) + skip * xc) * jax.nn.silu(z)


def even_mixer(x, w_in, gm_ln_g, gm_ln_b, gm_ws, gm_bs, conv_w, conv_b, wq, wk, wv, w_if, b_if,
               norm_w, skip, w_out):
    proj = x @ w_in
    a_u = jax.nn.gelu(proj[..., :A_WIDTH])
    a_v = jax.nn.gelu(proj[..., A_WIDTH:2 * A_WIDTH])
    b_x = proj[..., 2 * A_WIDTH:2 * A_WIDTH + B_WIDTH]
    b_z = proj[..., 2 * A_WIDTH + B_WIDTH:]
    y_a = chunked_gmlp(a_u, a_v, gm_ln_g, gm_ln_b, gm_ws, gm_bs)
    y_b = mlstm_mixer(b_x, b_z, conv_w, conv_b, wq, wk, wv, w_if, b_if, norm_w, skip)
    return jnp.concatenate([y_a, y_b], axis=-1) @ w_out


def sliding_window_gqa(q, k, v, sinks):
    bsz, seq = q.shape[0], q.shape[1]
    L = C_BLOCK
    nb = seq // L
    G = C_HEADS // C_KV_HEADS
    qb = q.reshape(bsz, nb, L, C_KV_HEADS, G, C_HEAD_DIM)

    def band(t):
        tb = t.reshape(bsz, nb, L, C_KV_HEADS, C_HEAD_DIM)
        prev = jnp.pad(tb, ((0, 0), (1, 0), (0, 0), (0, 0), (0, 0)))[:, :-1]
        return jnp.concatenate([prev, tb], axis=2)

    kb, vb = band(k), band(v)
    s = jnp.einsum('bnqhgd,bnkhd->bnhgqk', qb, kb).astype(jnp.float32) * C_HEAD_DIM ** -0.5
    qi = jnp.arange(L)[:, None] + L
    kj = jnp.arange(2 * L)[None, :]
    blk = jnp.arange(nb)[:, None, None]
    valid = (kj <= qi) & (qi - kj < C_WINDOW) & (blk * L - L + kj >= 0)
    s = jnp.where(valid[None, :, None, None], s, -jnp.inf)
    sink = sinks.astype(jnp.float32).reshape(1, 1, C_KV_HEADS, G, 1, 1)
    m = jnp.maximum(s.max(-1, keepdims=True), sink)
    p = jnp.exp(s - m)
    p = p / (p.sum(-1, keepdims=True) + jnp.exp(sink - m))
    o = jnp.einsum('bnhgqk,bnkhd->bnqhgd', p, vb)
    return o.reshape(bsz, seq, C_Q)


def swa_mixer(x, w_qkv, b_qkv, sinks, w_o, cos, sin):
    bsz, seq, _ = x.shape
    qkv = x @ w_qkv + b_qkv
    q = qkv[..., :C_Q].reshape(bsz, seq, C_HEADS, C_HEAD_DIM)
    k = qkv[..., C_Q:C_Q + C_KV].reshape(bsz, seq, C_KV_HEADS, C_HEAD_DIM)
    v = qkv[..., C_Q + C_KV:].reshape(bsz, seq, C_KV_HEADS, C_HEAD_DIM)
    q = apply_rotary(q, cos, sin)
    k = apply_rotary(k, cos, sin)
    return sliding_window_gqa(q, k, v, sinks) @ w_o


def memory_cross_attention(x, mem, wq, wkv, wo):
    bsz, seq, _ = x.shape
    m_len = mem.shape[1]
    q = (x @ wq).reshape(bsz, seq, X_HEADS, X_HEAD_DIM)
    kv = mem @ wkv
    k = kv[..., :D_MODEL].reshape(bsz, m_len, X_HEADS, X_HEAD_DIM)
    v = kv[..., D_MODEL:].reshape(bsz, m_len, X_HEADS, X_HEAD_DIM)
    s = jnp.einsum('bshd,bmhd->bhsm', q, k).astype(jnp.float32) * X_HEAD_DIM ** -0.5
    p = jax.nn.softmax(s, axis=-1)
    o = jnp.einsum('bhsm,bmhd->bshd', p, v).reshape(bsz, seq, D_MODEL)
    return o @ wo


def hierarchical_moe(x, w_rg, b_rg, w_re, b_re, w1, w3, w2):
    bsz, seq, d = x.shape
    n_tok = bsz * seq
    xf = x.reshape(n_tok, d)
    lg = (xf @ w_rg + b_rg).astype(jnp.float32)
    pg = jax.nn.softmax(lg, axis=-1)
    gi = jnp.argmax(lg, axis=-1)
    gate_g = jnp.take_along_axis(pg, gi[:, None], axis=-1)[:, 0]
    le = (xf @ w_re + b_re).astype(jnp.float32).reshape(n_tok, N_GROUPS, EXPERTS_PER_GROUP)
    le_sel = jnp.take_along_axis(le, gi[:, None, None], axis=1)[:, 0]
    top_v, top_i = lax.top_k(le_sel, TOP_K)
    pe = jax.nn.softmax(top_v, axis=-1)
    eid = (gi[:, None] * EXPERTS_PER_GROUP + top_i).reshape(-1)
    wts = (gate_g[:, None] * pe).reshape(-1)
    tok = jnp.repeat(jnp.arange(n_tok, dtype=jnp.int32), TOP_K)
    n_asg = n_tok * TOP_K
    order = jnp.argsort(eid)
    se, stok, sw = eid[order], tok[order], wts[order]
    counts = jnp.bincount(eid, length=N_EXPERTS)
    starts = jnp.cumsum(counts) - counts
    pcounts = (counts + MOE_BLOCK - 1) // MOE_BLOCK * MOE_BLOCK
    pends = jnp.cumsum(pcounts)
    pstarts = pends - pcounts
    dest = pstarts[se] + jnp.arange(n_asg, dtype=jnp.int32) - starts[se]
    n_pad = n_asg + N_EXPERTS * MOE_BLOCK
    n_blk = n_pad // MOE_BLOCK
    buf_tok = jnp.full((n_pad,), n_tok, jnp.int32).at[dest].set(stok)
    buf_w = jnp.zeros((n_pad,), jnp.float32).at[dest].set(sw)
    blk_exp = jnp.minimum(jnp.searchsorted(pends, jnp.arange(n_blk, dtype=jnp.int32) * MOE_BLOCK, side='right'),
                          N_EXPERTS - 1)
    xpad = jnp.concatenate([xf, jnp.zeros((1, d), xf.dtype)], axis=0)
    xb = xpad[buf_tok].reshape(n_blk, MOE_BLOCK, d)

    def expert_block(args):
        xblk, e = args
        h = jax.nn.silu(xblk @ w1[e]) * (xblk @ w3[e])
        return h @ w2[e]

    yb = lax.map(expert_block, (xb, blk_exp)).reshape(n_pad, d)
    out = jnp.zeros((n_tok + 1, d), yb.dtype).at[buf_tok].add(yb * buf_w[:, None].astype(yb.dtype))
    return out[:n_tok].reshape(bsz, seq, d)


def setup_inputs(seed: int = 0) -> dict:
    key = jax.random.key(seed)
    keys = jax.random.split(key, 64)
    cnt = [0]

    def nrm(shape, scale):
        k = keys[cnt[0]]
        cnt[0] += 1
        return jax.random.normal(k, shape, jnp.float32) * scale

    NE, NO, D, H = N_EVEN, N_ODD, D_MODEL, B_HEADS
    f_bias = jnp.linspace(3.0, 6.0, H, dtype=jnp.float32)[None, :] + nrm((NE, H), 0.1)
    i_bias = nrm((NE, H), 0.1)
    return {
        'x': nrm((BATCH, SEQ, D), 1.0),
        'mem': nrm((BATCH, MEM_LEN, D), 1.0),
        'ln_g': 1.0 + nrm((DEPTH, 3, D), 0.02),
        'ln_b': nrm((DEPTH, 3, D), 0.02),
        'ev_w_in': nrm((NE, D, EVEN_IN), D ** -0.5),
        'ev_gm_ln_g': 1.0 + nrm((NE, A_WIDTH), 0.02),
        'ev_gm_ln_b': nrm((NE, A_WIDTH), 0.02),
        'ev_gm_ws': nrm((NE, A_GROUPS, A_CHUNK, A_CHUNK), A_CHUNK ** -0.5),
        'ev_gm_bs': 1.0 + nrm((NE, A_GROUPS, A_CHUNK), 0.1),
        'ev_conv_w': nrm((NE, B_CONV, B_WIDTH), B_CONV ** -0.5),
        'ev_conv_b': nrm((NE, B_WIDTH), 0.02),
        'ev_wq': nrm((NE, H, B_HEAD_DIM, B_HEAD_DIM), B_HEAD_DIM ** -0.5),
        'ev_wk': nrm((NE, H, B_HEAD_DIM, B_HEAD_DIM), B_HEAD_DIM ** -0.5),
        'ev_wv': nrm((NE, H, B_HEAD_DIM, B_HEAD_DIM), B_HEAD_DIM ** -0.5),
        'ev_w_if': nrm((NE, 3 * B_WIDTH, 2 * H), (3 * B_WIDTH) ** -0.5),
        'ev_b_if': jnp.concatenate([i_bias, f_bias], axis=-1),
        'ev_norm_w': 1.0 + nrm((NE, B_WIDTH), 0.02),
        'ev_skip': 1.0 + nrm((NE, B_WIDTH), 0.02),
        'ev_w_out': nrm((NE, EVEN_MIX, D), DN_BETA * EVEN_MIX ** -0.5),
        'od_w_qkv': nrm((NO, D, C_Q + 2 * C_KV), D ** -0.5),
        'od_b_qkv': nrm((NO, C_Q + 2 * C_KV), 0.02),
        'od_sinks': nrm((NO, C_HEADS), 0.5),
        'od_w_o': nrm((NO, C_Q, D), DN_BETA * C_Q ** -0.5),
        'xa_wq': nrm((DEPTH, D, D), D ** -0.5),
        'xa_wkv': nrm((DEPTH, D, 2 * D), D ** -0.5),
        'xa_wo': nrm((DEPTH, D, D), DN_BETA * D ** -0.5),
        'moe_w_rg': nrm((DEPTH, D, N_GROUPS), D ** -0.5),
        'moe_b_rg': nrm((DEPTH, N_GROUPS), 0.01),
        'moe_w_re': nrm((DEPTH, D, N_EXPERTS), D ** -0.5),
        'moe_b_re': nrm((DEPTH, N_EXPERTS), 0.01),
        'moe_w1': nrm((DEPTH, N_EXPERTS, D, D_EXPERT), D ** -0.5),
        'moe_w3': nrm((DEPTH, N_EXPERTS, D, D_EXPERT), D ** -0.5),
        'moe_w2': nrm((DEPTH, N_EXPERTS, D_EXPERT, D), DN_BETA * D_EXPERT ** -0.5),
    }


def reference(x, mem, ln_g, ln_b,
              ev_w_in, ev_gm_ln_g, ev_gm_ln_b, ev_gm_ws, ev_gm_bs,
              ev_conv_w, ev_conv_b, ev_wq, ev_wk, ev_wv, ev_w_if, ev_b_if,
              ev_norm_w, ev_skip, ev_w_out,
              od_w_qkv, od_b_qkv, od_sinks, od_w_o,
              xa_wq, xa_wkv, xa_wo,
              moe_w_rg, moe_b_rg, moe_w_re, moe_b_re, moe_w1, moe_w3, moe_w2):
    cos, sin = rotary_tables(x.shape[1])
    for l in range(DEPTH):
        if l % 2 == 0:
            e = l // 2
            y = even_mixer(x, ev_w_in[e], ev_gm_ln_g[e], ev_gm_ln_b[e], ev_gm_ws[e], ev_gm_bs[e],
                           ev_conv_w[e], ev_conv_b[e], ev_wq[e], ev_wk[e], ev_wv[e], ev_w_if[e],
                           ev_b_if[e], ev_norm_w[e], ev_skip[e], ev_w_out[e])
        else:
            o = l // 2
            y = swa_mixer(x, od_w_qkv[o], od_b_qkv[o], od_sinks[o], od_w_o[o], cos, sin)
        x = layer_norm(DN_ALPHA * x + y, ln_g[l, 0], ln_b[l, 0])
        y = memory_cross_attention(x, mem, xa_wq[l], xa_wkv[l], xa_wo[l])
        x = layer_norm(DN_ALPHA * x + y, ln_g[l, 1], ln_b[l, 1])
        y = hierarchical_moe(x, moe_w_rg[l], moe_b_rg[l], moe_w_re[l], moe_b_re[l],
                             moe_w1[l], moe_w3[l], moe_w2[l])
        x = layer_norm(DN_ALPHA * x + y, ln_g[l, 2], ln_b[l, 2])
    return x
```

```python
import functools
import math

import jax
import jax.numpy as jnp
from jax import lax
from jax.experimental import pallas as pl
from jax.experimental.pallas import tpu as pltpu

F32 = jnp.float32
BF16 = jnp.bfloat16

A_GROUPS = 4
CHUNK = 128
B_HEADS = 4
B_CONV = 4
C_HEAD_DIM = 64
C_KV_HEADS = 4
X_HEADS = 4
N_GROUPS = 4
EXPERTS_PER_GROUP = 8
N_EXPERTS = N_GROUPS * EXPERTS_PER_GROUP
ROPE_THETA = 10000.0
LN_EPS = 1e-5
DEPTH = 2
DN_ALPHA = (2 * DEPTH) ** 0.25

LANES = 128
VMEM_LIMIT = 48 * 1024 * 1024
NEG = -1e30


def _cparams(sem):
    return pltpu.CompilerParams(dimension_semantics=sem, vmem_limit_bytes=VMEM_LIMIT)


def _full(shape):
    nd = len(shape)
    return pl.BlockSpec(shape, lambda *_: (0,) * nd)


def _dot(a, b):
    return jnp.dot(a, b, preferred_element_type=F32)


def _dot_nt(a, b):
    return lax.dot_general(a, b, (((1,), (1,)), ((), ())), preferred_element_type=F32)


def _dot_tn(a, b):
    return lax.dot_general(a, b, (((0,), (0,)), ((), ())), preferred_element_type=F32)


def _split_dot(a, b_bf16):
    hi = a.astype(BF16)
    lo = (a - hi.astype(F32)).astype(BF16)
    return _dot(hi, b_bf16) + _dot(lo, b_bf16)


def _ln(x, g, b):
    mu = jnp.mean(x, axis=-1, keepdims=True)
    xc = x - mu
    var = jnp.mean(xc * xc, axis=-1, keepdims=True)
    return xc * lax.rsqrt(var + LN_EPS) * g + b


def _silu(x):
    return x * (1.0 / (1.0 + jnp.exp(-x)))


def _gelu(x):
    return 0.5 * x * (1.0 + jnp.tanh(math.sqrt(2.0 / math.pi) * (x + 0.044715 * (x * x * x))))


def _log_sigmoid(x):
    return jnp.minimum(x, 0.0) - jnp.log(1.0 + jnp.exp(-jnp.abs(x)))


def _even_kernel(x_ref, w_in_ref, gm_g_ref, gm_b_ref, gm_w_ref, gm_bias_ref,
                 conv_w_ref, conv_b_ref, wq_ref, wk_ref, wv_ref, wif_t_ref, wif_ref, bif_t_ref,
                 bif_ref, norm_w_ref, skip_ref, w_out_ref, ln_g_ref, ln_b_ref,
                 o_ref,
                 xm_buf, ct_ref, n_ref, m_ref, *, ts, aw, bw):
    dh = bw // B_HEADS
    agd = aw // A_GROUPS
    nck = ts // CHUNK
    pad = 8
    j = pl.program_id(1)

    @pl.when(j == 0)
    def _():
        xm_buf[0:pad, :] = jnp.zeros((pad, bw), F32)
        ct_ref[...] = jnp.zeros_like(ct_ref)
        n_ref[...] = jnp.zeros_like(n_ref)
        m_ref[...] = jnp.zeros_like(m_ref)

    x = x_ref[0]
    proj = _dot(x.astype(BF16), w_in_ref[...])
    a_u = _gelu(proj[:, :aw])
    a_v = _gelu(proj[:, aw:2 * aw])
    xm = proj[:, 2 * aw:2 * aw + bw]
    z = proj[:, 2 * aw + bw:]

    vn = _ln(a_v, gm_g_ref[...], gm_b_ref[...]).astype(BF16)
    ya_chunks = []
    for c in range(nck):
        cols = []
        for g in range(A_GROUPS):
            v_cg = vn[c * CHUNK:(c + 1) * CHUNK, g * agd:(g + 1) * agd]
            cols.append(_dot(gm_w_ref[g], v_cg))
        ya_chunks.append(jnp.concatenate(cols, axis=1) + gm_bias_ref[...])
    y_a = a_u * jnp.concatenate(ya_chunks, axis=0)

    xm_buf[pad:pad + ts, :] = xm
    conv = conv_b_ref[...] + conv_w_ref[B_CONV - 1:B_CONV, :] * xm
    for k in range(B_CONV - 1):
        sh = B_CONV - 1 - k
        conv = conv + conv_w_ref[k:k + 1, :] * xm_buf[pad - sh:pad - sh + ts, :]
    xm_buf[pad - (B_CONV - 1):pad, :] = xm_buf[pad + ts - (B_CONV - 1):pad + ts, :]
    xc = _silu(conv)
    xc_b = xc.astype(BF16)
    xm_b = xm.astype(BF16)
    q = _dot(xc_b, wq_ref[...])
    k_ = _dot(xc_b, wk_ref[...])
    v = _dot(xm_b, wv_ref[...])
    gate_in = jnp.concatenate([q, k_, v], axis=1).astype(BF16)
    gates = _dot(gate_in, wif_ref[...]) + bif_ref[...]
    gates_t = _dot_nt(wif_t_ref[...], gate_in) + bif_t_ref[...]
    ig_c = gates[:, :B_HEADS]
    lf_c = _log_sigmoid(gates[:, B_HEADS:])
    ig_r = gates_t[:B_HEADS, :]
    lf_r = _log_sigmoid(gates_t[B_HEADS:, :])

    row = lax.broadcasted_iota(jnp.int32, (CHUNK, CHUNK), 0)
    col = lax.broadcasted_iota(jnp.int32, (CHUNK, CHUNK), 1)
    causal = col <= row
    tril = jnp.where(causal, 1.0, 0.0).astype(BF16)
    triu = jnp.where(row <= col, 1.0, 0.0).astype(BF16)
    q_b = q.astype(BF16)
    k_b = (k_ * dh ** -0.5).astype(BF16)
    v_b = v.astype(BF16)

    h_chunks = []
    for c in range(nck):
        sl = slice(c * CHUNK, (c + 1) * CHUNK)
        b_c = _split_dot_left(tril, lf_c[sl, :])
        b_r = _split_dot(lf_r[:, sl], triu)
        heads = []
        for h in range(B_HEADS):
            hs = slice(h * dh, (h + 1) * dh)
            qh, kh, vh = q_b[sl, hs], k_b[sl, hs], v_b[sl, hs]
            bcol = b_c[:, h:h + 1]
            brow = b_r[h:h + 1, :]
            igrow = ig_r[h:h + 1, sl]
            igcol = ig_c[sl, h:h + 1]
            m_prev = m_ref[h:h + 1, 0:1]
            dmat = jnp.where(causal, bcol - brow + igrow, NEG)
            inter = bcol + m_prev
            mt = jnp.maximum(jnp.max(dmat, axis=1, keepdims=True), inter)
            w_intra = jnp.exp(dmat - mt)
            w_state = jnp.exp(inter - mt)
            s = _dot_nt(qh, kh) * w_intra
            ct = ct_ref[h]
            n_row = n_ref[h:h + 1, :]
            num = _dot(s.astype(BF16), vh) + w_state * _dot(qh, ct.astype(BF16))
            qf = qh.astype(F32)
            nq = jnp.sum(s, axis=1, keepdims=True) + w_state * jnp.sum(qf * n_row, axis=1, keepdims=True)
            hv = num / jnp.maximum(jnp.abs(nq), jnp.exp(-mt))
            bl = brow[:, CHUNK - 1:CHUNK]
            g_row = bl - brow + igrow
            g_col = bl - bcol + igcol
            m_new = jnp.maximum(bl + m_prev, jnp.max(g_row, axis=1, keepdims=True))
            decay = jnp.exp(bl + m_prev - m_new)
            wg_col = jnp.exp(g_col - m_new)
            kw = kh.astype(F32) * wg_col
            ct_ref[h] = decay * ct + _dot_tn(kw.astype(BF16), vh)
            n_ref[h:h + 1, :] = decay * n_row + jnp.sum(kw, axis=0, keepdims=True)
            m_ref[h:h + 1, :] = jnp.broadcast_to(m_new, (1, LANES))
            hc = hv - jnp.mean(hv, axis=1, keepdims=True)
            hn = hc * lax.rsqrt(jnp.mean(hc * hc, axis=1, keepdims=True) + LN_EPS)
            heads.append(hn)
        h_chunks.append(jnp.concatenate(heads, axis=1))
    hn_all = jnp.concatenate(h_chunks, axis=0) if nck > 1 else h_chunks[0]
    y_b = (hn_all * norm_w_ref[...] + skip_ref[...] * xc) * _silu(z)

    mix = jnp.concatenate([y_a, y_b], axis=1).astype(BF16)
    y = _dot(mix, w_out_ref[...])
    o_ref[0] = _ln(DN_ALPHA * x + y, ln_g_ref[...], ln_b_ref[...])


def _split_dot_left(a_bf16, b):
    hi = b.astype(BF16)
    lo = (b - hi.astype(F32)).astype(BF16)
    return _dot(a_bf16, hi) + _dot(a_bf16, lo)


def _block_diag(w):
    hh, d, _ = w.shape
    eye = jnp.eye(hh, dtype=w.dtype)
    return jnp.einsum('hde,hg->hdge', w, eye).reshape(hh * d, hh * d)


def even_mixer_layer(x, w_in, gm_ln_g, gm_ln_b, gm_ws, gm_bs, conv_w, conv_b, wq, wk, wv, w_if,
                     b_if, norm_w, skip, w_out, ln_g, ln_b, *, ts):
    bsz, seq, d = x.shape
    aw = gm_ln_g.shape[0]
    bw = conv_b.shape[0]
    agd = aw // A_GROUPS
    causal = jnp.tril(jnp.ones((CHUNK, CHUNK), dtype=bool))
    gm_w = jnp.where(causal[None], gm_ws, 0.0).astype(BF16)
    gm_bias = jnp.repeat(gm_bs.T, agd, axis=1)
    row = lambda a: a.reshape(1, -1)
    args = (x, w_in.astype(BF16), row(gm_ln_g), row(gm_ln_b), gm_w, gm_bias,
            conv_w, row(conv_b), _block_diag(wq).astype(BF16), _block_diag(wk).astype(BF16),
            _block_diag(wv).astype(BF16), w_if.T.astype(BF16), w_if.astype(BF16),
            b_if.reshape(-1, 1), row(b_if), row(norm_w), row(skip), w_out.astype(BF16),
            row(ln_g), row(ln_b))
    in_specs = [pl.BlockSpec((1, ts, d), lambda b, j: (b, j, 0))] + [_full(a.shape) for a in args[1:]]
    dh = bw // B_HEADS
    return pl.pallas_call(
        functools.partial(_even_kernel, ts=ts, aw=aw, bw=bw),
        grid=(bsz, seq // ts),
        in_specs=in_specs,
        out_specs=pl.BlockSpec((1, ts, d), lambda b, j: (b, j, 0)),
        out_shape=jax.ShapeDtypeStruct((bsz, seq, d), F32),
        scratch_shapes=[pltpu.VMEM((8 + ts, bw), F32),
                        pltpu.VMEM((B_HEADS, dh, dh), F32),
                        pltpu.VMEM((B_HEADS, dh), F32),
                        pltpu.VMEM((B_HEADS, LANES), F32)],
        compiler_params=_cparams(("arbitrary", "arbitrary")),
        name="even_mixer",
    )(*args)


def _matmul_kernel(a_ref, b_ref, o_ref):
    o_ref[...] = _dot(a_ref[...].astype(BF16), b_ref[...].astype(BF16)).astype(o_ref.dtype)


def matmul(a, b, *, tm, tn, out_dtype):
    m, k = a.shape
    n = b.shape[1]
    return pl.pallas_call(
        _matmul_kernel,
        grid=(n // tn, m // tm),
        in_specs=[pl.BlockSpec((tm, k), lambda j, i: (i, 0)),
                  pl.BlockSpec((k, tn), lambda j, i: (0, j))],
        out_specs=pl.BlockSpec((tm, tn), lambda j, i: (i, j)),
        out_shape=jax.ShapeDtypeStruct((m, n), out_dtype),
        compiler_params=_cparams(("arbitrary", "arbitrary")),
        name="matmul",
    )(a, b)


ROUTE_W = 128


def _xattn_kernel(x_ref, kv_ref, wq_ref, wo_ref, ln_g_ref, ln_b_ref, wr_ref, br_ref,
                  o_ref, route_ref, cnt_ref, *, d):
    dh = d // X_HEADS
    first = jnp.logical_and(pl.program_id(0) == 0, pl.program_id(1) == 0)

    @pl.when(first)
    def _():
        cnt_ref[...] = jnp.zeros_like(cnt_ref)

    x = x_ref[0]
    ts = x.shape[0]
    q = (_dot(x.astype(BF16), wq_ref[...]) * dh ** -0.5).astype(BF16)
    outs = []
    for h in range(X_HEADS):
        kh = kv_ref[0, :, h * dh:(h + 1) * dh]
        vh = kv_ref[0, :, d + h * dh:d + (h + 1) * dh]
        s = _dot_nt(q[:, h * dh:(h + 1) * dh], kh)
        s = s - jnp.max(s, axis=1, keepdims=True)
        p = jnp.exp(s)
        l = jnp.sum(p, axis=1, keepdims=True)
        outs.append(_dot(p.astype(BF16), vh) / l)
    o = jnp.concatenate(outs, axis=1).astype(BF16)
    y = _dot(o, wo_ref[...])
    x2 = _ln(DN_ALPHA * x + y, ln_g_ref[...], ln_b_ref[...])
    o_ref[0] = x2

    logits = jnp.dot(x2, wr_ref[...], preferred_element_type=F32,
                     precision=lax.Precision.HIGHEST) + br_ref[...]
    lane = lax.broadcasted_iota(jnp.int32, (ts, ROUTE_W), 1)
    is_g = lane < N_GROUPS
    lg = jnp.where(is_g, logits, NEG)
    mg = jnp.max(lg, axis=1, keepdims=True)
    gi = jnp.min(jnp.where(jnp.logical_and(is_g, lg == mg), lane, ROUTE_W), axis=1, keepdims=True)
    gate_g = 1.0 / jnp.sum(jnp.where(is_g, jnp.exp(lg - mg), 0.0), axis=1, keepdims=True)
    lo = N_GROUPS + gi * EXPERTS_PER_GROUP
    in_grp = jnp.logical_and(lane >= lo, lane < lo + EXPERTS_PER_GROUP)
    le = jnp.where(in_grp, logits, NEG)
    v1 = jnp.max(le, axis=1, keepdims=True)
    i1 = jnp.min(jnp.where(jnp.logical_and(in_grp, le == v1), lane, ROUTE_W), axis=1, keepdims=True)
    le2 = jnp.where(lane == i1, NEG, le)
    v2 = jnp.max(le2, axis=1, keepdims=True)
    i2 = jnp.min(jnp.where(jnp.logical_and(in_grp, le2 == v2), lane, ROUTE_W), axis=1, keepdims=True)
    e21 = jnp.exp(v2 - v1)
    p1 = 1.0 / (1.0 + e21)
    p2 = e21 * p1
    e1 = (i1 - N_GROUPS).astype(F32)
    e2 = (i2 - N_GROUPS).astype(F32)
    rec = jnp.where(lane == 0, e1, 0.0)
    rec = jnp.where(lane == 1, e2, rec)
    rec = jnp.where(lane == 2, gate_g * p1, rec)
    rec = jnp.where(lane == 3, gate_g * p2, rec)
    route_ref[0] = rec
    sel = jnp.logical_or(lane == i1 - N_GROUPS, lane == i2 - N_GROUPS)
    cnt_ref[...] += jnp.sum(jnp.where(sel, 1.0, 0.0), axis=0, keepdims=True)


def xattn_router_layer(x, kv, wq, wo, ln_g, ln_b, w_rg, b_rg, w_re, b_re, *, ts):
    bsz, seq, d = x.shape
    m_len = kv.shape[1]
    wr = jnp.zeros((d, ROUTE_W), F32).at[:, :N_GROUPS].set(w_rg).at[:, N_GROUPS:N_GROUPS + N_EXPERTS].set(w_re)
    br = jnp.zeros((1, ROUTE_W), F32).at[0, :N_GROUPS].set(b_rg).at[0, N_GROUPS:N_GROUPS + N_EXPERTS].set(b_re)
    args = (x, kv, wq.astype(BF16), wo.astype(BF16), ln_g.reshape(1, -1), ln_b.reshape(1, -1), wr, br)
    in_specs = [pl.BlockSpec((1, ts, d), lambda b, j: (b, j, 0)),
                pl.BlockSpec((1, m_len, 2 * d), lambda b, j: (b, 0, 0))] + [_full(a.shape) for a in args[2:]]
    return pl.pallas_call(
        functools.partial(_xattn_kernel, d=d),
        grid=(bsz, seq // ts),
        in_specs=in_specs,
        out_specs=[pl.BlockSpec((1, ts, d), lambda b, j: (b, j, 0)),
                   pl.BlockSpec((1, ts, ROUTE_W), lambda b, j: (b, j, 0)),
                   pl.BlockSpec((1, ROUTE_W), lambda b, j: (0, 0))],
        out_shape=[jax.ShapeDtypeStruct((bsz, seq, d), F32),
                   jax.ShapeDtypeStruct((bsz, seq, ROUTE_W), F32),
                   jax.ShapeDtypeStruct((1, ROUTE_W), F32)],
        compiler_params=_cparams(("arbitrary", "arbitrary")),
        name="xattn_router",
    )(*args)


def _swa_kernel(x_ref, wqkv_ref, bqkv_ref, cos_ref, sin_ref, sink_ref, wo_ref, ln_g_ref, ln_b_ref,
                o_ref, kprev, vprev, *, ts, cq, ckv):
    j = pl.program_id(1)
    nb = ts // CHUNK
    dh = C_HEAD_DIM
    grp = (cq // dh) // C_KV_HEADS

    @pl.when(j == 0)
    def _():
        kprev[...] = jnp.zeros_like(kprev)
        vprev[...] = jnp.zeros_like(vprev)

    x = x_ref[0]
    qkv = _dot(x.astype(BF16), wqkv_ref[...]) + bqkv_ref[...]
    cos = cos_ref[...]
    sin = sin_ref[...]
    lane = lax.broadcasted_iota(jnp.int32, (ts, LANES), 1)
    first_half = (lane % dh) < (dh // 2)

    def rope(t):
        outs = []
        for c in range(t.shape[1] // LANES):
            tc = t[:, c * LANES:(c + 1) * LANES]
            rot = jnp.where(first_half, pltpu.roll(tc, LANES - dh // 2, 1), pltpu.roll(tc, dh // 2, 1))
            outs.append(tc * cos + rot * sin)
        return jnp.concatenate(outs, axis=1)

    q = (rope(qkv[:, :cq]) * dh ** -0.5).astype(BF16)
    k = rope(qkv[:, cq:cq + ckv]).astype(BF16)
    v = qkv[:, cq + ckv:].astype(BF16)

    rows = grp * CHUNK
    r_i = lax.broadcasted_iota(jnp.int32, (rows, 2 * CHUNK), 0) % CHUNK
    c_i = lax.broadcasted_iota(jnp.int32, (rows, 2 * CHUNK), 1)
    band = jnp.logical_and(c_i > r_i, c_i <= r_i + CHUNK)
    r_g = lax.broadcasted_iota(jnp.int32, (rows, 1), 0) // CHUNK
    blocks = []
    for c in range(nb):
        sl = slice(c * CHUNK, (c + 1) * CHUNK)
        if c == 0:
            kb = jnp.concatenate([kprev[...].astype(BF16), k[sl]], axis=0)
            vb = jnp.concatenate([vprev[...].astype(BF16), v[sl]], axis=0)
            first_key = jnp.where(j > 0, 0, CHUNK)
            valid = jnp.logical_and(band, c_i >= first_key)
        else:
            kb = k[(c - 1) * CHUNK:(c + 1) * CHUNK]
            vb = v[(c - 1) * CHUNK:(c + 1) * CHUNK]
            valid = band
        heads = []
        for h in range(C_KV_HEADS):
            kh = kb[:, h * dh:(h + 1) * dh]
            vh = vb[:, h * dh:(h + 1) * dh]
            qs = jnp.concatenate([q[sl, (h * grp + g) * dh:(h * grp + g + 1) * dh] for g in range(grp)], axis=0)
            sink = jnp.zeros((rows, 1), F32)
            for g in range(grp):
                sink = jnp.where(r_g == g, sink_ref[h * grp + g], sink)
            s = jnp.where(valid, _dot_nt(qs, kh), NEG)
            m = jnp.maximum(jnp.max(s, axis=1, keepdims=True), sink)
            p = jnp.exp(s - m)
            den = jnp.sum(p, axis=1, keepdims=True) + jnp.exp(sink - m)
            o = _dot(p.astype(BF16), vh) / den
            heads.extend(o[g * CHUNK:(g + 1) * CHUNK] for g in range(grp))
        blocks.append(jnp.concatenate(heads, axis=1))
    kprev[...] = k[(nb - 1) * CHUNK:].astype(F32)
    vprev[...] = v[(nb - 1) * CHUNK:].astype(F32)
    att = (jnp.concatenate(blocks, axis=0) if nb > 1 else blocks[0]).astype(BF16)
    y = _dot(att, wo_ref[...])
    o_ref[0] = _ln(DN_ALPHA * x + y, ln_g_ref[...], ln_b_ref[...])


def swa_mixer_layer(x, w_qkv, b_qkv, sinks, w_o, ln_g, ln_b, *, ts):
    bsz, seq, d = x.shape
    cq = w_o.shape[0]
    ckv = (w_qkv.shape[1] - cq) // 2
    dh = C_HEAD_DIM
    inv = ROPE_THETA ** (-jnp.arange(0, dh, 2, dtype=F32) / dh)
    ang = jnp.arange(seq, dtype=F32)[:, None] * inv[None, :]
    reps = LANES // (dh // 2)
    sign = jnp.tile(jnp.concatenate([-jnp.ones((dh // 2,), F32), jnp.ones((dh // 2,), F32)]), LANES // dh)
    cos_t = jnp.tile(jnp.cos(ang), (1, reps))
    sin_t = jnp.tile(jnp.sin(ang), (1, reps)) * sign[None, :]
    args = (x, w_qkv.astype(BF16), b_qkv.reshape(1, -1), cos_t, sin_t, sinks.astype(F32),
            w_o.astype(BF16), ln_g.reshape(1, -1), ln_b.reshape(1, -1))
    in_specs = [pl.BlockSpec((1, ts, d), lambda b, j: (b, j, 0)),
                _full(args[1].shape), _full(args[2].shape),
                pl.BlockSpec((ts, LANES), lambda b, j: (j, 0)),
                pl.BlockSpec((ts, LANES), lambda b, j: (j, 0)),
                pl.BlockSpec(memory_space=pltpu.SMEM),
                _full(args[6].shape), _full(args[7].shape), _full(args[8].shape)]
    return pl.pallas_call(
        functools.partial(_swa_kernel, ts=ts, cq=cq, ckv=ckv),
        grid=(bsz, seq // ts),
        in_specs=in_specs,
        out_specs=pl.BlockSpec((1, ts, d), lambda b, j: (b, j, 0)),
        out_shape=jax.ShapeDtypeStruct((bsz, seq, d), F32),
        scratch_shapes=[pltpu.VMEM((CHUNK, ckv), F32), pltpu.VMEM((CHUNK, ckv), F32)],
        compiler_params=_cparams(("arbitrary", "arbitrary")),
        name="swa_mixer",
    )(*args)


def _slot_kernel(route_ref, pstart_ref, dest_ref, carry_ref, *, tb):
    @pl.when(pl.program_id(0) == 0)
    def _():
        carry_ref[...] = pstart_ref[...]

    rec = route_ref[...]
    lane = lax.broadcasted_iota(jnp.int32, (tb, ROUTE_W), 1)
    e0 = rec[:, 0:1].astype(jnp.int32)
    e1 = rec[:, 1:2].astype(jnp.int32)
    oh0 = lane == e0
    oh1 = lane == e1
    ohs = jnp.where(jnp.logical_or(oh0, oh1), 1.0, 0.0)
    r = lax.broadcasted_iota(jnp.int32, (tb, tb), 0)
    c = lax.broadcasted_iota(jnp.int32, (tb, tb), 1)
    before = jnp.where(c < r, 1.0, 0.0).astype(BF16)
    prefix = _dot(before, ohs.astype(BF16)) + carry_ref[...]
    d0 = jnp.sum(jnp.where(oh0, prefix, 0.0), axis=1, keepdims=True)
    d1 = jnp.sum(jnp.where(oh1, prefix, 0.0), axis=1, keepdims=True)
    dest_ref[...] = jnp.where(lane == 0, d0, jnp.where(lane == 1, d1, 0.0)).astype(jnp.int32)
    carry_ref[...] += jnp.sum(ohs, axis=0, keepdims=True)


def moe_slots(route, pstart, *, tb):
    n = route.shape[0]
    return pl.pallas_call(
        functools.partial(_slot_kernel, tb=tb),
        grid=(n // tb,),
        in_specs=[pl.BlockSpec((tb, ROUTE_W), lambda i: (i, 0)), _full((1, ROUTE_W))],
        out_specs=pl.BlockSpec((tb, ROUTE_W), lambda i: (i, 0)),
        out_shape=jax.ShapeDtypeStruct((n, ROUTE_W), jnp.int32),
        scratch_shapes=[pltpu.VMEM((1, ROUTE_W), F32)],
        compiler_params=_cparams(("arbitrary",)),
        name="moe_slots",
    )(route, pstart)


def _ffn_kernel(blk_exp_ref, new_exp_ref, nblk_ref, xs_ref, w1_ref, w3_ref, w2_ref, ys_ref,
                w1_b, w3_b, w2_b):
    i = pl.program_id(0)
    used = i < nblk_ref[0]

    @pl.when(jnp.logical_and(used, new_exp_ref[i] == 1))
    def _():
        w1_b[...] = w1_ref[0].astype(BF16)
        w3_b[...] = w3_ref[0].astype(BF16)
        w2_b[...] = w2_ref[0].astype(BF16)

    @pl.when(used)
    def _():
        x = xs_ref[...].astype(BF16)
        h1 = _dot(x, w1_b[...])
        h3 = _dot(x, w3_b[...])
        h = (_silu(h1) * h3).astype(BF16)
        ys_ref[...] = _dot(h, w2_b[...])

    @pl.when(jnp.logical_not(used))
    def _():
        ys_ref[...] = jnp.zeros_like(ys_ref)


def moe_ffn(xs, blk_exp, new_exp, nblk, w1, w3, w2, *, bm):
    n_pad, d = xs.shape
    de = w1.shape[2]
    n_blk = n_pad // bm

    def x_map(i, be, ne, nb):
        return (jnp.minimum(i, nb[0] - 1), 0)

    def w_map(i, be, ne, nb):
        return (be[i], 0, 0)

    return pl.pallas_call(
        _ffn_kernel,
        grid_spec=pltpu.PrefetchScalarGridSpec(
            num_scalar_prefetch=3,
            grid=(n_blk,),
            in_specs=[pl.BlockSpec((bm, d), x_map),
                      pl.BlockSpec((1, d, de), w_map),
                      pl.BlockSpec((1, d, de), w_map),
                      pl.BlockSpec((1, de, d), w_map)],
            out_specs=pl.BlockSpec((bm, d), lambda i, be, ne, nb: (i, 0)),
            scratch_shapes=[pltpu.VMEM((d, de), BF16), pltpu.VMEM((d, de), BF16),
                            pltpu.VMEM((de, d), BF16)]),
        out_shape=jax.ShapeDtypeStruct((n_pad, d), F32),
        compiler_params=_cparams(("arbitrary",)),
        name="moe_ffn",
    )(blk_exp, new_exp, nblk, xs, w1, w3, w2)


def _combine_kernel(x_ref, g0_ref, g1_ref, route_ref, ln_g_ref, ln_b_ref, o_ref):
    rec = route_ref[...]
    y = rec[:, 2:3] * g0_ref[...] + rec[:, 3:4] * g1_ref[...]
    o_ref[...] = _ln(DN_ALPHA * x_ref[...] + y, ln_g_ref[...], ln_b_ref[...])


def moe_combine(x, g0, g1, route, ln_g, ln_b, *, tb):
    n, d = x.shape
    row_spec = pl.BlockSpec((tb, d), lambda i: (i, 0))
    return pl.pallas_call(
        _combine_kernel,
        grid=(n // tb,),
        in_specs=[row_spec, row_spec, row_spec, pl.BlockSpec((tb, ROUTE_W), lambda i: (i, 0)),
                  _full((1, d)), _full((1, d))],
        out_specs=row_spec,
        out_shape=jax.ShapeDtypeStruct((n, d), F32),
        compiler_params=_cparams(("arbitrary",)),
        name="moe_combine",
    )(x, g0, g1, route, ln_g.reshape(1, -1), ln_b.reshape(1, -1))


MOE_BM = 256


def hierarchical_moe_layer(x2, route, counts, w1, w3, w2, ln_g, ln_b):
    bsz, seq, d = x2.shape
    n = bsz * seq
    bm = MOE_BM
    xf = x2.reshape(n, d)
    rt = route.reshape(n, ROUTE_W)
    n_blk = (2 * n) // bm + N_EXPERTS
    n_pad = n_blk * bm
    cnt = counts[0, :N_EXPERTS].astype(jnp.int32)
    pcnt = (cnt + bm - 1) // bm * bm
    pends = jnp.cumsum(pcnt)
    pstart = pends - pcnt
    nblk = (pends[-1] // bm).astype(jnp.int32).reshape(1)
    blk_exp = jnp.minimum(jnp.searchsorted(pends, jnp.arange(n_blk, dtype=jnp.int32) * bm, side='right'),
                          N_EXPERTS - 1).astype(jnp.int32)
    last_exp = blk_exp[jnp.maximum(nblk[0] - 1, 0)]
    blk_exp = jnp.where(jnp.arange(n_blk) < nblk[0], blk_exp, last_exp)
    new_exp = jnp.concatenate([jnp.ones((1,), jnp.int32),
                               (blk_exp[1:] != blk_exp[:-1]).astype(jnp.int32)])
    pstart_rec = jnp.zeros((1, ROUTE_W), F32).at[0, :N_EXPERTS].set(pstart.astype(F32))
    dest = moe_slots(rt, pstart_rec, tb=min(512, n))[:, :2]
    xs = moe_dispatch(xf, dest, n_pad)
    ys = moe_ffn(xs, blk_exp, new_exp, nblk, w1, w3, w2, bm=bm)
    g0, g1 = moe_gather(ys, dest)
    out = moe_combine(xf, g0, g1, rt, ln_g, ln_b, tb=min(512, n))
    return out.reshape(bsz, seq, d)


def moe_dispatch(xf, dest, n_pad):
    n, d = xf.shape
    tok = jnp.repeat(jnp.arange(n, dtype=jnp.int32), 2)
    buf_tok = jnp.zeros((n_pad,), jnp.int32).at[dest.reshape(-1)].set(tok)
    return xf[buf_tok]


def moe_gather(ys, dest):
    return ys[dest[:, 0]], ys[dest[:, 1]]


def kernel(x, mem, ln_g, ln_b, ev_w_in, ev_gm_ln_g, ev_gm_ln_b, ev_gm_ws, ev_gm_bs, ev_conv_w, ev_conv_b, ev_wq, ev_wk, ev_wv, ev_w_if, ev_b_if, ev_norm_w, ev_skip, ev_w_out, od_w_qkv, od_b_qkv, od_sinks, od_w_o, xa_wq, xa_wkv, xa_wo, moe_w_rg, moe_b_rg, moe_w_re, moe_b_re, moe_w1, moe_w3, moe_w2):
    bsz, seq, d = x.shape
    m_len = mem.shape[1]
    depth = ln_g.shape[0]
    ts = min(256, seq)
    memf = mem.reshape(bsz * m_len, d)
    for l in range(depth):
        if l % 2 == 0:
            e = l // 2
            x = even_mixer_layer(x, ev_w_in[e], ev_gm_ln_g[e], ev_gm_ln_b[e], ev_gm_ws[e], ev_gm_bs[e],
                                 ev_conv_w[e], ev_conv_b[e], ev_wq[e], ev_wk[e], ev_wv[e], ev_w_if[e],
                                 ev_b_if[e], ev_norm_w[e], ev_skip[e], ev_w_out[e],
                                 ln_g[l, 0], ln_b[l, 0], ts=ts)
        else:
            o = l // 2
            x = swa_mixer_layer(x, od_w_qkv[o], od_b_qkv[o], od_sinks[o], od_w_o[o],
                                ln_g[l, 0], ln_b[l, 0], ts=ts)
        kv = matmul(memf, xa_wkv[l], tm=min(512, bsz * m_len), tn=512, out_dtype=BF16)
        kv = kv.reshape(bsz, m_len, 2 * d)
        x, route, counts = xattn_router_layer(x, kv, xa_wq[l], xa_wo[l], ln_g[l, 1], ln_b[l, 1],
                                              moe_w_rg[l], moe_b_rg[l], moe_w_re[l], moe_b_re[l], ts=ts)
        x = hierarchical_moe_layer(x, route, counts, moe_w1[l], moe_w3[l], moe_w2[l],
                                   ln_g[l, 2], ln_b[l, 2])
    return x
```

```python
import functools
import math

import jax
import jax.numpy as jnp
from jax import lax
from jax.experimental import pallas as pl
from jax.experimental.pallas import tpu as pltpu
from jax.experimental.pallas import tpu_sc as plsc

F32 = jnp.float32
BF16 = jnp.bfloat16

A_GROUPS = 4
CHUNK = 128
B_HEADS = 4
B_CONV = 4
C_HEAD_DIM = 64
C_KV_HEADS = 4
X_HEADS = 4
N_GROUPS = 4
EXPERTS_PER_GROUP = 8
N_EXPERTS = N_GROUPS * EXPERTS_PER_GROUP
ROPE_THETA = 10000.0
LN_EPS = 1e-5
DEPTH = 2
DN_ALPHA = (2 * DEPTH) ** 0.25

LANES = 128
VMEM_LIMIT = 48 * 1024 * 1024
NEG = -1e30


def _cparams(sem):
    return pltpu.CompilerParams(dimension_semantics=sem, vmem_limit_bytes=VMEM_LIMIT)


def _full(shape):
    nd = len(shape)
    return pl.BlockSpec(shape, lambda *_: (0,) * nd)


def _dot(a, b):
    return jnp.dot(a, b, preferred_element_type=F32)


def _dot_nt(a, b):
    return lax.dot_general(a, b, (((1,), (1,)), ((), ())), preferred_element_type=F32)


def _dot_tn(a, b):
    return lax.dot_general(a, b, (((0,), (0,)), ((), ())), preferred_element_type=F32)


def _split_dot(a, b_bf16):
    hi = a.astype(BF16)
    lo = (a - hi.astype(F32)).astype(BF16)
    return _dot(hi, b_bf16) + _dot(lo, b_bf16)


def _ln(x, g, b):
    mu = jnp.mean(x, axis=-1, keepdims=True)
    xc = x - mu
    var = jnp.mean(xc * xc, axis=-1, keepdims=True)
    return xc * lax.rsqrt(var + LN_EPS) * g + b


def _silu(x):
    return x * (1.0 / (1.0 + jnp.exp(-x)))


def _gelu(x):
    return 0.5 * x * (1.0 + jnp.tanh(math.sqrt(2.0 / math.pi) * (x + 0.044715 * (x * x * x))))


def _log_sigmoid(x):
    return jnp.minimum(x, 0.0) - jnp.log(1.0 + jnp.exp(-jnp.abs(x)))


def _even_kernel(x_ref, w_in_ref, gm_g_ref, gm_b_ref, gm_w_ref, gm_bias_ref,
                 conv_w_ref, conv_b_ref, wq_ref, wk_ref, wv_ref, wif_t_ref, wif_ref, bif_t_ref,
                 bif_ref, norm_w_ref, skip_ref, w_out_ref, ln_g_ref, ln_b_ref,
                 o_ref,
                 xm_buf, ct_ref, n_ref, m_ref, *, ts, aw, bw):
    dh = bw // B_HEADS
    agd = aw // A_GROUPS
    nck = ts // CHUNK
    pad = 8
    j = pl.program_id(1)

    @pl.when(j == 0)
    def _():
        xm_buf[0:pad, :] = jnp.zeros((pad, bw), F32)
        ct_ref[...] = jnp.zeros_like(ct_ref)
        n_ref[...] = jnp.zeros_like(n_ref)
        m_ref[...] = jnp.zeros_like(m_ref)

    x = x_ref[0]
    proj = _dot(x.astype(BF16), w_in_ref[...])
    a_u = _gelu(proj[:, :aw])
    a_v = _gelu(proj[:, aw:2 * aw])
    xm = proj[:, 2 * aw:2 * aw + bw]
    z = proj[:, 2 * aw + bw:]

    vn = _ln(a_v, gm_g_ref[...], gm_b_ref[...]).astype(BF16)
    ya_chunks = []
    for c in range(nck):
        cols = []
        for g in range(A_GROUPS):
            v_cg = vn[c * CHUNK:(c + 1) * CHUNK, g * agd:(g + 1) * agd]
            cols.append(_dot(gm_w_ref[g], v_cg))
        ya_chunks.append(jnp.concatenate(cols, axis=1) + gm_bias_ref[...])
    y_a = a_u * jnp.concatenate(ya_chunks, axis=0)

    xm_buf[pad:pad + ts, :] = xm
    conv = conv_b_ref[...] + conv_w_ref[B_CONV - 1:B_CONV, :] * xm
    for k in range(B_CONV - 1):
        sh = B_CONV - 1 - k
        conv = conv + conv_w_ref[k:k + 1, :] * xm_buf[pad - sh:pad - sh + ts, :]
    xm_buf[pad - (B_CONV - 1):pad, :] = xm_buf[pad + ts - (B_CONV - 1):pad + ts, :]
    xc = _silu(conv)
    xc_b = xc.astype(BF16)
    xm_b = xm.astype(BF16)
    q = _dot(xc_b, wq_ref[...])
    k_ = _dot(xc_b, wk_ref[...])
    v = _dot(xm_b, wv_ref[...])
    gate_in = jnp.concatenate([q, k_, v], axis=1).astype(BF16)
    gates = _dot(gate_in, wif_ref[...]) + bif_ref[...]
    gates_t = _dot_nt(wif_t_ref[...], gate_in) + bif_t_ref[...]
    ig_c = gates[:, :B_HEADS]
    lf_c = _log_sigmoid(gates[:, B_HEADS:])
    ig_r = gates_t[:B_HEADS, :]
    lf_r = _log_sigmoid(gates_t[B_HEADS:, :])

    row = lax.broadcasted_iota(jnp.int32, (CHUNK, CHUNK), 0)
    col = lax.broadcasted_iota(jnp.int32, (CHUNK, CHUNK), 1)
    causal = col <= row
    tril = jnp.where(causal, 1.0, 0.0).astype(BF16)
    triu = jnp.where(row <= col, 1.0, 0.0).astype(BF16)
    q_b = q.astype(BF16)
    k_b = (k_ * dh ** -0.5).astype(BF16)
    v_b = v.astype(BF16)

    h_chunks = []
    for c in range(nck):
        sl = slice(c * CHUNK, (c + 1) * CHUNK)
        b_c = _split_dot_left(tril, lf_c[sl, :])
        b_r = _split_dot(lf_r[:, sl], triu)
        heads = []
        for h in range(B_HEADS):
            hs = slice(h * dh, (h + 1) * dh)
            qh, kh, vh = q_b[sl, hs], k_b[sl, hs], v_b[sl, hs]
            bcol = b_c[:, h:h + 1]
            brow = b_r[h:h + 1, :]
            igrow = ig_r[h:h + 1, sl]
            igcol = ig_c[sl, h:h + 1]
            m_prev = m_ref[h:h + 1, 0:1]
            dmat = jnp.where(causal, bcol - brow + igrow, NEG)
            inter = bcol + m_prev
            mt = jnp.maximum(jnp.max(dmat, axis=1, keepdims=True), inter)
            w_intra = jnp.exp(dmat - mt)
            w_state = jnp.exp(inter - mt)
            s = _dot_nt(qh, kh) * w_intra
            ct = ct_ref[h]
            n_row = n_ref[h:h + 1, :]
            num = _dot(s.astype(BF16), vh) + w_state * _dot(qh, ct.astype(BF16))
            qf = qh.astype(F32)
            nq = jnp.sum(s, axis=1, keepdims=True) + w_state * jnp.sum(qf * n_row, axis=1, keepdims=True)
            hv = num / jnp.maximum(jnp.abs(nq), jnp.exp(-mt))
            bl = brow[:, CHUNK - 1:CHUNK]
            g_row = bl - brow + igrow
            g_col = bl - bcol + igcol
            m_new = jnp.maximum(bl + m_prev, jnp.max(g_row, axis=1, keepdims=True))
            decay = jnp.exp(bl + m_prev - m_new)
            wg_col = jnp.exp(g_col - m_new)
            kw = kh.astype(F32) * wg_col
            ct_ref[h] = decay * ct + _dot_tn(kw.astype(BF16), vh)
            n_ref[h:h + 1, :] = decay * n_row + jnp.sum(kw, axis=0, keepdims=True)
            m_ref[h:h + 1, :] = jnp.broadcast_to(m_new, (1, LANES))
            hc = hv - jnp.mean(hv, axis=1, keepdims=True)
            hn = hc * lax.rsqrt(jnp.mean(hc * hc, axis=1, keepdims=True) + LN_EPS)
            heads.append(hn)
        h_chunks.append(jnp.concatenate(heads, axis=1))
    hn_all = jnp.concatenate(h_chunks, axis=0) if nck > 1 else h_chunks[0]
    y_b = (hn_all * norm_w_ref[...] + skip_ref[...] * xc) * _silu(z)

    mix = jnp.concatenate([y_a, y_b], axis=1).astype(BF16)
    y = _dot(mix, w_out_ref[...])
    o_ref[0] = _ln(DN_ALPHA * x + y, ln_g_ref[...], ln_b_ref[...])


def _split_dot_left(a_bf16, b):
    hi = b.astype(BF16)
    lo = (b - hi.astype(F32)).astype(BF16)
    return _dot(a_bf16, hi) + _dot(a_bf16, lo)


def _block_diag(w):
    hh, d, _ = w.shape
    eye = jnp.eye(hh, dtype=w.dtype)
    return jnp.einsum('hde,hg->hdge', w, eye).reshape(hh * d, hh * d)


def even_mixer_layer(x, w_in, gm_ln_g, gm_ln_b, gm_ws, gm_bs, conv_w, conv_b, wq, wk, wv, w_if,
                     b_if, norm_w, skip, w_out, ln_g, ln_b, *, ts):
    bsz, seq, d = x.shape
    aw = gm_ln_g.shape[0]
    bw = conv_b.shape[0]
    agd = aw // A_GROUPS
    causal = jnp.tril(jnp.ones((CHUNK, CHUNK), dtype=bool))
    gm_w = jnp.where(causal[None], gm_ws, 0.0).astype(BF16)
    gm_bias = jnp.repeat(gm_bs.T, agd, axis=1)
    row = lambda a: a.reshape(1, -1)
    args = (x, w_in.astype(BF16), row(gm_ln_g), row(gm_ln_b), gm_w, gm_bias,
            conv_w, row(conv_b), _block_diag(wq).astype(BF16), _block_diag(wk).astype(BF16),
            _block_diag(wv).astype(BF16), w_if.T.astype(BF16), w_if.astype(BF16),
            b_if.reshape(-1, 1), row(b_if), row(norm_w), row(skip), w_out.astype(BF16),
            row(ln_g), row(ln_b))
    in_specs = [pl.BlockSpec((1, ts, d), lambda b, j: (b, j, 0))] + [_full(a.shape) for a in args[1:]]
    dh = bw // B_HEADS
    return pl.pallas_call(
        functools.partial(_even_kernel, ts=ts, aw=aw, bw=bw),
        grid=(bsz, seq // ts),
        in_specs=in_specs,
        out_specs=pl.BlockSpec((1, ts, d), lambda b, j: (b, j, 0)),
        out_shape=jax.ShapeDtypeStruct((bsz, seq, d), F32),
        scratch_shapes=[pltpu.VMEM((8 + ts, bw), F32),
                        pltpu.VMEM((B_HEADS, dh, dh), F32),
                        pltpu.VMEM((B_HEADS, dh), F32),
                        pltpu.VMEM((B_HEADS, LANES), F32)],
        compiler_params=_cparams(("arbitrary", "arbitrary")),
        name="even_mixer",
    )(*args)


def _matmul_kernel(a_ref, b_ref, o_ref):
    o_ref[...] = _dot(a_ref[...].astype(BF16), b_ref[...].astype(BF16)).astype(o_ref.dtype)


def matmul(a, b, *, tm, tn, out_dtype):
    m, k = a.shape
    n = b.shape[1]
    return pl.pallas_call(
        _matmul_kernel,
        grid=(n // tn, m // tm),
        in_specs=[pl.BlockSpec((tm, k), lambda j, i: (i, 0)),
                  pl.BlockSpec((k, tn), lambda j, i: (0, j))],
        out_specs=pl.BlockSpec((tm, tn), lambda j, i: (i, j)),
        out_shape=jax.ShapeDtypeStruct((m, n), out_dtype),
        compiler_params=_cparams(("arbitrary", "arbitrary")),
        name="matmul",
    )(a, b)


ROUTE_W = 128
DEST_ROWS = 8


def _xattn_kernel(x_ref, kv_ref, wq_ref, wo_ref, ln_g_ref, ln_b_ref, wr_ref, br_ref,
                  o_ref, route_ref, cnt_ref, *, d):
    dh = d // X_HEADS
    first = jnp.logical_and(pl.program_id(0) == 0, pl.program_id(1) == 0)

    @pl.when(first)
    def _():
        cnt_ref[...] = jnp.zeros_like(cnt_ref)

    x = x_ref[0]
    ts = x.shape[0]
    q = (_dot(x.astype(BF16), wq_ref[...]) * dh ** -0.5).astype(BF16)
    outs = []
    for h in range(X_HEADS):
        kh = kv_ref[0, :, h * dh:(h + 1) * dh]
        vh = kv_ref[0, :, d + h * dh:d + (h + 1) * dh]
        s = _dot_nt(q[:, h * dh:(h + 1) * dh], kh)
        s = s - jnp.max(s, axis=1, keepdims=True)
        p = jnp.exp(s)
        l = jnp.sum(p, axis=1, keepdims=True)
        outs.append(_dot(p.astype(BF16), vh) / l)
    o = jnp.concatenate(outs, axis=1).astype(BF16)
    y = _dot(o, wo_ref[...])
    x2 = _ln(DN_ALPHA * x + y, ln_g_ref[...], ln_b_ref[...])
    o_ref[0] = x2

    logits = jnp.dot(x2, wr_ref[...], preferred_element_type=F32,
                     precision=lax.Precision.HIGHEST) + br_ref[...]
    lane = lax.broadcasted_iota(jnp.int32, (ts, ROUTE_W), 1)
    is_g = lane < N_GROUPS
    lg = jnp.where(is_g, logits, NEG)
    mg = jnp.max(lg, axis=1, keepdims=True)
    gi = jnp.min(jnp.where(jnp.logical_and(is_g, lg == mg), lane, ROUTE_W), axis=1, keepdims=True)
    gate_g = 1.0 / jnp.sum(jnp.where(is_g, jnp.exp(lg - mg), 0.0), axis=1, keepdims=True)
    lo = N_GROUPS + gi * EXPERTS_PER_GROUP
    in_grp = jnp.logical_and(lane >= lo, lane < lo + EXPERTS_PER_GROUP)
    le = jnp.where(in_grp, logits, NEG)
    v1 = jnp.max(le, axis=1, keepdims=True)
    i1 = jnp.min(jnp.where(jnp.logical_and(in_grp, le == v1), lane, ROUTE_W), axis=1, keepdims=True)
    le2 = jnp.where(lane == i1, NEG, le)
    v2 = jnp.max(le2, axis=1, keepdims=True)
    i2 = jnp.min(jnp.where(jnp.logical_and(in_grp, le2 == v2), lane, ROUTE_W), axis=1, keepdims=True)
    e21 = jnp.exp(v2 - v1)
    p1 = 1.0 / (1.0 + e21)
    p2 = e21 * p1
    e1 = (i1 - N_GROUPS).astype(F32)
    e2 = (i2 - N_GROUPS).astype(F32)
    rec = jnp.where(lane == 0, e1, 0.0)
    rec = jnp.where(lane == 1, e2, rec)
    rec = jnp.where(lane == 2, gate_g * p1, rec)
    rec = jnp.where(lane == 3, gate_g * p2, rec)
    route_ref[0] = rec
    sel = jnp.logical_or(lane == i1 - N_GROUPS, lane == i2 - N_GROUPS)
    cnt_ref[...] += jnp.sum(jnp.where(sel, 1.0, 0.0), axis=0, keepdims=True)


def xattn_router_layer(x, kv, wq, wo, ln_g, ln_b, w_rg, b_rg, w_re, b_re, *, ts):
    bsz, seq, d = x.shape
    m_len = kv.shape[1]
    wr = jnp.zeros((d, ROUTE_W), F32).at[:, :N_GROUPS].set(w_rg).at[:, N_GROUPS:N_GROUPS + N_EXPERTS].set(w_re)
    br = jnp.zeros((1, ROUTE_W), F32).at[0, :N_GROUPS].set(b_rg).at[0, N_GROUPS:N_GROUPS + N_EXPERTS].set(b_re)
    args = (x, kv, wq.astype(BF16), wo.astype(BF16), ln_g.reshape(1, -1), ln_b.reshape(1, -1), wr, br)
    in_specs = [pl.BlockSpec((1, ts, d), lambda b, j: (b, j, 0)),
                pl.BlockSpec((1, m_len, 2 * d), lambda b, j: (b, 0, 0))] + [_full(a.shape) for a in args[2:]]
    return pl.pallas_call(
        functools.partial(_xattn_kernel, d=d),
        grid=(bsz, seq // ts),
        in_specs=in_specs,
        out_specs=[pl.BlockSpec((1, ts, d), lambda b, j: (b, j, 0)),
                   pl.BlockSpec((1, ts, ROUTE_W), lambda b, j: (b, j, 0)),
                   pl.BlockSpec((1, ROUTE_W), lambda b, j: (0, 0))],
        out_shape=[jax.ShapeDtypeStruct((bsz, seq, d), F32),
                   jax.ShapeDtypeStruct((bsz, seq, ROUTE_W), F32),
                   jax.ShapeDtypeStruct((1, ROUTE_W), F32)],
        compiler_params=_cparams(("arbitrary", "arbitrary")),
        name="xattn_router",
    )(*args)


def _swa_kernel(x_ref, wqkv_ref, bqkv_ref, cos_ref, sin_ref, sink_ref, wo_ref, ln_g_ref, ln_b_ref,
                o_ref, kprev, vprev, *, ts, cq, ckv):
    j = pl.program_id(1)
    nb = ts // CHUNK
    dh = C_HEAD_DIM
    grp = (cq // dh) // C_KV_HEADS

    @pl.when(j == 0)
    def _():
        kprev[...] = jnp.zeros_like(kprev)
        vprev[...] = jnp.zeros_like(vprev)

    x = x_ref[0]
    qkv = _dot(x.astype(BF16), wqkv_ref[...]) + bqkv_ref[...]
    cos = cos_ref[...]
    sin = sin_ref[...]
    lane = lax.broadcasted_iota(jnp.int32, (ts, LANES), 1)
    first_half = (lane % dh) < (dh // 2)

    def rope(t):
        outs = []
        for c in range(t.shape[1] // LANES):
            tc = t[:, c * LANES:(c + 1) * LANES]
            rot = jnp.where(first_half, pltpu.roll(tc, LANES - dh // 2, 1), pltpu.roll(tc, dh // 2, 1))
            outs.append(tc * cos + rot * sin)
        return jnp.concatenate(outs, axis=1)

    q = (rope(qkv[:, :cq]) * dh ** -0.5).astype(BF16)
    k = rope(qkv[:, cq:cq + ckv]).astype(BF16)
    v = qkv[:, cq + ckv:].astype(BF16)

    rows = grp * CHUNK
    r_i = lax.broadcasted_iota(jnp.int32, (rows, 2 * CHUNK), 0) % CHUNK
    c_i = lax.broadcasted_iota(jnp.int32, (rows, 2 * CHUNK), 1)
    band = jnp.logical_and(c_i > r_i, c_i <= r_i + CHUNK)
    r_g = lax.broadcasted_iota(jnp.int32, (rows, 1), 0) // CHUNK
    blocks = []
    for c in range(nb):
        sl = slice(c * CHUNK, (c + 1) * CHUNK)
        if c == 0:
            kb = jnp.concatenate([kprev[...].astype(BF16), k[sl]], axis=0)
            vb = jnp.concatenate([vprev[...].astype(BF16), v[sl]], axis=0)
            first_key = jnp.where(j > 0, 0, CHUNK)
            valid = jnp.logical_and(band, c_i >= first_key)
        else:
            kb = k[(c - 1) * CHUNK:(c + 1) * CHUNK]
            vb = v[(c - 1) * CHUNK:(c + 1) * CHUNK]
            valid = band
        heads = []
        for h in range(C_KV_HEADS):
            kh = kb[:, h * dh:(h + 1) * dh]
            vh = vb[:, h * dh:(h + 1) * dh]
            qs = jnp.concatenate([q[sl, (h * grp + g) * dh:(h * grp + g + 1) * dh] for g in range(grp)], axis=0)
            sink = jnp.zeros((rows, 1), F32)
            for g in range(grp):
                sink = jnp.where(r_g == g, sink_ref[h * grp + g], sink)
            s = jnp.where(valid, _dot_nt(qs, kh), NEG)
            m = jnp.maximum(jnp.max(s, axis=1, keepdims=True), sink)
            p = jnp.exp(s - m)
            den = jnp.sum(p, axis=1, keepdims=True) + jnp.exp(sink - m)
            o = _dot(p.astype(BF16), vh) / den
            heads.extend(o[g * CHUNK:(g + 1) * CHUNK] for g in range(grp))
        blocks.append(jnp.concatenate(heads, axis=1))
    kprev[...] = k[(nb - 1) * CHUNK:].astype(F32)
    vprev[...] = v[(nb - 1) * CHUNK:].astype(F32)
    att = (jnp.concatenate(blocks, axis=0) if nb > 1 else blocks[0]).astype(BF16)
    y = _dot(att, wo_ref[...])
    o_ref[0] = _ln(DN_ALPHA * x + y, ln_g_ref[...], ln_b_ref[...])


def swa_mixer_layer(x, w_qkv, b_qkv, sinks, w_o, ln_g, ln_b, *, ts):
    bsz, seq, d = x.shape
    cq = w_o.shape[0]
    ckv = (w_qkv.shape[1] - cq) // 2
    dh = C_HEAD_DIM
    inv = ROPE_THETA ** (-jnp.arange(0, dh, 2, dtype=F32) / dh)
    ang = jnp.arange(seq, dtype=F32)[:, None] * inv[None, :]
    reps = LANES // (dh // 2)
    sign = jnp.tile(jnp.concatenate([-jnp.ones((dh // 2,), F32), jnp.ones((dh // 2,), F32)]), LANES // dh)
    cos_t = jnp.tile(jnp.cos(ang), (1, reps))
    sin_t = jnp.tile(jnp.sin(ang), (1, reps)) * sign[None, :]
    args = (x, w_qkv.astype(BF16), b_qkv.reshape(1, -1), cos_t, sin_t, sinks.astype(F32),
            w_o.astype(BF16), ln_g.reshape(1, -1), ln_b.reshape(1, -1))
    in_specs = [pl.BlockSpec((1, ts, d), lambda b, j: (b, j, 0)),
                _full(args[1].shape), _full(args[2].shape),
                pl.BlockSpec((ts, LANES), lambda b, j: (j, 0)),
                pl.BlockSpec((ts, LANES), lambda b, j: (j, 0)),
                pl.BlockSpec(memory_space=pltpu.SMEM),
                _full(args[6].shape), _full(args[7].shape), _full(args[8].shape)]
    return pl.pallas_call(
        functools.partial(_swa_kernel, ts=ts, cq=cq, ckv=ckv),
        grid=(bsz, seq // ts),
        in_specs=in_specs,
        out_specs=pl.BlockSpec((1, ts, d), lambda b, j: (b, j, 0)),
        out_shape=jax.ShapeDtypeStruct((bsz, seq, d), F32),
        scratch_shapes=[pltpu.VMEM((CHUNK, ckv), F32), pltpu.VMEM((CHUNK, ckv), F32)],
        compiler_params=_cparams(("arbitrary", "arbitrary")),
        name="swa_mixer",
    )(*args)


def _slot_kernel(route_ref, pstart_ref, dest_ref, carry_ref, *, tb):
    @pl.when(pl.program_id(0) == 0)
    def _():
        carry_ref[...] = pstart_ref[...]

    rec = route_ref[...]
    lane = lax.broadcasted_iota(jnp.int32, (tb, ROUTE_W), 1)
    e0 = rec[:, 0:1].astype(jnp.int32)
    e1 = rec[:, 1:2].astype(jnp.int32)
    oh0 = lane == e0
    oh1 = lane == e1
    ohs = jnp.where(jnp.logical_or(oh0, oh1), 1.0, 0.0)
    r = lax.broadcasted_iota(jnp.int32, (tb, tb), 0)
    c = lax.broadcasted_iota(jnp.int32, (tb, tb), 1)
    before = jnp.where(c < r, 1.0, 0.0).astype(BF16)
    prefix = _dot(before, ohs.astype(BF16)) + carry_ref[...]
    d0 = jnp.sum(jnp.where(oh0, prefix, 0.0), axis=1, keepdims=True)
    d1 = jnp.sum(jnp.where(oh1, prefix, 0.0), axis=1, keepdims=True)
    dest = jnp.where(lane == 0, d0, jnp.where(lane == 1, d1, 0.0))
    dest_ref[...] = dest.T[:DEST_ROWS].astype(jnp.int32)
    carry_ref[...] += jnp.sum(ohs, axis=0, keepdims=True)


def moe_slots(route, pstart, *, tb):
    n = route.shape[0]
    return pl.pallas_call(
        functools.partial(_slot_kernel, tb=tb),
        grid=(n // tb,),
        in_specs=[pl.BlockSpec((tb, ROUTE_W), lambda i: (i, 0)), _full((1, ROUTE_W))],
        out_specs=pl.BlockSpec((DEST_ROWS, tb), lambda i: (0, i)),
        out_shape=jax.ShapeDtypeStruct((DEST_ROWS, n), jnp.int32),
        scratch_shapes=[pltpu.VMEM((1, ROUTE_W), F32)],
        compiler_params=_cparams(("arbitrary",)),
        name="moe_slots",
    )(route, pstart)


def _ffn_kernel(blk_exp_ref, new_exp_ref, nblk_ref, xs_ref, w1_ref, w3_ref, w2_ref, ys_ref,
                w1_b, w3_b, w2_b):
    i = pl.program_id(0)
    used = i < nblk_ref[0]

    @pl.when(jnp.logical_and(used, new_exp_ref[i] == 1))
    def _():
        w1_b[...] = w1_ref[0].astype(BF16)
        w3_b[...] = w3_ref[0].astype(BF16)
        w2_b[...] = w2_ref[0].astype(BF16)

    @pl.when(used)
    def _():
        x = xs_ref[...].astype(BF16)
        h1 = _dot(x, w1_b[...])
        h3 = _dot(x, w3_b[...])
        h = (_silu(h1) * h3).astype(BF16)
        ys_ref[...] = _dot(h, w2_b[...])

    @pl.when(jnp.logical_not(used))
    def _():
        ys_ref[...] = jnp.zeros_like(ys_ref)


def moe_ffn(xs, blk_exp, new_exp, nblk, w1, w3, w2, *, bm):
    n_pad, d = xs.shape
    de = w1.shape[2]
    n_blk = n_pad // bm

    def x_map(i, be, ne, nb):
        return (jnp.minimum(i, nb[0] - 1), 0)

    def w_map(i, be, ne, nb):
        return (be[i], 0, 0)

    return pl.pallas_call(
        _ffn_kernel,
        grid_spec=pltpu.PrefetchScalarGridSpec(
            num_scalar_prefetch=3,
            grid=(n_blk,),
            in_specs=[pl.BlockSpec((bm, d), x_map),
                      pl.BlockSpec((1, d, de), w_map),
                      pl.BlockSpec((1, d, de), w_map),
                      pl.BlockSpec((1, de, d), w_map)],
            out_specs=pl.BlockSpec((bm, d), lambda i, be, ne, nb: (i, 0)),
            scratch_shapes=[pltpu.VMEM((d, de), BF16), pltpu.VMEM((d, de), BF16),
                            pltpu.VMEM((de, d), BF16)]),
        out_shape=jax.ShapeDtypeStruct((n_pad, d), F32),
        compiler_params=_cparams(("arbitrary",)),
        name="moe_ffn",
    )(blk_exp, new_exp, nblk, xs, w1, w3, w2)


def _combine_kernel(x_ref, g0_ref, g1_ref, route_ref, ln_g_ref, ln_b_ref, o_ref):
    rec = route_ref[...]
    y = rec[:, 2:3] * g0_ref[0] + rec[:, 3:4] * g1_ref[0]
    o_ref[...] = _ln(DN_ALPHA * x_ref[...] + y, ln_g_ref[...], ln_b_ref[...])


def moe_combine(x, g, route, ln_g, ln_b, *, tb):
    n, d = x.shape
    row_spec = pl.BlockSpec((tb, d), lambda i: (i, 0))
    return pl.pallas_call(
        _combine_kernel,
        grid=(n // tb,),
        in_specs=[row_spec,
                  pl.BlockSpec((1, tb, d), lambda i: (0, i, 0)),
                  pl.BlockSpec((1, tb, d), lambda i: (1, i, 0)),
                  pl.BlockSpec((tb, ROUTE_W), lambda i: (i, 0)),
                  _full((1, d)), _full((1, d))],
        out_specs=row_spec,
        out_shape=jax.ShapeDtypeStruct((n, d), F32),
        compiler_params=_cparams(("arbitrary",)),
        name="moe_combine",
    )(x, g, g, route, ln_g.reshape(1, -1), ln_b.reshape(1, -1))


MOE_BM = 256


def hierarchical_moe_layer(x2, route, counts, w1, w3, w2, ln_g, ln_b):
    bsz, seq, d = x2.shape
    n = bsz * seq
    bm = MOE_BM
    xf = x2.reshape(n, d)
    rt = route.reshape(n, ROUTE_W)
    n_blk = (2 * n) // bm + N_EXPERTS
    n_pad = n_blk * bm
    cnt = counts[0, :N_EXPERTS].astype(jnp.int32)
    pcnt = (cnt + bm - 1) // bm * bm
    pends = jnp.cumsum(pcnt)
    pstart = pends - pcnt
    nblk = (pends[-1] // bm).astype(jnp.int32).reshape(1)
    blk_row = jnp.arange(n_blk, dtype=jnp.int32) * bm
    blk_exp = jnp.minimum(jnp.sum((pends[None, :] <= blk_row[:, None]).astype(jnp.int32), axis=1),
                          N_EXPERTS - 1)
    last_exp = blk_exp[jnp.maximum(nblk[0] - 1, 0)]
    blk_exp = jnp.where(jnp.arange(n_blk) < nblk[0], blk_exp, last_exp)
    new_exp = jnp.concatenate([jnp.ones((1,), jnp.int32),
                               (blk_exp[1:] != blk_exp[:-1]).astype(jnp.int32)])
    pstart_rec = jnp.zeros((1, ROUTE_W), F32).at[0, :N_EXPERTS].set(pstart.astype(F32))
    dest = moe_slots(rt, pstart_rec, tb=min(512, n))
    xs = moe_dispatch(xf, dest[0], dest[1], n_pad)
    ys = moe_ffn(xs, blk_exp, new_exp, nblk, w1, w3, w2, bm=bm)
    g = moe_gather(ys, dest[0], dest[1])
    out = moe_combine(xf, g, rt, ln_g, ln_b, tb=min(512, n))
    return out.reshape(bsz, seq, d)


SC_ROWS = 64


def _sc_mesh():
    return plsc.VectorSubcoreMesh(core_axis_name="c", subcore_axis_name="s")


def moe_dispatch(xf, dest0, dest1, n_pad):
    n, d = xf.shape
    info = plsc.get_sparse_core_info()
    nw = info.num_cores * info.num_subcores
    per_w = n // nw
    r = min(SC_ROWS, per_w)

    def body(x_hbm, d0_hbm, d1_hbm, xs_hbm, i0_v, i1_v, rows_v, sem):
        wid = lax.axis_index("s") * info.num_cores + lax.axis_index("c")

        @pl.loop(0, per_w // r)
        def _(c):
            base = pl.multiple_of(wid * per_w + c * r, 8)
            pltpu.sync_copy(d0_hbm.at[pl.ds(base, r)], i0_v)
            pltpu.sync_copy(d1_hbm.at[pl.ds(base, r)], i1_v)
            pltpu.sync_copy(x_hbm.at[pl.ds(base, r)], rows_v)
            pltpu.async_copy(rows_v, xs_hbm.at[i0_v], sem).wait()
            pltpu.async_copy(rows_v, xs_hbm.at[i1_v], sem).wait()

    return pl.kernel(
        body, out_type=jax.ShapeDtypeStruct((n_pad, d), xf.dtype), mesh=_sc_mesh(),
        scratch_types=[pltpu.VMEM((r,), jnp.int32), pltpu.VMEM((r,), jnp.int32),
                       pltpu.VMEM((r, d), xf.dtype), pltpu.SemaphoreType.DMA],
        name="moe_dispatch",
    )(xf, dest0, dest1)


def moe_gather(ys, dest0, dest1):
    n = dest0.shape[0]
    d = ys.shape[1]
    info = plsc.get_sparse_core_info()
    nw = info.num_cores * info.num_subcores
    per_w = n // nw
    r = min(SC_ROWS, per_w)

    def body(ys_hbm, d0_hbm, d1_hbm, g_hbm, i_v, rows_v, sem):
        wid = lax.axis_index("s") * info.num_cores + lax.axis_index("c")

        @pl.loop(0, per_w // r)
        def _(c):
            base = pl.multiple_of(wid * per_w + c * r, 8)
            for k, d_hbm in enumerate((d0_hbm, d1_hbm)):
                pltpu.sync_copy(d_hbm.at[pl.ds(base, r)], i_v)
                pltpu.async_copy(ys_hbm.at[i_v], rows_v, sem).wait()
                pltpu.sync_copy(rows_v, g_hbm.at[k, pl.ds(base, r)])

    return pl.kernel(
        body, out_type=jax.ShapeDtypeStruct((2, n, d), ys.dtype), mesh=_sc_mesh(),
        scratch_types=[pltpu.VMEM((r,), jnp.int32), pltpu.VMEM((r, d), ys.dtype),
                       pltpu.SemaphoreType.DMA],
        name="moe_gather",
    )(ys, dest0, dest1)


def kernel(x, mem, ln_g, ln_b, ev_w_in, ev_gm_ln_g, ev_gm_ln_b, ev_gm_ws, ev_gm_bs, ev_conv_w, ev_conv_b, ev_wq, ev_wk, ev_wv, ev_w_if, ev_b_if, ev_norm_w, ev_skip, ev_w_out, od_w_qkv, od_b_qkv, od_sinks, od_w_o, xa_wq, xa_wkv, xa_wo, moe_w_rg, moe_b_rg, moe_w_re, moe_b_re, moe_w1, moe_w3, moe_w2):
    bsz, seq, d = x.shape
    m_len = mem.shape[1]
    depth = ln_g.shape[0]
    ts = min(256, seq)
    memf = mem.reshape(bsz * m_len, d)
    for l in range(depth):
        if l % 2 == 0:
            e = l // 2
            x = even_mixer_layer(x, ev_w_in[e], ev_gm_ln_g[e], ev_gm_ln_b[e], ev_gm_ws[e], ev_gm_bs[e],
                                 ev_conv_w[e], ev_conv_b[e], ev_wq[e], ev_wk[e], ev_wv[e], ev_w_if[e],
                                 ev_b_if[e], ev_norm_w[e], ev_skip[e], ev_w_out[e],
                                 ln_g[l, 0], ln_b[l, 0], ts=ts)
        else:
            o = l // 2
            x = swa_mixer_layer(x, od_w_qkv[o], od_b_qkv[o], od_sinks[o], od_w_o[o],
                                ln_g[l, 0], ln_b[l, 0], ts=ts)
        kv = matmul(memf, xa_wkv[l], tm=min(512, bsz * m_len), tn=512, out_dtype=BF16)
        kv = kv.reshape(bsz, m_len, 2 * d)
        x, route, counts = xattn_router_layer(x, kv, xa_wq[l], xa_wo[l], ln_g[l, 1], ln_b[l, 1],
                                              moe_w_rg[l], moe_b_rg[l], moe_w_re[l], moe_b_re[l], ts=ts)
        x = hierarchical_moe_layer(x, route, counts, moe_w1[l], moe_w3[l], moe_w2[l],
                                   ln_g[l, 2], ln_b[l, 2])
    return x
```

```python
import functools
import math

import jax
import jax.numpy as jnp
from jax import lax
from jax.experimental import pallas as pl
from jax.experimental.pallas import tpu as pltpu
from jax.experimental.pallas import tpu_sc as plsc

F32 = jnp.float32
BF16 = jnp.bfloat16

A_GROUPS = 4
CHUNK = 128
B_HEADS = 4
B_CONV = 4
C_HEAD_DIM = 64
C_KV_HEADS = 4
X_HEADS = 4
N_GROUPS = 4
EXPERTS_PER_GROUP = 8
N_EXPERTS = N_GROUPS * EXPERTS_PER_GROUP
ROPE_THETA = 10000.0
LN_EPS = 1e-5
DEPTH = 2
DN_ALPHA = (2 * DEPTH) ** 0.25

LANES = 128
VMEM_LIMIT = 48 * 1024 * 1024
NEG = -1e30


def _cparams(sem):
    return pltpu.CompilerParams(dimension_semantics=sem, vmem_limit_bytes=VMEM_LIMIT)


def _full(shape):
    nd = len(shape)
    return pl.BlockSpec(shape, lambda *_: (0,) * nd)


def _dot(a, b):
    return jnp.dot(a, b, preferred_element_type=F32)


def _dot_nt(a, b):
    return lax.dot_general(a, b, (((1,), (1,)), ((), ())), preferred_element_type=F32)


def _dot_tn(a, b):
    return lax.dot_general(a, b, (((0,), (0,)), ((), ())), preferred_element_type=F32)


def _split_dot(a, b_bf16):
    hi = a.astype(BF16)
    lo = (a - hi.astype(F32)).astype(BF16)
    return _dot(hi, b_bf16) + _dot(lo, b_bf16)


def _ln(x, g, b):
    mu = jnp.mean(x, axis=-1, keepdims=True)
    xc = x - mu
    var = jnp.mean(xc * xc, axis=-1, keepdims=True)
    return xc * lax.rsqrt(var + LN_EPS) * g + b


def _silu(x):
    return x * (1.0 / (1.0 + jnp.exp(-x)))


def _gelu(x):
    return 0.5 * x * (1.0 + jnp.tanh(math.sqrt(2.0 / math.pi) * (x + 0.044715 * (x * x * x))))


def _log_sigmoid(x):
    return jnp.minimum(x, 0.0) - jnp.log(1.0 + jnp.exp(-jnp.abs(x)))


def _even_kernel(x_ref, w_in_ref, gm_g_ref, gm_b_ref, gm_w_ref, gm_bias_ref,
                 conv_w_ref, conv_b_ref, wq_ref, wk_ref, wv_ref, wif_t_ref, wif_ref, bif_t_ref,
                 bif_ref, norm_w_ref, skip_ref, w_out_ref, ln_g_ref, ln_b_ref,
                 o_ref,
                 xm_buf, ct_ref, n_ref, m_ref, *, ts, aw, bw):
    dh = bw // B_HEADS
    agd = aw // A_GROUPS
    nck = ts // CHUNK
    pad = 8
    j = pl.program_id(1)

    @pl.when(j == 0)
    def _():
        xm_buf[0:pad, :] = jnp.zeros((pad, bw), F32)
        ct_ref[...] = jnp.zeros_like(ct_ref)
        n_ref[...] = jnp.zeros_like(n_ref)
        m_ref[...] = jnp.zeros_like(m_ref)

    x = x_ref[0]
    proj = _dot(x.astype(BF16), w_in_ref[...])
    a_u = _gelu(proj[:, :aw])
    a_v = _gelu(proj[:, aw:2 * aw])
    xm = proj[:, 2 * aw:2 * aw + bw]
    z = proj[:, 2 * aw + bw:]

    vn = _ln(a_v, gm_g_ref[...], gm_b_ref[...]).astype(BF16)
    ya_chunks = []
    for c in range(nck):
        cols = []
        for g in range(A_GROUPS):
            v_cg = vn[c * CHUNK:(c + 1) * CHUNK, g * agd:(g + 1) * agd]
            cols.append(_dot(gm_w_ref[g], v_cg))
        ya_chunks.append(jnp.concatenate(cols, axis=1) + gm_bias_ref[...])
    y_a = a_u * jnp.concatenate(ya_chunks, axis=0)

    xm_buf[pad:pad + ts, :] = xm
    conv = conv_b_ref[...] + conv_w_ref[B_CONV - 1:B_CONV, :] * xm
    for k in range(B_CONV - 1):
        sh = B_CONV - 1 - k
        conv = conv + conv_w_ref[k:k + 1, :] * xm_buf[pad - sh:pad - sh + ts, :]
    xm_buf[pad - (B_CONV - 1):pad, :] = xm_buf[pad + ts - (B_CONV - 1):pad + ts, :]
    xc = _silu(conv)
    xc_b = xc.astype(BF16)
    xm_b = xm.astype(BF16)
    q = _dot(xc_b, wq_ref[...])
    k_ = _dot(xc_b, wk_ref[...])
    v = _dot(xm_b, wv_ref[...])
    gate_in = jnp.concatenate([q, k_, v], axis=1).astype(BF16)
    gates = _dot(gate_in, wif_ref[...]) + bif_ref[...]
    gates_t = _dot_nt(wif_t_ref[...], gate_in) + bif_t_ref[...]
    ig_c = gates[:, :B_HEADS]
    lf_c = _log_sigmoid(gates[:, B_HEADS:])
    ig_r = gates_t[:B_HEADS, :]
    lf_r = _log_sigmoid(gates_t[B_HEADS:, :])

    row = lax.broadcasted_iota(jnp.int32, (CHUNK, CHUNK), 0)
    col = lax.broadcasted_iota(jnp.int32, (CHUNK, CHUNK), 1)
    causal = col <= row
    tril = jnp.where(causal, 1.0, 0.0).astype(BF16)
    triu = jnp.where(row <= col, 1.0, 0.0).astype(BF16)
    q_b = q.astype(BF16)
    k_b = (k_ * dh ** -0.5).astype(BF16)
    v_b = v.astype(BF16)

    h_chunks = []
    for c in range(nck):
        sl = slice(c * CHUNK, (c + 1) * CHUNK)
        b_c = _split_dot_left(tril, lf_c[sl, :])
        b_r = _split_dot(lf_r[:, sl], triu)
        heads = []
        for h in range(B_HEADS):
            hs = slice(h * dh, (h + 1) * dh)
            qh, kh, vh = q_b[sl, hs], k_b[sl, hs], v_b[sl, hs]
            bcol = b_c[:, h:h + 1]
            brow = b_r[h:h + 1, :]
            igrow = ig_r[h:h + 1, sl]
            igcol = ig_c[sl, h:h + 1]
            m_prev = m_ref[h:h + 1, 0:1]
            dmat = jnp.where(causal, bcol - brow + igrow, NEG)
            inter = bcol + m_prev
            mt = jnp.maximum(jnp.max(dmat, axis=1, keepdims=True), inter)
            w_intra = jnp.exp(dmat - mt)
            w_state = jnp.exp(inter - mt)
            s = _dot_nt(qh, kh) * w_intra
            ct = ct_ref[h]
            n_row = n_ref[h:h + 1, :]
            num = _dot(s.astype(BF16), vh) + w_state * _dot(qh, ct.astype(BF16))
            qf = qh.astype(F32)
            nq = jnp.sum(s, axis=1, keepdims=True) + w_state * jnp.sum(qf * n_row, axis=1, keepdims=True)
            hv = num / jnp.maximum(jnp.abs(nq), jnp.exp(-mt))
            bl = brow[:, CHUNK - 1:CHUNK]
            g_row = bl - brow + igrow
            g_col = bl - bcol + igcol
            m_new = jnp.maximum(bl + m_prev, jnp.max(g_row, axis=1, keepdims=True))
            decay = jnp.exp(bl + m_prev - m_new)
            wg_col = jnp.exp(g_col - m_new)
            kw = kh.astype(F32) * wg_col
            ct_ref[h] = decay * ct + _dot_tn(kw.astype(BF16), vh)
            n_ref[h:h + 1, :] = decay * n_row + jnp.sum(kw, axis=0, keepdims=True)
            m_ref[h:h + 1, :] = jnp.broadcast_to(m_new, (1, LANES))
            hc = hv - jnp.mean(hv, axis=1, keepdims=True)
            hn = hc * lax.rsqrt(jnp.mean(hc * hc, axis=1, keepdims=True) + LN_EPS)
            heads.append(hn)
        h_chunks.append(jnp.concatenate(heads, axis=1))
    hn_all = jnp.concatenate(h_chunks, axis=0) if nck > 1 else h_chunks[0]
    y_b = (hn_all * norm_w_ref[...] + skip_ref[...] * xc) * _silu(z)

    mix = jnp.concatenate([y_a, y_b], axis=1).astype(BF16)
    y = _dot(mix, w_out_ref[...])
    o_ref[0] = _ln(DN_ALPHA * x + y, ln_g_ref[...], ln_b_ref[...])


def _split_dot_left(a_bf16, b):
    hi = b.astype(BF16)
    lo = (b - hi.astype(F32)).astype(BF16)
    return _dot(a_bf16, hi) + _dot(a_bf16, lo)


def _block_diag(w):
    hh, d, _ = w.shape
    eye = jnp.eye(hh, dtype=w.dtype)
    return jnp.einsum('hde,hg->hdge', w, eye).reshape(hh * d, hh * d)


def even_mixer_layer(x, w_in, gm_ln_g, gm_ln_b, gm_ws, gm_bs, conv_w, conv_b, wq, wk, wv, w_if,
                     b_if, norm_w, skip, w_out, ln_g, ln_b, *, ts):
    bsz, seq, d = x.shape
    aw = gm_ln_g.shape[0]
    bw = conv_b.shape[0]
    agd = aw // A_GROUPS
    causal = jnp.tril(jnp.ones((CHUNK, CHUNK), dtype=bool))
    gm_w = jnp.where(causal[None], gm_ws, 0.0).astype(BF16)
    gm_bias = jnp.repeat(gm_bs.T, agd, axis=1)
    row = lambda a: a.reshape(1, -1)
    args = (x, w_in.astype(BF16), row(gm_ln_g), row(gm_ln_b), gm_w, gm_bias,
            conv_w, row(conv_b), _block_diag(wq).astype(BF16), _block_diag(wk).astype(BF16),
            _block_diag(wv).astype(BF16), w_if.T.astype(BF16), w_if.astype(BF16),
            b_if.reshape(-1, 1), row(b_if), row(norm_w), row(skip), w_out.astype(BF16),
            row(ln_g), row(ln_b))
    in_specs = [pl.BlockSpec((1, ts, d), lambda b, j: (b, j, 0))] + [_full(a.shape) for a in args[1:]]
    dh = bw // B_HEADS
    return pl.pallas_call(
        functools.partial(_even_kernel, ts=ts, aw=aw, bw=bw),
        grid=(bsz, seq // ts),
        in_specs=in_specs,
        out_specs=pl.BlockSpec((1, ts, d), lambda b, j: (b, j, 0)),
        out_shape=jax.ShapeDtypeStruct((bsz, seq, d), F32),
        scratch_shapes=[pltpu.VMEM((8 + ts, bw), F32),
                        pltpu.VMEM((B_HEADS, dh, dh), F32),
                        pltpu.VMEM((B_HEADS, dh), F32),
                        pltpu.VMEM((B_HEADS, LANES), F32)],
        compiler_params=_cparams(("arbitrary", "arbitrary")),
        name="even_mixer",
    )(*args)


def _matmul_kernel(a_ref, b_ref, o_ref):
    o_ref[...] = _dot(a_ref[...].astype(BF16), b_ref[...].astype(BF16)).astype(o_ref.dtype)


def matmul(a, b, *, tm, tn, out_dtype):
    m, k = a.shape
    n = b.shape[1]
    return pl.pallas_call(
        _matmul_kernel,
        grid=(n // tn, m // tm),
        in_specs=[pl.BlockSpec((tm, k), lambda j, i: (i, 0)),
                  pl.BlockSpec((k, tn), lambda j, i: (0, j))],
        out_specs=pl.BlockSpec((tm, tn), lambda j, i: (i, j)),
        out_shape=jax.ShapeDtypeStruct((m, n), out_dtype),
        compiler_params=_cparams(("arbitrary", "arbitrary")),
        name="matmul",
    )(a, b)


ROUTE_W = 128
XATTN_TS = 1024
DEST_ROWS = 8


def _xattn_kernel(x_ref, kv_ref, wq_ref, wo_ref, ln_g_ref, ln_b_ref, wr_ref, br_ref,
                  o_ref, route_ref, cnt_ref, *, d):
    dh = d // X_HEADS
    first = jnp.logical_and(pl.program_id(0) == 0, pl.program_id(1) == 0)

    @pl.when(first)
    def _():
        cnt_ref[...] = jnp.zeros_like(cnt_ref)

    x = x_ref[0]
    ts = x.shape[0]
    q = (_dot(x.astype(BF16), wq_ref[...]) * dh ** -0.5).astype(BF16)
    outs = []
    for h in range(X_HEADS):
        kh = kv_ref[0, :, h * dh:(h + 1) * dh]
        vh = kv_ref[0, :, d + h * dh:d + (h + 1) * dh]
        s = _dot_nt(q[:, h * dh:(h + 1) * dh], kh)
        s = s - jnp.max(s, axis=1, keepdims=True)
        p = jnp.exp(s)
        l = jnp.sum(p, axis=1, keepdims=True)
        outs.append(_dot(p.astype(BF16), vh) / l)
    o = jnp.concatenate(outs, axis=1).astype(BF16)
    y = _dot(o, wo_ref[...])
    x2 = _ln(DN_ALPHA * x + y, ln_g_ref[...], ln_b_ref[...])
    o_ref[0] = x2

    x_hi = x2.astype(BF16)
    x_lo = (x2 - x_hi.astype(F32)).astype(BF16)
    logits = (_dot(x_hi, wr_ref[0]) + _dot(x_lo, wr_ref[0]) + _dot(x_hi, wr_ref[1])) + br_ref[...]
    lane = lax.broadcasted_iota(jnp.int32, (ts, ROUTE_W), 1)
    is_g = lane < N_GROUPS
    lg = jnp.where(is_g, logits, NEG)
    mg = jnp.max(lg, axis=1, keepdims=True)
    gi = jnp.min(jnp.where(jnp.logical_and(is_g, lg == mg), lane, ROUTE_W), axis=1, keepdims=True)
    gate_g = 1.0 / jnp.sum(jnp.where(is_g, jnp.exp(lg - mg), 0.0), axis=1, keepdims=True)
    lo = N_GROUPS + gi * EXPERTS_PER_GROUP
    in_grp = jnp.logical_and(lane >= lo, lane < lo + EXPERTS_PER_GROUP)
    le = jnp.where(in_grp, logits, NEG)
    v1 = jnp.max(le, axis=1, keepdims=True)
    i1 = jnp.min(jnp.where(jnp.logical_and(in_grp, le == v1), lane, ROUTE_W), axis=1, keepdims=True)
    le2 = jnp.where(lane == i1, NEG, le)
    v2 = jnp.max(le2, axis=1, keepdims=True)
    i2 = jnp.min(jnp.where(jnp.logical_and(in_grp, le2 == v2), lane, ROUTE_W), axis=1, keepdims=True)
    e21 = jnp.exp(v2 - v1)
    p1 = 1.0 / (1.0 + e21)
    p2 = e21 * p1
    e1 = (i1 - N_GROUPS).astype(F32)
    e2 = (i2 - N_GROUPS).astype(F32)
    rec = jnp.where(lane == 0, e1, 0.0)
    rec = jnp.where(lane == 1, e2, rec)
    rec = jnp.where(lane == 2, gate_g * p1, rec)
    rec = jnp.where(lane == 3, gate_g * p2, rec)
    route_ref[0] = rec
    sel = jnp.logical_or(lane == i1 - N_GROUPS, lane == i2 - N_GROUPS)
    cnt_ref[...] += jnp.sum(jnp.where(sel, 1.0, 0.0), axis=0, keepdims=True)


def xattn_router_layer(x, kv, wq, wo, ln_g, ln_b, w_rg, b_rg, w_re, b_re, *, ts):
    bsz, seq, d = x.shape
    m_len = kv.shape[1]
    wr = jnp.zeros((d, ROUTE_W), F32).at[:, :N_GROUPS].set(w_rg).at[:, N_GROUPS:N_GROUPS + N_EXPERTS].set(w_re)
    br = jnp.zeros((1, ROUTE_W), F32).at[0, :N_GROUPS].set(b_rg).at[0, N_GROUPS:N_GROUPS + N_EXPERTS].set(b_re)
    wr_hi = wr.astype(BF16)
    wr_lo = (wr - wr_hi.astype(F32)).astype(BF16)
    wr = jnp.stack([wr_hi, wr_lo])
    args = (x, kv, wq.astype(BF16), wo.astype(BF16), ln_g.reshape(1, -1), ln_b.reshape(1, -1), wr, br)
    in_specs = [pl.BlockSpec((1, ts, d), lambda b, j: (b, j, 0)),
                pl.BlockSpec((1, m_len, 2 * d), lambda b, j: (b, 0, 0))] + [_full(a.shape) for a in args[2:]]
    return pl.pallas_call(
        functools.partial(_xattn_kernel, d=d),
        grid=(bsz, seq // ts),
        in_specs=in_specs,
        out_specs=[pl.BlockSpec((1, ts, d), lambda b, j: (b, j, 0)),
                   pl.BlockSpec((1, ts, ROUTE_W), lambda b, j: (b, j, 0)),
                   pl.BlockSpec((1, ROUTE_W), lambda b, j: (0, 0))],
        out_shape=[jax.ShapeDtypeStruct((bsz, seq, d), F32),
                   jax.ShapeDtypeStruct((bsz, seq, ROUTE_W), F32),
                   jax.ShapeDtypeStruct((1, ROUTE_W), F32)],
        compiler_params=_cparams(("arbitrary", "arbitrary")),
        name="xattn_router",
    )(*args)


def _swa_kernel(x_ref, wqkv_ref, bqkv_ref, cos_ref, sin_ref, sink_ref, wo_ref, ln_g_ref, ln_b_ref,
                o_ref, kprev, vprev, *, ts, cq, ckv):
    j = pl.program_id(1)
    nb = ts // CHUNK
    dh = C_HEAD_DIM
    grp = (cq // dh) // C_KV_HEADS

    @pl.when(j == 0)
    def _():
        kprev[...] = jnp.zeros_like(kprev)
        vprev[...] = jnp.zeros_like(vprev)

    x = x_ref[0]
    qkv = _dot(x.astype(BF16), wqkv_ref[...]) + bqkv_ref[...]
    cos = cos_ref[...]
    sin = sin_ref[...]
    lane = lax.broadcasted_iota(jnp.int32, (ts, LANES), 1)
    first_half = (lane % dh) < (dh // 2)

    def rope(t):
        outs = []
        for c in range(t.shape[1] // LANES):
            tc = t[:, c * LANES:(c + 1) * LANES]
            rot = jnp.where(first_half, pltpu.roll(tc, LANES - dh // 2, 1), pltpu.roll(tc, dh // 2, 1))
            outs.append(tc * cos + rot * sin)
        return jnp.concatenate(outs, axis=1)

    q = (rope(qkv[:, :cq]) * dh ** -0.5).astype(BF16)
    k = rope(qkv[:, cq:cq + ckv]).astype(BF16)
    v = qkv[:, cq + ckv:].astype(BF16)

    rows = grp * CHUNK
    r_i = lax.broadcasted_iota(jnp.int32, (rows, 2 * CHUNK), 0) % CHUNK
    c_i = lax.broadcasted_iota(jnp.int32, (rows, 2 * CHUNK), 1)
    band = jnp.logical_and(c_i > r_i, c_i <= r_i + CHUNK)
    r_g = lax.broadcasted_iota(jnp.int32, (rows, 1), 0) // CHUNK
    blocks = []
    for c in range(nb):
        sl = slice(c * CHUNK, (c + 1) * CHUNK)
        if c == 0:
            kb = jnp.concatenate([kprev[...].astype(BF16), k[sl]], axis=0)
            vb = jnp.concatenate([vprev[...].astype(BF16), v[sl]], axis=0)
            first_key = jnp.where(j > 0, 0, CHUNK)
            valid = jnp.logical_and(band, c_i >= first_key)
        else:
            kb = k[(c - 1) * CHUNK:(c + 1) * CHUNK]
            vb = v[(c - 1) * CHUNK:(c + 1) * CHUNK]
            valid = band
        heads = []
        for h in range(C_KV_HEADS):
            kh = kb[:, h * dh:(h + 1) * dh]
            vh = vb[:, h * dh:(h + 1) * dh]
            qs = jnp.concatenate([q[sl, (h * grp + g) * dh:(h * grp + g + 1) * dh] for g in range(grp)], axis=0)
            sink = jnp.zeros((rows, 1), F32)
            for g in range(grp):
                sink = jnp.where(r_g == g, sink_ref[h * grp + g], sink)
            s = jnp.where(valid, _dot_nt(qs, kh), NEG)
            m = jnp.maximum(jnp.max(s, axis=1, keepdims=True), sink)
            p = jnp.exp(s - m)
            den = jnp.sum(p, axis=1, keepdims=True) + jnp.exp(sink - m)
            o = _dot(p.astype(BF16), vh) / den
            heads.extend(o[g * CHUNK:(g + 1) * CHUNK] for g in range(grp))
        blocks.append(jnp.concatenate(heads, axis=1))
    kprev[...] = k[(nb - 1) * CHUNK:].astype(F32)
    vprev[...] = v[(nb - 1) * CHUNK:].astype(F32)
    att = (jnp.concatenate(blocks, axis=0) if nb > 1 else blocks[0]).astype(BF16)
    y = _dot(att, wo_ref[...])
    o_ref[0] = _ln(DN_ALPHA * x + y, ln_g_ref[...], ln_b_ref[...])


def swa_mixer_layer(x, w_qkv, b_qkv, sinks, w_o, ln_g, ln_b, *, ts):
    bsz, seq, d = x.shape
    cq = w_o.shape[0]
    ckv = (w_qkv.shape[1] - cq) // 2
    dh = C_HEAD_DIM
    inv = ROPE_THETA ** (-jnp.arange(0, dh, 2, dtype=F32) / dh)
    ang = jnp.arange(seq, dtype=F32)[:, None] * inv[None, :]
    reps = LANES // (dh // 2)
    sign = jnp.tile(jnp.concatenate([-jnp.ones((dh // 2,), F32), jnp.ones((dh // 2,), F32)]), LANES // dh)
    cos_t = jnp.tile(jnp.cos(ang), (1, reps))
    sin_t = jnp.tile(jnp.sin(ang), (1, reps)) * sign[None, :]
    args = (x, w_qkv.astype(BF16), b_qkv.reshape(1, -1), cos_t, sin_t, sinks.astype(F32),
            w_o.astype(BF16), ln_g.reshape(1, -1), ln_b.reshape(1, -1))
    in_specs = [pl.BlockSpec((1, ts, d), lambda b, j: (b, j, 0)),
                _full(args[1].shape), _full(args[2].shape),
                pl.BlockSpec((ts, LANES), lambda b, j: (j, 0)),
                pl.BlockSpec((ts, LANES), lambda b, j: (j, 0)),
                pl.BlockSpec(memory_space=pltpu.SMEM),
                _full(args[6].shape), _full(args[7].shape), _full(args[8].shape)]
    return pl.pallas_call(
        functools.partial(_swa_kernel, ts=ts, cq=cq, ckv=ckv),
        grid=(bsz, seq // ts),
        in_specs=in_specs,
        out_specs=pl.BlockSpec((1, ts, d), lambda b, j: (b, j, 0)),
        out_shape=jax.ShapeDtypeStruct((bsz, seq, d), F32),
        scratch_shapes=[pltpu.VMEM((CHUNK, ckv), F32), pltpu.VMEM((CHUNK, ckv), F32)],
        compiler_params=_cparams(("arbitrary", "arbitrary")),
        name="swa_mixer",
    )(*args)


def _slot_kernel(route_ref, pstart_ref, dest_ref, carry_ref, *, tb):
    @pl.when(pl.program_id(0) == 0)
    def _():
        carry_ref[...] = pstart_ref[...]

    rec = route_ref[...]
    lane = lax.broadcasted_iota(jnp.int32, (tb, ROUTE_W), 1)
    e0 = rec[:, 0:1].astype(jnp.int32)
    e1 = rec[:, 1:2].astype(jnp.int32)
    oh0 = lane == e0
    oh1 = lane == e1
    ohs = jnp.where(jnp.logical_or(oh0, oh1), 1.0, 0.0)
    r = lax.broadcasted_iota(jnp.int32, (tb, tb), 0)
    c = lax.broadcasted_iota(jnp.int32, (tb, tb), 1)
    before = jnp.where(c < r, 1.0, 0.0).astype(BF16)
    prefix = _dot(before, ohs.astype(BF16)) + carry_ref[...]
    d0 = jnp.sum(jnp.where(oh0, prefix, 0.0), axis=1, keepdims=True)
    d1 = jnp.sum(jnp.where(oh1, prefix, 0.0), axis=1, keepdims=True)
    dest = jnp.where(lane == 0, d0, jnp.where(lane == 1, d1, 0.0))
    dest_ref[...] = dest.T[:DEST_ROWS].astype(jnp.int32)
    carry_ref[...] += jnp.sum(ohs, axis=0, keepdims=True)


def moe_slots(route, pstart, *, tb):
    n = route.shape[0]
    return pl.pallas_call(
        functools.partial(_slot_kernel, tb=tb),
        grid=(n // tb,),
        in_specs=[pl.BlockSpec((tb, ROUTE_W), lambda i: (i, 0)), _full((1, ROUTE_W))],
        out_specs=pl.BlockSpec((DEST_ROWS, tb), lambda i: (0, i)),
        out_shape=jax.ShapeDtypeStruct((DEST_ROWS, n), jnp.int32),
        scratch_shapes=[pltpu.VMEM((1, ROUTE_W), F32)],
        compiler_params=_cparams(("arbitrary",)),
        name="moe_slots",
    )(route, pstart)


def _ffn_kernel(blk_exp_ref, new_exp_ref, nblk_ref, xs_ref, w1_ref, w3_ref, w2_ref, ys_ref,
                w1_b, w3_b, w2_b):
    i = pl.program_id(0)
    used = i < nblk_ref[0]

    @pl.when(jnp.logical_and(used, new_exp_ref[i] == 1))
    def _():
        w1_b[...] = w1_ref[0, 0].astype(BF16)
        w3_b[...] = w3_ref[0, 0].astype(BF16)
        w2_b[...] = w2_ref[0, 0].astype(BF16)

    @pl.when(used)
    def _():
        x = xs_ref[...].astype(BF16)
        h1 = _dot(x, w1_b[...])
        h3 = _dot(x, w3_b[...])
        h = (_silu(h1) * h3).astype(BF16)
        ys_ref[...] = _dot(h, w2_b[...])

    @pl.when(jnp.logical_not(used))
    def _():
        ys_ref[...] = jnp.zeros_like(ys_ref)


def moe_ffn(xs, blk_exp, new_exp, nblk, w1, w3, w2, *, layer, bm):
    n_pad, d = xs.shape
    de = w1.shape[3]
    n_blk = n_pad // bm

    def x_map(i, be, ne, nb):
        return (jnp.minimum(i, nb[0] - 1), 0)

    def w_map(i, be, ne, nb):
        return (layer, be[i], 0, 0)

    return pl.pallas_call(
        _ffn_kernel,
        grid_spec=pltpu.PrefetchScalarGridSpec(
            num_scalar_prefetch=3,
            grid=(n_blk,),
            in_specs=[pl.BlockSpec((bm, d), x_map),
                      pl.BlockSpec((1, 1, d, de), w_map),
                      pl.BlockSpec((1, 1, d, de), w_map),
                      pl.BlockSpec((1, 1, de, d), w_map)],
            out_specs=pl.BlockSpec((bm, d), lambda i, be, ne, nb: (i, 0)),
            scratch_shapes=[pltpu.VMEM((d, de), BF16), pltpu.VMEM((d, de), BF16),
                            pltpu.VMEM((de, d), BF16)]),
        out_shape=jax.ShapeDtypeStruct((n_pad, d), F32),
        compiler_params=_cparams(("arbitrary",)),
        name="moe_ffn",
    )(blk_exp, new_exp, nblk, xs, w1, w3, w2)


def _combine_kernel(x_ref, g0_ref, g1_ref, route_ref, ln_g_ref, ln_b_ref, o_ref):
    rec = route_ref[...]
    y = rec[:, 2:3] * g0_ref[0] + rec[:, 3:4] * g1_ref[0]
    o_ref[...] = _ln(DN_ALPHA * x_ref[...] + y, ln_g_ref[...], ln_b_ref[...])


def moe_combine(x, g, route, ln_g, ln_b, *, tb):
    n, d = x.shape
    row_spec = pl.BlockSpec((tb, d), lambda i: (i, 0))
    return pl.pallas_call(
        _combine_kernel,
        grid=(n // tb,),
        in_specs=[row_spec,
                  pl.BlockSpec((1, tb, d), lambda i: (0, i, 0)),
                  pl.BlockSpec((1, tb, d), lambda i: (1, i, 0)),
                  pl.BlockSpec((tb, ROUTE_W), lambda i: (i, 0)),
                  _full((1, d)), _full((1, d))],
        out_specs=row_spec,
        out_shape=jax.ShapeDtypeStruct((n, d), F32),
        compiler_params=_cparams(("arbitrary",)),
        name="moe_combine",
    )(x, g, g, route, ln_g.reshape(1, -1), ln_b.reshape(1, -1))


MOE_BM = 512
MIXER_TS = 512


def hierarchical_moe_layer(x2, route, counts, w1, w3, w2, ln_g, ln_b, *, layer):
    bsz, seq, d = x2.shape
    n = bsz * seq
    bm = MOE_BM
    xf = x2.reshape(n, d)
    rt = route.reshape(n, ROUTE_W)
    n_blk = (2 * n) // bm + N_EXPERTS
    n_pad = n_blk * bm
    cnt = counts[0, :N_EXPERTS].astype(jnp.int32)
    pcnt = (cnt + bm - 1) // bm * bm
    pends = jnp.cumsum(pcnt)
    pstart = pends - pcnt
    nblk = (pends[-1] // bm).astype(jnp.int32).reshape(1)
    blk_row = jnp.arange(n_blk, dtype=jnp.int32) * bm
    blk_exp = jnp.minimum(jnp.sum((pends[None, :] <= blk_row[:, None]).astype(jnp.int32), axis=1),
                          N_EXPERTS - 1)
    last_exp = blk_exp[jnp.maximum(nblk[0] - 1, 0)]
    blk_exp = jnp.where(jnp.arange(n_blk) < nblk[0], blk_exp, last_exp)
    new_exp = jnp.concatenate([jnp.ones((1,), jnp.int32),
                               (blk_exp[1:] != blk_exp[:-1]).astype(jnp.int32)])
    pstart_rec = jnp.zeros((1, ROUTE_W), F32).at[0, :N_EXPERTS].set(pstart.astype(F32))
    dest = moe_slots(rt, pstart_rec, tb=min(512, n))
    xs = moe_dispatch(xf, dest[0], dest[1], n_pad)
    ys = moe_ffn(xs, blk_exp, new_exp, nblk, w1, w3, w2, layer=layer, bm=bm)
    g = moe_gather(ys, dest[0], dest[1])
    out = moe_combine(xf, g, rt, ln_g, ln_b, tb=min(512, n))
    return out.reshape(bsz, seq, d)


SC_ROWS = 64


def _sc_mesh():
    return plsc.VectorSubcoreMesh(core_axis_name="c", subcore_axis_name="s")


def moe_dispatch(xf, dest0, dest1, n_pad):
    n, d = xf.shape
    info = plsc.get_sparse_core_info()
    nw = info.num_cores * info.num_subcores
    per_w = n // nw
    r = min(SC_ROWS, per_w)

    def body(x_hbm, d0_hbm, d1_hbm, xs_hbm, i0_v, i1_v, rows_v, sem):
        wid = lax.axis_index("s") * info.num_cores + lax.axis_index("c")

        @pl.loop(0, per_w // r)
        def _(c):
            base = pl.multiple_of(wid * per_w + c * r, 8)
            pltpu.sync_copy(d0_hbm.at[pl.ds(base, r)], i0_v)
            pltpu.sync_copy(d1_hbm.at[pl.ds(base, r)], i1_v)
            pltpu.sync_copy(x_hbm.at[pl.ds(base, r)], rows_v)
            pltpu.async_copy(rows_v, xs_hbm.at[i0_v], sem).wait()
            pltpu.async_copy(rows_v, xs_hbm.at[i1_v], sem).wait()

    return pl.kernel(
        body, out_type=jax.ShapeDtypeStruct((n_pad, d), xf.dtype), mesh=_sc_mesh(),
        scratch_types=[pltpu.VMEM((r,), jnp.int32), pltpu.VMEM((r,), jnp.int32),
                       pltpu.VMEM((r, d), xf.dtype), pltpu.SemaphoreType.DMA],
        name="moe_dispatch",
    )(xf, dest0, dest1)


def moe_gather(ys, dest0, dest1):
    n = dest0.shape[0]
    d = ys.shape[1]
    info = plsc.get_sparse_core_info()
    nw = info.num_cores * info.num_subcores
    per_w = n // nw
    r = min(SC_ROWS, per_w)

    def body(ys_hbm, d0_hbm, d1_hbm, g_hbm, i_v, rows_v, sem):
        wid = lax.axis_index("s") * info.num_cores + lax.axis_index("c")

        @pl.loop(0, per_w // r)
        def _(c):
            base = pl.multiple_of(wid * per_w + c * r, 8)
            for k, d_hbm in enumerate((d0_hbm, d1_hbm)):
                pltpu.sync_copy(d_hbm.at[pl.ds(base, r)], i_v)
                pltpu.async_copy(ys_hbm.at[i_v], rows_v, sem).wait()
                pltpu.sync_copy(rows_v, g_hbm.at[k, pl.ds(base, r)])

    return pl.kernel(
        body, out_type=jax.ShapeDtypeStruct((2, n, d), ys.dtype), mesh=_sc_mesh(),
        scratch_types=[pltpu.VMEM((r,), jnp.int32), pltpu.VMEM((r, d), ys.dtype),
                       pltpu.SemaphoreType.DMA],
        name="moe_gather",
    )(ys, dest0, dest1)


def kernel(x, mem, ln_g, ln_b, ev_w_in, ev_gm_ln_g, ev_gm_ln_b, ev_gm_ws, ev_gm_bs, ev_conv_w, ev_conv_b, ev_wq, ev_wk, ev_wv, ev_w_if, ev_b_if, ev_norm_w, ev_skip, ev_w_out, od_w_qkv, od_b_qkv, od_sinks, od_w_o, xa_wq, xa_wkv, xa_wo, moe_w_rg, moe_b_rg, moe_w_re, moe_b_re, moe_w1, moe_w3, moe_w2):
    bsz, seq, d = x.shape
    m_len = mem.shape[1]
    depth = ln_g.shape[0]
    ts = min(MIXER_TS, seq)
    memf = mem.reshape(bsz * m_len, d)
    for l in range(depth):
        if l % 2 == 0:
            e = l // 2
            x = even_mixer_layer(x, ev_w_in[e], ev_gm_ln_g[e], ev_gm_ln_b[e], ev_gm_ws[e], ev_gm_bs[e],
                                 ev_conv_w[e], ev_conv_b[e], ev_wq[e], ev_wk[e], ev_wv[e], ev_w_if[e],
                                 ev_b_if[e], ev_norm_w[e], ev_skip[e], ev_w_out[e],
                                 ln_g[l, 0], ln_b[l, 0], ts=ts)
        else:
            o = l // 2
            x = swa_mixer_layer(x, od_w_qkv[o], od_b_qkv[o], od_sinks[o], od_w_o[o],
                                ln_g[l, 0], ln_b[l, 0], ts=ts)
        kv = matmul(memf, xa_wkv[l], tm=min(512, bsz * m_len), tn=512, out_dtype=BF16)
        kv = kv.reshape(bsz, m_len, 2 * d)
        x, route, counts = xattn_router_layer(x, kv, xa_wq[l], xa_wo[l], ln_g[l, 1], ln_b[l, 1],
                                              moe_w_rg[l], moe_b_rg[l], moe_w_re[l], moe_b_re[l],
                                              ts=min(XATTN_TS, seq))
        x = hierarchical_moe_layer(x, route, counts, moe_w1, moe_w3, moe_w2,
                                   ln_g[l, 2], ln_b[l, 2], layer=l)
    return x
```

```python
import functools
import math

import jax
import jax.numpy as jnp
from jax import lax
from jax.experimental import pallas as pl
from jax.experimental.pallas import tpu as pltpu
from jax.experimental.pallas import tpu_sc as plsc

F32 = jnp.float32
BF16 = jnp.bfloat16

A_GROUPS = 4
CHUNK = 128
B_HEADS = 4
B_CONV = 4
C_HEAD_DIM = 64
C_KV_HEADS = 4
X_HEADS = 4
N_GROUPS = 4
EXPERTS_PER_GROUP = 8
N_EXPERTS = N_GROUPS * EXPERTS_PER_GROUP
ROPE_THETA = 10000.0
LN_EPS = 1e-5
DEPTH = 2
DN_ALPHA = (2 * DEPTH) ** 0.25

LANES = 128
VMEM_LIMIT = 48 * 1024 * 1024
NEG = -1e30


def _cparams(sem):
    return pltpu.CompilerParams(dimension_semantics=sem, vmem_limit_bytes=VMEM_LIMIT)


def _full(shape):
    nd = len(shape)
    return pl.BlockSpec(shape, lambda *_: (0,) * nd)


def _dot(a, b):
    return jnp.dot(a, b, preferred_element_type=F32)


def _dot_nt(a, b):
    return lax.dot_general(a, b, (((1,), (1,)), ((), ())), preferred_element_type=F32)


def _dot_tn(a, b):
    return lax.dot_general(a, b, (((0,), (0,)), ((), ())), preferred_element_type=F32)


def _split_dot(a, b_bf16):
    hi = a.astype(BF16)
    lo = (a - hi.astype(F32)).astype(BF16)
    return _dot(hi, b_bf16) + _dot(lo, b_bf16)


def _ln(x, g, b):
    mu = jnp.mean(x, axis=-1, keepdims=True)
    xc = x - mu
    var = jnp.mean(xc * xc, axis=-1, keepdims=True)
    return xc * lax.rsqrt(var + LN_EPS) * g + b


def _silu(x):
    return x * (1.0 / (1.0 + jnp.exp(-x)))


def _gelu(x):
    return 0.5 * x * (1.0 + jnp.tanh(math.sqrt(2.0 / math.pi) * (x + 0.044715 * (x * x * x))))


def _pack_halves(x):
    c = x.shape[1] // 2
    lo = lax.bitcast_convert_type(x[:, :c].astype(BF16).astype(F32), jnp.uint32)
    hi = lax.bitcast_convert_type(x[:, c:].astype(BF16).astype(F32), jnp.uint32)
    return lax.bitcast_convert_type((lo >> 16) | hi, jnp.int32)


def _unpack_halves(p):
    u = lax.bitcast_convert_type(p, jnp.uint32)
    lo = lax.bitcast_convert_type(u << 16, F32)
    hi = lax.bitcast_convert_type(u & jnp.uint32(0xFFFF0000), F32)
    return lo, hi


def _log_sigmoid(x):
    return jnp.minimum(x, 0.0) - jnp.log(1.0 + jnp.exp(-jnp.abs(x)))


def _even_kernel(x_ref, w_in_ref, gm_g_ref, gm_b_ref, gm_w_ref, gm_bias_ref,
                 conv_w_ref, conv_b_ref, wq_ref, wk_ref, wv_ref, wif_t_ref, wif_ref, bif_t_ref,
                 bif_ref, norm_w_ref, skip_ref, w_out_ref, ln_g_ref, ln_b_ref,
                 o_ref,
                 xm_buf, ct_ref, n_ref, m_ref, *, ts, aw, bw):
    dh = bw // B_HEADS
    agd = aw // A_GROUPS
    nck = ts // CHUNK
    pad = 8
    j = pl.program_id(1)

    @pl.when(j == 0)
    def _():
        xm_buf[0:pad, :] = jnp.zeros((pad, bw), F32)
        ct_ref[...] = jnp.zeros_like(ct_ref)
        n_ref[...] = jnp.zeros_like(n_ref)
        m_ref[...] = jnp.zeros_like(m_ref)

    x = x_ref[0]
    proj = _dot(x.astype(BF16), w_in_ref[...])
    a_u = _gelu(proj[:, :aw])
    a_v = _gelu(proj[:, aw:2 * aw])
    xm = proj[:, 2 * aw:2 * aw + bw]
    z = proj[:, 2 * aw + bw:]

    vn = _ln(a_v, gm_g_ref[...], gm_b_ref[...]).astype(BF16)
    ya_chunks = []
    for c in range(nck):
        cols = []
        for g in range(A_GROUPS):
            v_cg = vn[c * CHUNK:(c + 1) * CHUNK, g * agd:(g + 1) * agd]
            cols.append(_dot(gm_w_ref[g], v_cg))
        ya_chunks.append(jnp.concatenate(cols, axis=1) + gm_bias_ref[...])
    y_a = a_u * jnp.concatenate(ya_chunks, axis=0)

    xm_buf[pad:pad + ts, :] = xm
    conv = conv_b_ref[...] + conv_w_ref[B_CONV - 1:B_CONV, :] * xm
    for k in range(B_CONV - 1):
        sh = B_CONV - 1 - k
        conv = conv + conv_w_ref[k:k + 1, :] * xm_buf[pad - sh:pad - sh + ts, :]
    xm_buf[pad - (B_CONV - 1):pad, :] = xm_buf[pad + ts - (B_CONV - 1):pad + ts, :]
    xc = _silu(conv)
    xc_b = xc.astype(BF16)
    xm_b = xm.astype(BF16)
    q = _dot(xc_b, wq_ref[...])
    k_ = _dot(xc_b, wk_ref[...])
    v = _dot(xm_b, wv_ref[...])
    gate_in = jnp.concatenate([q, k_, v], axis=1).astype(BF16)
    gates = _dot(gate_in, wif_ref[...]) + bif_ref[...]
    gates_t = _dot_nt(wif_t_ref[...], gate_in) + bif_t_ref[...]
    ig_c = gates[:, :B_HEADS]
    lf_c = _log_sigmoid(gates[:, B_HEADS:])
    ig_r = gates_t[:B_HEADS, :]
    lf_r = _log_sigmoid(gates_t[B_HEADS:, :])

    row = lax.broadcasted_iota(jnp.int32, (CHUNK, CHUNK), 0)
    col = lax.broadcasted_iota(jnp.int32, (CHUNK, CHUNK), 1)
    causal = col <= row
    tril = jnp.where(causal, 1.0, 0.0).astype(BF16)
    triu = jnp.where(row <= col, 1.0, 0.0).astype(BF16)
    q_b = q.astype(BF16)
    k_b = (k_ * dh ** -0.5).astype(BF16)
    v_b = v.astype(BF16)

    h_chunks = []
    for c in range(nck):
        sl = slice(c * CHUNK, (c + 1) * CHUNK)
        b_c = _split_dot_left(tril, lf_c[sl, :])
        b_r = _split_dot(lf_r[:, sl], triu)
        heads = []
        for h in range(B_HEADS):
            hs = slice(h * dh, (h + 1) * dh)
            qh, kh, vh = q_b[sl, hs], k_b[sl, hs], v_b[sl, hs]
            bcol = b_c[:, h:h + 1]
            brow = b_r[h:h + 1, :]
            igrow = ig_r[h:h + 1, sl]
            igcol = ig_c[sl, h:h + 1]
            m_prev = m_ref[h:h + 1, 0:1]
            dmat = jnp.where(causal, bcol - brow + igrow, NEG)
            inter = bcol + m_prev
            mt = jnp.maximum(jnp.max(dmat, axis=1, keepdims=True), inter)
            w_intra = jnp.exp(dmat - mt)
            w_state = jnp.exp(inter - mt)
            s = _dot_nt(qh, kh) * w_intra
            ct = ct_ref[h]
            n_row = n_ref[h:h + 1, :]
            num = _dot(s.astype(BF16), vh) + w_state * _dot(qh, ct.astype(BF16))
            qf = qh.astype(F32)
            nq = jnp.sum(s, axis=1, keepdims=True) + w_state * jnp.sum(qf * n_row, axis=1, keepdims=True)
            hv = num / jnp.maximum(jnp.abs(nq), jnp.exp(-mt))
            bl = brow[:, CHUNK - 1:CHUNK]
            g_row = bl - brow + igrow
            g_col = bl - bcol + igcol
            m_new = jnp.maximum(bl + m_prev, jnp.max(g_row, axis=1, keepdims=True))
            decay = jnp.exp(bl + m_prev - m_new)
            wg_col = jnp.exp(g_col - m_new)
            kw = kh.astype(F32) * wg_col
            ct_ref[h] = decay * ct + _dot_tn(kw.astype(BF16), vh)
            n_ref[h:h + 1, :] = decay * n_row + jnp.sum(kw, axis=0, keepdims=True)
            m_ref[h:h + 1, :] = jnp.broadcast_to(m_new, (1, LANES))
            hc = hv - jnp.mean(hv, axis=1, keepdims=True)
            hn = hc * lax.rsqrt(jnp.mean(hc * hc, axis=1, keepdims=True) + LN_EPS)
            heads.append(hn)
        h_chunks.append(jnp.concatenate(heads, axis=1))
    hn_all = jnp.concatenate(h_chunks, axis=0) if nck > 1 else h_chunks[0]
    y_b = (hn_all * norm_w_ref[...] + skip_ref[...] * xc) * _silu(z)

    mix = jnp.concatenate([y_a, y_b], axis=1).astype(BF16)
    y = _dot(mix, w_out_ref[...])
    o_ref[0] = _ln(DN_ALPHA * x + y, ln_g_ref[...], ln_b_ref[...])


def _split_dot_left(a_bf16, b):
    hi = b.astype(BF16)
    lo = (b - hi.astype(F32)).astype(BF16)
    return _dot(a_bf16, hi) + _dot(a_bf16, lo)


def _block_diag(w):
    hh, d, _ = w.shape
    eye = jnp.eye(hh, dtype=w.dtype)
    return jnp.einsum('hde,hg->hdge', w, eye).reshape(hh * d, hh * d)


def even_mixer_layer(x, w_in, gm_ln_g, gm_ln_b, gm_ws, gm_bs, conv_w, conv_b, wq, wk, wv, w_if,
                     b_if, norm_w, skip, w_out, ln_g, ln_b, *, ts):
    bsz, seq, d = x.shape
    aw = gm_ln_g.shape[0]
    bw = conv_b.shape[0]
    agd = aw // A_GROUPS
    causal = jnp.tril(jnp.ones((CHUNK, CHUNK), dtype=bool))
    gm_w = jnp.where(causal[None], gm_ws, 0.0).astype(BF16)
    gm_bias = jnp.repeat(gm_bs.T, agd, axis=1)
    row = lambda a: a.reshape(1, -1)
    args = (x, w_in.astype(BF16), row(gm_ln_g), row(gm_ln_b), gm_w, gm_bias,
            conv_w, row(conv_b), _block_diag(wq).astype(BF16), _block_diag(wk).astype(BF16),
            _block_diag(wv).astype(BF16), w_if.T.astype(BF16), w_if.astype(BF16),
            b_if.reshape(-1, 1), row(b_if), row(norm_w), row(skip), w_out.astype(BF16),
            row(ln_g), row(ln_b))
    in_specs = [pl.BlockSpec((1, ts, d), lambda b, j: (b, j, 0))] + [_full(a.shape) for a in args[1:]]
    dh = bw // B_HEADS
    return pl.pallas_call(
        functools.partial(_even_kernel, ts=ts, aw=aw, bw=bw),
        grid=(bsz, seq // ts),
        in_specs=in_specs,
        out_specs=pl.BlockSpec((1, ts, d), lambda b, j: (b, j, 0)),
        out_shape=jax.ShapeDtypeStruct((bsz, seq, d), F32),
        scratch_shapes=[pltpu.VMEM((8 + ts, bw), F32),
                        pltpu.VMEM((B_HEADS, dh, dh), F32),
                        pltpu.VMEM((B_HEADS, dh), F32),
                        pltpu.VMEM((B_HEADS, LANES), F32)],
        compiler_params=_cparams(("arbitrary", "arbitrary")),
        name="even_mixer",
    )(*args)


def _matmul_kernel(a_ref, b_ref, o_ref):
    o_ref[...] = _dot(a_ref[...].astype(BF16), b_ref[...].astype(BF16)).astype(o_ref.dtype)


def matmul(a, b, *, tm, tn, out_dtype):
    m, k = a.shape
    n = b.shape[1]
    return pl.pallas_call(
        _matmul_kernel,
        grid=(n // tn, m // tm),
        in_specs=[pl.BlockSpec((tm, k), lambda j, i: (i, 0)),
                  pl.BlockSpec((k, tn), lambda j, i: (0, j))],
        out_specs=pl.BlockSpec((tm, tn), lambda j, i: (i, j)),
        out_shape=jax.ShapeDtypeStruct((m, n), out_dtype),
        compiler_params=_cparams(("arbitrary", "arbitrary")),
        name="matmul",
    )(a, b)


ROUTE_W = 128
XATTN_TS = 1024
DEST_ROWS = 8


def _xattn_kernel(x_ref, kv_ref, wq_ref, wo_ref, ln_g_ref, ln_b_ref, wr_ref, br_ref,
                  o_ref, xp_ref, route_ref, cnt_ref, *, d):
    dh = d // X_HEADS
    first = jnp.logical_and(pl.program_id(0) == 0, pl.program_id(1) == 0)

    @pl.when(first)
    def _():
        cnt_ref[...] = jnp.zeros_like(cnt_ref)

    x = x_ref[0]
    ts = x.shape[0]
    q = (_dot(x.astype(BF16), wq_ref[...]) * dh ** -0.5).astype(BF16)
    outs = []
    for h in range(X_HEADS):
        kh = kv_ref[0, :, h * dh:(h + 1) * dh]
        vh = kv_ref[0, :, d + h * dh:d + (h + 1) * dh]
        s = _dot_nt(q[:, h * dh:(h + 1) * dh], kh)
        s = s - jnp.max(s, axis=1, keepdims=True)
        p = jnp.exp(s)
        l = jnp.sum(p, axis=1, keepdims=True)
        outs.append(_dot(p.astype(BF16), vh) / l)
    o = jnp.concatenate(outs, axis=1).astype(BF16)
    y = _dot(o, wo_ref[...])
    x2 = _ln(DN_ALPHA * x + y, ln_g_ref[...], ln_b_ref[...])
    o_ref[0] = x2
    xp_ref[0] = _pack_halves(x2)

    x_hi = x2.astype(BF16)
    x_lo = (x2 - x_hi.astype(F32)).astype(BF16)
    logits = (_dot(x_hi, wr_ref[0]) + _dot(x_lo, wr_ref[0]) + _dot(x_hi, wr_ref[1])) + br_ref[...]
    lane = lax.broadcasted_iota(jnp.int32, (ts, ROUTE_W), 1)
    is_g = lane < N_GROUPS
    lg = jnp.where(is_g, logits, NEG)
    mg = jnp.max(lg, axis=1, keepdims=True)
    gi = jnp.min(jnp.where(jnp.logical_and(is_g, lg == mg), lane, ROUTE_W), axis=1, keepdims=True)
    gate_g = 1.0 / jnp.sum(jnp.where(is_g, jnp.exp(lg - mg), 0.0), axis=1, keepdims=True)
    lo = N_GROUPS + gi * EXPERTS_PER_GROUP
    in_grp = jnp.logical_and(lane >= lo, lane < lo + EXPERTS_PER_GROUP)
    le = jnp.where(in_grp, logits, NEG)
    v1 = jnp.max(le, axis=1, keepdims=True)
    i1 = jnp.min(jnp.where(jnp.logical_and(in_grp, le == v1), lane, ROUTE_W), axis=1, keepdims=True)
    le2 = jnp.where(lane == i1, NEG, le)
    v2 = jnp.max(le2, axis=1, keepdims=True)
    i2 = jnp.min(jnp.where(jnp.logical_and(in_grp, le2 == v2), lane, ROUTE_W), axis=1, keepdims=True)
    e21 = jnp.exp(v2 - v1)
    p1 = 1.0 / (1.0 + e21)
    p2 = e21 * p1
    e1 = (i1 - N_GROUPS).astype(F32)
    e2 = (i2 - N_GROUPS).astype(F32)
    rec = jnp.where(lane == 0, e1, 0.0)
    rec = jnp.where(lane == 1, e2, rec)
    rec = jnp.where(lane == 2, gate_g * p1, rec)
    rec = jnp.where(lane == 3, gate_g * p2, rec)
    route_ref[0] = rec
    sel = jnp.logical_or(lane == i1 - N_GROUPS, lane == i2 - N_GROUPS)
    cnt_ref[...] += jnp.sum(jnp.where(sel, 1.0, 0.0), axis=0, keepdims=True)


def xattn_router_layer(x, kv, wq, wo, ln_g, ln_b, w_rg, b_rg, w_re, b_re, *, ts):
    bsz, seq, d = x.shape
    m_len = kv.shape[1]
    wr = jnp.zeros((d, ROUTE_W), F32).at[:, :N_GROUPS].set(w_rg).at[:, N_GROUPS:N_GROUPS + N_EXPERTS].set(w_re)
    br = jnp.zeros((1, ROUTE_W), F32).at[0, :N_GROUPS].set(b_rg).at[0, N_GROUPS:N_GROUPS + N_EXPERTS].set(b_re)
    wr_hi = wr.astype(BF16)
    wr_lo = (wr - wr_hi.astype(F32)).astype(BF16)
    wr = jnp.stack([wr_hi, wr_lo])
    args = (x, kv, wq.astype(BF16), wo.astype(BF16), ln_g.reshape(1, -1), ln_b.reshape(1, -1), wr, br)
    in_specs = [pl.BlockSpec((1, ts, d), lambda b, j: (b, j, 0)),
                pl.BlockSpec((1, m_len, 2 * d), lambda b, j: (b, 0, 0))] + [_full(a.shape) for a in args[2:]]
    return pl.pallas_call(
        functools.partial(_xattn_kernel, d=d),
        grid=(bsz, seq // ts),
        in_specs=in_specs,
        out_specs=[pl.BlockSpec((1, ts, d), lambda b, j: (b, j, 0)),
                   pl.BlockSpec((1, ts, d // 2), lambda b, j: (b, j, 0)),
                   pl.BlockSpec((1, ts, ROUTE_W), lambda b, j: (b, j, 0)),
                   pl.BlockSpec((1, ROUTE_W), lambda b, j: (0, 0))],
        out_shape=[jax.ShapeDtypeStruct((bsz, seq, d), F32),
                   jax.ShapeDtypeStruct((bsz, seq, d // 2), jnp.int32),
                   jax.ShapeDtypeStruct((bsz, seq, ROUTE_W), F32),
                   jax.ShapeDtypeStruct((1, ROUTE_W), F32)],
        compiler_params=_cparams(("arbitrary", "arbitrary")),
        name="xattn_router",
    )(*args)


def _swa_kernel(x_ref, wqkv_ref, bqkv_ref, cos_ref, sin_ref, sink_ref, wo_ref, ln_g_ref, ln_b_ref,
                o_ref, kprev, vprev, *, ts, cq, ckv):
    j = pl.program_id(1)
    nb = ts // CHUNK
    dh = C_HEAD_DIM
    grp = (cq // dh) // C_KV_HEADS

    @pl.when(j == 0)
    def _():
        kprev[...] = jnp.zeros_like(kprev)
        vprev[...] = jnp.zeros_like(vprev)

    x = x_ref[0]
    qkv = _dot(x.astype(BF16), wqkv_ref[...]) + bqkv_ref[...]
    cos = cos_ref[...]
    sin = sin_ref[...]
    lane = lax.broadcasted_iota(jnp.int32, (ts, LANES), 1)
    first_half = (lane % dh) < (dh // 2)

    def rope(t):
        outs = []
        for c in range(t.shape[1] // LANES):
            tc = t[:, c * LANES:(c + 1) * LANES]
            rot = jnp.where(first_half, pltpu.roll(tc, LANES - dh // 2, 1), pltpu.roll(tc, dh // 2, 1))
            outs.append(tc * cos + rot * sin)
        return jnp.concatenate(outs, axis=1)

    q = (rope(qkv[:, :cq]) * dh ** -0.5).astype(BF16)
    k = rope(qkv[:, cq:cq + ckv]).astype(BF16)
    v = qkv[:, cq + ckv:].astype(BF16)

    rows = grp * CHUNK
    r_i = lax.broadcasted_iota(jnp.int32, (rows, 2 * CHUNK), 0) % CHUNK
    c_i = lax.broadcasted_iota(jnp.int32, (rows, 2 * CHUNK), 1)
    band = jnp.logical_and(c_i > r_i, c_i <= r_i + CHUNK)
    r_g = lax.broadcasted_iota(jnp.int32, (rows, 1), 0) // CHUNK
    blocks = []
    for c in range(nb):
        sl = slice(c * CHUNK, (c + 1) * CHUNK)
        if c == 0:
            kb = jnp.concatenate([kprev[...].astype(BF16), k[sl]], axis=0)
            vb = jnp.concatenate([vprev[...].astype(BF16), v[sl]], axis=0)
            first_key = jnp.where(j > 0, 0, CHUNK)
            valid = jnp.logical_and(band, c_i >= first_key)
        else:
            kb = k[(c - 1) * CHUNK:(c + 1) * CHUNK]
            vb = v[(c - 1) * CHUNK:(c + 1) * CHUNK]
            valid = band
        heads = []
        for h in range(C_KV_HEADS):
            kh = kb[:, h * dh:(h + 1) * dh]
            vh = vb[:, h * dh:(h + 1) * dh]
            qs = jnp.concatenate([q[sl, (h * grp + g) * dh:(h * grp + g + 1) * dh] for g in range(grp)], axis=0)
            sink = jnp.zeros((rows, 1), F32)
            for g in range(grp):
                sink = jnp.where(r_g == g, sink_ref[h * grp + g], sink)
            s = jnp.where(valid, _dot_nt(qs, kh), NEG)
            m = jnp.maximum(jnp.max(s, axis=1, keepdims=True), sink)
            p = jnp.exp(s - m)
            den = jnp.sum(p, axis=1, keepdims=True) + jnp.exp(sink - m)
            o = _dot(p.astype(BF16), vh) / den
            heads.extend(o[g * CHUNK:(g + 1) * CHUNK] for g in range(grp))
        blocks.append(jnp.concatenate(heads, axis=1))
    kprev[...] = k[(nb - 1) * CHUNK:].astype(F32)
    vprev[...] = v[(nb - 1) * CHUNK:].astype(F32)
    att = (jnp.concatenate(blocks, axis=0) if nb > 1 else blocks[0]).astype(BF16)
    y = _dot(att, wo_ref[...])
    o_ref[0] = _ln(DN_ALPHA * x + y, ln_g_ref[...], ln_b_ref[...])


def swa_mixer_layer(x, w_qkv, b_qkv, sinks, w_o, ln_g, ln_b, *, ts):
    bsz, seq, d = x.shape
    cq = w_o.shape[0]
    ckv = (w_qkv.shape[1] - cq) // 2
    dh = C_HEAD_DIM
    inv = ROPE_THETA ** (-jnp.arange(0, dh, 2, dtype=F32) / dh)
    ang = jnp.arange(seq, dtype=F32)[:, None] * inv[None, :]
    reps = LANES // (dh // 2)
    sign = jnp.tile(jnp.concatenate([-jnp.ones((dh // 2,), F32), jnp.ones((dh // 2,), F32)]), LANES // dh)
    cos_t = jnp.tile(jnp.cos(ang), (1, reps))
    sin_t = jnp.tile(jnp.sin(ang), (1, reps)) * sign[None, :]
    args = (x, w_qkv.astype(BF16), b_qkv.reshape(1, -1), cos_t, sin_t, sinks.astype(F32),
            w_o.astype(BF16), ln_g.reshape(1, -1), ln_b.reshape(1, -1))
    in_specs = [pl.BlockSpec((1, ts, d), lambda b, j: (b, j, 0)),
                _full(args[1].shape), _full(args[2].shape),
                pl.BlockSpec((ts, LANES), lambda b, j: (j, 0)),
                pl.BlockSpec((ts, LANES), lambda b, j: (j, 0)),
                pl.BlockSpec(memory_space=pltpu.SMEM),
                _full(args[6].shape), _full(args[7].shape), _full(args[8].shape)]
    return pl.pallas_call(
        functools.partial(_swa_kernel, ts=ts, cq=cq, ckv=ckv),
        grid=(bsz, seq // ts),
        in_specs=in_specs,
        out_specs=pl.BlockSpec((1, ts, d), lambda b, j: (b, j, 0)),
        out_shape=jax.ShapeDtypeStruct((bsz, seq, d), F32),
        scratch_shapes=[pltpu.VMEM((CHUNK, ckv), F32), pltpu.VMEM((CHUNK, ckv), F32)],
        compiler_params=_cparams(("arbitrary", "arbitrary")),
        name="swa_mixer",
    )(*args)


def _slot_kernel(route_ref, pstart_ref, dest_ref, carry_ref, *, tb):
    @pl.when(pl.program_id(0) == 0)
    def _():
        carry_ref[...] = pstart_ref[...]

    rec = route_ref[...]
    lane = lax.broadcasted_iota(jnp.int32, (tb, ROUTE_W), 1)
    e0 = rec[:, 0:1].astype(jnp.int32)
    e1 = rec[:, 1:2].astype(jnp.int32)
    oh0 = lane == e0
    oh1 = lane == e1
    ohs = jnp.where(jnp.logical_or(oh0, oh1), 1.0, 0.0)
    r = lax.broadcasted_iota(jnp.int32, (tb, tb), 0)
    c = lax.broadcasted_iota(jnp.int32, (tb, tb), 1)
    before = jnp.where(c < r, 1.0, 0.0).astype(BF16)
    prefix = _dot(before, ohs.astype(BF16)) + carry_ref[...]
    d0 = jnp.sum(jnp.where(oh0, prefix, 0.0), axis=1, keepdims=True)
    d1 = jnp.sum(jnp.where(oh1, prefix, 0.0), axis=1, keepdims=True)
    dest = jnp.where(lane == 0, d0, jnp.where(lane == 1, d1, 0.0))
    dest_ref[...] = dest.T[:DEST_ROWS].astype(jnp.int32)
    carry_ref[...] += jnp.sum(ohs, axis=0, keepdims=True)


def moe_slots(route, pstart, *, tb):
    n = route.shape[0]
    return pl.pallas_call(
        functools.partial(_slot_kernel, tb=tb),
        grid=(n // tb,),
        in_specs=[pl.BlockSpec((tb, ROUTE_W), lambda i: (i, 0)), _full((1, ROUTE_W))],
        out_specs=pl.BlockSpec((DEST_ROWS, tb), lambda i: (0, i)),
        out_shape=jax.ShapeDtypeStruct((DEST_ROWS, n), jnp.int32),
        scratch_shapes=[pltpu.VMEM((1, ROUTE_W), F32)],
        compiler_params=_cparams(("arbitrary",)),
        name="moe_slots",
    )(route, pstart)


def _ffn_kernel(blk_exp_ref, new_exp_ref, nblk_ref, xs_ref, w1_ref, w3_ref, w2_ref, ys_ref,
                w1_b, w3_b, w2_b):
    i = pl.program_id(0)
    used = i < nblk_ref[0]

    @pl.when(jnp.logical_and(used, new_exp_ref[i] == 1))
    def _():
        w1_b[...] = w1_ref[0, 0].astype(BF16)
        w3_b[...] = w3_ref[0, 0].astype(BF16)
        w2_b[...] = w2_ref[0, 0].astype(BF16)

    @pl.when(used)
    def _():
        x_lo, x_hi = _unpack_halves(xs_ref[...])
        x_lo, x_hi = x_lo.astype(BF16), x_hi.astype(BF16)
        dl = x_lo.shape[1]
        h1 = _dot(x_lo, w1_b[:dl, :]) + _dot(x_hi, w1_b[dl:, :])
        h3 = _dot(x_lo, w3_b[:dl, :]) + _dot(x_hi, w3_b[dl:, :])
        h = (_silu(h1) * h3).astype(BF16)
        ys_ref[...] = _pack_halves(_dot(h, w2_b[...]))

    @pl.when(jnp.logical_not(used))
    def _():
        ys_ref[...] = jnp.zeros_like(ys_ref)


def moe_ffn(xs, blk_exp, new_exp, nblk, w1, w3, w2, *, layer, bm):
    n_pad, dp = xs.shape
    d = 2 * dp
    de = w1.shape[3]
    n_blk = n_pad // bm

    def x_map(i, be, ne, nb):
        return (jnp.minimum(i, nb[0] - 1), 0)

    def w_map(i, be, ne, nb):
        return (layer, be[i], 0, 0)

    return pl.pallas_call(
        _ffn_kernel,
        grid_spec=pltpu.PrefetchScalarGridSpec(
            num_scalar_prefetch=3,
            grid=(n_blk,),
            in_specs=[pl.BlockSpec((bm, dp), x_map),
                      pl.BlockSpec((1, 1, d, de), w_map),
                      pl.BlockSpec((1, 1, d, de), w_map),
                      pl.BlockSpec((1, 1, de, d), w_map)],
            out_specs=pl.BlockSpec((bm, dp), lambda i, be, ne, nb: (i, 0)),
            scratch_shapes=[pltpu.VMEM((d, de), BF16), pltpu.VMEM((d, de), BF16),
                            pltpu.VMEM((de, d), BF16)]),
        out_shape=jax.ShapeDtypeStruct((n_pad, dp), jnp.int32),
        compiler_params=_cparams(("arbitrary",)),
        name="moe_ffn",
    )(blk_exp, new_exp, nblk, xs, w1, w3, w2)


def _combine_kernel(x_ref, g0_ref, g1_ref, route_ref, ln_g_ref, ln_b_ref, o_ref):
    rec = route_ref[...]
    a_lo, a_hi = _unpack_halves(g0_ref[0])
    b_lo, b_hi = _unpack_halves(g1_ref[0])
    w0, w1 = rec[:, 2:3], rec[:, 3:4]
    y = jnp.concatenate([w0 * a_lo + w1 * b_lo, w0 * a_hi + w1 * b_hi], axis=1)
    o_ref[...] = _ln(DN_ALPHA * x_ref[...] + y, ln_g_ref[...], ln_b_ref[...])


def moe_combine(x, g, route, ln_g, ln_b, *, tb):
    n, d = x.shape
    row_spec = pl.BlockSpec((tb, d), lambda i: (i, 0))
    return pl.pallas_call(
        _combine_kernel,
        grid=(n // tb,),
        in_specs=[row_spec,
                  pl.BlockSpec((1, tb, d // 2), lambda i: (0, i, 0)),
                  pl.BlockSpec((1, tb, d // 2), lambda i: (1, i, 0)),
                  pl.BlockSpec((tb, ROUTE_W), lambda i: (i, 0)),
                  _full((1, d)), _full((1, d))],
        out_specs=row_spec,
        out_shape=jax.ShapeDtypeStruct((n, d), F32),
        compiler_params=_cparams(("arbitrary",)),
        name="moe_combine",
    )(x, g, g, route, ln_g.reshape(1, -1), ln_b.reshape(1, -1))


MOE_BM = 512
MIXER_TS = 512


def hierarchical_moe_layer(x2, xp, route, counts, w1, w3, w2, ln_g, ln_b, *, layer):
    bsz, seq, d = x2.shape
    n = bsz * seq
    bm = MOE_BM
    xf = x2.reshape(n, d)
    rt = route.reshape(n, ROUTE_W)
    n_blk = (2 * n) // bm + N_EXPERTS
    n_pad = n_blk * bm
    cnt = counts[0, :N_EXPERTS].astype(jnp.int32)
    pcnt = (cnt + bm - 1) // bm * bm
    pends = jnp.cumsum(pcnt)
    pstart = pends - pcnt
    nblk = (pends[-1] // bm).astype(jnp.int32).reshape(1)
    blk_row = jnp.arange(n_blk, dtype=jnp.int32) * bm
    blk_exp = jnp.minimum(jnp.sum((pends[None, :] <= blk_row[:, None]).astype(jnp.int32), axis=1),
                          N_EXPERTS - 1)
    last_exp = blk_exp[jnp.maximum(nblk[0] - 1, 0)]
    blk_exp = jnp.where(jnp.arange(n_blk) < nblk[0], blk_exp, last_exp)
    new_exp = jnp.concatenate([jnp.ones((1,), jnp.int32),
                               (blk_exp[1:] != blk_exp[:-1]).astype(jnp.int32)])
    pstart_rec = jnp.zeros((1, ROUTE_W), F32).at[0, :N_EXPERTS].set(pstart.astype(F32))
    dest = moe_slots(rt, pstart_rec, tb=min(512, n))
    xs = moe_dispatch(xp.reshape(n, d // 2), dest[0], dest[1], n_pad)
    ys = moe_ffn(xs, blk_exp, new_exp, nblk, w1, w3, w2, layer=layer, bm=bm)
    g = moe_gather(ys, dest[0], dest[1])
    out = moe_combine(xf, g, rt, ln_g, ln_b, tb=min(512, n))
    return out.reshape(bsz, seq, d)


SC_ROWS = 128


def _sc_mesh():
    return plsc.VectorSubcoreMesh(core_axis_name="c", subcore_axis_name="s")


def moe_dispatch(xf, dest0, dest1, n_pad):
    n, d = xf.shape
    info = plsc.get_sparse_core_info()
    nw = info.num_cores * info.num_subcores
    per_w = n // nw
    r = min(SC_ROWS, per_w)

    def body(x_hbm, d0_hbm, d1_hbm, xs_hbm, i0_v, i1_v, rows_v, sem):
        wid = lax.axis_index("s") * info.num_cores + lax.axis_index("c")

        @pl.loop(0, per_w // r)
        def _(c):
            base = pl.multiple_of(wid * per_w + c * r, 8)
            pltpu.sync_copy(d0_hbm.at[pl.ds(base, r)], i0_v)
            pltpu.sync_copy(d1_hbm.at[pl.ds(base, r)], i1_v)
            pltpu.sync_copy(x_hbm.at[pl.ds(base, r)], rows_v)
            pltpu.async_copy(rows_v, xs_hbm.at[i0_v], sem).wait()
            pltpu.async_copy(rows_v, xs_hbm.at[i1_v], sem).wait()

    return pl.kernel(
        body, out_type=jax.ShapeDtypeStruct((n_pad, d), xf.dtype), mesh=_sc_mesh(),
        scratch_types=[pltpu.VMEM((r,), jnp.int32), pltpu.VMEM((r,), jnp.int32),
                       pltpu.VMEM((r, d), xf.dtype), pltpu.SemaphoreType.DMA],
        name="moe_dispatch",
    )(xf, dest0, dest1)


def moe_gather(ys, dest0, dest1):
    n = dest0.shape[0]
    d = ys.shape[1]
    info = plsc.get_sparse_core_info()
    nw = info.num_cores * info.num_subcores
    per_w = n // nw
    r = min(SC_ROWS, per_w)

    def body(ys_hbm, d0_hbm, d1_hbm, g_hbm, i_v, rows_v, sem):
        wid = lax.axis_index("s") * info.num_cores + lax.axis_index("c")

        @pl.loop(0, per_w // r)
        def _(c):
            base = pl.multiple_of(wid * per_w + c * r, 8)
            for k, d_hbm in enumerate((d0_hbm, d1_hbm)):
                pltpu.sync_copy(d_hbm.at[pl.ds(base, r)], i_v)
                pltpu.async_copy(ys_hbm.at[i_v], rows_v, sem).wait()
                pltpu.sync_copy(rows_v, g_hbm.at[k, pl.ds(base, r)])

    return pl.kernel(
        body, out_type=jax.ShapeDtypeStruct((2, n, d), ys.dtype), mesh=_sc_mesh(),
        scratch_types=[pltpu.VMEM((r,), jnp.int32), pltpu.VMEM((r, d), ys.dtype),
                       pltpu.SemaphoreType.DMA],
        name="moe_gather",
    )(ys, dest0, dest1)


def kernel(x, mem, ln_g, ln_b, ev_w_in, ev_gm_ln_g, ev_gm_ln_b, ev_gm_ws, ev_gm_bs, ev_conv_w, ev_conv_b, ev_wq, ev_wk, ev_wv, ev_w_if, ev_b_if, ev_norm_w, ev_skip, ev_w_out, od_w_qkv, od_b_qkv, od_sinks, od_w_o, xa_wq, xa_wkv, xa_wo, moe_w_rg, moe_b_rg, moe_w_re, moe_b_re, moe_w1, moe_w3, moe_w2):
    bsz, seq, d = x.shape
    m_len = mem.shape[1]
    depth = ln_g.shape[0]
    ts = min(MIXER_TS, seq)
    memf = mem.reshape(bsz * m_len, d)
    for l in range(depth):
        if l % 2 == 0:
            e = l // 2
            x = even_mixer_layer(x, ev_w_in[e], ev_gm_ln_g[e], ev_gm_ln_b[e], ev_gm_ws[e], ev_gm_bs[e],
                                 ev_conv_w[e], ev_conv_b[e], ev_wq[e], ev_wk[e], ev_wv[e], ev_w_if[e],
                                 ev_b_if[e], ev_norm_w[e], ev_skip[e], ev_w_out[e],
                                 ln_g[l, 0], ln_b[l, 0], ts=ts)
        else:
            o = l // 2
            x = swa_mixer_layer(x, od_w_qkv[o], od_b_qkv[o], od_sinks[o], od_w_o[o],
                                ln_g[l, 0], ln_b[l, 0], ts=ts)
        kv = matmul(memf, xa_wkv[l], tm=min(512, bsz * m_len), tn=512, out_dtype=BF16)
        kv = kv.reshape(bsz, m_len, 2 * d)
        x, xp, route, counts = xattn_router_layer(x, kv, xa_wq[l], xa_wo[l], ln_g[l, 1], ln_b[l, 1],
                                                  moe_w_rg[l], moe_b_rg[l], moe_w_re[l], moe_b_re[l],
                                                  ts=min(XATTN_TS, seq))
        x = hierarchical_moe_layer(x, xp, route, counts, moe_w1, moe_w3, moe_w2,
                                   ln_g[l, 2], ln_b[l, 2], layer=l)
    return x
```

```python
import functools
import math

import jax
import jax.numpy as jnp
from jax import lax
from jax.experimental import pallas as pl
from jax.experimental.pallas import tpu as pltpu
from jax.experimental.pallas import tpu_sc as plsc

F32 = jnp.float32
BF16 = jnp.bfloat16

A_GROUPS = 4
CHUNK = 128
B_HEADS = 4
B_CONV = 4
C_HEAD_DIM = 64
C_KV_HEADS = 4
X_HEADS = 4
N_GROUPS = 4
EXPERTS_PER_GROUP = 8
N_EXPERTS = N_GROUPS * EXPERTS_PER_GROUP
ROPE_THETA = 10000.0
LN_EPS = 1e-5
DEPTH = 2
DN_ALPHA = (2 * DEPTH) ** 0.25

LANES = 128
VMEM_LIMIT = 48 * 1024 * 1024
NEG = -1e30


def _cparams(sem):
    return pltpu.CompilerParams(dimension_semantics=sem, vmem_limit_bytes=VMEM_LIMIT)


def _full(shape):
    nd = len(shape)
    return pl.BlockSpec(shape, lambda *_: (0,) * nd)


def _dot(a, b):
    return jnp.dot(a, b, preferred_element_type=F32)


def _dot_nt(a, b):
    return lax.dot_general(a, b, (((1,), (1,)), ((), ())), preferred_element_type=F32)


def _dot_tn(a, b):
    return lax.dot_general(a, b, (((0,), (0,)), ((), ())), preferred_element_type=F32)


def _split_dot(a, b_bf16):
    hi = a.astype(BF16)
    lo = (a - hi.astype(F32)).astype(BF16)
    return _dot(hi, b_bf16) + _dot(lo, b_bf16)


def _ln(x, g, b):
    mu = jnp.mean(x, axis=-1, keepdims=True)
    xc = x - mu
    var = jnp.mean(xc * xc, axis=-1, keepdims=True)
    return xc * lax.rsqrt(var + LN_EPS) * g + b


def _silu(x):
    return x * (1.0 / (1.0 + jnp.exp(-x)))


def _gelu(x):
    return 0.5 * x * (1.0 + jnp.tanh(math.sqrt(2.0 / math.pi) * (x + 0.044715 * (x * x * x))))


def _pack_halves(x):
    c = x.shape[1] // 2
    lo = lax.bitcast_convert_type(x[:, :c].astype(BF16).astype(F32), jnp.uint32)
    hi = lax.bitcast_convert_type(x[:, c:].astype(BF16).astype(F32), jnp.uint32)
    return lax.bitcast_convert_type((lo >> 16) | hi, jnp.int32)


def _unpack_halves(p):
    u = lax.bitcast_convert_type(p, jnp.uint32)
    lo = lax.bitcast_convert_type(u << 16, F32)
    hi = lax.bitcast_convert_type(u & jnp.uint32(0xFFFF0000), F32)
    return lo, hi


def _log_sigmoid(x):
    return jnp.minimum(x, 0.0) - jnp.log(1.0 + jnp.exp(-jnp.abs(x)))


def _even_kernel(x_ref, w_in_ref, gm_g_ref, gm_b_ref, gm_w_ref, gm_bias_ref,
                 conv_w_ref, conv_b_ref, wq_ref, wk_ref, wv_ref, wif_t_ref, bif_t_ref,
                 norm_w_ref, skip_ref, w_out_ref, ln_g_ref, ln_b_ref,
                 o_ref,
                 xm_buf, ct_ref, m_ref, *, ts, aw, bw):
    dh = bw // B_HEADS
    agd = aw // A_GROUPS
    nck = ts // CHUNK
    pad = 8
    j = pl.program_id(1)

    @pl.when(j == 0)
    def _():
        xm_buf[0:pad, :] = jnp.zeros((pad, bw), F32)
        ct_ref[...] = jnp.zeros_like(ct_ref)
        m_ref[...] = jnp.zeros_like(m_ref)

    row = lax.broadcasted_iota(jnp.int32, (CHUNK, CHUNK), 0)
    col = lax.broadcasted_iota(jnp.int32, (CHUNK, CHUNK), 1)
    causal = col <= row
    diag = col == row
    triu = jnp.where(row <= col, 1.0, 0.0).astype(BF16)
    ones_blk = jnp.ones((CHUNK, LANES), BF16)

    x = x_ref[0]
    proj = _dot(x.astype(BF16), w_in_ref[...])
    a_u = _gelu(proj[:, :aw])
    a_v = _gelu(proj[:, aw:2 * aw])
    xm = proj[:, 2 * aw:2 * aw + bw]
    z = proj[:, 2 * aw + bw:]

    vn = _ln(a_v, gm_g_ref[...], gm_b_ref[...]).astype(BF16)
    ya_chunks = []
    for c in range(nck):
        cols = []
        for g in range(A_GROUPS):
            v_cg = vn[c * CHUNK:(c + 1) * CHUNK, g * agd:(g + 1) * agd]
            cols.append(_dot(gm_w_ref[g], v_cg))
        ya_chunks.append(jnp.concatenate(cols, axis=1) + gm_bias_ref[...])
    y_a = a_u * jnp.concatenate(ya_chunks, axis=0)

    xm_buf[pad:pad + ts, :] = xm
    conv = conv_b_ref[...] + conv_w_ref[B_CONV - 1:B_CONV, :] * xm
    for k in range(B_CONV - 1):
        sh = B_CONV - 1 - k
        conv = conv + conv_w_ref[k:k + 1, :] * xm_buf[pad - sh:pad - sh + ts, :]
    xm_buf[pad - (B_CONV - 1):pad, :] = xm_buf[pad + ts - (B_CONV - 1):pad + ts, :]
    xc = _silu(conv)
    xc_b = xc.astype(BF16)
    q = _dot(xc_b, wq_ref[...])
    k_ = _dot(xc_b, wk_ref[...])
    v = _dot(xm.astype(BF16), wv_ref[...])
    gate_in = jnp.concatenate([q, k_, v], axis=1).astype(BF16)
    gates_t = _dot_nt(wif_t_ref[...], gate_in) + bif_t_ref[...]
    ig_all = gates_t[:B_HEADS, :]
    lf_all = _log_sigmoid(gates_t[B_HEADS:, :])
    q_b = q.astype(BF16)
    k_b = (k_ * dh ** -0.5).astype(BF16)
    v_b = v.astype(BF16)
    gate_z = _silu(z)

    h_chunks = []
    for c in range(nck):
        sl = slice(c * CHUNK, (c + 1) * CHUNK)
        lf_r = lf_all[:, sl]
        b_r = _split_dot(lf_r, triu)
        a_r = ig_all[:, sl] - b_r
        heads = []
        for h in range(B_HEADS):
            hs = slice(h * dh, (h + 1) * dh)
            qh, kh = q_b[sl, hs], k_b[sl, hs]
            vaug = jnp.concatenate([v_b[sl, hs], ones_blk], axis=1)
            a_row = a_r[h:h + 1, :]
            m_row = m_ref[h:h + 1, :]
            amat = jnp.where(causal, a_row, NEG)
            mx = jnp.maximum(jnp.max(amat, axis=1, keepdims=True), m_row)
            w_intra = jnp.exp(amat - mx)
            w_state = jnp.exp(m_row - mx)
            s = _dot_nt(qh, kh) * w_intra
            caug = ct_ref[h]
            naug = (_dot(s.astype(BF16), vaug)
                    + jnp.concatenate([w_state, w_state], axis=1) * _dot(qh, caug.astype(BF16)))
            num, nq = naug[:, :dh], naug[:, dh:]
            b_t = _split_dot(jnp.where(causal, lf_r[h:h + 1, :], 0.0), ones_blk)
            hv = num / jnp.maximum(jnp.abs(nq), jnp.exp(-(b_t + mx)))
            bl = b_r[h:h + 1, CHUNK - 1:CHUNK]
            g_row = bl + a_row
            m_new = jnp.maximum(bl + m_row, jnp.max(g_row, axis=1, keepdims=True))
            decay = jnp.exp(bl + m_row - m_new)
            wg_row = jnp.exp(g_row - m_new)
            wg_t = _split_dot(jnp.where(diag, wg_row, 0.0), ones_blk)
            kw = (kh.astype(F32) * wg_t).astype(BF16)
            ct_ref[h] = jnp.concatenate([decay, decay], axis=1) * caug + _dot_tn(kw, vaug)
            m_ref[h:h + 1, :] = m_new
            hc = hv - jnp.mean(hv, axis=1, keepdims=True)
            hn = hc * lax.rsqrt(jnp.mean(hc * hc, axis=1, keepdims=True) + LN_EPS)
            heads.append(hn)
        h_chunks.append(jnp.concatenate(heads, axis=1))
    hn_all = jnp.concatenate(h_chunks, axis=0) if nck > 1 else h_chunks[0]
    y_b = (hn_all * norm_w_ref[...] + skip_ref[...] * xc) * gate_z

    mix = jnp.concatenate([y_a, y_b], axis=1).astype(BF16)
    y = _dot(mix, w_out_ref[...])
    o_ref[0] = _ln(DN_ALPHA * x + y, ln_g_ref[...], ln_b_ref[...])


def _split_dot_left(a_bf16, b):
    hi = b.astype(BF16)
    lo = (b - hi.astype(F32)).astype(BF16)
    return _dot(a_bf16, hi) + _dot(a_bf16, lo)


def _block_diag(w):
    hh, d, _ = w.shape
    eye = jnp.eye(hh, dtype=w.dtype)
    return jnp.einsum('hde,hg->hdge', w, eye).reshape(hh * d, hh * d)


def even_mixer_layer(x, w_in, gm_ln_g, gm_ln_b, gm_ws, gm_bs, conv_w, conv_b, wq, wk, wv, w_if,
                     b_if, norm_w, skip, w_out, ln_g, ln_b, *, ts):
    bsz, seq, d = x.shape
    aw = gm_ln_g.shape[0]
    bw = conv_b.shape[0]
    agd = aw // A_GROUPS
    causal = jnp.tril(jnp.ones((CHUNK, CHUNK), dtype=bool))
    gm_w = jnp.where(causal[None], gm_ws, 0.0).astype(BF16)
    gm_bias = jnp.repeat(gm_bs.T, agd, axis=1)
    row = lambda a: a.reshape(1, -1)
    args = (x, w_in.astype(BF16), row(gm_ln_g), row(gm_ln_b), gm_w, gm_bias,
            conv_w, row(conv_b), _block_diag(wq).astype(BF16), _block_diag(wk).astype(BF16),
            _block_diag(wv).astype(BF16), w_if.T.astype(BF16),
            b_if.reshape(-1, 1), row(norm_w), row(skip), w_out.astype(BF16),
            row(ln_g), row(ln_b))
    in_specs = [pl.BlockSpec((1, ts, d), lambda b, j: (b, j, 0))] + [_full(a.shape) for a in args[1:]]
    dh = bw // B_HEADS
    assert dh == LANES and CHUNK == LANES
    return pl.pallas_call(
        functools.partial(_even_kernel, ts=ts, aw=aw, bw=bw),
        grid=(bsz, seq // ts),
        in_specs=in_specs,
        out_specs=pl.BlockSpec((1, ts, d), lambda b, j: (b, j, 0)),
        out_shape=jax.ShapeDtypeStruct((bsz, seq, d), F32),
        scratch_shapes=[pltpu.VMEM((8 + ts, bw), F32),
                        pltpu.VMEM((B_HEADS, dh, dh + LANES), F32),
                        pltpu.VMEM((B_HEADS, LANES), F32)],
        compiler_params=_cparams(("arbitrary", "arbitrary")),
        name="even_mixer",
    )(*args)


def _matmul_kernel(a_ref, b_ref, o_ref):
    o_ref[...] = _dot(a_ref[...].astype(BF16), b_ref[...].astype(BF16)).astype(o_ref.dtype)


def matmul(a, b, *, tm, tn, out_dtype):
    m, k = a.shape
    n = b.shape[1]
    return pl.pallas_call(
        _matmul_kernel,
        grid=(n // tn, m // tm),
        in_specs=[pl.BlockSpec((tm, k), lambda j, i: (i, 0)),
                  pl.BlockSpec((k, tn), lambda j, i: (0, j))],
        out_specs=pl.BlockSpec((tm, tn), lambda j, i: (i, j)),
        out_shape=jax.ShapeDtypeStruct((m, n), out_dtype),
        compiler_params=_cparams(("arbitrary", "arbitrary")),
        name="matmul",
    )(a, b)


ROUTE_W = 128
XATTN_TS = 1024
DEST_ROWS = 8


def _xattn_kernel(x_ref, kv_ref, wq_ref, wo_ref, ln_g_ref, ln_b_ref, wr_ref, br_ref,
                  o_ref, xp_ref, route_ref, cnt_ref, *, d):
    dh = d // X_HEADS
    first = jnp.logical_and(pl.program_id(0) == 0, pl.program_id(1) == 0)

    @pl.when(first)
    def _():
        cnt_ref[...] = jnp.zeros_like(cnt_ref)

    x = x_ref[0]
    ts = x.shape[0]
    q = (_dot(x.astype(BF16), wq_ref[...]) * dh ** -0.5).astype(BF16)
    outs = []
    for h in range(X_HEADS):
        kh = kv_ref[0, :, h * dh:(h + 1) * dh]
        vh = kv_ref[0, :, d + h * dh:d + (h + 1) * dh]
        s = _dot_nt(q[:, h * dh:(h + 1) * dh], kh)
        s = s - jnp.max(s, axis=1, keepdims=True)
        p = jnp.exp(s)
        l = jnp.sum(p, axis=1, keepdims=True)
        outs.append(_dot(p.astype(BF16), vh) / l)
    o = jnp.concatenate(outs, axis=1).astype(BF16)
    y = _dot(o, wo_ref[...])
    x2 = _ln(DN_ALPHA * x + y, ln_g_ref[...], ln_b_ref[...])
    o_ref[0] = x2
    xp_ref[0] = _pack_halves(x2)

    x_hi = x2.astype(BF16)
    x_lo = (x2 - x_hi.astype(F32)).astype(BF16)
    logits = (_dot(x_hi, wr_ref[0]) + _dot(x_lo, wr_ref[0]) + _dot(x_hi, wr_ref[1])) + br_ref[...]
    lane = lax.broadcasted_iota(jnp.int32, (ts, ROUTE_W), 1)
    is_g = lane < N_GROUPS
    lg = jnp.where(is_g, logits, NEG)
    mg = jnp.max(lg, axis=1, keepdims=True)
    gi = jnp.min(jnp.where(jnp.logical_and(is_g, lg == mg), lane, ROUTE_W), axis=1, keepdims=True)
    gate_g = 1.0 / jnp.sum(jnp.where(is_g, jnp.exp(lg - mg), 0.0), axis=1, keepdims=True)
    lo = N_GROUPS + gi * EXPERTS_PER_GROUP
    in_grp = jnp.logical_and(lane >= lo, lane < lo + EXPERTS_PER_GROUP)
    le = jnp.where(in_grp, logits, NEG)
    v1 = jnp.max(le, axis=1, keepdims=True)
    i1 = jnp.min(jnp.where(jnp.logical_and(in_grp, le == v1), lane, ROUTE_W), axis=1, keepdims=True)
    le2 = jnp.where(lane == i1, NEG, le)
    v2 = jnp.max(le2, axis=1, keepdims=True)
    i2 = jnp.min(jnp.where(jnp.logical_and(in_grp, le2 == v2), lane, ROUTE_W), axis=1, keepdims=True)
    e21 = jnp.exp(v2 - v1)
    p1 = 1.0 / (1.0 + e21)
    p2 = e21 * p1
    e1 = (i1 - N_GROUPS).astype(F32)
    e2 = (i2 - N_GROUPS).astype(F32)
    rec = jnp.where(lane == 0, e1, 0.0)
    rec = jnp.where(lane == 1, e2, rec)
    rec = jnp.where(lane == 2, gate_g * p1, rec)
    rec = jnp.where(lane == 3, gate_g * p2, rec)
    route_ref[0] = rec
    sel = jnp.logical_or(lane == i1 - N_GROUPS, lane == i2 - N_GROUPS)
    cnt_ref[...] += jnp.sum(jnp.where(sel, 1.0, 0.0), axis=0, keepdims=True)


def xattn_router_layer(x, kv, wq, wo, ln_g, ln_b, w_rg, b_rg, w_re, b_re, *, ts):
    bsz, seq, d = x.shape
    m_len = kv.shape[1]
    wr = jnp.zeros((d, ROUTE_W), F32).at[:, :N_GROUPS].set(w_rg).at[:, N_GROUPS:N_GROUPS + N_EXPERTS].set(w_re)
    br = jnp.zeros((1, ROUTE_W), F32).at[0, :N_GROUPS].set(b_rg).at[0, N_GROUPS:N_GROUPS + N_EXPERTS].set(b_re)
    wr_hi = wr.astype(BF16)
    wr_lo = (wr - wr_hi.astype(F32)).astype(BF16)
    wr = jnp.stack([wr_hi, wr_lo])
    args = (x, kv, wq.astype(BF16), wo.astype(BF16), ln_g.reshape(1, -1), ln_b.reshape(1, -1), wr, br)
    in_specs = [pl.BlockSpec((1, ts, d), lambda b, j: (b, j, 0)),
                pl.BlockSpec((1, m_len, 2 * d), lambda b, j: (b, 0, 0))] + [_full(a.shape) for a in args[2:]]
    return pl.pallas_call(
        functools.partial(_xattn_kernel, d=d),
        grid=(bsz, seq // ts),
        in_specs=in_specs,
        out_specs=[pl.BlockSpec((1, ts, d), lambda b, j: (b, j, 0)),
                   pl.BlockSpec((1, ts, d // 2), lambda b, j: (b, j, 0)),
                   pl.BlockSpec((1, ts, ROUTE_W), lambda b, j: (b, j, 0)),
                   pl.BlockSpec((1, ROUTE_W), lambda b, j: (0, 0))],
        out_shape=[jax.ShapeDtypeStruct((bsz, seq, d), F32),
                   jax.ShapeDtypeStruct((bsz, seq, d // 2), jnp.int32),
                   jax.ShapeDtypeStruct((bsz, seq, ROUTE_W), F32),
                   jax.ShapeDtypeStruct((1, ROUTE_W), F32)],
        compiler_params=_cparams(("arbitrary", "arbitrary")),
        name="xattn_router",
    )(*args)


def _swa_kernel(x_ref, wqkv_ref, bqkv_ref, cos_ref, sin_ref, sink_ref, wo_ref, ln_g_ref, ln_b_ref,
                o_ref, kprev, vprev, *, ts, cq, ckv):
    j = pl.program_id(1)
    nb = ts // CHUNK
    dh = C_HEAD_DIM
    grp = (cq // dh) // C_KV_HEADS

    @pl.when(j == 0)
    def _():
        kprev[...] = jnp.zeros_like(kprev)
        vprev[...] = jnp.zeros_like(vprev)

    x = x_ref[0]
    qkv = _dot(x.astype(BF16), wqkv_ref[...]) + bqkv_ref[...]
    cos = cos_ref[...]
    sin = sin_ref[...]
    lane = lax.broadcasted_iota(jnp.int32, (ts, LANES), 1)
    first_half = (lane % dh) < (dh // 2)

    def rope(t):
        outs = []
        for c in range(t.shape[1] // LANES):
            tc = t[:, c * LANES:(c + 1) * LANES]
            rot = jnp.where(first_half, pltpu.roll(tc, LANES - dh // 2, 1), pltpu.roll(tc, dh // 2, 1))
            outs.append(tc * cos + rot * sin)
        return jnp.concatenate(outs, axis=1)

    q = (rope(qkv[:, :cq]) * dh ** -0.5).astype(BF16)
    k = rope(qkv[:, cq:cq + 2 * ckv]).astype(BF16)
    v = qkv[:, cq + 2 * ckv:].astype(BF16)

    r_i = lax.broadcasted_iota(jnp.int32, (CHUNK, 2 * CHUNK), 0)
    c_i = lax.broadcasted_iota(jnp.int32, (CHUNK, 2 * CHUNK), 1)
    band = jnp.logical_and(c_i > r_i, c_i <= r_i + CHUNK)
    sink_col = c_i == 0
    lane_k = lax.broadcasted_iota(jnp.int32, (2 * CHUNK, LANES), 1)
    key_row = lax.broadcasted_iota(jnp.int32, (2 * CHUNK, LANES), 0)
    lane_q = lax.broadcasted_iota(jnp.int32, (CHUNK, LANES), 1)
    ones_blk = jnp.ones((2 * CHUNK, LANES), BF16)
    zero_b = jnp.zeros((), BF16)
    blocks = []
    for c in range(nb):
        sl = slice(c * CHUNK, (c + 1) * CHUNK)
        if c == 0:
            kb = jnp.concatenate([kprev[...].astype(BF16), k[sl]], axis=0)
            vb = jnp.concatenate([vprev[...].astype(BF16), v[sl]], axis=0)
            first_key = jnp.where(j > 0, 0, CHUNK)
            valid = jnp.logical_and(band, c_i >= first_key)
        else:
            kb = k[(c - 1) * CHUNK:(c + 1) * CHUNK]
            vb = v[(c - 1) * CHUNK:(c + 1) * CHUNK]
            valid = band
        tiles = []
        for h in range(C_KV_HEADS):
            kd = kb[:, h * LANES:(h + 1) * LANES]
            vd = vb[:, h * LANES:(h + 1) * LANES]
            k_lo = jnp.where(lane_k < dh, kd, zero_b)
            k_hi = jnp.where(lane_k >= dh, kd, zero_b)
            vz = jnp.where(key_row == 0, zero_b, vd)
            q2 = jnp.concatenate([q[sl, (2 * h) * LANES:(2 * h + 1) * LANES],
                                  q[sl, (2 * h + 1) * LANES:(2 * h + 2) * LANES]], axis=0)
            s_lo = _dot_nt(q2, k_lo)
            s_hi = _dot_nt(q2, k_hi)
            parts = []
            for qk, g in ((s_lo[:CHUNK], 0), (s_lo[CHUNK:], 2), (s_hi[:CHUNK], 1), (s_hi[CHUNK:], 3)):
                sink = sink_ref[h * grp + g]
                parts.append(jnp.where(valid, qk, jnp.where(sink_col, sink, NEG)))
            s = jnp.concatenate(parts, axis=0)
            p = jnp.exp(s - jnp.max(s, axis=1, keepdims=True)).astype(BF16)
            den = _dot(p, ones_blk)
            o2 = _dot(p, vz) / den
            tiles.append(jnp.where(lane_q < dh, o2[:CHUNK], o2[2 * CHUNK:3 * CHUNK]))
            tiles.append(jnp.where(lane_q < dh, o2[CHUNK:2 * CHUNK], o2[3 * CHUNK:]))
        blocks.append(jnp.concatenate(tiles, axis=1))
    kprev[...] = k[(nb - 1) * CHUNK:].astype(F32)
    vprev[...] = v[(nb - 1) * CHUNK:].astype(F32)
    att = (jnp.concatenate(blocks, axis=0) if nb > 1 else blocks[0]).astype(BF16)
    y = _dot(att, wo_ref[...])
    o_ref[0] = _ln(DN_ALPHA * x + y, ln_g_ref[...], ln_b_ref[...])


def swa_mixer_layer(x, w_qkv, b_qkv, sinks, w_o, ln_g, ln_b, *, ts):
    bsz, seq, d = x.shape
    cq = w_o.shape[0]
    ckv = (w_qkv.shape[1] - cq) // 2
    dh = C_HEAD_DIM
    inv = ROPE_THETA ** (-jnp.arange(0, dh, 2, dtype=F32) / dh)
    ang = jnp.arange(seq, dtype=F32)[:, None] * inv[None, :]
    reps = LANES // (dh // 2)
    sign = jnp.tile(jnp.concatenate([-jnp.ones((dh // 2,), F32), jnp.ones((dh // 2,), F32)]), LANES // dh)
    cos_t = jnp.tile(jnp.cos(ang), (1, reps))
    sin_t = jnp.tile(jnp.sin(ang), (1, reps)) * sign[None, :]
    assert 2 * dh == LANES

    def dup_heads(t):
        th = t.reshape(t.shape[:-1] + (ckv // dh, dh))
        return jnp.concatenate([th, th], axis=-1).reshape(t.shape[:-1] + (2 * ckv,))

    w_all = jnp.concatenate([w_qkv[:, :cq], dup_heads(w_qkv[:, cq:cq + ckv]),
                             dup_heads(w_qkv[:, cq + ckv:])], axis=1)
    b_all = jnp.concatenate([b_qkv[:cq], dup_heads(b_qkv[cq:cq + ckv]), dup_heads(b_qkv[cq + ckv:])])
    args = (x, w_all.astype(BF16), b_all.reshape(1, -1), cos_t, sin_t, sinks.astype(F32),
            w_o.astype(BF16), ln_g.reshape(1, -1), ln_b.reshape(1, -1))
    in_specs = [pl.BlockSpec((1, ts, d), lambda b, j: (b, j, 0)),
                _full(args[1].shape), _full(args[2].shape),
                pl.BlockSpec((ts, LANES), lambda b, j: (j, 0)),
                pl.BlockSpec((ts, LANES), lambda b, j: (j, 0)),
                pl.BlockSpec(memory_space=pltpu.SMEM),
                _full(args[6].shape), _full(args[7].shape), _full(args[8].shape)]
    return pl.pallas_call(
        functools.partial(_swa_kernel, ts=ts, cq=cq, ckv=ckv),
        grid=(bsz, seq // ts),
        in_specs=in_specs,
        out_specs=pl.BlockSpec((1, ts, d), lambda b, j: (b, j, 0)),
        out_shape=jax.ShapeDtypeStruct((bsz, seq, d), F32),
        scratch_shapes=[pltpu.VMEM((CHUNK, 2 * ckv), F32), pltpu.VMEM((CHUNK, 2 * ckv), F32)],
        compiler_params=_cparams(("arbitrary", "arbitrary")),
        name="swa_mixer",
    )(*args)


def _slot_kernel(route_ref, pstart_ref, dest_ref, carry_ref, *, tb):
    @pl.when(pl.program_id(0) == 0)
    def _():
        carry_ref[...] = pstart_ref[...]

    rec = route_ref[...]
    lane = lax.broadcasted_iota(jnp.int32, (tb, ROUTE_W), 1)
    e0 = rec[:, 0:1].astype(jnp.int32)
    e1 = rec[:, 1:2].astype(jnp.int32)
    oh0 = lane == e0
    oh1 = lane == e1
    ohs = jnp.where(jnp.logical_or(oh0, oh1), 1.0, 0.0)
    r = lax.broadcasted_iota(jnp.int32, (tb, tb), 0)
    c = lax.broadcasted_iota(jnp.int32, (tb, tb), 1)
    before = jnp.where(c < r, 1.0, 0.0).astype(BF16)
    prefix = _dot(before, ohs.astype(BF16)) + carry_ref[...]
    d0 = jnp.sum(jnp.where(oh0, prefix, 0.0), axis=1, keepdims=True)
    d1 = jnp.sum(jnp.where(oh1, prefix, 0.0), axis=1, keepdims=True)
    dest = jnp.where(lane == 0, d0, jnp.where(lane == 1, d1, 0.0))
    dest_ref[...] = dest.T[:DEST_ROWS].astype(jnp.int32)
    carry_ref[...] += jnp.sum(ohs, axis=0, keepdims=True)


def moe_slots(route, pstart, *, tb):
    n = route.shape[0]
    return pl.pallas_call(
        functools.partial(_slot_kernel, tb=tb),
        grid=(n // tb,),
        in_specs=[pl.BlockSpec((tb, ROUTE_W), lambda i: (i, 0)), _full((1, ROUTE_W))],
        out_specs=pl.BlockSpec((DEST_ROWS, tb), lambda i: (0, i)),
        out_shape=jax.ShapeDtypeStruct((DEST_ROWS, n), jnp.int32),
        scratch_shapes=[pltpu.VMEM((1, ROUTE_W), F32)],
        compiler_params=_cparams(("arbitrary",)),
        name="moe_slots",
    )(route, pstart)


def _ffn_kernel(blk_exp_ref, new_exp_ref, nblk_ref, xs_ref, w1_ref, w3_ref, w2_ref, ys_ref,
                w1_b, w3_b, w2_b):
    i = pl.program_id(0)
    used = i < nblk_ref[0]

    @pl.when(jnp.logical_and(used, new_exp_ref[i] == 1))
    def _():
        w1_b[...] = w1_ref[0, 0].astype(BF16)
        w3_b[...] = w3_ref[0, 0].astype(BF16)
        w2_b[...] = w2_ref[0, 0].astype(BF16)

    @pl.when(used)
    def _():
        x_lo, x_hi = _unpack_halves(xs_ref[...])
        x_lo, x_hi = x_lo.astype(BF16), x_hi.astype(BF16)
        dl = x_lo.shape[1]
        h1 = _dot(x_lo, w1_b[:dl, :]) + _dot(x_hi, w1_b[dl:, :])
        h3 = _dot(x_lo, w3_b[:dl, :]) + _dot(x_hi, w3_b[dl:, :])
        h = (_silu(h1) * h3).astype(BF16)
        ys_ref[...] = _pack_halves(_dot(h, w2_b[...]))

    @pl.when(jnp.logical_not(used))
    def _():
        ys_ref[...] = jnp.zeros_like(ys_ref)


def moe_ffn(xs, blk_exp, new_exp, nblk, w1, w3, w2, *, layer, bm):
    n_pad, dp = xs.shape
    d = 2 * dp
    de = w1.shape[3]
    n_blk = n_pad // bm

    def x_map(i, be, ne, nb):
        return (jnp.minimum(i, nb[0] - 1), 0)

    def w_map(i, be, ne, nb):
        return (layer, be[i], 0, 0)

    return pl.pallas_call(
        _ffn_kernel,
        grid_spec=pltpu.PrefetchScalarGridSpec(
            num_scalar_prefetch=3,
            grid=(n_blk,),
            in_specs=[pl.BlockSpec((bm, dp), x_map),
                      pl.BlockSpec((1, 1, d, de), w_map),
                      pl.BlockSpec((1, 1, d, de), w_map),
                      pl.BlockSpec((1, 1, de, d), w_map)],
            out_specs=pl.BlockSpec((bm, dp), lambda i, be, ne, nb: (i, 0)),
            scratch_shapes=[pltpu.VMEM((d, de), BF16), pltpu.VMEM((d, de), BF16),
                            pltpu.VMEM((de, d), BF16)]),
        out_shape=jax.ShapeDtypeStruct((n_pad, dp), jnp.int32),
        compiler_params=_cparams(("arbitrary",)),
        name="moe_ffn",
    )(blk_exp, new_exp, nblk, xs, w1, w3, w2)


def _combine_kernel(x_ref, g0_ref, g1_ref, route_ref, ln_g_ref, ln_b_ref, o_ref):
    rec = route_ref[...]
    a_lo, a_hi = _unpack_halves(g0_ref[0])
    b_lo, b_hi = _unpack_halves(g1_ref[0])
    w0, w1 = rec[:, 2:3], rec[:, 3:4]
    y = jnp.concatenate([w0 * a_lo + w1 * b_lo, w0 * a_hi + w1 * b_hi], axis=1)
    o_ref[...] = _ln(DN_ALPHA * x_ref[...] + y, ln_g_ref[...], ln_b_ref[...])


def moe_combine(x, g, route, ln_g, ln_b, *, tb):
    n, d = x.shape
    row_spec = pl.BlockSpec((tb, d), lambda i: (i, 0))
    return pl.pallas_call(
        _combine_kernel,
        grid=(n // tb,),
        in_specs=[row_spec,
                  pl.BlockSpec((1, tb, d // 2), lambda i: (0, i, 0)),
                  pl.BlockSpec((1, tb, d // 2), lambda i: (1, i, 0)),
                  pl.BlockSpec((tb, ROUTE_W), lambda i: (i, 0)),
                  _full((1, d)), _full((1, d))],
        out_specs=row_spec,
        out_shape=jax.ShapeDtypeStruct((n, d), F32),
        compiler_params=_cparams(("arbitrary",)),
        name="moe_combine",
    )(x, g, g, route, ln_g.reshape(1, -1), ln_b.reshape(1, -1))


MOE_BM = 512
MIXER_TS = 512


def hierarchical_moe_layer(x2, xp, route, counts, w1, w3, w2, ln_g, ln_b, *, layer):
    bsz, seq, d = x2.shape
    n = bsz * seq
    bm = MOE_BM
    xf = x2.reshape(n, d)
    rt = route.reshape(n, ROUTE_W)
    n_blk = (2 * n) // bm + N_EXPERTS
    n_pad = n_blk * bm
    cnt = counts[0, :N_EXPERTS].astype(jnp.int32)
    pcnt = (cnt + bm - 1) // bm * bm
    pends = jnp.cumsum(pcnt)
    pstart = pends - pcnt
    nblk = (pends[-1] // bm).astype(jnp.int32).reshape(1)
    blk_row = jnp.arange(n_blk, dtype=jnp.int32) * bm
    blk_exp = jnp.minimum(jnp.sum((pends[None, :] <= blk_row[:, None]).astype(jnp.int32), axis=1),
                          N_EXPERTS - 1)
    last_exp = blk_exp[jnp.maximum(nblk[0] - 1, 0)]
    blk_exp = jnp.where(jnp.arange(n_blk) < nblk[0], blk_exp, last_exp)
    new_exp = jnp.concatenate([jnp.ones((1,), jnp.int32),
                               (blk_exp[1:] != blk_exp[:-1]).astype(jnp.int32)])
    pstart_rec = jnp.zeros((1, ROUTE_W), F32).at[0, :N_EXPERTS].set(pstart.astype(F32))
    dest = moe_slots(rt, pstart_rec, tb=min(512, n))
    xs = moe_dispatch(xp.reshape(n, d // 2), dest[0], dest[1], n_pad)
    ys = moe_ffn(xs, blk_exp, new_exp, nblk, w1, w3, w2, layer=layer, bm=bm)
    g = moe_gather(ys, dest[0], dest[1])
    out = moe_combine(xf, g, rt, ln_g, ln_b, tb=min(512, n))
    return out.reshape(bsz, seq, d)


SC_ROWS = 128


def _sc_mesh():
    return plsc.VectorSubcoreMesh(core_axis_name="c", subcore_axis_name="s")


def moe_dispatch(xf, dest0, dest1, n_pad):
    n, d = xf.shape
    info = plsc.get_sparse_core_info()
    nw = info.num_cores * info.num_subcores
    per_w = n // nw
    r = min(SC_ROWS, per_w)

    def body(x_hbm, d0_hbm, d1_hbm, xs_hbm, i0_v, i1_v, rows_v, sem):
        wid = lax.axis_index("s") * info.num_cores + lax.axis_index("c")

        @pl.loop(0, per_w // r)
        def _(c):
            base = pl.multiple_of(wid * per_w + c * r, 8)
            pltpu.sync_copy(d0_hbm.at[pl.ds(base, r)], i0_v)
            pltpu.sync_copy(d1_hbm.at[pl.ds(base, r)], i1_v)
            pltpu.sync_copy(x_hbm.at[pl.ds(base, r)], rows_v)
            pltpu.async_copy(rows_v, xs_hbm.at[i0_v], sem).wait()
            pltpu.async_copy(rows_v, xs_hbm.at[i1_v], sem).wait()

    return pl.kernel(
        body, out_type=jax.ShapeDtypeStruct((n_pad, d), xf.dtype), mesh=_sc_mesh(),
        scratch_types=[pltpu.VMEM((r,), jnp.int32), pltpu.VMEM((r,), jnp.int32),
                       pltpu.VMEM((r, d), xf.dtype), pltpu.SemaphoreType.DMA],
        name="moe_dispatch",
    )(xf, dest0, dest1)


def moe_gather(ys, dest0, dest1):
    n = dest0.shape[0]
    d = ys.shape[1]
    info = plsc.get_sparse_core_info()
    nw = info.num_cores * info.num_subcores
    per_w = n // nw
    r = min(SC_ROWS, per_w)

    def body(ys_hbm, d0_hbm, d1_hbm, g_hbm, i_v, rows_v, sem):
        wid = lax.axis_index("s") * info.num_cores + lax.axis_index("c")

        @pl.loop(0, per_w // r)
        def _(c):
            base = pl.multiple_of(wid * per_w + c * r, 8)
            for k, d_hbm in enumerate((d0_hbm, d1_hbm)):
                pltpu.sync_copy(d_hbm.at[pl.ds(base, r)], i_v)
                pltpu.async_copy(ys_hbm.at[i_v], rows_v, sem).wait()
                pltpu.sync_copy(rows_v, g_hbm.at[k, pl.ds(base, r)])

    return pl.kernel(
        body, out_type=jax.ShapeDtypeStruct((2, n, d), ys.dtype), mesh=_sc_mesh(),
        scratch_types=[pltpu.VMEM((r,), jnp.int32), pltpu.VMEM((r, d), ys.dtype),
                       pltpu.SemaphoreType.DMA],
        name="moe_gather",
    )(ys, dest0, dest1)


def kernel(x, mem, ln_g, ln_b, ev_w_in, ev_gm_ln_g, ev_gm_ln_b, ev_gm_ws, ev_gm_bs, ev_conv_w, ev_conv_b, ev_wq, ev_wk, ev_wv, ev_w_if, ev_b_if, ev_norm_w, ev_skip, ev_w_out, od_w_qkv, od_b_qkv, od_sinks, od_w_o, xa_wq, xa_wkv, xa_wo, moe_w_rg, moe_b_rg, moe_w_re, moe_b_re, moe_w1, moe_w3, moe_w2):
    bsz, seq, d = x.shape
    m_len = mem.shape[1]
    depth = ln_g.shape[0]
    ts = min(MIXER_TS, seq)
    memf = mem.reshape(bsz * m_len, d)
    for l in range(depth):
        if l % 2 == 0:
            e = l // 2
            x = even_mixer_layer(x, ev_w_in[e], ev_gm_ln_g[e], ev_gm_ln_b[e], ev_gm_ws[e], ev_gm_bs[e],
                                 ev_conv_w[e], ev_conv_b[e], ev_wq[e], ev_wk[e], ev_wv[e], ev_w_if[e],
                                 ev_b_if[e], ev_norm_w[e], ev_skip[e], ev_w_out[e],
                                 ln_g[l, 0], ln_b[l, 0], ts=ts)
        else:
            o = l // 2
            x = swa_mixer_layer(x, od_w_qkv[o], od_b_qkv[o], od_sinks[o], od_w_o[o],
                                ln_g[l, 0], ln_b[l, 0], ts=ts)
        kv = matmul(memf, xa_wkv[l], tm=min(512, bsz * m_len), tn=512, out_dtype=BF16)
        kv = kv.reshape(bsz, m_len, 2 * d)
        x, xp, route, counts = xattn_router_layer(x, kv, xa_wq[l], xa_wo[l], ln_g[l, 1], ln_b[l, 1],
                                                  moe_w_rg[l], moe_b_rg[l], moe_w_re[l], moe_b_re[l],
                                                  ts=min(XATTN_TS, seq))
        x = hierarchical_moe_layer(x, xp, route, counts, moe_w1, moe_w3, moe_w2,
                                   ln_g[l, 2], ln_b[l, 2], layer=l)
    return x
```

```python
import functools
import math

import jax
import jax.numpy as jnp
from jax import lax
from jax.experimental import pallas as pl
from jax.experimental.pallas import tpu as pltpu
from jax.experimental.pallas import tpu_sc as plsc

F32 = jnp.float32
BF16 = jnp.bfloat16

A_GROUPS = 4
CHUNK = 128
B_HEADS = 4
B_CONV = 4
C_HEAD_DIM = 64
C_KV_HEADS = 4
X_HEADS = 4
N_GROUPS = 4
EXPERTS_PER_GROUP = 8
N_EXPERTS = N_GROUPS * EXPERTS_PER_GROUP
ROPE_THETA = 10000.0
LN_EPS = 1e-5
DEPTH = 2
DN_ALPHA = (2 * DEPTH) ** 0.25

LANES = 128
VMEM_LIMIT = 48 * 1024 * 1024
NEG = -1e30


def _cparams(sem):
    return pltpu.CompilerParams(dimension_semantics=sem, vmem_limit_bytes=VMEM_LIMIT)


def _full(shape):
    nd = len(shape)
    return pl.BlockSpec(shape, lambda *_: (0,) * nd)


def _dot(a, b):
    return jnp.dot(a, b, preferred_element_type=F32)


def _dot_nt(a, b):
    return lax.dot_general(a, b, (((1,), (1,)), ((), ())), preferred_element_type=F32)


def _dot_tn(a, b):
    return lax.dot_general(a, b, (((0,), (0,)), ((), ())), preferred_element_type=F32)


def _split_dot(a, b_bf16):
    hi = a.astype(BF16)
    lo = (a - hi.astype(F32)).astype(BF16)
    return _dot(hi, b_bf16) + _dot(lo, b_bf16)


def _ln(x, g, b):
    mu = jnp.mean(x, axis=-1, keepdims=True)
    xc = x - mu
    var = jnp.mean(xc * xc, axis=-1, keepdims=True)
    return xc * lax.rsqrt(var + LN_EPS) * g + b


def _silu(x):
    return x * (1.0 / (1.0 + jnp.exp(-x)))


def _gelu(x):
    return 0.5 * x * (1.0 + jnp.tanh(math.sqrt(2.0 / math.pi) * (x + 0.044715 * (x * x * x))))


def _pack_halves(x):
    c = x.shape[1] // 2
    lo = lax.bitcast_convert_type(x[:, :c].astype(BF16).astype(F32), jnp.uint32)
    hi = lax.bitcast_convert_type(x[:, c:].astype(BF16).astype(F32), jnp.uint32)
    return lax.bitcast_convert_type((lo >> 16) | hi, jnp.int32)


def _unpack_halves(p):
    u = lax.bitcast_convert_type(p, jnp.uint32)
    lo = lax.bitcast_convert_type(u << 16, F32)
    hi = lax.bitcast_convert_type(u & jnp.uint32(0xFFFF0000), F32)
    return lo, hi


def _log_sigmoid(x):
    return jnp.minimum(x, 0.0) - jnp.log(1.0 + jnp.exp(-jnp.abs(x)))


def _even_kernel(x_ref, w_in_ref, gm_g_ref, gm_b_ref, gm_w_ref, gm_bias_ref,
                 conv_w_ref, conv_b_ref, wq_ref, wk_ref, wv_ref, wif_t_ref, bif_t_ref,
                 norm_w_ref, skip_ref, w_out_ref, ln_g_ref, ln_b_ref,
                 o_ref,
                 xm_buf, ct_ref, m_ref, *, ts, aw, bw):
    dh = bw // B_HEADS
    agd = aw // A_GROUPS
    nck = ts // CHUNK
    pad = 8
    j = pl.program_id(1)

    @pl.when(j == 0)
    def _():
        xm_buf[0:pad, :] = jnp.zeros((pad, bw), F32)
        ct_ref[...] = jnp.zeros_like(ct_ref)
        m_ref[...] = jnp.zeros_like(m_ref)

    row = lax.broadcasted_iota(jnp.int32, (CHUNK, CHUNK), 0)
    col = lax.broadcasted_iota(jnp.int32, (CHUNK, CHUNK), 1)
    causal = col <= row
    diag = col == row
    triu = jnp.where(row <= col, 1.0, 0.0).astype(BF16)
    ones_blk = jnp.ones((CHUNK, LANES), BF16)

    x = x_ref[0]
    proj = _dot(x.astype(BF16), w_in_ref[...])
    a_u = _gelu(proj[:, :aw])
    a_v = _gelu(proj[:, aw:2 * aw])
    xm = proj[:, 2 * aw:2 * aw + bw]
    z = proj[:, 2 * aw + bw:]

    vn = _ln(a_v, gm_g_ref[...], gm_b_ref[...]).astype(BF16)
    ya_chunks = []
    for c in range(nck):
        cols = []
        for g in range(A_GROUPS):
            v_cg = vn[c * CHUNK:(c + 1) * CHUNK, g * agd:(g + 1) * agd]
            cols.append(_dot(gm_w_ref[g], v_cg))
        ya_chunks.append(jnp.concatenate(cols, axis=1) + gm_bias_ref[...])
    y_a = a_u * jnp.concatenate(ya_chunks, axis=0)

    xm_buf[pad:pad + ts, :] = xm
    conv = conv_b_ref[...] + conv_w_ref[B_CONV - 1:B_CONV, :] * xm
    for k in range(B_CONV - 1):
        sh = B_CONV - 1 - k
        conv = conv + conv_w_ref[k:k + 1, :] * xm_buf[pad - sh:pad - sh + ts, :]
    xm_buf[pad - (B_CONV - 1):pad, :] = xm_buf[pad + ts - (B_CONV - 1):pad + ts, :]
    xc = _silu(conv)
    xc_b = xc.astype(BF16)
    q = _dot_head_pairs(xc_b, wq_ref)
    k_ = _dot_head_pairs(xc_b, wk_ref)
    v = _dot_head_pairs(xm.astype(BF16), wv_ref)
    gate_in = jnp.concatenate([q, k_, v], axis=1).astype(BF16)
    gates_t = _dot_nt(wif_t_ref[...], gate_in) + bif_t_ref[...]
    ig_all = gates_t[:B_HEADS, :]
    lf_all = _log_sigmoid(gates_t[B_HEADS:, :])
    q_b = q.astype(BF16)
    k_b = (k_ * dh ** -0.5).astype(BF16)
    v_b = v.astype(BF16)
    gate_z = _silu(z)

    h_chunks = []
    for c in range(nck):
        sl = slice(c * CHUNK, (c + 1) * CHUNK)
        lf_r = lf_all[:, sl]
        b_r = _split_dot(lf_r, triu)
        a_r = ig_all[:, sl] - b_r
        heads = []
        for h in range(B_HEADS):
            hs = slice(h * dh, (h + 1) * dh)
            qh, kh = q_b[sl, hs], k_b[sl, hs]
            vaug = jnp.concatenate([v_b[sl, hs], ones_blk], axis=1)
            a_row = a_r[h:h + 1, :]
            m_row = m_ref[h:h + 1, :]
            amat = jnp.where(causal, a_row, NEG)
            mx = jnp.maximum(jnp.max(amat, axis=1, keepdims=True), m_row)
            w_intra = jnp.exp(amat - mx)
            w_state = jnp.exp(m_row - mx)
            s = _dot_nt(qh, kh) * w_intra
            caug = ct_ref[h]
            naug = (_dot(s.astype(BF16), vaug)
                    + jnp.concatenate([w_state, w_state], axis=1) * _dot(qh, caug.astype(BF16)))
            num, nq = naug[:, :dh], naug[:, dh:]
            b_t = _split_dot(jnp.where(causal, lf_r[h:h + 1, :], 0.0), ones_blk)
            hv = num / jnp.maximum(jnp.abs(nq), jnp.exp(-(b_t + mx)))
            bl = b_r[h:h + 1, CHUNK - 1:CHUNK]
            g_row = bl + a_row
            m_new = jnp.maximum(bl + m_row, jnp.max(g_row, axis=1, keepdims=True))
            decay = jnp.exp(bl + m_row - m_new)
            wg_row = jnp.exp(g_row - m_new)
            wg_t = _dot(jnp.where(diag, wg_row, 0.0).astype(BF16), ones_blk)
            kw = (kh.astype(F32) * wg_t).astype(BF16)
            ct_ref[h] = jnp.concatenate([decay, decay], axis=1) * caug + _dot_tn(kw, vaug)
            m_ref[h:h + 1, :] = m_new
            hc = hv - jnp.mean(hv, axis=1, keepdims=True)
            hn = hc * lax.rsqrt(jnp.mean(hc * hc, axis=1, keepdims=True) + LN_EPS)
            heads.append(hn)
        h_chunks.append(jnp.concatenate(heads, axis=1))
    hn_all = jnp.concatenate(h_chunks, axis=0) if nck > 1 else h_chunks[0]
    y_b = (hn_all * norm_w_ref[...] + skip_ref[...] * xc) * gate_z

    mix = jnp.concatenate([y_a, y_b], axis=1).astype(BF16)
    y = _dot(mix, w_out_ref[...])
    o_ref[0] = _ln(DN_ALPHA * x + y, ln_g_ref[...], ln_b_ref[...])


def _split_dot_left(a_bf16, b):
    hi = b.astype(BF16)
    lo = (b - hi.astype(F32)).astype(BF16)
    return _dot(a_bf16, hi) + _dot(a_bf16, lo)


def _block_diag_pairs(w):
    hh, d, _ = w.shape
    wp = w.reshape(hh // 2, 2, d, d)
    eye = jnp.eye(2, dtype=w.dtype)
    return jnp.einsum('pade,ab->padbe', wp, eye).reshape(hh // 2, 2 * d, 2 * d)


def _dot_head_pairs(x, w_ref):
    npair, w2, _ = w_ref.shape
    return jnp.concatenate([_dot(x[:, p * w2:(p + 1) * w2], w_ref[p]) for p in range(npair)], axis=1)


def even_mixer_layer(x, w_in, gm_ln_g, gm_ln_b, gm_ws, gm_bs, conv_w, conv_b, wq, wk, wv, w_if,
                     b_if, norm_w, skip, w_out, ln_g, ln_b, *, ts):
    bsz, seq, d = x.shape
    aw = gm_ln_g.shape[0]
    bw = conv_b.shape[0]
    agd = aw // A_GROUPS
    causal = jnp.tril(jnp.ones((CHUNK, CHUNK), dtype=bool))
    gm_w = jnp.where(causal[None], gm_ws, 0.0).astype(BF16)
    gm_bias = jnp.repeat(gm_bs.T, agd, axis=1)
    row = lambda a: a.reshape(1, -1)
    args = (x, w_in.astype(BF16), row(gm_ln_g), row(gm_ln_b), gm_w, gm_bias,
            conv_w, row(conv_b), _block_diag_pairs(wq).astype(BF16), _block_diag_pairs(wk).astype(BF16),
            _block_diag_pairs(wv).astype(BF16), w_if.T.astype(BF16),
            b_if.reshape(-1, 1), row(norm_w), row(skip), w_out.astype(BF16),
            row(ln_g), row(ln_b))
    in_specs = [pl.BlockSpec((1, ts, d), lambda b, j: (b, j, 0))] + [_full(a.shape) for a in args[1:]]
    dh = bw // B_HEADS
    assert dh == LANES and CHUNK == LANES
    return pl.pallas_call(
        functools.partial(_even_kernel, ts=ts, aw=aw, bw=bw),
        grid=(bsz, seq // ts),
        in_specs=in_specs,
        out_specs=pl.BlockSpec((1, ts, d), lambda b, j: (b, j, 0)),
        out_shape=jax.ShapeDtypeStruct((bsz, seq, d), F32),
        scratch_shapes=[pltpu.VMEM((8 + ts, bw), F32),
                        pltpu.VMEM((B_HEADS, dh, dh + LANES), F32),
                        pltpu.VMEM((B_HEADS, LANES), F32)],
        compiler_params=_cparams(("arbitrary", "arbitrary")),
        name="even_mixer",
    )(*args)


def _memfold_kernel(mem_ref, wkv_ref, wq_ref, wo_ref, wqk_ref, vo_ref, *, d):
    dh = d // X_HEADS
    m_len = mem_ref.shape[1]
    kv = _dot(mem_ref[0].astype(BF16), wkv_ref[...])
    k = (kv[:, :d] * dh ** -0.5).astype(BF16)
    v = kv[:, d:].astype(BF16)
    for h in range(X_HEADS):
        hs = slice(h * dh, (h + 1) * dh)
        ms = slice(h * m_len, (h + 1) * m_len)
        wqk_ref[0, :, ms] = _dot_nt(wq_ref[:, hs], k[:, hs]).astype(BF16)
        vo_ref[0, ms, :] = _dot(v[:, hs], wo_ref[hs, :]).astype(BF16)


def memory_fold(mem, wkv, wq, wo):
    bsz, m_len, d = mem.shape
    hm = X_HEADS * m_len
    args = (mem, wkv.astype(BF16), wq.astype(BF16), wo.astype(BF16))
    return pl.pallas_call(
        functools.partial(_memfold_kernel, d=d),
        grid=(bsz,),
        in_specs=[pl.BlockSpec((1, m_len, d), lambda b: (b, 0, 0))] + [_full(a.shape) for a in args[1:]],
        out_specs=[pl.BlockSpec((1, d, hm), lambda b: (b, 0, 0)),
                   pl.BlockSpec((1, hm, d), lambda b: (b, 0, 0))],
        out_shape=[jax.ShapeDtypeStruct((bsz, d, hm), BF16),
                   jax.ShapeDtypeStruct((bsz, hm, d), BF16)],
        compiler_params=_cparams(("arbitrary",)),
        name="memory_fold",
    )(*args)


ROUTE_W = 128
XATTN_TS = 1024
XATTN_GROUP = 1024
DEST_ROWS = 8


def _xattn_kernel(x_ref, wqk_ref, vo_ref, ln_g_ref, ln_b_ref, wr_ref, br_ref,
                  o_ref, xp_ref, route_ref, cnt_ref):
    m_len = wqk_ref.shape[2] // X_HEADS
    first = jnp.logical_and(pl.program_id(0) == 0, pl.program_id(1) == 0)

    @pl.when(first)
    def _():
        cnt_ref[...] = jnp.zeros_like(cnt_ref)

    ts = XATTN_GROUP if x_ref.shape[1] % XATTN_GROUP == 0 else x_ref.shape[1]
    ngroups = x_ref.shape[1] // ts
    lane = lax.broadcasted_iota(jnp.int32, (ts, ROUTE_W), 1)
    is_g = lane < N_GROUPS
    st = [dict() for _ in range(ngroups)]

    def stage_q(g):
        x = x_ref[0, g * ts:(g + 1) * ts, :]
        st[g]["x"] = x
        st[g]["s"] = _dot(x.astype(BF16), wqk_ref[0])

    def stage_attn(g):
        s_all = st[g].pop("s")
        probs = []
        for h in range(X_HEADS):
            s = s_all[:, h * m_len:(h + 1) * m_len]
            p = jnp.exp(s - jnp.max(s, axis=1, keepdims=True))
            probs.append((p / jnp.sum(p, axis=1, keepdims=True)).astype(BF16))
        st[g]["p"] = jnp.concatenate(probs, axis=1)

    def stage_out(g):
        y = _dot(st[g].pop("p"), vo_ref[0])
        x2 = _ln(DN_ALPHA * st[g].pop("x") + y, ln_g_ref[...], ln_b_ref[...])
        rows = slice(g * ts, (g + 1) * ts)
        o_ref[0, rows, :] = x2
        xp_ref[0, rows, :] = _pack_halves(x2)
        x_hi = x2.astype(BF16)
        x_lo = (x2 - x_hi.astype(F32)).astype(BF16)
        st[g]["logits"] = (_dot(x_hi, wr_ref[0]) + _dot(x_lo, wr_ref[0]) + _dot(x_hi, wr_ref[1])) + br_ref[...]

    def stage_route(g):
        logits = st[g].pop("logits")
        lg = jnp.where(is_g, logits, NEG)
        mg = jnp.max(lg, axis=1, keepdims=True)
        gi = jnp.min(jnp.where(jnp.logical_and(is_g, lg == mg), lane, ROUTE_W), axis=1, keepdims=True)
        gate_g = 1.0 / jnp.sum(jnp.where(is_g, jnp.exp(lg - mg), 0.0), axis=1, keepdims=True)
        lo = N_GROUPS + gi * EXPERTS_PER_GROUP
        in_grp = jnp.logical_and(lane >= lo, lane < lo + EXPERTS_PER_GROUP)
        le = jnp.where(in_grp, logits, NEG)
        v1 = jnp.max(le, axis=1, keepdims=True)
        i1 = jnp.min(jnp.where(jnp.logical_and(in_grp, le == v1), lane, ROUTE_W), axis=1, keepdims=True)
        le2 = jnp.where(lane == i1, NEG, le)
        v2 = jnp.max(le2, axis=1, keepdims=True)
        i2 = jnp.min(jnp.where(jnp.logical_and(in_grp, le2 == v2), lane, ROUTE_W), axis=1, keepdims=True)
        e21 = jnp.exp(v2 - v1)
        p1 = 1.0 / (1.0 + e21)
        p2 = e21 * p1
        e1 = (i1 - N_GROUPS).astype(F32)
        e2 = (i2 - N_GROUPS).astype(F32)
        rec = jnp.where(lane == 0, e1, 0.0)
        rec = jnp.where(lane == 1, e2, rec)
        rec = jnp.where(lane == 2, gate_g * p1, rec)
        rec = jnp.where(lane == 3, gate_g * p2, rec)
        route_ref[0, g * ts:(g + 1) * ts, :] = rec
        sel = jnp.logical_or(lane == i1 - N_GROUPS, lane == i2 - N_GROUPS)
        st[g]["counts"] = jnp.sum(jnp.where(sel, 1.0, 0.0), axis=0, keepdims=True)

    stages = (stage_q, stage_attn, stage_out, stage_route)
    for t in range(ngroups + len(stages) - 1):
        for k, stage in enumerate(stages):
            if 0 <= t - k < ngroups:
                stage(t - k)
    counts = st[0]["counts"]
    for g in range(1, ngroups):
        counts = counts + st[g]["counts"]
    cnt_ref[...] += counts


def xattn_router_layer(x, wqk, vo, ln_g, ln_b, w_rg, b_rg, w_re, b_re, *, ts):
    bsz, seq, d = x.shape
    hm = wqk.shape[2]
    wr = jnp.zeros((d, ROUTE_W), F32).at[:, :N_GROUPS].set(w_rg).at[:, N_GROUPS:N_GROUPS + N_EXPERTS].set(w_re)
    br = jnp.zeros((1, ROUTE_W), F32).at[0, :N_GROUPS].set(b_rg).at[0, N_GROUPS:N_GROUPS + N_EXPERTS].set(b_re)
    wr_hi = wr.astype(BF16)
    wr_lo = (wr - wr_hi.astype(F32)).astype(BF16)
    wr = jnp.stack([wr_hi, wr_lo])
    args = (x, wqk, vo, ln_g.reshape(1, -1), ln_b.reshape(1, -1), wr, br)
    in_specs = [pl.BlockSpec((1, ts, d), lambda b, j: (b, j, 0)),
                pl.BlockSpec((1, d, hm), lambda b, j: (b, 0, 0)),
                pl.BlockSpec((1, hm, d), lambda b, j: (b, 0, 0))] + [_full(a.shape) for a in args[3:]]
    return pl.pallas_call(
        _xattn_kernel,
        grid=(bsz, seq // ts),
        in_specs=in_specs,
        out_specs=[pl.BlockSpec((1, ts, d), lambda b, j: (b, j, 0)),
                   pl.BlockSpec((1, ts, d // 2), lambda b, j: (b, j, 0)),
                   pl.BlockSpec((1, ts, ROUTE_W), lambda b, j: (b, j, 0)),
                   pl.BlockSpec((1, ROUTE_W), lambda b, j: (0, 0))],
        out_shape=[jax.ShapeDtypeStruct((bsz, seq, d), F32),
                   jax.ShapeDtypeStruct((bsz, seq, d // 2), jnp.int32),
                   jax.ShapeDtypeStruct((bsz, seq, ROUTE_W), F32),
                   jax.ShapeDtypeStruct((1, ROUTE_W), F32)],
        compiler_params=_cparams(("arbitrary", "arbitrary")),
        name="xattn_router",
    )(*args)


def _swa_kernel(x_ref, wqkv_ref, bqkv_ref, cos_ref, sin_ref, sink_ref, wo_ref, ln_g_ref, ln_b_ref,
                o_ref, kprev, vprev, *, ts, cq, ckv):
    j = pl.program_id(1)
    nb = ts // CHUNK
    dh = C_HEAD_DIM
    grp = (cq // dh) // C_KV_HEADS

    @pl.when(j == 0)
    def _():
        kprev[...] = jnp.zeros_like(kprev)
        vprev[...] = jnp.zeros_like(vprev)

    x = x_ref[0]
    qkv = _dot(x.astype(BF16), wqkv_ref[...]) + bqkv_ref[...]
    cos = cos_ref[...]
    sin = sin_ref[...]
    lane = lax.broadcasted_iota(jnp.int32, (ts, LANES), 1)
    first_half = (lane % dh) < (dh // 2)

    def rope(t):
        outs = []
        for c in range(t.shape[1] // LANES):
            tc = t[:, c * LANES:(c + 1) * LANES]
            rot = jnp.where(first_half, pltpu.roll(tc, LANES - dh // 2, 1), pltpu.roll(tc, dh // 2, 1))
            outs.append(tc * cos + rot * sin)
        return jnp.concatenate(outs, axis=1)

    q = (rope(qkv[:, :cq]) * dh ** -0.5).astype(BF16)
    k = rope(qkv[:, cq:cq + 2 * ckv]).astype(BF16)
    v = qkv[:, cq + 2 * ckv:].astype(BF16)

    r_i = lax.broadcasted_iota(jnp.int32, (CHUNK, 2 * CHUNK), 0)
    c_i = lax.broadcasted_iota(jnp.int32, (CHUNK, 2 * CHUNK), 1)
    band = jnp.logical_and(c_i > r_i, c_i <= r_i + CHUNK)
    sink_col = c_i == 0
    lane_k = lax.broadcasted_iota(jnp.int32, (2 * CHUNK, LANES), 1)
    key_row = lax.broadcasted_iota(jnp.int32, (2 * CHUNK, LANES), 0)
    lane_q = lax.broadcasted_iota(jnp.int32, (CHUNK, LANES), 1)
    ones_blk = jnp.ones((2 * CHUNK, LANES), BF16)
    zero_b = jnp.zeros((), BF16)
    blocks = []
    for c in range(nb):
        sl = slice(c * CHUNK, (c + 1) * CHUNK)
        if c == 0:
            kb = jnp.concatenate([kprev[...].astype(BF16), k[sl]], axis=0)
            vb = jnp.concatenate([vprev[...].astype(BF16), v[sl]], axis=0)
            first_key = jnp.where(j > 0, 0, CHUNK)
            valid = jnp.logical_and(band, c_i >= first_key)
        else:
            kb = k[(c - 1) * CHUNK:(c + 1) * CHUNK]
            vb = v[(c - 1) * CHUNK:(c + 1) * CHUNK]
            valid = band
        tiles = []
        for h in range(C_KV_HEADS):
            kd = kb[:, h * LANES:(h + 1) * LANES]
            vd = vb[:, h * LANES:(h + 1) * LANES]
            k_lo = jnp.where(lane_k < dh, kd, zero_b)
            k_hi = jnp.where(lane_k >= dh, kd, zero_b)
            vz = jnp.where(key_row == 0, zero_b, vd)
            q2 = jnp.concatenate([q[sl, (2 * h) * LANES:(2 * h + 1) * LANES],
                                  q[sl, (2 * h + 1) * LANES:(2 * h + 2) * LANES]], axis=0)
            s_lo = _dot_nt(q2, k_lo)
            s_hi = _dot_nt(q2, k_hi)
            parts = []
            for qk, g in ((s_lo[:CHUNK], 0), (s_lo[CHUNK:], 2), (s_hi[:CHUNK], 1), (s_hi[CHUNK:], 3)):
                sink = sink_ref[h * grp + g]
                parts.append(jnp.where(valid, qk, jnp.where(sink_col, sink, NEG)))
            s = jnp.concatenate(parts, axis=0)
            p = jnp.exp(s - jnp.max(s, axis=1, keepdims=True)).astype(BF16)
            den = _dot(p, ones_blk)
            o2 = _dot(p, vz) / den
            tiles.append(jnp.where(lane_q < dh, o2[:CHUNK], o2[2 * CHUNK:3 * CHUNK]))
            tiles.append(jnp.where(lane_q < dh, o2[CHUNK:2 * CHUNK], o2[3 * CHUNK:]))
        blocks.append(jnp.concatenate(tiles, axis=1))
    kprev[...] = k[(nb - 1) * CHUNK:].astype(F32)
    vprev[...] = v[(nb - 1) * CHUNK:].astype(F32)
    att = (jnp.concatenate(blocks, axis=0) if nb > 1 else blocks[0]).astype(BF16)
    y = _dot(att, wo_ref[...])
    o_ref[0] = _ln(DN_ALPHA * x + y, ln_g_ref[...], ln_b_ref[...])


def swa_mixer_layer(x, w_qkv, b_qkv, sinks, w_o, ln_g, ln_b, *, ts):
    bsz, seq, d = x.shape
    cq = w_o.shape[0]
    ckv = (w_qkv.shape[1] - cq) // 2
    dh = C_HEAD_DIM
    inv = ROPE_THETA ** (-jnp.arange(0, dh, 2, dtype=F32) / dh)
    ang = jnp.arange(seq, dtype=F32)[:, None] * inv[None, :]
    reps = LANES // (dh // 2)
    sign = jnp.tile(jnp.concatenate([-jnp.ones((dh // 2,), F32), jnp.ones((dh // 2,), F32)]), LANES // dh)
    cos_t = jnp.tile(jnp.cos(ang), (1, reps))
    sin_t = jnp.tile(jnp.sin(ang), (1, reps)) * sign[None, :]
    assert 2 * dh == LANES

    def dup_heads(t):
        th = t.reshape(t.shape[:-1] + (ckv // dh, dh))
        return jnp.concatenate([th, th], axis=-1).reshape(t.shape[:-1] + (2 * ckv,))

    w_all = jnp.concatenate([w_qkv[:, :cq], dup_heads(w_qkv[:, cq:cq + ckv]),
                             dup_heads(w_qkv[:, cq + ckv:])], axis=1)
    b_all = jnp.concatenate([b_qkv[:cq], dup_heads(b_qkv[cq:cq + ckv]), dup_heads(b_qkv[cq + ckv:])])
    args = (x, w_all.astype(BF16), b_all.reshape(1, -1), cos_t, sin_t, sinks.astype(F32),
            w_o.astype(BF16), ln_g.reshape(1, -1), ln_b.reshape(1, -1))
    in_specs = [pl.BlockSpec((1, ts, d), lambda b, j: (b, j, 0)),
                _full(args[1].shape), _full(args[2].shape),
                pl.BlockSpec((ts, LANES), lambda b, j: (j, 0)),
                pl.BlockSpec((ts, LANES), lambda b, j: (j, 0)),
                pl.BlockSpec(memory_space=pltpu.SMEM),
                _full(args[6].shape), _full(args[7].shape), _full(args[8].shape)]
    return pl.pallas_call(
        functools.partial(_swa_kernel, ts=ts, cq=cq, ckv=ckv),
        grid=(bsz, seq // ts),
        in_specs=in_specs,
        out_specs=pl.BlockSpec((1, ts, d), lambda b, j: (b, j, 0)),
        out_shape=jax.ShapeDtypeStruct((bsz, seq, d), F32),
        scratch_shapes=[pltpu.VMEM((CHUNK, 2 * ckv), F32), pltpu.VMEM((CHUNK, 2 * ckv), F32)],
        compiler_params=_cparams(("arbitrary", "arbitrary")),
        name="swa_mixer",
    )(*args)


def _slot_kernel(route_ref, pstart_ref, dest_ref, carry_ref, *, tb):
    @pl.when(pl.program_id(0) == 0)
    def _():
        carry_ref[...] = pstart_ref[...]

    rec = route_ref[...]
    lane = lax.broadcasted_iota(jnp.int32, (tb, ROUTE_W), 1)
    e0 = rec[:, 0:1].astype(jnp.int32)
    e1 = rec[:, 1:2].astype(jnp.int32)
    oh0 = lane == e0
    oh1 = lane == e1
    ohs = jnp.where(jnp.logical_or(oh0, oh1), 1.0, 0.0)
    r = lax.broadcasted_iota(jnp.int32, (tb, tb), 0)
    c = lax.broadcasted_iota(jnp.int32, (tb, tb), 1)
    before = jnp.where(c < r, 1.0, 0.0).astype(BF16)
    prefix = _dot(before, ohs.astype(BF16)) + carry_ref[...]
    d0 = jnp.sum(jnp.where(oh0, prefix, 0.0), axis=1, keepdims=True)
    d1 = jnp.sum(jnp.where(oh1, prefix, 0.0), axis=1, keepdims=True)
    dest = jnp.where(lane == 0, d0, jnp.where(lane == 1, d1, 0.0))
    dest_ref[...] = dest.T[:DEST_ROWS].astype(jnp.int32)
    carry_ref[...] += jnp.sum(ohs, axis=0, keepdims=True)


def moe_slots(route, pstart, *, tb):
    n = route.shape[0]
    return pl.pallas_call(
        functools.partial(_slot_kernel, tb=tb),
        grid=(n // tb,),
        in_specs=[pl.BlockSpec((tb, ROUTE_W), lambda i: (i, 0)), _full((1, ROUTE_W))],
        out_specs=pl.BlockSpec((DEST_ROWS, tb), lambda i: (0, i)),
        out_shape=jax.ShapeDtypeStruct((DEST_ROWS, n), jnp.int32),
        scratch_shapes=[pltpu.VMEM((1, ROUTE_W), F32)],
        compiler_params=_cparams(("arbitrary",)),
        name="moe_slots",
    )(route, pstart)


def _ffn_kernel(blk_exp_ref, new_exp_ref, nblk_ref, xs_ref, w1_ref, w3_ref, w2_ref, ys_ref,
                w1_b, w3_b, w2_b):
    i = pl.program_id(0)
    used = i < nblk_ref[0]

    @pl.when(jnp.logical_and(used, new_exp_ref[i] == 1))
    def _():
        w1_b[...] = w1_ref[0, 0].astype(BF16)
        w3_b[...] = w3_ref[0, 0].astype(BF16)
        w2_b[...] = w2_ref[0, 0].astype(BF16)

    @pl.when(used)
    def _():
        x_lo, x_hi = _unpack_halves(xs_ref[...])
        x_lo, x_hi = x_lo.astype(BF16), x_hi.astype(BF16)
        dl = x_lo.shape[1]
        h1 = _dot(x_lo, w1_b[:dl, :]) + _dot(x_hi, w1_b[dl:, :])
        h3 = _dot(x_lo, w3_b[:dl, :]) + _dot(x_hi, w3_b[dl:, :])
        h = (_silu(h1) * h3).astype(BF16)
        ys_ref[...] = _pack_halves(_dot(h, w2_b[...]))

    @pl.when(jnp.logical_not(used))
    def _():
        ys_ref[...] = jnp.zeros_like(ys_ref)


def moe_ffn(xs, blk_exp, new_exp, nblk, w1, w3, w2, *, layer, bm):
    n_pad, dp = xs.shape
    d = 2 * dp
    de = w1.shape[3]
    n_blk = n_pad // bm

    def x_map(i, be, ne, nb):
        return (jnp.minimum(i, nb[0] - 1), 0)

    def w_map(i, be, ne, nb):
        return (layer, be[i], 0, 0)

    return pl.pallas_call(
        _ffn_kernel,
        grid_spec=pltpu.PrefetchScalarGridSpec(
            num_scalar_prefetch=3,
            grid=(n_blk,),
            in_specs=[pl.BlockSpec((bm, dp), x_map),
                      pl.BlockSpec((1, 1, d, de), w_map),
                      pl.BlockSpec((1, 1, d, de), w_map),
                      pl.BlockSpec((1, 1, de, d), w_map)],
            out_specs=pl.BlockSpec((bm, dp), lambda i, be, ne, nb: (i, 0)),
            scratch_shapes=[pltpu.VMEM((d, de), BF16), pltpu.VMEM((d, de), BF16),
                            pltpu.VMEM((de, d), BF16)]),
        out_shape=jax.ShapeDtypeStruct((n_pad, dp), jnp.int32),
        compiler_params=_cparams(("arbitrary",)),
        name="moe_ffn",
    )(blk_exp, new_exp, nblk, xs, w1, w3, w2)


def _combine_kernel(x_ref, g0_ref, g1_ref, route_ref, ln_g_ref, ln_b_ref, o_ref):
    rec = route_ref[...]
    a_lo, a_hi = _unpack_halves(g0_ref[0])
    b_lo, b_hi = _unpack_halves(g1_ref[0])
    w0, w1 = rec[:, 2:3], rec[:, 3:4]
    y = jnp.concatenate([w0 * a_lo + w1 * b_lo, w0 * a_hi + w1 * b_hi], axis=1)
    o_ref[...] = _ln(DN_ALPHA * x_ref[...] + y, ln_g_ref[...], ln_b_ref[...])


def moe_combine(x, g, route, ln_g, ln_b, *, tb):
    n, d = x.shape
    row_spec = pl.BlockSpec((tb, d), lambda i: (i, 0))
    return pl.pallas_call(
        _combine_kernel,
        grid=(n // tb,),
        in_specs=[row_spec,
                  pl.BlockSpec((1, tb, d // 2), lambda i: (0, i, 0)),
                  pl.BlockSpec((1, tb, d // 2), lambda i: (1, i, 0)),
                  pl.BlockSpec((tb, ROUTE_W), lambda i: (i, 0)),
                  _full((1, d)), _full((1, d))],
        out_specs=row_spec,
        out_shape=jax.ShapeDtypeStruct((n, d), F32),
        compiler_params=_cparams(("arbitrary",)),
        name="moe_combine",
    )(x, g, g, route, ln_g.reshape(1, -1), ln_b.reshape(1, -1))


MOE_BM = 512
MIXER_TS = 512


def hierarchical_moe_layer(x2, xp, route, counts, w1, w3, w2, ln_g, ln_b, *, layer):
    bsz, seq, d = x2.shape
    n = bsz * seq
    bm = MOE_BM
    xf = x2.reshape(n, d)
    rt = route.reshape(n, ROUTE_W)
    n_blk = (2 * n) // bm + N_EXPERTS
    n_pad = n_blk * bm
    cnt = counts[0, :N_EXPERTS].astype(jnp.int32)
    pcnt = (cnt + bm - 1) // bm * bm
    pends = jnp.cumsum(pcnt)
    pstart = pends - pcnt
    nblk = (pends[-1] // bm).astype(jnp.int32).reshape(1)
    blk_row = jnp.arange(n_blk, dtype=jnp.int32) * bm
    blk_exp = jnp.minimum(jnp.sum((pends[None, :] <= blk_row[:, None]).astype(jnp.int32), axis=1),
                          N_EXPERTS - 1)
    last_exp = blk_exp[jnp.maximum(nblk[0] - 1, 0)]
    blk_exp = jnp.where(jnp.arange(n_blk) < nblk[0], blk_exp, last_exp)
    new_exp = jnp.concatenate([jnp.ones((1,), jnp.int32),
                               (blk_exp[1:] != blk_exp[:-1]).astype(jnp.int32)])
    pstart_rec = jnp.zeros((1, ROUTE_W), F32).at[0, :N_EXPERTS].set(pstart.astype(F32))
    dest = moe_slots(rt, pstart_rec, tb=min(512, n))
    xs = moe_dispatch(xp.reshape(n, d // 2), dest[0], dest[1], n_pad)
    ys = moe_ffn(xs, blk_exp, new_exp, nblk, w1, w3, w2, layer=layer, bm=bm)
    g = moe_gather(ys, dest[0], dest[1])
    out = moe_combine(xf, g, rt, ln_g, ln_b, tb=min(512, n))
    return out.reshape(bsz, seq, d)


SC_ROWS = 128


def _sc_mesh():
    return plsc.VectorSubcoreMesh(core_axis_name="c", subcore_axis_name="s")


def moe_dispatch(xf, dest0, dest1, n_pad):
    n, d = xf.shape
    info = plsc.get_sparse_core_info()
    nw = info.num_cores * info.num_subcores
    per_w = n // nw
    r = min(SC_ROWS, per_w)

    def body(x_hbm, d0_hbm, d1_hbm, xs_hbm, i0_v, i1_v, rows_v, sem):
        wid = lax.axis_index("s") * info.num_cores + lax.axis_index("c")

        @pl.loop(0, per_w // r)
        def _(c):
            base = pl.multiple_of(wid * per_w + c * r, 8)
            pltpu.sync_copy(d0_hbm.at[pl.ds(base, r)], i0_v)
            pltpu.sync_copy(d1_hbm.at[pl.ds(base, r)], i1_v)
            pltpu.sync_copy(x_hbm.at[pl.ds(base, r)], rows_v)
            pltpu.async_copy(rows_v, xs_hbm.at[i0_v], sem).wait()
            pltpu.async_copy(rows_v, xs_hbm.at[i1_v], sem).wait()

    return pl.kernel(
        body, out_type=jax.ShapeDtypeStruct((n_pad, d), xf.dtype), mesh=_sc_mesh(),
        scratch_types=[pltpu.VMEM((r,), jnp.int32), pltpu.VMEM((r,), jnp.int32),
                       pltpu.VMEM((r, d), xf.dtype), pltpu.SemaphoreType.DMA],
        name="moe_dispatch",
    )(xf, dest0, dest1)


def moe_gather(ys, dest0, dest1):
    n = dest0.shape[0]
    d = ys.shape[1]
    info = plsc.get_sparse_core_info()
    nw = info.num_cores * info.num_subcores
    per_w = n // nw
    r = min(SC_ROWS, per_w)

    def body(ys_hbm, d0_hbm, d1_hbm, g_hbm, i_v, rows_v, sem):
        wid = lax.axis_index("s") * info.num_cores + lax.axis_index("c")

        @pl.loop(0, per_w // r)
        def _(c):
            base = pl.multiple_of(wid * per_w + c * r, 8)
            for k, d_hbm in enumerate((d0_hbm, d1_hbm)):
                pltpu.sync_copy(d_hbm.at[pl.ds(base, r)], i_v)
                pltpu.async_copy(ys_hbm.at[i_v], rows_v, sem).wait()
                pltpu.sync_copy(rows_v, g_hbm.at[k, pl.ds(base, r)])

    return pl.kernel(
        body, out_type=jax.ShapeDtypeStruct((2, n, d), ys.dtype), mesh=_sc_mesh(),
        scratch_types=[pltpu.VMEM((r,), jnp.int32), pltpu.VMEM((r, d), ys.dtype),
                       pltpu.SemaphoreType.DMA],
        name="moe_gather",
    )(ys, dest0, dest1)


def kernel(x, mem, ln_g, ln_b, ev_w_in, ev_gm_ln_g, ev_gm_ln_b, ev_gm_ws, ev_gm_bs, ev_conv_w, ev_conv_b, ev_wq, ev_wk, ev_wv, ev_w_if, ev_b_if, ev_norm_w, ev_skip, ev_w_out, od_w_qkv, od_b_qkv, od_sinks, od_w_o, xa_wq, xa_wkv, xa_wo, moe_w_rg, moe_b_rg, moe_w_re, moe_b_re, moe_w1, moe_w3, moe_w2):
    bsz, seq, d = x.shape
    depth = ln_g.shape[0]
    ts = min(MIXER_TS, seq)
    for l in range(depth):
        if l % 2 == 0:
            e = l // 2
            x = even_mixer_layer(x, ev_w_in[e], ev_gm_ln_g[e], ev_gm_ln_b[e], ev_gm_ws[e], ev_gm_bs[e],
                                 ev_conv_w[e], ev_conv_b[e], ev_wq[e], ev_wk[e], ev_wv[e], ev_w_if[e],
                                 ev_b_if[e], ev_norm_w[e], ev_skip[e], ev_w_out[e],
                                 ln_g[l, 0], ln_b[l, 0], ts=ts)
        else:
            o = l // 2
            x = swa_mixer_layer(x, od_w_qkv[o], od_b_qkv[o], od_sinks[o], od_w_o[o],
                                ln_g[l, 0], ln_b[l, 0], ts=ts)
        wqk, vo = memory_fold(mem, xa_wkv[l], xa_wq[l], xa_wo[l])
        x, xp, route, counts = xattn_router_layer(x, wqk, vo, ln_g[l, 1], ln_b[l, 1],
                                                  moe_w_rg[l], moe_b_rg[l], moe_w_re[l], moe_b_re[l],
                                                  ts=min(XATTN_TS, seq))
        x = hierarchical_moe_layer(x, xp, route, counts, moe_w1, moe_w3, moe_w2,
                                   ln_g[l, 2], ln_b[l, 2], layer=l)
    return x
```

```python
import functools
import math

import jax
import jax.numpy as jnp
from jax import lax
from jax.experimental import pallas as pl
from jax.experimental.pallas import tpu as pltpu
from jax.experimental.pallas import tpu_sc as plsc

F32 = jnp.float32
BF16 = jnp.bfloat16

A_GROUPS = 4
CHUNK = 128
B_HEADS = 4
B_CONV = 4
C_HEAD_DIM = 64
C_KV_HEADS = 4
X_HEADS = 4
N_GROUPS = 4
EXPERTS_PER_GROUP = 8
N_EXPERTS = N_GROUPS * EXPERTS_PER_GROUP
ROPE_THETA = 10000.0
LN_EPS = 1e-5
DEPTH = 2
DN_ALPHA = (2 * DEPTH) ** 0.25

LANES = 128
VMEM_LIMIT = 48 * 1024 * 1024
NEG = -1e30


def _cparams(sem):
    return pltpu.CompilerParams(dimension_semantics=sem, vmem_limit_bytes=VMEM_LIMIT)


def _full(shape):
    nd = len(shape)
    return pl.BlockSpec(shape, lambda *_: (0,) * nd)


def _dot(a, b):
    return jnp.dot(a, b, preferred_element_type=F32)


def _dot_nt(a, b):
    return lax.dot_general(a, b, (((1,), (1,)), ((), ())), preferred_element_type=F32)


def _dot_tn(a, b):
    return lax.dot_general(a, b, (((0,), (0,)), ((), ())), preferred_element_type=F32)


def _split_dot(a, b_bf16):
    hi = a.astype(BF16)
    lo = (a - hi.astype(F32)).astype(BF16)
    return _dot(hi, b_bf16) + _dot(lo, b_bf16)


def _ln(x, g, b):
    mu = jnp.mean(x, axis=-1, keepdims=True)
    xc = x - mu
    var = jnp.mean(xc * xc, axis=-1, keepdims=True)
    return xc * lax.rsqrt(var + LN_EPS) * g + b


def _silu(x):
    return x * (1.0 / (1.0 + jnp.exp(-x)))


def _gelu(x):
    return 0.5 * x * (1.0 + jnp.tanh(math.sqrt(2.0 / math.pi) * (x + 0.044715 * (x * x * x))))


def _pack_halves(x):
    c = x.shape[1] // 2
    lo = lax.bitcast_convert_type(x[:, :c].astype(BF16).astype(F32), jnp.uint32)
    hi = lax.bitcast_convert_type(x[:, c:].astype(BF16).astype(F32), jnp.uint32)
    return lax.bitcast_convert_type((lo >> 16) | hi, jnp.int32)


def _unpack_halves(p):
    u = lax.bitcast_convert_type(p, jnp.uint32)
    lo = lax.bitcast_convert_type(u << 16, F32)
    hi = lax.bitcast_convert_type(u & jnp.uint32(0xFFFF0000), F32)
    return lo, hi


def _log_sigmoid(x):
    return jnp.minimum(x, 0.0) - jnp.log(1.0 + jnp.exp(-jnp.abs(x)))


def _even_kernel(x_ref, w_in_ref, gm_g_ref, gm_b_ref, gm_w_ref, gm_bias_ref,
                 conv_w_ref, conv_b_ref, wq_ref, wk_ref, wv_ref, wif_t_ref, bif_t_ref,
                 norm_w_ref, skip_ref, w_out_ref, ln_g_ref, ln_b_ref,
                 o_ref,
                 xm_buf, ct_ref, m_ref, *, ts, aw, bw):
    dh = bw // B_HEADS
    agd = aw // A_GROUPS
    nck = ts // CHUNK
    pad = 8
    j = pl.program_id(1)

    @pl.when(j == 0)
    def _():
        xm_buf[0:pad, :] = jnp.zeros((pad, bw), F32)
        ct_ref[...] = jnp.zeros_like(ct_ref)
        m_ref[...] = jnp.zeros_like(m_ref)

    row = lax.broadcasted_iota(jnp.int32, (CHUNK, CHUNK), 0)
    col = lax.broadcasted_iota(jnp.int32, (CHUNK, CHUNK), 1)
    causal = col <= row
    diag = col == row
    triu = jnp.where(row <= col, 1.0, 0.0).astype(BF16)
    ones_blk = jnp.ones((CHUNK, LANES), BF16)

    x = x_ref[0]
    proj = _dot(x.astype(BF16), w_in_ref[...])
    a_u = _gelu(proj[:, :aw])
    a_v = _gelu(proj[:, aw:2 * aw])
    xm = proj[:, 2 * aw:2 * aw + bw]
    z = proj[:, 2 * aw + bw:]

    vn = _ln(a_v, gm_g_ref[...], gm_b_ref[...]).astype(BF16)
    ya_chunks = []
    for c in range(nck):
        cols = []
        for g in range(A_GROUPS):
            v_cg = vn[c * CHUNK:(c + 1) * CHUNK, g * agd:(g + 1) * agd]
            cols.append(_dot(gm_w_ref[g], v_cg))
        ya_chunks.append(jnp.concatenate(cols, axis=1) + gm_bias_ref[...])
    y_a = a_u * jnp.concatenate(ya_chunks, axis=0)

    xm_buf[pad:pad + ts, :] = xm
    conv = conv_b_ref[...] + conv_w_ref[B_CONV - 1:B_CONV, :] * xm
    for k in range(B_CONV - 1):
        sh = B_CONV - 1 - k
        conv = conv + conv_w_ref[k:k + 1, :] * xm_buf[pad - sh:pad - sh + ts, :]
    xm_buf[pad - (B_CONV - 1):pad, :] = xm_buf[pad + ts - (B_CONV - 1):pad + ts, :]
    xc = _silu(conv)
    xc_b = xc.astype(BF16)
    q = _dot_head_pairs(xc_b, wq_ref)
    k_ = _dot_head_pairs(xc_b, wk_ref)
    v = _dot_head_pairs(xm.astype(BF16), wv_ref)
    gate_in = jnp.concatenate([q, k_, v], axis=1).astype(BF16)
    gates_t = _dot_nt(wif_t_ref[...], gate_in) + bif_t_ref[...]
    ig_all = gates_t[:B_HEADS, :]
    lf_all = _log_sigmoid(gates_t[B_HEADS:, :])
    q_b = q.astype(BF16)
    k_b = (k_ * dh ** -0.5).astype(BF16)
    v_b = v.astype(BF16)
    gate_z = _silu(z)

    h_chunks = []
    for c in range(nck):
        sl = slice(c * CHUNK, (c + 1) * CHUNK)
        lf_r = lf_all[:, sl]
        b_r = _split_dot(lf_r, triu)
        a_r = ig_all[:, sl] - b_r
        heads = []
        for h in range(B_HEADS):
            hs = slice(h * dh, (h + 1) * dh)
            qh, kh = q_b[sl, hs], k_b[sl, hs]
            vaug = jnp.concatenate([v_b[sl, hs], ones_blk], axis=1)
            a_row = a_r[h:h + 1, :]
            m_row = m_ref[h:h + 1, :]
            amat = jnp.where(causal, a_row, NEG)
            mx = jnp.maximum(jnp.max(amat, axis=1, keepdims=True), m_row)
            w_intra = jnp.exp(amat - mx)
            w_state = jnp.exp(m_row - mx)
            s = _dot_nt(qh, kh) * w_intra
            caug = ct_ref[h]
            naug = (_dot(s.astype(BF16), vaug)
                    + jnp.concatenate([w_state, w_state], axis=1) * _dot(qh, caug.astype(BF16)))
            num, nq = naug[:, :dh], naug[:, dh:]
            b_t = _split_dot(jnp.where(causal, lf_r[h:h + 1, :], 0.0), ones_blk)
            hv = num / jnp.maximum(jnp.abs(nq), jnp.exp(-(b_t + mx)))
            bl = b_r[h:h + 1, CHUNK - 1:CHUNK]
            g_row = bl + a_row
            m_new = jnp.maximum(bl + m_row, jnp.max(g_row, axis=1, keepdims=True))
            decay = jnp.exp(bl + m_row - m_new)
            wg_row = jnp.exp(g_row - m_new)
            wg_t = _dot(jnp.where(diag, wg_row, 0.0).astype(BF16), ones_blk)
            kw = (kh.astype(F32) * wg_t).astype(BF16)
            ct_ref[h] = jnp.concatenate([decay, decay], axis=1) * caug + _dot_tn(kw, vaug)
            m_ref[h:h + 1, :] = m_new
            hc = hv - jnp.mean(hv, axis=1, keepdims=True)
            hn = hc * lax.rsqrt(jnp.mean(hc * hc, axis=1, keepdims=True) + LN_EPS)
            heads.append(hn)
        h_chunks.append(jnp.concatenate(heads, axis=1))
    hn_all = jnp.concatenate(h_chunks, axis=0) if nck > 1 else h_chunks[0]
    y_b = (hn_all * norm_w_ref[...] + skip_ref[...] * xc) * gate_z

    mix = jnp.concatenate([y_a, y_b], axis=1).astype(BF16)
    y = _dot(mix, w_out_ref[...])
    o_ref[0] = _ln(DN_ALPHA * x + y, ln_g_ref[...], ln_b_ref[...])


def _split_dot_left(a_bf16, b):
    hi = b.astype(BF16)
    lo = (b - hi.astype(F32)).astype(BF16)
    return _dot(a_bf16, hi) + _dot(a_bf16, lo)


def _block_diag_pairs(w):
    hh, d, _ = w.shape
    wp = w.reshape(hh // 2, 2, d, d)
    eye = jnp.eye(2, dtype=w.dtype)
    return jnp.einsum('pade,ab->padbe', wp, eye).reshape(hh // 2, 2 * d, 2 * d)


def _dot_head_pairs(x, w_ref):
    npair, w2, _ = w_ref.shape
    return jnp.concatenate([_dot(x[:, p * w2:(p + 1) * w2], w_ref[p]) for p in range(npair)], axis=1)


def even_mixer_layer(x, w_in, gm_ln_g, gm_ln_b, gm_ws, gm_bs, conv_w, conv_b, wq, wk, wv, w_if,
                     b_if, norm_w, skip, w_out, ln_g, ln_b, *, ts):
    bsz, seq, d = x.shape
    aw = gm_ln_g.shape[0]
    bw = conv_b.shape[0]
    agd = aw // A_GROUPS
    causal = jnp.tril(jnp.ones((CHUNK, CHUNK), dtype=bool))
    gm_w = jnp.where(causal[None], gm_ws, 0.0).astype(BF16)
    gm_bias = jnp.repeat(gm_bs.T, agd, axis=1)
    row = lambda a: a.reshape(1, -1)
    args = (x, w_in.astype(BF16), row(gm_ln_g), row(gm_ln_b), gm_w, gm_bias,
            conv_w, row(conv_b), _block_diag_pairs(wq).astype(BF16), _block_diag_pairs(wk).astype(BF16),
            _block_diag_pairs(wv).astype(BF16), w_if.T.astype(BF16),
            b_if.reshape(-1, 1), row(norm_w), row(skip), w_out.astype(BF16),
            row(ln_g), row(ln_b))
    in_specs = [pl.BlockSpec((1, ts, d), lambda b, j: (b, j, 0))] + [_full(a.shape) for a in args[1:]]
    dh = bw // B_HEADS
    assert dh == LANES and CHUNK == LANES
    return pl.pallas_call(
        functools.partial(_even_kernel, ts=ts, aw=aw, bw=bw),
        grid=(bsz, seq // ts),
        in_specs=in_specs,
        out_specs=pl.BlockSpec((1, ts, d), lambda b, j: (b, j, 0)),
        out_shape=jax.ShapeDtypeStruct((bsz, seq, d), F32),
        scratch_shapes=[pltpu.VMEM((8 + ts, bw), F32),
                        pltpu.VMEM((B_HEADS, dh, dh + LANES), F32),
                        pltpu.VMEM((B_HEADS, LANES), F32)],
        compiler_params=_cparams(("arbitrary", "arbitrary")),
        name="even_mixer",
    )(*args)


def _memfold_kernel(mem_ref, wkv_ref, wq_ref, wo_ref, wqk_ref, vo_ref, *, d):
    dh = d // X_HEADS
    m_len = mem_ref.shape[1]
    kv = _dot(mem_ref[0].astype(BF16), wkv_ref[...])
    k = (kv[:, :d] * dh ** -0.5).astype(BF16)
    v = kv[:, d:].astype(BF16)
    for h in range(X_HEADS):
        hs = slice(h * dh, (h + 1) * dh)
        ms = slice(h * m_len, (h + 1) * m_len)
        wqk_ref[0, :, ms] = _dot_nt(wq_ref[:, hs], k[:, hs]).astype(BF16)
        vo_ref[0, ms, :] = _dot(v[:, hs], wo_ref[hs, :]).astype(BF16)


def memory_fold(mem, wkv, wq, wo):
    bsz, m_len, d = mem.shape
    hm = X_HEADS * m_len
    args = (mem, wkv.astype(BF16), wq.astype(BF16), wo.astype(BF16))
    return pl.pallas_call(
        functools.partial(_memfold_kernel, d=d),
        grid=(bsz,),
        in_specs=[pl.BlockSpec((1, m_len, d), lambda b: (b, 0, 0))] + [_full(a.shape) for a in args[1:]],
        out_specs=[pl.BlockSpec((1, d, hm), lambda b: (b, 0, 0)),
                   pl.BlockSpec((1, hm, d), lambda b: (b, 0, 0))],
        out_shape=[jax.ShapeDtypeStruct((bsz, d, hm), BF16),
                   jax.ShapeDtypeStruct((bsz, hm, d), BF16)],
        compiler_params=_cparams(("arbitrary",)),
        name="memory_fold",
    )(*args)


ROUTE_W = 128
XATTN_TS = 1024
XATTN_GROUP = 1024
DEST_ROWS = 8


def _xattn_kernel(x_ref, wqk_ref, vo_ref, ln_g_ref, ln_b_ref, wr_ref, br_ref,
                  o_ref, xp_ref, route_ref, cnt_ref):
    m_len = wqk_ref.shape[2] // X_HEADS
    @pl.when(pl.program_id(1) == 0)
    def _():
        cnt_ref[...] = jnp.zeros_like(cnt_ref)

    ts = XATTN_GROUP if x_ref.shape[1] % XATTN_GROUP == 0 else x_ref.shape[1]
    ngroups = x_ref.shape[1] // ts
    lane = lax.broadcasted_iota(jnp.int32, (ts, ROUTE_W), 1)
    is_g = lane < N_GROUPS
    st = [dict() for _ in range(ngroups)]

    def stage_q(g):
        x = x_ref[0, g * ts:(g + 1) * ts, :]
        st[g]["x"] = x
        st[g]["s"] = _dot(x.astype(BF16), wqk_ref[0])

    def stage_attn(g):
        s_all = st[g].pop("s")
        probs = []
        for h in range(X_HEADS):
            s = s_all[:, h * m_len:(h + 1) * m_len]
            p = jnp.exp(s - jnp.max(s, axis=1, keepdims=True))
            probs.append((p / jnp.sum(p, axis=1, keepdims=True)).astype(BF16))
        st[g]["p"] = jnp.concatenate(probs, axis=1)

    def stage_out(g):
        y = _dot(st[g].pop("p"), vo_ref[0])
        x2 = _ln(DN_ALPHA * st[g].pop("x") + y, ln_g_ref[...], ln_b_ref[...])
        rows = slice(g * ts, (g + 1) * ts)
        o_ref[0, rows, :] = x2
        xp_ref[0, rows, :] = _pack_halves(x2)
        x_hi = x2.astype(BF16)
        x_lo = (x2 - x_hi.astype(F32)).astype(BF16)
        st[g]["logits"] = (_dot(x_hi, wr_ref[0]) + _dot(x_lo, wr_ref[0]) + _dot(x_hi, wr_ref[1])) + br_ref[...]

    def stage_route(g):
        logits = st[g].pop("logits")
        lg = jnp.where(is_g, logits, NEG)
        mg = jnp.max(lg, axis=1, keepdims=True)
        gi = jnp.min(jnp.where(jnp.logical_and(is_g, lg == mg), lane, ROUTE_W), axis=1, keepdims=True)
        gate_g = 1.0 / jnp.sum(jnp.where(is_g, jnp.exp(lg - mg), 0.0), axis=1, keepdims=True)
        lo = N_GROUPS + gi * EXPERTS_PER_GROUP
        in_grp = jnp.logical_and(lane >= lo, lane < lo + EXPERTS_PER_GROUP)
        le = jnp.where(in_grp, logits, NEG)
        v1 = jnp.max(le, axis=1, keepdims=True)
        i1 = jnp.min(jnp.where(jnp.logical_and(in_grp, le == v1), lane, ROUTE_W), axis=1, keepdims=True)
        le2 = jnp.where(lane == i1, NEG, le)
        v2 = jnp.max(le2, axis=1, keepdims=True)
        i2 = jnp.min(jnp.where(jnp.logical_and(in_grp, le2 == v2), lane, ROUTE_W), axis=1, keepdims=True)
        e21 = jnp.exp(v2 - v1)
        p1 = 1.0 / (1.0 + e21)
        p2 = e21 * p1
        e1 = (i1 - N_GROUPS).astype(F32)
        e2 = (i2 - N_GROUPS).astype(F32)
        rec = jnp.where(lane == 0, e1, 0.0)
        rec = jnp.where(lane == 1, e2, rec)
        rec = jnp.where(lane == 2, gate_g * p1, rec)
        rec = jnp.where(lane == 3, gate_g * p2, rec)
        route_ref[0, g * ts:(g + 1) * ts, :] = rec
        sel = jnp.logical_or(lane == i1 - N_GROUPS, lane == i2 - N_GROUPS)
        st[g]["counts"] = jnp.sum(jnp.where(sel, 1.0, 0.0), axis=0, keepdims=True)

    stages = (stage_q, stage_attn, stage_out, stage_route)
    for t in range(ngroups + len(stages) - 1):
        for k, stage in enumerate(stages):
            if 0 <= t - k < ngroups:
                stage(t - k)
    counts = st[0]["counts"]
    for g in range(1, ngroups):
        counts = counts + st[g]["counts"]
    cnt_ref[...] += counts


def xattn_router_layer(x, wqk, vo, ln_g, ln_b, w_rg, b_rg, w_re, b_re, *, ts):
    bsz, seq, d = x.shape
    hm = wqk.shape[2]
    wr = jnp.zeros((d, ROUTE_W), F32).at[:, :N_GROUPS].set(w_rg).at[:, N_GROUPS:N_GROUPS + N_EXPERTS].set(w_re)
    br = jnp.zeros((1, ROUTE_W), F32).at[0, :N_GROUPS].set(b_rg).at[0, N_GROUPS:N_GROUPS + N_EXPERTS].set(b_re)
    wr_hi = wr.astype(BF16)
    wr_lo = (wr - wr_hi.astype(F32)).astype(BF16)
    wr = jnp.stack([wr_hi, wr_lo])
    args = (x, wqk, vo, ln_g.reshape(1, -1), ln_b.reshape(1, -1), wr, br)
    in_specs = [pl.BlockSpec((1, ts, d), lambda b, j: (b, j, 0)),
                pl.BlockSpec((1, d, hm), lambda b, j: (b, 0, 0)),
                pl.BlockSpec((1, hm, d), lambda b, j: (b, 0, 0))] + [_full(a.shape) for a in args[3:]]
    return pl.pallas_call(
        _xattn_kernel,
        grid=(bsz, seq // ts),
        in_specs=in_specs,
        out_specs=[pl.BlockSpec((1, ts, d), lambda b, j: (b, j, 0)),
                   pl.BlockSpec((1, ts, d // 2), lambda b, j: (b, j, 0)),
                   pl.BlockSpec((1, ts, ROUTE_W), lambda b, j: (b, j, 0)),
                   pl.BlockSpec((1, 1, ROUTE_W), lambda b, j: (b, 0, 0))],
        out_shape=[jax.ShapeDtypeStruct((bsz, seq, d), F32),
                   jax.ShapeDtypeStruct((bsz, seq, d // 2), jnp.int32),
                   jax.ShapeDtypeStruct((bsz, seq, ROUTE_W), F32),
                   jax.ShapeDtypeStruct((bsz, 1, ROUTE_W), F32)],
        compiler_params=_cparams(("arbitrary", "arbitrary")),
        name="xattn_router",
    )(*args)


def _swa_kernel(x_ref, wqkv_ref, bqkv_ref, cos_ref, sin_ref, sink_ref, wo_ref, ln_g_ref, ln_b_ref,
                o_ref, kprev, vprev, *, ts, cq, ckv):
    j = pl.program_id(1)
    nb = ts // CHUNK
    dh = C_HEAD_DIM
    grp = (cq // dh) // C_KV_HEADS

    @pl.when(j == 0)
    def _():
        kprev[...] = jnp.zeros_like(kprev)
        vprev[...] = jnp.zeros_like(vprev)

    x = x_ref[0]
    qkv = _dot(x.astype(BF16), wqkv_ref[...]) + bqkv_ref[...]
    cos = cos_ref[...]
    sin = sin_ref[...]
    lane = lax.broadcasted_iota(jnp.int32, (ts, LANES), 1)
    first_half = (lane % dh) < (dh // 2)

    def rope(t):
        outs = []
        for c in range(t.shape[1] // LANES):
            tc = t[:, c * LANES:(c + 1) * LANES]
            rot = jnp.where(first_half, pltpu.roll(tc, LANES - dh // 2, 1), pltpu.roll(tc, dh // 2, 1))
            outs.append(tc * cos + rot * sin)
        return jnp.concatenate(outs, axis=1)

    q = (rope(qkv[:, :cq]) * dh ** -0.5).astype(BF16)
    k = rope(qkv[:, cq:cq + 2 * ckv]).astype(BF16)
    v = qkv[:, cq + 2 * ckv:].astype(BF16)

    r_i = lax.broadcasted_iota(jnp.int32, (CHUNK, 2 * CHUNK), 0)
    c_i = lax.broadcasted_iota(jnp.int32, (CHUNK, 2 * CHUNK), 1)
    band = jnp.logical_and(c_i > r_i, c_i <= r_i + CHUNK)
    sink_col = c_i == 0
    lane_k = lax.broadcasted_iota(jnp.int32, (2 * CHUNK, LANES), 1)
    key_row = lax.broadcasted_iota(jnp.int32, (2 * CHUNK, LANES), 0)
    lane_q = lax.broadcasted_iota(jnp.int32, (CHUNK, LANES), 1)
    ones_blk = jnp.ones((2 * CHUNK, LANES), BF16)
    zero_b = jnp.zeros((), BF16)
    blocks = []
    for c in range(nb):
        sl = slice(c * CHUNK, (c + 1) * CHUNK)
        if c == 0:
            kb = jnp.concatenate([kprev[...].astype(BF16), k[sl]], axis=0)
            vb = jnp.concatenate([vprev[...].astype(BF16), v[sl]], axis=0)
            first_key = jnp.where(j > 0, 0, CHUNK)
            valid = jnp.logical_and(band, c_i >= first_key)
        else:
            kb = k[(c - 1) * CHUNK:(c + 1) * CHUNK]
            vb = v[(c - 1) * CHUNK:(c + 1) * CHUNK]
            valid = band
        tiles = []
        for h in range(C_KV_HEADS):
            kd = kb[:, h * LANES:(h + 1) * LANES]
            vd = vb[:, h * LANES:(h + 1) * LANES]
            k_lo = jnp.where(lane_k < dh, kd, zero_b)
            k_hi = jnp.where(lane_k >= dh, kd, zero_b)
            vz = jnp.where(key_row == 0, zero_b, vd)
            q2 = jnp.concatenate([q[sl, (2 * h) * LANES:(2 * h + 1) * LANES],
                                  q[sl, (2 * h + 1) * LANES:(2 * h + 2) * LANES]], axis=0)
            s_lo = _dot_nt(q2, k_lo)
            s_hi = _dot_nt(q2, k_hi)
            parts = []
            for qk, g in ((s_lo[:CHUNK], 0), (s_lo[CHUNK:], 2), (s_hi[:CHUNK], 1), (s_hi[CHUNK:], 3)):
                sink = sink_ref[h * grp + g]
                parts.append(jnp.where(valid, qk, jnp.where(sink_col, sink, NEG)))
            s = jnp.concatenate(parts, axis=0)
            p = jnp.exp(s - jnp.max(s, axis=1, keepdims=True)).astype(BF16)
            den = _dot(p, ones_blk)
            o2 = _dot(p, vz) / den
            tiles.append(jnp.where(lane_q < dh, o2[:CHUNK], o2[2 * CHUNK:3 * CHUNK]))
            tiles.append(jnp.where(lane_q < dh, o2[CHUNK:2 * CHUNK], o2[3 * CHUNK:]))
        blocks.append(jnp.concatenate(tiles, axis=1))
    kprev[...] = k[(nb - 1) * CHUNK:].astype(F32)
    vprev[...] = v[(nb - 1) * CHUNK:].astype(F32)
    att = (jnp.concatenate(blocks, axis=0) if nb > 1 else blocks[0]).astype(BF16)
    y = _dot(att, wo_ref[...])
    o_ref[0] = _ln(DN_ALPHA * x + y, ln_g_ref[...], ln_b_ref[...])


def swa_mixer_layer(x, w_qkv, b_qkv, sinks, w_o, ln_g, ln_b, *, ts):
    bsz, seq, d = x.shape
    cq = w_o.shape[0]
    ckv = (w_qkv.shape[1] - cq) // 2
    dh = C_HEAD_DIM
    inv = ROPE_THETA ** (-jnp.arange(0, dh, 2, dtype=F32) / dh)
    ang = jnp.arange(seq, dtype=F32)[:, None] * inv[None, :]
    reps = LANES // (dh // 2)
    sign = jnp.tile(jnp.concatenate([-jnp.ones((dh // 2,), F32), jnp.ones((dh // 2,), F32)]), LANES // dh)
    cos_t = jnp.tile(jnp.cos(ang), (1, reps))
    sin_t = jnp.tile(jnp.sin(ang), (1, reps)) * sign[None, :]
    assert 2 * dh == LANES

    def dup_heads(t):
        th = t.reshape(t.shape[:-1] + (ckv // dh, dh))
        return jnp.concatenate([th, th], axis=-1).reshape(t.shape[:-1] + (2 * ckv,))

    w_all = jnp.concatenate([w_qkv[:, :cq], dup_heads(w_qkv[:, cq:cq + ckv]),
                             dup_heads(w_qkv[:, cq + ckv:])], axis=1)
    b_all = jnp.concatenate([b_qkv[:cq], dup_heads(b_qkv[cq:cq + ckv]), dup_heads(b_qkv[cq + ckv:])])
    args = (x, w_all.astype(BF16), b_all.reshape(1, -1), cos_t, sin_t, sinks.astype(F32),
            w_o.astype(BF16), ln_g.reshape(1, -1), ln_b.reshape(1, -1))
    in_specs = [pl.BlockSpec((1, ts, d), lambda b, j: (b, j, 0)),
                _full(args[1].shape), _full(args[2].shape),
                pl.BlockSpec((ts, LANES), lambda b, j: (j, 0)),
                pl.BlockSpec((ts, LANES), lambda b, j: (j, 0)),
                pl.BlockSpec(memory_space=pltpu.SMEM),
                _full(args[6].shape), _full(args[7].shape), _full(args[8].shape)]
    return pl.pallas_call(
        functools.partial(_swa_kernel, ts=ts, cq=cq, ckv=ckv),
        grid=(bsz, seq // ts),
        in_specs=in_specs,
        out_specs=pl.BlockSpec((1, ts, d), lambda b, j: (b, j, 0)),
        out_shape=jax.ShapeDtypeStruct((bsz, seq, d), F32),
        scratch_shapes=[pltpu.VMEM((CHUNK, 2 * ckv), F32), pltpu.VMEM((CHUNK, 2 * ckv), F32)],
        compiler_params=_cparams(("arbitrary", "arbitrary")),
        name="swa_mixer",
    )(*args)


def _slot_kernel(route_ref, pstart_ref, dest_ref, carry_ref, *, tb):
    @pl.when(pl.program_id(0) == 0)
    def _():
        carry_ref[...] = pstart_ref[...]

    rec = route_ref[...]
    lane = lax.broadcasted_iota(jnp.int32, (tb, ROUTE_W), 1)
    e0 = rec[:, 0:1].astype(jnp.int32)
    e1 = rec[:, 1:2].astype(jnp.int32)
    oh0 = lane == e0
    oh1 = lane == e1
    ohs = jnp.where(jnp.logical_or(oh0, oh1), 1.0, 0.0)
    r = lax.broadcasted_iota(jnp.int32, (tb, tb), 0)
    c = lax.broadcasted_iota(jnp.int32, (tb, tb), 1)
    before = jnp.where(c < r, 1.0, 0.0).astype(BF16)
    prefix = _dot(before, ohs.astype(BF16)) + carry_ref[...]
    d0 = jnp.sum(jnp.where(oh0, prefix, 0.0), axis=1, keepdims=True)
    d1 = jnp.sum(jnp.where(oh1, prefix, 0.0), axis=1, keepdims=True)
    dest = jnp.where(lane == 0, d0, jnp.where(lane == 1, d1, 0.0))
    dest_ref[...] = dest.T[:DEST_ROWS].astype(jnp.int32)
    carry_ref[...] += jnp.sum(ohs, axis=0, keepdims=True)


def moe_slots(route, pstart, *, row0, n, tb):
    blk0 = row0 // tb
    return pl.pallas_call(
        functools.partial(_slot_kernel, tb=tb),
        grid=(n // tb,),
        in_specs=[pl.BlockSpec((tb, ROUTE_W), lambda i: (blk0 + i, 0)), _full((1, ROUTE_W))],
        out_specs=pl.BlockSpec((DEST_ROWS, tb), lambda i: (0, i)),
        out_shape=jax.ShapeDtypeStruct((DEST_ROWS, n), jnp.int32),
        scratch_shapes=[pltpu.VMEM((1, ROUTE_W), F32)],
        compiler_params=_cparams(("arbitrary",)),
        name="moe_slots",
    )(route, pstart)


def _ffn_kernel(blk_exp_ref, new_exp_ref, nblk_ref, xs_ref, w1_ref, w3_ref, w2_ref, ys_ref,
                w1_b, w3_b, w2_b):
    i = pl.program_id(0)
    used = i < nblk_ref[0]

    @pl.when(jnp.logical_and(used, new_exp_ref[i] == 1))
    def _():
        w1_b[...] = w1_ref[0, 0].astype(BF16)
        w3_b[...] = w3_ref[0, 0].astype(BF16)
        w2_b[...] = w2_ref[0, 0].astype(BF16)

    @pl.when(used)
    def _():
        x_lo, x_hi = _unpack_halves(xs_ref[...])
        x_lo, x_hi = x_lo.astype(BF16), x_hi.astype(BF16)
        dl = x_lo.shape[1]
        h1 = _dot(x_lo, w1_b[:dl, :]) + _dot(x_hi, w1_b[dl:, :])
        h3 = _dot(x_lo, w3_b[:dl, :]) + _dot(x_hi, w3_b[dl:, :])
        h = (_silu(h1) * h3).astype(BF16)
        ys_ref[...] = _pack_halves(_dot(h, w2_b[...]))

    @pl.when(jnp.logical_not(used))
    def _():
        ys_ref[...] = jnp.zeros_like(ys_ref)


def moe_ffn(xs, blk_exp, new_exp, nblk, w1, w3, w2, *, layer, bm):
    n_pad, dp = xs.shape
    d = 2 * dp
    de = w1.shape[3]
    n_blk = n_pad // bm

    def x_map(i, be, ne, nb):
        return (jnp.minimum(i, nb[0] - 1), 0)

    def w_map(i, be, ne, nb):
        return (layer, be[i], 0, 0)

    return pl.pallas_call(
        _ffn_kernel,
        grid_spec=pltpu.PrefetchScalarGridSpec(
            num_scalar_prefetch=3,
            grid=(n_blk,),
            in_specs=[pl.BlockSpec((bm, dp), x_map),
                      pl.BlockSpec((1, 1, d, de), w_map),
                      pl.BlockSpec((1, 1, d, de), w_map),
                      pl.BlockSpec((1, 1, de, d), w_map)],
            out_specs=pl.BlockSpec((bm, dp), lambda i, be, ne, nb: (i, 0)),
            scratch_shapes=[pltpu.VMEM((d, de), BF16), pltpu.VMEM((d, de), BF16),
                            pltpu.VMEM((de, d), BF16)]),
        out_shape=jax.ShapeDtypeStruct((n_pad, dp), jnp.int32),
        compiler_params=_cparams(("arbitrary",)),
        name="moe_ffn",
    )(blk_exp, new_exp, nblk, xs, w1, w3, w2)


def _combine_kernel(x_ref, g0_ref, g1_ref, route_ref, ln_g_ref, ln_b_ref, *rest):
    o_ref = rest[-1]
    rec = route_ref[...]
    a_lo, a_hi = _unpack_halves(g0_ref[0])
    b_lo, b_hi = _unpack_halves(g1_ref[0])
    w0, w1 = rec[:, 2:3], rec[:, 3:4]
    y = jnp.concatenate([w0 * a_lo + w1 * b_lo, w0 * a_hi + w1 * b_hi], axis=1)
    o_ref[...] = _ln(DN_ALPHA * x_ref[...] + y, ln_g_ref[...], ln_b_ref[...])


def moe_combine(x, g, route, ln_g, ln_b, *, row0, tb, prev=None):
    n, d = x.shape
    m = g.shape[1]
    blk0 = row0 // tb
    row_spec = pl.BlockSpec((tb, d), lambda i: (blk0 + i, 0))
    args = [x, g, g, route, ln_g.reshape(1, -1), ln_b.reshape(1, -1)]
    in_specs = [row_spec,
                pl.BlockSpec((1, tb, d // 2), lambda i: (0, i, 0)),
                pl.BlockSpec((1, tb, d // 2), lambda i: (1, i, 0)),
                pl.BlockSpec((tb, ROUTE_W), lambda i: (blk0 + i, 0)),
                _full((1, d)), _full((1, d))]
    aliases = {}
    if prev is not None:
        args.append(prev)
        in_specs.append(pl.BlockSpec(memory_space=pl.ANY))
        aliases = {len(args) - 1: 0}
    return pl.pallas_call(
        _combine_kernel,
        grid=(m // tb,),
        in_specs=in_specs,
        out_specs=row_spec,
        out_shape=jax.ShapeDtypeStruct((n, d), F32),
        input_output_aliases=aliases,
        compiler_params=_cparams(("arbitrary",)),
        name="moe_combine",
    )(*args)


MOE_BM = 512
MOE_PARTS = 2
MIXER_TS = 512


def hierarchical_moe_layer(x2, xp, route, counts, w1, w3, w2, ln_g, ln_b, *, layer):
    bsz, seq, d = x2.shape
    n = bsz * seq
    bm = MOE_BM
    xf = x2.reshape(n, d)
    xpf = xp.reshape(n, d // 2)
    rt = route.reshape(n, ROUTE_W)
    parts = MOE_PARTS if bsz % MOE_PARTS == 0 else 1
    m = n // parts
    tb = min(512, m)
    n_blk = (2 * m) // bm + N_EXPERTS
    cnt_parts = counts.reshape(parts, bsz // parts, ROUTE_W).sum(axis=1)[:, :N_EXPERTS].astype(jnp.int32)
    out = None
    for part in range(parts):
        row0 = part * m
        pcnt = (cnt_parts[part] + bm - 1) // bm * bm
        pends = jnp.cumsum(pcnt)
        pstart = pends - pcnt
        nblk = (pends[-1] // bm).astype(jnp.int32).reshape(1)
        blk_row = jnp.arange(n_blk, dtype=jnp.int32) * bm
        blk_exp = jnp.minimum(jnp.sum((pends[None, :] <= blk_row[:, None]).astype(jnp.int32), axis=1),
                              N_EXPERTS - 1)
        last_exp = blk_exp[jnp.maximum(nblk[0] - 1, 0)]
        blk_exp = jnp.where(jnp.arange(n_blk) < nblk[0], blk_exp, last_exp)
        new_exp = jnp.concatenate([jnp.ones((1,), jnp.int32),
                                   (blk_exp[1:] != blk_exp[:-1]).astype(jnp.int32)])
        pstart_rec = jnp.zeros((1, ROUTE_W), F32).at[0, :N_EXPERTS].set(pstart.astype(F32))
        dest = moe_slots(rt, pstart_rec, row0=row0, n=m, tb=tb)
        xs = moe_dispatch(xpf, dest[0], dest[1], n_blk * bm, row0=row0)
        ys = moe_ffn(xs, blk_exp, new_exp, nblk, w1, w3, w2, layer=layer, bm=bm)
        g = moe_gather(ys, dest[0], dest[1])
        out = moe_combine(xf, g, rt, ln_g, ln_b, row0=row0, tb=tb, prev=out)
    return out.reshape(bsz, seq, d)


SC_ROWS = 128


def _sc_mesh():
    return plsc.VectorSubcoreMesh(core_axis_name="c", subcore_axis_name="s")


def moe_dispatch(xf, dest0, dest1, n_pad, *, row0):
    n = dest0.shape[0]
    d = xf.shape[1]
    info = plsc.get_sparse_core_info()
    nw = info.num_cores * info.num_subcores
    per_w = n // nw
    r = min(SC_ROWS, per_w)

    def body(x_hbm, d0_hbm, d1_hbm, xs_hbm, i0_v, i1_v, rows_v, sem):
        wid = lax.axis_index("s") * info.num_cores + lax.axis_index("c")

        @pl.loop(0, per_w // r)
        def _(c):
            base = pl.multiple_of(wid * per_w + c * r, 8)
            pltpu.sync_copy(d0_hbm.at[pl.ds(base, r)], i0_v)
            pltpu.sync_copy(d1_hbm.at[pl.ds(base, r)], i1_v)
            pltpu.sync_copy(x_hbm.at[pl.ds(row0 + base, r)], rows_v)
            pltpu.async_copy(rows_v, xs_hbm.at[i0_v], sem).wait()
            pltpu.async_copy(rows_v, xs_hbm.at[i1_v], sem).wait()

    return pl.kernel(
        body, out_type=jax.ShapeDtypeStruct((n_pad, d), xf.dtype), mesh=_sc_mesh(),
        scratch_types=[pltpu.VMEM((r,), jnp.int32), pltpu.VMEM((r,), jnp.int32),
                       pltpu.VMEM((r, d), xf.dtype), pltpu.SemaphoreType.DMA],
        name="moe_dispatch",
    )(xf, dest0, dest1)


def moe_gather(ys, dest0, dest1):
    n = dest0.shape[0]
    d = ys.shape[1]
    info = plsc.get_sparse_core_info()
    nw = info.num_cores * info.num_subcores
    per_w = n // nw
    r = min(SC_ROWS, per_w)

    def body(ys_hbm, d0_hbm, d1_hbm, g_hbm, i_v, rows_v, sem):
        wid = lax.axis_index("s") * info.num_cores + lax.axis_index("c")

        @pl.loop(0, per_w // r)
        def _(c):
            base = pl.multiple_of(wid * per_w + c * r, 8)
            for k, d_hbm in enumerate((d0_hbm, d1_hbm)):
                pltpu.sync_copy(d_hbm.at[pl.ds(base, r)], i_v)
                pltpu.async_copy(ys_hbm.at[i_v], rows_v, sem).wait()
                pltpu.sync_copy(rows_v, g_hbm.at[k, pl.ds(base, r)])

    return pl.kernel(
        body, out_type=jax.ShapeDtypeStruct((2, n, d), ys.dtype), mesh=_sc_mesh(),
        scratch_types=[pltpu.VMEM((r,), jnp.int32), pltpu.VMEM((r, d), ys.dtype),
                       pltpu.SemaphoreType.DMA],
        name="moe_gather",
    )(ys, dest0, dest1)


def kernel(x, mem, ln_g, ln_b, ev_w_in, ev_gm_ln_g, ev_gm_ln_b, ev_gm_ws, ev_gm_bs, ev_conv_w, ev_conv_b, ev_wq, ev_wk, ev_wv, ev_w_if, ev_b_if, ev_norm_w, ev_skip, ev_w_out, od_w_qkv, od_b_qkv, od_sinks, od_w_o, xa_wq, xa_wkv, xa_wo, moe_w_rg, moe_b_rg, moe_w_re, moe_b_re, moe_w1, moe_w3, moe_w2):
    bsz, seq, d = x.shape
    depth = ln_g.shape[0]
    ts = min(MIXER_TS, seq)
    for l in range(depth):
        if l % 2 == 0:
            e = l // 2
            x = even_mixer_layer(x, ev_w_in[e], ev_gm_ln_g[e], ev_gm_ln_b[e], ev_gm_ws[e], ev_gm_bs[e],
                                 ev_conv_w[e], ev_conv_b[e], ev_wq[e], ev_wk[e], ev_wv[e], ev_w_if[e],
                                 ev_b_if[e], ev_norm_w[e], ev_skip[e], ev_w_out[e],
                                 ln_g[l, 0], ln_b[l, 0], ts=ts)
        else:
            o = l // 2
            x = swa_mixer_layer(x, od_w_qkv[o], od_b_qkv[o], od_sinks[o], od_w_o[o],
                                ln_g[l, 0], ln_b[l, 0], ts=ts)
        wqk, vo = memory_fold(mem, xa_wkv[l], xa_wq[l], xa_wo[l])
        x, xp, route, counts = xattn_router_layer(x, wqk, vo, ln_g[l, 1], ln_b[l, 1],
                                                  moe_w_rg[l], moe_b_rg[l], moe_w_re[l], moe_b_re[l],
                                                  ts=min(XATTN_TS, seq))
        x = hierarchical_moe_layer(x, xp, route, counts, moe_w1, moe_w3, moe_w2,
                                   ln_g[l, 2], ln_b[l, 2], layer=l)
    return x
```

```python
import functools
import math

import jax
import jax.numpy as jnp
from jax import lax
from jax.experimental import pallas as pl
from jax.experimental.pallas import tpu as pltpu
from jax.experimental.pallas import tpu_sc as plsc

F32 = jnp.float32
BF16 = jnp.bfloat16

A_GROUPS = 4
CHUNK = 128
B_HEADS = 4
B_CONV = 4
C_HEAD_DIM = 64
C_KV_HEADS = 4
X_HEADS = 4
N_GROUPS = 4
EXPERTS_PER_GROUP = 8
N_EXPERTS = N_GROUPS * EXPERTS_PER_GROUP
ROPE_THETA = 10000.0
LN_EPS = 1e-5
DEPTH = 2
DN_ALPHA = (2 * DEPTH) ** 0.25

LANES = 128
VMEM_LIMIT = 48 * 1024 * 1024
NEG = -1e30


def _cparams(sem):
    return pltpu.CompilerParams(dimension_semantics=sem, vmem_limit_bytes=VMEM_LIMIT)


def _full(shape):
    nd = len(shape)
    return pl.BlockSpec(shape, lambda *_: (0,) * nd)


def _dot(a, b):
    return jnp.dot(a, b, preferred_element_type=F32)


def _dot_nt(a, b):
    return lax.dot_general(a, b, (((1,), (1,)), ((), ())), preferred_element_type=F32)


def _dot_tn(a, b):
    return lax.dot_general(a, b, (((0,), (0,)), ((), ())), preferred_element_type=F32)


def _split_dot(a, b_bf16):
    hi = a.astype(BF16)
    lo = (a - hi.astype(F32)).astype(BF16)
    return _dot(hi, b_bf16) + _dot(lo, b_bf16)


def _ln(x, g, b):
    mu = jnp.mean(x, axis=-1, keepdims=True)
    xc = x - mu
    var = jnp.mean(xc * xc, axis=-1, keepdims=True)
    return xc * lax.rsqrt(var + LN_EPS) * g + b


LOG2E = math.log2(math.e)


def _silu(x):
    return x * (1.0 / (1.0 + jnp.exp2(x * -LOG2E)))


def _gelu(x):
    c = -2.0 * LOG2E * math.sqrt(2.0 / math.pi)
    return x * (1.0 / (1.0 + jnp.exp2(x * (c + (c * 0.044715) * (x * x)))))


def _pack_halves(x):
    c = x.shape[1] // 2
    lo = lax.bitcast_convert_type(x[:, :c].astype(BF16).astype(F32), jnp.uint32)
    hi = lax.bitcast_convert_type(x[:, c:].astype(BF16).astype(F32), jnp.uint32)
    return lax.bitcast_convert_type((lo >> 16) | hi, jnp.int32)


def _unpack_halves(p):
    u = lax.bitcast_convert_type(p, jnp.uint32)
    lo = lax.bitcast_convert_type(u << 16, F32)
    hi = lax.bitcast_convert_type(u & jnp.uint32(0xFFFF0000), F32)
    return lo, hi


def _interleave(gens, skew):
    live = list(enumerate(gens))
    t = 0
    while live:
        for k, g in list(live):
            if t >= k * skew:
                try:
                    next(g)
                except StopIteration:
                    live.remove((k, g))
        t += 1


def _log_sigmoid(x):
    return jnp.minimum(x, 0.0) - jnp.log(1.0 + jnp.exp(-jnp.abs(x)))


def _even_kernel(x_ref, w_in_ref, gm_g_ref, gm_b_ref, gm_w_ref, gm_bias_ref,
                 conv_w_ref, conv_b_ref, wq_ref, wk_ref, wv_ref, wif_t_ref, bif_t_ref,
                 norm_w_ref, skip_ref, w_out_ref, ln_g_ref, ln_b_ref,
                 o_ref,
                 xm_buf, ct_ref, m_ref, *, ts, aw, bw):
    dh = bw // B_HEADS
    agd = aw // A_GROUPS
    nck = ts // CHUNK
    pad = 8
    j = pl.program_id(1)

    @pl.when(j == 0)
    def _():
        xm_buf[0:pad, :] = jnp.zeros((pad, bw), F32)
        ct_ref[...] = jnp.zeros_like(ct_ref)
        m_ref[...] = jnp.zeros_like(m_ref)

    row = lax.broadcasted_iota(jnp.int32, (CHUNK, CHUNK), 0)
    col = lax.broadcasted_iota(jnp.int32, (CHUNK, CHUNK), 1)
    causal = col <= row
    diag = col == row
    triu = jnp.where(row <= col, 1.0, 0.0).astype(BF16)
    ones_blk = jnp.ones((CHUNK, LANES), BF16)

    x = x_ref[0]
    proj = _dot(x.astype(BF16), w_in_ref[...])
    a_u = _gelu(proj[:, :aw])
    a_v = _gelu(proj[:, aw:2 * aw])
    xm = proj[:, 2 * aw:2 * aw + bw]
    z = proj[:, 2 * aw + bw:]

    vn = _ln(a_v, gm_g_ref[...], gm_b_ref[...]).astype(BF16)
    ya_chunks = []
    for c in range(nck):
        cols = []
        for g in range(A_GROUPS):
            v_cg = vn[c * CHUNK:(c + 1) * CHUNK, g * agd:(g + 1) * agd]
            cols.append(_dot(gm_w_ref[g], v_cg))
        ya_chunks.append(jnp.concatenate(cols, axis=1) + gm_bias_ref[...])
    y_a = a_u * jnp.concatenate(ya_chunks, axis=0)

    xm_buf[pad:pad + ts, :] = xm
    conv = conv_b_ref[...] + conv_w_ref[B_CONV - 1:B_CONV, :] * xm
    for k in range(B_CONV - 1):
        sh = B_CONV - 1 - k
        conv = conv + conv_w_ref[k:k + 1, :] * xm_buf[pad - sh:pad - sh + ts, :]
    xm_buf[pad - (B_CONV - 1):pad, :] = xm_buf[pad + ts - (B_CONV - 1):pad + ts, :]
    xc = _silu(conv)
    xc_b = xc.astype(BF16)
    q = _dot_head_pairs(xc_b, wq_ref)
    k_ = _dot_head_pairs(xc_b, wk_ref)
    v = _dot_head_pairs(xm.astype(BF16), wv_ref)
    gate_in = jnp.concatenate([q, k_, v], axis=1).astype(BF16)
    gates_t = _dot_nt(wif_t_ref[...], gate_in) + bif_t_ref[...]
    ig_all = gates_t[:B_HEADS, :]
    lf_all = _log_sigmoid(gates_t[B_HEADS:, :])
    q_b = q.astype(BF16)
    k_b = (k_ * dh ** -0.5).astype(BF16)
    v_b = v.astype(BF16)
    gate_z = _silu(z)

    h_chunks = []
    for c in range(nck):
        sl = slice(c * CHUNK, (c + 1) * CHUNK)
        lf_r = lf_all[:, sl]
        b_r = _split_dot(lf_r, triu)
        a_r = ig_all[:, sl] - b_r
        heads = []
        for h in range(B_HEADS):
            hs = slice(h * dh, (h + 1) * dh)
            qh, kh = q_b[sl, hs], k_b[sl, hs]
            vaug = jnp.concatenate([v_b[sl, hs], ones_blk], axis=1)
            a_row = a_r[h:h + 1, :]
            m_row = m_ref[h:h + 1, :]
            amat = jnp.where(causal, a_row, NEG)
            mx = jnp.maximum(jnp.max(amat, axis=1, keepdims=True), m_row)
            w_intra = jnp.exp(amat - mx)
            w_state = jnp.exp(m_row - mx)
            s = _dot_nt(qh, kh) * w_intra
            caug = ct_ref[h]
            naug = (_dot(s.astype(BF16), vaug)
                    + jnp.concatenate([w_state, w_state], axis=1) * _dot(qh, caug.astype(BF16)))
            num, nq = naug[:, :dh], naug[:, dh:]
            b_t = _split_dot(jnp.where(causal, lf_r[h:h + 1, :], 0.0), ones_blk)
            hv = num / jnp.maximum(jnp.abs(nq), jnp.exp(-(b_t + mx)))
            bl = b_r[h:h + 1, CHUNK - 1:CHUNK]
            g_row = bl + a_row
            m_new = jnp.maximum(bl + m_row, jnp.max(g_row, axis=1, keepdims=True))
            decay = jnp.exp(bl + m_row - m_new)
            wg_row = jnp.exp(g_row - m_new)
            wg_t = _dot(jnp.where(diag, wg_row, 0.0).astype(BF16), ones_blk)
            kw = (kh.astype(F32) * wg_t).astype(BF16)
            ct_ref[h] = jnp.concatenate([decay, decay], axis=1) * caug + _dot_tn(kw, vaug)
            m_ref[h:h + 1, :] = m_new
            hc = hv - jnp.mean(hv, axis=1, keepdims=True)
            hn = hc * lax.rsqrt(jnp.mean(hc * hc, axis=1, keepdims=True) + LN_EPS)
            heads.append(hn)
        h_chunks.append(jnp.concatenate(heads, axis=1))
    hn_all = jnp.concatenate(h_chunks, axis=0) if nck > 1 else h_chunks[0]
    y_b = (hn_all * norm_w_ref[...] + skip_ref[...] * xc) * gate_z

    mix = jnp.concatenate([y_a, y_b], axis=1).astype(BF16)
    y = _dot(mix, w_out_ref[...])
    o_ref[0] = _ln(DN_ALPHA * x + y, ln_g_ref[...], ln_b_ref[...])


def _split_dot_left(a_bf16, b):
    hi = b.astype(BF16)
    lo = (b - hi.astype(F32)).astype(BF16)
    return _dot(a_bf16, hi) + _dot(a_bf16, lo)


def _block_diag_pairs(w):
    hh, d, _ = w.shape
    wp = w.reshape(hh // 2, 2, d, d)
    eye = jnp.eye(2, dtype=w.dtype)
    return jnp.einsum('pade,ab->padbe', wp, eye).reshape(hh // 2, 2 * d, 2 * d)


def _dot_head_pairs(x, w_ref):
    npair, w2, _ = w_ref.shape
    return jnp.concatenate([_dot(x[:, p * w2:(p + 1) * w2], w_ref[p]) for p in range(npair)], axis=1)


def even_mixer_layer(x, w_in, gm_ln_g, gm_ln_b, gm_ws, gm_bs, conv_w, conv_b, wq, wk, wv, w_if,
                     b_if, norm_w, skip, w_out, ln_g, ln_b, *, ts):
    bsz, seq, d = x.shape
    aw = gm_ln_g.shape[0]
    bw = conv_b.shape[0]
    agd = aw // A_GROUPS
    causal = jnp.tril(jnp.ones((CHUNK, CHUNK), dtype=bool))
    gm_w = jnp.where(causal[None], gm_ws, 0.0).astype(BF16)
    gm_bias = jnp.repeat(gm_bs.T, agd, axis=1)
    row = lambda a: a.reshape(1, -1)
    args = (x, w_in.astype(BF16), row(gm_ln_g), row(gm_ln_b), gm_w, gm_bias,
            conv_w, row(conv_b), _block_diag_pairs(wq).astype(BF16), _block_diag_pairs(wk).astype(BF16),
            _block_diag_pairs(wv).astype(BF16), w_if.T.astype(BF16),
            b_if.reshape(-1, 1), row(norm_w), row(skip), w_out.astype(BF16),
            row(ln_g), row(ln_b))
    in_specs = [pl.BlockSpec((1, ts, d), lambda b, j: (b, j, 0))] + [_full(a.shape) for a in args[1:]]
    dh = bw // B_HEADS
    assert dh == LANES and CHUNK == LANES
    return pl.pallas_call(
        functools.partial(_even_kernel, ts=ts, aw=aw, bw=bw),
        grid=(bsz, seq // ts),
        in_specs=in_specs,
        out_specs=pl.BlockSpec((1, ts, d), lambda b, j: (b, j, 0)),
        out_shape=jax.ShapeDtypeStruct((bsz, seq, d), F32),
        scratch_shapes=[pltpu.VMEM((8 + ts, bw), F32),
                        pltpu.VMEM((B_HEADS, dh, dh + LANES), F32),
                        pltpu.VMEM((B_HEADS, LANES), F32)],
        compiler_params=_cparams(("arbitrary", "arbitrary")),
        name="even_mixer",
    )(*args)


def _memfold_kernel(mem_ref, wkv_ref, wq_ref, wo_ref, wqk_ref, vo_ref, *, d):
    dh = d // X_HEADS
    m_len = mem_ref.shape[1]
    kv = _dot(mem_ref[0].astype(BF16), wkv_ref[...])
    k = (kv[:, :d] * (dh ** -0.5 * LOG2E)).astype(BF16)
    v = kv[:, d:].astype(BF16)
    for h in range(X_HEADS):
        hs = slice(h * dh, (h + 1) * dh)
        ms = slice(h * m_len, (h + 1) * m_len)
        wqk_ref[0, :, ms] = _dot_nt(wq_ref[:, hs], k[:, hs]).astype(BF16)
        vo_ref[0, ms, :] = _dot(v[:, hs], wo_ref[hs, :]).astype(BF16)


def memory_fold(mem, wkv, wq, wo):
    bsz, m_len, d = mem.shape
    hm = X_HEADS * m_len
    args = (mem, wkv.astype(BF16), wq.astype(BF16), wo.astype(BF16))
    return pl.pallas_call(
        functools.partial(_memfold_kernel, d=d),
        grid=(bsz,),
        in_specs=[pl.BlockSpec((1, m_len, d), lambda b: (b, 0, 0))] + [_full(a.shape) for a in args[1:]],
        out_specs=[pl.BlockSpec((1, d, hm), lambda b: (b, 0, 0)),
                   pl.BlockSpec((1, hm, d), lambda b: (b, 0, 0))],
        out_shape=[jax.ShapeDtypeStruct((bsz, d, hm), BF16),
                   jax.ShapeDtypeStruct((bsz, hm, d), BF16)],
        compiler_params=_cparams(("arbitrary",)),
        name="memory_fold",
    )(*args)


ROUTE_W = 128
XATTN_TS = 1024
XATTN_GROUP = 1024
XATTN_NTILE = 1024
XATTN_SKEW = 1
DEST_ROWS = 8


def _xattn_kernel(x_ref, wqk_ref, vo_ref, ln_g_ref, ln_b_ref, wr_ref, br_ref,
                  o_ref, xp_ref, route_ref, cnt_ref):
    m_len = wqk_ref.shape[2] // X_HEADS
    @pl.when(pl.program_id(1) == 0)
    def _():
        cnt_ref[...] = jnp.zeros_like(cnt_ref)

    ts = XATTN_GROUP if x_ref.shape[1] % XATTN_GROUP == 0 else x_ref.shape[1]
    ngroups = x_ref.shape[1] // ts
    d = x_ref.shape[2]
    lane = lax.broadcasted_iota(jnp.int32, (ts, ROUTE_W), 1)
    is_g = lane < N_GROUPS
    group_counts = []

    def row_group(g):
        rows = slice(g * ts, (g + 1) * ts)
        x = x_ref[0, rows, :]
        xb = x.astype(BF16)
        yield
        probs = []
        for h in range(X_HEADS):
            s = _dot(xb, wqk_ref[0, :, h * m_len:(h + 1) * m_len])
            yield
            p = jnp.exp2(s - jnp.max(s, axis=1, keepdims=True))
            probs.append((p / jnp.sum(p, axis=1, keepdims=True)).astype(BF16))
            yield
        p_all = jnp.concatenate(probs, axis=1)
        tiles = []
        for n in range(d // XATTN_NTILE):
            cs = slice(n * XATTN_NTILE, (n + 1) * XATTN_NTILE)
            tiles.append(DN_ALPHA * x[:, cs] + _dot(p_all, vo_ref[0, :, cs]))
            yield
        x2 = _ln(jnp.concatenate(tiles, axis=1), ln_g_ref[...], ln_b_ref[...])
        o_ref[0, rows, :] = x2
        yield
        xp_ref[0, rows, :] = _pack_halves(x2)
        x_hi = x2.astype(BF16)
        x_lo = (x2 - x_hi.astype(F32)).astype(BF16)
        yield
        logits = (_dot(x_hi, wr_ref[0]) + _dot(x_lo, wr_ref[0]) + _dot(x_hi, wr_ref[1])) + br_ref[...]
        yield
        lg = jnp.where(is_g, logits, NEG)
        mg = jnp.max(lg, axis=1, keepdims=True)
        gi = jnp.min(jnp.where(jnp.logical_and(is_g, lg == mg), lane, ROUTE_W), axis=1, keepdims=True)
        gate_g = 1.0 / jnp.sum(jnp.where(is_g, jnp.exp(lg - mg), 0.0), axis=1, keepdims=True)
        lo = N_GROUPS + gi * EXPERTS_PER_GROUP
        in_grp = jnp.logical_and(lane >= lo, lane < lo + EXPERTS_PER_GROUP)
        le = jnp.where(in_grp, logits, NEG)
        v1 = jnp.max(le, axis=1, keepdims=True)
        i1 = jnp.min(jnp.where(jnp.logical_and(in_grp, le == v1), lane, ROUTE_W), axis=1, keepdims=True)
        yield
        le2 = jnp.where(lane == i1, NEG, le)
        v2 = jnp.max(le2, axis=1, keepdims=True)
        i2 = jnp.min(jnp.where(jnp.logical_and(in_grp, le2 == v2), lane, ROUTE_W), axis=1, keepdims=True)
        e21 = jnp.exp(v2 - v1)
        p1 = 1.0 / (1.0 + e21)
        p2 = e21 * p1
        e1 = (i1 - N_GROUPS).astype(F32)
        e2 = (i2 - N_GROUPS).astype(F32)
        rec = jnp.where(lane == 0, e1, 0.0)
        rec = jnp.where(lane == 1, e2, rec)
        rec = jnp.where(lane == 2, gate_g * p1, rec)
        rec = jnp.where(lane == 3, gate_g * p2, rec)
        route_ref[0, rows, :] = rec
        sel = jnp.logical_or(lane == i1 - N_GROUPS, lane == i2 - N_GROUPS)
        group_counts.append(jnp.sum(jnp.where(sel, 1.0, 0.0), axis=0, keepdims=True))

    _interleave([row_group(g) for g in range(ngroups)], skew=XATTN_SKEW)
    counts = group_counts[0]
    for c in group_counts[1:]:
        counts = counts + c
    cnt_ref[...] += counts


def xattn_router_layer(x, wqk, vo, ln_g, ln_b, w_rg, b_rg, w_re, b_re, *, ts):
    bsz, seq, d = x.shape
    hm = wqk.shape[2]
    wr = jnp.zeros((d, ROUTE_W), F32).at[:, :N_GROUPS].set(w_rg).at[:, N_GROUPS:N_GROUPS + N_EXPERTS].set(w_re)
    br = jnp.zeros((1, ROUTE_W), F32).at[0, :N_GROUPS].set(b_rg).at[0, N_GROUPS:N_GROUPS + N_EXPERTS].set(b_re)
    wr_hi = wr.astype(BF16)
    wr_lo = (wr - wr_hi.astype(F32)).astype(BF16)
    wr = jnp.stack([wr_hi, wr_lo])
    args = (x, wqk, vo, ln_g.reshape(1, -1), ln_b.reshape(1, -1), wr, br)
    in_specs = [pl.BlockSpec((1, ts, d), lambda b, j: (b, j, 0)),
                pl.BlockSpec((1, d, hm), lambda b, j: (b, 0, 0)),
                pl.BlockSpec((1, hm, d), lambda b, j: (b, 0, 0))] + [_full(a.shape) for a in args[3:]]
    return pl.pallas_call(
        _xattn_kernel,
        grid=(bsz, seq // ts),
        in_specs=in_specs,
        out_specs=[pl.BlockSpec((1, ts, d), lambda b, j: (b, j, 0)),
                   pl.BlockSpec((1, ts, d // 2), lambda b, j: (b, j, 0)),
                   pl.BlockSpec((1, ts, ROUTE_W), lambda b, j: (b, j, 0)),
                   pl.BlockSpec((1, 1, ROUTE_W), lambda b, j: (b, 0, 0))],
        out_shape=[jax.ShapeDtypeStruct((bsz, seq, d), F32),
                   jax.ShapeDtypeStruct((bsz, seq, d // 2), jnp.int32),
                   jax.ShapeDtypeStruct((bsz, seq, ROUTE_W), F32),
                   jax.ShapeDtypeStruct((bsz, 1, ROUTE_W), F32)],
        compiler_params=_cparams(("arbitrary", "arbitrary")),
        name="xattn_router",
    )(*args)


def _swa_kernel(x_ref, wqkv_ref, bqkv_ref, cos_ref, sin_ref, sink_ref, wo_ref, ln_g_ref, ln_b_ref,
                o_ref, kprev, vprev, *, ts, cq, ckv):
    j = pl.program_id(1)
    nb = ts // CHUNK
    dh = C_HEAD_DIM
    grp = (cq // dh) // C_KV_HEADS

    @pl.when(j == 0)
    def _():
        kprev[...] = jnp.zeros_like(kprev)
        vprev[...] = jnp.zeros_like(vprev)

    x = x_ref[0]
    qkv = _dot(x.astype(BF16), wqkv_ref[...]) + bqkv_ref[...]
    cos = cos_ref[...]
    sin = sin_ref[...]
    lane = lax.broadcasted_iota(jnp.int32, (ts, LANES), 1)
    first_half = (lane % dh) < (dh // 2)

    def rope(t):
        outs = []
        for c in range(t.shape[1] // LANES):
            tc = t[:, c * LANES:(c + 1) * LANES]
            rot = jnp.where(first_half, pltpu.roll(tc, LANES - dh // 2, 1), pltpu.roll(tc, dh // 2, 1))
            outs.append(tc * cos + rot * sin)
        return jnp.concatenate(outs, axis=1)

    q = rope(qkv[:, :cq]).astype(BF16)
    k = rope(qkv[:, cq:cq + 2 * ckv]).astype(BF16)
    v = qkv[:, cq + 2 * ckv:].astype(BF16)

    r_i = lax.broadcasted_iota(jnp.int32, (CHUNK, 2 * CHUNK), 0)
    c_i = lax.broadcasted_iota(jnp.int32, (CHUNK, 2 * CHUNK), 1)
    band = jnp.logical_and(c_i > r_i, c_i <= r_i + CHUNK)
    sink_col = c_i == 0
    lane_k = lax.broadcasted_iota(jnp.int32, (2 * CHUNK, LANES), 1)
    key_row = lax.broadcasted_iota(jnp.int32, (2 * CHUNK, LANES), 0)
    lane_q = lax.broadcasted_iota(jnp.int32, (CHUNK, LANES), 1)
    ones_blk = jnp.ones((2 * CHUNK, LANES), BF16)
    zero_b = jnp.zeros((), BF16)
    blocks = []
    for c in range(nb):
        sl = slice(c * CHUNK, (c + 1) * CHUNK)
        if c == 0:
            kb = jnp.concatenate([kprev[...].astype(BF16), k[sl]], axis=0)
            vb = jnp.concatenate([vprev[...].astype(BF16), v[sl]], axis=0)
            first_key = jnp.where(j > 0, 0, CHUNK)
            valid = jnp.logical_and(band, c_i >= first_key)
        else:
            kb = k[(c - 1) * CHUNK:(c + 1) * CHUNK]
            vb = v[(c - 1) * CHUNK:(c + 1) * CHUNK]
            valid = band
        tiles = []
        for h in range(C_KV_HEADS):
            kd = kb[:, h * LANES:(h + 1) * LANES]
            vd = vb[:, h * LANES:(h + 1) * LANES]
            k_lo = jnp.where(lane_k < dh, kd, zero_b)
            k_hi = jnp.where(lane_k >= dh, kd, zero_b)
            vz = jnp.where(key_row == 0, zero_b, vd)
            q2 = jnp.concatenate([q[sl, (2 * h) * LANES:(2 * h + 1) * LANES],
                                  q[sl, (2 * h + 1) * LANES:(2 * h + 2) * LANES]], axis=0)
            s_lo = _dot_nt(q2, k_lo)
            s_hi = _dot_nt(q2, k_hi)
            parts = []
            for qk, g in ((s_lo[:CHUNK], 0), (s_lo[CHUNK:], 2), (s_hi[:CHUNK], 1), (s_hi[CHUNK:], 3)):
                sink = sink_ref[h * grp + g]
                parts.append(jnp.where(valid, qk, jnp.where(sink_col, sink, NEG)))
            s = jnp.concatenate(parts, axis=0)
            p = jnp.exp2(s - jnp.max(s, axis=1, keepdims=True)).astype(BF16)
            den = _dot(p, ones_blk)
            o2 = _dot(p, vz) / den
            tiles.append(jnp.where(lane_q < dh, o2[:CHUNK], o2[2 * CHUNK:3 * CHUNK]))
            tiles.append(jnp.where(lane_q < dh, o2[CHUNK:2 * CHUNK], o2[3 * CHUNK:]))
        blocks.append(jnp.concatenate(tiles, axis=1))
    kprev[...] = k[(nb - 1) * CHUNK:].astype(F32)
    vprev[...] = v[(nb - 1) * CHUNK:].astype(F32)
    att = (jnp.concatenate(blocks, axis=0) if nb > 1 else blocks[0]).astype(BF16)
    y = _dot(att, wo_ref[...])
    o_ref[0] = _ln(DN_ALPHA * x + y, ln_g_ref[...], ln_b_ref[...])


def swa_mixer_layer(x, w_qkv, b_qkv, sinks, w_o, ln_g, ln_b, *, ts):
    bsz, seq, d = x.shape
    cq = w_o.shape[0]
    ckv = (w_qkv.shape[1] - cq) // 2
    dh = C_HEAD_DIM
    inv = ROPE_THETA ** (-jnp.arange(0, dh, 2, dtype=F32) / dh)
    ang = jnp.arange(seq, dtype=F32)[:, None] * inv[None, :]
    reps = LANES // (dh // 2)
    sign = jnp.tile(jnp.concatenate([-jnp.ones((dh // 2,), F32), jnp.ones((dh // 2,), F32)]), LANES // dh)
    cos_t = jnp.tile(jnp.cos(ang), (1, reps))
    sin_t = jnp.tile(jnp.sin(ang), (1, reps)) * sign[None, :]
    assert 2 * dh == LANES

    def dup_heads(t):
        th = t.reshape(t.shape[:-1] + (ckv // dh, dh))
        return jnp.concatenate([th, th], axis=-1).reshape(t.shape[:-1] + (2 * ckv,))

    qs = dh ** -0.5 * LOG2E
    w_all = jnp.concatenate([w_qkv[:, :cq] * qs, dup_heads(w_qkv[:, cq:cq + ckv]),
                             dup_heads(w_qkv[:, cq + ckv:])], axis=1)
    b_all = jnp.concatenate([b_qkv[:cq] * qs, dup_heads(b_qkv[cq:cq + ckv]), dup_heads(b_qkv[cq + ckv:])])
    args = (x, w_all.astype(BF16), b_all.reshape(1, -1), cos_t, sin_t, sinks.astype(F32) * LOG2E,
            w_o.astype(BF16), ln_g.reshape(1, -1), ln_b.reshape(1, -1))
    in_specs = [pl.BlockSpec((1, ts, d), lambda b, j: (b, j, 0)),
                _full(args[1].shape), _full(args[2].shape),
                pl.BlockSpec((ts, LANES), lambda b, j: (j, 0)),
                pl.BlockSpec((ts, LANES), lambda b, j: (j, 0)),
                pl.BlockSpec(memory_space=pltpu.SMEM),
                _full(args[6].shape), _full(args[7].shape), _full(args[8].shape)]
    return pl.pallas_call(
        functools.partial(_swa_kernel, ts=ts, cq=cq, ckv=ckv),
        grid=(bsz, seq // ts),
        in_specs=in_specs,
        out_specs=pl.BlockSpec((1, ts, d), lambda b, j: (b, j, 0)),
        out_shape=jax.ShapeDtypeStruct((bsz, seq, d), F32),
        scratch_shapes=[pltpu.VMEM((CHUNK, 2 * ckv), F32), pltpu.VMEM((CHUNK, 2 * ckv), F32)],
        compiler_params=_cparams(("arbitrary", "arbitrary")),
        name="swa_mixer",
    )(*args)


def _slot_kernel(route_ref, pstart_ref, dest_ref, carry_ref, *, tb):
    @pl.when(pl.program_id(0) == 0)
    def _():
        carry_ref[...] = pstart_ref[...]

    rec = route_ref[...]
    lane = lax.broadcasted_iota(jnp.int32, (tb, ROUTE_W), 1)
    e0 = rec[:, 0:1].astype(jnp.int32)
    e1 = rec[:, 1:2].astype(jnp.int32)
    oh0 = lane == e0
    oh1 = lane == e1
    ohs = jnp.where(jnp.logical_or(oh0, oh1), 1.0, 0.0)
    r = lax.broadcasted_iota(jnp.int32, (tb, tb), 0)
    c = lax.broadcasted_iota(jnp.int32, (tb, tb), 1)
    before = jnp.where(c < r, 1.0, 0.0).astype(BF16)
    prefix = _dot(before, ohs.astype(BF16)) + carry_ref[...]
    d0 = jnp.sum(jnp.where(oh0, prefix, 0.0), axis=1, keepdims=True)
    d1 = jnp.sum(jnp.where(oh1, prefix, 0.0), axis=1, keepdims=True)
    dest = jnp.where(lane == 0, d0, jnp.where(lane == 1, d1, 0.0))
    dest_ref[...] = dest.T[:DEST_ROWS].astype(jnp.int32)
    carry_ref[...] += jnp.sum(ohs, axis=0, keepdims=True)


def moe_slots(route, pstart, *, row0, n, tb):
    blk0 = row0 // tb
    return pl.pallas_call(
        functools.partial(_slot_kernel, tb=tb),
        grid=(n // tb,),
        in_specs=[pl.BlockSpec((tb, ROUTE_W), lambda i: (blk0 + i, 0)), _full((1, ROUTE_W))],
        out_specs=pl.BlockSpec((DEST_ROWS, tb), lambda i: (0, i)),
        out_shape=jax.ShapeDtypeStruct((DEST_ROWS, n), jnp.int32),
        scratch_shapes=[pltpu.VMEM((1, ROUTE_W), F32)],
        compiler_params=_cparams(("arbitrary",)),
        name="moe_slots",
    )(route, pstart)


def _ffn_kernel(blk_exp_ref, new_exp_ref, nblk_ref, xs_ref, w1_ref, w3_ref, w2_ref, ys_ref,
                w1_b, w3_b, w2_b):
    i = pl.program_id(0)
    used = i < nblk_ref[0]

    @pl.when(jnp.logical_and(used, new_exp_ref[i] == 1))
    def _():
        w1_b[...] = w1_ref[0, 0].astype(BF16)
        w3_b[...] = w3_ref[0, 0].astype(BF16)
        w2_b[...] = w2_ref[0, 0].astype(BF16)

    @pl.when(used)
    def _():
        def row_group(rows):
            x_lo, x_hi = _unpack_halves(xs_ref[rows, :])
            x_lo, x_hi = x_lo.astype(BF16), x_hi.astype(BF16)
            dl = x_lo.shape[1]
            yield
            h1 = _dot(x_lo, w1_b[:dl, :]) + _dot(x_hi, w1_b[dl:, :])
            yield
            h3 = _dot(x_lo, w3_b[:dl, :]) + _dot(x_hi, w3_b[dl:, :])
            yield
            h = (_silu(h1) * h3).astype(BF16)
            yield
            y = _dot(h, w2_b[...])
            yield
            ys_ref[rows, :] = _pack_halves(y)

        _interleave([row_group(slice(r * FFN_ROWS, (r + 1) * FFN_ROWS))
                     for r in range(xs_ref.shape[0] // FFN_ROWS)], skew=1)

    @pl.when(jnp.logical_not(used))
    def _():
        ys_ref[...] = jnp.zeros_like(ys_ref)


def moe_ffn(xs, blk_exp, new_exp, nblk, w1, w3, w2, *, layer, bm):
    n_pad, dp = xs.shape
    d = 2 * dp
    de = w1.shape[3]
    n_blk = n_pad // bm

    def x_map(i, be, ne, nb):
        return (jnp.minimum(i, nb[0] - 1), 0)

    def w_map(i, be, ne, nb):
        return (layer, be[i], 0, 0)

    return pl.pallas_call(
        _ffn_kernel,
        grid_spec=pltpu.PrefetchScalarGridSpec(
            num_scalar_prefetch=3,
            grid=(n_blk,),
            in_specs=[pl.BlockSpec((bm, dp), x_map),
                      pl.BlockSpec((1, 1, d, de), w_map),
                      pl.BlockSpec((1, 1, d, de), w_map),
                      pl.BlockSpec((1, 1, de, d), w_map)],
            out_specs=pl.BlockSpec((bm, dp), lambda i, be, ne, nb: (i, 0)),
            scratch_shapes=[pltpu.VMEM((d, de), BF16), pltpu.VMEM((d, de), BF16),
                            pltpu.VMEM((de, d), BF16)]),
        out_shape=jax.ShapeDtypeStruct((n_pad, dp), jnp.int32),
        compiler_params=_cparams(("arbitrary",)),
        name="moe_ffn",
    )(blk_exp, new_exp, nblk, xs, w1, w3, w2)


def _combine_kernel(x_ref, g0_ref, g1_ref, route_ref, ln_g_ref, ln_b_ref, *rest):
    o_ref = rest[-1]
    rec = route_ref[...]
    a_lo, a_hi = _unpack_halves(g0_ref[0])
    b_lo, b_hi = _unpack_halves(g1_ref[0])
    w0, w1 = rec[:, 2:3], rec[:, 3:4]
    y = jnp.concatenate([w0 * a_lo + w1 * b_lo, w0 * a_hi + w1 * b_hi], axis=1)
    o_ref[...] = _ln(DN_ALPHA * x_ref[...] + y, ln_g_ref[...], ln_b_ref[...])


def moe_combine(x, g, route, ln_g, ln_b, *, row0, tb, prev=None):
    n, d = x.shape
    m = g.shape[1]
    blk0 = row0 // tb
    row_spec = pl.BlockSpec((tb, d), lambda i: (blk0 + i, 0))
    args = [x, g, g, route, ln_g.reshape(1, -1), ln_b.reshape(1, -1)]
    in_specs = [row_spec,
                pl.BlockSpec((1, tb, d // 2), lambda i: (0, i, 0)),
                pl.BlockSpec((1, tb, d // 2), lambda i: (1, i, 0)),
                pl.BlockSpec((tb, ROUTE_W), lambda i: (blk0 + i, 0)),
                _full((1, d)), _full((1, d))]
    aliases = {}
    if prev is not None:
        args.append(prev)
        in_specs.append(pl.BlockSpec(memory_space=pl.ANY))
        aliases = {len(args) - 1: 0}
    return pl.pallas_call(
        _combine_kernel,
        grid=(m // tb,),
        in_specs=in_specs,
        out_specs=row_spec,
        out_shape=jax.ShapeDtypeStruct((n, d), F32),
        input_output_aliases=aliases,
        compiler_params=_cparams(("arbitrary",)),
        name="moe_combine",
    )(*args)


MOE_BM = 512
FFN_ROWS = 256
MOE_PARTS = 1
MIXER_TS = 512


def hierarchical_moe_layer(x2, xp, route, counts, w1, w3, w2, ln_g, ln_b, *, layer):
    bsz, seq, d = x2.shape
    n = bsz * seq
    bm = MOE_BM
    xf = x2.reshape(n, d)
    xpf = xp.reshape(n, d // 2)
    rt = route.reshape(n, ROUTE_W)
    parts = MOE_PARTS if bsz % MOE_PARTS == 0 else 1
    m = n // parts
    tb = min(512, m)
    n_blk = (2 * m) // bm + N_EXPERTS
    cnt_parts = counts.reshape(parts, bsz // parts, ROUTE_W).sum(axis=1)[:, :N_EXPERTS].astype(jnp.int32)
    out = None
    for part in range(parts):
        row0 = part * m
        pcnt = (cnt_parts[part] + bm - 1) // bm * bm
        pends = jnp.cumsum(pcnt)
        pstart = pends - pcnt
        nblk = (pends[-1] // bm).astype(jnp.int32).reshape(1)
        blk_row = jnp.arange(n_blk, dtype=jnp.int32) * bm
        blk_exp = jnp.minimum(jnp.sum((pends[None, :] <= blk_row[:, None]).astype(jnp.int32), axis=1),
                              N_EXPERTS - 1)
        last_exp = blk_exp[jnp.maximum(nblk[0] - 1, 0)]
        blk_exp = jnp.where(jnp.arange(n_blk) < nblk[0], blk_exp, last_exp)
        new_exp = jnp.concatenate([jnp.ones((1,), jnp.int32),
                                   (blk_exp[1:] != blk_exp[:-1]).astype(jnp.int32)])
        pstart_rec = jnp.zeros((1, ROUTE_W), F32).at[0, :N_EXPERTS].set(pstart.astype(F32))
        dest = moe_slots(rt, pstart_rec, row0=row0, n=m, tb=tb)
        xs = moe_dispatch(xpf, dest[0], dest[1], n_blk * bm, row0=row0)
        ys = moe_ffn(xs, blk_exp, new_exp, nblk, w1, w3, w2, layer=layer, bm=bm)
        g = moe_gather(ys, dest[0], dest[1])
        out = moe_combine(xf, g, rt, ln_g, ln_b, row0=row0, tb=tb, prev=out)
    return out.reshape(bsz, seq, d)


SC_ROWS = 128


def _sc_mesh():
    return plsc.VectorSubcoreMesh(core_axis_name="c", subcore_axis_name="s")


def moe_dispatch(xf, dest0, dest1, n_pad, *, row0):
    n = dest0.shape[0]
    d = xf.shape[1]
    info = plsc.get_sparse_core_info()
    nw = info.num_cores * info.num_subcores
    per_w = n // nw
    r = min(SC_ROWS, per_w)

    def body(x_hbm, d0_hbm, d1_hbm, xs_hbm, i0_v, i1_v, rows_v, sem):
        wid = lax.axis_index("s") * info.num_cores + lax.axis_index("c")

        @pl.loop(0, per_w // r)
        def _(c):
            base = pl.multiple_of(wid * per_w + c * r, 8)
            pltpu.sync_copy(d0_hbm.at[pl.ds(base, r)], i0_v)
            pltpu.sync_copy(d1_hbm.at[pl.ds(base, r)], i1_v)
            pltpu.sync_copy(x_hbm.at[pl.ds(row0 + base, r)], rows_v)
            pltpu.async_copy(rows_v, xs_hbm.at[i0_v], sem).wait()
            pltpu.async_copy(rows_v, xs_hbm.at[i1_v], sem).wait()

    return pl.kernel(
        body, out_type=jax.ShapeDtypeStruct((n_pad, d), xf.dtype), mesh=_sc_mesh(),
        scratch_types=[pltpu.VMEM((r,), jnp.int32), pltpu.VMEM((r,), jnp.int32),
                       pltpu.VMEM((r, d), xf.dtype), pltpu.SemaphoreType.DMA],
        name="moe_dispatch",
    )(xf, dest0, dest1)


def moe_gather(ys, dest0, dest1):
    n = dest0.shape[0]
    d = ys.shape[1]
    info = plsc.get_sparse_core_info()
    nw = info.num_cores * info.num_subcores
    per_w = n // nw
    r = min(SC_ROWS, per_w)

    def body(ys_hbm, d0_hbm, d1_hbm, g_hbm, i_v, rows_v, sem):
        wid = lax.axis_index("s") * info.num_cores + lax.axis_index("c")

        @pl.loop(0, per_w // r)
        def _(c):
            base = pl.multiple_of(wid * per_w + c * r, 8)
            for k, d_hbm in enumerate((d0_hbm, d1_hbm)):
                pltpu.sync_copy(d_hbm.at[pl.ds(base, r)], i_v)
                pltpu.async_copy(ys_hbm.at[i_v], rows_v, sem).wait()
                pltpu.sync_copy(rows_v, g_hbm.at[k, pl.ds(base, r)])

    return pl.kernel(
        body, out_type=jax.ShapeDtypeStruct((2, n, d), ys.dtype), mesh=_sc_mesh(),
        scratch_types=[pltpu.VMEM((r,), jnp.int32), pltpu.VMEM((r, d), ys.dtype),
                       pltpu.SemaphoreType.DMA],
        name="moe_gather",
    )(ys, dest0, dest1)


def kernel(x, mem, ln_g, ln_b, ev_w_in, ev_gm_ln_g, ev_gm_ln_b, ev_gm_ws, ev_gm_bs, ev_conv_w, ev_conv_b, ev_wq, ev_wk, ev_wv, ev_w_if, ev_b_if, ev_norm_w, ev_skip, ev_w_out, od_w_qkv, od_b_qkv, od_sinks, od_w_o, xa_wq, xa_wkv, xa_wo, moe_w_rg, moe_b_rg, moe_w_re, moe_b_re, moe_w1, moe_w3, moe_w2):
    bsz, seq, d = x.shape
    depth = ln_g.shape[0]
    ts = min(MIXER_TS, seq)
    for l in range(depth):
        if l % 2 == 0:
            e = l // 2
            x = even_mixer_layer(x, ev_w_in[e], ev_gm_ln_g[e], ev_gm_ln_b[e], ev_gm_ws[e], ev_gm_bs[e],
                                 ev_conv_w[e], ev_conv_b[e], ev_wq[e], ev_wk[e], ev_wv[e], ev_w_if[e],
                                 ev_b_if[e], ev_norm_w[e], ev_skip[e], ev_w_out[e],
                                 ln_g[l, 0], ln_b[l, 0], ts=ts)
        else:
            o = l // 2
            x = swa_mixer_layer(x, od_w_qkv[o], od_b_qkv[o], od_sinks[o], od_w_o[o],
                                ln_g[l, 0], ln_b[l, 0], ts=ts)
        wqk, vo = memory_fold(mem, xa_wkv[l], xa_wq[l], xa_wo[l])
        x, xp, route, counts = xattn_router_layer(x, wqk, vo, ln_g[l, 1], ln_b[l, 1],
                                                  moe_w_rg[l], moe_b_rg[l], moe_w_re[l], moe_b_re[l],
                                                  ts=min(XATTN_TS, seq))
        x = hierarchical_moe_layer(x, xp, route, counts, moe_w1, moe_w3, moe_w2,
                                   ln_g[l, 2], ln_b[l, 2], layer=l)
    return x
```

```python
import functools
import math

import jax
import jax.numpy as jnp
from jax import lax
from jax.experimental import pallas as pl
from jax.experimental.pallas import tpu as pltpu
from jax.experimental.pallas import tpu_sc as plsc

F32 = jnp.float32
BF16 = jnp.bfloat16

A_GROUPS = 4
CHUNK = 128
B_HEADS = 4
B_CONV = 4
C_HEAD_DIM = 64
C_KV_HEADS = 4
X_HEADS = 4
N_GROUPS = 4
EXPERTS_PER_GROUP = 8
N_EXPERTS = N_GROUPS * EXPERTS_PER_GROUP
ROPE_THETA = 10000.0
LN_EPS = 1e-5
DEPTH = 2
DN_ALPHA = (2 * DEPTH) ** 0.25

LANES = 128
VMEM_LIMIT = 48 * 1024 * 1024
NEG = -1e30


def _cparams(sem):
    return pltpu.CompilerParams(dimension_semantics=sem, vmem_limit_bytes=VMEM_LIMIT)


def _full(shape):
    nd = len(shape)
    return pl.BlockSpec(shape, lambda *_: (0,) * nd)


def _dot(a, b):
    return jnp.dot(a, b, preferred_element_type=F32)


def _dot_nt(a, b):
    return lax.dot_general(a, b, (((1,), (1,)), ((), ())), preferred_element_type=F32)


def _dot_tn(a, b):
    return lax.dot_general(a, b, (((0,), (0,)), ((), ())), preferred_element_type=F32)


def _split_dot(a, b_bf16):
    hi = a.astype(BF16)
    lo = (a - hi.astype(F32)).astype(BF16)
    return _dot(hi, b_bf16) + _dot(lo, b_bf16)


def _ln(x, g, b):
    mu = jnp.mean(x, axis=-1, keepdims=True)
    xc = x - mu
    var = jnp.mean(xc * xc, axis=-1, keepdims=True)
    return xc * lax.rsqrt(var + LN_EPS) * g + b


LOG2E = math.log2(math.e)


def _silu(x):
    return x * (1.0 / (1.0 + jnp.exp2(x * -LOG2E)))


def _gelu(x):
    return 0.5 * x * (1.0 + jnp.tanh(math.sqrt(2.0 / math.pi) * (x + 0.044715 * (x * x * x))))


def _pack_halves(x):
    c = x.shape[1] // 2
    lo = lax.bitcast_convert_type(x[:, :c].astype(BF16).astype(F32), jnp.uint32)
    hi = lax.bitcast_convert_type(x[:, c:].astype(BF16).astype(F32), jnp.uint32)
    return lax.bitcast_convert_type((lo >> 16) | hi, jnp.int32)


def _unpack_halves(p):
    u = lax.bitcast_convert_type(p, jnp.uint32)
    lo = lax.bitcast_convert_type(u << 16, F32)
    hi = lax.bitcast_convert_type(u & jnp.uint32(0xFFFF0000), F32)
    return lo, hi


def _log_sigmoid(x):
    return jnp.minimum(x, 0.0) - jnp.log(1.0 + jnp.exp(-jnp.abs(x)))


def _even_kernel(x_ref, w_in_ref, gm_g_ref, gm_b_ref, gm_w_ref, gm_bias_ref,
                 conv_w_ref, conv_b_ref, wq_ref, wk_ref, wv_ref, wif_t_ref, bif_t_ref,
                 norm_w_ref, skip_ref, w_out_ref, ln_g_ref, ln_b_ref,
                 o_ref,
                 xm_buf, ct_ref, m_ref, *, ts, aw, bw):
    dh = bw // B_HEADS
    agd = aw // A_GROUPS
    nck = ts // CHUNK
    pad = 8
    j = pl.program_id(1)

    @pl.when(j == 0)
    def _():
        xm_buf[0:pad, :] = jnp.zeros((pad, bw), F32)
        ct_ref[...] = jnp.zeros_like(ct_ref)
        m_ref[...] = jnp.zeros_like(m_ref)

    row = lax.broadcasted_iota(jnp.int32, (CHUNK, CHUNK), 0)
    col = lax.broadcasted_iota(jnp.int32, (CHUNK, CHUNK), 1)
    causal = col <= row
    diag = col == row
    triu = jnp.where(row <= col, 1.0, 0.0).astype(BF16)
    ones_blk = jnp.ones((CHUNK, LANES), BF16)

    x = x_ref[0]
    proj = _dot(x.astype(BF16), w_in_ref[...])
    a_u = _gelu(proj[:, :aw])
    a_v = _gelu(proj[:, aw:2 * aw])
    xm = proj[:, 2 * aw:2 * aw + bw]
    z = proj[:, 2 * aw + bw:]

    vn = _ln(a_v, gm_g_ref[...], gm_b_ref[...]).astype(BF16)
    ya_chunks = []
    for c in range(nck):
        cols = []
        for g in range(A_GROUPS):
            v_cg = vn[c * CHUNK:(c + 1) * CHUNK, g * agd:(g + 1) * agd]
            cols.append(_dot(gm_w_ref[g], v_cg))
        ya_chunks.append(jnp.concatenate(cols, axis=1) + gm_bias_ref[...])
    y_a = a_u * jnp.concatenate(ya_chunks, axis=0)

    xm_buf[pad:pad + ts, :] = xm
    conv = conv_b_ref[...] + conv_w_ref[B_CONV - 1:B_CONV, :] * xm
    for k in range(B_CONV - 1):
        sh = B_CONV - 1 - k
        conv = conv + conv_w_ref[k:k + 1, :] * xm_buf[pad - sh:pad - sh + ts, :]
    xm_buf[pad - (B_CONV - 1):pad, :] = xm_buf[pad + ts - (B_CONV - 1):pad + ts, :]
    xc = _silu(conv)
    xc_b = xc.astype(BF16)
    q = _dot_head_pairs(xc_b, wq_ref)
    k_ = _dot_head_pairs(xc_b, wk_ref)
    v = _dot_head_pairs(xm.astype(BF16), wv_ref)
    gate_in = jnp.concatenate([q, k_, v], axis=1).astype(BF16)
    gates_t = _dot_nt(wif_t_ref[...], gate_in) + bif_t_ref[...]
    ig_all = gates_t[:B_HEADS, :]
    lf_all = _log_sigmoid(gates_t[B_HEADS:, :])
    q_b = q.astype(BF16)
    k_b = (k_ * dh ** -0.5).astype(BF16)
    v_b = v.astype(BF16)
    gate_z = _silu(z)

    h_chunks = []
    for c in range(nck):
        sl = slice(c * CHUNK, (c + 1) * CHUNK)
        lf_r = lf_all[:, sl]
        b_r = _split_dot(lf_r, triu)
        a_r = ig_all[:, sl] - b_r
        heads = []
        for h in range(B_HEADS):
            hs = slice(h * dh, (h + 1) * dh)
            qh, kh = q_b[sl, hs], k_b[sl, hs]
            vaug = jnp.concatenate([v_b[sl, hs], ones_blk], axis=1)
            a_row = a_r[h:h + 1, :]
            m_row = m_ref[h:h + 1, :]
            amat = jnp.where(causal, a_row, NEG)
            mx = jnp.maximum(jnp.max(amat, axis=1, keepdims=True), m_row)
            w_intra = jnp.exp(amat - mx)
            w_state = jnp.exp(m_row - mx)
            s = _dot_nt(qh, kh) * w_intra
            caug = ct_ref[h]
            naug = (_dot(s.astype(BF16), vaug)
                    + jnp.concatenate([w_state, w_state], axis=1) * _dot(qh, caug.astype(BF16)))
            num, nq = naug[:, :dh], naug[:, dh:]
            b_t = _split_dot(jnp.where(causal, lf_r[h:h + 1, :], 0.0), ones_blk)
            hv = num / jnp.maximum(jnp.abs(nq), jnp.exp(-(b_t + mx)))
            bl = b_r[h:h + 1, CHUNK - 1:CHUNK]
            g_row = bl + a_row
            m_new = jnp.maximum(bl + m_row, jnp.max(g_row, axis=1, keepdims=True))
            decay = jnp.exp(bl + m_row - m_new)
            wg_row = jnp.exp(g_row - m_new)
            wg_t = _dot(jnp.where(diag, wg_row, 0.0).astype(BF16), ones_blk)
            kw = (kh.astype(F32) * wg_t).astype(BF16)
            ct_ref[h] = jnp.concatenate([decay, decay], axis=1) * caug + _dot_tn(kw, vaug)
            m_ref[h:h + 1, :] = m_new
            hc = hv - jnp.mean(hv, axis=1, keepdims=True)
            hn = hc * lax.rsqrt(jnp.mean(hc * hc, axis=1, keepdims=True) + LN_EPS)
            heads.append(hn)
        h_chunks.append(jnp.concatenate(heads, axis=1))
    hn_all = jnp.concatenate(h_chunks, axis=0) if nck > 1 else h_chunks[0]
    y_b = (hn_all * norm_w_ref[...] + skip_ref[...] * xc) * gate_z

    mix = jnp.concatenate([y_a, y_b], axis=1).astype(BF16)
    y = _dot(mix, w_out_ref[...])
    o_ref[0] = _ln(DN_ALPHA * x + y, ln_g_ref[...], ln_b_ref[...])


def _split_dot_left(a_bf16, b):
    hi = b.astype(BF16)
    lo = (b - hi.astype(F32)).astype(BF16)
    return _dot(a_bf16, hi) + _dot(a_bf16, lo)


def _block_diag_pairs(w):
    hh, d, _ = w.shape
    wp = w.reshape(hh // 2, 2, d, d)
    eye = jnp.eye(2, dtype=w.dtype)
    return jnp.einsum('pade,ab->padbe', wp, eye).reshape(hh // 2, 2 * d, 2 * d)


def _dot_head_pairs(x, w_ref):
    npair, w2, _ = w_ref.shape
    return jnp.concatenate([_dot(x[:, p * w2:(p + 1) * w2], w_ref[p]) for p in range(npair)], axis=1)


def even_mixer_layer(x, w_in, gm_ln_g, gm_ln_b, gm_ws, gm_bs, conv_w, conv_b, wq, wk, wv, w_if,
                     b_if, norm_w, skip, w_out, ln_g, ln_b, *, ts):
    bsz, seq, d = x.shape
    aw = gm_ln_g.shape[0]
    bw = conv_b.shape[0]
    agd = aw // A_GROUPS
    causal = jnp.tril(jnp.ones((CHUNK, CHUNK), dtype=bool))
    gm_w = jnp.where(causal[None], gm_ws, 0.0).astype(BF16)
    gm_bias = jnp.repeat(gm_bs.T, agd, axis=1)
    row = lambda a: a.reshape(1, -1)
    args = (x, w_in.astype(BF16), row(gm_ln_g), row(gm_ln_b), gm_w, gm_bias,
            conv_w, row(conv_b), _block_diag_pairs(wq).astype(BF16), _block_diag_pairs(wk).astype(BF16),
            _block_diag_pairs(wv).astype(BF16), w_if.T.astype(BF16),
            b_if.reshape(-1, 1), row(norm_w), row(skip), w_out.astype(BF16),
            row(ln_g), row(ln_b))
    in_specs = [pl.BlockSpec((1, ts, d), lambda b, j: (b, j, 0))] + [_full(a.shape) for a in args[1:]]
    dh = bw // B_HEADS
    assert dh == LANES and CHUNK == LANES
    return pl.pallas_call(
        functools.partial(_even_kernel, ts=ts, aw=aw, bw=bw),
        grid=(bsz, seq // ts),
        in_specs=in_specs,
        out_specs=pl.BlockSpec((1, ts, d), lambda b, j: (b, j, 0)),
        out_shape=jax.ShapeDtypeStruct((bsz, seq, d), F32),
        scratch_shapes=[pltpu.VMEM((8 + ts, bw), F32),
                        pltpu.VMEM((B_HEADS, dh, dh + LANES), F32),
                        pltpu.VMEM((B_HEADS, LANES), F32)],
        compiler_params=_cparams(("arbitrary", "arbitrary")),
        name="even_mixer",
    )(*args)


def _memfold_kernel(mem_ref, wkv_ref, wq_ref, wo_ref, wqk_ref, vo_ref, *, d):
    dh = d // X_HEADS
    m_len = mem_ref.shape[1]
    kv = _dot(mem_ref[0].astype(BF16), wkv_ref[...])
    k = (kv[:, :d] * (dh ** -0.5 * LOG2E)).astype(BF16)
    v = kv[:, d:].astype(BF16)
    for h in range(X_HEADS):
        hs = slice(h * dh, (h + 1) * dh)
        ms = slice(h * m_len, (h + 1) * m_len)
        wqk_ref[0, :, ms] = _dot_nt(wq_ref[:, hs], k[:, hs]).astype(BF16)
        vo_ref[0, ms, :] = _dot(v[:, hs], wo_ref[hs, :]).astype(BF16)


def memory_fold(mem, wkv, wq, wo):
    bsz, m_len, d = mem.shape
    hm = X_HEADS * m_len
    args = (mem, wkv.astype(BF16), wq.astype(BF16), wo.astype(BF16))
    return pl.pallas_call(
        functools.partial(_memfold_kernel, d=d),
        grid=(bsz,),
        in_specs=[pl.BlockSpec((1, m_len, d), lambda b: (b, 0, 0))] + [_full(a.shape) for a in args[1:]],
        out_specs=[pl.BlockSpec((1, d, hm), lambda b: (b, 0, 0)),
                   pl.BlockSpec((1, hm, d), lambda b: (b, 0, 0))],
        out_shape=[jax.ShapeDtypeStruct((bsz, d, hm), BF16),
                   jax.ShapeDtypeStruct((bsz, hm, d), BF16)],
        compiler_params=_cparams(("arbitrary",)),
        name="memory_fold",
    )(*args)


ROUTE_W = 128
XATTN_TS = 1024
DEST_ROWS = 8


def _xattn_kernel(x_ref, wqk_ref, vo_ref, ln_g_ref, ln_b_ref, wr_ref, br_ref,
                  o_ref, xp_ref, route_ref, cnt_ref):
    m_len = wqk_ref.shape[2] // X_HEADS

    @pl.when(pl.program_id(1) == 0)
    def _():
        cnt_ref[...] = jnp.zeros_like(cnt_ref)

    x = x_ref[0]
    ts = x.shape[0]
    xb = x.astype(BF16)
    probs = []
    for h in range(X_HEADS):
        s = _dot(xb, wqk_ref[0, :, h * m_len:(h + 1) * m_len])
        p = jnp.exp2(s - jnp.max(s, axis=1, keepdims=True))
        probs.append((p / jnp.sum(p, axis=1, keepdims=True)).astype(BF16))
    y = _dot(jnp.concatenate(probs, axis=1), vo_ref[0])
    x2 = _ln(DN_ALPHA * x + y, ln_g_ref[...], ln_b_ref[...])
    o_ref[0] = x2
    xp_ref[0] = _pack_halves(x2)

    x_hi = x2.astype(BF16)
    x_lo = (x2 - x_hi.astype(F32)).astype(BF16)
    logits = (_dot(x_hi, wr_ref[0]) + _dot(x_lo, wr_ref[0]) + _dot(x_hi, wr_ref[1])) + br_ref[...]
    lane = lax.broadcasted_iota(jnp.int32, (ts, ROUTE_W), 1)
    is_g = lane < N_GROUPS
    lg = jnp.where(is_g, logits, NEG)
    mg = jnp.max(lg, axis=1, keepdims=True)
    gi = jnp.min(jnp.where(jnp.logical_and(is_g, lg == mg), lane, ROUTE_W), axis=1, keepdims=True)
    gate_g = 1.0 / jnp.sum(jnp.where(is_g, jnp.exp(lg - mg), 0.0), axis=1, keepdims=True)
    lo = N_GROUPS + gi * EXPERTS_PER_GROUP
    in_grp = jnp.logical_and(lane >= lo, lane < lo + EXPERTS_PER_GROUP)
    le = jnp.where(in_grp, logits, NEG)
    v1 = jnp.max(le, axis=1, keepdims=True)
    i1 = jnp.min(jnp.where(jnp.logical_and(in_grp, le == v1), lane, ROUTE_W), axis=1, keepdims=True)
    le2 = jnp.where(lane == i1, NEG, le)
    v2 = jnp.max(le2, axis=1, keepdims=True)
    i2 = jnp.min(jnp.where(jnp.logical_and(in_grp, le2 == v2), lane, ROUTE_W), axis=1, keepdims=True)
    e21 = jnp.exp(v2 - v1)
    p1 = 1.0 / (1.0 + e21)
    p2 = e21 * p1
    e1 = (i1 - N_GROUPS).astype(F32)
    e2 = (i2 - N_GROUPS).astype(F32)
    rec = jnp.where(lane == 0, e1, 0.0)
    rec = jnp.where(lane == 1, e2, rec)
    rec = jnp.where(lane == 2, gate_g * p1, rec)
    rec = jnp.where(lane == 3, gate_g * p2, rec)
    route_ref[0] = rec
    sel = jnp.logical_or(lane == i1 - N_GROUPS, lane == i2 - N_GROUPS)
    cnt_ref[0] += jnp.sum(jnp.where(sel, 1.0, 0.0), axis=0, keepdims=True)


def xattn_router_layer(x, wqk, vo, ln_g, ln_b, w_rg, b_rg, w_re, b_re, *, ts):
    bsz, seq, d = x.shape
    hm = wqk.shape[2]
    wr = jnp.zeros((d, ROUTE_W), F32).at[:, :N_GROUPS].set(w_rg).at[:, N_GROUPS:N_GROUPS + N_EXPERTS].set(w_re)
    br = jnp.zeros((1, ROUTE_W), F32).at[0, :N_GROUPS].set(b_rg).at[0, N_GROUPS:N_GROUPS + N_EXPERTS].set(b_re)
    wr_hi = wr.astype(BF16)
    wr_lo = (wr - wr_hi.astype(F32)).astype(BF16)
    wr = jnp.stack([wr_hi, wr_lo])
    args = (x, wqk, vo, ln_g.reshape(1, -1), ln_b.reshape(1, -1), wr, br)
    in_specs = [pl.BlockSpec((1, ts, d), lambda b, j: (b, j, 0)),
                pl.BlockSpec((1, d, hm), lambda b, j: (b, 0, 0)),
                pl.BlockSpec((1, hm, d), lambda b, j: (b, 0, 0))] + [_full(a.shape) for a in args[3:]]
    return pl.pallas_call(
        _xattn_kernel,
        grid=(bsz, seq // ts),
        in_specs=in_specs,
        out_specs=[pl.BlockSpec((1, ts, d), lambda b, j: (b, j, 0)),
                   pl.BlockSpec((1, ts, d // 2), lambda b, j: (b, j, 0)),
                   pl.BlockSpec((1, ts, ROUTE_W), lambda b, j: (b, j, 0)),
                   pl.BlockSpec((1, 1, ROUTE_W), lambda b, j: (b, 0, 0))],
        out_shape=[jax.ShapeDtypeStruct((bsz, seq, d), F32),
                   jax.ShapeDtypeStruct((bsz, seq, d // 2), jnp.int32),
                   jax.ShapeDtypeStruct((bsz, seq, ROUTE_W), F32),
                   jax.ShapeDtypeStruct((bsz, 1, ROUTE_W), F32)],
        compiler_params=_cparams(("arbitrary", "arbitrary")),
        name="xattn_router",
    )(*args)


def _moe_output(x2, g0, g1, rec, ln_g, ln_b):
    a_lo, a_hi = _unpack_halves(g0)
    b_lo, b_hi = _unpack_halves(g1)
    w0, w1 = rec[:, 2:3], rec[:, 3:4]
    y = jnp.concatenate([w0 * a_lo + w1 * b_lo, w0 * a_hi + w1 * b_hi], axis=1)
    return _ln(DN_ALPHA * x2 + y, ln_g, ln_b)


def _swa_kernel(x_ref, g0_ref, g1_ref, route_ref, pln_g_ref, pln_b_ref,
                wqkv_ref, bqkv_ref, cos_ref, sin_ref, sink_ref, wo_ref, ln_g_ref, ln_b_ref,
                o_ref, kprev, vprev, *, ts, cq, ckv):
    j = pl.program_id(1)
    nb = ts // CHUNK
    dh = C_HEAD_DIM
    grp = (cq // dh) // C_KV_HEADS

    @pl.when(j == 0)
    def _():
        kprev[...] = jnp.zeros_like(kprev)
        vprev[...] = jnp.zeros_like(vprev)

    x = _moe_output(x_ref[0], g0_ref[0, 0], g1_ref[0, 0], route_ref[0], pln_g_ref[...], pln_b_ref[...])
    qkv = _dot(x.astype(BF16), wqkv_ref[...]) + bqkv_ref[...]
    cos = cos_ref[...]
    sin = sin_ref[...]
    lane = lax.broadcasted_iota(jnp.int32, (ts, LANES), 1)
    first_half = (lane % dh) < (dh // 2)

    def rope(t):
        outs = []
        for c in range(t.shape[1] // LANES):
            tc = t[:, c * LANES:(c + 1) * LANES]
            rot = jnp.where(first_half, pltpu.roll(tc, LANES - dh // 2, 1), pltpu.roll(tc, dh // 2, 1))
            outs.append(tc * cos + rot * sin)
        return jnp.concatenate(outs, axis=1)

    q = rope(qkv[:, :cq]).astype(BF16)
    k = rope(qkv[:, cq:cq + 2 * ckv]).astype(BF16)
    v = qkv[:, cq + 2 * ckv:].astype(BF16)

    r_i = lax.broadcasted_iota(jnp.int32, (CHUNK, 2 * CHUNK), 0)
    c_i = lax.broadcasted_iota(jnp.int32, (CHUNK, 2 * CHUNK), 1)
    band = jnp.logical_and(c_i > r_i, c_i <= r_i + CHUNK)
    sink_col = c_i == 0
    lane_k = lax.broadcasted_iota(jnp.int32, (2 * CHUNK, LANES), 1)
    key_row = lax.broadcasted_iota(jnp.int32, (2 * CHUNK, LANES), 0)
    lane_q = lax.broadcasted_iota(jnp.int32, (CHUNK, LANES), 1)
    ones_blk = jnp.ones((2 * CHUNK, LANES), BF16)
    zero_b = jnp.zeros((), BF16)
    blocks = []
    for c in range(nb):
        sl = slice(c * CHUNK, (c + 1) * CHUNK)
        if c == 0:
            kb = jnp.concatenate([kprev[...].astype(BF16), k[sl]], axis=0)
            vb = jnp.concatenate([vprev[...].astype(BF16), v[sl]], axis=0)
            first_key = jnp.where(j > 0, 0, CHUNK)
            valid = jnp.logical_and(band, c_i >= first_key)
        else:
            kb = k[(c - 1) * CHUNK:(c + 1) * CHUNK]
            vb = v[(c - 1) * CHUNK:(c + 1) * CHUNK]
            valid = band
        tiles = []
        for h in range(C_KV_HEADS):
            kd = kb[:, h * LANES:(h + 1) * LANES]
            vd = vb[:, h * LANES:(h + 1) * LANES]
            k_lo = jnp.where(lane_k < dh, kd, zero_b)
            k_hi = jnp.where(lane_k >= dh, kd, zero_b)
            vz = jnp.where(key_row == 0, zero_b, vd)
            q2 = jnp.concatenate([q[sl, (2 * h) * LANES:(2 * h + 1) * LANES],
                                  q[sl, (2 * h + 1) * LANES:(2 * h + 2) * LANES]], axis=0)
            s_lo = _dot_nt(q2, k_lo)
            s_hi = _dot_nt(q2, k_hi)
            parts = []
            for qk, g in ((s_lo[:CHUNK], 0), (s_lo[CHUNK:], 2), (s_hi[:CHUNK], 1), (s_hi[CHUNK:], 3)):
                sink = sink_ref[h * grp + g]
                parts.append(jnp.where(valid, qk, jnp.where(sink_col, sink, NEG)))
            s = jnp.concatenate(parts, axis=0)
            p = jnp.exp2(s - jnp.max(s, axis=1, keepdims=True)).astype(BF16)
            den = _dot(p, ones_blk)
            o2 = _dot(p, vz) / den
            tiles.append(jnp.where(lane_q < dh, o2[:CHUNK], o2[2 * CHUNK:3 * CHUNK]))
            tiles.append(jnp.where(lane_q < dh, o2[CHUNK:2 * CHUNK], o2[3 * CHUNK:]))
        blocks.append(jnp.concatenate(tiles, axis=1))
    kprev[...] = k[(nb - 1) * CHUNK:].astype(F32)
    vprev[...] = v[(nb - 1) * CHUNK:].astype(F32)
    att = (jnp.concatenate(blocks, axis=0) if nb > 1 else blocks[0]).astype(BF16)
    y = _dot(att, wo_ref[...])
    o_ref[0] = _ln(DN_ALPHA * x + y, ln_g_ref[...], ln_b_ref[...])


def swa_mixer_layer(pending, w_qkv, b_qkv, sinks, w_o, ln_g, ln_b, *, ts):
    x, g, route, pln_g, pln_b = pending
    bsz, seq, d = x.shape
    g = g.reshape(2, bsz, seq, d // 2)
    cq = w_o.shape[0]
    ckv = (w_qkv.shape[1] - cq) // 2
    dh = C_HEAD_DIM
    inv = ROPE_THETA ** (-jnp.arange(0, dh, 2, dtype=F32) / dh)
    ang = jnp.arange(seq, dtype=F32)[:, None] * inv[None, :]
    reps = LANES // (dh // 2)
    sign = jnp.tile(jnp.concatenate([-jnp.ones((dh // 2,), F32), jnp.ones((dh // 2,), F32)]), LANES // dh)
    cos_t = jnp.tile(jnp.cos(ang), (1, reps))
    sin_t = jnp.tile(jnp.sin(ang), (1, reps)) * sign[None, :]
    assert 2 * dh == LANES

    def dup_heads(t):
        th = t.reshape(t.shape[:-1] + (ckv // dh, dh))
        return jnp.concatenate([th, th], axis=-1).reshape(t.shape[:-1] + (2 * ckv,))

    qs = dh ** -0.5 * LOG2E
    w_all = jnp.concatenate([w_qkv[:, :cq] * qs, dup_heads(w_qkv[:, cq:cq + ckv]),
                             dup_heads(w_qkv[:, cq + ckv:])], axis=1)
    b_all = jnp.concatenate([b_qkv[:cq] * qs, dup_heads(b_qkv[cq:cq + ckv]), dup_heads(b_qkv[cq + ckv:])])
    args = (x, g, g, route, pln_g.reshape(1, -1), pln_b.reshape(1, -1),
            w_all.astype(BF16), b_all.reshape(1, -1), cos_t, sin_t, sinks.astype(F32) * LOG2E,
            w_o.astype(BF16), ln_g.reshape(1, -1), ln_b.reshape(1, -1))
    in_specs = [pl.BlockSpec((1, ts, d), lambda b, j: (b, j, 0)),
                pl.BlockSpec((1, 1, ts, d // 2), lambda b, j: (0, b, j, 0)),
                pl.BlockSpec((1, 1, ts, d // 2), lambda b, j: (1, b, j, 0)),
                pl.BlockSpec((1, ts, ROUTE_W), lambda b, j: (b, j, 0)),
                _full((1, d)), _full((1, d)),
                _full(args[6].shape), _full(args[7].shape),
                pl.BlockSpec((ts, LANES), lambda b, j: (j, 0)),
                pl.BlockSpec((ts, LANES), lambda b, j: (j, 0)),
                pl.BlockSpec(memory_space=pltpu.SMEM),
                _full(args[11].shape), _full(args[12].shape), _full(args[13].shape)]
    return pl.pallas_call(
        functools.partial(_swa_kernel, ts=ts, cq=cq, ckv=ckv),
        grid=(bsz, seq // ts),
        in_specs=in_specs,
        out_specs=pl.BlockSpec((1, ts, d), lambda b, j: (b, j, 0)),
        out_shape=jax.ShapeDtypeStruct((bsz, seq, d), F32),
        scratch_shapes=[pltpu.VMEM((CHUNK, 2 * ckv), F32), pltpu.VMEM((CHUNK, 2 * ckv), F32)],
        compiler_params=_cparams(("arbitrary", "arbitrary")),
        name="swa_mixer",
    )(*args)


def _slot_kernel(route_ref, pstart_ref, dest_ref, carry_ref, *, tb):
    @pl.when(pl.program_id(0) == 0)
    def _():
        carry_ref[...] = pstart_ref[...]

    rec = route_ref[...]
    lane = lax.broadcasted_iota(jnp.int32, (tb, ROUTE_W), 1)
    e0 = rec[:, 0:1].astype(jnp.int32)
    e1 = rec[:, 1:2].astype(jnp.int32)
    oh0 = lane == e0
    oh1 = lane == e1
    ohs = jnp.where(jnp.logical_or(oh0, oh1), 1.0, 0.0)
    r = lax.broadcasted_iota(jnp.int32, (tb, tb), 0)
    c = lax.broadcasted_iota(jnp.int32, (tb, tb), 1)
    before = jnp.where(c < r, 1.0, 0.0).astype(BF16)
    prefix = _dot(before, ohs.astype(BF16)) + carry_ref[...]
    d0 = jnp.sum(jnp.where(oh0, prefix, 0.0), axis=1, keepdims=True)
    d1 = jnp.sum(jnp.where(oh1, prefix, 0.0), axis=1, keepdims=True)
    dest = jnp.where(lane == 0, d0, jnp.where(lane == 1, d1, 0.0))
    dest_ref[...] = dest.T[:DEST_ROWS].astype(jnp.int32)
    carry_ref[...] += jnp.sum(ohs, axis=0, keepdims=True)


def moe_slots(route, pstart, *, tb):
    n = route.shape[0]
    return pl.pallas_call(
        functools.partial(_slot_kernel, tb=tb),
        grid=(n // tb,),
        in_specs=[pl.BlockSpec((tb, ROUTE_W), lambda i: (i, 0)), _full((1, ROUTE_W))],
        out_specs=pl.BlockSpec((DEST_ROWS, tb), lambda i: (0, i)),
        out_shape=jax.ShapeDtypeStruct((DEST_ROWS, n), jnp.int32),
        scratch_shapes=[pltpu.VMEM((1, ROUTE_W), F32)],
        compiler_params=_cparams(("arbitrary",)),
        name="moe_slots",
    )(route, pstart)


def _ffn_kernel(blk_exp_ref, new_exp_ref, nblk_ref, xs_ref, w1_ref, w3_ref, w2_ref, ys_ref,
                w1_b, w3_b, w2_b):
    i = pl.program_id(0)
    used = i < nblk_ref[0]

    @pl.when(jnp.logical_and(used, new_exp_ref[i] == 1))
    def _():
        w1_b[...] = w1_ref[0, 0].astype(BF16)
        w3_b[...] = w3_ref[0, 0].astype(BF16)
        w2_b[...] = w2_ref[0, 0].astype(BF16)

    @pl.when(used)
    def _():
        x_lo, x_hi = _unpack_halves(xs_ref[...])
        x_lo, x_hi = x_lo.astype(BF16), x_hi.astype(BF16)
        dl = x_lo.shape[1]
        h1 = _dot(x_lo, w1_b[:dl, :]) + _dot(x_hi, w1_b[dl:, :])
        h3 = _dot(x_lo, w3_b[:dl, :]) + _dot(x_hi, w3_b[dl:, :])
        h = (_silu(h1) * h3).astype(BF16)
        ys_ref[...] = _pack_halves(_dot(h, w2_b[...]))

    @pl.when(jnp.logical_not(used))
    def _():
        ys_ref[...] = jnp.zeros_like(ys_ref)


def moe_ffn(xs, blk_exp, new_exp, nblk, w1, w3, w2, *, layer, bm):
    n_pad, dp = xs.shape
    d = 2 * dp
    de = w1.shape[3]
    n_blk = n_pad // bm

    def x_map(i, be, ne, nb):
        return (jnp.minimum(i, nb[0] - 1), 0)

    def w_map(i, be, ne, nb):
        return (layer, be[i], 0, 0)

    return pl.pallas_call(
        _ffn_kernel,
        grid_spec=pltpu.PrefetchScalarGridSpec(
            num_scalar_prefetch=3,
            grid=(n_blk,),
            in_specs=[pl.BlockSpec((bm, dp), x_map),
                      pl.BlockSpec((1, 1, d, de), w_map),
                      pl.BlockSpec((1, 1, d, de), w_map),
                      pl.BlockSpec((1, 1, de, d), w_map)],
            out_specs=pl.BlockSpec((bm, dp), lambda i, be, ne, nb: (i, 0)),
            scratch_shapes=[pltpu.VMEM((d, de), BF16), pltpu.VMEM((d, de), BF16),
                            pltpu.VMEM((de, d), BF16)]),
        out_shape=jax.ShapeDtypeStruct((n_pad, dp), jnp.int32),
        compiler_params=_cparams(("arbitrary",)),
        name="moe_ffn",
    )(blk_exp, new_exp, nblk, xs, w1, w3, w2)


def _combine_kernel(x_ref, g0_ref, g1_ref, route_ref, ln_g_ref, ln_b_ref, o_ref):
    o_ref[...] = _moe_output(x_ref[...], g0_ref[0], g1_ref[0], route_ref[...], ln_g_ref[...], ln_b_ref[...])


def moe_combine(x, g, route, ln_g, ln_b, *, tb):
    n, d = x.shape
    row_spec = pl.BlockSpec((tb, d), lambda i: (i, 0))
    return pl.pallas_call(
        _combine_kernel,
        grid=(n // tb,),
        in_specs=[row_spec,
                  pl.BlockSpec((1, tb, d // 2), lambda i: (0, i, 0)),
                  pl.BlockSpec((1, tb, d // 2), lambda i: (1, i, 0)),
                  pl.BlockSpec((tb, ROUTE_W), lambda i: (i, 0)),
                  _full((1, d)), _full((1, d))],
        out_specs=row_spec,
        out_shape=jax.ShapeDtypeStruct((n, d), F32),
        compiler_params=_cparams(("arbitrary",)),
        name="moe_combine",
    )(x, g, g, route, ln_g.reshape(1, -1), ln_b.reshape(1, -1))


MOE_BM = 512
MIXER_TS = 512


def hierarchical_moe_layer(x2, xp, route, counts, w1, w3, w2, ln_g, ln_b, *, layer, defer_combine):
    bsz, seq, d = x2.shape
    n = bsz * seq
    bm = MOE_BM
    rt = route.reshape(n, ROUTE_W)
    tb = min(512, n)
    n_blk = (2 * n) // bm + N_EXPERTS
    cnt = counts.sum(axis=0)[0, :N_EXPERTS].astype(jnp.int32)
    pcnt = (cnt + bm - 1) // bm * bm
    pends = jnp.cumsum(pcnt)
    pstart = pends - pcnt
    nblk = (pends[-1] // bm).astype(jnp.int32).reshape(1)
    blk_row = jnp.arange(n_blk, dtype=jnp.int32) * bm
    blk_exp = jnp.minimum(jnp.sum((pends[None, :] <= blk_row[:, None]).astype(jnp.int32), axis=1),
                          N_EXPERTS - 1)
    last_exp = blk_exp[jnp.maximum(nblk[0] - 1, 0)]
    blk_exp = jnp.where(jnp.arange(n_blk) < nblk[0], blk_exp, last_exp)
    new_exp = jnp.concatenate([jnp.ones((1,), jnp.int32),
                               (blk_exp[1:] != blk_exp[:-1]).astype(jnp.int32)])
    pstart_rec = jnp.zeros((1, ROUTE_W), F32).at[0, :N_EXPERTS].set(pstart.astype(F32))
    dest = moe_slots(rt, pstart_rec, tb=tb)
    xs = moe_dispatch(xp.reshape(n, d // 2), dest[0], dest[1], n_blk * bm)
    ys = moe_ffn(xs, blk_exp, new_exp, nblk, w1, w3, w2, layer=layer, bm=bm)
    g = moe_gather(ys, dest[0], dest[1])
    if defer_combine:
        return x2, g, route, ln_g, ln_b
    return moe_combine(x2.reshape(n, d), g, rt, ln_g, ln_b, tb=tb).reshape(bsz, seq, d)


SC_ROWS = 128


def _sc_mesh():
    return plsc.VectorSubcoreMesh(core_axis_name="c", subcore_axis_name="s")


def moe_dispatch(xf, dest0, dest1, n_pad):
    n, d = xf.shape
    info = plsc.get_sparse_core_info()
    nw = info.num_cores * info.num_subcores
    per_w = n // nw
    r = min(SC_ROWS, per_w)

    def body(x_hbm, d0_hbm, d1_hbm, xs_hbm, i0_v, i1_v, rows_v, sem):
        wid = lax.axis_index("s") * info.num_cores + lax.axis_index("c")

        @pl.loop(0, per_w // r)
        def _(c):
            base = pl.multiple_of(wid * per_w + c * r, 8)
            pltpu.sync_copy(d0_hbm.at[pl.ds(base, r)], i0_v)
            pltpu.sync_copy(d1_hbm.at[pl.ds(base, r)], i1_v)
            pltpu.sync_copy(x_hbm.at[pl.ds(base, r)], rows_v)
            pltpu.async_copy(rows_v, xs_hbm.at[i0_v], sem).wait()
            pltpu.async_copy(rows_v, xs_hbm.at[i1_v], sem).wait()

    return pl.kernel(
        body, out_type=jax.ShapeDtypeStruct((n_pad, d), xf.dtype), mesh=_sc_mesh(),
        scratch_types=[pltpu.VMEM((r,), jnp.int32), pltpu.VMEM((r,), jnp.int32),
                       pltpu.VMEM((r, d), xf.dtype), pltpu.SemaphoreType.DMA],
        name="moe_dispatch",
    )(xf, dest0, dest1)


def moe_gather(ys, dest0, dest1):
    n = dest0.shape[0]
    d = ys.shape[1]
    info = plsc.get_sparse_core_info()
    nw = info.num_cores * info.num_subcores
    per_w = n // nw
    r = min(SC_ROWS, per_w)

    def body(ys_hbm, d0_hbm, d1_hbm, g_hbm, i_v, rows_v, sem):
        wid = lax.axis_index("s") * info.num_cores + lax.axis_index("c")

        @pl.loop(0, per_w // r)
        def _(c):
            base = pl.multiple_of(wid * per_w + c * r, 8)
            for k, d_hbm in enumerate((d0_hbm, d1_hbm)):
                pltpu.sync_copy(d_hbm.at[pl.ds(base, r)], i_v)
                pltpu.async_copy(ys_hbm.at[i_v], rows_v, sem).wait()
                pltpu.sync_copy(rows_v, g_hbm.at[k, pl.ds(base, r)])

    return pl.kernel(
        body, out_type=jax.ShapeDtypeStruct((2, n, d), ys.dtype), mesh=_sc_mesh(),
        scratch_types=[pltpu.VMEM((r,), jnp.int32), pltpu.VMEM((r, d), ys.dtype),
                       pltpu.SemaphoreType.DMA],
        name="moe_gather",
    )(ys, dest0, dest1)


def kernel(x, mem, ln_g, ln_b, ev_w_in, ev_gm_ln_g, ev_gm_ln_b, ev_gm_ws, ev_gm_bs, ev_conv_w, ev_conv_b, ev_wq, ev_wk, ev_wv, ev_w_if, ev_b_if, ev_norm_w, ev_skip, ev_w_out, od_w_qkv, od_b_qkv, od_sinks, od_w_o, xa_wq, xa_wkv, xa_wo, moe_w_rg, moe_b_rg, moe_w_re, moe_b_re, moe_w1, moe_w3, moe_w2):
    bsz, seq, d = x.shape
    depth = ln_g.shape[0]
    ts = min(MIXER_TS, seq)
    for l in range(depth):
        if l % 2 == 0:
            e = l // 2
            x = even_mixer_layer(x, ev_w_in[e], ev_gm_ln_g[e], ev_gm_ln_b[e], ev_gm_ws[e], ev_gm_bs[e],
                                 ev_conv_w[e], ev_conv_b[e], ev_wq[e], ev_wk[e], ev_wv[e], ev_w_if[e],
                                 ev_b_if[e], ev_norm_w[e], ev_skip[e], ev_w_out[e],
                                 ln_g[l, 0], ln_b[l, 0], ts=ts)
        else:
            o = l // 2
            x = swa_mixer_layer(x, od_w_qkv[o], od_b_qkv[o], od_sinks[o], od_w_o[o],
                                ln_g[l, 0], ln_b[l, 0], ts=ts)
        wqk, vo = memory_fold(mem, xa_wkv[l], xa_wq[l], xa_wo[l])
        x, xp, route, counts = xattn_router_layer(x, wqk, vo, ln_g[l, 1], ln_b[l, 1],
                                                  moe_w_rg[l], moe_b_rg[l], moe_w_re[l], moe_b_re[l],
                                                  ts=min(XATTN_TS, seq))
        x = hierarchical_moe_layer(x, xp, route, counts, moe_w1, moe_w3, moe_w2,
                                   ln_g[l, 2], ln_b[l, 2], layer=l,
                                   defer_combine=(l + 1 < depth and (l + 1) % 2 == 1))
    return x
```

```python
import functools
import math

import jax
import jax.numpy as jnp
from jax import lax
from jax.experimental import pallas as pl
from jax.experimental.pallas import tpu as pltpu
from jax.experimental.pallas import tpu_sc as plsc

F32 = jnp.float32
BF16 = jnp.bfloat16

A_GROUPS = 4
CHUNK = 128
B_HEADS = 4
B_CONV = 4
C_HEAD_DIM = 64
C_KV_HEADS = 4
X_HEADS = 4
N_GROUPS = 4
EXPERTS_PER_GROUP = 8
N_EXPERTS = N_GROUPS * EXPERTS_PER_GROUP
ROPE_THETA = 10000.0
LN_EPS = 1e-5
DEPTH = 2
DN_ALPHA = (2 * DEPTH) ** 0.25

LANES = 128
VMEM_LIMIT = 48 * 1024 * 1024
NEG = -1e30


def _cparams(sem):
    return pltpu.CompilerParams(dimension_semantics=sem, vmem_limit_bytes=VMEM_LIMIT)


def _full(shape):
    nd = len(shape)
    return pl.BlockSpec(shape, lambda *_: (0,) * nd)


def _dot(a, b):
    return jnp.dot(a, b, preferred_element_type=F32)


def _dot_nt(a, b):
    return lax.dot_general(a, b, (((1,), (1,)), ((), ())), preferred_element_type=F32)


def _dot_tn(a, b):
    return lax.dot_general(a, b, (((0,), (0,)), ((), ())), preferred_element_type=F32)


def _split_dot(a, b_bf16):
    hi = a.astype(BF16)
    lo = (a - hi.astype(F32)).astype(BF16)
    return _dot(hi, b_bf16) + _dot(lo, b_bf16)


def _ln(x, g, b):
    mu = jnp.mean(x, axis=-1, keepdims=True)
    xc = x - mu
    var = jnp.mean(xc * xc, axis=-1, keepdims=True)
    return xc * lax.rsqrt(var + LN_EPS) * g + b


LOG2E = math.log2(math.e)


def _silu(x):
    return x * (1.0 / (1.0 + jnp.exp2(x * -LOG2E)))


def _gelu(x):
    return 0.5 * x * (1.0 + jnp.tanh(math.sqrt(2.0 / math.pi) * (x + 0.044715 * (x * x * x))))


def _pack_halves(x):
    c = x.shape[1] // 2
    lo = lax.bitcast_convert_type(x[:, :c].astype(BF16).astype(F32), jnp.uint32)
    hi = lax.bitcast_convert_type(x[:, c:].astype(BF16).astype(F32), jnp.uint32)
    return lax.bitcast_convert_type((lo >> 16) | hi, jnp.int32)


def _unpack_halves(p):
    u = lax.bitcast_convert_type(p, jnp.uint32)
    lo = lax.bitcast_convert_type(u << 16, F32)
    hi = lax.bitcast_convert_type(u & jnp.uint32(0xFFFF0000), F32)
    return lo, hi


def _log_sigmoid(x):
    return jnp.minimum(x, 0.0) - jnp.log(1.0 + jnp.exp(-jnp.abs(x)))


def _even_kernel(x_ref, w_in_ref, gm_g_ref, gm_b_ref, gm_w_ref, gm_bias_ref,
                 conv_w_ref, conv_b_ref, wq_ref, wk_ref, wv_ref, wif_t_ref, bif_t_ref,
                 norm_w_ref, skip_ref, w_out_ref, ln_g_ref, ln_b_ref,
                 o_ref,
                 xm_buf, ct_ref, m_ref, *, ts, aw, bw):
    dh = bw // B_HEADS
    agd = aw // A_GROUPS
    nck = ts // CHUNK
    pad = 8
    j = pl.program_id(1)

    @pl.when(j == 0)
    def _():
        xm_buf[0:pad, :] = jnp.zeros((pad, bw), F32)
        ct_ref[...] = jnp.zeros_like(ct_ref)
        m_ref[...] = jnp.zeros_like(m_ref)

    row = lax.broadcasted_iota(jnp.int32, (CHUNK, CHUNK), 0)
    col = lax.broadcasted_iota(jnp.int32, (CHUNK, CHUNK), 1)
    causal = col <= row
    diag = col == row
    triu = jnp.where(row <= col, 1.0, 0.0).astype(BF16)
    ones_blk = jnp.ones((CHUNK, LANES), BF16)

    x = x_ref[0]
    proj = _dot(x.astype(BF16), w_in_ref[...])
    a_u = _gelu(proj[:, :aw])
    a_v = _gelu(proj[:, aw:2 * aw])
    xm = proj[:, 2 * aw:2 * aw + bw]
    z = proj[:, 2 * aw + bw:]

    vn = _ln(a_v, gm_g_ref[...], gm_b_ref[...]).astype(BF16)
    ya_chunks = []
    for c in range(nck):
        cols = []
        for g in range(A_GROUPS):
            v_cg = vn[c * CHUNK:(c + 1) * CHUNK, g * agd:(g + 1) * agd]
            cols.append(_dot(gm_w_ref[g], v_cg))
        ya_chunks.append(jnp.concatenate(cols, axis=1) + gm_bias_ref[...])
    y_a = a_u * jnp.concatenate(ya_chunks, axis=0)

    xm_buf[pad:pad + ts, :] = xm
    conv = conv_b_ref[...] + conv_w_ref[B_CONV - 1:B_CONV, :] * xm
    for k in range(B_CONV - 1):
        sh = B_CONV - 1 - k
        conv = conv + conv_w_ref[k:k + 1, :] * xm_buf[pad - sh:pad - sh + ts, :]
    xm_buf[pad - (B_CONV - 1):pad, :] = xm_buf[pad + ts - (B_CONV - 1):pad + ts, :]
    xc = _silu(conv)
    xc_b = xc.astype(BF16)
    q = _dot_head_pairs(xc_b, wq_ref)
    k_ = _dot_head_pairs(xc_b, wk_ref)
    v = _dot_head_pairs(xm.astype(BF16), wv_ref)
    gate_in = jnp.concatenate([q, k_, v], axis=1).astype(BF16)
    gates_t = _dot_nt(wif_t_ref[...], gate_in) + bif_t_ref[...]
    ig_all = gates_t[:B_HEADS, :]
    lf_all = _log_sigmoid(gates_t[B_HEADS:, :])
    q_b = q.astype(BF16)
    k_b = (k_ * dh ** -0.5).astype(BF16)
    v_b = v.astype(BF16)
    gate_z = _silu(z)

    h_chunks = []
    for c in range(nck):
        sl = slice(c * CHUNK, (c + 1) * CHUNK)
        lf_r = lf_all[:, sl]
        b_r = _split_dot(lf_r, triu)
        a_r = ig_all[:, sl] - b_r
        heads = []
        for h in range(B_HEADS):
            hs = slice(h * dh, (h + 1) * dh)
            qh, kh = q_b[sl, hs], k_b[sl, hs]
            vaug = jnp.concatenate([v_b[sl, hs], ones_blk], axis=1)
            a_row = a_r[h:h + 1, :]
            m_row = m_ref[h:h + 1, :]
            amat = jnp.where(causal, a_row, NEG)
            mx = jnp.maximum(jnp.max(amat, axis=1, keepdims=True), m_row)
            w_intra = jnp.exp(amat - mx)
            w_state = jnp.exp(m_row - mx)
            s = _dot_nt(qh, kh) * w_intra
            caug = ct_ref[h]
            naug = (_dot(s.astype(BF16), vaug)
                    + jnp.concatenate([w_state, w_state], axis=1) * _dot(qh, caug.astype(BF16)))
            num, nq = naug[:, :dh], naug[:, dh:]
            b_t = _split_dot(jnp.where(causal, lf_r[h:h + 1, :], 0.0), ones_blk)
            hv = num / jnp.maximum(jnp.abs(nq), jnp.exp(-(b_t + mx)))
            bl = b_r[h:h + 1, CHUNK - 1:CHUNK]
            g_row = bl + a_row
            m_new = jnp.maximum(bl + m_row, jnp.max(g_row, axis=1, keepdims=True))
            decay = jnp.exp(bl + m_row - m_new)
            wg_row = jnp.exp(g_row - m_new)
            wg_t = _dot(jnp.where(diag, wg_row, 0.0).astype(BF16), ones_blk)
            kw = (kh.astype(F32) * wg_t).astype(BF16)
            ct_ref[h] = jnp.concatenate([decay, decay], axis=1) * caug + _dot_tn(kw, vaug)
            m_ref[h:h + 1, :] = m_new
            hc = hv - jnp.mean(hv, axis=1, keepdims=True)
            hn = hc * lax.rsqrt(jnp.mean(hc * hc, axis=1, keepdims=True) + LN_EPS)
            heads.append(hn)
        h_chunks.append(jnp.concatenate(heads, axis=1))
    hn_all = jnp.concatenate(h_chunks, axis=0) if nck > 1 else h_chunks[0]
    y_b = (hn_all * norm_w_ref[...] + skip_ref[...] * xc) * gate_z

    mix = jnp.concatenate([y_a, y_b], axis=1).astype(BF16)
    y = _dot(mix, w_out_ref[...])
    o_ref[0] = _ln(DN_ALPHA * x + y, ln_g_ref[...], ln_b_ref[...])


def _split_dot_left(a_bf16, b):
    hi = b.astype(BF16)
    lo = (b - hi.astype(F32)).astype(BF16)
    return _dot(a_bf16, hi) + _dot(a_bf16, lo)


def _block_diag_pairs(w):
    hh, d, _ = w.shape
    wp = w.reshape(hh // 2, 2, d, d)
    eye = jnp.eye(2, dtype=w.dtype)
    return jnp.einsum('pade,ab->padbe', wp, eye).reshape(hh // 2, 2 * d, 2 * d)


def _dot_head_pairs(x, w_ref):
    npair, w2, _ = w_ref.shape
    return jnp.concatenate([_dot(x[:, p * w2:(p + 1) * w2], w_ref[p]) for p in range(npair)], axis=1)


def even_mixer_layer(x, w_in, gm_ln_g, gm_ln_b, gm_ws, gm_bs, conv_w, conv_b, wq, wk, wv, w_if,
                     b_if, norm_w, skip, w_out, ln_g, ln_b, *, ts):
    bsz, seq, d = x.shape
    aw = gm_ln_g.shape[0]
    bw = conv_b.shape[0]
    agd = aw // A_GROUPS
    causal = jnp.tril(jnp.ones((CHUNK, CHUNK), dtype=bool))
    gm_w = jnp.where(causal[None], gm_ws, 0.0).astype(BF16)
    gm_bias = jnp.repeat(gm_bs.T, agd, axis=1)
    row = lambda a: a.reshape(1, -1)
    args = (x, w_in.astype(BF16), row(gm_ln_g), row(gm_ln_b), gm_w, gm_bias,
            conv_w, row(conv_b), _block_diag_pairs(wq).astype(BF16), _block_diag_pairs(wk).astype(BF16),
            _block_diag_pairs(wv).astype(BF16), w_if.T.astype(BF16),
            b_if.reshape(-1, 1), row(norm_w), row(skip), w_out.astype(BF16),
            row(ln_g), row(ln_b))
    in_specs = [pl.BlockSpec((1, ts, d), lambda b, j: (b, j, 0))] + [_full(a.shape) for a in args[1:]]
    dh = bw // B_HEADS
    assert dh == LANES and CHUNK == LANES
    return pl.pallas_call(
        functools.partial(_even_kernel, ts=ts, aw=aw, bw=bw),
        grid=(bsz, seq // ts),
        in_specs=in_specs,
        out_specs=pl.BlockSpec((1, ts, d), lambda b, j: (b, j, 0)),
        out_shape=jax.ShapeDtypeStruct((bsz, seq, d), F32),
        scratch_shapes=[pltpu.VMEM((8 + ts, bw), F32),
                        pltpu.VMEM((B_HEADS, dh, dh + LANES), F32),
                        pltpu.VMEM((B_HEADS, LANES), F32)],
        compiler_params=_cparams(("arbitrary", "arbitrary")),
        name="even_mixer",
    )(*args)


def _memfold_kernel(mem_ref, wkv_ref, wq_ref, wo_ref, wqk_ref, vo_ref, *, d):
    dh = d // X_HEADS
    m_len = mem_ref.shape[1]
    kv = _dot(mem_ref[0].astype(BF16), wkv_ref[...])
    k = (kv[:, :d] * (dh ** -0.5 * LOG2E)).astype(BF16)
    v = kv[:, d:].astype(BF16)
    for h in range(X_HEADS):
        hs = slice(h * dh, (h + 1) * dh)
        ms = slice(h * m_len, (h + 1) * m_len)
        wqk_ref[0, :, ms] = _dot_nt(wq_ref[:, hs], k[:, hs]).astype(BF16)
        vo_ref[0, ms, :] = _dot(v[:, hs], wo_ref[hs, :]).astype(BF16)


def memory_fold(mem, wkv, wq, wo):
    bsz, m_len, d = mem.shape
    hm = X_HEADS * m_len
    args = (mem, wkv.astype(BF16), wq.astype(BF16), wo.astype(BF16))
    return pl.pallas_call(
        functools.partial(_memfold_kernel, d=d),
        grid=(bsz,),
        in_specs=[pl.BlockSpec((1, m_len, d), lambda b: (b, 0, 0))] + [_full(a.shape) for a in args[1:]],
        out_specs=[pl.BlockSpec((1, d, hm), lambda b: (b, 0, 0)),
                   pl.BlockSpec((1, hm, d), lambda b: (b, 0, 0))],
        out_shape=[jax.ShapeDtypeStruct((bsz, d, hm), BF16),
                   jax.ShapeDtypeStruct((bsz, hm, d), BF16)],
        compiler_params=_cparams(("arbitrary",)),
        name="memory_fold",
    )(*args)


ROUTE_W = 128
XATTN_TS = 1024
DEST_ROWS = 8


def _xattn_kernel(x_ref, wqk_ref, vo_ref, ln_g_ref, ln_b_ref, wr_ref, br_ref,
                  o_ref, xp_ref, route_ref, cnt_ref):
    m_len = wqk_ref.shape[2] // X_HEADS

    @pl.when(pl.program_id(1) == 0)
    def _():
        cnt_ref[...] = jnp.zeros_like(cnt_ref)

    x = x_ref[0]
    ts = x.shape[0]
    xb = x.astype(BF16)
    probs = []
    for h in range(X_HEADS):
        s = _dot(xb, wqk_ref[0, :, h * m_len:(h + 1) * m_len])
        p = jnp.exp2(s - jnp.max(s, axis=1, keepdims=True))
        probs.append((p / jnp.sum(p, axis=1, keepdims=True)).astype(BF16))
    y = _dot(jnp.concatenate(probs, axis=1), vo_ref[0])
    x2 = _ln(DN_ALPHA * x + y, ln_g_ref[...], ln_b_ref[...])
    o_ref[0] = x2
    xp_ref[0] = _pack_halves(x2)

    x_hi = x2.astype(BF16)
    x_lo = (x2 - x_hi.astype(F32)).astype(BF16)
    logits = (_dot(x_hi, wr_ref[0]) + _dot(x_lo, wr_ref[0]) + _dot(x_hi, wr_ref[1])) + br_ref[...]
    lane = lax.broadcasted_iota(jnp.int32, (ts, ROUTE_W), 1)
    is_g = lane < N_GROUPS
    lg = jnp.where(is_g, logits, NEG)
    mg = jnp.max(lg, axis=1, keepdims=True)
    gi = jnp.min(jnp.where(jnp.logical_and(is_g, lg == mg), lane, ROUTE_W), axis=1, keepdims=True)
    gate_g = 1.0 / jnp.sum(jnp.where(is_g, jnp.exp(lg - mg), 0.0), axis=1, keepdims=True)
    lo = N_GROUPS + gi * EXPERTS_PER_GROUP
    in_grp = jnp.logical_and(lane >= lo, lane < lo + EXPERTS_PER_GROUP)
    le = jnp.where(in_grp, logits, NEG)
    v1 = jnp.max(le, axis=1, keepdims=True)
    i1 = jnp.min(jnp.where(jnp.logical_and(in_grp, le == v1), lane, ROUTE_W), axis=1, keepdims=True)
    le2 = jnp.where(lane == i1, NEG, le)
    v2 = jnp.max(le2, axis=1, keepdims=True)
    i2 = jnp.min(jnp.where(jnp.logical_and(in_grp, le2 == v2), lane, ROUTE_W), axis=1, keepdims=True)
    e21 = jnp.exp(v2 - v1)
    p1 = 1.0 / (1.0 + e21)
    p2 = e21 * p1
    e1 = (i1 - N_GROUPS).astype(F32)
    e2 = (i2 - N_GROUPS).astype(F32)
    rec = jnp.where(lane == 0, e1, 0.0)
    rec = jnp.where(lane == 1, e2, rec)
    rec = jnp.where(lane == 2, gate_g * p1, rec)
    rec = jnp.where(lane == 3, gate_g * p2, rec)
    route_ref[0] = rec
    sel = jnp.logical_or(lane == i1 - N_GROUPS, lane == i2 - N_GROUPS)
    cnt_ref[0] += jnp.sum(jnp.where(sel, 1.0, 0.0), axis=0, keepdims=True)


def xattn_router_layer(x, wqk, vo, ln_g, ln_b, w_rg, b_rg, w_re, b_re, *, ts):
    bsz, seq, d = x.shape
    hm = wqk.shape[2]
    wr = jnp.zeros((d, ROUTE_W), F32).at[:, :N_GROUPS].set(w_rg).at[:, N_GROUPS:N_GROUPS + N_EXPERTS].set(w_re)
    br = jnp.zeros((1, ROUTE_W), F32).at[0, :N_GROUPS].set(b_rg).at[0, N_GROUPS:N_GROUPS + N_EXPERTS].set(b_re)
    wr_hi = wr.astype(BF16)
    wr_lo = (wr - wr_hi.astype(F32)).astype(BF16)
    wr = jnp.stack([wr_hi, wr_lo])
    args = (x, wqk, vo, ln_g.reshape(1, -1), ln_b.reshape(1, -1), wr, br)
    in_specs = [pl.BlockSpec((1, ts, d), lambda b, j: (b, j, 0)),
                pl.BlockSpec((1, d, hm), lambda b, j: (b, 0, 0)),
                pl.BlockSpec((1, hm, d), lambda b, j: (b, 0, 0))] + [_full(a.shape) for a in args[3:]]
    return pl.pallas_call(
        _xattn_kernel,
        grid=(bsz, seq // ts),
        in_specs=in_specs,
        out_specs=[pl.BlockSpec((1, ts, d), lambda b, j: (b, j, 0)),
                   pl.BlockSpec((1, ts, d // 2), lambda b, j: (b, j, 0)),
                   pl.BlockSpec((1, ts, ROUTE_W), lambda b, j: (b, j, 0)),
                   pl.BlockSpec((1, 1, ROUTE_W), lambda b, j: (b, 0, 0))],
        out_shape=[jax.ShapeDtypeStruct((bsz, seq, d), F32),
                   jax.ShapeDtypeStruct((bsz, seq, d // 2), jnp.int32),
                   jax.ShapeDtypeStruct((bsz, seq, ROUTE_W), F32),
                   jax.ShapeDtypeStruct((bsz, 1, ROUTE_W), F32)],
        compiler_params=_cparams(("arbitrary", "arbitrary")),
        name="xattn_router",
    )(*args)


def _moe_output(x2, g0, g1, rec, ln_g, ln_b):
    a_lo, a_hi = _unpack_halves(g0)
    b_lo, b_hi = _unpack_halves(g1)
    w0, w1 = rec[:, 2:3], rec[:, 3:4]
    y = jnp.concatenate([w0 * a_lo + w1 * b_lo, w0 * a_hi + w1 * b_hi], axis=1)
    return _ln(DN_ALPHA * x2 + y, ln_g, ln_b)


def _swa_kernel(x_ref, g0_ref, g1_ref, route_ref, pln_g_ref, pln_b_ref,
                wqkv_ref, bqkv_ref, cos_ref, sin_ref, sink_ref, wo_ref, ln_g_ref, ln_b_ref,
                o_ref, kprev, vprev, *, ts, cq, ckv):
    j = pl.program_id(1)
    nb = ts // CHUNK
    dh = C_HEAD_DIM
    grp = (cq // dh) // C_KV_HEADS

    @pl.when(j == 0)
    def _():
        kprev[...] = jnp.zeros_like(kprev)
        vprev[...] = jnp.zeros_like(vprev)

    x = _moe_output(x_ref[0], g0_ref[0, 0], g1_ref[0, 0], route_ref[0], pln_g_ref[...], pln_b_ref[...])
    qkv = _dot(x.astype(BF16), wqkv_ref[...]) + bqkv_ref[...]
    cos = cos_ref[...]
    sin = sin_ref[...]
    lane = lax.broadcasted_iota(jnp.int32, (ts, LANES), 1)
    first_half = (lane % dh) < (dh // 2)

    def rope(t):
        outs = []
        for c in range(t.shape[1] // LANES):
            tc = t[:, c * LANES:(c + 1) * LANES]
            rot = jnp.where(first_half, pltpu.roll(tc, LANES - dh // 2, 1), pltpu.roll(tc, dh // 2, 1))
            outs.append(tc * cos + rot * sin)
        return jnp.concatenate(outs, axis=1)

    q = rope(qkv[:, :cq]).astype(BF16)
    k = rope(qkv[:, cq:cq + 2 * ckv]).astype(BF16)
    v = qkv[:, cq + 2 * ckv:].astype(BF16)

    r_i = lax.broadcasted_iota(jnp.int32, (CHUNK, 2 * CHUNK), 0)
    c_i = lax.broadcasted_iota(jnp.int32, (CHUNK, 2 * CHUNK), 1)
    band = jnp.logical_and(c_i > r_i, c_i <= r_i + CHUNK)
    sink_col = c_i == 0
    lane_k = lax.broadcasted_iota(jnp.int32, (2 * CHUNK, LANES), 1)
    key_row = lax.broadcasted_iota(jnp.int32, (2 * CHUNK, LANES), 0)
    lane_q = lax.broadcasted_iota(jnp.int32, (CHUNK, LANES), 1)
    ones_blk = jnp.ones((2 * CHUNK, LANES), BF16)
    zero_b = jnp.zeros((), BF16)
    blocks = []
    for c in range(nb):
        sl = slice(c * CHUNK, (c + 1) * CHUNK)
        if c == 0:
            kb = jnp.concatenate([kprev[...].astype(BF16), k[sl]], axis=0)
            vb = jnp.concatenate([vprev[...].astype(BF16), v[sl]], axis=0)
            first_key = jnp.where(j > 0, 0, CHUNK)
            valid = jnp.logical_and(band, c_i >= first_key)
        else:
            kb = k[(c - 1) * CHUNK:(c + 1) * CHUNK]
            vb = v[(c - 1) * CHUNK:(c + 1) * CHUNK]
            valid = band
        tiles = []
        for h in range(C_KV_HEADS):
            kd = kb[:, h * LANES:(h + 1) * LANES]
            vd = vb[:, h * LANES:(h + 1) * LANES]
            k_lo = jnp.where(lane_k < dh, kd, zero_b)
            k_hi = jnp.where(lane_k >= dh, kd, zero_b)
            vz = jnp.where(key_row == 0, zero_b, vd)
            q2 = jnp.concatenate([q[sl, (2 * h) * LANES:(2 * h + 1) * LANES],
                                  q[sl, (2 * h + 1) * LANES:(2 * h + 2) * LANES]], axis=0)
            s_lo = _dot_nt(q2, k_lo)
            s_hi = _dot_nt(q2, k_hi)
            parts = []
            for qk, g in ((s_lo[:CHUNK], 0), (s_lo[CHUNK:], 2), (s_hi[:CHUNK], 1), (s_hi[CHUNK:], 3)):
                sink = sink_ref[h * grp + g]
                parts.append(jnp.where(valid, qk, jnp.where(sink_col, sink, NEG)))
            s = jnp.concatenate(parts, axis=0)
            p = jnp.exp2(s - jnp.max(s, axis=1, keepdims=True)).astype(BF16)
            den = _dot(p, ones_blk)
            o2 = _dot(p, vz) / den
            tiles.append(jnp.where(lane_q < dh, o2[:CHUNK], o2[2 * CHUNK:3 * CHUNK]))
            tiles.append(jnp.where(lane_q < dh, o2[CHUNK:2 * CHUNK], o2[3 * CHUNK:]))
        blocks.append(jnp.concatenate(tiles, axis=1))
    kprev[...] = k[(nb - 1) * CHUNK:].astype(F32)
    vprev[...] = v[(nb - 1) * CHUNK:].astype(F32)
    att = (jnp.concatenate(blocks, axis=0) if nb > 1 else blocks[0]).astype(BF16)
    y = _dot(att, wo_ref[...])
    o_ref[0] = _ln(DN_ALPHA * x + y, ln_g_ref[...], ln_b_ref[...])


def swa_mixer_layer(pending, w_qkv, b_qkv, sinks, w_o, ln_g, ln_b, *, ts):
    x, g, route, pln_g, pln_b = pending
    bsz, seq, d = x.shape
    g = g.reshape(2, bsz, seq, d // 2)
    cq = w_o.shape[0]
    ckv = (w_qkv.shape[1] - cq) // 2
    dh = C_HEAD_DIM
    inv = ROPE_THETA ** (-jnp.arange(0, dh, 2, dtype=F32) / dh)
    ang = jnp.arange(seq, dtype=F32)[:, None] * inv[None, :]
    reps = LANES // (dh // 2)
    sign = jnp.tile(jnp.concatenate([-jnp.ones((dh // 2,), F32), jnp.ones((dh // 2,), F32)]), LANES // dh)
    cos_t = jnp.tile(jnp.cos(ang), (1, reps))
    sin_t = jnp.tile(jnp.sin(ang), (1, reps)) * sign[None, :]
    assert 2 * dh == LANES

    def dup_heads(t):
        th = t.reshape(t.shape[:-1] + (ckv // dh, dh))
        return jnp.concatenate([th, th], axis=-1).reshape(t.shape[:-1] + (2 * ckv,))

    qs = dh ** -0.5 * LOG2E
    w_all = jnp.concatenate([w_qkv[:, :cq] * qs, dup_heads(w_qkv[:, cq:cq + ckv]),
                             dup_heads(w_qkv[:, cq + ckv:])], axis=1)
    b_all = jnp.concatenate([b_qkv[:cq] * qs, dup_heads(b_qkv[cq:cq + ckv]), dup_heads(b_qkv[cq + ckv:])])
    args = (x, g, g, route, pln_g.reshape(1, -1), pln_b.reshape(1, -1),
            w_all.astype(BF16), b_all.reshape(1, -1), cos_t, sin_t, sinks.astype(F32) * LOG2E,
            w_o.astype(BF16), ln_g.reshape(1, -1), ln_b.reshape(1, -1))
    in_specs = [pl.BlockSpec((1, ts, d), lambda b, j: (b, j, 0)),
                pl.BlockSpec((1, 1, ts, d // 2), lambda b, j: (0, b, j, 0)),
                pl.BlockSpec((1, 1, ts, d // 2), lambda b, j: (1, b, j, 0)),
                pl.BlockSpec((1, ts, ROUTE_W), lambda b, j: (b, j, 0)),
                _full((1, d)), _full((1, d)),
                _full(args[6].shape), _full(args[7].shape),
                pl.BlockSpec((ts, LANES), lambda b, j: (j, 0)),
                pl.BlockSpec((ts, LANES), lambda b, j: (j, 0)),
                pl.BlockSpec(memory_space=pltpu.SMEM),
                _full(args[11].shape), _full(args[12].shape), _full(args[13].shape)]
    return pl.pallas_call(
        functools.partial(_swa_kernel, ts=ts, cq=cq, ckv=ckv),
        grid=(bsz, seq // ts),
        in_specs=in_specs,
        out_specs=pl.BlockSpec((1, ts, d), lambda b, j: (b, j, 0)),
        out_shape=jax.ShapeDtypeStruct((bsz, seq, d), F32),
        scratch_shapes=[pltpu.VMEM((CHUNK, 2 * ckv), F32), pltpu.VMEM((CHUNK, 2 * ckv), F32)],
        compiler_params=_cparams(("arbitrary", "arbitrary")),
        name="swa_mixer",
    )(*args)


def _slot_kernel(route_ref, pstart_ref, dest_ref, carry_ref, *, tb):
    @pl.when(pl.program_id(0) == 0)
    def _():
        carry_ref[...] = pstart_ref[...]

    rec = route_ref[...]
    lane = lax.broadcasted_iota(jnp.int32, (tb, ROUTE_W), 1)
    e0 = rec[:, 0:1].astype(jnp.int32)
    e1 = rec[:, 1:2].astype(jnp.int32)
    oh0 = lane == e0
    oh1 = lane == e1
    ohs = jnp.where(jnp.logical_or(oh0, oh1), 1.0, 0.0)
    r = lax.broadcasted_iota(jnp.int32, (tb, tb), 0)
    c = lax.broadcasted_iota(jnp.int32, (tb, tb), 1)
    before = jnp.where(c < r, 1.0, 0.0).astype(BF16)
    prefix = _dot(before, ohs.astype(BF16)) + carry_ref[...]
    d0 = jnp.sum(jnp.where(oh0, prefix, 0.0), axis=1, keepdims=True)
    d1 = jnp.sum(jnp.where(oh1, prefix, 0.0), axis=1, keepdims=True)
    dest = jnp.where(lane == 0, d0, jnp.where(lane == 1, d1, 0.0))
    dest_ref[...] = dest.T[:DEST_ROWS].astype(jnp.int32)
    carry_ref[...] += jnp.sum(ohs, axis=0, keepdims=True)


def moe_slots(route, pstart, *, tb):
    n = route.shape[0]
    return pl.pallas_call(
        functools.partial(_slot_kernel, tb=tb),
        grid=(n // tb,),
        in_specs=[pl.BlockSpec((tb, ROUTE_W), lambda i: (i, 0)), _full((1, ROUTE_W))],
        out_specs=pl.BlockSpec((DEST_ROWS, tb), lambda i: (0, i)),
        out_shape=jax.ShapeDtypeStruct((DEST_ROWS, n), jnp.int32),
        scratch_shapes=[pltpu.VMEM((1, ROUTE_W), F32)],
        compiler_params=_cparams(("arbitrary",)),
        name="moe_slots",
    )(route, pstart)


def _ffn_kernel(blk_exp_ref, new_exp_ref, nblk_ref, xs_ref, w1_ref, w3_ref, w2_ref, ys_ref,
                w1_b, w3_b, w2_b):
    i = pl.program_id(0)
    used = i < nblk_ref[0]

    @pl.when(jnp.logical_and(used, new_exp_ref[i] == 1))
    def _():
        w1_b[...] = w1_ref[0, 0].astype(BF16)
        w3_b[...] = w3_ref[0, 0].astype(BF16)
        w2_b[...] = w2_ref[0, 0].astype(BF16)

    @pl.when(used)
    def _():
        x_lo, x_hi = _unpack_halves(xs_ref[...])
        x_lo, x_hi = x_lo.astype(BF16), x_hi.astype(BF16)
        dl = x_lo.shape[1]
        h1 = _dot(x_lo, w1_b[:dl, :]) + _dot(x_hi, w1_b[dl:, :])
        h3 = _dot(x_lo, w3_b[:dl, :]) + _dot(x_hi, w3_b[dl:, :])
        h = (_silu(h1) * h3).astype(BF16)
        ys_ref[...] = _pack_halves(_dot(h, w2_b[...]))

    @pl.when(jnp.logical_not(used))
    def _():
        ys_ref[...] = jnp.zeros_like(ys_ref)


def moe_ffn(xs, blk_exp, new_exp, nblk, w1, w3, w2, *, layer, bm):
    n_pad, dp = xs.shape
    d = 2 * dp
    de = w1.shape[3]
    n_blk = n_pad // bm

    def x_map(i, be, ne, nb):
        return (jnp.minimum(i, nb[0] - 1), 0)

    def w_map(i, be, ne, nb):
        return (layer, be[i], 0, 0)

    return pl.pallas_call(
        _ffn_kernel,
        grid_spec=pltpu.PrefetchScalarGridSpec(
            num_scalar_prefetch=3,
            grid=(n_blk,),
            in_specs=[pl.BlockSpec((bm, dp), x_map),
                      pl.BlockSpec((1, 1, d, de), w_map),
                      pl.BlockSpec((1, 1, d, de), w_map),
                      pl.BlockSpec((1, 1, de, d), w_map)],
            out_specs=pl.BlockSpec((bm, dp), lambda i, be, ne, nb: (i, 0)),
            scratch_shapes=[pltpu.VMEM((d, de), BF16), pltpu.VMEM((d, de), BF16),
                            pltpu.VMEM((de, d), BF16)]),
        out_shape=jax.ShapeDtypeStruct((n_pad, dp), jnp.int32),
        compiler_params=_cparams(("arbitrary",)),
        name="moe_ffn",
    )(blk_exp, new_exp, nblk, xs, w1, w3, w2)


def _combine_kernel(x_ref, g0_ref, g1_ref, route_ref, ln_g_ref, ln_b_ref, o_ref):
    o_ref[...] = _moe_output(x_ref[...], g0_ref[0], g1_ref[0], route_ref[...], ln_g_ref[...], ln_b_ref[...])


def moe_combine(x, g, route, ln_g, ln_b, *, tb):
    n, d = x.shape
    row_spec = pl.BlockSpec((tb, d), lambda i: (i, 0))
    return pl.pallas_call(
        _combine_kernel,
        grid=(n // tb,),
        in_specs=[row_spec,
                  pl.BlockSpec((1, tb, d // 2), lambda i: (0, i, 0)),
                  pl.BlockSpec((1, tb, d // 2), lambda i: (1, i, 0)),
                  pl.BlockSpec((tb, ROUTE_W), lambda i: (i, 0)),
                  _full((1, d)), _full((1, d))],
        out_specs=row_spec,
        out_shape=jax.ShapeDtypeStruct((n, d), F32),
        compiler_params=_cparams(("arbitrary",)),
        name="moe_combine",
    )(x, g, g, route, ln_g.reshape(1, -1), ln_b.reshape(1, -1))


MOE_BM = 512
MIXER_TS = 512


def hierarchical_moe_layer(x2, xp, route, counts, w1, w3, w2, ln_g, ln_b, *, layer, defer_combine):
    bsz, seq, d = x2.shape
    n = bsz * seq
    bm = MOE_BM
    rt = route.reshape(n, ROUTE_W)
    tb = min(512, n)
    n_blk = (2 * n) // bm + N_EXPERTS
    cnt = counts.sum(axis=0)[0, :N_EXPERTS].astype(jnp.int32)
    pcnt = (cnt + bm - 1) // bm * bm
    pends = jnp.cumsum(pcnt)
    pstart = pends - pcnt
    nblk = (pends[-1] // bm).astype(jnp.int32).reshape(1)
    blk_row = jnp.arange(n_blk, dtype=jnp.int32) * bm
    blk_exp = jnp.minimum(jnp.sum((pends[None, :] <= blk_row[:, None]).astype(jnp.int32), axis=1),
                          N_EXPERTS - 1)
    last_exp = blk_exp[jnp.maximum(nblk[0] - 1, 0)]
    blk_exp = jnp.where(jnp.arange(n_blk) < nblk[0], blk_exp, last_exp)
    new_exp = jnp.concatenate([jnp.ones((1,), jnp.int32),
                               (blk_exp[1:] != blk_exp[:-1]).astype(jnp.int32)])
    pstart_rec = jnp.zeros((1, ROUTE_W), F32).at[0, :N_EXPERTS].set(pstart.astype(F32))
    dest = moe_slots(rt, pstart_rec, tb=tb)
    xs = moe_dispatch(xp.reshape(n, d // 2), dest[0], dest[1], n_blk * bm)
    ys = moe_ffn(xs, blk_exp, new_exp, nblk, w1, w3, w2, layer=layer, bm=bm)
    g = moe_gather(ys, dest[0], dest[1])
    if defer_combine:
        return x2, g, route, ln_g, ln_b
    return moe_combine(x2.reshape(n, d), g, rt, ln_g, ln_b, tb=tb).reshape(bsz, seq, d)


SC_ROWS = 64
SC_SLOTS = 2


def _sc_mesh():
    return plsc.VectorSubcoreMesh(core_axis_name="c", subcore_axis_name="s")


def moe_dispatch(xf, dest0, dest1, n_pad):
    n, d = xf.shape
    info = plsc.get_sparse_core_info()
    nw = info.num_cores * info.num_subcores
    per_w = n // nw
    r = min(SC_ROWS, per_w)
    nchunk = per_w // r

    def body(x_hbm, d0_hbm, d1_hbm, xs_hbm, *scratch):
        i0_v, i1_v = scratch[0:SC_SLOTS], scratch[SC_SLOTS:2 * SC_SLOTS]
        rows_v, sems = scratch[2 * SC_SLOTS:3 * SC_SLOTS], scratch[3 * SC_SLOTS]
        wid = lax.axis_index("s") * info.num_cores + lax.axis_index("c")
        pending = {}
        for c in range(nchunk):
            s = c % SC_SLOTS
            for cp in pending.pop(s, ()):
                cp.wait()
            base = pl.multiple_of(wid * per_w + c * r, 8)
            pltpu.sync_copy(d0_hbm.at[pl.ds(base, r)], i0_v[s])
            pltpu.sync_copy(d1_hbm.at[pl.ds(base, r)], i1_v[s])
            pltpu.sync_copy(x_hbm.at[pl.ds(base, r)], rows_v[s])
            pending[s] = (pltpu.async_copy(rows_v[s], xs_hbm.at[i0_v[s]], sems.at[2 * s]),
                          pltpu.async_copy(rows_v[s], xs_hbm.at[i1_v[s]], sems.at[2 * s + 1]))
        for cps in pending.values():
            for cp in cps:
                cp.wait()

    return pl.kernel(
        body, out_type=jax.ShapeDtypeStruct((n_pad, d), xf.dtype), mesh=_sc_mesh(),
        scratch_types=([pltpu.VMEM((r,), jnp.int32)] * (2 * SC_SLOTS) + [pltpu.VMEM((r, d), xf.dtype)] * SC_SLOTS
                       + [pltpu.SemaphoreType.DMA((2 * SC_SLOTS,))]),
        name="moe_dispatch",
    )(xf, dest0, dest1)


def moe_gather(ys, dest0, dest1):
    n = dest0.shape[0]
    d = ys.shape[1]
    info = plsc.get_sparse_core_info()
    nw = info.num_cores * info.num_subcores
    per_w = n // nw
    r = min(SC_ROWS, per_w)
    nchunk = per_w // r

    def body(ys_hbm, d0_hbm, d1_hbm, g_hbm, *scratch):
        i_v, rows_v, sems = scratch[0:SC_SLOTS], scratch[SC_SLOTS:2 * SC_SLOTS], scratch[2 * SC_SLOTS]
        wid = lax.axis_index("s") * info.num_cores + lax.axis_index("c")
        pending = {}
        items = [(c, k) for c in range(nchunk) for k in range(2)]
        for j, (c, k) in enumerate(items):
            s = j % SC_SLOTS
            if s in pending:
                pending.pop(s).wait()
            base = pl.multiple_of(wid * per_w + c * r, 8)
            pltpu.sync_copy((d0_hbm, d1_hbm)[k].at[pl.ds(base, r)], i_v[s])
            pltpu.async_copy(ys_hbm.at[i_v[s]], rows_v[s], sems.at[2 * s]).wait()
            pending[s] = pltpu.async_copy(rows_v[s], g_hbm.at[k, pl.ds(base, r)], sems.at[2 * s + 1])
        for cp in pending.values():
            cp.wait()

    return pl.kernel(
        body, out_type=jax.ShapeDtypeStruct((2, n, d), ys.dtype), mesh=_sc_mesh(),
        scratch_types=([pltpu.VMEM((r,), jnp.int32)] * SC_SLOTS + [pltpu.VMEM((r, d), ys.dtype)] * SC_SLOTS
                       + [pltpu.SemaphoreType.DMA((2 * SC_SLOTS,))]),
        name="moe_gather",
    )(ys, dest0, dest1)


def kernel(x, mem, ln_g, ln_b, ev_w_in, ev_gm_ln_g, ev_gm_ln_b, ev_gm_ws, ev_gm_bs, ev_conv_w, ev_conv_b, ev_wq, ev_wk, ev_wv, ev_w_if, ev_b_if, ev_norm_w, ev_skip, ev_w_out, od_w_qkv, od_b_qkv, od_sinks, od_w_o, xa_wq, xa_wkv, xa_wo, moe_w_rg, moe_b_rg, moe_w_re, moe_b_re, moe_w1, moe_w3, moe_w2):
    bsz, seq, d = x.shape
    depth = ln_g.shape[0]
    ts = min(MIXER_TS, seq)
    for l in range(depth):
        if l % 2 == 0:
            e = l // 2
            x = even_mixer_layer(x, ev_w_in[e], ev_gm_ln_g[e], ev_gm_ln_b[e], ev_gm_ws[e], ev_gm_bs[e],
                                 ev_conv_w[e], ev_conv_b[e], ev_wq[e], ev_wk[e], ev_wv[e], ev_w_if[e],
                                 ev_b_if[e], ev_norm_w[e], ev_skip[e], ev_w_out[e],
                                 ln_g[l, 0], ln_b[l, 0], ts=ts)
        else:
            o = l // 2
            x = swa_mixer_layer(x, od_w_qkv[o], od_b_qkv[o], od_sinks[o], od_w_o[o],
                                ln_g[l, 0], ln_b[l, 0], ts=ts)
        wqk, vo = memory_fold(mem, xa_wkv[l], xa_wq[l], xa_wo[l])
        x, xp, route, counts = xattn_router_layer(x, wqk, vo, ln_g[l, 1], ln_b[l, 1],
                                                  moe_w_rg[l], moe_b_rg[l], moe_w_re[l], moe_b_re[l],
                                                  ts=min(XATTN_TS, seq))
        x = hierarchical_moe_layer(x, xp, route, counts, moe_w1, moe_w3, moe_w2,
                                   ln_g[l, 2], ln_b[l, 2], layer=l,
                                   defer_combine=(l + 1 < depth and (l + 1) % 2 == 1))
    return x
```

```python
import functools
import math

import jax
import jax.numpy as jnp
from jax import lax
from jax.experimental import pallas as pl
from jax.experimental.pallas import tpu as pltpu
from jax.experimental.pallas import tpu_sc as plsc

F32 = jnp.float32
BF16 = jnp.bfloat16

A_GROUPS = 4
CHUNK = 128
B_HEADS = 4
B_CONV = 4
C_HEAD_DIM = 64
C_KV_HEADS = 4
X_HEADS = 4
N_GROUPS = 4
EXPERTS_PER_GROUP = 8
N_EXPERTS = N_GROUPS * EXPERTS_PER_GROUP
ROPE_THETA = 10000.0
LN_EPS = 1e-5
DEPTH = 2
DN_ALPHA = (2 * DEPTH) ** 0.25

LANES = 128
VMEM_LIMIT = 48 * 1024 * 1024
NEG = -1e30


def _cparams(sem):
    return pltpu.CompilerParams(dimension_semantics=sem, vmem_limit_bytes=VMEM_LIMIT)


def _full(shape):
    nd = len(shape)
    return pl.BlockSpec(shape, lambda *_: (0,) * nd)


def _dot(a, b):
    return jnp.dot(a, b, preferred_element_type=F32)


def _dot_nt(a, b):
    return lax.dot_general(a, b, (((1,), (1,)), ((), ())), preferred_element_type=F32)


def _dot_tn(a, b):
    return lax.dot_general(a, b, (((0,), (0,)), ((), ())), preferred_element_type=F32)


def _split_dot(a, b_bf16):
    hi = a.astype(BF16)
    lo = (a - hi.astype(F32)).astype(BF16)
    return _dot(hi, b_bf16) + _dot(lo, b_bf16)


def _ln(x, g, b):
    mu = jnp.mean(x, axis=-1, keepdims=True)
    xc = x - mu
    var = jnp.mean(xc * xc, axis=-1, keepdims=True)
    return xc * lax.rsqrt(var + LN_EPS) * g + b


LOG2E = math.log2(math.e)


def _silu(x):
    return x * (1.0 / (1.0 + jnp.exp2(x * -LOG2E)))


def _gelu(x):
    return 0.5 * x * (1.0 + jnp.tanh(math.sqrt(2.0 / math.pi) * (x + 0.044715 * (x * x * x))))


def _pack_halves(x):
    c = x.shape[1] // 2
    lo = lax.bitcast_convert_type(x[:, :c].astype(BF16).astype(F32), jnp.uint32)
    hi = lax.bitcast_convert_type(x[:, c:].astype(BF16).astype(F32), jnp.uint32)
    return lax.bitcast_convert_type((lo >> 16) | hi, jnp.int32)


def _unpack_halves(p):
    u = lax.bitcast_convert_type(p, jnp.uint32)
    lo = lax.bitcast_convert_type(u << 16, F32)
    hi = lax.bitcast_convert_type(u & jnp.uint32(0xFFFF0000), F32)
    return lo, hi


def _log_sigmoid(x):
    return jnp.minimum(x, 0.0) - jnp.log(1.0 + jnp.exp(-jnp.abs(x)))


def _even_kernel(x_ref, w_in_ref, gm_g_ref, gm_b_ref, gm_w_ref, gm_bias_ref,
                 conv_w_ref, conv_b_ref, wq_ref, wk_ref, wv_ref, wif_t_ref, bif_t_ref,
                 norm_w_ref, skip_ref, w_out_ref, ln_g_ref, ln_b_ref,
                 o_ref,
                 xm_buf, ct_ref, m_ref, *, ts, aw, bw):
    dh = bw // B_HEADS
    agd = aw // A_GROUPS
    nck = ts // CHUNK
    pad = 8
    j = pl.program_id(1)

    @pl.when(j == 0)
    def _():
        xm_buf[0:pad, :] = jnp.zeros((pad, bw), F32)
        ct_ref[...] = jnp.zeros_like(ct_ref)
        m_ref[...] = jnp.zeros_like(m_ref)

    row = lax.broadcasted_iota(jnp.int32, (CHUNK, CHUNK), 0)
    col = lax.broadcasted_iota(jnp.int32, (CHUNK, CHUNK), 1)
    causal = col <= row
    diag = col == row
    triu = jnp.where(row <= col, 1.0, 0.0).astype(BF16)
    ones_blk = jnp.ones((CHUNK, LANES), BF16)

    x = x_ref[0]
    proj = _dot(x.astype(BF16), w_in_ref[...])
    a_u = _gelu(proj[:, :aw])
    a_v = _gelu(proj[:, aw:2 * aw])
    xm = proj[:, 2 * aw:2 * aw + bw]
    z = proj[:, 2 * aw + bw:]

    vn = _ln(a_v, gm_g_ref[...], gm_b_ref[...]).astype(BF16)
    ya_chunks = []
    for c in range(nck):
        cols = []
        for g in range(A_GROUPS):
            v_cg = vn[c * CHUNK:(c + 1) * CHUNK, g * agd:(g + 1) * agd]
            cols.append(_dot(gm_w_ref[g], v_cg))
        ya_chunks.append(jnp.concatenate(cols, axis=1) + gm_bias_ref[...])
    y_a = a_u * jnp.concatenate(ya_chunks, axis=0)

    xm_buf[pad:pad + ts, :] = xm
    conv = conv_b_ref[...] + conv_w_ref[B_CONV - 1:B_CONV, :] * xm
    for k in range(B_CONV - 1):
        sh = B_CONV - 1 - k
        conv = conv + conv_w_ref[k:k + 1, :] * xm_buf[pad - sh:pad - sh + ts, :]
    xm_buf[pad - (B_CONV - 1):pad, :] = xm_buf[pad + ts - (B_CONV - 1):pad + ts, :]
    xc = _silu(conv)
    xc_b = xc.astype(BF16)
    q = _dot_head_pairs(xc_b, wq_ref)
    k_ = _dot_head_pairs(xc_b, wk_ref)
    v = _dot_head_pairs(xm.astype(BF16), wv_ref)
    gate_in = jnp.concatenate([q, k_, v], axis=1).astype(BF16)
    gates_t = _dot_nt(wif_t_ref[...], gate_in) + bif_t_ref[...]
    ig_all = gates_t[:B_HEADS, :]
    lf_all = _log_sigmoid(gates_t[B_HEADS:, :])
    q_b = q.astype(BF16)
    k_b = (k_ * dh ** -0.5).astype(BF16)
    v_b = v.astype(BF16)
    gate_z = _silu(z)

    caugs = [ct_ref[h] for h in range(B_HEADS)]
    m_rows = [m_ref[h:h + 1, :] for h in range(B_HEADS)]
    h_chunks = []
    for c in range(nck):
        sl = slice(c * CHUNK, (c + 1) * CHUNK)
        lf_r = lf_all[:, sl]
        b_r = _split_dot(lf_r, triu)
        a_r = ig_all[:, sl] - b_r
        heads = []
        for h in range(B_HEADS):
            hs = slice(h * dh, (h + 1) * dh)
            qh, kh = q_b[sl, hs], k_b[sl, hs]
            vaug = jnp.concatenate([v_b[sl, hs], ones_blk], axis=1)
            a_row = a_r[h:h + 1, :]
            m_row = m_rows[h]
            amat = jnp.where(causal, a_row, NEG)
            mx = jnp.maximum(jnp.max(amat, axis=1, keepdims=True), m_row)
            w_intra = jnp.exp(amat - mx)
            w_state = jnp.exp(m_row - mx)
            s = _dot_nt(qh, kh) * w_intra
            caug = caugs[h]
            naug = (_dot(s.astype(BF16), vaug)
                    + jnp.concatenate([w_state, w_state], axis=1) * _dot(qh, caug.astype(BF16)))
            num, nq = naug[:, :dh], naug[:, dh:]
            b_t = _split_dot(jnp.where(causal, lf_r[h:h + 1, :], 0.0), ones_blk)
            hv = num / jnp.maximum(jnp.abs(nq), jnp.exp(-(b_t + mx)))
            bl = b_r[h:h + 1, CHUNK - 1:CHUNK]
            g_row = bl + a_row
            m_new = jnp.maximum(bl + m_row, jnp.max(g_row, axis=1, keepdims=True))
            decay = jnp.exp(bl + m_row - m_new)
            wg_row = jnp.exp(g_row - m_new)
            wg_t = _dot(jnp.where(diag, wg_row, 0.0).astype(BF16), ones_blk)
            kw = (kh.astype(F32) * wg_t).astype(BF16)
            caugs[h] = jnp.concatenate([decay, decay], axis=1) * caug + _dot_tn(kw, vaug)
            m_rows[h] = m_new
            hc = hv - jnp.mean(hv, axis=1, keepdims=True)
            hn = hc * lax.rsqrt(jnp.mean(hc * hc, axis=1, keepdims=True) + LN_EPS)
            heads.append(hn)
        h_chunks.append(jnp.concatenate(heads, axis=1))
    for h in range(B_HEADS):
        ct_ref[h] = caugs[h]
        m_ref[h:h + 1, :] = m_rows[h]
    hn_all = jnp.concatenate(h_chunks, axis=0) if nck > 1 else h_chunks[0]
    y_b = (hn_all * norm_w_ref[...] + skip_ref[...] * xc) * gate_z

    mix = jnp.concatenate([y_a, y_b], axis=1).astype(BF16)
    y = _dot(mix, w_out_ref[...])
    o_ref[0] = _ln(DN_ALPHA * x + y, ln_g_ref[...], ln_b_ref[...])


def _split_dot_left(a_bf16, b):
    hi = b.astype(BF16)
    lo = (b - hi.astype(F32)).astype(BF16)
    return _dot(a_bf16, hi) + _dot(a_bf16, lo)


def _block_diag_pairs(w):
    hh, d, _ = w.shape
    wp = w.reshape(hh // 2, 2, d, d)
    eye = jnp.eye(2, dtype=w.dtype)
    return jnp.einsum('pade,ab->padbe', wp, eye).reshape(hh // 2, 2 * d, 2 * d)


def _dot_head_pairs(x, w_ref):
    npair, w2, _ = w_ref.shape
    return jnp.concatenate([_dot(x[:, p * w2:(p + 1) * w2], w_ref[p]) for p in range(npair)], axis=1)


def even_mixer_layer(x, w_in, gm_ln_g, gm_ln_b, gm_ws, gm_bs, conv_w, conv_b, wq, wk, wv, w_if,
                     b_if, norm_w, skip, w_out, ln_g, ln_b, *, ts):
    bsz, seq, d = x.shape
    aw = gm_ln_g.shape[0]
    bw = conv_b.shape[0]
    agd = aw // A_GROUPS
    causal = jnp.tril(jnp.ones((CHUNK, CHUNK), dtype=bool))
    gm_w = jnp.where(causal[None], gm_ws, 0.0).astype(BF16)
    gm_bias = jnp.repeat(gm_bs.T, agd, axis=1)
    row = lambda a: a.reshape(1, -1)
    args = (x, w_in.astype(BF16), row(gm_ln_g), row(gm_ln_b), gm_w, gm_bias,
            conv_w, row(conv_b), _block_diag_pairs(wq).astype(BF16), _block_diag_pairs(wk).astype(BF16),
            _block_diag_pairs(wv).astype(BF16), w_if.T.astype(BF16),
            b_if.reshape(-1, 1), row(norm_w), row(skip), w_out.astype(BF16),
            row(ln_g), row(ln_b))
    in_specs = [pl.BlockSpec((1, ts, d), lambda b, j: (b, j, 0))] + [_full(a.shape) for a in args[1:]]
    dh = bw // B_HEADS
    assert dh == LANES and CHUNK == LANES
    return pl.pallas_call(
        functools.partial(_even_kernel, ts=ts, aw=aw, bw=bw),
        grid=(bsz, seq // ts),
        in_specs=in_specs,
        out_specs=pl.BlockSpec((1, ts, d), lambda b, j: (b, j, 0)),
        out_shape=jax.ShapeDtypeStruct((bsz, seq, d), F32),
        scratch_shapes=[pltpu.VMEM((8 + ts, bw), F32),
                        pltpu.VMEM((B_HEADS, dh, dh + LANES), F32),
                        pltpu.VMEM((B_HEADS, LANES), F32)],
        compiler_params=_cparams(("arbitrary", "arbitrary")),
        name="even_mixer",
    )(*args)


def _memfold_kernel(mem_ref, wkv_ref, wq_ref, wo_ref, wqk_ref, vo_ref, *, d):
    dh = d // X_HEADS
    m_len = mem_ref.shape[1]
    kv = _dot(mem_ref[0].astype(BF16), wkv_ref[...])
    k = (kv[:, :d] * (dh ** -0.5 * LOG2E)).astype(BF16)
    v = kv[:, d:].astype(BF16)
    for h in range(X_HEADS):
        hs = slice(h * dh, (h + 1) * dh)
        ms = slice(h * m_len, (h + 1) * m_len)
        wqk_ref[0, :, ms] = _dot_nt(wq_ref[:, hs], k[:, hs]).astype(BF16)
        vo_ref[0, ms, :] = _dot(v[:, hs], wo_ref[hs, :]).astype(BF16)


def memory_fold(mem, wkv, wq, wo):
    bsz, m_len, d = mem.shape
    hm = X_HEADS * m_len
    args = (mem, wkv.astype(BF16), wq.astype(BF16), wo.astype(BF16))
    return pl.pallas_call(
        functools.partial(_memfold_kernel, d=d),
        grid=(bsz,),
        in_specs=[pl.BlockSpec((1, m_len, d), lambda b: (b, 0, 0))] + [_full(a.shape) for a in args[1:]],
        out_specs=[pl.BlockSpec((1, d, hm), lambda b: (b, 0, 0)),
                   pl.BlockSpec((1, hm, d), lambda b: (b, 0, 0))],
        out_shape=[jax.ShapeDtypeStruct((bsz, d, hm), BF16),
                   jax.ShapeDtypeStruct((bsz, hm, d), BF16)],
        compiler_params=_cparams(("arbitrary",)),
        name="memory_fold",
    )(*args)


ROUTE_W = 128
XATTN_TS = 1024
DEST_ROWS = 8


def _xattn_kernel(x_ref, wqk_ref, vo_ref, ln_g_ref, ln_b_ref, wr_ref, br_ref,
                  o_ref, xp_ref, route_ref, cnt_ref):
    m_len = wqk_ref.shape[2] // X_HEADS

    @pl.when(pl.program_id(1) == 0)
    def _():
        cnt_ref[...] = jnp.zeros_like(cnt_ref)

    x = x_ref[0]
    ts = x.shape[0]
    xb = x.astype(BF16)
    probs = []
    for h in range(X_HEADS):
        s = _dot(xb, wqk_ref[0, :, h * m_len:(h + 1) * m_len])
        p = jnp.exp2(s - jnp.max(s, axis=1, keepdims=True))
        probs.append((p / jnp.sum(p, axis=1, keepdims=True)).astype(BF16))
    y = _dot(jnp.concatenate(probs, axis=1), vo_ref[0])
    x2 = _ln(DN_ALPHA * x + y, ln_g_ref[...], ln_b_ref[...])
    o_ref[0] = x2
    xp_ref[0] = _pack_halves(x2)

    x_hi = x2.astype(BF16)
    x_lo = (x2 - x_hi.astype(F32)).astype(BF16)
    logits = (_dot(x_hi, wr_ref[0]) + _dot(x_lo, wr_ref[0]) + _dot(x_hi, wr_ref[1])) + br_ref[...]
    lane = lax.broadcasted_iota(jnp.int32, (ts, ROUTE_W), 1)
    is_g = lane < N_GROUPS
    lg = jnp.where(is_g, logits, NEG)
    mg = jnp.max(lg, axis=1, keepdims=True)
    gi = jnp.min(jnp.where(jnp.logical_and(is_g, lg == mg), lane, ROUTE_W), axis=1, keepdims=True)
    gate_g = 1.0 / jnp.sum(jnp.where(is_g, jnp.exp(lg - mg), 0.0), axis=1, keepdims=True)
    lo = N_GROUPS + gi * EXPERTS_PER_GROUP
    in_grp = jnp.logical_and(lane >= lo, lane < lo + EXPERTS_PER_GROUP)
    le = jnp.where(in_grp, logits, NEG)
    v1 = jnp.max(le, axis=1, keepdims=True)
    i1 = jnp.min(jnp.where(jnp.logical_and(in_grp, le == v1), lane, ROUTE_W), axis=1, keepdims=True)
    le2 = jnp.where(lane == i1, NEG, le)
    v2 = jnp.max(le2, axis=1, keepdims=True)
    i2 = jnp.min(jnp.where(jnp.logical_and(in_grp, le2 == v2), lane, ROUTE_W), axis=1, keepdims=True)
    e21 = jnp.exp(v2 - v1)
    p1 = 1.0 / (1.0 + e21)
    p2 = e21 * p1
    e1 = (i1 - N_GROUPS).astype(F32)
    e2 = (i2 - N_GROUPS).astype(F32)
    rec = jnp.where(lane == 0, e1, 0.0)
    rec = jnp.where(lane == 1, e2, rec)
    rec = jnp.where(lane == 2, gate_g * p1, rec)
    rec = jnp.where(lane == 3, gate_g * p2, rec)
    route_ref[0] = rec
    sel = jnp.logical_or(lane == i1 - N_GROUPS, lane == i2 - N_GROUPS)
    cnt_ref[0] += jnp.sum(jnp.where(sel, 1.0, 0.0), axis=0, keepdims=True)


def xattn_router_layer(x, wqk, vo, ln_g, ln_b, w_rg, b_rg, w_re, b_re, *, ts):
    bsz, seq, d = x.shape
    hm = wqk.shape[2]
    wr = jnp.zeros((d, ROUTE_W), F32).at[:, :N_GROUPS].set(w_rg).at[:, N_GROUPS:N_GROUPS + N_EXPERTS].set(w_re)
    br = jnp.zeros((1, ROUTE_W), F32).at[0, :N_GROUPS].set(b_rg).at[0, N_GROUPS:N_GROUPS + N_EXPERTS].set(b_re)
    wr_hi = wr.astype(BF16)
    wr_lo = (wr - wr_hi.astype(F32)).astype(BF16)
    wr = jnp.stack([wr_hi, wr_lo])
    args = (x, wqk, vo, ln_g.reshape(1, -1), ln_b.reshape(1, -1), wr, br)
    in_specs = [pl.BlockSpec((1, ts, d), lambda b, j: (b, j, 0)),
                pl.BlockSpec((1, d, hm), lambda b, j: (b, 0, 0)),
                pl.BlockSpec((1, hm, d), lambda b, j: (b, 0, 0))] + [_full(a.shape) for a in args[3:]]
    return pl.pallas_call(
        _xattn_kernel,
        grid=(bsz, seq // ts),
        in_specs=in_specs,
        out_specs=[pl.BlockSpec((1, ts, d), lambda b, j: (b, j, 0)),
                   pl.BlockSpec((1, ts, d // 2), lambda b, j: (b, j, 0)),
                   pl.BlockSpec((1, ts, ROUTE_W), lambda b, j: (b, j, 0)),
                   pl.BlockSpec((1, 1, ROUTE_W), lambda b, j: (b, 0, 0))],
        out_shape=[jax.ShapeDtypeStruct((bsz, seq, d), F32),
                   jax.ShapeDtypeStruct((bsz, seq, d // 2), jnp.int32),
                   jax.ShapeDtypeStruct((bsz, seq, ROUTE_W), F32),
                   jax.ShapeDtypeStruct((bsz, 1, ROUTE_W), F32)],
        compiler_params=_cparams(("arbitrary", "arbitrary")),
        name="xattn_router",
    )(*args)


def _moe_output(x2, g0, g1, rec, ln_g, ln_b):
    a_lo, a_hi = _unpack_halves(g0)
    b_lo, b_hi = _unpack_halves(g1)
    w0, w1 = rec[:, 2:3], rec[:, 3:4]
    y = jnp.concatenate([w0 * a_lo + w1 * b_lo, w0 * a_hi + w1 * b_hi], axis=1)
    return _ln(DN_ALPHA * x2 + y, ln_g, ln_b)


def _swa_kernel(x_ref, g0_ref, g1_ref, route_ref, pln_g_ref, pln_b_ref,
                wqkv_ref, bqkv_ref, cos_ref, sin_ref, sink_ref, wo_ref, ln_g_ref, ln_b_ref,
                o_ref, kprev, vprev, *, ts, cq, ckv):
    j = pl.program_id(1)
    nb = ts // CHUNK
    dh = C_HEAD_DIM
    grp = (cq // dh) // C_KV_HEADS

    @pl.when(j == 0)
    def _():
        kprev[...] = jnp.zeros_like(kprev)
        vprev[...] = jnp.zeros_like(vprev)

    x = _moe_output(x_ref[0], g0_ref[0, 0], g1_ref[0, 0], route_ref[0], pln_g_ref[...], pln_b_ref[...])
    qkv = _dot(x.astype(BF16), wqkv_ref[...]) + bqkv_ref[...]
    cos = cos_ref[...]
    sin = sin_ref[...]
    lane = lax.broadcasted_iota(jnp.int32, (ts, LANES), 1)
    first_half = (lane % dh) < (dh // 2)

    def rope(t):
        outs = []
        for c in range(t.shape[1] // LANES):
            tc = t[:, c * LANES:(c + 1) * LANES]
            rot = jnp.where(first_half, pltpu.roll(tc, LANES - dh // 2, 1), pltpu.roll(tc, dh // 2, 1))
            outs.append(tc * cos + rot * sin)
        return jnp.concatenate(outs, axis=1)

    q = rope(qkv[:, :cq]).astype(BF16)
    k = rope(qkv[:, cq:cq + 2 * ckv]).astype(BF16)
    v = qkv[:, cq + 2 * ckv:].astype(BF16)

    r_i = lax.broadcasted_iota(jnp.int32, (CHUNK, 2 * CHUNK), 0)
    c_i = lax.broadcasted_iota(jnp.int32, (CHUNK, 2 * CHUNK), 1)
    band = jnp.logical_and(c_i > r_i, c_i <= r_i + CHUNK)
    sink_col = c_i == 0
    lane_k = lax.broadcasted_iota(jnp.int32, (2 * CHUNK, LANES), 1)
    key_row = lax.broadcasted_iota(jnp.int32, (2 * CHUNK, LANES), 0)
    lane_q = lax.broadcasted_iota(jnp.int32, (CHUNK, LANES), 1)
    ones_blk = jnp.ones((2 * CHUNK, LANES), BF16)
    zero_b = jnp.zeros((), BF16)
    blocks = []
    for c in range(nb):
        sl = slice(c * CHUNK, (c + 1) * CHUNK)
        if c == 0:
            kb = jnp.concatenate([kprev[...].astype(BF16), k[sl]], axis=0)
            vb = jnp.concatenate([vprev[...].astype(BF16), v[sl]], axis=0)
            first_key = jnp.where(j > 0, 0, CHUNK)
            valid = jnp.logical_and(band, c_i >= first_key)
        else:
            kb = k[(c - 1) * CHUNK:(c + 1) * CHUNK]
            vb = v[(c - 1) * CHUNK:(c + 1) * CHUNK]
            valid = band
        tiles = []
        for h in range(C_KV_HEADS):
            kd = kb[:, h * LANES:(h + 1) * LANES]
            vd = vb[:, h * LANES:(h + 1) * LANES]
            k_lo = jnp.where(lane_k < dh, kd, zero_b)
            k_hi = jnp.where(lane_k >= dh, kd, zero_b)
            vz = jnp.where(key_row == 0, zero_b, vd)
            q2 = jnp.concatenate([q[sl, (2 * h) * LANES:(2 * h + 1) * LANES],
                                  q[sl, (2 * h + 1) * LANES:(2 * h + 2) * LANES]], axis=0)
            s_lo = _dot_nt(q2, k_lo)
            s_hi = _dot_nt(q2, k_hi)
            parts = []
            for qk, g in ((s_lo[:CHUNK], 0), (s_lo[CHUNK:], 2), (s_hi[:CHUNK], 1), (s_hi[CHUNK:], 3)):
                sink = sink_ref[h * grp + g]
                parts.append(jnp.where(valid, qk, jnp.where(sink_col, sink, NEG)))
            s = jnp.concatenate(parts, axis=0)
            p = jnp.exp2(s - jnp.max(s, axis=1, keepdims=True)).astype(BF16)
            den = _dot(p, ones_blk)
            o2 = _dot(p, vz) / den
            tiles.append(jnp.where(lane_q < dh, o2[:CHUNK], o2[2 * CHUNK:3 * CHUNK]))
            tiles.append(jnp.where(lane_q < dh, o2[CHUNK:2 * CHUNK], o2[3 * CHUNK:]))
        blocks.append(jnp.concatenate(tiles, axis=1))
    kprev[...] = k[(nb - 1) * CHUNK:].astype(F32)
    vprev[...] = v[(nb - 1) * CHUNK:].astype(F32)
    att = (jnp.concatenate(blocks, axis=0) if nb > 1 else blocks[0]).astype(BF16)
    y = _dot(att, wo_ref[...])
    o_ref[0] = _ln(DN_ALPHA * x + y, ln_g_ref[...], ln_b_ref[...])


def swa_mixer_layer(pending, w_qkv, b_qkv, sinks, w_o, ln_g, ln_b, *, ts):
    x, g, route, pln_g, pln_b = pending
    bsz, seq, d = x.shape
    g = g.reshape(2, bsz, seq, d // 2)
    cq = w_o.shape[0]
    ckv = (w_qkv.shape[1] - cq) // 2
    dh = C_HEAD_DIM
    inv = ROPE_THETA ** (-jnp.arange(0, dh, 2, dtype=F32) / dh)
    ang = jnp.arange(seq, dtype=F32)[:, None] * inv[None, :]
    reps = LANES // (dh // 2)
    sign = jnp.tile(jnp.concatenate([-jnp.ones((dh // 2,), F32), jnp.ones((dh // 2,), F32)]), LANES // dh)
    cos_t = jnp.tile(jnp.cos(ang), (1, reps))
    sin_t = jnp.tile(jnp.sin(ang), (1, reps)) * sign[None, :]
    assert 2 * dh == LANES

    def dup_heads(t):
        th = t.reshape(t.shape[:-1] + (ckv // dh, dh))
        return jnp.concatenate([th, th], axis=-1).reshape(t.shape[:-1] + (2 * ckv,))

    qs = dh ** -0.5 * LOG2E
    w_all = jnp.concatenate([w_qkv[:, :cq] * qs, dup_heads(w_qkv[:, cq:cq + ckv]),
                             dup_heads(w_qkv[:, cq + ckv:])], axis=1)
    b_all = jnp.concatenate([b_qkv[:cq] * qs, dup_heads(b_qkv[cq:cq + ckv]), dup_heads(b_qkv[cq + ckv:])])
    args = (x, g, g, route, pln_g.reshape(1, -1), pln_b.reshape(1, -1),
            w_all.astype(BF16), b_all.reshape(1, -1), cos_t, sin_t, sinks.astype(F32) * LOG2E,
            w_o.astype(BF16), ln_g.reshape(1, -1), ln_b.reshape(1, -1))
    in_specs = [pl.BlockSpec((1, ts, d), lambda b, j: (b, j, 0)),
                pl.BlockSpec((1, 1, ts, d // 2), lambda b, j: (0, b, j, 0)),
                pl.BlockSpec((1, 1, ts, d // 2), lambda b, j: (1, b, j, 0)),
                pl.BlockSpec((1, ts, ROUTE_W), lambda b, j: (b, j, 0)),
                _full((1, d)), _full((1, d)),
                _full(args[6].shape), _full(args[7].shape),
                pl.BlockSpec((ts, LANES), lambda b, j: (j, 0)),
                pl.BlockSpec((ts, LANES), lambda b, j: (j, 0)),
                pl.BlockSpec(memory_space=pltpu.SMEM),
                _full(args[11].shape), _full(args[12].shape), _full(args[13].shape)]
    return pl.pallas_call(
        functools.partial(_swa_kernel, ts=ts, cq=cq, ckv=ckv),
        grid=(bsz, seq // ts),
        in_specs=in_specs,
        out_specs=pl.BlockSpec((1, ts, d), lambda b, j: (b, j, 0)),
        out_shape=jax.ShapeDtypeStruct((bsz, seq, d), F32),
        scratch_shapes=[pltpu.VMEM((CHUNK, 2 * ckv), F32), pltpu.VMEM((CHUNK, 2 * ckv), F32)],
        compiler_params=_cparams(("arbitrary", "arbitrary")),
        name="swa_mixer",
    )(*args)


def _slot_kernel(route_ref, pstart_ref, dest_ref, carry_ref, *, tb):
    @pl.when(pl.program_id(0) == 0)
    def _():
        carry_ref[...] = pstart_ref[...]

    rec = route_ref[...]
    lane = lax.broadcasted_iota(jnp.int32, (tb, ROUTE_W), 1)
    e0 = rec[:, 0:1].astype(jnp.int32)
    e1 = rec[:, 1:2].astype(jnp.int32)
    oh0 = lane == e0
    oh1 = lane == e1
    ohs = jnp.where(jnp.logical_or(oh0, oh1), 1.0, 0.0)
    r = lax.broadcasted_iota(jnp.int32, (tb, tb), 0)
    c = lax.broadcasted_iota(jnp.int32, (tb, tb), 1)
    before = jnp.where(c < r, 1.0, 0.0).astype(BF16)
    prefix = _dot(before, ohs.astype(BF16)) + carry_ref[...]
    d0 = jnp.sum(jnp.where(oh0, prefix, 0.0), axis=1, keepdims=True)
    d1 = jnp.sum(jnp.where(oh1, prefix, 0.0), axis=1, keepdims=True)
    dest = jnp.where(lane == 0, d0, jnp.where(lane == 1, d1, 0.0))
    dest_ref[...] = dest.T[:DEST_ROWS].astype(jnp.int32)
    carry_ref[...] += jnp.sum(ohs, axis=0, keepdims=True)


def moe_slots(route, pstart, *, tb):
    n = route.shape[0]
    return pl.pallas_call(
        functools.partial(_slot_kernel, tb=tb),
        grid=(n // tb,),
        in_specs=[pl.BlockSpec((tb, ROUTE_W), lambda i: (i, 0)), _full((1, ROUTE_W))],
        out_specs=pl.BlockSpec((DEST_ROWS, tb), lambda i: (0, i)),
        out_shape=jax.ShapeDtypeStruct((DEST_ROWS, n), jnp.int32),
        scratch_shapes=[pltpu.VMEM((1, ROUTE_W), F32)],
        compiler_params=_cparams(("arbitrary",)),
        name="moe_slots",
    )(route, pstart)


def _ffn_kernel(blk_exp_ref, new_exp_ref, nblk_ref, xs_ref, w1_ref, w3_ref, w2_ref, ys_ref,
                w1_b, w3_b, w2_b):
    i = pl.program_id(0)
    used = i < nblk_ref[0]

    @pl.when(jnp.logical_and(used, new_exp_ref[i] == 1))
    def _():
        w1_b[...] = w1_ref[0, 0].astype(BF16)
        w3_b[...] = w3_ref[0, 0].astype(BF16)
        w2_b[...] = w2_ref[0, 0].astype(BF16)

    @pl.when(used)
    def _():
        x_lo, x_hi = _unpack_halves(xs_ref[...])
        x_lo, x_hi = x_lo.astype(BF16), x_hi.astype(BF16)
        dl = x_lo.shape[1]
        h1 = _dot(x_lo, w1_b[:dl, :]) + _dot(x_hi, w1_b[dl:, :])
        h3 = _dot(x_lo, w3_b[:dl, :]) + _dot(x_hi, w3_b[dl:, :])
        h = (_silu(h1) * h3).astype(BF16)
        ys_ref[...] = _pack_halves(_dot(h, w2_b[...]))

    @pl.when(jnp.logical_not(used))
    def _():
        ys_ref[...] = jnp.zeros_like(ys_ref)


def moe_ffn(xs, blk_exp, new_exp, nblk, w1, w3, w2, *, layer, bm):
    n_pad, dp = xs.shape
    d = 2 * dp
    de = w1.shape[3]
    n_blk = n_pad // bm

    def x_map(i, be, ne, nb):
        return (jnp.minimum(i, nb[0] - 1), 0)

    def w_map(i, be, ne, nb):
        return (layer, be[i], 0, 0)

    return pl.pallas_call(
        _ffn_kernel,
        grid_spec=pltpu.PrefetchScalarGridSpec(
            num_scalar_prefetch=3,
            grid=(n_blk,),
            in_specs=[pl.BlockSpec((bm, dp), x_map),
                      pl.BlockSpec((1, 1, d, de), w_map),
                      pl.BlockSpec((1, 1, d, de), w_map),
                      pl.BlockSpec((1, 1, de, d), w_map)],
            out_specs=pl.BlockSpec((bm, dp), lambda i, be, ne, nb: (i, 0)),
            scratch_shapes=[pltpu.VMEM((d, de), BF16), pltpu.VMEM((d, de), BF16),
                            pltpu.VMEM((de, d), BF16)]),
        out_shape=jax.ShapeDtypeStruct((n_pad, dp), jnp.int32),
        compiler_params=_cparams(("arbitrary",)),
        name="moe_ffn",
    )(blk_exp, new_exp, nblk, xs, w1, w3, w2)


def _combine_kernel(x_ref, g0_ref, g1_ref, route_ref, ln_g_ref, ln_b_ref, o_ref):
    o_ref[...] = _moe_output(x_ref[...], g0_ref[0], g1_ref[0], route_ref[...], ln_g_ref[...], ln_b_ref[...])


def moe_combine(x, g, route, ln_g, ln_b, *, tb):
    n, d = x.shape
    row_spec = pl.BlockSpec((tb, d), lambda i: (i, 0))
    return pl.pallas_call(
        _combine_kernel,
        grid=(n // tb,),
        in_specs=[row_spec,
                  pl.BlockSpec((1, tb, d // 2), lambda i: (0, i, 0)),
                  pl.BlockSpec((1, tb, d // 2), lambda i: (1, i, 0)),
                  pl.BlockSpec((tb, ROUTE_W), lambda i: (i, 0)),
                  _full((1, d)), _full((1, d))],
        out_specs=row_spec,
        out_shape=jax.ShapeDtypeStruct((n, d), F32),
        compiler_params=_cparams(("arbitrary",)),
        name="moe_combine",
    )(x, g, g, route, ln_g.reshape(1, -1), ln_b.reshape(1, -1))


MOE_BM = 1024
MIXER_TS = 1024


def hierarchical_moe_layer(x2, xp, route, counts, w1, w3, w2, ln_g, ln_b, *, layer, defer_combine):
    bsz, seq, d = x2.shape
    n = bsz * seq
    bm = MOE_BM
    rt = route.reshape(n, ROUTE_W)
    tb = min(512, n)
    n_blk = (2 * n) // bm + N_EXPERTS
    cnt = counts.sum(axis=0)[0, :N_EXPERTS].astype(jnp.int32)
    pcnt = (cnt + bm - 1) // bm * bm
    pends = jnp.cumsum(pcnt)
    pstart = pends - pcnt
    nblk = (pends[-1] // bm).astype(jnp.int32).reshape(1)
    blk_row = jnp.arange(n_blk, dtype=jnp.int32) * bm
    blk_exp = jnp.minimum(jnp.sum((pends[None, :] <= blk_row[:, None]).astype(jnp.int32), axis=1),
                          N_EXPERTS - 1)
    last_exp = blk_exp[jnp.maximum(nblk[0] - 1, 0)]
    blk_exp = jnp.where(jnp.arange(n_blk) < nblk[0], blk_exp, last_exp)
    new_exp = jnp.concatenate([jnp.ones((1,), jnp.int32),
                               (blk_exp[1:] != blk_exp[:-1]).astype(jnp.int32)])
    pstart_rec = jnp.zeros((1, ROUTE_W), F32).at[0, :N_EXPERTS].set(pstart.astype(F32))
    dest = moe_slots(rt, pstart_rec, tb=tb)
    xs = moe_dispatch(xp.reshape(n, d // 2), dest[0], dest[1], n_blk * bm)
    ys = moe_ffn(xs, blk_exp, new_exp, nblk, w1, w3, w2, layer=layer, bm=bm)
    g = moe_gather(ys, dest[0], dest[1])
    if defer_combine:
        return x2, g, route, ln_g, ln_b
    return moe_combine(x2.reshape(n, d), g, rt, ln_g, ln_b, tb=tb).reshape(bsz, seq, d)


SC_ROWS = 128


def _sc_mesh():
    return plsc.VectorSubcoreMesh(core_axis_name="c", subcore_axis_name="s")


def moe_dispatch(xf, dest0, dest1, n_pad):
    n, d = xf.shape
    info = plsc.get_sparse_core_info()
    nw = info.num_cores * info.num_subcores
    per_w = n // nw
    r = min(SC_ROWS, per_w)

    def body(x_hbm, d0_hbm, d1_hbm, xs_hbm, i0_v, i1_v, rows_v, sem):
        wid = lax.axis_index("s") * info.num_cores + lax.axis_index("c")

        @pl.loop(0, per_w // r)
        def _(c):
            base = pl.multiple_of(wid * per_w + c * r, 8)
            pltpu.sync_copy(d0_hbm.at[pl.ds(base, r)], i0_v)
            pltpu.sync_copy(d1_hbm.at[pl.ds(base, r)], i1_v)
            pltpu.sync_copy(x_hbm.at[pl.ds(base, r)], rows_v)
            pltpu.async_copy(rows_v, xs_hbm.at[i0_v], sem).wait()
            pltpu.async_copy(rows_v, xs_hbm.at[i1_v], sem).wait()

    return pl.kernel(
        body, out_type=jax.ShapeDtypeStruct((n_pad, d), xf.dtype), mesh=_sc_mesh(),
        scratch_types=[pltpu.VMEM((r,), jnp.int32), pltpu.VMEM((r,), jnp.int32),
                       pltpu.VMEM((r, d), xf.dtype), pltpu.SemaphoreType.DMA],
        name="moe_dispatch",
    )(xf, dest0, dest1)


def moe_gather(ys, dest0, dest1):
    n = dest0.shape[0]
    d = ys.shape[1]
    info = plsc.get_sparse_core_info()
    nw = info.num_cores * info.num_subcores
    per_w = n // nw
    r = min(SC_ROWS, per_w)

    def body(ys_hbm, d0_hbm, d1_hbm, g_hbm, i_v, rows_v, sem):
        wid = lax.axis_index("s") * info.num_cores + lax.axis_index("c")

        @pl.loop(0, per_w // r)
        def _(c):
            base = pl.multiple_of(wid * per_w + c * r, 8)
            for k, d_hbm in enumerate((d0_hbm, d1_hbm)):
                pltpu.sync_copy(d_hbm.at[pl.ds(base, r)], i_v)
                pltpu.async_copy(ys_hbm.at[i_v], rows_v, sem).wait()
                pltpu.sync_copy(rows_v, g_hbm.at[k, pl.ds(base, r)])

    return pl.kernel(
        body, out_type=jax.ShapeDtypeStruct((2, n, d), ys.dtype), mesh=_sc_mesh(),
        scratch_types=[pltpu.VMEM((r,), jnp.int32), pltpu.VMEM((r, d), ys.dtype),
                       pltpu.SemaphoreType.DMA],
        name="moe_gather",
    )(ys, dest0, dest1)


def kernel(x, mem, ln_g, ln_b, ev_w_in, ev_gm_ln_g, ev_gm_ln_b, ev_gm_ws, ev_gm_bs, ev_conv_w, ev_conv_b, ev_wq, ev_wk, ev_wv, ev_w_if, ev_b_if, ev_norm_w, ev_skip, ev_w_out, od_w_qkv, od_b_qkv, od_sinks, od_w_o, xa_wq, xa_wkv, xa_wo, moe_w_rg, moe_b_rg, moe_w_re, moe_b_re, moe_w1, moe_w3, moe_w2):
    bsz, seq, d = x.shape
    depth = ln_g.shape[0]
    ts = min(MIXER_TS, seq)
    for l in range(depth):
        if l % 2 == 0:
            e = l // 2
            x = even_mixer_layer(x, ev_w_in[e], ev_gm_ln_g[e], ev_gm_ln_b[e], ev_gm_ws[e], ev_gm_bs[e],
                                 ev_conv_w[e], ev_conv_b[e], ev_wq[e], ev_wk[e], ev_wv[e], ev_w_if[e],
                                 ev_b_if[e], ev_norm_w[e], ev_skip[e], ev_w_out[e],
                                 ln_g[l, 0], ln_b[l, 0], ts=ts)
        else:
            o = l // 2
            x = swa_mixer_layer(x, od_w_qkv[o], od_b_qkv[o], od_sinks[o], od_w_o[o],
                                ln_g[l, 0], ln_b[l, 0], ts=ts)
        wqk, vo = memory_fold(mem, xa_wkv[l], xa_wq[l], xa_wo[l])
        x, xp, route, counts = xattn_router_layer(x, wqk, vo, ln_g[l, 1], ln_b[l, 1],
                                                  moe_w_rg[l], moe_b_rg[l], moe_w_re[l], moe_b_re[l],
                                                  ts=min(XATTN_TS, seq))
        x = hierarchical_moe_layer(x, xp, route, counts, moe_w1, moe_w3, moe_w2,
                                   ln_g[l, 2], ln_b[l, 2], layer=l,
                                   defer_combine=(l + 1 < depth and (l + 1) % 2 == 1))
    return x
```

```python
import functools
import math

import jax
import jax.numpy as jnp
from jax import lax
from jax.experimental import pallas as pl
from jax.experimental.pallas import tpu as pltpu
from jax.experimental.pallas import tpu_sc as plsc

F32 = jnp.float32
BF16 = jnp.bfloat16

A_GROUPS = 4
CHUNK = 128
B_HEADS = 4
B_CONV = 4
C_HEAD_DIM = 64
C_KV_HEADS = 4
X_HEADS = 4
N_GROUPS = 4
EXPERTS_PER_GROUP = 8
N_EXPERTS = N_GROUPS * EXPERTS_PER_GROUP
ROPE_THETA = 10000.0
LN_EPS = 1e-5
DEPTH = 2
DN_ALPHA = (2 * DEPTH) ** 0.25

LANES = 128
VMEM_LIMIT = 48 * 1024 * 1024
NEG = -1e30


def _cparams(sem):
    return pltpu.CompilerParams(dimension_semantics=sem, vmem_limit_bytes=VMEM_LIMIT)


def _full(shape):
    nd = len(shape)
    return pl.BlockSpec(shape, lambda *_: (0,) * nd)


def _dot(a, b):
    return jnp.dot(a, b, preferred_element_type=F32)


def _dot_nt(a, b):
    return lax.dot_general(a, b, (((1,), (1,)), ((), ())), preferred_element_type=F32)


def _dot_tn(a, b):
    return lax.dot_general(a, b, (((0,), (0,)), ((), ())), preferred_element_type=F32)


def _split_dot(a, b_bf16):
    hi = a.astype(BF16)
    lo = (a - hi.astype(F32)).astype(BF16)
    return _dot(hi, b_bf16) + _dot(lo, b_bf16)


def _ln(x, g, b):
    mu = jnp.mean(x, axis=-1, keepdims=True)
    xc = x - mu
    var = jnp.mean(xc * xc, axis=-1, keepdims=True)
    return xc * lax.rsqrt(var + LN_EPS) * g + b


LOG2E = math.log2(math.e)


def _silu(x):
    return x * (1.0 / (1.0 + jnp.exp2(x * -LOG2E)))


def _gelu(x):
    return 0.5 * x * (1.0 + jnp.tanh(math.sqrt(2.0 / math.pi) * (x + 0.044715 * (x * x * x))))


def _pack_halves(x):
    c = x.shape[1] // 2
    lo = lax.bitcast_convert_type(x[:, :c].astype(BF16).astype(F32), jnp.uint32)
    hi = lax.bitcast_convert_type(x[:, c:].astype(BF16).astype(F32), jnp.uint32)
    return lax.bitcast_convert_type((lo >> 16) | hi, jnp.int32)


def _unpack_halves(p):
    u = lax.bitcast_convert_type(p, jnp.uint32)
    lo = lax.bitcast_convert_type(u << 16, F32)
    hi = lax.bitcast_convert_type(u & jnp.uint32(0xFFFF0000), F32)
    return lo, hi


def _log_sigmoid(x):
    return jnp.minimum(x, 0.0) - jnp.log(1.0 + jnp.exp(-jnp.abs(x)))


def _even_kernel(x_ref, w_in_ref, gm_g_ref, gm_b_ref, gm_w_ref, gm_bias_ref,
                 conv_w_ref, conv_b_ref, wq_ref, wk_ref, wv_ref, wif_t_ref, bif_t_ref,
                 norm_w_ref, skip_ref, w_out_ref, ln_g_ref, ln_b_ref,
                 o_ref,
                 xm_buf, ct_ref, m_ref, *, ts, aw, bw):
    dh = bw // B_HEADS
    agd = aw // A_GROUPS
    nck = ts // CHUNK
    pad = 8
    j = pl.program_id(1)

    @pl.when(j == 0)
    def _():
        xm_buf[0:pad, :] = jnp.zeros((pad, bw), F32)
        ct_ref[...] = jnp.zeros_like(ct_ref)
        m_ref[...] = jnp.zeros_like(m_ref)

    row = lax.broadcasted_iota(jnp.int32, (CHUNK, CHUNK), 0)
    col = lax.broadcasted_iota(jnp.int32, (CHUNK, CHUNK), 1)
    causal = col <= row
    diag = col == row
    triu = jnp.where(row <= col, 1.0, 0.0).astype(BF16)
    ones_blk = jnp.ones((CHUNK, LANES), BF16)

    x = x_ref[0]
    proj = _dot(x.astype(BF16), w_in_ref[...])
    a_u = _gelu(proj[:, :aw])
    a_v = _gelu(proj[:, aw:2 * aw])
    xm = proj[:, 2 * aw:2 * aw + bw]
    z = proj[:, 2 * aw + bw:]

    vn = _ln(a_v, gm_g_ref[...], gm_b_ref[...]).astype(BF16)
    ya_chunks = []
    for c in range(nck):
        cols = []
        for g in range(A_GROUPS):
            v_cg = vn[c * CHUNK:(c + 1) * CHUNK, g * agd:(g + 1) * agd]
            cols.append(_dot(gm_w_ref[g], v_cg))
        ya_chunks.append(jnp.concatenate(cols, axis=1) + gm_bias_ref[...])
    y_a = a_u * jnp.concatenate(ya_chunks, axis=0)

    xm_buf[pad:pad + ts, :] = xm
    conv = conv_b_ref[...] + conv_w_ref[B_CONV - 1:B_CONV, :] * xm
    for k in range(B_CONV - 1):
        sh = B_CONV - 1 - k
        conv = conv + conv_w_ref[k:k + 1, :] * xm_buf[pad - sh:pad - sh + ts, :]
    xm_buf[pad - (B_CONV - 1):pad, :] = xm_buf[pad + ts - (B_CONV - 1):pad + ts, :]
    xc = _silu(conv)
    xc_b = xc.astype(BF16)
    q = _dot_head_pairs(xc_b, wq_ref)
    k_ = _dot_head_pairs(xc_b, wk_ref)
    v = _dot_head_pairs(xm.astype(BF16), wv_ref)
    gate_in = jnp.concatenate([q, k_, v], axis=1).astype(BF16)
    gates_t = _dot_nt(wif_t_ref[...], gate_in) + bif_t_ref[...]
    ig_all = gates_t[:B_HEADS, :]
    lf_all = _log_sigmoid(gates_t[B_HEADS:, :])
    q_b = q.astype(BF16)
    k_b = (k_ * dh ** -0.5).astype(BF16)
    v_b = v.astype(BF16)
    gate_z = _silu(z)

    lf_c = [lf_all[:, c * CHUNK:(c + 1) * CHUNK] for c in range(nck)]
    b_all = _split_dot(jnp.concatenate(lf_c, axis=0), triu)
    m_prev = m_ref[...]
    a_c, m_c, decay_c, wg_c = [], [], [], []
    for c in range(nck):
        b_r = b_all[c * B_HEADS:(c + 1) * B_HEADS]
        a_r = ig_all[:, c * CHUNK:(c + 1) * CHUNK] - b_r
        bl = b_r[:, CHUNK - 1:CHUNK]
        g_r = bl + a_r
        m_new = jnp.maximum(bl + m_prev, jnp.max(g_r, axis=1, keepdims=True))
        a_c.append(a_r)
        m_c.append(m_prev)
        decay_c.append(jnp.exp(bl + m_prev - m_new))
        wg_c.append(jnp.exp(g_r - m_new))
        m_prev = m_new
    m_ref[...] = m_prev

    units = [(c, h) for c in range(nck) for h in range(B_HEADS)]
    lmat = jnp.concatenate([jnp.where(causal, lf_c[c][h:h + 1, :], 0.0) for c, h in units], axis=0)
    dgm = jnp.concatenate([jnp.where(diag, wg_c[c][h:h + 1, :], 0.0) for c, h in units], axis=0)
    b_t_all = _split_dot(lmat, ones_blk)
    wg_t_all = _dot(dgm.astype(BF16), ones_blk)

    caugs = [ct_ref[h] for h in range(B_HEADS)]
    h_chunks = []
    for c in range(nck):
        sl = slice(c * CHUNK, (c + 1) * CHUNK)
        heads = []
        for h in range(B_HEADS):
            u = c * B_HEADS + h
            us = slice(u * CHUNK, (u + 1) * CHUNK)
            hs = slice(h * dh, (h + 1) * dh)
            qh, kh = q_b[sl, hs], k_b[sl, hs]
            vaug = jnp.concatenate([v_b[sl, hs], ones_blk], axis=1)
            m_row = m_c[c][h:h + 1, :]
            amat = jnp.where(causal, a_c[c][h:h + 1, :], NEG)
            mx = jnp.maximum(jnp.max(amat, axis=1, keepdims=True), m_row)
            w_intra = jnp.exp(amat - mx)
            w_state = jnp.exp(m_row - mx)
            s = _dot_nt(qh, kh) * w_intra
            kw = (kh.astype(F32) * wg_t_all[us]).astype(BF16)
            both = _dot(jnp.concatenate([s.astype(BF16), kw.T], axis=0), vaug)
            caug = caugs[h]
            naug = both[:CHUNK] + jnp.concatenate([w_state, w_state], axis=1) * _dot(qh, caug.astype(BF16))
            num, nq = naug[:, :dh], naug[:, dh:]
            hv = num / jnp.maximum(jnp.abs(nq), jnp.exp(-(b_t_all[us] + mx)))
            decay = decay_c[c][h:h + 1, :]
            caugs[h] = jnp.concatenate([decay, decay], axis=1) * caug + both[CHUNK:]
            hc = hv - jnp.mean(hv, axis=1, keepdims=True)
            hn = hc * lax.rsqrt(jnp.mean(hc * hc, axis=1, keepdims=True) + LN_EPS)
            heads.append(hn)
        h_chunks.append(jnp.concatenate(heads, axis=1))
    for h in range(B_HEADS):
        ct_ref[h] = caugs[h]
    hn_all = jnp.concatenate(h_chunks, axis=0) if nck > 1 else h_chunks[0]
    y_b = (hn_all * norm_w_ref[...] + skip_ref[...] * xc) * gate_z

    mix = jnp.concatenate([y_a, y_b], axis=1).astype(BF16)
    y = _dot(mix, w_out_ref[...])
    o_ref[0] = _ln(DN_ALPHA * x + y, ln_g_ref[...], ln_b_ref[...])


def _split_dot_left(a_bf16, b):
    hi = b.astype(BF16)
    lo = (b - hi.astype(F32)).astype(BF16)
    return _dot(a_bf16, hi) + _dot(a_bf16, lo)


def _block_diag_pairs(w):
    hh, d, _ = w.shape
    wp = w.reshape(hh // 2, 2, d, d)
    eye = jnp.eye(2, dtype=w.dtype)
    return jnp.einsum('pade,ab->padbe', wp, eye).reshape(hh // 2, 2 * d, 2 * d)


def _dot_head_pairs(x, w_ref):
    npair, w2, _ = w_ref.shape
    return jnp.concatenate([_dot(x[:, p * w2:(p + 1) * w2], w_ref[p]) for p in range(npair)], axis=1)


def even_mixer_layer(x, w_in, gm_ln_g, gm_ln_b, gm_ws, gm_bs, conv_w, conv_b, wq, wk, wv, w_if,
                     b_if, norm_w, skip, w_out, ln_g, ln_b, *, ts):
    bsz, seq, d = x.shape
    aw = gm_ln_g.shape[0]
    bw = conv_b.shape[0]
    agd = aw // A_GROUPS
    causal = jnp.tril(jnp.ones((CHUNK, CHUNK), dtype=bool))
    gm_w = jnp.where(causal[None], gm_ws, 0.0).astype(BF16)
    gm_bias = jnp.repeat(gm_bs.T, agd, axis=1)
    row = lambda a: a.reshape(1, -1)
    args = (x, w_in.astype(BF16), row(gm_ln_g), row(gm_ln_b), gm_w, gm_bias,
            conv_w, row(conv_b), _block_diag_pairs(wq).astype(BF16), _block_diag_pairs(wk).astype(BF16),
            _block_diag_pairs(wv).astype(BF16), w_if.T.astype(BF16),
            b_if.reshape(-1, 1), row(norm_w), row(skip), w_out.astype(BF16),
            row(ln_g), row(ln_b))
    in_specs = [pl.BlockSpec((1, ts, d), lambda b, j: (b, j, 0))] + [_full(a.shape) for a in args[1:]]
    dh = bw // B_HEADS
    assert dh == LANES and CHUNK == LANES
    return pl.pallas_call(
        functools.partial(_even_kernel, ts=ts, aw=aw, bw=bw),
        grid=(bsz, seq // ts),
        in_specs=in_specs,
        out_specs=pl.BlockSpec((1, ts, d), lambda b, j: (b, j, 0)),
        out_shape=jax.ShapeDtypeStruct((bsz, seq, d), F32),
        scratch_shapes=[pltpu.VMEM((8 + ts, bw), F32),
                        pltpu.VMEM((B_HEADS, dh, dh + LANES), F32),
                        pltpu.VMEM((B_HEADS, LANES), F32)],
        compiler_params=_cparams(("arbitrary", "arbitrary")),
        name="even_mixer",
    )(*args)


def _memfold_kernel(mem_ref, wkv_ref, wq_ref, wo_ref, wqk_ref, vo_ref, *, d):
    dh = d // X_HEADS
    m_len = mem_ref.shape[1]
    kv = _dot(mem_ref[0].astype(BF16), wkv_ref[...])
    k = (kv[:, :d] * (dh ** -0.5 * LOG2E)).astype(BF16)
    v = kv[:, d:].astype(BF16)
    for h in range(X_HEADS):
        hs = slice(h * dh, (h + 1) * dh)
        ms = slice(h * m_len, (h + 1) * m_len)
        wqk_ref[0, :, ms] = _dot_nt(wq_ref[:, hs], k[:, hs]).astype(BF16)
        vo_ref[0, ms, :] = _dot(v[:, hs], wo_ref[hs, :]).astype(BF16)


def memory_fold(mem, wkv, wq, wo):
    bsz, m_len, d = mem.shape
    hm = X_HEADS * m_len
    args = (mem, wkv.astype(BF16), wq.astype(BF16), wo.astype(BF16))
    return pl.pallas_call(
        functools.partial(_memfold_kernel, d=d),
        grid=(bsz,),
        in_specs=[pl.BlockSpec((1, m_len, d), lambda b: (b, 0, 0))] + [_full(a.shape) for a in args[1:]],
        out_specs=[pl.BlockSpec((1, d, hm), lambda b: (b, 0, 0)),
                   pl.BlockSpec((1, hm, d), lambda b: (b, 0, 0))],
        out_shape=[jax.ShapeDtypeStruct((bsz, d, hm), BF16),
                   jax.ShapeDtypeStruct((bsz, hm, d), BF16)],
        compiler_params=_cparams(("arbitrary",)),
        name="memory_fold",
    )(*args)


ROUTE_W = 128
XATTN_TS = 1024
DEST_ROWS = 8


def _xattn_kernel(x_ref, wqk_ref, vo_ref, ln_g_ref, ln_b_ref, wr_ref, br_ref,
                  o_ref, xp_ref, route_ref, cnt_ref):
    m_len = wqk_ref.shape[2] // X_HEADS

    @pl.when(pl.program_id(1) == 0)
    def _():
        cnt_ref[...] = jnp.zeros_like(cnt_ref)

    x = x_ref[0]
    ts = x.shape[0]
    xb = x.astype(BF16)
    probs = []
    for h in range(X_HEADS):
        s = _dot(xb, wqk_ref[0, :, h * m_len:(h + 1) * m_len])
        p = jnp.exp2(s - jnp.max(s, axis=1, keepdims=True))
        probs.append((p / jnp.sum(p, axis=1, keepdims=True)).astype(BF16))
    y = _dot(jnp.concatenate(probs, axis=1), vo_ref[0])
    x2 = _ln(DN_ALPHA * x + y, ln_g_ref[...], ln_b_ref[...])
    o_ref[0] = x2
    xp_ref[0] = _pack_halves(x2)

    x_hi = x2.astype(BF16)
    x_lo = (x2 - x_hi.astype(F32)).astype(BF16)
    logits = (_dot(x_hi, wr_ref[0]) + _dot(x_lo, wr_ref[0]) + _dot(x_hi, wr_ref[1])) + br_ref[...]
    lane = lax.broadcasted_iota(jnp.int32, (ts, ROUTE_W), 1)
    is_g = lane < N_GROUPS
    lg = jnp.where(is_g, logits, NEG)
    mg = jnp.max(lg, axis=1, keepdims=True)
    gi = jnp.min(jnp.where(jnp.logical_and(is_g, lg == mg), lane, ROUTE_W), axis=1, keepdims=True)
    gate_g = 1.0 / jnp.sum(jnp.where(is_g, jnp.exp(lg - mg), 0.0), axis=1, keepdims=True)
    lo = N_GROUPS + gi * EXPERTS_PER_GROUP
    in_grp = jnp.logical_and(lane >= lo, lane < lo + EXPERTS_PER_GROUP)
    le = jnp.where(in_grp, logits, NEG)
    v1 = jnp.max(le, axis=1, keepdims=True)
    i1 = jnp.min(jnp.where(jnp.logical_and(in_grp, le == v1), lane, ROUTE_W), axis=1, keepdims=True)
    le2 = jnp.where(lane == i1, NEG, le)
    v2 = jnp.max(le2, axis=1, keepdims=True)
    i2 = jnp.min(jnp.where(jnp.logical_and(in_grp, le2 == v2), lane, ROUTE_W), axis=1, keepdims=True)
    e21 = jnp.exp(v2 - v1)
    p1 = 1.0 / (1.0 + e21)
    p2 = e21 * p1
    e1 = (i1 - N_GROUPS).astype(F32)
    e2 = (i2 - N_GROUPS).astype(F32)
    rec = jnp.where(lane == 0, e1, 0.0)
    rec = jnp.where(lane == 1, e2, rec)
    rec = jnp.where(lane == 2, gate_g * p1, rec)
    rec = jnp.where(lane == 3, gate_g * p2, rec)
    route_ref[0] = rec
    sel = jnp.logical_or(lane == i1 - N_GROUPS, lane == i2 - N_GROUPS)
    cnt_ref[0] += jnp.sum(jnp.where(sel, 1.0, 0.0), axis=0, keepdims=True)


def xattn_router_layer(x, wqk, vo, ln_g, ln_b, w_rg, b_rg, w_re, b_re, *, ts):
    bsz, seq, d = x.shape
    hm = wqk.shape[2]
    wr = jnp.zeros((d, ROUTE_W), F32).at[:, :N_GROUPS].set(w_rg).at[:, N_GROUPS:N_GROUPS + N_EXPERTS].set(w_re)
    br = jnp.zeros((1, ROUTE_W), F32).at[0, :N_GROUPS].set(b_rg).at[0, N_GROUPS:N_GROUPS + N_EXPERTS].set(b_re)
    wr_hi = wr.astype(BF16)
    wr_lo = (wr - wr_hi.astype(F32)).astype(BF16)
    wr = jnp.stack([wr_hi, wr_lo])
    args = (x, wqk, vo, ln_g.reshape(1, -1), ln_b.reshape(1, -1), wr, br)
    in_specs = [pl.BlockSpec((1, ts, d), lambda b, j: (b, j, 0)),
                pl.BlockSpec((1, d, hm), lambda b, j: (b, 0, 0)),
                pl.BlockSpec((1, hm, d), lambda b, j: (b, 0, 0))] + [_full(a.shape) for a in args[3:]]
    return pl.pallas_call(
        _xattn_kernel,
        grid=(bsz, seq // ts),
        in_specs=in_specs,
        out_specs=[pl.BlockSpec((1, ts, d), lambda b, j: (b, j, 0)),
                   pl.BlockSpec((1, ts, d // 2), lambda b, j: (b, j, 0)),
                   pl.BlockSpec((1, ts, ROUTE_W), lambda b, j: (b, j, 0)),
                   pl.BlockSpec((1, 1, ROUTE_W), lambda b, j: (b, 0, 0))],
        out_shape=[jax.ShapeDtypeStruct((bsz, seq, d), F32),
                   jax.ShapeDtypeStruct((bsz, seq, d // 2), jnp.int32),
                   jax.ShapeDtypeStruct((bsz, seq, ROUTE_W), F32),
                   jax.ShapeDtypeStruct((bsz, 1, ROUTE_W), F32)],
        compiler_params=_cparams(("arbitrary", "arbitrary")),
        name="xattn_router",
    )(*args)


def _moe_output(x2, g0, g1, rec, ln_g, ln_b):
    a_lo, a_hi = _unpack_halves(g0)
    b_lo, b_hi = _unpack_halves(g1)
    w0, w1 = rec[:, 2:3], rec[:, 3:4]
    y = jnp.concatenate([w0 * a_lo + w1 * b_lo, w0 * a_hi + w1 * b_hi], axis=1)
    return _ln(DN_ALPHA * x2 + y, ln_g, ln_b)


def _swa_kernel(x_ref, g0_ref, g1_ref, route_ref, pln_g_ref, pln_b_ref,
                wqkv_ref, bqkv_ref, cos_ref, sin_ref, sink_ref, wo_ref, ln_g_ref, ln_b_ref,
                o_ref, kprev, vprev, *, ts, cq, ckv):
    j = pl.program_id(1)
    nb = ts // CHUNK
    dh = C_HEAD_DIM
    grp = (cq // dh) // C_KV_HEADS

    @pl.when(j == 0)
    def _():
        kprev[...] = jnp.zeros_like(kprev)
        vprev[...] = jnp.zeros_like(vprev)

    x = _moe_output(x_ref[0], g0_ref[0, 0], g1_ref[0, 0], route_ref[0], pln_g_ref[...], pln_b_ref[...])
    qkv = _dot(x.astype(BF16), wqkv_ref[...]) + bqkv_ref[...]
    cos = cos_ref[...]
    sin = sin_ref[...]
    lane = lax.broadcasted_iota(jnp.int32, (ts, LANES), 1)
    first_half = (lane % dh) < (dh // 2)

    def rope(t):
        outs = []
        for c in range(t.shape[1] // LANES):
            tc = t[:, c * LANES:(c + 1) * LANES]
            rot = jnp.where(first_half, pltpu.roll(tc, LANES - dh // 2, 1), pltpu.roll(tc, dh // 2, 1))
            outs.append(tc * cos + rot * sin)
        return jnp.concatenate(outs, axis=1)

    q = rope(qkv[:, :cq]).astype(BF16)
    k = rope(qkv[:, cq:cq + 2 * ckv]).astype(BF16)
    v = qkv[:, cq + 2 * ckv:].astype(BF16)

    r_i = lax.broadcasted_iota(jnp.int32, (CHUNK, 2 * CHUNK), 0)
    c_i = lax.broadcasted_iota(jnp.int32, (CHUNK, 2 * CHUNK), 1)
    band = jnp.logical_and(c_i > r_i, c_i <= r_i + CHUNK)
    sink_col = c_i == 0
    lane_k = lax.broadcasted_iota(jnp.int32, (2 * CHUNK, LANES), 1)
    key_row = lax.broadcasted_iota(jnp.int32, (2 * CHUNK, LANES), 0)
    lane_q = lax.broadcasted_iota(jnp.int32, (CHUNK, LANES), 1)
    ones_blk = jnp.ones((2 * CHUNK, LANES), BF16)
    zero_b = jnp.zeros((), BF16)
    blocks = []
    for c in range(nb):
        sl = slice(c * CHUNK, (c + 1) * CHUNK)
        if c == 0:
            kb = jnp.concatenate([kprev[...].astype(BF16), k[sl]], axis=0)
            vb = jnp.concatenate([vprev[...].astype(BF16), v[sl]], axis=0)
            first_key = jnp.where(j > 0, 0, CHUNK)
            valid = jnp.logical_and(band, c_i >= first_key)
        else:
            kb = k[(c - 1) * CHUNK:(c + 1) * CHUNK]
            vb = v[(c - 1) * CHUNK:(c + 1) * CHUNK]
            valid = band
        tiles = []
        for h in range(C_KV_HEADS):
            kd = kb[:, h * LANES:(h + 1) * LANES]
            vd = vb[:, h * LANES:(h + 1) * LANES]
            k_lo = jnp.where(lane_k < dh, kd, zero_b)
            k_hi = jnp.where(lane_k >= dh, kd, zero_b)
            vz = jnp.where(key_row == 0, zero_b, vd)
            q2 = jnp.concatenate([q[sl, (2 * h) * LANES:(2 * h + 1) * LANES],
                                  q[sl, (2 * h + 1) * LANES:(2 * h + 2) * LANES]], axis=0)
            s_lo = _dot_nt(q2, k_lo)
            s_hi = _dot_nt(q2, k_hi)
            parts = []
            for qk, g in ((s_lo[:CHUNK], 0), (s_lo[CHUNK:], 2), (s_hi[:CHUNK], 1), (s_hi[CHUNK:], 3)):
                sink = sink_ref[h * grp + g]
                parts.append(jnp.where(valid, qk, jnp.where(sink_col, sink, NEG)))
            s = jnp.concatenate(parts, axis=0)
            p = jnp.exp2(s - jnp.max(s, axis=1, keepdims=True)).astype(BF16)
            den = _dot(p, ones_blk)
            o2 = _dot(p, vz) / den
            tiles.append(jnp.where(lane_q < dh, o2[:CHUNK], o2[2 * CHUNK:3 * CHUNK]))
            tiles.append(jnp.where(lane_q < dh, o2[CHUNK:2 * CHUNK], o2[3 * CHUNK:]))
        blocks.append(jnp.concatenate(tiles, axis=1))
    kprev[...] = k[(nb - 1) * CHUNK:].astype(F32)
    vprev[...] = v[(nb - 1) * CHUNK:].astype(F32)
    att = (jnp.concatenate(blocks, axis=0) if nb > 1 else blocks[0]).astype(BF16)
    y = _dot(att, wo_ref[...])
    o_ref[0] = _ln(DN_ALPHA * x + y, ln_g_ref[...], ln_b_ref[...])


def swa_mixer_layer(pending, w_qkv, b_qkv, sinks, w_o, ln_g, ln_b, *, ts):
    x, g, route, pln_g, pln_b = pending
    bsz, seq, d = x.shape
    g = g.reshape(2, bsz, seq, d // 2)
    cq = w_o.shape[0]
    ckv = (w_qkv.shape[1] - cq) // 2
    dh = C_HEAD_DIM
    inv = ROPE_THETA ** (-jnp.arange(0, dh, 2, dtype=F32) / dh)
    ang = jnp.arange(seq, dtype=F32)[:, None] * inv[None, :]
    reps = LANES // (dh // 2)
    sign = jnp.tile(jnp.concatenate([-jnp.ones((dh // 2,), F32), jnp.ones((dh // 2,), F32)]), LANES // dh)
    cos_t = jnp.tile(jnp.cos(ang), (1, reps))
    sin_t = jnp.tile(jnp.sin(ang), (1, reps)) * sign[None, :]
    assert 2 * dh == LANES

    def dup_heads(t):
        th = t.reshape(t.shape[:-1] + (ckv // dh, dh))
        return jnp.concatenate([th, th], axis=-1).reshape(t.shape[:-1] + (2 * ckv,))

    qs = dh ** -0.5 * LOG2E
    w_all = jnp.concatenate([w_qkv[:, :cq] * qs, dup_heads(w_qkv[:, cq:cq + ckv]),
                             dup_heads(w_qkv[:, cq + ckv:])], axis=1)
    b_all = jnp.concatenate([b_qkv[:cq] * qs, dup_heads(b_qkv[cq:cq + ckv]), dup_heads(b_qkv[cq + ckv:])])
    args = (x, g, g, route, pln_g.reshape(1, -1), pln_b.reshape(1, -1),
            w_all.astype(BF16), b_all.reshape(1, -1), cos_t, sin_t, sinks.astype(F32) * LOG2E,
            w_o.astype(BF16), ln_g.reshape(1, -1), ln_b.reshape(1, -1))
    in_specs = [pl.BlockSpec((1, ts, d), lambda b, j: (b, j, 0)),
                pl.BlockSpec((1, 1, ts, d // 2), lambda b, j: (0, b, j, 0)),
                pl.BlockSpec((1, 1, ts, d // 2), lambda b, j: (1, b, j, 0)),
                pl.BlockSpec((1, ts, ROUTE_W), lambda b, j: (b, j, 0)),
                _full((1, d)), _full((1, d)),
                _full(args[6].shape), _full(args[7].shape),
                pl.BlockSpec((ts, LANES), lambda b, j: (j, 0)),
                pl.BlockSpec((ts, LANES), lambda b, j: (j, 0)),
                pl.BlockSpec(memory_space=pltpu.SMEM),
                _full(args[11].shape), _full(args[12].shape), _full(args[13].shape)]
    return pl.pallas_call(
        functools.partial(_swa_kernel, ts=ts, cq=cq, ckv=ckv),
        grid=(bsz, seq // ts),
        in_specs=in_specs,
        out_specs=pl.BlockSpec((1, ts, d), lambda b, j: (b, j, 0)),
        out_shape=jax.ShapeDtypeStruct((bsz, seq, d), F32),
        scratch_shapes=[pltpu.VMEM((CHUNK, 2 * ckv), F32), pltpu.VMEM((CHUNK, 2 * ckv), F32)],
        compiler_params=_cparams(("arbitrary", "arbitrary")),
        name="swa_mixer",
    )(*args)


def _slot_kernel(route_ref, pstart_ref, dest_ref, carry_ref, *, tb):
    @pl.when(pl.program_id(0) == 0)
    def _():
        carry_ref[...] = pstart_ref[...]

    rec = route_ref[...]
    lane = lax.broadcasted_iota(jnp.int32, (tb, ROUTE_W), 1)
    e0 = rec[:, 0:1].astype(jnp.int32)
    e1 = rec[:, 1:2].astype(jnp.int32)
    oh0 = lane == e0
    oh1 = lane == e1
    ohs = jnp.where(jnp.logical_or(oh0, oh1), 1.0, 0.0)
    r = lax.broadcasted_iota(jnp.int32, (tb, tb), 0)
    c = lax.broadcasted_iota(jnp.int32, (tb, tb), 1)
    before = jnp.where(c < r, 1.0, 0.0).astype(BF16)
    prefix = _dot(before, ohs.astype(BF16)) + carry_ref[...]
    d0 = jnp.sum(jnp.where(oh0, prefix, 0.0), axis=1, keepdims=True)
    d1 = jnp.sum(jnp.where(oh1, prefix, 0.0), axis=1, keepdims=True)
    dest = jnp.where(lane == 0, d0, jnp.where(lane == 1, d1, 0.0))
    dest_ref[...] = dest.T[:DEST_ROWS].astype(jnp.int32)
    carry_ref[...] += jnp.sum(ohs, axis=0, keepdims=True)


def moe_slots(route, pstart, *, tb):
    n = route.shape[0]
    return pl.pallas_call(
        functools.partial(_slot_kernel, tb=tb),
        grid=(n // tb,),
        in_specs=[pl.BlockSpec((tb, ROUTE_W), lambda i: (i, 0)), _full((1, ROUTE_W))],
        out_specs=pl.BlockSpec((DEST_ROWS, tb), lambda i: (0, i)),
        out_shape=jax.ShapeDtypeStruct((DEST_ROWS, n), jnp.int32),
        scratch_shapes=[pltpu.VMEM((1, ROUTE_W), F32)],
        compiler_params=_cparams(("arbitrary",)),
        name="moe_slots",
    )(route, pstart)


def _ffn_kernel(blk_exp_ref, new_exp_ref, nblk_ref, xs_ref, w1_ref, w3_ref, w2_ref, ys_ref,
                w1_b, w3_b, w2_b):
    i = pl.program_id(0)
    used = i < nblk_ref[0]

    @pl.when(jnp.logical_and(used, new_exp_ref[i] == 1))
    def _():
        w1_b[...] = w1_ref[0, 0].astype(BF16)
        w3_b[...] = w3_ref[0, 0].astype(BF16)
        w2_b[...] = w2_ref[0, 0].astype(BF16)

    @pl.when(used)
    def _():
        x_lo, x_hi = _unpack_halves(xs_ref[...])
        x_lo, x_hi = x_lo.astype(BF16), x_hi.astype(BF16)
        dl = x_lo.shape[1]
        h1 = _dot(x_lo, w1_b[:dl, :]) + _dot(x_hi, w1_b[dl:, :])
        h3 = _dot(x_lo, w3_b[:dl, :]) + _dot(x_hi, w3_b[dl:, :])
        h = (_silu(h1) * h3).astype(BF16)
        ys_ref[...] = _pack_halves(_dot(h, w2_b[...]))

    @pl.when(jnp.logical_not(used))
    def _():
        ys_ref[...] = jnp.zeros_like(ys_ref)


def moe_ffn(xs, blk_exp, new_exp, nblk, w1, w3, w2, *, layer, bm):
    n_pad, dp = xs.shape
    d = 2 * dp
    de = w1.shape[3]
    n_blk = n_pad // bm

    def x_map(i, be, ne, nb):
        return (jnp.minimum(i, nb[0] - 1), 0)

    def w_map(i, be, ne, nb):
        return (layer, be[i], 0, 0)

    return pl.pallas_call(
        _ffn_kernel,
        grid_spec=pltpu.PrefetchScalarGridSpec(
            num_scalar_prefetch=3,
            grid=(n_blk,),
            in_specs=[pl.BlockSpec((bm, dp), x_map),
                      pl.BlockSpec((1, 1, d, de), w_map),
                      pl.BlockSpec((1, 1, d, de), w_map),
                      pl.BlockSpec((1, 1, de, d), w_map)],
            out_specs=pl.BlockSpec((bm, dp), lambda i, be, ne, nb: (i, 0)),
            scratch_shapes=[pltpu.VMEM((d, de), BF16), pltpu.VMEM((d, de), BF16),
                            pltpu.VMEM((de, d), BF16)]),
        out_shape=jax.ShapeDtypeStruct((n_pad, dp), jnp.int32),
        compiler_params=_cparams(("arbitrary",)),
        name="moe_ffn",
    )(blk_exp, new_exp, nblk, xs, w1, w3, w2)


def _combine_kernel(x_ref, g0_ref, g1_ref, route_ref, ln_g_ref, ln_b_ref, o_ref):
    o_ref[...] = _moe_output(x_ref[...], g0_ref[0], g1_ref[0], route_ref[...], ln_g_ref[...], ln_b_ref[...])


def moe_combine(x, g, route, ln_g, ln_b, *, tb):
    n, d = x.shape
    row_spec = pl.BlockSpec((tb, d), lambda i: (i, 0))
    return pl.pallas_call(
        _combine_kernel,
        grid=(n // tb,),
        in_specs=[row_spec,
                  pl.BlockSpec((1, tb, d // 2), lambda i: (0, i, 0)),
                  pl.BlockSpec((1, tb, d // 2), lambda i: (1, i, 0)),
                  pl.BlockSpec((tb, ROUTE_W), lambda i: (i, 0)),
                  _full((1, d)), _full((1, d))],
        out_specs=row_spec,
        out_shape=jax.ShapeDtypeStruct((n, d), F32),
        compiler_params=_cparams(("arbitrary",)),
        name="moe_combine",
    )(x, g, g, route, ln_g.reshape(1, -1), ln_b.reshape(1, -1))


MOE_BM = 1024
MIXER_TS = 1024


def hierarchical_moe_layer(x2, xp, route, counts, w1, w3, w2, ln_g, ln_b, *, layer, defer_combine):
    bsz, seq, d = x2.shape
    n = bsz * seq
    bm = MOE_BM
    rt = route.reshape(n, ROUTE_W)
    tb = min(512, n)
    n_blk = (2 * n) // bm + N_EXPERTS
    cnt = counts.sum(axis=0)[0, :N_EXPERTS].astype(jnp.int32)
    pcnt = (cnt + bm - 1) // bm * bm
    pends = jnp.cumsum(pcnt)
    pstart = pends - pcnt
    nblk = (pends[-1] // bm).astype(jnp.int32).reshape(1)
    blk_row = jnp.arange(n_blk, dtype=jnp.int32) * bm
    blk_exp = jnp.minimum(jnp.sum((pends[None, :] <= blk_row[:, None]).astype(jnp.int32), axis=1),
                          N_EXPERTS - 1)
    last_exp = blk_exp[jnp.maximum(nblk[0] - 1, 0)]
    blk_exp = jnp.where(jnp.arange(n_blk) < nblk[0], blk_exp, last_exp)
    new_exp = jnp.concatenate([jnp.ones((1,), jnp.int32),
                               (blk_exp[1:] != blk_exp[:-1]).astype(jnp.int32)])
    pstart_rec = jnp.zeros((1, ROUTE_W), F32).at[0, :N_EXPERTS].set(pstart.astype(F32))
    dest = moe_slots(rt, pstart_rec, tb=tb)
    xs = moe_dispatch(xp.reshape(n, d // 2), dest[0], dest[1], n_blk * bm)
    ys = moe_ffn(xs, blk_exp, new_exp, nblk, w1, w3, w2, layer=layer, bm=bm)
    g = moe_gather(ys, dest[0], dest[1])
    if defer_combine:
        return x2, g, route, ln_g, ln_b
    return moe_combine(x2.reshape(n, d), g, rt, ln_g, ln_b, tb=tb).reshape(bsz, seq, d)


SC_ROWS = 128


def _sc_mesh():
    return plsc.VectorSubcoreMesh(core_axis_name="c", subcore_axis_name="s")


def moe_dispatch(xf, dest0, dest1, n_pad):
    n, d = xf.shape
    info = plsc.get_sparse_core_info()
    nw = info.num_cores * info.num_subcores
    per_w = n // nw
    r = min(SC_ROWS, per_w)

    def body(x_hbm, d0_hbm, d1_hbm, xs_hbm, i0_v, i1_v, rows_v, sem):
        wid = lax.axis_index("s") * info.num_cores + lax.axis_index("c")

        @pl.loop(0, per_w // r)
        def _(c):
            base = pl.multiple_of(wid * per_w + c * r, 8)
            pltpu.sync_copy(d0_hbm.at[pl.ds(base, r)], i0_v)
            pltpu.sync_copy(d1_hbm.at[pl.ds(base, r)], i1_v)
            pltpu.sync_copy(x_hbm.at[pl.ds(base, r)], rows_v)
            pltpu.async_copy(rows_v, xs_hbm.at[i0_v], sem).wait()
            pltpu.async_copy(rows_v, xs_hbm.at[i1_v], sem).wait()

    return pl.kernel(
        body, out_type=jax.ShapeDtypeStruct((n_pad, d), xf.dtype), mesh=_sc_mesh(),
        scratch_types=[pltpu.VMEM((r,), jnp.int32), pltpu.VMEM((r,), jnp.int32),
                       pltpu.VMEM((r, d), xf.dtype), pltpu.SemaphoreType.DMA],
        name="moe_dispatch",
    )(xf, dest0, dest1)


def moe_gather(ys, dest0, dest1):
    n = dest0.shape[0]
    d = ys.shape[1]
    info = plsc.get_sparse_core_info()
    nw = info.num_cores * info.num_subcores
    per_w = n // nw
    r = min(SC_ROWS, per_w)

    def body(ys_hbm, d0_hbm, d1_hbm, g_hbm, i_v, rows_v, sem):
        wid = lax.axis_index("s") * info.num_cores + lax.axis_index("c")

        @pl.loop(0, per_w // r)
        def _(c):
            base = pl.multiple_of(wid * per_w + c * r, 8)
            for k, d_hbm in enumerate((d0_hbm, d1_hbm)):
                pltpu.sync_copy(d_hbm.at[pl.ds(base, r)], i_v)
                pltpu.async_copy(ys_hbm.at[i_v], rows_v, sem).wait()
                pltpu.sync_copy(rows_v, g_hbm.at[k, pl.ds(base, r)])

    return pl.kernel(
        body, out_type=jax.ShapeDtypeStruct((2, n, d), ys.dtype), mesh=_sc_mesh(),
        scratch_types=[pltpu.VMEM((r,), jnp.int32), pltpu.VMEM((r, d), ys.dtype),
                       pltpu.SemaphoreType.DMA],
        name="moe_gather",
    )(ys, dest0, dest1)


def kernel(x, mem, ln_g, ln_b, ev_w_in, ev_gm_ln_g, ev_gm_ln_b, ev_gm_ws, ev_gm_bs, ev_conv_w, ev_conv_b, ev_wq, ev_wk, ev_wv, ev_w_if, ev_b_if, ev_norm_w, ev_skip, ev_w_out, od_w_qkv, od_b_qkv, od_sinks, od_w_o, xa_wq, xa_wkv, xa_wo, moe_w_rg, moe_b_rg, moe_w_re, moe_b_re, moe_w1, moe_w3, moe_w2):
    bsz, seq, d = x.shape
    depth = ln_g.shape[0]
    ts = min(MIXER_TS, seq)
    for l in range(depth):
        if l % 2 == 0:
            e = l // 2
            x = even_mixer_layer(x, ev_w_in[e], ev_gm_ln_g[e], ev_gm_ln_b[e], ev_gm_ws[e], ev_gm_bs[e],
                                 ev_conv_w[e], ev_conv_b[e], ev_wq[e], ev_wk[e], ev_wv[e], ev_w_if[e],
                                 ev_b_if[e], ev_norm_w[e], ev_skip[e], ev_w_out[e],
                                 ln_g[l, 0], ln_b[l, 0], ts=ts)
        else:
            o = l // 2
            x = swa_mixer_layer(x, od_w_qkv[o], od_b_qkv[o], od_sinks[o], od_w_o[o],
                                ln_g[l, 0], ln_b[l, 0], ts=ts)
        wqk, vo = memory_fold(mem, xa_wkv[l], xa_wq[l], xa_wo[l])
        x, xp, route, counts = xattn_router_layer(x, wqk, vo, ln_g[l, 1], ln_b[l, 1],
                                                  moe_w_rg[l], moe_b_rg[l], moe_w_re[l], moe_b_re[l],
                                                  ts=min(XATTN_TS, seq))
        x = hierarchical_moe_layer(x, xp, route, counts, moe_w1, moe_w3, moe_w2,
                                   ln_g[l, 2], ln_b[l, 2], layer=l,
                                   defer_combine=(l + 1 < depth and (l + 1) % 2 == 1))
    return x
```

```python
import functools
import math

import jax
import jax.numpy as jnp
from jax import lax
from jax.experimental import pallas as pl
from jax.experimental.pallas import tpu as pltpu
from jax.experimental.pallas import tpu_sc as plsc

F32 = jnp.float32
BF16 = jnp.bfloat16

A_GROUPS = 4
CHUNK = 128
B_HEADS = 4
B_CONV = 4
C_HEAD_DIM = 64
C_KV_HEADS = 4
X_HEADS = 4
N_GROUPS = 4
EXPERTS_PER_GROUP = 8
N_EXPERTS = N_GROUPS * EXPERTS_PER_GROUP
ROPE_THETA = 10000.0
LN_EPS = 1e-5
DEPTH = 2
DN_ALPHA = (2 * DEPTH) ** 0.25

LANES = 128
VMEM_LIMIT = 48 * 1024 * 1024
NEG = -1e30


def _cparams(sem):
    return pltpu.CompilerParams(dimension_semantics=sem, vmem_limit_bytes=VMEM_LIMIT)


def _full(shape):
    nd = len(shape)
    return pl.BlockSpec(shape, lambda *_: (0,) * nd)


def _dot(a, b):
    return jnp.dot(a, b, preferred_element_type=F32)


def _dot_nt(a, b):
    return lax.dot_general(a, b, (((1,), (1,)), ((), ())), preferred_element_type=F32)


def _split_dot(a, b_bf16):
    hi = a.astype(BF16)
    lo = (a - hi.astype(F32)).astype(BF16)
    return _dot(hi, b_bf16) + _dot(lo, b_bf16)


def _ln(x, g, b):
    mu = jnp.mean(x, axis=-1, keepdims=True)
    xc = x - mu
    var = jnp.mean(xc * xc, axis=-1, keepdims=True)
    return xc * lax.rsqrt(var + LN_EPS) * g + b


LOG2E = math.log2(math.e)


def _silu(x):
    return x * (1.0 / (1.0 + jnp.exp2(x * -LOG2E)))


def _gelu(x):
    return 0.5 * x * (1.0 + jnp.tanh(math.sqrt(2.0 / math.pi) * (x + 0.044715 * (x * x * x))))


def _pack_halves(x):
    c = x.shape[1] // 2
    lo = lax.bitcast_convert_type(x[:, :c].astype(BF16).astype(F32), jnp.uint32)
    hi = lax.bitcast_convert_type(x[:, c:].astype(BF16).astype(F32), jnp.uint32)
    return lax.bitcast_convert_type((lo >> 16) | hi, jnp.int32)


def _unpack_halves(p):
    u = lax.bitcast_convert_type(p, jnp.uint32)
    lo = lax.bitcast_convert_type(u << 16, F32)
    hi = lax.bitcast_convert_type(u & jnp.uint32(0xFFFF0000), F32)
    return lo, hi


def _log_sigmoid(x):
    return jnp.minimum(x, 0.0) - jnp.log(1.0 + jnp.exp(-jnp.abs(x)))


def _even_kernel(x_ref, w_in_ref, gm_g_ref, gm_b_ref, gm_w_ref, gm_bias_ref,
                 conv_w_ref, conv_b_ref, wq_ref, wk_ref, wv_ref, wif_t_ref, bif_t_ref,
                 norm_w_ref, skip_ref, w_out_ref, ln_g_ref, ln_b_ref,
                 o_ref,
                 xm_buf, ct_ref, m_ref, *, ts, aw, bw):
    dh = bw // B_HEADS
    agd = aw // A_GROUPS
    nck = ts // CHUNK
    pad = 8
    j = pl.program_id(1)

    @pl.when(j == 0)
    def _():
        xm_buf[0:pad, :] = jnp.zeros((pad, bw), F32)
        ct_ref[...] = jnp.zeros_like(ct_ref)
        m_ref[...] = jnp.zeros_like(m_ref)

    row = lax.broadcasted_iota(jnp.int32, (CHUNK, CHUNK), 0)
    col = lax.broadcasted_iota(jnp.int32, (CHUNK, CHUNK), 1)
    causal = col <= row
    diag = col == row
    triu = jnp.where(row <= col, 1.0, 0.0).astype(BF16)
    ones_blk = jnp.ones((CHUNK, LANES), BF16)

    x = x_ref[0]
    proj = _dot(x.astype(BF16), w_in_ref[...])
    a_u = _gelu(proj[:, :aw])
    a_v = _gelu(proj[:, aw:2 * aw])
    xm = proj[:, 2 * aw:2 * aw + bw]
    z = proj[:, 2 * aw + bw:]

    vn = _ln(a_v, gm_g_ref[...], gm_b_ref[...]).astype(BF16)
    ya_chunks = []
    for c in range(nck):
        cols = []
        for g in range(A_GROUPS):
            v_cg = vn[c * CHUNK:(c + 1) * CHUNK, g * agd:(g + 1) * agd]
            cols.append(_dot(gm_w_ref[g], v_cg))
        ya_chunks.append(jnp.concatenate(cols, axis=1) + gm_bias_ref[...])
    y_a = a_u * jnp.concatenate(ya_chunks, axis=0)

    xm_buf[pad:pad + ts, :] = xm
    conv = conv_b_ref[...] + conv_w_ref[B_CONV - 1:B_CONV, :] * xm
    for k in range(B_CONV - 1):
        sh = B_CONV - 1 - k
        conv = conv + conv_w_ref[k:k + 1, :] * xm_buf[pad - sh:pad - sh + ts, :]
    xm_buf[pad - (B_CONV - 1):pad, :] = xm_buf[pad + ts - (B_CONV - 1):pad + ts, :]
    xc = _silu(conv)
    xc_b = xc.astype(BF16)
    q = _dot_head_pairs(xc_b, wq_ref)
    k_ = _dot_head_pairs(xc_b, wk_ref)
    v = _dot_head_pairs(xm.astype(BF16), wv_ref)
    gate_in = jnp.concatenate([q, k_, v], axis=1).astype(BF16)
    gates_t = _dot_nt(wif_t_ref[...], gate_in) + bif_t_ref[...]
    ig_all = gates_t[:B_HEADS, :]
    lf_all = _log_sigmoid(gates_t[B_HEADS:, :])
    q_b = q.astype(BF16)
    k_b = (k_ * dh ** -0.5).astype(BF16)
    v_b = v.astype(BF16)
    gate_z = _silu(z)

    lf_c = [lf_all[:, c * CHUNK:(c + 1) * CHUNK] for c in range(nck)]
    b_all = _split_dot(jnp.concatenate(lf_c, axis=0), triu)
    m_prev = m_ref[...]
    a_c, m_c, decay_c, wg_c = [], [], [], []
    for c in range(nck):
        b_r = b_all[c * B_HEADS:(c + 1) * B_HEADS]
        a_r = ig_all[:, c * CHUNK:(c + 1) * CHUNK] - b_r
        bl = b_r[:, CHUNK - 1:CHUNK]
        g_r = bl + a_r
        m_new = jnp.maximum(bl + m_prev, jnp.max(g_r, axis=1, keepdims=True))
        a_c.append(a_r)
        m_c.append(m_prev)
        decay_c.append(jnp.exp(bl + m_prev - m_new))
        wg_c.append(jnp.exp(g_r - m_new))
        m_prev = m_new
    m_ref[...] = m_prev

    units = [(c, h) for c in range(nck) for h in range(B_HEADS)]
    lmat = jnp.concatenate([jnp.where(causal, lf_c[c][h:h + 1, :], 0.0) for c, h in units], axis=0)
    dgm = jnp.concatenate([jnp.where(diag, wg_c[c][h:h + 1, :], 0.0) for c, h in units], axis=0)
    b_t_all = _split_dot(lmat, ones_blk)
    wg_t_all = _dot(dgm.astype(BF16), ones_blk)

    caugs = [ct_ref[h] for h in range(B_HEADS)]
    h_chunks = []
    for c in range(nck):
        sl = slice(c * CHUNK, (c + 1) * CHUNK)
        heads = []
        for h in range(B_HEADS):
            u = c * B_HEADS + h
            us = slice(u * CHUNK, (u + 1) * CHUNK)
            hs = slice(h * dh, (h + 1) * dh)
            qh, kh = q_b[sl, hs], k_b[sl, hs]
            vaug = jnp.concatenate([v_b[sl, hs], ones_blk], axis=1)
            m_row = m_c[c][h:h + 1, :]
            amat = jnp.where(causal, a_c[c][h:h + 1, :], NEG)
            mx = jnp.maximum(jnp.max(amat, axis=1, keepdims=True), m_row)
            w_intra = jnp.exp(amat - mx)
            w_state = jnp.exp(m_row - mx)
            s = _dot_nt(qh, kh) * w_intra
            kw = (kh.astype(F32) * wg_t_all[us]).astype(BF16)
            both = _dot(jnp.concatenate([s.astype(BF16), kw.T], axis=0), vaug)
            caug = caugs[h]
            naug = both[:CHUNK] + jnp.concatenate([w_state, w_state], axis=1) * _dot(qh, caug.astype(BF16))
            num, nq = naug[:, :dh], naug[:, dh:]
            hv = num / jnp.maximum(jnp.abs(nq), jnp.exp(-(b_t_all[us] + mx)))
            decay = decay_c[c][h:h + 1, :]
            caugs[h] = jnp.concatenate([decay, decay], axis=1) * caug + both[CHUNK:]
            hc = hv - jnp.mean(hv, axis=1, keepdims=True)
            hn = hc * lax.rsqrt(jnp.mean(hc * hc, axis=1, keepdims=True) + LN_EPS)
            heads.append(hn)
        h_chunks.append(jnp.concatenate(heads, axis=1))
    for h in range(B_HEADS):
        ct_ref[h] = caugs[h]
    hn_all = jnp.concatenate(h_chunks, axis=0) if nck > 1 else h_chunks[0]
    y_b = (hn_all * norm_w_ref[...] + skip_ref[...] * xc) * gate_z

    mix = jnp.concatenate([y_a, y_b], axis=1).astype(BF16)
    y = _dot(mix, w_out_ref[...])
    o_ref[0] = _ln(DN_ALPHA * x + y, ln_g_ref[...], ln_b_ref[...])


def _block_diag_pairs(w):
    hh, d, _ = w.shape
    wp = w.reshape(hh // 2, 2, d, d)
    eye = jnp.eye(2, dtype=w.dtype)
    return jnp.einsum('pade,ab->padbe', wp, eye).reshape(hh // 2, 2 * d, 2 * d)


def _dot_head_pairs(x, w_ref):
    npair, w2, _ = w_ref.shape
    return jnp.concatenate([_dot(x[:, p * w2:(p + 1) * w2], w_ref[p]) for p in range(npair)], axis=1)


def even_mixer_layer(x, w_in, gm_ln_g, gm_ln_b, gm_ws, gm_bs, conv_w, conv_b, wq, wk, wv, w_if,
                     b_if, norm_w, skip, w_out, ln_g, ln_b, *, ts):
    bsz, seq, d = x.shape
    aw = gm_ln_g.shape[0]
    bw = conv_b.shape[0]
    agd = aw // A_GROUPS
    causal = jnp.tril(jnp.ones((CHUNK, CHUNK), dtype=bool))
    gm_w = jnp.where(causal[None], gm_ws, 0.0).astype(BF16)
    gm_bias = jnp.repeat(gm_bs.T, agd, axis=1)
    row = lambda a: a.reshape(1, -1)
    args = (x, w_in.astype(BF16), row(gm_ln_g), row(gm_ln_b), gm_w, gm_bias,
            conv_w, row(conv_b), _block_diag_pairs(wq).astype(BF16), _block_diag_pairs(wk).astype(BF16),
            _block_diag_pairs(wv).astype(BF16), w_if.T.astype(BF16),
            b_if.reshape(-1, 1), row(norm_w), row(skip), w_out.astype(BF16),
            row(ln_g), row(ln_b))
    in_specs = [pl.BlockSpec((1, ts, d), lambda b, j: (b, j, 0))] + [_full(a.shape) for a in args[1:]]
    dh = bw // B_HEADS
    assert dh == LANES and CHUNK == LANES
    return pl.pallas_call(
        functools.partial(_even_kernel, ts=ts, aw=aw, bw=bw),
        grid=(bsz, seq // ts),
        in_specs=in_specs,
        out_specs=pl.BlockSpec((1, ts, d), lambda b, j: (b, j, 0)),
        out_shape=jax.ShapeDtypeStruct((bsz, seq, d), F32),
        scratch_shapes=[pltpu.VMEM((8 + ts, bw), F32),
                        pltpu.VMEM((B_HEADS, dh, dh + LANES), F32),
                        pltpu.VMEM((B_HEADS, LANES), F32)],
        compiler_params=_cparams(("arbitrary", "arbitrary")),
        name="even_mixer",
    )(*args)


def _memfold_kernel(mem_ref, wkv_ref, wq_ref, wo_ref, wqk_ref, vo_ref, *, d):
    dh = d // X_HEADS
    m_len = mem_ref.shape[1]
    kv = _dot(mem_ref[0].astype(BF16), wkv_ref[...])
    k = (kv[:, :d] * (dh ** -0.5 * LOG2E)).astype(BF16)
    v = kv[:, d:].astype(BF16)
    for h in range(X_HEADS):
        hs = slice(h * dh, (h + 1) * dh)
        ms = slice(h * m_len, (h + 1) * m_len)
        wqk_ref[0, :, ms] = _dot_nt(wq_ref[:, hs], k[:, hs]).astype(BF16)
        vo_ref[0, ms, :] = _dot(v[:, hs], wo_ref[hs, :]).astype(BF16)


def memory_fold(mem, wkv, wq, wo):
    bsz, m_len, d = mem.shape
    hm = X_HEADS * m_len
    args = (mem, wkv.astype(BF16), wq.astype(BF16), wo.astype(BF16))
    return pl.pallas_call(
        functools.partial(_memfold_kernel, d=d),
        grid=(bsz,),
        in_specs=[pl.BlockSpec((1, m_len, d), lambda b: (b, 0, 0))] + [_full(a.shape) for a in args[1:]],
        out_specs=[pl.BlockSpec((1, d, hm), lambda b: (b, 0, 0)),
                   pl.BlockSpec((1, hm, d), lambda b: (b, 0, 0))],
        out_shape=[jax.ShapeDtypeStruct((bsz, d, hm), BF16),
                   jax.ShapeDtypeStruct((bsz, hm, d), BF16)],
        compiler_params=_cparams(("arbitrary",)),
        name="memory_fold",
    )(*args)


ROUTE_W = 128
XATTN_TS = 1024
DEST_ROWS = 8


def _xattn_kernel(x_ref, wqk_ref, vo_ref, ln_g_ref, ln_b_ref, wr_ref, br_ref,
                  o_ref, xp_ref, route_ref, cnt_ref):
    m_len = wqk_ref.shape[2] // X_HEADS

    @pl.when(pl.program_id(1) == 0)
    def _():
        cnt_ref[...] = jnp.zeros_like(cnt_ref)

    x = x_ref[0]
    ts = x.shape[0]
    xb = x.astype(BF16)
    probs = []
    for h in range(X_HEADS):
        s = _dot(xb, wqk_ref[0, :, h * m_len:(h + 1) * m_len])
        p = jnp.exp2(s - jnp.max(s, axis=1, keepdims=True))
        probs.append((p / jnp.sum(p, axis=1, keepdims=True)).astype(BF16))
    y = _dot(jnp.concatenate(probs, axis=1), vo_ref[0])
    x2 = _ln(DN_ALPHA * x + y, ln_g_ref[...], ln_b_ref[...])
    o_ref[0] = x2
    xp_ref[0] = _pack_halves(x2)

    x_hi = x2.astype(BF16)
    x_lo = (x2 - x_hi.astype(F32)).astype(BF16)
    logits = (_dot(x_hi, wr_ref[0]) + _dot(x_lo, wr_ref[0]) + _dot(x_hi, wr_ref[1])) + br_ref[...]
    lane = lax.broadcasted_iota(jnp.int32, (ts, ROUTE_W), 1)
    is_g = lane < N_GROUPS
    lg = jnp.where(is_g, logits, NEG)
    mg = jnp.max(lg, axis=1, keepdims=True)
    gi = jnp.min(jnp.where(jnp.logical_and(is_g, lg == mg), lane, ROUTE_W), axis=1, keepdims=True)
    gate_g = 1.0 / jnp.sum(jnp.where(is_g, jnp.exp(lg - mg), 0.0), axis=1, keepdims=True)
    lo = N_GROUPS + gi * EXPERTS_PER_GROUP
    in_grp = jnp.logical_and(lane >= lo, lane < lo + EXPERTS_PER_GROUP)
    le = jnp.where(in_grp, logits, NEG)
    v1 = jnp.max(le, axis=1, keepdims=True)
    i1 = jnp.min(jnp.where(jnp.logical_and(in_grp, le == v1), lane, ROUTE_W), axis=1, keepdims=True)
    le2 = jnp.where(lane == i1, NEG, le)
    v2 = jnp.max(le2, axis=1, keepdims=True)
    i2 = jnp.min(jnp.where(jnp.logical_and(in_grp, le2 == v2), lane, ROUTE_W), axis=1, keepdims=True)
    e21 = jnp.exp(v2 - v1)
    p1 = 1.0 / (1.0 + e21)
    p2 = e21 * p1
    e1 = (i1 - N_GROUPS).astype(F32)
    e2 = (i2 - N_GROUPS).astype(F32)
    rec = jnp.where(lane == 0, e1, 0.0)
    rec = jnp.where(lane == 1, e2, rec)
    rec = jnp.where(lane == 2, gate_g * p1, rec)
    rec = jnp.where(lane == 3, gate_g * p2, rec)
    route_ref[0] = rec
    sel = jnp.logical_or(lane == i1 - N_GROUPS, lane == i2 - N_GROUPS)
    cnt_ref[0] += jnp.sum(jnp.where(sel, 1.0, 0.0), axis=0, keepdims=True)


def xattn_router_layer(x, wqk, vo, ln_g, ln_b, w_rg, b_rg, w_re, b_re, *, ts):
    bsz, seq, d = x.shape
    hm = wqk.shape[2]
    wr = jnp.zeros((d, ROUTE_W), F32).at[:, :N_GROUPS].set(w_rg).at[:, N_GROUPS:N_GROUPS + N_EXPERTS].set(w_re)
    br = jnp.zeros((1, ROUTE_W), F32).at[0, :N_GROUPS].set(b_rg).at[0, N_GROUPS:N_GROUPS + N_EXPERTS].set(b_re)
    wr_hi = wr.astype(BF16)
    wr_lo = (wr - wr_hi.astype(F32)).astype(BF16)
    wr = jnp.stack([wr_hi, wr_lo])
    args = (x, wqk, vo, ln_g.reshape(1, -1), ln_b.reshape(1, -1), wr, br)
    in_specs = [pl.BlockSpec((1, ts, d), lambda b, j: (b, j, 0)),
                pl.BlockSpec((1, d, hm), lambda b, j: (b, 0, 0)),
                pl.BlockSpec((1, hm, d), lambda b, j: (b, 0, 0))] + [_full(a.shape) for a in args[3:]]
    return pl.pallas_call(
        _xattn_kernel,
        grid=(bsz, seq // ts),
        in_specs=in_specs,
        out_specs=[pl.BlockSpec((1, ts, d), lambda b, j: (b, j, 0)),
                   pl.BlockSpec((1, ts, d // 2), lambda b, j: (b, j, 0)),
                   pl.BlockSpec((1, ts, ROUTE_W), lambda b, j: (b, j, 0)),
                   pl.BlockSpec((1, 1, ROUTE_W), lambda b, j: (b, 0, 0))],
        out_shape=[jax.ShapeDtypeStruct((bsz, seq, d), F32),
                   jax.ShapeDtypeStruct((bsz, seq, d // 2), jnp.int32),
                   jax.ShapeDtypeStruct((bsz, seq, ROUTE_W), F32),
                   jax.ShapeDtypeStruct((bsz, 1, ROUTE_W), F32)],
        compiler_params=_cparams(("arbitrary", "arbitrary")),
        name="xattn_router",
    )(*args)


def _moe_output(x2, g0, g1, rec, ln_g, ln_b):
    a_lo, a_hi = _unpack_halves(g0)
    b_lo, b_hi = _unpack_halves(g1)
    w0, w1 = rec[:, 2:3], rec[:, 3:4]
    y = jnp.concatenate([w0 * a_lo + w1 * b_lo, w0 * a_hi + w1 * b_hi], axis=1)
    return _ln(DN_ALPHA * x2 + y, ln_g, ln_b)


def _swa_kernel(x_ref, g0_ref, g1_ref, route_ref, pln_g_ref, pln_b_ref,
                wqkv_ref, bqkv_ref, cos_ref, sin_ref, sink_ref, wo_ref, ln_g_ref, ln_b_ref,
                o_ref, kprev, vprev, *, ts, cq, ckv):
    j = pl.program_id(1)
    nb = ts // CHUNK
    dh = C_HEAD_DIM
    grp = (cq // dh) // C_KV_HEADS

    @pl.when(j == 0)
    def _():
        kprev[...] = jnp.zeros_like(kprev)
        vprev[...] = jnp.zeros_like(vprev)

    x = _moe_output(x_ref[0], g0_ref[0, 0], g1_ref[0, 0], route_ref[0], pln_g_ref[...], pln_b_ref[...])
    qkv = _dot(x.astype(BF16), wqkv_ref[...]) + bqkv_ref[...]
    cos = cos_ref[...]
    sin = sin_ref[...]
    lane = lax.broadcasted_iota(jnp.int32, (ts, LANES), 1)
    first_half = (lane % dh) < (dh // 2)

    def rope(t):
        outs = []
        for c in range(t.shape[1] // LANES):
            tc = t[:, c * LANES:(c + 1) * LANES]
            rot = jnp.where(first_half, pltpu.roll(tc, LANES - dh // 2, 1), pltpu.roll(tc, dh // 2, 1))
            outs.append(tc * cos + rot * sin)
        return jnp.concatenate(outs, axis=1)

    q = rope(qkv[:, :cq]).astype(BF16)
    k = rope(qkv[:, cq:cq + 2 * ckv]).astype(BF16)
    v = qkv[:, cq + 2 * ckv:].astype(BF16)

    r_i = lax.broadcasted_iota(jnp.int32, (CHUNK, 2 * CHUNK), 0)
    c_i = lax.broadcasted_iota(jnp.int32, (CHUNK, 2 * CHUNK), 1)
    band = jnp.logical_and(c_i > r_i, c_i <= r_i + CHUNK)
    sink_col = c_i == 0
    lane_k = lax.broadcasted_iota(jnp.int32, (2 * CHUNK, LANES), 1)
    key_row = lax.broadcasted_iota(jnp.int32, (2 * CHUNK, LANES), 0)
    lane_q = lax.broadcasted_iota(jnp.int32, (CHUNK, LANES), 1)
    ones_blk = jnp.ones((2 * CHUNK, LANES), BF16)
    zero_b = jnp.zeros((), BF16)
    blocks = []
    for c in range(nb):
        sl = slice(c * CHUNK, (c + 1) * CHUNK)
        if c == 0:
            kb = jnp.concatenate([kprev[...].astype(BF16), k[sl]], axis=0)
            vb = jnp.concatenate([vprev[...].astype(BF16), v[sl]], axis=0)
            first_key = jnp.where(j > 0, 0, CHUNK)
            valid = jnp.logical_and(band, c_i >= first_key)
        else:
            kb = k[(c - 1) * CHUNK:(c + 1) * CHUNK]
            vb = v[(c - 1) * CHUNK:(c + 1) * CHUNK]
            valid = band
        tiles = []
        for h in range(C_KV_HEADS):
            kd = kb[:, h * LANES:(h + 1) * LANES]
            vd = vb[:, h * LANES:(h + 1) * LANES]
            k_lo = jnp.where(lane_k < dh, kd, zero_b)
            k_hi = jnp.where(lane_k >= dh, kd, zero_b)
            vz = jnp.where(key_row == 0, zero_b, vd)
            q2 = jnp.concatenate([q[sl, (2 * h) * LANES:(2 * h + 1) * LANES],
                                  q[sl, (2 * h + 1) * LANES:(2 * h + 2) * LANES]], axis=0)
            s_lo = _dot_nt(q2, k_lo)
            s_hi = _dot_nt(q2, k_hi)
            parts = []
            for qk, g in ((s_lo[:CHUNK], 0), (s_lo[CHUNK:], 2), (s_hi[:CHUNK], 1), (s_hi[CHUNK:], 3)):
                sink = sink_ref[h * grp + g]
                parts.append(jnp.where(valid, qk, jnp.where(sink_col, sink, NEG)))
            s = jnp.concatenate(parts, axis=0)
            p = jnp.exp2(s - jnp.max(s, axis=1, keepdims=True)).astype(BF16)
            den = _dot(p, ones_blk)
            o2 = _dot(p, vz) / den
            tiles.append(jnp.where(lane_q < dh, o2[:CHUNK], o2[2 * CHUNK:3 * CHUNK]))
            tiles.append(jnp.where(lane_q < dh, o2[CHUNK:2 * CHUNK], o2[3 * CHUNK:]))
        blocks.append(jnp.concatenate(tiles, axis=1))
    kprev[...] = k[(nb - 1) * CHUNK:].astype(F32)
    vprev[...] = v[(nb - 1) * CHUNK:].astype(F32)
    att = (jnp.concatenate(blocks, axis=0) if nb > 1 else blocks[0]).astype(BF16)
    y = _dot(att, wo_ref[...])
    o_ref[0] = _ln(DN_ALPHA * x + y, ln_g_ref[...], ln_b_ref[...])


def swa_mixer_layer(pending, w_qkv, b_qkv, sinks, w_o, ln_g, ln_b, *, ts):
    x, g, route, pln_g, pln_b = pending
    bsz, seq, d = x.shape
    g = g.reshape(2, bsz, seq, d // 2)
    cq = w_o.shape[0]
    ckv = (w_qkv.shape[1] - cq) // 2
    dh = C_HEAD_DIM
    inv = ROPE_THETA ** (-jnp.arange(0, dh, 2, dtype=F32) / dh)
    ang = jnp.arange(seq, dtype=F32)[:, None] * inv[None, :]
    reps = LANES // (dh // 2)
    sign = jnp.tile(jnp.concatenate([-jnp.ones((dh // 2,), F32), jnp.ones((dh // 2,), F32)]), LANES // dh)
    cos_t = jnp.tile(jnp.cos(ang), (1, reps))
    sin_t = jnp.tile(jnp.sin(ang), (1, reps)) * sign[None, :]
    assert 2 * dh == LANES

    def dup_heads(t):
        th = t.reshape(t.shape[:-1] + (ckv // dh, dh))
        return jnp.concatenate([th, th], axis=-1).reshape(t.shape[:-1] + (2 * ckv,))

    qs = dh ** -0.5 * LOG2E
    w_all = jnp.concatenate([w_qkv[:, :cq] * qs, dup_heads(w_qkv[:, cq:cq + ckv]),
                             dup_heads(w_qkv[:, cq + ckv:])], axis=1)
    b_all = jnp.concatenate([b_qkv[:cq] * qs, dup_heads(b_qkv[cq:cq + ckv]), dup_heads(b_qkv[cq + ckv:])])
    args = (x, g, g, route, pln_g.reshape(1, -1), pln_b.reshape(1, -1),
            w_all.astype(BF16), b_all.reshape(1, -1), cos_t, sin_t, sinks.astype(F32) * LOG2E,
            w_o.astype(BF16), ln_g.reshape(1, -1), ln_b.reshape(1, -1))
    in_specs = [pl.BlockSpec((1, ts, d), lambda b, j: (b, j, 0)),
                pl.BlockSpec((1, 1, ts, d // 2), lambda b, j: (0, b, j, 0)),
                pl.BlockSpec((1, 1, ts, d // 2), lambda b, j: (1, b, j, 0)),
                pl.BlockSpec((1, ts, ROUTE_W), lambda b, j: (b, j, 0)),
                _full((1, d)), _full((1, d)),
                _full(args[6].shape), _full(args[7].shape),
                pl.BlockSpec((ts, LANES), lambda b, j: (j, 0)),
                pl.BlockSpec((ts, LANES), lambda b, j: (j, 0)),
                pl.BlockSpec(memory_space=pltpu.SMEM),
                _full(args[11].shape), _full(args[12].shape), _full(args[13].shape)]
    return pl.pallas_call(
        functools.partial(_swa_kernel, ts=ts, cq=cq, ckv=ckv),
        grid=(bsz, seq // ts),
        in_specs=in_specs,
        out_specs=pl.BlockSpec((1, ts, d), lambda b, j: (b, j, 0)),
        out_shape=jax.ShapeDtypeStruct((bsz, seq, d), F32),
        scratch_shapes=[pltpu.VMEM((CHUNK, 2 * ckv), F32), pltpu.VMEM((CHUNK, 2 * ckv), F32)],
        compiler_params=_cparams(("arbitrary", "arbitrary")),
        name="swa_mixer",
    )(*args)


def _slot_kernel(route_ref, pstart_ref, dest_ref, carry_ref, *, tb):
    @pl.when(pl.program_id(0) == 0)
    def _():
        carry_ref[...] = pstart_ref[...]

    rec = route_ref[...]
    lane = lax.broadcasted_iota(jnp.int32, (tb, ROUTE_W), 1)
    e0 = rec[:, 0:1].astype(jnp.int32)
    e1 = rec[:, 1:2].astype(jnp.int32)
    oh0 = lane == e0
    oh1 = lane == e1
    ohs = jnp.where(jnp.logical_or(oh0, oh1), 1.0, 0.0)
    r = lax.broadcasted_iota(jnp.int32, (tb, tb), 0)
    c = lax.broadcasted_iota(jnp.int32, (tb, tb), 1)
    before = jnp.where(c < r, 1.0, 0.0).astype(BF16)
    prefix = _dot(before, ohs.astype(BF16)) + carry_ref[...]
    d0 = jnp.sum(jnp.where(oh0, prefix, 0.0), axis=1, keepdims=True)
    d1 = jnp.sum(jnp.where(oh1, prefix, 0.0), axis=1, keepdims=True)
    dest = jnp.where(lane == 0, d0, jnp.where(lane == 1, d1, 0.0))
    dest_ref[...] = dest.T[:DEST_ROWS].astype(jnp.int32)
    carry_ref[...] += jnp.sum(ohs, axis=0, keepdims=True)


def moe_slots(route, pstart, *, tb):
    n = route.shape[0]
    return pl.pallas_call(
        functools.partial(_slot_kernel, tb=tb),
        grid=(n // tb,),
        in_specs=[pl.BlockSpec((tb, ROUTE_W), lambda i: (i, 0)), _full((1, ROUTE_W))],
        out_specs=pl.BlockSpec((DEST_ROWS, tb), lambda i: (0, i)),
        out_shape=jax.ShapeDtypeStruct((DEST_ROWS, n), jnp.int32),
        scratch_shapes=[pltpu.VMEM((1, ROUTE_W), F32)],
        compiler_params=_cparams(("arbitrary",)),
        name="moe_slots",
    )(route, pstart)


def _ffn_kernel(blk_exp_ref, new_exp_ref, nblk_ref, xs_ref, w1_ref, w3_ref, w2_ref, ys_ref,
                w1_b, w3_b, w2_b):
    i = pl.program_id(0)
    used = i < nblk_ref[0]

    @pl.when(jnp.logical_and(used, new_exp_ref[i] == 1))
    def _():
        w1_b[...] = w1_ref[0, 0].astype(BF16)
        w3_b[...] = w3_ref[0, 0].astype(BF16)
        w2_b[...] = w2_ref[0, 0].astype(BF16)

    @pl.when(used)
    def _():
        x_lo, x_hi = _unpack_halves(xs_ref[...])
        x_lo, x_hi = x_lo.astype(BF16), x_hi.astype(BF16)
        dl = x_lo.shape[1]
        h1 = _dot(x_lo, w1_b[:dl, :]) + _dot(x_hi, w1_b[dl:, :])
        h3 = _dot(x_lo, w3_b[:dl, :]) + _dot(x_hi, w3_b[dl:, :])
        h = (_silu(h1) * h3).astype(BF16)
        ys_ref[...] = _pack_halves(_dot(h, w2_b[...]))

    @pl.when(jnp.logical_not(used))
    def _():
        ys_ref[...] = jnp.zeros_like(ys_ref)


def moe_ffn(xs, blk_exp, new_exp, nblk, w1, w3, w2, *, layer, bm):
    n_pad, dp = xs.shape
    d = 2 * dp
    de = w1.shape[3]
    n_blk = n_pad // bm

    def x_map(i, be, ne, nb):
        return (jnp.minimum(i, nb[0] - 1), 0)

    def w_map(i, be, ne, nb):
        return (layer, be[i], 0, 0)

    return pl.pallas_call(
        _ffn_kernel,
        grid_spec=pltpu.PrefetchScalarGridSpec(
            num_scalar_prefetch=3,
            grid=(n_blk,),
            in_specs=[pl.BlockSpec((bm, dp), x_map),
                      pl.BlockSpec((1, 1, d, de), w_map),
                      pl.BlockSpec((1, 1, d, de), w_map),
                      pl.BlockSpec((1, 1, de, d), w_map)],
            out_specs=pl.BlockSpec((bm, dp), lambda i, be, ne, nb: (i, 0)),
            scratch_shapes=[pltpu.VMEM((d, de), BF16), pltpu.VMEM((d, de), BF16),
                            pltpu.VMEM((de, d), BF16)]),
        out_shape=jax.ShapeDtypeStruct((n_pad, dp), jnp.int32),
        compiler_params=_cparams(("arbitrary",)),
        name="moe_ffn",
    )(blk_exp, new_exp, nblk, xs, w1, w3, w2)


def _combine_kernel(x_ref, g0_ref, g1_ref, route_ref, ln_g_ref, ln_b_ref, o_ref):
    o_ref[...] = _moe_output(x_ref[...], g0_ref[0], g1_ref[0], route_ref[...], ln_g_ref[...], ln_b_ref[...])


def moe_combine(x, g, route, ln_g, ln_b, *, tb):
    n, d = x.shape
    row_spec = pl.BlockSpec((tb, d), lambda i: (i, 0))
    return pl.pallas_call(
        _combine_kernel,
        grid=(n // tb,),
        in_specs=[row_spec,
                  pl.BlockSpec((1, tb, d // 2), lambda i: (0, i, 0)),
                  pl.BlockSpec((1, tb, d // 2), lambda i: (1, i, 0)),
                  pl.BlockSpec((tb, ROUTE_W), lambda i: (i, 0)),
                  _full((1, d)), _full((1, d))],
        out_specs=row_spec,
        out_shape=jax.ShapeDtypeStruct((n, d), F32),
        compiler_params=_cparams(("arbitrary",)),
        name="moe_combine",
    )(x, g, g, route, ln_g.reshape(1, -1), ln_b.reshape(1, -1))


MOE_BM = 1024
MIXER_TS = 1024


def hierarchical_moe_layer(x2, xp, route, counts, w1, w3, w2, ln_g, ln_b, *, layer, defer_combine):
    bsz, seq, d = x2.shape
    n = bsz * seq
    bm = MOE_BM
    rt = route.reshape(n, ROUTE_W)
    tb = min(512, n)
    n_blk = (2 * n) // bm + N_EXPERTS
    cnt = counts.sum(axis=0)[0, :N_EXPERTS].astype(jnp.int32)
    pcnt = (cnt + bm - 1) // bm * bm
    pends = jnp.cumsum(pcnt)
    pstart = pends - pcnt
    nblk = (pends[-1] // bm).astype(jnp.int32).reshape(1)
    blk_row = jnp.arange(n_blk, dtype=jnp.int32) * bm
    blk_exp = jnp.minimum(jnp.sum((pends[None, :] <= blk_row[:, None]).astype(jnp.int32), axis=1),
                          N_EXPERTS - 1)
    last_exp = blk_exp[jnp.maximum(nblk[0] - 1, 0)]
    blk_exp = jnp.where(jnp.arange(n_blk) < nblk[0], blk_exp, last_exp)
    new_exp = jnp.concatenate([jnp.ones((1,), jnp.int32),
                               (blk_exp[1:] != blk_exp[:-1]).astype(jnp.int32)])
    pstart_rec = jnp.zeros((1, ROUTE_W), F32).at[0, :N_EXPERTS].set(pstart.astype(F32))
    dest = moe_slots(rt, pstart_rec, tb=tb)
    xs = moe_dispatch(xp.reshape(n, d // 2), dest[0], dest[1], n_blk * bm)
    ys = moe_ffn(xs, blk_exp, new_exp, nblk, w1, w3, w2, layer=layer, bm=bm)
    g = moe_gather(ys, dest[0], dest[1])
    if defer_combine:
        return x2, g, route, ln_g, ln_b
    return moe_combine(x2.reshape(n, d), g, rt, ln_g, ln_b, tb=tb).reshape(bsz, seq, d)


SC_ROWS = 128


def _sc_mesh():
    return plsc.VectorSubcoreMesh(core_axis_name="c", subcore_axis_name="s")


def moe_dispatch(xf, dest0, dest1, n_pad):
    n, d = xf.shape
    info = plsc.get_sparse_core_info()
    nw = info.num_cores * info.num_subcores
    per_w = n // nw
    r = min(SC_ROWS, per_w)

    def body(x_hbm, d0_hbm, d1_hbm, xs_hbm, i0_v, i1_v, rows_v, sem):
        wid = lax.axis_index("s") * info.num_cores + lax.axis_index("c")

        @pl.loop(0, per_w // r)
        def _(c):
            base = pl.multiple_of(wid * per_w + c * r, 8)
            pltpu.sync_copy(d0_hbm.at[pl.ds(base, r)], i0_v)
            pltpu.sync_copy(d1_hbm.at[pl.ds(base, r)], i1_v)
            pltpu.sync_copy(x_hbm.at[pl.ds(base, r)], rows_v)
            pltpu.async_copy(rows_v, xs_hbm.at[i0_v], sem).wait()
            pltpu.async_copy(rows_v, xs_hbm.at[i1_v], sem).wait()

    return pl.kernel(
        body, out_type=jax.ShapeDtypeStruct((n_pad, d), xf.dtype), mesh=_sc_mesh(),
        scratch_types=[pltpu.VMEM((r,), jnp.int32), pltpu.VMEM((r,), jnp.int32),
                       pltpu.VMEM((r, d), xf.dtype), pltpu.SemaphoreType.DMA],
        name="moe_dispatch",
    )(xf, dest0, dest1)


def moe_gather(ys, dest0, dest1):
    n = dest0.shape[0]
    d = ys.shape[1]
    info = plsc.get_sparse_core_info()
    nw = info.num_cores * info.num_subcores
    per_w = n // nw
    r = min(SC_ROWS, per_w)

    def body(ys_hbm, d0_hbm, d1_hbm, g_hbm, i_v, rows_v, sem):
        wid = lax.axis_index("s") * info.num_cores + lax.axis_index("c")

        @pl.loop(0, per_w // r)
        def _(c):
            base = pl.multiple_of(wid * per_w + c * r, 8)
            for k, d_hbm in enumerate((d0_hbm, d1_hbm)):
                pltpu.sync_copy(d_hbm.at[pl.ds(base, r)], i_v)
                pltpu.async_copy(ys_hbm.at[i_v], rows_v, sem).wait()
                pltpu.sync_copy(rows_v, g_hbm.at[k, pl.ds(base, r)])

    return pl.kernel(
        body, out_type=jax.ShapeDtypeStruct((2, n, d), ys.dtype), mesh=_sc_mesh(),
        scratch_types=[pltpu.VMEM((r,), jnp.int32), pltpu.VMEM((r, d), ys.dtype),
                       pltpu.SemaphoreType.DMA],
        name="moe_gather",
    )(ys, dest0, dest1)


def kernel(x, mem, ln_g, ln_b, ev_w_in, ev_gm_ln_g, ev_gm_ln_b, ev_gm_ws, ev_gm_bs, ev_conv_w, ev_conv_b, ev_wq, ev_wk, ev_wv, ev_w_if, ev_b_if, ev_norm_w, ev_skip, ev_w_out, od_w_qkv, od_b_qkv, od_sinks, od_w_o, xa_wq, xa_wkv, xa_wo, moe_w_rg, moe_b_rg, moe_w_re, moe_b_re, moe_w1, moe_w3, moe_w2):
    bsz, seq, d = x.shape
    depth = ln_g.shape[0]
    assert depth == DEPTH
    ts = min(MIXER_TS, seq)
    for l in range(depth):
        if l % 2 == 0:
            e = l // 2
            x = even_mixer_layer(x, ev_w_in[e], ev_gm_ln_g[e], ev_gm_ln_b[e], ev_gm_ws[e], ev_gm_bs[e],
                                 ev_conv_w[e], ev_conv_b[e], ev_wq[e], ev_wk[e], ev_wv[e], ev_w_if[e],
                                 ev_b_if[e], ev_norm_w[e], ev_skip[e], ev_w_out[e],
                                 ln_g[l, 0], ln_b[l, 0], ts=ts)
        else:
            o = l // 2
            x = swa_mixer_layer(x, od_w_qkv[o], od_b_qkv[o], od_sinks[o], od_w_o[o],
                                ln_g[l, 0], ln_b[l, 0], ts=ts)
        wqk, vo = memory_fold(mem, xa_wkv[l], xa_wq[l], xa_wo[l])
        x, xp, route, counts = xattn_router_layer(x, wqk, vo, ln_g[l, 1], ln_b[l, 1],
                                                  moe_w_rg[l], moe_b_rg[l], moe_w_re[l], moe_b_re[l],
                                                  ts=min(XATTN_TS, seq))
        x = hierarchical_moe_layer(x, xp, route, counts, moe_w1, moe_w3, moe_w2,
                                   ln_g[l, 2], ln_b[l, 2], layer=l,
                                   defer_combine=(l + 1 < depth and (l + 1) % 2 == 1))
    return x
```

```python
import functools
import math

import jax
import jax.numpy as jnp
from jax import lax
from jax.experimental import pallas as pl
from jax.experimental.pallas import tpu as pltpu
from jax.experimental.pallas import tpu_sc as plsc

F32 = jnp.float32
BF16 = jnp.bfloat16

A_GROUPS = 4
CHUNK = 128
B_HEADS = 4
B_CONV = 4
C_HEAD_DIM = 64
C_KV_HEADS = 4
X_HEADS = 4
N_GROUPS = 4
EXPERTS_PER_GROUP = 8
N_EXPERTS = N_GROUPS * EXPERTS_PER_GROUP
ROPE_THETA = 10000.0
LN_EPS = 1e-5
DEPTH = 2
DN_ALPHA = (2 * DEPTH) ** 0.25

LANES = 128
VMEM_LIMIT = 48 * 1024 * 1024
NEG = -1e30


def _cparams(sem):
    return pltpu.CompilerParams(dimension_semantics=sem, vmem_limit_bytes=VMEM_LIMIT)


def _full(shape):
    nd = len(shape)
    return pl.BlockSpec(shape, lambda *_: (0,) * nd)


def _dot(a, b):
    return jnp.dot(a, b, preferred_element_type=F32)


def _dot_nt(a, b):
    return lax.dot_general(a, b, (((1,), (1,)), ((), ())), preferred_element_type=F32)


def _split_dot(a, b_bf16):
    hi = a.astype(BF16)
    lo = (a - hi.astype(F32)).astype(BF16)
    return _dot(hi, b_bf16) + _dot(lo, b_bf16)


def _ln(x, g, b):
    mu = jnp.mean(x, axis=-1, keepdims=True)
    xc = x - mu
    var = jnp.mean(xc * xc, axis=-1, keepdims=True)
    return xc * lax.rsqrt(var + LN_EPS) * g + b


LOG2E = math.log2(math.e)


def _silu(x):
    return x * (1.0 / (1.0 + jnp.exp2(x * -LOG2E)))


def _gelu(x):
    return 0.5 * x * (1.0 + jnp.tanh(math.sqrt(2.0 / math.pi) * (x + 0.044715 * (x * x * x))))


def _pack_halves(x):
    c = x.shape[1] // 2
    lo = lax.bitcast_convert_type(x[:, :c].astype(BF16).astype(F32), jnp.uint32)
    hi = lax.bitcast_convert_type(x[:, c:].astype(BF16).astype(F32), jnp.uint32)
    return lax.bitcast_convert_type((lo >> 16) | hi, jnp.int32)


def _unpack_halves(p):
    u = lax.bitcast_convert_type(p, jnp.uint32)
    lo = lax.bitcast_convert_type(u << 16, F32)
    hi = lax.bitcast_convert_type(u & jnp.uint32(0xFFFF0000), F32)
    return lo, hi


def _log_sigmoid(x):
    return jnp.minimum(x, 0.0) - jnp.log(1.0 + jnp.exp(-jnp.abs(x)))


def _even_kernel(x_ref, w_in_ref, gm_g_ref, gm_b_ref, gm_w_ref, gm_bias_ref,
                 conv_w_ref, conv_b_ref, wq_ref, wk_ref, wv_ref, wif_t_ref, bif_t_ref,
                 norm_w_ref, skip_ref, w_out_ref, ln_g_ref, ln_b_ref,
                 o_ref,
                 xm_buf, ct_ref, m_ref, *, ts, aw, bw):
    dh = bw // B_HEADS
    agd = aw // A_GROUPS
    nck = ts // CHUNK
    pad = 8
    j = pl.program_id(1)

    @pl.when(j == 0)
    def _():
        xm_buf[0:pad, :] = jnp.zeros((pad, bw), F32)
        ct_ref[...] = jnp.zeros_like(ct_ref)
        m_ref[...] = jnp.zeros_like(m_ref)

    row = lax.broadcasted_iota(jnp.int32, (CHUNK, CHUNK), 0)
    col = lax.broadcasted_iota(jnp.int32, (CHUNK, CHUNK), 1)
    causal = col <= row
    diag = col == row
    triu = jnp.where(row <= col, 1.0, 0.0).astype(BF16)
    ones_blk = jnp.ones((CHUNK, LANES), BF16)

    x = x_ref[0]
    proj = _dot(x.astype(BF16), w_in_ref[...])
    a_u = _gelu(proj[:, :aw])
    a_v = _gelu(proj[:, aw:2 * aw])
    xm = proj[:, 2 * aw:2 * aw + bw]
    z = proj[:, 2 * aw + bw:]

    vn = _ln(a_v, gm_g_ref[...], gm_b_ref[...]).astype(BF16)
    ya_chunks = []
    for c in range(nck):
        cols = []
        for g in range(A_GROUPS):
            v_cg = vn[c * CHUNK:(c + 1) * CHUNK, g * agd:(g + 1) * agd]
            cols.append(_dot(gm_w_ref[g], v_cg))
        ya_chunks.append(jnp.concatenate(cols, axis=1) + gm_bias_ref[...])
    y_a = a_u * jnp.concatenate(ya_chunks, axis=0)

    xm_buf[pad:pad + ts, :] = xm
    conv = conv_b_ref[...] + conv_w_ref[B_CONV - 1:B_CONV, :] * xm
    for k in range(B_CONV - 1):
        sh = B_CONV - 1 - k
        conv = conv + conv_w_ref[k:k + 1, :] * xm_buf[pad - sh:pad - sh + ts, :]
    xm_buf[pad - (B_CONV - 1):pad, :] = xm_buf[pad + ts - (B_CONV - 1):pad + ts, :]
    xc = _silu(conv)
    xc_b = xc.astype(BF16)
    q = _dot_head_pairs(xc_b, wq_ref)
    k_ = _dot_head_pairs(xc_b, wk_ref)
    v = _dot_head_pairs(xm.astype(BF16), wv_ref)
    gate_in = jnp.concatenate([q, k_, v], axis=1).astype(BF16)
    gates_t = _dot_nt(wif_t_ref[...], gate_in) + bif_t_ref[...]
    ig_all = gates_t[:B_HEADS, :]
    lf_all = _log_sigmoid(gates_t[B_HEADS:, :])
    q_b = q.astype(BF16)
    k_b = (k_ * dh ** -0.5).astype(BF16)
    v_b = v.astype(BF16)
    gate_z = _silu(z)

    lf_c = [lf_all[:, c * CHUNK:(c + 1) * CHUNK] for c in range(nck)]
    b_all = _split_dot(jnp.concatenate(lf_c, axis=0), triu)
    m_prev = m_ref[...]
    a_c, m_c, decay_c, wg_c = [], [], [], []
    for c in range(nck):
        b_r = b_all[c * B_HEADS:(c + 1) * B_HEADS]
        a_r = ig_all[:, c * CHUNK:(c + 1) * CHUNK] - b_r
        bl = b_r[:, CHUNK - 1:CHUNK]
        g_r = bl + a_r
        m_new = jnp.maximum(bl + m_prev, jnp.max(g_r, axis=1, keepdims=True))
        a_c.append(a_r)
        m_c.append(m_prev)
        decay_c.append(jnp.exp(bl + m_prev - m_new))
        wg_c.append(jnp.exp(g_r - m_new))
        m_prev = m_new
    m_ref[...] = m_prev

    units = [(c, h) for c in range(nck) for h in range(B_HEADS)]
    lmat = jnp.concatenate([jnp.where(causal, lf_c[c][h:h + 1, :], 0.0) for c, h in units], axis=0)
    dgm = jnp.concatenate([jnp.where(diag, wg_c[c][h:h + 1, :], 0.0) for c, h in units], axis=0)
    b_t_all = _split_dot(lmat, ones_blk)
    wg_t_all = _dot(dgm.astype(BF16), ones_blk)

    caugs = [ct_ref[h] for h in range(B_HEADS)]
    h_chunks = []
    for c in range(nck):
        sl = slice(c * CHUNK, (c + 1) * CHUNK)
        heads = []
        for h in range(B_HEADS):
            u = c * B_HEADS + h
            us = slice(u * CHUNK, (u + 1) * CHUNK)
            hs = slice(h * dh, (h + 1) * dh)
            qh, kh = q_b[sl, hs], k_b[sl, hs]
            vaug = jnp.concatenate([v_b[sl, hs], ones_blk], axis=1)
            m_row = m_c[c][h:h + 1, :]
            amat = jnp.where(causal, a_c[c][h:h + 1, :], NEG)
            mx = jnp.maximum(jnp.max(amat, axis=1, keepdims=True), m_row)
            w_intra = jnp.exp(amat - mx)
            w_state = jnp.exp(m_row - mx)
            s = _dot_nt(qh, kh) * w_intra
            kw = (kh.astype(F32) * wg_t_all[us]).astype(BF16)
            both = _dot(jnp.concatenate([s.astype(BF16), kw.T], axis=0), vaug)
            caug = caugs[h]
            naug = both[:CHUNK] + jnp.concatenate([w_state, w_state], axis=1) * _dot(qh, caug.astype(BF16))
            num, nq = naug[:, :dh], naug[:, dh:]
            hv = num / jnp.maximum(jnp.abs(nq), jnp.exp(-(b_t_all[us] + mx)))
            decay = decay_c[c][h:h + 1, :]
            caugs[h] = jnp.concatenate([decay, decay], axis=1) * caug + both[CHUNK:]
            hc = hv - jnp.mean(hv, axis=1, keepdims=True)
            hn = hc * lax.rsqrt(jnp.mean(hc * hc, axis=1, keepdims=True) + LN_EPS)
            heads.append(hn)
        h_chunks.append(jnp.concatenate(heads, axis=1))
    for h in range(B_HEADS):
        ct_ref[h] = caugs[h]
    hn_all = jnp.concatenate(h_chunks, axis=0) if nck > 1 else h_chunks[0]
    y_b = (hn_all * norm_w_ref[...] + skip_ref[...] * xc) * gate_z

    mix = jnp.concatenate([y_a, y_b], axis=1).astype(BF16)
    y = _dot(mix, w_out_ref[...])
    o_ref[0] = _ln(DN_ALPHA * x + y, ln_g_ref[...], ln_b_ref[...])


def _block_diag_pairs(w):
    hh, d, _ = w.shape
    wp = w.reshape(hh // 2, 2, d, d)
    eye = jnp.eye(2, dtype=w.dtype)
    return jnp.einsum('pade,ab->padbe', wp, eye).reshape(hh // 2, 2 * d, 2 * d)


def _dot_head_pairs(x, w_ref):
    npair, w2, _ = w_ref.shape
    return jnp.concatenate([_dot(x[:, p * w2:(p + 1) * w2], w_ref[p]) for p in range(npair)], axis=1)


def even_mixer_layer(x, w_in, gm_ln_g, gm_ln_b, gm_ws, gm_bs, conv_w, conv_b, wq, wk, wv, w_if,
                     b_if, norm_w, skip, w_out, ln_g, ln_b, *, ts):
    bsz, seq, d = x.shape
    aw = gm_ln_g.shape[0]
    bw = conv_b.shape[0]
    agd = aw // A_GROUPS
    causal = jnp.tril(jnp.ones((CHUNK, CHUNK), dtype=bool))
    gm_w = jnp.where(causal[None], gm_ws, 0.0).astype(BF16)
    gm_bias = jnp.repeat(gm_bs.T, agd, axis=1)
    row = lambda a: a.reshape(1, -1)
    args = (x, w_in.astype(BF16), row(gm_ln_g), row(gm_ln_b), gm_w, gm_bias,
            conv_w, row(conv_b), _block_diag_pairs(wq).astype(BF16), _block_diag_pairs(wk).astype(BF16),
            _block_diag_pairs(wv).astype(BF16), w_if.T.astype(BF16),
            b_if.reshape(-1, 1), row(norm_w), row(skip), w_out.astype(BF16),
            row(ln_g), row(ln_b))
    in_specs = [pl.BlockSpec((1, ts, d), lambda b, j: (b, j, 0))] + [_full(a.shape) for a in args[1:]]
    dh = bw // B_HEADS
    assert dh == LANES and CHUNK == LANES
    return pl.pallas_call(
        functools.partial(_even_kernel, ts=ts, aw=aw, bw=bw),
        grid=(bsz, seq // ts),
        in_specs=in_specs,
        out_specs=pl.BlockSpec((1, ts, d), lambda b, j: (b, j, 0)),
        out_shape=jax.ShapeDtypeStruct((bsz, seq, d), F32),
        scratch_shapes=[pltpu.VMEM((8 + ts, bw), F32),
                        pltpu.VMEM((B_HEADS, dh, dh + LANES), F32),
                        pltpu.VMEM((B_HEADS, LANES), F32)],
        compiler_params=_cparams(("arbitrary", "arbitrary")),
        name="even_mixer",
    )(*args)


def _memfold_kernel(mem_ref, wkv_ref, wq_ref, wo_ref, wqk_ref, vo_ref, *, d):
    dh = d // X_HEADS
    m_len = mem_ref.shape[1]
    kv = _dot(mem_ref[0].astype(BF16), wkv_ref[...])
    k = (kv[:, :d] * (dh ** -0.5 * LOG2E)).astype(BF16)
    v = kv[:, d:].astype(BF16)
    for h in range(X_HEADS):
        hs = slice(h * dh, (h + 1) * dh)
        ms = slice(h * m_len, (h + 1) * m_len)
        wqk_ref[0, :, ms] = _dot_nt(wq_ref[:, hs], k[:, hs]).astype(BF16)
        vo_ref[0, ms, :] = _dot(v[:, hs], wo_ref[hs, :]).astype(BF16)


def memory_fold(mem, wkv, wq, wo):
    bsz, m_len, d = mem.shape
    hm = X_HEADS * m_len
    args = (mem, wkv.astype(BF16), wq.astype(BF16), wo.astype(BF16))
    return pl.pallas_call(
        functools.partial(_memfold_kernel, d=d),
        grid=(bsz,),
        in_specs=[pl.BlockSpec((1, m_len, d), lambda b: (b, 0, 0))] + [_full(a.shape) for a in args[1:]],
        out_specs=[pl.BlockSpec((1, d, hm), lambda b: (b, 0, 0)),
                   pl.BlockSpec((1, hm, d), lambda b: (b, 0, 0))],
        out_shape=[jax.ShapeDtypeStruct((bsz, d, hm), BF16),
                   jax.ShapeDtypeStruct((bsz, hm, d), BF16)],
        compiler_params=_cparams(("arbitrary",)),
        name="memory_fold",
    )(*args)


ROUTE_W = 128
XATTN_TS = 1024
DEST_ROWS = 8


def _xattn_kernel(x_ref, wqk_ref, vo_ref, ln_g_ref, ln_b_ref, wr_ref, br_ref,
                  o_ref, xp_ref, route_ref, cnt_ref):
    m_len = wqk_ref.shape[2] // X_HEADS

    @pl.when(pl.program_id(1) == 0)
    def _():
        cnt_ref[...] = jnp.zeros_like(cnt_ref)

    x = x_ref[0]
    ts = x.shape[0]
    xb = x.astype(BF16)
    probs = []
    for h in range(X_HEADS):
        s = _dot(xb, wqk_ref[0, :, h * m_len:(h + 1) * m_len])
        p = jnp.exp2(s - jnp.max(s, axis=1, keepdims=True))
        probs.append((p / jnp.sum(p, axis=1, keepdims=True)).astype(BF16))
    y = _dot(jnp.concatenate(probs, axis=1), vo_ref[0])
    x2 = _ln(DN_ALPHA * x + y, ln_g_ref[...], ln_b_ref[...])
    o_ref[0] = x2
    xp_ref[0] = _pack_halves(x2)

    x_hi = x2.astype(BF16)
    x_lo = (x2 - x_hi.astype(F32)).astype(BF16)
    logits = (_dot(x_hi, wr_ref[0]) + _dot(x_lo, wr_ref[0]) + _dot(x_hi, wr_ref[1])) + br_ref[...]
    lane = lax.broadcasted_iota(jnp.int32, (ts, ROUTE_W), 1)
    is_g = lane < N_GROUPS
    lg = jnp.where(is_g, logits, NEG)
    mg = jnp.max(lg, axis=1, keepdims=True)
    gi = jnp.min(jnp.where(jnp.logical_and(is_g, lg == mg), lane, ROUTE_W), axis=1, keepdims=True)
    gate_g = 1.0 / jnp.sum(jnp.where(is_g, jnp.exp(lg - mg), 0.0), axis=1, keepdims=True)
    lo = N_GROUPS + gi * EXPERTS_PER_GROUP
    in_grp = jnp.logical_and(lane >= lo, lane < lo + EXPERTS_PER_GROUP)
    le = jnp.where(in_grp, logits, NEG)
    v1 = jnp.max(le, axis=1, keepdims=True)
    i1 = jnp.min(jnp.where(jnp.logical_and(in_grp, le == v1), lane, ROUTE_W), axis=1, keepdims=True)
    le2 = jnp.where(lane == i1, NEG, le)
    v2 = jnp.max(le2, axis=1, keepdims=True)
    i2 = jnp.min(jnp.where(jnp.logical_and(in_grp, le2 == v2), lane, ROUTE_W), axis=1, keepdims=True)
    e21 = jnp.exp(v2 - v1)
    p1 = 1.0 / (1.0 + e21)
    p2 = e21 * p1
    e1 = (i1 - N_GROUPS).astype(F32)
    e2 = (i2 - N_GROUPS).astype(F32)
    rec = jnp.where(lane == 0, e1, 0.0)
    rec = jnp.where(lane == 1, e2, rec)
    rec = jnp.where(lane == 2, gate_g * p1, rec)
    rec = jnp.where(lane == 3, gate_g * p2, rec)
    route_ref[0] = rec
    sel = jnp.logical_or(lane == i1 - N_GROUPS, lane == i2 - N_GROUPS)
    cnt_ref[0] += jnp.sum(jnp.where(sel, 1.0, 0.0), axis=0, keepdims=True)


def xattn_router_layer(x, wqk, vo, ln_g, ln_b, w_rg, b_rg, w_re, b_re, *, ts):
    bsz, seq, d = x.shape
    hm = wqk.shape[2]
    wr = jnp.zeros((d, ROUTE_W), F32).at[:, :N_GROUPS].set(w_rg).at[:, N_GROUPS:N_GROUPS + N_EXPERTS].set(w_re)
    br = jnp.zeros((1, ROUTE_W), F32).at[0, :N_GROUPS].set(b_rg).at[0, N_GROUPS:N_GROUPS + N_EXPERTS].set(b_re)
    wr_hi = wr.astype(BF16)
    wr_lo = (wr - wr_hi.astype(F32)).astype(BF16)
    wr = jnp.stack([wr_hi, wr_lo])
    args = (x, wqk, vo, ln_g.reshape(1, -1), ln_b.reshape(1, -1), wr, br)
    in_specs = [pl.BlockSpec((1, ts, d), lambda b, j: (b, j, 0)),
                pl.BlockSpec((1, d, hm), lambda b, j: (b, 0, 0)),
                pl.BlockSpec((1, hm, d), lambda b, j: (b, 0, 0))] + [_full(a.shape) for a in args[3:]]
    return pl.pallas_call(
        _xattn_kernel,
        grid=(bsz, seq // ts),
        in_specs=in_specs,
        out_specs=[pl.BlockSpec((1, ts, d), lambda b, j: (b, j, 0)),
                   pl.BlockSpec((1, ts, d // 2), lambda b, j: (b, j, 0)),
                   pl.BlockSpec((1, ts, ROUTE_W), lambda b, j: (b, j, 0)),
                   pl.BlockSpec((1, 1, ROUTE_W), lambda b, j: (b, 0, 0))],
        out_shape=[jax.ShapeDtypeStruct((bsz, seq, d), F32),
                   jax.ShapeDtypeStruct((bsz, seq, d // 2), jnp.int32),
                   jax.ShapeDtypeStruct((bsz, seq, ROUTE_W), F32),
                   jax.ShapeDtypeStruct((bsz, 1, ROUTE_W), F32)],
        compiler_params=_cparams(("arbitrary", "arbitrary")),
        name="xattn_router",
    )(*args)


def _moe_output(x2, g0, g1, rec, ln_g, ln_b):
    a_lo, a_hi = _unpack_halves(g0)
    b_lo, b_hi = _unpack_halves(g1)
    w0, w1 = rec[:, 2:3], rec[:, 3:4]
    y = jnp.concatenate([w0 * a_lo + w1 * b_lo, w0 * a_hi + w1 * b_hi], axis=1)
    return _ln(DN_ALPHA * x2 + y, ln_g, ln_b)


def _swa_kernel(x_ref, g0_ref, g1_ref, route_ref, pln_g_ref, pln_b_ref,
                wqkv_ref, bqkv_ref, cos_ref, sin_ref, sink_ref, wo_ref, ln_g_ref, ln_b_ref,
                o_ref, kprev, vprev, *, ts, cq, ckv):
    j = pl.program_id(1)
    nb = ts // CHUNK
    dh = C_HEAD_DIM
    grp = (cq // dh) // C_KV_HEADS

    @pl.when(j == 0)
    def _():
        kprev[...] = jnp.zeros_like(kprev)
        vprev[...] = jnp.zeros_like(vprev)

    x = _moe_output(x_ref[0], g0_ref[0, 0], g1_ref[0, 0], route_ref[0], pln_g_ref[...], pln_b_ref[...])
    qkv = _dot(x.astype(BF16), wqkv_ref[...]) + bqkv_ref[...]
    cos = cos_ref[...]
    sin = sin_ref[...]
    def rope(t):
        outs = []
        for c in range(t.shape[1] // LANES):
            tc = t[:, c * LANES:(c + 1) * LANES]
            outs.append(tc * cos + pltpu.roll(tc, LANES // 2, 1) * sin)
        return jnp.concatenate(outs, axis=1)

    q = rope(qkv[:, :cq]).astype(BF16)
    k = rope(qkv[:, cq:cq + 2 * ckv]).astype(BF16)
    v = qkv[:, cq + 2 * ckv:].astype(BF16)

    r_i = lax.broadcasted_iota(jnp.int32, (CHUNK, 2 * CHUNK), 0)
    c_i = lax.broadcasted_iota(jnp.int32, (CHUNK, 2 * CHUNK), 1)
    band = jnp.logical_and(c_i > r_i, c_i <= r_i + CHUNK)
    sink_col = c_i == 0
    head_a = (lax.broadcasted_iota(jnp.int32, (2 * CHUNK, LANES), 1) % dh) < (dh // 2)
    key_row = lax.broadcasted_iota(jnp.int32, (2 * CHUNK, LANES), 0)
    lane_q = lax.broadcasted_iota(jnp.int32, (CHUNK, LANES), 1)
    ones_blk = jnp.ones((2 * CHUNK, LANES), BF16)
    zero_b = jnp.zeros((), BF16)
    blocks = []
    for c in range(nb):
        sl = slice(c * CHUNK, (c + 1) * CHUNK)
        if c == 0:
            kb = jnp.concatenate([kprev[...].astype(BF16), k[sl]], axis=0)
            vb = jnp.concatenate([vprev[...].astype(BF16), v[sl]], axis=0)
            first_key = jnp.where(j > 0, 0, CHUNK)
            valid = jnp.logical_and(band, c_i >= first_key)
        else:
            kb = k[(c - 1) * CHUNK:(c + 1) * CHUNK]
            vb = v[(c - 1) * CHUNK:(c + 1) * CHUNK]
            valid = band
        tiles = []
        for h in range(C_KV_HEADS):
            kd = kb[:, h * LANES:(h + 1) * LANES]
            vd = vb[:, h * LANES:(h + 1) * LANES]
            k_lo = jnp.where(head_a, kd, zero_b)
            k_hi = jnp.where(head_a, zero_b, kd)
            vz = jnp.where(key_row == 0, zero_b, vd)
            q2 = jnp.concatenate([q[sl, (2 * h) * LANES:(2 * h + 1) * LANES],
                                  q[sl, (2 * h + 1) * LANES:(2 * h + 2) * LANES]], axis=0)
            s_lo = _dot_nt(q2, k_lo)
            s_hi = _dot_nt(q2, k_hi)
            parts = []
            for qk, g in ((s_lo[:CHUNK], 0), (s_lo[CHUNK:], 2), (s_hi[:CHUNK], 1), (s_hi[CHUNK:], 3)):
                sink = sink_ref[h * grp + g]
                parts.append(jnp.where(valid, qk, jnp.where(sink_col, sink, NEG)))
            s = jnp.concatenate(parts, axis=0)
            p = jnp.exp2(s - jnp.max(s, axis=1, keepdims=True)).astype(BF16)
            den = _dot(p, ones_blk)
            o2 = _dot(p, vz) / den
            tiles.append(jnp.where(lane_q < dh, o2[:CHUNK], o2[2 * CHUNK:3 * CHUNK]))
            tiles.append(jnp.where(lane_q < dh, o2[CHUNK:2 * CHUNK], o2[3 * CHUNK:]))
        blocks.append(jnp.concatenate(tiles, axis=1))
    kprev[...] = k[(nb - 1) * CHUNK:].astype(F32)
    vprev[...] = v[(nb - 1) * CHUNK:].astype(F32)
    att = (jnp.concatenate(blocks, axis=0) if nb > 1 else blocks[0]).astype(BF16)
    y = _dot(att, wo_ref[...])
    o_ref[0] = _ln(DN_ALPHA * x + y, ln_g_ref[...], ln_b_ref[...])


def swa_mixer_layer(pending, w_qkv, b_qkv, sinks, w_o, ln_g, ln_b, *, ts):
    x, g, route, pln_g, pln_b = pending
    bsz, seq, d = x.shape
    g = g.reshape(2, bsz, seq, d // 2)
    cq = w_o.shape[0]
    ckv = (w_qkv.shape[1] - cq) // 2
    dh = C_HEAD_DIM
    inv = ROPE_THETA ** (-jnp.arange(0, dh, 2, dtype=F32) / dh)
    ang = jnp.arange(seq, dtype=F32)[:, None] * inv[None, :]
    reps = LANES // (dh // 2)
    assert 2 * dh == LANES
    sign = jnp.concatenate([-jnp.ones((LANES // 2,), F32), jnp.ones((LANES // 2,), F32)])
    cos_t = jnp.tile(jnp.cos(ang), (1, reps))
    sin_t = jnp.tile(jnp.sin(ang), (1, reps)) * sign[None, :]

    def pair_halves(t):
        th = t.reshape(t.shape[:-1] + (-1, 2, 2, dh // 2))
        return jnp.swapaxes(th, -3, -2).reshape(t.shape)

    def dup_heads(t, halves):
        th = t.reshape(t.shape[:-1] + (ckv // dh, 2, dh // 2)) if halves else t.reshape(t.shape[:-1] + (ckv // dh, 1, dh))
        return jnp.concatenate([th, th], axis=-1).reshape(t.shape[:-1] + (2 * ckv,))

    qs = dh ** -0.5 * LOG2E
    w_all = jnp.concatenate([pair_halves(w_qkv[:, :cq] * qs), dup_heads(w_qkv[:, cq:cq + ckv], True),
                             dup_heads(w_qkv[:, cq + ckv:], False)], axis=1)
    b_all = jnp.concatenate([pair_halves(b_qkv[:cq] * qs), dup_heads(b_qkv[cq:cq + ckv], True),
                             dup_heads(b_qkv[cq + ckv:], False)])
    args = (x, g, g, route, pln_g.reshape(1, -1), pln_b.reshape(1, -1),
            w_all.astype(BF16), b_all.reshape(1, -1), cos_t, sin_t, sinks.astype(F32) * LOG2E,
            w_o.astype(BF16), ln_g.reshape(1, -1), ln_b.reshape(1, -1))
    in_specs = [pl.BlockSpec((1, ts, d), lambda b, j: (b, j, 0)),
                pl.BlockSpec((1, 1, ts, d // 2), lambda b, j: (0, b, j, 0)),
                pl.BlockSpec((1, 1, ts, d // 2), lambda b, j: (1, b, j, 0)),
                pl.BlockSpec((1, ts, ROUTE_W), lambda b, j: (b, j, 0)),
                _full((1, d)), _full((1, d)),
                _full(args[6].shape), _full(args[7].shape),
                pl.BlockSpec((ts, LANES), lambda b, j: (j, 0)),
                pl.BlockSpec((ts, LANES), lambda b, j: (j, 0)),
                pl.BlockSpec(memory_space=pltpu.SMEM),
                _full(args[11].shape), _full(args[12].shape), _full(args[13].shape)]
    return pl.pallas_call(
        functools.partial(_swa_kernel, ts=ts, cq=cq, ckv=ckv),
        grid=(bsz, seq // ts),
        in_specs=in_specs,
        out_specs=pl.BlockSpec((1, ts, d), lambda b, j: (b, j, 0)),
        out_shape=jax.ShapeDtypeStruct((bsz, seq, d), F32),
        scratch_shapes=[pltpu.VMEM((CHUNK, 2 * ckv), F32), pltpu.VMEM((CHUNK, 2 * ckv), F32)],
        compiler_params=_cparams(("arbitrary", "arbitrary")),
        name="swa_mixer",
    )(*args)


def _slot_kernel(route_ref, pstart_ref, dest_ref, carry_ref, *, tb):
    @pl.when(pl.program_id(0) == 0)
    def _():
        carry_ref[...] = pstart_ref[...]

    rec = route_ref[...]
    lane = lax.broadcasted_iota(jnp.int32, (tb, ROUTE_W), 1)
    e0 = rec[:, 0:1].astype(jnp.int32)
    e1 = rec[:, 1:2].astype(jnp.int32)
    oh0 = lane == e0
    oh1 = lane == e1
    ohs = jnp.where(jnp.logical_or(oh0, oh1), 1.0, 0.0)
    r = lax.broadcasted_iota(jnp.int32, (tb, tb), 0)
    c = lax.broadcasted_iota(jnp.int32, (tb, tb), 1)
    before = jnp.where(c < r, 1.0, 0.0).astype(BF16)
    prefix = _dot(before, ohs.astype(BF16)) + carry_ref[...]
    d0 = jnp.sum(jnp.where(oh0, prefix, 0.0), axis=1, keepdims=True)
    d1 = jnp.sum(jnp.where(oh1, prefix, 0.0), axis=1, keepdims=True)
    dest = jnp.where(lane == 0, d0, jnp.where(lane == 1, d1, 0.0))
    dest_ref[...] = dest.T[:DEST_ROWS].astype(jnp.int32)
    carry_ref[...] += jnp.sum(ohs, axis=0, keepdims=True)


def moe_slots(route, pstart, *, tb):
    n = route.shape[0]
    return pl.pallas_call(
        functools.partial(_slot_kernel, tb=tb),
        grid=(n // tb,),
        in_specs=[pl.BlockSpec((tb, ROUTE_W), lambda i: (i, 0)), _full((1, ROUTE_W))],
        out_specs=pl.BlockSpec((DEST_ROWS, tb), lambda i: (0, i)),
        out_shape=jax.ShapeDtypeStruct((DEST_ROWS, n), jnp.int32),
        scratch_shapes=[pltpu.VMEM((1, ROUTE_W), F32)],
        compiler_params=_cparams(("arbitrary",)),
        name="moe_slots",
    )(route, pstart)


def _ffn_kernel(blk_exp_ref, new_exp_ref, nblk_ref, xs_ref, w1_ref, w3_ref, w2_ref, ys_ref,
                w1_b, w3_b, w2_b):
    i = pl.program_id(0)
    used = i < nblk_ref[0]

    @pl.when(jnp.logical_and(used, new_exp_ref[i] == 1))
    def _():
        w1_b[...] = w1_ref[0, 0].astype(BF16)
        w3_b[...] = w3_ref[0, 0].astype(BF16)
        w2_b[...] = w2_ref[0, 0].astype(BF16)

    @pl.when(used)
    def _():
        x_lo, x_hi = _unpack_halves(xs_ref[...])
        x = jnp.concatenate([x_lo.astype(BF16), x_hi.astype(BF16)], axis=1)
        h1 = _dot(x, w1_b[...])
        h3 = _dot(x, w3_b[...])
        h = (_silu(h1) * h3).astype(BF16)
        ys_ref[...] = _pack_halves(_dot(h, w2_b[...]))

    @pl.when(jnp.logical_not(used))
    def _():
        ys_ref[...] = jnp.zeros_like(ys_ref)


def moe_ffn(xs, blk_exp, new_exp, nblk, w1, w3, w2, *, layer, bm):
    n_pad, dp = xs.shape
    d = 2 * dp
    de = w1.shape[3]
    n_blk = n_pad // bm

    def x_map(i, be, ne, nb):
        return (jnp.minimum(i, nb[0] - 1), 0)

    def w_map(i, be, ne, nb):
        return (layer, be[i], 0, 0)

    return pl.pallas_call(
        _ffn_kernel,
        grid_spec=pltpu.PrefetchScalarGridSpec(
            num_scalar_prefetch=3,
            grid=(n_blk,),
            in_specs=[pl.BlockSpec((bm, dp), x_map),
                      pl.BlockSpec((1, 1, d, de), w_map),
                      pl.BlockSpec((1, 1, d, de), w_map),
                      pl.BlockSpec((1, 1, de, d), w_map)],
            out_specs=pl.BlockSpec((bm, dp), lambda i, be, ne, nb: (i, 0)),
            scratch_shapes=[pltpu.VMEM((d, de), BF16), pltpu.VMEM((d, de), BF16),
                            pltpu.VMEM((de, d), BF16)]),
        out_shape=jax.ShapeDtypeStruct((n_pad, dp), jnp.int32),
        compiler_params=_cparams(("arbitrary",)),
        name="moe_ffn",
    )(blk_exp, new_exp, nblk, xs, w1, w3, w2)


def _combine_kernel(x_ref, g0_ref, g1_ref, route_ref, ln_g_ref, ln_b_ref, o_ref):
    o_ref[...] = _moe_output(x_ref[...], g0_ref[0], g1_ref[0], route_ref[...], ln_g_ref[...], ln_b_ref[...])


def moe_combine(x, g, route, ln_g, ln_b, *, tb):
    n, d = x.shape
    row_spec = pl.BlockSpec((tb, d), lambda i: (i, 0))
    return pl.pallas_call(
        _combine_kernel,
        grid=(n // tb,),
        in_specs=[row_spec,
                  pl.BlockSpec((1, tb, d // 2), lambda i: (0, i, 0)),
                  pl.BlockSpec((1, tb, d // 2), lambda i: (1, i, 0)),
                  pl.BlockSpec((tb, ROUTE_W), lambda i: (i, 0)),
                  _full((1, d)), _full((1, d))],
        out_specs=row_spec,
        out_shape=jax.ShapeDtypeStruct((n, d), F32),
        compiler_params=_cparams(("arbitrary",)),
        name="moe_combine",
    )(x, g, g, route, ln_g.reshape(1, -1), ln_b.reshape(1, -1))


MOE_BM = 1024
MIXER_TS = 1024


def hierarchical_moe_layer(x2, xp, route, counts, w1, w3, w2, ln_g, ln_b, *, layer, defer_combine):
    bsz, seq, d = x2.shape
    n = bsz * seq
    bm = MOE_BM
    rt = route.reshape(n, ROUTE_W)
    tb = min(512, n)
    n_blk = (2 * n) // bm + N_EXPERTS
    cnt = counts.sum(axis=0)[0, :N_EXPERTS].astype(jnp.int32)
    pcnt = (cnt + bm - 1) // bm * bm
    pends = jnp.cumsum(pcnt)
    pstart = pends - pcnt
    nblk = (pends[-1] // bm).astype(jnp.int32).reshape(1)
    blk_row = jnp.arange(n_blk, dtype=jnp.int32) * bm
    blk_exp = jnp.minimum(jnp.sum((pends[None, :] <= blk_row[:, None]).astype(jnp.int32), axis=1),
                          N_EXPERTS - 1)
    last_exp = blk_exp[jnp.maximum(nblk[0] - 1, 0)]
    blk_exp = jnp.where(jnp.arange(n_blk) < nblk[0], blk_exp, last_exp)
    new_exp = jnp.concatenate([jnp.ones((1,), jnp.int32),
                               (blk_exp[1:] != blk_exp[:-1]).astype(jnp.int32)])
    pstart_rec = jnp.zeros((1, ROUTE_W), F32).at[0, :N_EXPERTS].set(pstart.astype(F32))
    dest = moe_slots(rt, pstart_rec, tb=tb)
    xs = moe_dispatch(xp.reshape(n, d // 2), dest[0], dest[1], n_blk * bm)
    ys = moe_ffn(xs, blk_exp, new_exp, nblk, w1, w3, w2, layer=layer, bm=bm)
    g = moe_gather(ys, dest[0], dest[1])
    if defer_combine:
        return x2, g, route, ln_g, ln_b
    return moe_combine(x2.reshape(n, d), g, rt, ln_g, ln_b, tb=tb).reshape(bsz, seq, d)


SC_ROWS = 128


def _sc_mesh():
    return plsc.VectorSubcoreMesh(core_axis_name="c", subcore_axis_name="s")


def moe_dispatch(xf, dest0, dest1, n_pad):
    n, d = xf.shape
    info = plsc.get_sparse_core_info()
    nw = info.num_cores * info.num_subcores
    per_w = n // nw
    r = min(SC_ROWS, per_w)

    def body(x_hbm, d0_hbm, d1_hbm, xs_hbm, i0_v, i1_v, rows_v, sem):
        wid = lax.axis_index("s") * info.num_cores + lax.axis_index("c")

        @pl.loop(0, per_w // r)
        def _(c):
            base = pl.multiple_of(wid * per_w + c * r, 8)
            pltpu.sync_copy(d0_hbm.at[pl.ds(base, r)], i0_v)
            pltpu.sync_copy(d1_hbm.at[pl.ds(base, r)], i1_v)
            pltpu.sync_copy(x_hbm.at[pl.ds(base, r)], rows_v)
            pltpu.async_copy(rows_v, xs_hbm.at[i0_v], sem).wait()
            pltpu.async_copy(rows_v, xs_hbm.at[i1_v], sem).wait()

    return pl.kernel(
        body, out_type=jax.ShapeDtypeStruct((n_pad, d), xf.dtype), mesh=_sc_mesh(),
        scratch_types=[pltpu.VMEM((r,), jnp.int32), pltpu.VMEM((r,), jnp.int32),
                       pltpu.VMEM((r, d), xf.dtype), pltpu.SemaphoreType.DMA],
        name="moe_dispatch",
    )(xf, dest0, dest1)


def moe_gather(ys, dest0, dest1):
    n = dest0.shape[0]
    d = ys.shape[1]
    info = plsc.get_sparse_core_info()
    nw = info.num_cores * info.num_subcores
    per_w = n // nw
    r = min(SC_ROWS, per_w)

    def body(ys_hbm, d0_hbm, d1_hbm, g_hbm, i_v, rows_v, sem):
        wid = lax.axis_index("s") * info.num_cores + lax.axis_index("c")

        @pl.loop(0, per_w // r)
        def _(c):
            base = pl.multiple_of(wid * per_w + c * r, 8)
            for k, d_hbm in enumerate((d0_hbm, d1_hbm)):
                pltpu.sync_copy(d_hbm.at[pl.ds(base, r)], i_v)
                pltpu.async_copy(ys_hbm.at[i_v], rows_v, sem).wait()
                pltpu.sync_copy(rows_v, g_hbm.at[k, pl.ds(base, r)])

    return pl.kernel(
        body, out_type=jax.ShapeDtypeStruct((2, n, d), ys.dtype), mesh=_sc_mesh(),
        scratch_types=[pltpu.VMEM((r,), jnp.int32), pltpu.VMEM((r, d), ys.dtype),
                       pltpu.SemaphoreType.DMA],
        name="moe_gather",
    )(ys, dest0, dest1)


def kernel(x, mem, ln_g, ln_b, ev_w_in, ev_gm_ln_g, ev_gm_ln_b, ev_gm_ws, ev_gm_bs, ev_conv_w, ev_conv_b, ev_wq, ev_wk, ev_wv, ev_w_if, ev_b_if, ev_norm_w, ev_skip, ev_w_out, od_w_qkv, od_b_qkv, od_sinks, od_w_o, xa_wq, xa_wkv, xa_wo, moe_w_rg, moe_b_rg, moe_w_re, moe_b_re, moe_w1, moe_w3, moe_w2):
    bsz, seq, d = x.shape
    depth = ln_g.shape[0]
    assert depth == DEPTH
    ts = min(MIXER_TS, seq)
    for l in range(depth):
        if l % 2 == 0:
            e = l // 2
            x = even_mixer_layer(x, ev_w_in[e], ev_gm_ln_g[e], ev_gm_ln_b[e], ev_gm_ws[e], ev_gm_bs[e],
                                 ev_conv_w[e], ev_conv_b[e], ev_wq[e], ev_wk[e], ev_wv[e], ev_w_if[e],
                                 ev_b_if[e], ev_norm_w[e], ev_skip[e], ev_w_out[e],
                                 ln_g[l, 0], ln_b[l, 0], ts=ts)
        else:
            o = l // 2
            x = swa_mixer_layer(x, od_w_qkv[o], od_b_qkv[o], od_sinks[o], od_w_o[o],
                                ln_g[l, 0], ln_b[l, 0], ts=ts)
        wqk, vo = memory_fold(mem, xa_wkv[l], xa_wq[l], xa_wo[l])
        x, xp, route, counts = xattn_router_layer(x, wqk, vo, ln_g[l, 1], ln_b[l, 1],
                                                  moe_w_rg[l], moe_b_rg[l], moe_w_re[l], moe_b_re[l],
                                                  ts=min(XATTN_TS, seq))
        x = hierarchical_moe_layer(x, xp, route, counts, moe_w1, moe_w3, moe_w2,
                                   ln_g[l, 2], ln_b[l, 2], layer=l,
                                   defer_combine=(l + 1 < depth and (l + 1) % 2 == 1))
    return x
```

```python
import functools
import math

import jax
import jax.numpy as jnp
from jax import lax
from jax.experimental import pallas as pl
from jax.experimental.pallas import tpu as pltpu
from jax.experimental.pallas import tpu_sc as plsc

F32 = jnp.float32
BF16 = jnp.bfloat16

A_GROUPS = 4
CHUNK = 128
B_HEADS = 4
B_CONV = 4
C_HEAD_DIM = 64
C_KV_HEADS = 4
X_HEADS = 4
N_GROUPS = 4
EXPERTS_PER_GROUP = 8
N_EXPERTS = N_GROUPS * EXPERTS_PER_GROUP
ROPE_THETA = 10000.0
LN_EPS = 1e-5
DEPTH = 2
DN_ALPHA = (2 * DEPTH) ** 0.25

LANES = 128
VMEM_LIMIT = 48 * 1024 * 1024
NEG = -1e30


def _cparams(sem):
    return pltpu.CompilerParams(dimension_semantics=sem, vmem_limit_bytes=VMEM_LIMIT)


def _full(shape):
    nd = len(shape)
    return pl.BlockSpec(shape, lambda *_: (0,) * nd)


def _dot(a, b):
    return jnp.dot(a, b, preferred_element_type=F32)


def _dot_nt(a, b):
    return lax.dot_general(a, b, (((1,), (1,)), ((), ())), preferred_element_type=F32)


def _split_dot(a, b_bf16):
    hi = a.astype(BF16)
    lo = (a - hi.astype(F32)).astype(BF16)
    return _dot(hi, b_bf16) + _dot(lo, b_bf16)


def _ln(x, g, b):
    mu = jnp.mean(x, axis=-1, keepdims=True)
    xc = x - mu
    var = jnp.mean(xc * xc, axis=-1, keepdims=True)
    return xc * lax.rsqrt(var + LN_EPS) * g + b


LOG2E = math.log2(math.e)


def _silu(x):
    return x * (1.0 / (1.0 + jnp.exp2(x * -LOG2E)))


def _gelu(x):
    return 0.5 * x * (1.0 + jnp.tanh(math.sqrt(2.0 / math.pi) * (x + 0.044715 * (x * x * x))))


def _pack_halves(x):
    c = x.shape[1] // 2
    lo = lax.bitcast_convert_type(x[:, :c].astype(BF16).astype(F32), jnp.uint32)
    hi = lax.bitcast_convert_type(x[:, c:].astype(BF16).astype(F32), jnp.uint32)
    return lax.bitcast_convert_type((lo >> 16) | hi, jnp.int32)


def _unpack_halves(p):
    u = lax.bitcast_convert_type(p, jnp.uint32)
    lo = lax.bitcast_convert_type(u << 16, F32)
    hi = lax.bitcast_convert_type(u & jnp.uint32(0xFFFF0000), F32)
    return lo, hi


def _log_sigmoid(x):
    return jnp.minimum(x, 0.0) - jnp.log(1.0 + jnp.exp(-jnp.abs(x)))


def _even_kernel(x_ref, w_in_ref, gm_g_ref, gm_b_ref, gm_w_ref, gm_bias_ref,
                 conv_w_ref, conv_b_ref, wq_ref, wk_ref, wv_ref, wif_t_ref, bif_t_ref,
                 norm_w_ref, skip_ref, w_out_ref, ln_g_ref, ln_b_ref,
                 o_ref,
                 xm_buf, ct_ref, m_ref, *, ts, aw, bw):
    dh = bw // B_HEADS
    agd = aw // A_GROUPS
    nck = ts // CHUNK
    pad = 8
    j = pl.program_id(1)

    @pl.when(j == 0)
    def _():
        xm_buf[0:pad, :] = jnp.zeros((pad, bw), F32)
        ct_ref[...] = jnp.zeros_like(ct_ref)
        m_ref[...] = jnp.zeros_like(m_ref)

    row = lax.broadcasted_iota(jnp.int32, (CHUNK, CHUNK), 0)
    col = lax.broadcasted_iota(jnp.int32, (CHUNK, CHUNK), 1)
    causal = col <= row
    diag = col == row
    triu = jnp.where(row <= col, 1.0, 0.0).astype(BF16)
    ones_blk = jnp.ones((CHUNK, LANES), BF16)

    x = x_ref[0]
    proj = _dot(x.astype(BF16), w_in_ref[...])
    a_u = _gelu(proj[:, :aw])
    a_v = _gelu(proj[:, aw:2 * aw])
    xm = proj[:, 2 * aw:2 * aw + bw]
    z = proj[:, 2 * aw + bw:]

    vn = _ln(a_v, gm_g_ref[...], gm_b_ref[...]).astype(BF16)
    ya_chunks = []
    for c in range(nck):
        cols = []
        for g in range(A_GROUPS):
            v_cg = vn[c * CHUNK:(c + 1) * CHUNK, g * agd:(g + 1) * agd]
            cols.append(_dot(gm_w_ref[g], v_cg))
        ya_chunks.append(jnp.concatenate(cols, axis=1) + gm_bias_ref[...])
    y_a = a_u * jnp.concatenate(ya_chunks, axis=0)

    xm_buf[pad:pad + ts, :] = xm
    conv = conv_b_ref[...] + conv_w_ref[B_CONV - 1:B_CONV, :] * xm
    for k in range(B_CONV - 1):
        sh = B_CONV - 1 - k
        conv = conv + conv_w_ref[k:k + 1, :] * xm_buf[pad - sh:pad - sh + ts, :]
    xm_buf[pad - (B_CONV - 1):pad, :] = xm_buf[pad + ts - (B_CONV - 1):pad + ts, :]
    xc = _silu(conv)
    xc_b = xc.astype(BF16)
    q = _dot_head_pairs(xc_b, wq_ref)
    k_ = _dot_head_pairs(xc_b, wk_ref)
    v = _dot_head_pairs(xm.astype(BF16), wv_ref)
    gate_in = jnp.concatenate([q, k_, v], axis=1).astype(BF16)
    gates_t = _dot_nt(wif_t_ref[...], gate_in) + bif_t_ref[...]
    ig_all = gates_t[:B_HEADS, :]
    lf_all = _log_sigmoid(gates_t[B_HEADS:, :])
    q_b = q.astype(BF16)
    k_b = (k_ * dh ** -0.5).astype(BF16)
    v_b = v.astype(BF16)
    gate_z = _silu(z)

    lf_c = [lf_all[:, c * CHUNK:(c + 1) * CHUNK] for c in range(nck)]
    b_all = _split_dot(jnp.concatenate(lf_c, axis=0), triu)
    m_prev = m_ref[...]
    a_c, m_c, decay_c, wg_c = [], [], [], []
    for c in range(nck):
        b_r = b_all[c * B_HEADS:(c + 1) * B_HEADS]
        a_r = ig_all[:, c * CHUNK:(c + 1) * CHUNK] - b_r
        bl = b_r[:, CHUNK - 1:CHUNK]
        g_r = bl + a_r
        m_new = jnp.maximum(bl + m_prev, jnp.max(g_r, axis=1, keepdims=True))
        a_c.append(a_r)
        m_c.append(m_prev)
        decay_c.append(jnp.exp(bl + m_prev - m_new))
        wg_c.append(jnp.exp(g_r - m_new))
        m_prev = m_new
    m_ref[...] = m_prev

    units = [(c, h) for c in range(nck) for h in range(B_HEADS)]
    lmat = jnp.concatenate([jnp.where(causal, lf_c[c][h:h + 1, :], 0.0) for c, h in units], axis=0)
    dgm = jnp.concatenate([jnp.where(diag, wg_c[c][h:h + 1, :], 0.0) for c, h in units], axis=0)
    b_t_all = _split_dot(lmat, ones_blk)
    wg_t_all = _dot(dgm.astype(BF16), ones_blk)

    caugs = [ct_ref[h] for h in range(B_HEADS)]
    h_chunks = []
    for c in range(nck):
        sl = slice(c * CHUNK, (c + 1) * CHUNK)
        heads = []
        for h in range(B_HEADS):
            u = c * B_HEADS + h
            us = slice(u * CHUNK, (u + 1) * CHUNK)
            hs = slice(h * dh, (h + 1) * dh)
            qh, kh = q_b[sl, hs], k_b[sl, hs]
            vaug = jnp.concatenate([v_b[sl, hs], ones_blk], axis=1)
            m_row = m_c[c][h:h + 1, :]
            amat = jnp.where(causal, a_c[c][h:h + 1, :], NEG)
            mx = jnp.maximum(jnp.max(amat, axis=1, keepdims=True), m_row)
            w_intra = jnp.exp(amat - mx)
            w_state = jnp.exp(m_row - mx)
            s = _dot_nt(qh, kh) * w_intra
            kw = (kh.astype(F32) * wg_t_all[us]).astype(BF16)
            both = _dot(jnp.concatenate([s.astype(BF16), kw.T], axis=0), vaug)
            caug = caugs[h]
            naug = both[:CHUNK] + jnp.concatenate([w_state, w_state], axis=1) * _dot(qh, caug.astype(BF16))
            num, nq = naug[:, :dh], naug[:, dh:]
            hv = num / jnp.maximum(jnp.abs(nq), jnp.exp(-(b_t_all[us] + mx)))
            decay = decay_c[c][h:h + 1, :]
            caugs[h] = jnp.concatenate([decay, decay], axis=1) * caug + both[CHUNK:]
            hc = hv - jnp.mean(hv, axis=1, keepdims=True)
            hn = hc * lax.rsqrt(jnp.mean(hc * hc, axis=1, keepdims=True) + LN_EPS)
            heads.append(hn)
        h_chunks.append(jnp.concatenate(heads, axis=1))
    for h in range(B_HEADS):
        ct_ref[h] = caugs[h]
    hn_all = jnp.concatenate(h_chunks, axis=0) if nck > 1 else h_chunks[0]
    y_b = (hn_all * norm_w_ref[...] + skip_ref[...] * xc) * gate_z

    mix = jnp.concatenate([y_a, y_b], axis=1).astype(BF16)
    y = _dot(mix, w_out_ref[...])
    o_ref[0] = _ln(DN_ALPHA * x + y, ln_g_ref[...], ln_b_ref[...])


def _block_diag_pairs(w):
    hh, d, _ = w.shape
    wp = w.reshape(hh // 2, 2, d, d)
    eye = jnp.eye(2, dtype=w.dtype)
    return jnp.einsum('pade,ab->padbe', wp, eye).reshape(hh // 2, 2 * d, 2 * d)


def _dot_head_pairs(x, w_ref):
    npair, w2, _ = w_ref.shape
    return jnp.concatenate([_dot(x[:, p * w2:(p + 1) * w2], w_ref[p]) for p in range(npair)], axis=1)


def even_mixer_layer(x, w_in, gm_ln_g, gm_ln_b, gm_ws, gm_bs, conv_w, conv_b, wq, wk, wv, w_if,
                     b_if, norm_w, skip, w_out, ln_g, ln_b, *, ts):
    bsz, seq, d = x.shape
    aw = gm_ln_g.shape[0]
    bw = conv_b.shape[0]
    agd = aw // A_GROUPS
    causal = jnp.tril(jnp.ones((CHUNK, CHUNK), dtype=bool))
    gm_w = jnp.where(causal[None], gm_ws, 0.0).astype(BF16)
    gm_bias = jnp.repeat(gm_bs.T, agd, axis=1)
    row = lambda a: a.reshape(1, -1)
    args = (x, w_in.astype(BF16), row(gm_ln_g), row(gm_ln_b), gm_w, gm_bias,
            conv_w, row(conv_b), _block_diag_pairs(wq).astype(BF16), _block_diag_pairs(wk).astype(BF16),
            _block_diag_pairs(wv).astype(BF16), w_if.T.astype(BF16),
            b_if.reshape(-1, 1), row(norm_w), row(skip), w_out.astype(BF16),
            row(ln_g), row(ln_b))
    in_specs = [pl.BlockSpec((1, ts, d), lambda b, j: (b, j, 0))] + [_full(a.shape) for a in args[1:]]
    dh = bw // B_HEADS
    assert dh == LANES and CHUNK == LANES
    return pl.pallas_call(
        functools.partial(_even_kernel, ts=ts, aw=aw, bw=bw),
        grid=(bsz, seq // ts),
        in_specs=in_specs,
        out_specs=pl.BlockSpec((1, ts, d), lambda b, j: (b, j, 0)),
        out_shape=jax.ShapeDtypeStruct((bsz, seq, d), F32),
        scratch_shapes=[pltpu.VMEM((8 + ts, bw), F32),
                        pltpu.VMEM((B_HEADS, dh, dh + LANES), F32),
                        pltpu.VMEM((B_HEADS, LANES), F32)],
        compiler_params=_cparams(("arbitrary", "arbitrary")),
        name="even_mixer",
    )(*args)


def _memfold_kernel(mem_ref, wkv_ref, wq_ref, wo_ref, wqk_ref, vo_ref, *, d):
    dh = d // X_HEADS
    m_len = mem_ref.shape[1]
    kv = _dot(mem_ref[0].astype(BF16), wkv_ref[...])
    k = (kv[:, :d] * (dh ** -0.5 * LOG2E)).astype(BF16)
    v = kv[:, d:].astype(BF16)
    for h in range(X_HEADS):
        hs = slice(h * dh, (h + 1) * dh)
        ms = slice(h * m_len, (h + 1) * m_len)
        wqk_ref[0, :, ms] = _dot_nt(wq_ref[:, hs], k[:, hs]).astype(BF16)
        vo_ref[0, ms, :] = _dot(v[:, hs], wo_ref[hs, :]).astype(BF16)


def memory_fold(mem, wkv, wq, wo):
    bsz, m_len, d = mem.shape
    hm = X_HEADS * m_len
    args = (mem, wkv.astype(BF16), wq.astype(BF16), wo.astype(BF16))
    return pl.pallas_call(
        functools.partial(_memfold_kernel, d=d),
        grid=(bsz,),
        in_specs=[pl.BlockSpec((1, m_len, d), lambda b: (b, 0, 0))] + [_full(a.shape) for a in args[1:]],
        out_specs=[pl.BlockSpec((1, d, hm), lambda b: (b, 0, 0)),
                   pl.BlockSpec((1, hm, d), lambda b: (b, 0, 0))],
        out_shape=[jax.ShapeDtypeStruct((bsz, d, hm), BF16),
                   jax.ShapeDtypeStruct((bsz, hm, d), BF16)],
        compiler_params=_cparams(("arbitrary",)),
        name="memory_fold",
    )(*args)


ROUTE_W = 128
XATTN_TS = 1024
DEST_ROWS = 8


def _xattn_kernel(x_ref, wqk_ref, vo_ref, ln_g_ref, ln_b_ref, wr_ref, br_ref,
                  o_ref, xp_ref, route_ref, cnt_ref):
    m_len = wqk_ref.shape[2] // X_HEADS

    @pl.when(pl.program_id(1) == 0)
    def _():
        cnt_ref[...] = jnp.zeros_like(cnt_ref)

    x = x_ref[0]
    ts = x.shape[0]
    xb = x.astype(BF16)
    probs = []
    for h in range(X_HEADS):
        s = _dot(xb, wqk_ref[0, :, h * m_len:(h + 1) * m_len])
        p = jnp.exp2(s - jnp.max(s, axis=1, keepdims=True))
        probs.append((p / jnp.sum(p, axis=1, keepdims=True)).astype(BF16))
    y = _dot(jnp.concatenate(probs, axis=1), vo_ref[0])
    x2 = _ln(DN_ALPHA * x + y, ln_g_ref[...], ln_b_ref[...])
    o_ref[0] = x2
    xp_ref[0] = _pack_halves(x2)

    x_hi = x2.astype(BF16)
    x_lo = (x2 - x_hi.astype(F32)).astype(BF16)
    logits = (_dot(x_hi, wr_ref[0]) + _dot(x_lo, wr_ref[0]) + _dot(x_hi, wr_ref[1])) + br_ref[...]
    lane = lax.broadcasted_iota(jnp.int32, (ts, ROUTE_W), 1)
    is_g = lane < N_GROUPS
    lg = jnp.where(is_g, logits, NEG)
    mg = jnp.max(lg, axis=1, keepdims=True)
    gi = jnp.min(jnp.where(jnp.logical_and(is_g, lg == mg), lane, ROUTE_W), axis=1, keepdims=True)
    gate_g = 1.0 / jnp.sum(jnp.where(is_g, jnp.exp(lg - mg), 0.0), axis=1, keepdims=True)
    lo = N_GROUPS + gi * EXPERTS_PER_GROUP
    in_grp = jnp.logical_and(lane >= lo, lane < lo + EXPERTS_PER_GROUP)
    le = jnp.where(in_grp, logits, NEG)
    v1 = jnp.max(le, axis=1, keepdims=True)
    i1 = jnp.min(jnp.where(jnp.logical_and(in_grp, le == v1), lane, ROUTE_W), axis=1, keepdims=True)
    le2 = jnp.where(lane == i1, NEG, le)
    v2 = jnp.max(le2, axis=1, keepdims=True)
    i2 = jnp.min(jnp.where(jnp.logical_and(in_grp, le2 == v2), lane, ROUTE_W), axis=1, keepdims=True)
    e21 = jnp.exp(v2 - v1)
    p1 = 1.0 / (1.0 + e21)
    p2 = e21 * p1
    e1 = (i1 - N_GROUPS).astype(F32)
    e2 = (i2 - N_GROUPS).astype(F32)
    rec = jnp.where(lane == 0, e1, 0.0)
    rec = jnp.where(lane == 1, e2, rec)
    rec = jnp.where(lane == 2, gate_g * p1, rec)
    rec = jnp.where(lane == 3, gate_g * p2, rec)
    route_ref[0] = rec
    sel = jnp.logical_or(lane == i1 - N_GROUPS, lane == i2 - N_GROUPS)
    cnt_ref[0] += jnp.sum(jnp.where(sel, 1.0, 0.0), axis=0, keepdims=True)


def xattn_router_layer(x, wqk, vo, ln_g, ln_b, w_rg, b_rg, w_re, b_re, *, ts):
    bsz, seq, d = x.shape
    hm = wqk.shape[2]
    wr = jnp.zeros((d, ROUTE_W), F32).at[:, :N_GROUPS].set(w_rg).at[:, N_GROUPS:N_GROUPS + N_EXPERTS].set(w_re)
    br = jnp.zeros((1, ROUTE_W), F32).at[0, :N_GROUPS].set(b_rg).at[0, N_GROUPS:N_GROUPS + N_EXPERTS].set(b_re)
    wr_hi = wr.astype(BF16)
    wr_lo = (wr - wr_hi.astype(F32)).astype(BF16)
    wr = jnp.stack([wr_hi, wr_lo])
    args = (x, wqk, vo, ln_g.reshape(1, -1), ln_b.reshape(1, -1), wr, br)
    in_specs = [pl.BlockSpec((1, ts, d), lambda b, j: (b, j, 0)),
                pl.BlockSpec((1, d, hm), lambda b, j: (b, 0, 0)),
                pl.BlockSpec((1, hm, d), lambda b, j: (b, 0, 0))] + [_full(a.shape) for a in args[3:]]
    return pl.pallas_call(
        _xattn_kernel,
        grid=(bsz, seq // ts),
        in_specs=in_specs,
        out_specs=[pl.BlockSpec((1, ts, d), lambda b, j: (b, j, 0)),
                   pl.BlockSpec((1, ts, d // 2), lambda b, j: (b, j, 0)),
                   pl.BlockSpec((1, ts, ROUTE_W), lambda b, j: (b, j, 0)),
                   pl.BlockSpec((1, 1, ROUTE_W), lambda b, j: (b, 0, 0))],
        out_shape=[jax.ShapeDtypeStruct((bsz, seq, d), F32),
                   jax.ShapeDtypeStruct((bsz, seq, d // 2), jnp.int32),
                   jax.ShapeDtypeStruct((bsz, seq, ROUTE_W), F32),
                   jax.ShapeDtypeStruct((bsz, 1, ROUTE_W), F32)],
        compiler_params=_cparams(("arbitrary", "arbitrary")),
        name="xattn_router",
    )(*args)


def _moe_output(x2, g0, g1, rec, ln_g, ln_b):
    a_lo, a_hi = _unpack_halves(g0)
    b_lo, b_hi = _unpack_halves(g1)
    w0, w1 = rec[:, 2:3], rec[:, 3:4]
    y = jnp.concatenate([w0 * a_lo + w1 * b_lo, w0 * a_hi + w1 * b_hi], axis=1)
    return _ln(DN_ALPHA * x2 + y, ln_g, ln_b)


def _swa_kernel(x_ref, g0_ref, g1_ref, route_ref, pln_g_ref, pln_b_ref,
                wqkv_ref, bqkv_ref, cos_ref, sin_ref, sink_ref, wo_ref, ln_g_ref, ln_b_ref,
                o_ref, kprev, vprev, *, ts, cq, ckv):
    j = pl.program_id(1)
    nb = ts // CHUNK
    dh = C_HEAD_DIM
    grp = (cq // dh) // C_KV_HEADS

    @pl.when(j == 0)
    def _():
        kprev[...] = jnp.zeros_like(kprev)
        vprev[...] = jnp.zeros_like(vprev)

    x = _moe_output(x_ref[0], g0_ref[0, 0], g1_ref[0, 0], route_ref[0], pln_g_ref[...], pln_b_ref[...])
    qkv = _dot(x.astype(BF16), wqkv_ref[...]) + bqkv_ref[...]
    cos = cos_ref[...]
    sin = sin_ref[...]
    lane = lax.broadcasted_iota(jnp.int32, (ts, LANES), 1)
    first_half = (lane % dh) < (dh // 2)

    def rope(t):
        outs = []
        for c in range(t.shape[1] // LANES):
            tc = t[:, c * LANES:(c + 1) * LANES]
            rot = jnp.where(first_half, pltpu.roll(tc, LANES - dh // 2, 1), pltpu.roll(tc, dh // 2, 1))
            outs.append(tc * cos + rot * sin)
        return jnp.concatenate(outs, axis=1)

    q = rope(qkv[:, :cq]).astype(BF16)
    k = rope(qkv[:, cq:cq + 2 * ckv]).astype(BF16)
    v = qkv[:, cq + 2 * ckv:].astype(BF16)

    r_i = lax.broadcasted_iota(jnp.int32, (CHUNK, 2 * CHUNK), 0)
    c_i = lax.broadcasted_iota(jnp.int32, (CHUNK, 2 * CHUNK), 1)
    band = jnp.logical_and(c_i > r_i, c_i <= r_i + CHUNK)
    sink_col = c_i == 0
    lane_k = lax.broadcasted_iota(jnp.int32, (2 * CHUNK, LANES), 1)
    key_row = lax.broadcasted_iota(jnp.int32, (2 * CHUNK, LANES), 0)
    lane_q = lax.broadcasted_iota(jnp.int32, (CHUNK, LANES), 1)
    ones_blk = jnp.ones((2 * CHUNK, LANES), BF16)
    zero_b = jnp.zeros((), BF16)
    blocks = []
    for c in range(nb):
        sl = slice(c * CHUNK, (c + 1) * CHUNK)
        if c == 0:
            kb = jnp.concatenate([kprev[...].astype(BF16), k[sl]], axis=0)
            vb = jnp.concatenate([vprev[...].astype(BF16), v[sl]], axis=0)
            first_key = jnp.where(j > 0, 0, CHUNK)
            valid = jnp.logical_and(band, c_i >= first_key)
        else:
            kb = k[(c - 1) * CHUNK:(c + 1) * CHUNK]
            vb = v[(c - 1) * CHUNK:(c + 1) * CHUNK]
            valid = band
        tiles = []
        for h in range(C_KV_HEADS):
            kd = kb[:, h * LANES:(h + 1) * LANES]
            vd = vb[:, h * LANES:(h + 1) * LANES]
            k_lo = jnp.where(lane_k < dh, kd, zero_b)
            k_hi = jnp.where(lane_k >= dh, kd, zero_b)
            vz = jnp.where(key_row == 0, zero_b, vd)
            q2 = jnp.concatenate([q[sl, (2 * h) * LANES:(2 * h + 1) * LANES],
                                  q[sl, (2 * h + 1) * LANES:(2 * h + 2) * LANES]], axis=0)
            s_lo = _dot_nt(q2, k_lo)
            s_hi = _dot_nt(q2, k_hi)
            parts = []
            for qk, g in ((s_lo[:CHUNK], 0), (s_lo[CHUNK:], 2), (s_hi[:CHUNK], 1), (s_hi[CHUNK:], 3)):
                sink = sink_ref[h * grp + g]
                parts.append(jnp.where(valid, qk, jnp.where(sink_col, sink, NEG)))
            s = jnp.concatenate(parts, axis=0)
            p = jnp.exp2(s - jnp.max(s, axis=1, keepdims=True)).astype(BF16)
            den = _dot(p, ones_blk)
            o2 = _dot(p, vz) / den
            tiles.append(jnp.where(lane_q < dh, o2[:CHUNK], o2[2 * CHUNK:3 * CHUNK]))
            tiles.append(jnp.where(lane_q < dh, o2[CHUNK:2 * CHUNK], o2[3 * CHUNK:]))
        blocks.append(jnp.concatenate(tiles, axis=1))
    kprev[...] = k[(nb - 1) * CHUNK:].astype(F32)
    vprev[...] = v[(nb - 1) * CHUNK:].astype(F32)
    att = (jnp.concatenate(blocks, axis=0) if nb > 1 else blocks[0]).astype(BF16)
    y = _dot(att, wo_ref[...])
    o_ref[0] = _ln(DN_ALPHA * x + y, ln_g_ref[...], ln_b_ref[...])


def swa_mixer_layer(pending, w_qkv, b_qkv, sinks, w_o, ln_g, ln_b, *, ts):
    x, g, route, pln_g, pln_b = pending
    bsz, seq, d = x.shape
    g = g.reshape(2, bsz, seq, d // 2)
    cq = w_o.shape[0]
    ckv = (w_qkv.shape[1] - cq) // 2
    dh = C_HEAD_DIM
    inv = ROPE_THETA ** (-jnp.arange(0, dh, 2, dtype=F32) / dh)
    ang = jnp.arange(seq, dtype=F32)[:, None] * inv[None, :]
    reps = LANES // (dh // 2)
    sign = jnp.tile(jnp.concatenate([-jnp.ones((dh // 2,), F32), jnp.ones((dh // 2,), F32)]), LANES // dh)
    cos_t = jnp.tile(jnp.cos(ang), (1, reps))
    sin_t = jnp.tile(jnp.sin(ang), (1, reps)) * sign[None, :]
    assert 2 * dh == LANES

    def dup_heads(t):
        th = t.reshape(t.shape[:-1] + (ckv // dh, dh))
        return jnp.concatenate([th, th], axis=-1).reshape(t.shape[:-1] + (2 * ckv,))

    qs = dh ** -0.5 * LOG2E
    w_all = jnp.concatenate([w_qkv[:, :cq] * qs, dup_heads(w_qkv[:, cq:cq + ckv]),
                             dup_heads(w_qkv[:, cq + ckv:])], axis=1)
    b_all = jnp.concatenate([b_qkv[:cq] * qs, dup_heads(b_qkv[cq:cq + ckv]), dup_heads(b_qkv[cq + ckv:])])
    args = (x, g, g, route, pln_g.reshape(1, -1), pln_b.reshape(1, -1),
            w_all.astype(BF16), b_all.reshape(1, -1), cos_t, sin_t, sinks.astype(F32) * LOG2E,
            w_o.astype(BF16), ln_g.reshape(1, -1), ln_b.reshape(1, -1))
    in_specs = [pl.BlockSpec((1, ts, d), lambda b, j: (b, j, 0)),
                pl.BlockSpec((1, 1, ts, d // 2), lambda b, j: (0, b, j, 0)),
                pl.BlockSpec((1, 1, ts, d // 2), lambda b, j: (1, b, j, 0)),
                pl.BlockSpec((1, ts, ROUTE_W), lambda b, j: (b, j, 0)),
                _full((1, d)), _full((1, d)),
                _full(args[6].shape), _full(args[7].shape),
                pl.BlockSpec((ts, LANES), lambda b, j: (j, 0)),
                pl.BlockSpec((ts, LANES), lambda b, j: (j, 0)),
                pl.BlockSpec(memory_space=pltpu.SMEM),
                _full(args[11].shape), _full(args[12].shape), _full(args[13].shape)]
    return pl.pallas_call(
        functools.partial(_swa_kernel, ts=ts, cq=cq, ckv=ckv),
        grid=(bsz, seq // ts),
        in_specs=in_specs,
        out_specs=pl.BlockSpec((1, ts, d), lambda b, j: (b, j, 0)),
        out_shape=jax.ShapeDtypeStruct((bsz, seq, d), F32),
        scratch_shapes=[pltpu.VMEM((CHUNK, 2 * ckv), F32), pltpu.VMEM((CHUNK, 2 * ckv), F32)],
        compiler_params=_cparams(("arbitrary", "arbitrary")),
        name="swa_mixer",
    )(*args)


def _slot_kernel(route_ref, pstart_ref, dest_ref, carry_ref, *, tb):
    @pl.when(pl.program_id(0) == 0)
    def _():
        carry_ref[...] = pstart_ref[...]

    rec = route_ref[...]
    lane = lax.broadcasted_iota(jnp.int32, (tb, ROUTE_W), 1)
    e0 = rec[:, 0:1].astype(jnp.int32)
    e1 = rec[:, 1:2].astype(jnp.int32)
    oh0 = lane == e0
    oh1 = lane == e1
    ohs = jnp.where(jnp.logical_or(oh0, oh1), 1.0, 0.0)
    r = lax.broadcasted_iota(jnp.int32, (tb, tb), 0)
    c = lax.broadcasted_iota(jnp.int32, (tb, tb), 1)
    before = jnp.where(c < r, 1.0, 0.0).astype(BF16)
    prefix = _dot(before, ohs.astype(BF16)) + carry_ref[...]
    d0 = jnp.sum(jnp.where(oh0, prefix, 0.0), axis=1, keepdims=True)
    d1 = jnp.sum(jnp.where(oh1, prefix, 0.0), axis=1, keepdims=True)
    dest = jnp.where(lane == 0, d0, jnp.where(lane == 1, d1, 0.0))
    dest_ref[...] = dest.T[:DEST_ROWS].astype(jnp.int32)
    carry_ref[...] += jnp.sum(ohs, axis=0, keepdims=True)


def moe_slots(route, pstart, *, tb):
    n = route.shape[0]
    return pl.pallas_call(
        functools.partial(_slot_kernel, tb=tb),
        grid=(n // tb,),
        in_specs=[pl.BlockSpec((tb, ROUTE_W), lambda i: (i, 0)), _full((1, ROUTE_W))],
        out_specs=pl.BlockSpec((DEST_ROWS, tb), lambda i: (0, i)),
        out_shape=jax.ShapeDtypeStruct((DEST_ROWS, n), jnp.int32),
        scratch_shapes=[pltpu.VMEM((1, ROUTE_W), F32)],
        compiler_params=_cparams(("arbitrary",)),
        name="moe_slots",
    )(route, pstart)


def _ffn_kernel(blk_exp_ref, new_exp_ref, nblk_ref, xs_ref, w1_ref, w3_ref, w2_ref, ys_ref,
                w1_b, w3_b, w2_b):
    i = pl.program_id(0)
    used = i < nblk_ref[0]

    @pl.when(jnp.logical_and(used, new_exp_ref[i] == 1))
    def _():
        w1_b[...] = w1_ref[0, 0].astype(BF16)
        w3_b[...] = w3_ref[0, 0].astype(BF16)
        w2_b[...] = w2_ref[0, 0].astype(BF16)

    @pl.when(used)
    def _():
        x_lo, x_hi = _unpack_halves(xs_ref[...])
        x_lo, x_hi = x_lo.astype(BF16), x_hi.astype(BF16)
        dl = x_lo.shape[1]
        h1 = _dot(x_lo, w1_b[:dl, :]) + _dot(x_hi, w1_b[dl:, :])
        h3 = _dot(x_lo, w3_b[:dl, :]) + _dot(x_hi, w3_b[dl:, :])
        h = (_silu(h1) * h3).astype(BF16)
        ys_ref[...] = _pack_halves(_dot(h, w2_b[...]))

    @pl.when(jnp.logical_not(used))
    def _():
        ys_ref[...] = jnp.zeros_like(ys_ref)


def moe_ffn(xs, blk_exp, new_exp, nblk, w1, w3, w2, *, layer, bm):
    n_pad, dp = xs.shape
    d = 2 * dp
    de = w1.shape[3]
    n_blk = n_pad // bm

    def x_map(i, be, ne, nb):
        return (jnp.minimum(i, nb[0] - 1), 0)

    def w_map(i, be, ne, nb):
        return (layer, be[i], 0, 0)

    return pl.pallas_call(
        _ffn_kernel,
        grid_spec=pltpu.PrefetchScalarGridSpec(
            num_scalar_prefetch=3,
            grid=(n_blk,),
            in_specs=[pl.BlockSpec((bm, dp), x_map),
                      pl.BlockSpec((1, 1, d, de), w_map),
                      pl.BlockSpec((1, 1, d, de), w_map),
                      pl.BlockSpec((1, 1, de, d), w_map)],
            out_specs=pl.BlockSpec((bm, dp), lambda i, be, ne, nb: (i, 0)),
            scratch_shapes=[pltpu.VMEM((d, de), BF16), pltpu.VMEM((d, de), BF16),
                            pltpu.VMEM((de, d), BF16)]),
        out_shape=jax.ShapeDtypeStruct((n_pad, dp), jnp.int32),
        compiler_params=_cparams(("arbitrary",)),
        name="moe_ffn",
    )(blk_exp, new_exp, nblk, xs, w1, w3, w2)


def _combine_kernel(x_ref, g0_ref, g1_ref, route_ref, ln_g_ref, ln_b_ref, o_ref):
    o_ref[...] = _moe_output(x_ref[...], g0_ref[0], g1_ref[0], route_ref[...], ln_g_ref[...], ln_b_ref[...])


def moe_combine(x, g, route, ln_g, ln_b, *, tb):
    n, d = x.shape
    row_spec = pl.BlockSpec((tb, d), lambda i: (i, 0))
    return pl.pallas_call(
        _combine_kernel,
        grid=(n // tb,),
        in_specs=[row_spec,
                  pl.BlockSpec((1, tb, d // 2), lambda i: (0, i, 0)),
                  pl.BlockSpec((1, tb, d // 2), lambda i: (1, i, 0)),
                  pl.BlockSpec((tb, ROUTE_W), lambda i: (i, 0)),
                  _full((1, d)), _full((1, d))],
        out_specs=row_spec,
        out_shape=jax.ShapeDtypeStruct((n, d), F32),
        compiler_params=_cparams(("arbitrary",)),
        name="moe_combine",
    )(x, g, g, route, ln_g.reshape(1, -1), ln_b.reshape(1, -1))


MOE_BM = 1024
MIXER_TS = 1024


def hierarchical_moe_layer(x2, xp, route, counts, w1, w3, w2, ln_g, ln_b, *, layer, defer_combine):
    bsz, seq, d = x2.shape
    n = bsz * seq
    bm = MOE_BM
    rt = route.reshape(n, ROUTE_W)
    tb = min(512, n)
    n_blk = (2 * n) // bm + N_EXPERTS
    cnt = counts.sum(axis=0)[0, :N_EXPERTS].astype(jnp.int32)
    pcnt = (cnt + bm - 1) // bm * bm
    pends = jnp.cumsum(pcnt)
    pstart = pends - pcnt
    nblk = (pends[-1] // bm).astype(jnp.int32).reshape(1)
    blk_row = jnp.arange(n_blk, dtype=jnp.int32) * bm
    blk_exp = jnp.minimum(jnp.sum((pends[None, :] <= blk_row[:, None]).astype(jnp.int32), axis=1),
                          N_EXPERTS - 1)
    last_exp = blk_exp[jnp.maximum(nblk[0] - 1, 0)]
    blk_exp = jnp.where(jnp.arange(n_blk) < nblk[0], blk_exp, last_exp)
    new_exp = jnp.concatenate([jnp.ones((1,), jnp.int32),
                               (blk_exp[1:] != blk_exp[:-1]).astype(jnp.int32)])
    pstart_rec = jnp.zeros((1, ROUTE_W), F32).at[0, :N_EXPERTS].set(pstart.astype(F32))
    dest = moe_slots(rt, pstart_rec, tb=tb)
    xs = moe_dispatch(xp.reshape(n, d // 2), dest[0], dest[1], n_blk * bm)
    ys = moe_ffn(xs, blk_exp, new_exp, nblk, w1, w3, w2, layer=layer, bm=bm)
    g = moe_gather(ys, dest[0], dest[1])
    if defer_combine:
        return x2, g, route, ln_g, ln_b
    return moe_combine(x2.reshape(n, d), g, rt, ln_g, ln_b, tb=tb).reshape(bsz, seq, d)


SC_ROWS = 128


def _sc_mesh():
    return plsc.VectorSubcoreMesh(core_axis_name="c", subcore_axis_name="s")


def moe_dispatch(xf, dest0, dest1, n_pad):
    n, d = xf.shape
    info = plsc.get_sparse_core_info()
    nw = info.num_cores * info.num_subcores
    per_w = n // nw
    r = min(SC_ROWS, per_w)

    def body(x_hbm, d0_hbm, d1_hbm, xs_hbm, i0_v, i1_v, rows_v, sem):
        wid = lax.axis_index("s") * info.num_cores + lax.axis_index("c")

        @pl.loop(0, per_w // r)
        def _(c):
            base = pl.multiple_of(wid * per_w + c * r, 8)
            pltpu.sync_copy(d0_hbm.at[pl.ds(base, r)], i0_v)
            pltpu.sync_copy(d1_hbm.at[pl.ds(base, r)], i1_v)
            pltpu.sync_copy(x_hbm.at[pl.ds(base, r)], rows_v)
            pltpu.async_copy(rows_v, xs_hbm.at[i0_v], sem).wait()
            pltpu.async_copy(rows_v, xs_hbm.at[i1_v], sem).wait()

    return pl.kernel(
        body, out_type=jax.ShapeDtypeStruct((n_pad, d), xf.dtype), mesh=_sc_mesh(),
        scratch_types=[pltpu.VMEM((r,), jnp.int32), pltpu.VMEM((r,), jnp.int32),
                       pltpu.VMEM((r, d), xf.dtype), pltpu.SemaphoreType.DMA],
        name="moe_dispatch",
    )(xf, dest0, dest1)


def moe_gather(ys, dest0, dest1):
    n = dest0.shape[0]
    d = ys.shape[1]
    info = plsc.get_sparse_core_info()
    nw = info.num_cores * info.num_subcores
    per_w = n // nw
    r = min(SC_ROWS, per_w)

    def body(ys_hbm, d0_hbm, d1_hbm, g_hbm, i_v, rows_v, sem):
        wid = lax.axis_index("s") * info.num_cores + lax.axis_index("c")

        @pl.loop(0, per_w // r)
        def _(c):
            base = pl.multiple_of(wid * per_w + c * r, 8)
            for k, d_hbm in enumerate((d0_hbm, d1_hbm)):
                pltpu.sync_copy(d_hbm.at[pl.ds(base, r)], i_v)
                pltpu.async_copy(ys_hbm.at[i_v], rows_v, sem).wait()
                pltpu.sync_copy(rows_v, g_hbm.at[k, pl.ds(base, r)])

    return pl.kernel(
        body, out_type=jax.ShapeDtypeStruct((2, n, d), ys.dtype), mesh=_sc_mesh(),
        scratch_types=[pltpu.VMEM((r,), jnp.int32), pltpu.VMEM((r, d), ys.dtype),
                       pltpu.SemaphoreType.DMA],
        name="moe_gather",
    )(ys, dest0, dest1)


def kernel(x, mem, ln_g, ln_b, ev_w_in, ev_gm_ln_g, ev_gm_ln_b, ev_gm_ws, ev_gm_bs, ev_conv_w, ev_conv_b, ev_wq, ev_wk, ev_wv, ev_w_if, ev_b_if, ev_norm_w, ev_skip, ev_w_out, od_w_qkv, od_b_qkv, od_sinks, od_w_o, xa_wq, xa_wkv, xa_wo, moe_w_rg, moe_b_rg, moe_w_re, moe_b_re, moe_w1, moe_w3, moe_w2):
    bsz, seq, d = x.shape
    depth = ln_g.shape[0]
    assert depth == DEPTH
    ts = min(MIXER_TS, seq)
    for l in range(depth):
        if l % 2 == 0:
            e = l // 2
            x = even_mixer_layer(x, ev_w_in[e], ev_gm_ln_g[e], ev_gm_ln_b[e], ev_gm_ws[e], ev_gm_bs[e],
                                 ev_conv_w[e], ev_conv_b[e], ev_wq[e], ev_wk[e], ev_wv[e], ev_w_if[e],
                                 ev_b_if[e], ev_norm_w[e], ev_skip[e], ev_w_out[e],
                                 ln_g[l, 0], ln_b[l, 0], ts=ts)
        else:
            o = l // 2
            x = swa_mixer_layer(x, od_w_qkv[o], od_b_qkv[o], od_sinks[o], od_w_o[o],
                                ln_g[l, 0], ln_b[l, 0], ts=ts)
        wqk, vo = memory_fold(mem, xa_wkv[l], xa_wq[l], xa_wo[l])
        x, xp, route, counts = xattn_router_layer(x, wqk, vo, ln_g[l, 1], ln_b[l, 1],
                                                  moe_w_rg[l], moe_b_rg[l], moe_w_re[l], moe_b_re[l],
                                                  ts=min(XATTN_TS, seq))
        x = hierarchical_moe_layer(x, xp, route, counts, moe_w1, moe_w3, moe_w2,
                                   ln_g[l, 2], ln_b[l, 2], layer=l,
                                   defer_combine=(l + 1 < depth and (l + 1) % 2 == 1))
    return x
```

```python
import functools
import math

import jax
import jax.numpy as jnp
from jax import lax
from jax.experimental import pallas as pl
from jax.experimental.pallas import tpu as pltpu
from jax.experimental.pallas import tpu_sc as plsc

F32 = jnp.float32
BF16 = jnp.bfloat16

A_GROUPS = 4
CHUNK = 128
B_HEADS = 4
B_CONV = 4
C_HEAD_DIM = 64
C_KV_HEADS = 4
X_HEADS = 4
N_GROUPS = 4
EXPERTS_PER_GROUP = 8
N_EXPERTS = N_GROUPS * EXPERTS_PER_GROUP
ROPE_THETA = 10000.0
LN_EPS = 1e-5
DEPTH = 2
DN_ALPHA = (2 * DEPTH) ** 0.25

LANES = 128
VMEM_LIMIT = 48 * 1024 * 1024
NEG = -1e30


def _cparams(sem):
    return pltpu.CompilerParams(dimension_semantics=sem, vmem_limit_bytes=VMEM_LIMIT)


def _full(shape):
    nd = len(shape)
    return pl.BlockSpec(shape, lambda *_: (0,) * nd)


def _dot(a, b):
    return jnp.dot(a, b, preferred_element_type=F32)


def _dot_nt(a, b):
    return lax.dot_general(a, b, (((1,), (1,)), ((), ())), preferred_element_type=F32)


def _split_dot(a, b_bf16):
    hi = a.astype(BF16)
    lo = (a - hi.astype(F32)).astype(BF16)
    return _dot(hi, b_bf16) + _dot(lo, b_bf16)


def _ln(x, g, b):
    mu = jnp.mean(x, axis=-1, keepdims=True)
    xc = x - mu
    var = jnp.mean(xc * xc, axis=-1, keepdims=True)
    return xc * lax.rsqrt(var + LN_EPS) * g + b


LOG2E = math.log2(math.e)


def _silu(x):
    return x * (1.0 / (1.0 + jnp.exp2(x * -LOG2E)))


def _gelu(x):
    return 0.5 * x * (1.0 + jnp.tanh(math.sqrt(2.0 / math.pi) * (x + 0.044715 * (x * x * x))))


def _pack_halves(x):
    c = x.shape[1] // 2
    lo = lax.bitcast_convert_type(x[:, :c].astype(BF16).astype(F32), jnp.uint32)
    hi = lax.bitcast_convert_type(x[:, c:].astype(BF16).astype(F32), jnp.uint32)
    return lax.bitcast_convert_type((lo >> 16) | hi, jnp.int32)


def _unpack_halves(p):
    u = lax.bitcast_convert_type(p, jnp.uint32)
    lo = lax.bitcast_convert_type(u << 16, F32)
    hi = lax.bitcast_convert_type(u & jnp.uint32(0xFFFF0000), F32)
    return lo, hi


def _log_sigmoid(x):
    return jnp.minimum(x, 0.0) - jnp.log(1.0 + jnp.exp(-jnp.abs(x)))


def _even_kernel(x_ref, w_in_ref, gm_g_ref, gm_b_ref, gm_w_ref, gm_bias_ref,
                 conv_w_ref, conv_b_ref, wq_ref, wk_ref, wv_ref, wif_t_ref, bif_t_ref,
                 norm_w_ref, skip_ref, w_out_ref, ln_g_ref, ln_b_ref,
                 o_ref,
                 xm_buf, ct_ref, m_ref, *, ts, aw, bw):
    dh = bw // B_HEADS
    agd = aw // A_GROUPS
    nck = ts // CHUNK
    pad = 8
    j = pl.program_id(1)

    @pl.when(j == 0)
    def _():
        xm_buf[0:pad, :] = jnp.zeros((pad, bw), F32)
        ct_ref[...] = jnp.zeros_like(ct_ref)
        m_ref[...] = jnp.zeros_like(m_ref)

    row = lax.broadcasted_iota(jnp.int32, (CHUNK, CHUNK), 0)
    col = lax.broadcasted_iota(jnp.int32, (CHUNK, CHUNK), 1)
    causal = col <= row
    diag = col == row
    triu = jnp.where(row <= col, 1.0, 0.0).astype(BF16)
    ones_blk = jnp.ones((CHUNK, LANES), BF16)

    x = x_ref[0]
    proj = _dot(x.astype(BF16), w_in_ref[...])
    a_u = _gelu(proj[:, :aw])
    a_v = _gelu(proj[:, aw:2 * aw])
    xm = proj[:, 2 * aw:2 * aw + bw]
    z = proj[:, 2 * aw + bw:]

    vn = _ln(a_v, gm_g_ref[...], gm_b_ref[...]).astype(BF16)
    ya_chunks = []
    for c in range(nck):
        cols = []
        for g in range(A_GROUPS):
            v_cg = vn[c * CHUNK:(c + 1) * CHUNK, g * agd:(g + 1) * agd]
            cols.append(_dot(gm_w_ref[g], v_cg))
        ya_chunks.append(jnp.concatenate(cols, axis=1) + gm_bias_ref[...])
    y_a = a_u * jnp.concatenate(ya_chunks, axis=0)

    xm_buf[pad:pad + ts, :] = xm
    conv = conv_b_ref[...] + conv_w_ref[B_CONV - 1:B_CONV, :] * xm
    for k in range(B_CONV - 1):
        sh = B_CONV - 1 - k
        conv = conv + conv_w_ref[k:k + 1, :] * xm_buf[pad - sh:pad - sh + ts, :]
    xm_buf[pad - (B_CONV - 1):pad, :] = xm_buf[pad + ts - (B_CONV - 1):pad + ts, :]
    xc = _silu(conv)
    xc_b = xc.astype(BF16)
    q = _dot_head_pairs(xc_b, wq_ref)
    k_ = _dot_head_pairs(xc_b, wk_ref)
    v = _dot_head_pairs(xm.astype(BF16), wv_ref)
    gate_in = jnp.concatenate([q, k_, v], axis=1).astype(BF16)
    gates_t = _dot_nt(wif_t_ref[...], gate_in) + bif_t_ref[...]
    ig_all = gates_t[:B_HEADS, :]
    lf_all = _log_sigmoid(gates_t[B_HEADS:, :])
    q_b = q.astype(BF16)
    k_b = (k_ * dh ** -0.5).astype(BF16)
    v_b = v.astype(BF16)
    gate_z = _silu(z)

    lf_c = [lf_all[:, c * CHUNK:(c + 1) * CHUNK] for c in range(nck)]
    b_all = _split_dot(jnp.concatenate(lf_c, axis=0), triu)
    m_prev = m_ref[...]
    a_c, m_c, decay_c, wg_c = [], [], [], []
    for c in range(nck):
        b_r = b_all[c * B_HEADS:(c + 1) * B_HEADS]
        a_r = ig_all[:, c * CHUNK:(c + 1) * CHUNK] - b_r
        bl = b_r[:, CHUNK - 1:CHUNK]
        g_r = bl + a_r
        m_new = jnp.maximum(bl + m_prev, jnp.max(g_r, axis=1, keepdims=True))
        a_c.append(a_r)
        m_c.append(m_prev)
        decay_c.append(jnp.exp(bl + m_prev - m_new))
        wg_c.append(jnp.exp(g_r - m_new))
        m_prev = m_new
    m_ref[...] = m_prev

    units = [(c, h) for c in range(nck) for h in range(B_HEADS)]
    lmat = jnp.concatenate([jnp.where(causal, lf_c[c][h:h + 1, :], 0.0) for c, h in units], axis=0)
    dgm = jnp.concatenate([jnp.where(diag, wg_c[c][h:h + 1, :], 0.0) for c, h in units], axis=0)
    b_t_all = _split_dot(lmat, ones_blk)
    wg_t_all = _dot(dgm.astype(BF16), ones_blk)

    caugs = [ct_ref[h] for h in range(B_HEADS)]
    h_chunks = []
    for c in range(nck):
        sl = slice(c * CHUNK, (c + 1) * CHUNK)
        heads = []
        for h in range(B_HEADS):
            u = c * B_HEADS + h
            us = slice(u * CHUNK, (u + 1) * CHUNK)
            hs = slice(h * dh, (h + 1) * dh)
            qh, kh = q_b[sl, hs], k_b[sl, hs]
            vaug = jnp.concatenate([v_b[sl, hs], ones_blk], axis=1)
            m_row = m_c[c][h:h + 1, :]
            amat = jnp.where(causal, a_c[c][h:h + 1, :], NEG)
            mx = jnp.maximum(jnp.max(amat, axis=1, keepdims=True), m_row)
            w_intra = jnp.exp(amat - mx)
            w_state = jnp.exp(m_row - mx)
            s = _dot_nt(qh, kh) * w_intra
            kw = (kh.astype(F32) * wg_t_all[us]).astype(BF16)
            both = _dot(jnp.concatenate([s.astype(BF16), kw.T], axis=0), vaug)
            caug = caugs[h]
            naug = both[:CHUNK] + jnp.concatenate([w_state, w_state], axis=1) * _dot(qh, caug.astype(BF16))
            num, nq = naug[:, :dh], naug[:, dh:]
            hv = num / jnp.maximum(jnp.abs(nq), jnp.exp(-(b_t_all[us] + mx)))
            decay = decay_c[c][h:h + 1, :]
            caugs[h] = jnp.concatenate([decay, decay], axis=1) * caug + both[CHUNK:]
            hc = hv - jnp.mean(hv, axis=1, keepdims=True)
            hn = hc * lax.rsqrt(jnp.mean(hc * hc, axis=1, keepdims=True) + LN_EPS)
            heads.append(hn)
        h_chunks.append(jnp.concatenate(heads, axis=1))
    for h in range(B_HEADS):
        ct_ref[h] = caugs[h]
    hn_all = jnp.concatenate(h_chunks, axis=0) if nck > 1 else h_chunks[0]
    y_b = (hn_all * norm_w_ref[...] + skip_ref[...] * xc) * gate_z

    mix = jnp.concatenate([y_a, y_b], axis=1).astype(BF16)
    y = _dot(mix, w_out_ref[...])
    o_ref[0] = _ln(DN_ALPHA * x + y, ln_g_ref[...], ln_b_ref[...])


def _block_diag_pairs(w):
    hh, d, _ = w.shape
    wp = w.reshape(hh // 2, 2, d, d)
    eye = jnp.eye(2, dtype=w.dtype)
    return jnp.einsum('pade,ab->padbe', wp, eye).reshape(hh // 2, 2 * d, 2 * d)


def _dot_head_pairs(x, w_ref):
    npair, w2, _ = w_ref.shape
    return jnp.concatenate([_dot(x[:, p * w2:(p + 1) * w2], w_ref[p]) for p in range(npair)], axis=1)


def even_mixer_layer(x, w_in, gm_ln_g, gm_ln_b, gm_ws, gm_bs, conv_w, conv_b, wq, wk, wv, w_if,
                     b_if, norm_w, skip, w_out, ln_g, ln_b, *, ts):
    bsz, seq, d = x.shape
    aw = gm_ln_g.shape[0]
    bw = conv_b.shape[0]
    agd = aw // A_GROUPS
    causal = jnp.tril(jnp.ones((CHUNK, CHUNK), dtype=bool))
    gm_w = jnp.where(causal[None], gm_ws, 0.0).astype(BF16)
    gm_bias = jnp.repeat(gm_bs.T, agd, axis=1)
    row = lambda a: a.reshape(1, -1)
    args = (x, w_in.astype(BF16), row(gm_ln_g), row(gm_ln_b), gm_w, gm_bias,
            conv_w, row(conv_b), _block_diag_pairs(wq).astype(BF16), _block_diag_pairs(wk).astype(BF16),
            _block_diag_pairs(wv).astype(BF16), w_if.T.astype(BF16),
            b_if.reshape(-1, 1), row(norm_w), row(skip), w_out.astype(BF16),
            row(ln_g), row(ln_b))
    in_specs = [pl.BlockSpec((1, ts, d), lambda b, j: (b, j, 0))] + [_full(a.shape) for a in args[1:]]
    dh = bw // B_HEADS
    assert dh == LANES and CHUNK == LANES
    return pl.pallas_call(
        functools.partial(_even_kernel, ts=ts, aw=aw, bw=bw),
        grid=(bsz, seq // ts),
        in_specs=in_specs,
        out_specs=pl.BlockSpec((1, ts, d), lambda b, j: (b, j, 0)),
        out_shape=jax.ShapeDtypeStruct((bsz, seq, d), F32),
        scratch_shapes=[pltpu.VMEM((8 + ts, bw), F32),
                        pltpu.VMEM((B_HEADS, dh, dh + LANES), F32),
                        pltpu.VMEM((B_HEADS, LANES), F32)],
        compiler_params=_cparams(("arbitrary", "arbitrary")),
        name="even_mixer",
    )(*args)


def _memfold_kernel(mem_ref, wkv_ref, wq_ref, wo_ref, wqk_ref, vo_ref, *, d):
    dh = d // X_HEADS
    m_len = mem_ref.shape[1]
    kv = _dot(mem_ref[0].astype(BF16), wkv_ref[...])
    k = (kv[:, :d] * (dh ** -0.5 * LOG2E)).astype(BF16)
    v = kv[:, d:].astype(BF16)
    for h in range(X_HEADS):
        hs = slice(h * dh, (h + 1) * dh)
        ms = slice(h * m_len, (h + 1) * m_len)
        wqk_ref[0, :, ms] = _dot_nt(wq_ref[:, hs], k[:, hs]).astype(BF16)
        vo_ref[0, ms, :] = _dot(v[:, hs], wo_ref[hs, :]).astype(BF16)


def memory_fold(mem, wkv, wq, wo):
    bsz, m_len, d = mem.shape
    hm = X_HEADS * m_len
    args = (mem, wkv.astype(BF16), wq.astype(BF16), wo.astype(BF16))
    return pl.pallas_call(
        functools.partial(_memfold_kernel, d=d),
        grid=(bsz,),
        in_specs=[pl.BlockSpec((1, m_len, d), lambda b: (b, 0, 0))] + [_full(a.shape) for a in args[1:]],
        out_specs=[pl.BlockSpec((1, d, hm), lambda b: (b, 0, 0)),
                   pl.BlockSpec((1, hm, d), lambda b: (b, 0, 0))],
        out_shape=[jax.ShapeDtypeStruct((bsz, d, hm), BF16),
                   jax.ShapeDtypeStruct((bsz, hm, d), BF16)],
        compiler_params=_cparams(("arbitrary",)),
        name="memory_fold",
    )(*args)


ROUTE_W = 128
XATTN_TS = 1024
DEST_ROWS = 8


def _xattn_kernel(x_ref, wqk_ref, vo_ref, ln_g_ref, ln_b_ref, wr_ref, br_ref,
                  o_ref, xp_ref, route_ref, cnt_ref):
    m_len = wqk_ref.shape[2] // X_HEADS

    @pl.when(pl.program_id(1) == 0)
    def _():
        cnt_ref[...] = jnp.zeros_like(cnt_ref)

    x = x_ref[0]
    ts = x.shape[0]
    xb = x.astype(BF16)
    probs = []
    for h in range(X_HEADS):
        s = _dot(xb, wqk_ref[0, :, h * m_len:(h + 1) * m_len])
        p = jnp.exp2(s - jnp.max(s, axis=1, keepdims=True))
        probs.append((p / jnp.sum(p, axis=1, keepdims=True)).astype(BF16))
    y = _dot(jnp.concatenate(probs, axis=1), vo_ref[0])
    x2 = _ln(DN_ALPHA * x + y, ln_g_ref[...], ln_b_ref[...])
    o_ref[0] = x2
    xp_ref[0] = _pack_halves(x2)

    x_hi = x2.astype(BF16)
    x_lo = (x2 - x_hi.astype(F32)).astype(BF16)
    logits = (_dot(x_hi, wr_ref[0]) + _dot(x_lo, wr_ref[0]) + _dot(x_hi, wr_ref[1])) + br_ref[...]
    lane = lax.broadcasted_iota(jnp.int32, (ts, ROUTE_W), 1)
    is_g = lane < N_GROUPS
    lg = jnp.where(is_g, logits, NEG)
    mg = jnp.max(lg, axis=1, keepdims=True)
    gi = jnp.min(jnp.where(jnp.logical_and(is_g, lg == mg), lane, ROUTE_W), axis=1, keepdims=True)
    gate_g = 1.0 / jnp.sum(jnp.where(is_g, jnp.exp(lg - mg), 0.0), axis=1, keepdims=True)
    lo = N_GROUPS + gi * EXPERTS_PER_GROUP
    in_grp = jnp.logical_and(lane >= lo, lane < lo + EXPERTS_PER_GROUP)
    le = jnp.where(in_grp, logits, NEG)
    v1 = jnp.max(le, axis=1, keepdims=True)
    i1 = jnp.min(jnp.where(jnp.logical_and(in_grp, le == v1), lane, ROUTE_W), axis=1, keepdims=True)
    le2 = jnp.where(lane == i1, NEG, le)
    v2 = jnp.max(le2, axis=1, keepdims=True)
    i2 = jnp.min(jnp.where(jnp.logical_and(in_grp, le2 == v2), lane, ROUTE_W), axis=1, keepdims=True)
    e21 = jnp.exp(v2 - v1)
    p1 = 1.0 / (1.0 + e21)
    p2 = e21 * p1
    e1 = (i1 - N_GROUPS).astype(F32)
    e2 = (i2 - N_GROUPS).astype(F32)
    rec = jnp.where(lane == 0, e1, 0.0)
    rec = jnp.where(lane == 1, e2, rec)
    rec = jnp.where(lane == 2, gate_g * p1, rec)
    rec = jnp.where(lane == 3, gate_g * p2, rec)
    route_ref[0] = rec
    sel = jnp.logical_or(lane == i1 - N_GROUPS, lane == i2 - N_GROUPS)
    cnt_ref[0] += jnp.sum(jnp.where(sel, 1.0, 0.0), axis=0, keepdims=True)


def xattn_router_layer(x, wqk, vo, ln_g, ln_b, w_rg, b_rg, w_re, b_re, *, ts):
    bsz, seq, d = x.shape
    hm = wqk.shape[2]
    wr = jnp.zeros((d, ROUTE_W), F32).at[:, :N_GROUPS].set(w_rg).at[:, N_GROUPS:N_GROUPS + N_EXPERTS].set(w_re)
    br = jnp.zeros((1, ROUTE_W), F32).at[0, :N_GROUPS].set(b_rg).at[0, N_GROUPS:N_GROUPS + N_EXPERTS].set(b_re)
    wr_hi = wr.astype(BF16)
    wr_lo = (wr - wr_hi.astype(F32)).astype(BF16)
    wr = jnp.stack([wr_hi, wr_lo])
    args = (x, wqk, vo, ln_g.reshape(1, -1), ln_b.reshape(1, -1), wr, br)
    in_specs = [pl.BlockSpec((1, ts, d), lambda b, j: (b, j, 0)),
                pl.BlockSpec((1, d, hm), lambda b, j: (b, 0, 0)),
                pl.BlockSpec((1, hm, d), lambda b, j: (b, 0, 0))] + [_full(a.shape) for a in args[3:]]
    return pl.pallas_call(
        _xattn_kernel,
        grid=(bsz, seq // ts),
        in_specs=in_specs,
        out_specs=[pl.BlockSpec((1, ts, d), lambda b, j: (b, j, 0)),
                   pl.BlockSpec((1, ts, d // 2), lambda b, j: (b, j, 0)),
                   pl.BlockSpec((1, ts, ROUTE_W), lambda b, j: (b, j, 0)),
                   pl.BlockSpec((1, 1, ROUTE_W), lambda b, j: (b, 0, 0))],
        out_shape=[jax.ShapeDtypeStruct((bsz, seq, d), F32),
                   jax.ShapeDtypeStruct((bsz, seq, d // 2), jnp.int32),
                   jax.ShapeDtypeStruct((bsz, seq, ROUTE_W), F32),
                   jax.ShapeDtypeStruct((bsz, 1, ROUTE_W), F32)],
        compiler_params=_cparams(("arbitrary", "arbitrary")),
        name="xattn_router",
    )(*args)


def _moe_output(x2, g0, g1, rec, ln_g, ln_b):
    a_lo, a_hi = _unpack_halves(g0)
    b_lo, b_hi = _unpack_halves(g1)
    w0, w1 = rec[:, 2:3], rec[:, 3:4]
    y = jnp.concatenate([w0 * a_lo + w1 * b_lo, w0 * a_hi + w1 * b_hi], axis=1)
    return _ln(DN_ALPHA * x2 + y, ln_g, ln_b)


def _swa_kernel(x_ref, g0_ref, g1_ref, route_ref, pln_g_ref, pln_b_ref,
                wqkv_ref, bqkv_ref, cos_ref, sin_ref, sink_ref, wo_ref, ln_g_ref, ln_b_ref,
                o_ref, kprev, vprev, *, ts, cq, ckv):
    j = pl.program_id(1)
    nb = ts // CHUNK
    dh = C_HEAD_DIM
    grp = (cq // dh) // C_KV_HEADS

    @pl.when(j == 0)
    def _():
        kprev[...] = jnp.zeros_like(kprev)
        vprev[...] = jnp.zeros_like(vprev)

    x = _moe_output(x_ref[0], g0_ref[0, 0], g1_ref[0, 0], route_ref[0], pln_g_ref[...], pln_b_ref[...])
    qkv = _dot(x.astype(BF16), wqkv_ref[...]) + bqkv_ref[...]
    cos = cos_ref[...]
    sin = sin_ref[...]
    lane = lax.broadcasted_iota(jnp.int32, (ts, LANES), 1)
    first_half = (lane % dh) < (dh // 2)

    def rope(t):
        outs = []
        for c in range(t.shape[1] // LANES):
            tc = t[:, c * LANES:(c + 1) * LANES]
            rot = jnp.where(first_half, pltpu.roll(tc, LANES - dh // 2, 1), pltpu.roll(tc, dh // 2, 1))
            outs.append(tc * cos + rot * sin)
        return jnp.concatenate(outs, axis=1)

    q = rope(qkv[:, :cq]).astype(BF16)
    k = rope(qkv[:, cq:cq + 2 * ckv]).astype(BF16)
    v = qkv[:, cq + 2 * ckv:].astype(BF16)

    r_i = lax.broadcasted_iota(jnp.int32, (CHUNK, 2 * CHUNK), 0)
    c_i = lax.broadcasted_iota(jnp.int32, (CHUNK, 2 * CHUNK), 1)
    band = jnp.logical_and(c_i > r_i, c_i <= r_i + CHUNK)
    sink_col = c_i == 0
    lane_k = lax.broadcasted_iota(jnp.int32, (2 * CHUNK, LANES), 1)
    key_row = lax.broadcasted_iota(jnp.int32, (2 * CHUNK, LANES), 0)
    lane_q = lax.broadcasted_iota(jnp.int32, (CHUNK, LANES), 1)
    ones_blk = jnp.ones((2 * CHUNK, LANES), BF16)
    zero_b = jnp.zeros((), BF16)
    blocks = []
    for c in range(nb):
        sl = slice(c * CHUNK, (c + 1) * CHUNK)
        if c == 0:
            kb = jnp.concatenate([kprev[...].astype(BF16), k[sl]], axis=0)
            vb = jnp.concatenate([vprev[...].astype(BF16), v[sl]], axis=0)
            first_key = jnp.where(j > 0, 0, CHUNK)
            valid = jnp.logical_and(band, c_i >= first_key)
        else:
            kb = k[(c - 1) * CHUNK:(c + 1) * CHUNK]
            vb = v[(c - 1) * CHUNK:(c + 1) * CHUNK]
            valid = band
        tiles = []
        for h in range(C_KV_HEADS):
            kd = kb[:, h * LANES:(h + 1) * LANES]
            vd = vb[:, h * LANES:(h + 1) * LANES]
            k_lo = jnp.where(lane_k < dh, kd, zero_b)
            k_hi = jnp.where(lane_k >= dh, kd, zero_b)
            vz = jnp.where(key_row == 0, zero_b, vd)
            q2 = jnp.concatenate([q[sl, (2 * h) * LANES:(2 * h + 1) * LANES],
                                  q[sl, (2 * h + 1) * LANES:(2 * h + 2) * LANES]], axis=0)
            s_lo = _dot_nt(q2, k_lo)
            s_hi = _dot_nt(q2, k_hi)
            parts = []
            for qk, g in ((s_lo[:CHUNK], 0), (s_lo[CHUNK:], 2), (s_hi[:CHUNK], 1), (s_hi[CHUNK:], 3)):
                sink = sink_ref[h * grp + g]
                parts.append(jnp.where(valid, qk, jnp.where(sink_col, sink, NEG)))
            s = jnp.concatenate(parts, axis=0)
            p = jnp.exp2(s - jnp.max(s, axis=1, keepdims=True)).astype(BF16)
            den = _dot(p, ones_blk)
            o2 = _dot(p, vz) / den
            tiles.append(jnp.where(lane_q < dh, o2[:CHUNK], o2[2 * CHUNK:3 * CHUNK]))
            tiles.append(jnp.where(lane_q < dh, o2[CHUNK:2 * CHUNK], o2[3 * CHUNK:]))
        blocks.append(jnp.concatenate(tiles, axis=1))
    kprev[...] = k[(nb - 1) * CHUNK:].astype(F32)
    vprev[...] = v[(nb - 1) * CHUNK:].astype(F32)
    att = (jnp.concatenate(blocks, axis=0) if nb > 1 else blocks[0]).astype(BF16)
    y = _dot(att, wo_ref[...])
    o_ref[0] = _ln(DN_ALPHA * x + y, ln_g_ref[...], ln_b_ref[...])


def swa_mixer_layer(pending, w_qkv, b_qkv, sinks, w_o, ln_g, ln_b, *, ts):
    x, g, route, pln_g, pln_b = pending
    bsz, seq, d = x.shape
    g = g.reshape(2, bsz, seq, d // 2)
    cq = w_o.shape[0]
    ckv = (w_qkv.shape[1] - cq) // 2
    dh = C_HEAD_DIM
    inv = ROPE_THETA ** (-jnp.arange(0, dh, 2, dtype=F32) / dh)
    ang = jnp.arange(seq, dtype=F32)[:, None] * inv[None, :]
    reps = LANES // (dh // 2)
    sign = jnp.tile(jnp.concatenate([-jnp.ones((dh // 2,), F32), jnp.ones((dh // 2,), F32)]), LANES // dh)
    cos_t = jnp.tile(jnp.cos(ang), (1, reps))
    sin_t = jnp.tile(jnp.sin(ang), (1, reps)) * sign[None, :]
    assert 2 * dh == LANES

    def dup_heads(t):
        th = t.reshape(t.shape[:-1] + (ckv // dh, dh))
        return jnp.concatenate([th, th], axis=-1).reshape(t.shape[:-1] + (2 * ckv,))

    qs = dh ** -0.5 * LOG2E
    w_all = jnp.concatenate([w_qkv[:, :cq] * qs, dup_heads(w_qkv[:, cq:cq + ckv]),
                             dup_heads(w_qkv[:, cq + ckv:])], axis=1)
    b_all = jnp.concatenate([b_qkv[:cq] * qs, dup_heads(b_qkv[cq:cq + ckv]), dup_heads(b_qkv[cq + ckv:])])
    args = (x, g, g, route, pln_g.reshape(1, -1), pln_b.reshape(1, -1),
            w_all.astype(BF16), b_all.reshape(1, -1), cos_t, sin_t, sinks.astype(F32) * LOG2E,
            w_o.astype(BF16), ln_g.reshape(1, -1), ln_b.reshape(1, -1))
    in_specs = [pl.BlockSpec((1, ts, d), lambda b, j: (b, j, 0)),
                pl.BlockSpec((1, 1, ts, d // 2), lambda b, j: (0, b, j, 0)),
                pl.BlockSpec((1, 1, ts, d // 2), lambda b, j: (1, b, j, 0)),
                pl.BlockSpec((1, ts, ROUTE_W), lambda b, j: (b, j, 0)),
                _full((1, d)), _full((1, d)),
                _full(args[6].shape), _full(args[7].shape),
                pl.BlockSpec((ts, LANES), lambda b, j: (j, 0)),
                pl.BlockSpec((ts, LANES), lambda b, j: (j, 0)),
                pl.BlockSpec(memory_space=pltpu.SMEM),
                _full(args[11].shape), _full(args[12].shape), _full(args[13].shape)]
    return pl.pallas_call(
        functools.partial(_swa_kernel, ts=ts, cq=cq, ckv=ckv),
        grid=(bsz, seq // ts),
        in_specs=in_specs,
        out_specs=pl.BlockSpec((1, ts, d), lambda b, j: (b, j, 0)),
        out_shape=jax.ShapeDtypeStruct((bsz, seq, d), F32),
        scratch_shapes=[pltpu.VMEM((CHUNK, 2 * ckv), F32), pltpu.VMEM((CHUNK, 2 * ckv), F32)],
        compiler_params=_cparams(("arbitrary", "arbitrary")),
        name="swa_mixer",
    )(*args)


def _slot_kernel(route_ref, pstart_ref, dest_ref, carry_ref, *, tb):
    @pl.when(pl.program_id(0) == 0)
    def _():
        carry_ref[...] = pstart_ref[...]

    rec = route_ref[...]
    lane = lax.broadcasted_iota(jnp.int32, (tb, ROUTE_W), 1)
    e0 = rec[:, 0:1].astype(jnp.int32)
    e1 = rec[:, 1:2].astype(jnp.int32)
    oh0 = lane == e0
    oh1 = lane == e1
    ohs = jnp.where(jnp.logical_or(oh0, oh1), 1.0, 0.0)
    r = lax.broadcasted_iota(jnp.int32, (tb, tb), 0)
    c = lax.broadcasted_iota(jnp.int32, (tb, tb), 1)
    before = jnp.where(c < r, 1.0, 0.0).astype(BF16)
    prefix = _dot(before, ohs.astype(BF16)) + carry_ref[...]
    d0 = jnp.sum(jnp.where(oh0, prefix, 0.0), axis=1, keepdims=True)
    d1 = jnp.sum(jnp.where(oh1, prefix, 0.0), axis=1, keepdims=True)
    dest = jnp.where(lane == 0, d0, jnp.where(lane == 1, d1, 0.0))
    dest_ref[...] = dest.T[:DEST_ROWS].astype(jnp.int32)
    carry_ref[...] += jnp.sum(ohs, axis=0, keepdims=True)


def moe_slots(route, pstart, *, tb):
    n = route.shape[0]
    return pl.pallas_call(
        functools.partial(_slot_kernel, tb=tb),
        grid=(n // tb,),
        in_specs=[pl.BlockSpec((tb, ROUTE_W), lambda i: (i, 0)), _full((1, ROUTE_W))],
        out_specs=pl.BlockSpec((DEST_ROWS, tb), lambda i: (0, i)),
        out_shape=jax.ShapeDtypeStruct((DEST_ROWS, n), jnp.int32),
        scratch_shapes=[pltpu.VMEM((1, ROUTE_W), F32)],
        compiler_params=_cparams(("arbitrary",)),
        name="moe_slots",
    )(route, pstart)


def _ffn_kernel(blk_exp_ref, new_exp_ref, nblk_ref, xs_ref, w1_ref, w3_ref, w2_ref, ys_ref,
                w1_b, w3_b, w2_b):
    i = pl.program_id(0)
    used = i < nblk_ref[0]

    @pl.when(jnp.logical_and(used, new_exp_ref[i] == 1))
    def _():
        w1_b[...] = w1_ref[0, 0].astype(BF16)
        w3_b[...] = w3_ref[0, 0].astype(BF16)
        w2_b[...] = w2_ref[0, 0].astype(BF16)

    @pl.when(used)
    def _():
        x_lo, x_hi = _unpack_halves(xs_ref[...])
        x_lo, x_hi = x_lo.astype(BF16), x_hi.astype(BF16)
        dl = x_lo.shape[1]
        h1 = _dot(x_lo, w1_b[:dl, :]) + _dot(x_hi, w1_b[dl:, :])
        h3 = _dot(x_lo, w3_b[:dl, :]) + _dot(x_hi, w3_b[dl:, :])
        h = (_silu(h1) * h3).astype(BF16)
        ys_ref[...] = _pack_halves(_dot(h, w2_b[...]))

    @pl.when(jnp.logical_not(used))
    def _():
        ys_ref[...] = jnp.zeros_like(ys_ref)


def moe_ffn(xs, blk_exp, new_exp, nblk, w1, w3, w2, *, layer, bm):
    n_pad, dp = xs.shape
    d = 2 * dp
    de = w1.shape[3]
    n_blk = n_pad // bm

    def x_map(i, be, ne, nb):
        return (jnp.minimum(i, nb[0] - 1), 0)

    def w_map(i, be, ne, nb):
        return (layer, be[i], 0, 0)

    return pl.pallas_call(
        _ffn_kernel,
        grid_spec=pltpu.PrefetchScalarGridSpec(
            num_scalar_prefetch=3,
            grid=(n_blk,),
            in_specs=[pl.BlockSpec((bm, dp), x_map),
                      pl.BlockSpec((1, 1, d, de), w_map),
                      pl.BlockSpec((1, 1, d, de), w_map),
                      pl.BlockSpec((1, 1, de, d), w_map)],
            out_specs=pl.BlockSpec((bm, dp), lambda i, be, ne, nb: (i, 0)),
            scratch_shapes=[pltpu.VMEM((d, de), BF16), pltpu.VMEM((d, de), BF16),
                            pltpu.VMEM((de, d), BF16)]),
        out_shape=jax.ShapeDtypeStruct((n_pad, dp), jnp.int32),
        compiler_params=_cparams(("arbitrary",)),
        name="moe_ffn",
    )(blk_exp, new_exp, nblk, xs, w1, w3, w2)


def _combine_kernel(x_ref, g0_ref, g1_ref, route_ref, ln_g_ref, ln_b_ref, *rest):
    o_ref = rest[-1]
    o_ref[...] = _moe_output(x_ref[...], g0_ref[0], g1_ref[0], route_ref[...], ln_g_ref[...], ln_b_ref[...])


def moe_combine(x, g, route, ln_g, ln_b, *, row0, tb, prev=None):
    n, d = x.shape
    m = g.shape[1]
    blk0 = row0 // tb
    row_spec = pl.BlockSpec((tb, d), lambda i: (blk0 + i, 0))
    args = [x, g, g, route, ln_g.reshape(1, -1), ln_b.reshape(1, -1)]
    in_specs = [row_spec,
                pl.BlockSpec((1, tb, d // 2), lambda i: (0, i, 0)),
                pl.BlockSpec((1, tb, d // 2), lambda i: (1, i, 0)),
                pl.BlockSpec((tb, ROUTE_W), lambda i: (blk0 + i, 0)),
                _full((1, d)), _full((1, d))]
    aliases = {}
    if prev is not None:
        args.append(prev)
        in_specs.append(pl.BlockSpec(memory_space=pl.ANY))
        aliases = {len(args) - 1: 0}
    return pl.pallas_call(
        _combine_kernel,
        grid=(m // tb,),
        in_specs=in_specs,
        out_specs=row_spec,
        out_shape=jax.ShapeDtypeStruct((n, d), F32),
        input_output_aliases=aliases,
        compiler_params=_cparams(("arbitrary",)),
        name="moe_combine",
    )(*args)


MOE_BM = 1024
COMBINE_PARTS = 2
MIXER_TS = 1024


def hierarchical_moe_layer(x2, xp, route, counts, w1, w3, w2, ln_g, ln_b, *, layer, defer_combine):
    bsz, seq, d = x2.shape
    n = bsz * seq
    bm = MOE_BM
    rt = route.reshape(n, ROUTE_W)
    tb = min(512, n)
    n_blk = (2 * n) // bm + N_EXPERTS
    cnt = counts.sum(axis=0)[0, :N_EXPERTS].astype(jnp.int32)
    pcnt = (cnt + bm - 1) // bm * bm
    pends = jnp.cumsum(pcnt)
    pstart = pends - pcnt
    nblk = (pends[-1] // bm).astype(jnp.int32).reshape(1)
    blk_row = jnp.arange(n_blk, dtype=jnp.int32) * bm
    blk_exp = jnp.minimum(jnp.sum((pends[None, :] <= blk_row[:, None]).astype(jnp.int32), axis=1),
                          N_EXPERTS - 1)
    last_exp = blk_exp[jnp.maximum(nblk[0] - 1, 0)]
    blk_exp = jnp.where(jnp.arange(n_blk) < nblk[0], blk_exp, last_exp)
    new_exp = jnp.concatenate([jnp.ones((1,), jnp.int32),
                               (blk_exp[1:] != blk_exp[:-1]).astype(jnp.int32)])
    pstart_rec = jnp.zeros((1, ROUTE_W), F32).at[0, :N_EXPERTS].set(pstart.astype(F32))
    dest = moe_slots(rt, pstart_rec, tb=tb)
    xs = moe_dispatch(xp.reshape(n, d // 2), dest[0], dest[1], n_blk * bm)
    ys = moe_ffn(xs, blk_exp, new_exp, nblk, w1, w3, w2, layer=layer, bm=bm)
    if defer_combine:
        return x2, moe_gather(ys, dest[0], dest[1]), route, ln_g, ln_b
    parts = COMBINE_PARTS if n % (COMBINE_PARTS * tb * 8) == 0 else 1
    m = n // parts
    out = None
    for p in range(parts):
        g = moe_gather(ys, dest[0, p * m:(p + 1) * m], dest[1, p * m:(p + 1) * m])
        out = moe_combine(x2.reshape(n, d), g, rt, ln_g, ln_b, row0=p * m, tb=tb, prev=out)
    return out.reshape(bsz, seq, d)


SC_ROWS = 128


def _sc_mesh():
    return plsc.VectorSubcoreMesh(core_axis_name="c", subcore_axis_name="s")


def moe_dispatch(xf, dest0, dest1, n_pad):
    n, d = xf.shape
    info = plsc.get_sparse_core_info()
    nw = info.num_cores * info.num_subcores
    per_w = n // nw
    r = min(SC_ROWS, per_w)

    def body(x_hbm, d0_hbm, d1_hbm, xs_hbm, i0_v, i1_v, rows_v, sem):
        wid = lax.axis_index("s") * info.num_cores + lax.axis_index("c")

        @pl.loop(0, per_w // r)
        def _(c):
            base = pl.multiple_of(wid * per_w + c * r, 8)
            pltpu.sync_copy(d0_hbm.at[pl.ds(base, r)], i0_v)
            pltpu.sync_copy(d1_hbm.at[pl.ds(base, r)], i1_v)
            pltpu.sync_copy(x_hbm.at[pl.ds(base, r)], rows_v)
            pltpu.async_copy(rows_v, xs_hbm.at[i0_v], sem).wait()
            pltpu.async_copy(rows_v, xs_hbm.at[i1_v], sem).wait()

    return pl.kernel(
        body, out_type=jax.ShapeDtypeStruct((n_pad, d), xf.dtype), mesh=_sc_mesh(),
        scratch_types=[pltpu.VMEM((r,), jnp.int32), pltpu.VMEM((r,), jnp.int32),
                       pltpu.VMEM((r, d), xf.dtype), pltpu.SemaphoreType.DMA],
        name="moe_dispatch",
    )(xf, dest0, dest1)


def moe_gather(ys, dest0, dest1):
    n = dest0.shape[0]
    d = ys.shape[1]
    info = plsc.get_sparse_core_info()
    nw = info.num_cores * info.num_subcores
    per_w = n // nw
    r = min(SC_ROWS, per_w)

    def body(ys_hbm, d0_hbm, d1_hbm, g_hbm, i_v, rows_v, sem):
        wid = lax.axis_index("s") * info.num_cores + lax.axis_index("c")

        @pl.loop(0, per_w // r)
        def _(c):
            base = pl.multiple_of(wid * per_w + c * r, 8)
            for k, d_hbm in enumerate((d0_hbm, d1_hbm)):
                pltpu.sync_copy(d_hbm.at[pl.ds(base, r)], i_v)
                pltpu.async_copy(ys_hbm.at[i_v], rows_v, sem).wait()
                pltpu.sync_copy(rows_v, g_hbm.at[k, pl.ds(base, r)])

    return pl.kernel(
        body, out_type=jax.ShapeDtypeStruct((2, n, d), ys.dtype), mesh=_sc_mesh(),
        scratch_types=[pltpu.VMEM((r,), jnp.int32), pltpu.VMEM((r, d), ys.dtype),
                       pltpu.SemaphoreType.DMA],
        name="moe_gather",
    )(ys, dest0, dest1)


def kernel(x, mem, ln_g, ln_b, ev_w_in, ev_gm_ln_g, ev_gm_ln_b, ev_gm_ws, ev_gm_bs, ev_conv_w, ev_conv_b, ev_wq, ev_wk, ev_wv, ev_w_if, ev_b_if, ev_norm_w, ev_skip, ev_w_out, od_w_qkv, od_b_qkv, od_sinks, od_w_o, xa_wq, xa_wkv, xa_wo, moe_w_rg, moe_b_rg, moe_w_re, moe_b_re, moe_w1, moe_w3, moe_w2):
    bsz, seq, d = x.shape
    depth = ln_g.shape[0]
    assert depth == DEPTH
    ts = min(MIXER_TS, seq)
    for l in range(depth):
        if l % 2 == 0:
            e = l // 2
            x = even_mixer_layer(x, ev_w_in[e], ev_gm_ln_g[e], ev_gm_ln_b[e], ev_gm_ws[e], ev_gm_bs[e],
                                 ev_conv_w[e], ev_conv_b[e], ev_wq[e], ev_wk[e], ev_wv[e], ev_w_if[e],
                                 ev_b_if[e], ev_norm_w[e], ev_skip[e], ev_w_out[e],
                                 ln_g[l, 0], ln_b[l, 0], ts=ts)
        else:
            o = l // 2
            x = swa_mixer_layer(x, od_w_qkv[o], od_b_qkv[o], od_sinks[o], od_w_o[o],
                                ln_g[l, 0], ln_b[l, 0], ts=ts)
        wqk, vo = memory_fold(mem, xa_wkv[l], xa_wq[l], xa_wo[l])
        x, xp, route, counts = xattn_router_layer(x, wqk, vo, ln_g[l, 1], ln_b[l, 1],
                                                  moe_w_rg[l], moe_b_rg[l], moe_w_re[l], moe_b_re[l],
                                                  ts=min(XATTN_TS, seq))
        x = hierarchical_moe_layer(x, xp, route, counts, moe_w1, moe_w3, moe_w2,
                                   ln_g[l, 2], ln_b[l, 2], layer=l,
                                   defer_combine=(l + 1 < depth and (l + 1) % 2 == 1))
    return x
```

```python
import functools
import math

import jax
import jax.numpy as jnp
from jax import lax
from jax.experimental import pallas as pl
from jax.experimental.pallas import tpu as pltpu
from jax.experimental.pallas import tpu_sc as plsc

F32 = jnp.float32
BF16 = jnp.bfloat16

A_GROUPS = 4
CHUNK = 128
B_HEADS = 4
B_CONV = 4
C_HEAD_DIM = 64
C_KV_HEADS = 4
X_HEADS = 4
N_GROUPS = 4
EXPERTS_PER_GROUP = 8
N_EXPERTS = N_GROUPS * EXPERTS_PER_GROUP
ROPE_THETA = 10000.0
LN_EPS = 1e-5
DEPTH = 2
DN_ALPHA = (2 * DEPTH) ** 0.25

LANES = 128
VMEM_LIMIT = 48 * 1024 * 1024
NEG = -1e30


def _cparams(sem):
    return pltpu.CompilerParams(dimension_semantics=sem, vmem_limit_bytes=VMEM_LIMIT)


def _full(shape):
    nd = len(shape)
    return pl.BlockSpec(shape, lambda *_: (0,) * nd)


def _dot(a, b):
    return jnp.dot(a, b, preferred_element_type=F32)


def _dot_nt(a, b):
    return lax.dot_general(a, b, (((1,), (1,)), ((), ())), preferred_element_type=F32)


def _split_dot(a, b_bf16):
    hi = a.astype(BF16)
    lo = (a - hi.astype(F32)).astype(BF16)
    return _dot(hi, b_bf16) + _dot(lo, b_bf16)


def _ln(x, g, b):
    mu = jnp.mean(x, axis=-1, keepdims=True)
    xc = x - mu
    var = jnp.mean(xc * xc, axis=-1, keepdims=True)
    return xc * lax.rsqrt(var + LN_EPS) * g + b


LOG2E = math.log2(math.e)


def _silu(x):
    return x * (1.0 / (1.0 + jnp.exp2(x * -LOG2E)))


def _gelu(x):
    return 0.5 * x * (1.0 + jnp.tanh(math.sqrt(2.0 / math.pi) * (x + 0.044715 * (x * x * x))))


def _pack_halves(x):
    c = x.shape[1] // 2
    lo = lax.bitcast_convert_type(x[:, :c].astype(BF16).astype(F32), jnp.uint32)
    hi = lax.bitcast_convert_type(x[:, c:].astype(BF16).astype(F32), jnp.uint32)
    return lax.bitcast_convert_type((lo >> 16) | hi, jnp.int32)


def _unpack_halves(p):
    u = lax.bitcast_convert_type(p, jnp.uint32)
    lo = lax.bitcast_convert_type(u << 16, F32)
    hi = lax.bitcast_convert_type(u & jnp.uint32(0xFFFF0000), F32)
    return lo, hi


def _log_sigmoid(x):
    return jnp.minimum(x, 0.0) - jnp.log(1.0 + jnp.exp(-jnp.abs(x)))


def _even_kernel(x_ref, w_in_ref, gm_g_ref, gm_b_ref, gm_w_ref, gm_bias_ref,
                 conv_w_ref, conv_b_ref, wq_ref, wk_ref, wv_ref, wif_t_ref, bif_t_ref,
                 norm_w_ref, skip_ref, w_out_ref, ln_g_ref, ln_b_ref,
                 o_ref,
                 xm_buf, ct_ref, m_ref, *, ts, aw, bw):
    dh = bw // B_HEADS
    agd = aw // A_GROUPS
    nck = ts // CHUNK
    pad = 8
    j = pl.program_id(1)

    @pl.when(j == 0)
    def _():
        xm_buf[0:pad, :] = jnp.zeros((pad, bw), F32)
        ct_ref[...] = jnp.zeros_like(ct_ref)
        m_ref[...] = jnp.zeros_like(m_ref)

    row = lax.broadcasted_iota(jnp.int32, (CHUNK, CHUNK), 0)
    col = lax.broadcasted_iota(jnp.int32, (CHUNK, CHUNK), 1)
    causal = col <= row
    diag = col == row
    triu = jnp.where(row <= col, 1.0, 0.0).astype(BF16)
    ones_blk = jnp.ones((CHUNK, LANES), BF16)

    x = x_ref[0]
    proj = _dot(x.astype(BF16), w_in_ref[...])
    a_u = _gelu(proj[:, :aw])
    a_v = _gelu(proj[:, aw:2 * aw])
    xm = proj[:, 2 * aw:2 * aw + bw]
    z = proj[:, 2 * aw + bw:]

    vn = _ln(a_v, gm_g_ref[...], gm_b_ref[...]).astype(BF16)
    ya_chunks = []
    for c in range(nck):
        cols = []
        for g in range(A_GROUPS):
            v_cg = vn[c * CHUNK:(c + 1) * CHUNK, g * agd:(g + 1) * agd]
            cols.append(_dot(gm_w_ref[g], v_cg))
        ya_chunks.append(jnp.concatenate(cols, axis=1) + gm_bias_ref[...])
    y_a = a_u * jnp.concatenate(ya_chunks, axis=0)

    xm_buf[pad:pad + ts, :] = xm
    conv = conv_b_ref[...] + conv_w_ref[B_CONV - 1:B_CONV, :] * xm
    for k in range(B_CONV - 1):
        sh = B_CONV - 1 - k
        conv = conv + conv_w_ref[k:k + 1, :] * xm_buf[pad - sh:pad - sh + ts, :]
    xm_buf[pad - (B_CONV - 1):pad, :] = xm_buf[pad + ts - (B_CONV - 1):pad + ts, :]
    xc = _silu(conv)
    xc_b = xc.astype(BF16)
    q = _dot_head_pairs(xc_b, wq_ref)
    k_ = _dot_head_pairs(xc_b, wk_ref)
    v = _dot_head_pairs(xm.astype(BF16), wv_ref)
    gate_in = jnp.concatenate([q, k_, v], axis=1).astype(BF16)
    gates_t = _dot_nt(wif_t_ref[...], gate_in) + bif_t_ref[...]
    ig_all = gates_t[:B_HEADS, :]
    lf_all = _log_sigmoid(gates_t[B_HEADS:, :])
    q_b = q.astype(BF16)
    k_b = (k_ * dh ** -0.5).astype(BF16)
    v_b = v.astype(BF16)
    gate_z = _silu(z)

    lf_c = [lf_all[:, c * CHUNK:(c + 1) * CHUNK] for c in range(nck)]
    b_all = _split_dot(jnp.concatenate(lf_c, axis=0), triu)
    m_prev = m_ref[...]
    a_c, m_c, decay_c, wg_c = [], [], [], []
    for c in range(nck):
        b_r = b_all[c * B_HEADS:(c + 1) * B_HEADS]
        a_r = ig_all[:, c * CHUNK:(c + 1) * CHUNK] - b_r
        bl = b_r[:, CHUNK - 1:CHUNK]
        g_r = bl + a_r
        m_new = jnp.maximum(bl + m_prev, jnp.max(g_r, axis=1, keepdims=True))
        a_c.append(a_r)
        m_c.append(m_prev)
        decay_c.append(jnp.exp(bl + m_prev - m_new))
        wg_c.append(jnp.exp(g_r - m_new))
        m_prev = m_new
    m_ref[...] = m_prev

    units = [(c, h) for c in range(nck) for h in range(B_HEADS)]
    lmat = jnp.concatenate([jnp.where(causal, lf_c[c][h:h + 1, :], 0.0) for c, h in units], axis=0)
    dgm = jnp.concatenate([jnp.where(diag, wg_c[c][h:h + 1, :], 0.0) for c, h in units], axis=0)
    b_t_all = _split_dot(lmat, ones_blk)
    wg_t_all = _dot(dgm.astype(BF16), ones_blk)

    caugs = [ct_ref[h] for h in range(B_HEADS)]
    h_chunks = []
    for c in range(nck):
        sl = slice(c * CHUNK, (c + 1) * CHUNK)
        heads = []
        for h in range(B_HEADS):
            u = c * B_HEADS + h
            us = slice(u * CHUNK, (u + 1) * CHUNK)
            hs = slice(h * dh, (h + 1) * dh)
            qh, kh = q_b[sl, hs], k_b[sl, hs]
            vaug = jnp.concatenate([v_b[sl, hs], ones_blk], axis=1)
            m_row = m_c[c][h:h + 1, :]
            amat = jnp.where(causal, a_c[c][h:h + 1, :], NEG)
            mx = jnp.maximum(jnp.max(amat, axis=1, keepdims=True), m_row)
            w_intra = jnp.exp(amat - mx)
            w_state = jnp.exp(m_row - mx)
            s = _dot_nt(qh, kh) * w_intra
            kw = (kh.astype(F32) * wg_t_all[us]).astype(BF16)
            both = _dot(jnp.concatenate([s.astype(BF16), kw.T], axis=0), vaug)
            caug = caugs[h]
            naug = both[:CHUNK] + jnp.concatenate([w_state, w_state], axis=1) * _dot(qh, caug.astype(BF16))
            num, nq = naug[:, :dh], naug[:, dh:]
            hv = num / jnp.maximum(jnp.abs(nq), jnp.exp(-(b_t_all[us] + mx)))
            decay = decay_c[c][h:h + 1, :]
            caugs[h] = jnp.concatenate([decay, decay], axis=1) * caug + both[CHUNK:]
            hc = hv - jnp.mean(hv, axis=1, keepdims=True)
            hn = hc * lax.rsqrt(jnp.mean(hc * hc, axis=1, keepdims=True) + LN_EPS)
            heads.append(hn)
        h_chunks.append(jnp.concatenate(heads, axis=1))
    for h in range(B_HEADS):
        ct_ref[h] = caugs[h]
    hn_all = jnp.concatenate(h_chunks, axis=0) if nck > 1 else h_chunks[0]
    y_b = (hn_all * norm_w_ref[...] + skip_ref[...] * xc) * gate_z

    mix = jnp.concatenate([y_a, y_b], axis=1).astype(BF16)
    y = _dot(mix, w_out_ref[...])
    o_ref[0] = _ln(DN_ALPHA * x + y, ln_g_ref[...], ln_b_ref[...])


def _block_diag_pairs(w):
    hh, d, _ = w.shape
    wp = w.reshape(hh // 2, 2, d, d)
    eye = jnp.eye(2, dtype=w.dtype)
    return jnp.einsum('pade,ab->padbe', wp, eye).reshape(hh // 2, 2 * d, 2 * d)


def _dot_head_pairs(x, w_ref):
    npair, w2, _ = w_ref.shape
    return jnp.concatenate([_dot(x[:, p * w2:(p + 1) * w2], w_ref[p]) for p in range(npair)], axis=1)


def even_mixer_layer(x, w_in, gm_ln_g, gm_ln_b, gm_ws, gm_bs, conv_w, conv_b, wq, wk, wv, w_if,
                     b_if, norm_w, skip, w_out, ln_g, ln_b, *, ts):
    bsz, seq, d = x.shape
    aw = gm_ln_g.shape[0]
    bw = conv_b.shape[0]
    agd = aw // A_GROUPS
    causal = jnp.tril(jnp.ones((CHUNK, CHUNK), dtype=bool))
    gm_w = jnp.where(causal[None], gm_ws, 0.0).astype(BF16)
    gm_bias = jnp.repeat(gm_bs.T, agd, axis=1)
    row = lambda a: a.reshape(1, -1)
    args = (x, w_in.astype(BF16), row(gm_ln_g), row(gm_ln_b), gm_w, gm_bias,
            conv_w, row(conv_b), _block_diag_pairs(wq).astype(BF16), _block_diag_pairs(wk).astype(BF16),
            _block_diag_pairs(wv).astype(BF16), w_if.T.astype(BF16),
            b_if.reshape(-1, 1), row(norm_w), row(skip), w_out.astype(BF16),
            row(ln_g), row(ln_b))
    in_specs = [pl.BlockSpec((1, ts, d), lambda b, j: (b, j, 0))] + [_full(a.shape) for a in args[1:]]
    dh = bw // B_HEADS
    assert dh == LANES and CHUNK == LANES
    return pl.pallas_call(
        functools.partial(_even_kernel, ts=ts, aw=aw, bw=bw),
        grid=(bsz, seq // ts),
        in_specs=in_specs,
        out_specs=pl.BlockSpec((1, ts, d), lambda b, j: (b, j, 0)),
        out_shape=jax.ShapeDtypeStruct((bsz, seq, d), F32),
        scratch_shapes=[pltpu.VMEM((8 + ts, bw), F32),
                        pltpu.VMEM((B_HEADS, dh, dh + LANES), F32),
                        pltpu.VMEM((B_HEADS, LANES), F32)],
        compiler_params=_cparams(("arbitrary", "arbitrary")),
        name="even_mixer",
    )(*args)


def _memfold_kernel(mem_ref, wkv_ref, wq_ref, wo_ref, wqk_ref, vo_ref, *, d):
    dh = d // X_HEADS
    m_len = mem_ref.shape[1]
    kv = _dot(mem_ref[0].astype(BF16), wkv_ref[...])
    k = (kv[:, :d] * (dh ** -0.5 * LOG2E)).astype(BF16)
    v = kv[:, d:].astype(BF16)
    for h in range(X_HEADS):
        hs = slice(h * dh, (h + 1) * dh)
        ms = slice(h * m_len, (h + 1) * m_len)
        wqk_ref[0, :, ms] = _dot_nt(wq_ref[:, hs], k[:, hs]).astype(BF16)
        vo_ref[0, ms, :] = _dot(v[:, hs], wo_ref[hs, :]).astype(BF16)


def memory_fold(mem, wkv, wq, wo):
    bsz, m_len, d = mem.shape
    hm = X_HEADS * m_len
    args = (mem, wkv.astype(BF16), wq.astype(BF16), wo.astype(BF16))
    return pl.pallas_call(
        functools.partial(_memfold_kernel, d=d),
        grid=(bsz,),
        in_specs=[pl.BlockSpec((1, m_len, d), lambda b: (b, 0, 0))] + [_full(a.shape) for a in args[1:]],
        out_specs=[pl.BlockSpec((1, d, hm), lambda b: (b, 0, 0)),
                   pl.BlockSpec((1, hm, d), lambda b: (b, 0, 0))],
        out_shape=[jax.ShapeDtypeStruct((bsz, d, hm), BF16),
                   jax.ShapeDtypeStruct((bsz, hm, d), BF16)],
        compiler_params=_cparams(("arbitrary",)),
        name="memory_fold",
    )(*args)


ROUTE_W = 128
XATTN_TS = 1024
DEST_ROWS = 8


def _xattn_kernel(x_ref, wqk_ref, vo_ref, ln_g_ref, ln_b_ref, wr_ref, br_ref,
                  o_ref, xp_ref, route_ref, cnt_ref):
    m_len = wqk_ref.shape[2] // X_HEADS

    @pl.when(pl.program_id(1) == 0)
    def _():
        cnt_ref[...] = jnp.zeros_like(cnt_ref)

    x = x_ref[0]
    ts = x.shape[0]
    xb = x.astype(BF16)
    probs = []
    for h in range(X_HEADS):
        s = _dot(xb, wqk_ref[0, :, h * m_len:(h + 1) * m_len])
        p = jnp.exp2(s - jnp.max(s, axis=1, keepdims=True))
        probs.append((p / jnp.sum(p, axis=1, keepdims=True)).astype(BF16))
    y = _dot(jnp.concatenate(probs, axis=1), vo_ref[0])
    x2 = _ln(DN_ALPHA * x + y, ln_g_ref[...], ln_b_ref[...])
    o_ref[0] = x2
    xp_ref[0] = _pack_halves(x2)

    x_hi = x2.astype(BF16)
    x_lo = (x2 - x_hi.astype(F32)).astype(BF16)
    logits = (_dot(x_hi, wr_ref[0]) + _dot(x_lo, wr_ref[0]) + _dot(x_hi, wr_ref[1])) + br_ref[...]
    lane = lax.broadcasted_iota(jnp.int32, (ts, ROUTE_W), 1)
    is_g = lane < N_GROUPS
    lg = jnp.where(is_g, logits, NEG)
    mg = jnp.max(lg, axis=1, keepdims=True)
    gi = jnp.min(jnp.where(jnp.logical_and(is_g, lg == mg), lane, ROUTE_W), axis=1, keepdims=True)
    gate_g = 1.0 / jnp.sum(jnp.where(is_g, jnp.exp(lg - mg), 0.0), axis=1, keepdims=True)
    lo = N_GROUPS + gi * EXPERTS_PER_GROUP
    in_grp = jnp.logical_and(lane >= lo, lane < lo + EXPERTS_PER_GROUP)
    le = jnp.where(in_grp, logits, NEG)
    v1 = jnp.max(le, axis=1, keepdims=True)
    i1 = jnp.min(jnp.where(jnp.logical_and(in_grp, le == v1), lane, ROUTE_W), axis=1, keepdims=True)
    le2 = jnp.where(lane == i1, NEG, le)
    v2 = jnp.max(le2, axis=1, keepdims=True)
    i2 = jnp.min(jnp.where(jnp.logical_and(in_grp, le2 == v2), lane, ROUTE_W), axis=1, keepdims=True)
    e21 = jnp.exp(v2 - v1)
    p1 = 1.0 / (1.0 + e21)
    p2 = e21 * p1
    e1 = (i1 - N_GROUPS).astype(F32)
    e2 = (i2 - N_GROUPS).astype(F32)
    rec = jnp.where(lane == 0, e1, 0.0)
    rec = jnp.where(lane == 1, e2, rec)
    rec = jnp.where(lane == 2, gate_g * p1, rec)
    rec = jnp.where(lane == 3, gate_g * p2, rec)
    route_ref[0] = rec
    sel = jnp.logical_or(lane == i1 - N_GROUPS, lane == i2 - N_GROUPS)
    cnt_ref[0] += jnp.sum(jnp.where(sel, 1.0, 0.0), axis=0, keepdims=True)


def xattn_router_layer(x, wqk, vo, ln_g, ln_b, w_rg, b_rg, w_re, b_re, *, ts):
    bsz, seq, d = x.shape
    hm = wqk.shape[2]
    wr = jnp.zeros((d, ROUTE_W), F32).at[:, :N_GROUPS].set(w_rg).at[:, N_GROUPS:N_GROUPS + N_EXPERTS].set(w_re)
    br = jnp.zeros((1, ROUTE_W), F32).at[0, :N_GROUPS].set(b_rg).at[0, N_GROUPS:N_GROUPS + N_EXPERTS].set(b_re)
    wr_hi = wr.astype(BF16)
    wr_lo = (wr - wr_hi.astype(F32)).astype(BF16)
    wr = jnp.stack([wr_hi, wr_lo])
    args = (x, wqk, vo, ln_g.reshape(1, -1), ln_b.reshape(1, -1), wr, br)
    in_specs = [pl.BlockSpec((1, ts, d), lambda b, j: (b, j, 0)),
                pl.BlockSpec((1, d, hm), lambda b, j: (b, 0, 0)),
                pl.BlockSpec((1, hm, d), lambda b, j: (b, 0, 0))] + [_full(a.shape) for a in args[3:]]
    return pl.pallas_call(
        _xattn_kernel,
        grid=(bsz, seq // ts),
        in_specs=in_specs,
        out_specs=[pl.BlockSpec((1, ts, d), lambda b, j: (b, j, 0)),
                   pl.BlockSpec((1, ts, d // 2), lambda b, j: (b, j, 0)),
                   pl.BlockSpec((1, ts, ROUTE_W), lambda b, j: (b, j, 0)),
                   pl.BlockSpec((1, 1, ROUTE_W), lambda b, j: (b, 0, 0))],
        out_shape=[jax.ShapeDtypeStruct((bsz, seq, d), F32),
                   jax.ShapeDtypeStruct((bsz, seq, d // 2), jnp.int32),
                   jax.ShapeDtypeStruct((bsz, seq, ROUTE_W), F32),
                   jax.ShapeDtypeStruct((bsz, 1, ROUTE_W), F32)],
        compiler_params=_cparams(("arbitrary", "arbitrary")),
        name="xattn_router",
    )(*args)


def _moe_output(x2, g0, g1, rec, ln_g, ln_b):
    a_lo, a_hi = _unpack_halves(g0)
    b_lo, b_hi = _unpack_halves(g1)
    w0, w1 = rec[:, 2:3], rec[:, 3:4]
    y = jnp.concatenate([w0 * a_lo + w1 * b_lo, w0 * a_hi + w1 * b_hi], axis=1)
    return _ln(DN_ALPHA * x2 + y, ln_g, ln_b)


def _swa_kernel(x_ref, g0_ref, g1_ref, route_ref, pln_g_ref, pln_b_ref,
                wqkv_ref, bqkv_ref, cos_ref, sin_ref, sink_ref, wo_ref, ln_g_ref, ln_b_ref,
                o_ref, kprev, vprev, *, ts, cq, ckv):
    j = pl.program_id(1)
    nb = ts // CHUNK
    dh = C_HEAD_DIM
    grp = (cq // dh) // C_KV_HEADS

    @pl.when(j == 0)
    def _():
        kprev[...] = jnp.zeros_like(kprev)
        vprev[...] = jnp.zeros_like(vprev)

    x = _moe_output(x_ref[0], g0_ref[0, 0], g1_ref[0, 0], route_ref[0], pln_g_ref[...], pln_b_ref[...])
    qkv = _dot(x.astype(BF16), wqkv_ref[...]) + bqkv_ref[...]
    cos = cos_ref[...]
    sin = sin_ref[...]
    lane = lax.broadcasted_iota(jnp.int32, (ts, LANES), 1)
    first_half = (lane % dh) < (dh // 2)

    def rope(t):
        outs = []
        for c in range(t.shape[1] // LANES):
            tc = t[:, c * LANES:(c + 1) * LANES]
            rot = jnp.where(first_half, pltpu.roll(tc, LANES - dh // 2, 1), pltpu.roll(tc, dh // 2, 1))
            outs.append(tc * cos + rot * sin)
        return jnp.concatenate(outs, axis=1)

    q = rope(qkv[:, :cq]).astype(BF16)
    k = rope(qkv[:, cq:cq + 2 * ckv]).astype(BF16)
    v = qkv[:, cq + 2 * ckv:].astype(BF16)

    r_i = lax.broadcasted_iota(jnp.int32, (CHUNK, 2 * CHUNK), 0)
    c_i = lax.broadcasted_iota(jnp.int32, (CHUNK, 2 * CHUNK), 1)
    band = jnp.logical_and(c_i > r_i, c_i <= r_i + CHUNK)
    sink_col = c_i == 0
    lane_k = lax.broadcasted_iota(jnp.int32, (2 * CHUNK, LANES), 1)
    key_row = lax.broadcasted_iota(jnp.int32, (2 * CHUNK, LANES), 0)
    lane_q = lax.broadcasted_iota(jnp.int32, (CHUNK, LANES), 1)
    ones_blk = jnp.ones((2 * CHUNK, LANES), BF16)
    zero_b = jnp.zeros((), BF16)
    blocks = []
    for c in range(nb):
        sl = slice(c * CHUNK, (c + 1) * CHUNK)
        if c == 0:
            kb = jnp.concatenate([kprev[...].astype(BF16), k[sl]], axis=0)
            vb = jnp.concatenate([vprev[...].astype(BF16), v[sl]], axis=0)
            first_key = jnp.where(j > 0, 0, CHUNK)
            valid = jnp.logical_and(band, c_i >= first_key)
        else:
            kb = k[(c - 1) * CHUNK:(c + 1) * CHUNK]
            vb = v[(c - 1) * CHUNK:(c + 1) * CHUNK]
            valid = band
        tiles = []
        for h in range(C_KV_HEADS):
            kd = kb[:, h * LANES:(h + 1) * LANES]
            vd = vb[:, h * LANES:(h + 1) * LANES]
            k_lo = jnp.where(lane_k < dh, kd, zero_b)
            k_hi = jnp.where(lane_k >= dh, kd, zero_b)
            vz = jnp.where(key_row == 0, zero_b, vd)
            q2 = jnp.concatenate([q[sl, (2 * h) * LANES:(2 * h + 1) * LANES],
                                  q[sl, (2 * h + 1) * LANES:(2 * h + 2) * LANES]], axis=0)
            s_lo = _dot_nt(q2, k_lo)
            s_hi = _dot_nt(q2, k_hi)
            parts = []
            for qk, g in ((s_lo[:CHUNK], 0), (s_lo[CHUNK:], 2), (s_hi[:CHUNK], 1), (s_hi[CHUNK:], 3)):
                sink = sink_ref[h * grp + g]
                parts.append(jnp.where(valid, qk, jnp.where(sink_col, sink, NEG)))
            s = jnp.concatenate(parts, axis=0)
            p = jnp.exp2(s - jnp.max(s, axis=1, keepdims=True)).astype(BF16)
            den = _dot(p, ones_blk)
            o2 = _dot(p, vz) / den
            tiles.append(jnp.where(lane_q < dh, o2[:CHUNK], o2[2 * CHUNK:3 * CHUNK]))
            tiles.append(jnp.where(lane_q < dh, o2[CHUNK:2 * CHUNK], o2[3 * CHUNK:]))
        blocks.append(jnp.concatenate(tiles, axis=1))
    kprev[...] = k[(nb - 1) * CHUNK:].astype(F32)
    vprev[...] = v[(nb - 1) * CHUNK:].astype(F32)
    att = (jnp.concatenate(blocks, axis=0) if nb > 1 else blocks[0]).astype(BF16)
    y = _dot(att, wo_ref[...])
    o_ref[0] = _ln(DN_ALPHA * x + y, ln_g_ref[...], ln_b_ref[...])


def swa_mixer_layer(pending, w_qkv, b_qkv, sinks, w_o, ln_g, ln_b, *, ts):
    x, g, route, pln_g, pln_b = pending
    bsz, seq, d = x.shape
    g = g.reshape(2, bsz, seq, d // 2)
    cq = w_o.shape[0]
    ckv = (w_qkv.shape[1] - cq) // 2
    dh = C_HEAD_DIM
    inv = ROPE_THETA ** (-jnp.arange(0, dh, 2, dtype=F32) / dh)
    ang = jnp.arange(seq, dtype=F32)[:, None] * inv[None, :]
    reps = LANES // (dh // 2)
    sign = jnp.tile(jnp.concatenate([-jnp.ones((dh // 2,), F32), jnp.ones((dh // 2,), F32)]), LANES // dh)
    cos_t = jnp.tile(jnp.cos(ang), (1, reps))
    sin_t = jnp.tile(jnp.sin(ang), (1, reps)) * sign[None, :]
    assert 2 * dh == LANES

    def dup_heads(t):
        th = t.reshape(t.shape[:-1] + (ckv // dh, dh))
        return jnp.concatenate([th, th], axis=-1).reshape(t.shape[:-1] + (2 * ckv,))

    qs = dh ** -0.5 * LOG2E
    w_all = jnp.concatenate([w_qkv[:, :cq] * qs, dup_heads(w_qkv[:, cq:cq + ckv]),
                             dup_heads(w_qkv[:, cq + ckv:])], axis=1)
    b_all = jnp.concatenate([b_qkv[:cq] * qs, dup_heads(b_qkv[cq:cq + ckv]), dup_heads(b_qkv[cq + ckv:])])
    args = (x, g, g, route, pln_g.reshape(1, -1), pln_b.reshape(1, -1),
            w_all.astype(BF16), b_all.reshape(1, -1), cos_t, sin_t, sinks.astype(F32) * LOG2E,
            w_o.astype(BF16), ln_g.reshape(1, -1), ln_b.reshape(1, -1))
    in_specs = [pl.BlockSpec((1, ts, d), lambda b, j: (b, j, 0)),
                pl.BlockSpec((1, 1, ts, d // 2), lambda b, j: (0, b, j, 0)),
                pl.BlockSpec((1, 1, ts, d // 2), lambda b, j: (1, b, j, 0)),
                pl.BlockSpec((1, ts, ROUTE_W), lambda b, j: (b, j, 0)),
                _full((1, d)), _full((1, d)),
                _full(args[6].shape), _full(args[7].shape),
                pl.BlockSpec((ts, LANES), lambda b, j: (j, 0)),
                pl.BlockSpec((ts, LANES), lambda b, j: (j, 0)),
                pl.BlockSpec(memory_space=pltpu.SMEM),
                _full(args[11].shape), _full(args[12].shape), _full(args[13].shape)]
    return pl.pallas_call(
        functools.partial(_swa_kernel, ts=ts, cq=cq, ckv=ckv),
        grid=(bsz, seq // ts),
        in_specs=in_specs,
        out_specs=pl.BlockSpec((1, ts, d), lambda b, j: (b, j, 0)),
        out_shape=jax.ShapeDtypeStruct((bsz, seq, d), F32),
        scratch_shapes=[pltpu.VMEM((CHUNK, 2 * ckv), F32), pltpu.VMEM((CHUNK, 2 * ckv), F32)],
        compiler_params=_cparams(("arbitrary", "arbitrary")),
        name="swa_mixer",
    )(*args)


def _slot_kernel(route_ref, pstart_ref, dest_ref, carry_ref, *, tb):
    @pl.when(pl.program_id(0) == 0)
    def _():
        carry_ref[...] = pstart_ref[...]

    rec = route_ref[...]
    lane = lax.broadcasted_iota(jnp.int32, (tb, ROUTE_W), 1)
    e0 = rec[:, 0:1].astype(jnp.int32)
    e1 = rec[:, 1:2].astype(jnp.int32)
    oh0 = lane == e0
    oh1 = lane == e1
    ohs = jnp.where(jnp.logical_or(oh0, oh1), 1.0, 0.0)
    r = lax.broadcasted_iota(jnp.int32, (tb, tb), 0)
    c = lax.broadcasted_iota(jnp.int32, (tb, tb), 1)
    before = jnp.where(c < r, 1.0, 0.0).astype(BF16)
    prefix = _dot(before, ohs.astype(BF16)) + carry_ref[...]
    d0 = jnp.sum(jnp.where(oh0, prefix, 0.0), axis=1, keepdims=True)
    d1 = jnp.sum(jnp.where(oh1, prefix, 0.0), axis=1, keepdims=True)
    dest = jnp.where(lane == 0, d0, jnp.where(lane == 1, d1, 0.0))
    dest_ref[...] = dest.T[:DEST_ROWS].astype(jnp.int32)
    carry_ref[...] += jnp.sum(ohs, axis=0, keepdims=True)


def moe_slots(route, pstart, *, tb):
    n = route.shape[0]
    return pl.pallas_call(
        functools.partial(_slot_kernel, tb=tb),
        grid=(n // tb,),
        in_specs=[pl.BlockSpec((tb, ROUTE_W), lambda i: (i, 0)), _full((1, ROUTE_W))],
        out_specs=pl.BlockSpec((DEST_ROWS, tb), lambda i: (0, i)),
        out_shape=jax.ShapeDtypeStruct((DEST_ROWS, n), jnp.int32),
        scratch_shapes=[pltpu.VMEM((1, ROUTE_W), F32)],
        compiler_params=_cparams(("arbitrary",)),
        name="moe_slots",
    )(route, pstart)


def _ffn_kernel(blk_exp_ref, new_exp_ref, nblk_ref, xs_ref, w1_ref, w3_ref, w2_ref, ys_ref,
                w1_b, w3_b, w2_b):
    i = pl.program_id(0)
    used = i < nblk_ref[0]

    @pl.when(jnp.logical_and(used, new_exp_ref[i] == 1))
    def _():
        w1_b[...] = w1_ref[0, 0].astype(BF16)
        w3_b[...] = w3_ref[0, 0].astype(BF16)
        w2_b[...] = w2_ref[0, 0].astype(BF16)

    @pl.when(used)
    def _():
        x_lo, x_hi = _unpack_halves(xs_ref[...])
        x_lo, x_hi = x_lo.astype(BF16), x_hi.astype(BF16)
        dl = x_lo.shape[1]
        h1 = _dot(x_lo, w1_b[:dl, :]) + _dot(x_hi, w1_b[dl:, :])
        h3 = _dot(x_lo, w3_b[:dl, :]) + _dot(x_hi, w3_b[dl:, :])
        h = (_silu(h1) * h3).astype(BF16)
        ys_ref[...] = _pack_halves(_dot(h, w2_b[...]))

    @pl.when(jnp.logical_not(used))
    def _():
        ys_ref[...] = jnp.zeros_like(ys_ref)


def moe_ffn(xs, blk_exp, new_exp, nblk, w1, w3, w2, *, layer, bm):
    n_pad, dp = xs.shape
    d = 2 * dp
    de = w1.shape[3]
    n_blk = n_pad // bm

    def x_map(i, be, ne, nb):
        return (jnp.minimum(i, nb[0] - 1), 0)

    def w_map(i, be, ne, nb):
        return (layer, be[i], 0, 0)

    return pl.pallas_call(
        _ffn_kernel,
        grid_spec=pltpu.PrefetchScalarGridSpec(
            num_scalar_prefetch=3,
            grid=(n_blk,),
            in_specs=[pl.BlockSpec((bm, dp), x_map),
                      pl.BlockSpec((1, 1, d, de), w_map),
                      pl.BlockSpec((1, 1, d, de), w_map),
                      pl.BlockSpec((1, 1, de, d), w_map)],
            out_specs=pl.BlockSpec((bm, dp), lambda i, be, ne, nb: (i, 0)),
            scratch_shapes=[pltpu.VMEM((d, de), BF16), pltpu.VMEM((d, de), BF16),
                            pltpu.VMEM((de, d), BF16)]),
        out_shape=jax.ShapeDtypeStruct((n_pad, dp), jnp.int32),
        compiler_params=_cparams(("arbitrary",)),
        name="moe_ffn",
    )(blk_exp, new_exp, nblk, xs, w1, w3, w2)


def _combine_kernel(x_ref, g0_ref, g1_ref, route_ref, ln_g_ref, ln_b_ref, *rest):
    o_ref = rest[-1]
    o_ref[...] = _moe_output(x_ref[...], g0_ref[0], g1_ref[0], route_ref[...], ln_g_ref[...], ln_b_ref[...])


def moe_combine(x, g, route, ln_g, ln_b, *, row0, tb, prev=None):
    n, d = x.shape
    m = g.shape[1]
    blk0 = row0 // tb
    row_spec = pl.BlockSpec((tb, d), lambda i: (blk0 + i, 0))
    args = [x, g, g, route, ln_g.reshape(1, -1), ln_b.reshape(1, -1)]
    in_specs = [row_spec,
                pl.BlockSpec((1, tb, d // 2), lambda i: (0, i, 0)),
                pl.BlockSpec((1, tb, d // 2), lambda i: (1, i, 0)),
                pl.BlockSpec((tb, ROUTE_W), lambda i: (blk0 + i, 0)),
                _full((1, d)), _full((1, d))]
    aliases = {}
    if prev is not None:
        args.append(prev)
        in_specs.append(pl.BlockSpec(memory_space=pl.ANY))
        aliases = {len(args) - 1: 0}
    return pl.pallas_call(
        _combine_kernel,
        grid=(m // tb,),
        in_specs=in_specs,
        out_specs=row_spec,
        out_shape=jax.ShapeDtypeStruct((n, d), F32),
        input_output_aliases=aliases,
        compiler_params=_cparams(("arbitrary",)),
        name="moe_combine",
    )(*args)


MOE_BM = 1024
COMBINE_PARTS = 4
MIXER_TS = 1024


def hierarchical_moe_layer(x2, xp, route, counts, w1, w3, w2, ln_g, ln_b, *, layer, defer_combine):
    bsz, seq, d = x2.shape
    n = bsz * seq
    bm = MOE_BM
    rt = route.reshape(n, ROUTE_W)
    tb = min(512, n)
    n_blk = (2 * n) // bm + N_EXPERTS
    cnt = counts.sum(axis=0)[0, :N_EXPERTS].astype(jnp.int32)
    pcnt = (cnt + bm - 1) // bm * bm
    pends = jnp.cumsum(pcnt)
    pstart = pends - pcnt
    nblk = (pends[-1] // bm).astype(jnp.int32).reshape(1)
    blk_row = jnp.arange(n_blk, dtype=jnp.int32) * bm
    blk_exp = jnp.minimum(jnp.sum((pends[None, :] <= blk_row[:, None]).astype(jnp.int32), axis=1),
                          N_EXPERTS - 1)
    last_exp = blk_exp[jnp.maximum(nblk[0] - 1, 0)]
    blk_exp = jnp.where(jnp.arange(n_blk) < nblk[0], blk_exp, last_exp)
    new_exp = jnp.concatenate([jnp.ones((1,), jnp.int32),
                               (blk_exp[1:] != blk_exp[:-1]).astype(jnp.int32)])
    pstart_rec = jnp.zeros((1, ROUTE_W), F32).at[0, :N_EXPERTS].set(pstart.astype(F32))
    dest = moe_slots(rt, pstart_rec, tb=tb)
    xs = moe_dispatch(xp.reshape(n, d // 2), dest[0], dest[1], n_blk * bm)
    ys = moe_ffn(xs, blk_exp, new_exp, nblk, w1, w3, w2, layer=layer, bm=bm)
    if defer_combine:
        return x2, moe_gather(ys, dest[0], dest[1]), route, ln_g, ln_b
    parts = COMBINE_PARTS if n % (COMBINE_PARTS * tb * 8) == 0 else 1
    m = n // parts
    out = None
    for p in range(parts):
        g = moe_gather(ys, dest[0, p * m:(p + 1) * m], dest[1, p * m:(p + 1) * m])
        out = moe_combine(x2.reshape(n, d), g, rt, ln_g, ln_b, row0=p * m, tb=tb, prev=out)
    return out.reshape(bsz, seq, d)


SC_ROWS = 128


def _sc_mesh():
    return plsc.VectorSubcoreMesh(core_axis_name="c", subcore_axis_name="s")


def moe_dispatch(xf, dest0, dest1, n_pad):
    n, d = xf.shape
    info = plsc.get_sparse_core_info()
    nw = info.num_cores * info.num_subcores
    per_w = n // nw
    r = min(SC_ROWS, per_w)

    def body(x_hbm, d0_hbm, d1_hbm, xs_hbm, i0_v, i1_v, rows_v, sem):
        wid = lax.axis_index("s") * info.num_cores + lax.axis_index("c")

        @pl.loop(0, per_w // r)
        def _(c):
            base = pl.multiple_of(wid * per_w + c * r, 8)
            pltpu.sync_copy(d0_hbm.at[pl.ds(base, r)], i0_v)
            pltpu.sync_copy(d1_hbm.at[pl.ds(base, r)], i1_v)
            pltpu.sync_copy(x_hbm.at[pl.ds(base, r)], rows_v)
            pltpu.async_copy(rows_v, xs_hbm.at[i0_v], sem).wait()
            pltpu.async_copy(rows_v, xs_hbm.at[i1_v], sem).wait()

    return pl.kernel(
        body, out_type=jax.ShapeDtypeStruct((n_pad, d), xf.dtype), mesh=_sc_mesh(),
        scratch_types=[pltpu.VMEM((r,), jnp.int32), pltpu.VMEM((r,), jnp.int32),
                       pltpu.VMEM((r, d), xf.dtype), pltpu.SemaphoreType.DMA],
        name="moe_dispatch",
    )(xf, dest0, dest1)


def moe_gather(ys, dest0, dest1):
    n = dest0.shape[0]
    d = ys.shape[1]
    info = plsc.get_sparse_core_info()
    nw = info.num_cores * info.num_subcores
    per_w = n // nw
    r = min(SC_ROWS, per_w)

    def body(ys_hbm, d0_hbm, d1_hbm, g_hbm, i_v, rows_v, sem):
        wid = lax.axis_index("s") * info.num_cores + lax.axis_index("c")

        @pl.loop(0, per_w // r)
        def _(c):
            base = pl.multiple_of(wid * per_w + c * r, 8)
            for k, d_hbm in enumerate((d0_hbm, d1_hbm)):
                pltpu.sync_copy(d_hbm.at[pl.ds(base, r)], i_v)
                pltpu.async_copy(ys_hbm.at[i_v], rows_v, sem).wait()
                pltpu.sync_copy(rows_v, g_hbm.at[k, pl.ds(base, r)])

    return pl.kernel(
        body, out_type=jax.ShapeDtypeStruct((2, n, d), ys.dtype), mesh=_sc_mesh(),
        scratch_types=[pltpu.VMEM((r,), jnp.int32), pltpu.VMEM((r, d), ys.dtype),
                       pltpu.SemaphoreType.DMA],
        name="moe_gather",
    )(ys, dest0, dest1)


def kernel(x, mem, ln_g, ln_b, ev_w_in, ev_gm_ln_g, ev_gm_ln_b, ev_gm_ws, ev_gm_bs, ev_conv_w, ev_conv_b, ev_wq, ev_wk, ev_wv, ev_w_if, ev_b_if, ev_norm_w, ev_skip, ev_w_out, od_w_qkv, od_b_qkv, od_sinks, od_w_o, xa_wq, xa_wkv, xa_wo, moe_w_rg, moe_b_rg, moe_w_re, moe_b_re, moe_w1, moe_w3, moe_w2):
    bsz, seq, d = x.shape
    depth = ln_g.shape[0]
    assert depth == DEPTH
    ts = min(MIXER_TS, seq)
    for l in range(depth):
        if l % 2 == 0:
            e = l // 2
            x = even_mixer_layer(x, ev_w_in[e], ev_gm_ln_g[e], ev_gm_ln_b[e], ev_gm_ws[e], ev_gm_bs[e],
                                 ev_conv_w[e], ev_conv_b[e], ev_wq[e], ev_wk[e], ev_wv[e], ev_w_if[e],
                                 ev_b_if[e], ev_norm_w[e], ev_skip[e], ev_w_out[e],
                                 ln_g[l, 0], ln_b[l, 0], ts=ts)
        else:
            o = l // 2
            x = swa_mixer_layer(x, od_w_qkv[o], od_b_qkv[o], od_sinks[o], od_w_o[o],
                                ln_g[l, 0], ln_b[l, 0], ts=ts)
        wqk, vo = memory_fold(mem, xa_wkv[l], xa_wq[l], xa_wo[l])
        x, xp, route, counts = xattn_router_layer(x, wqk, vo, ln_g[l, 1], ln_b[l, 1],
                                                  moe_w_rg[l], moe_b_rg[l], moe_w_re[l], moe_b_re[l],
                                                  ts=min(XATTN_TS, seq))
        x = hierarchical_moe_layer(x, xp, route, counts, moe_w1, moe_w3, moe_w2,
                                   ln_g[l, 2], ln_b[l, 2], layer=l,
                                   defer_combine=(l + 1 < depth and (l + 1) % 2 == 1))
    return x
```

```python
import functools
import math

import jax
import jax.numpy as jnp
from jax import lax
from jax.experimental import pallas as pl
from jax.experimental.pallas import tpu as pltpu
from jax.experimental.pallas import tpu_sc as plsc

F32 = jnp.float32
BF16 = jnp.bfloat16

A_GROUPS = 4
CHUNK = 128
B_HEADS = 4
B_CONV = 4
C_HEAD_DIM = 64
C_KV_HEADS = 4
X_HEADS = 4
N_GROUPS = 4
EXPERTS_PER_GROUP = 8
N_EXPERTS = N_GROUPS * EXPERTS_PER_GROUP
ROPE_THETA = 10000.0
LN_EPS = 1e-5
DEPTH = 2
DN_ALPHA = (2 * DEPTH) ** 0.25

LANES = 128
VMEM_LIMIT = 48 * 1024 * 1024
NEG = -1e30


def _cparams(sem):
    return pltpu.CompilerParams(dimension_semantics=sem, vmem_limit_bytes=VMEM_LIMIT)


def _full(shape):
    nd = len(shape)
    return pl.BlockSpec(shape, lambda *_: (0,) * nd)


def _dot(a, b):
    return jnp.dot(a, b, preferred_element_type=F32)


def _dot_nt(a, b):
    return lax.dot_general(a, b, (((1,), (1,)), ((), ())), preferred_element_type=F32)


def _split_dot(a, b_bf16):
    hi = a.astype(BF16)
    lo = (a - hi.astype(F32)).astype(BF16)
    return _dot(hi, b_bf16) + _dot(lo, b_bf16)


def _ln(x, g, b):
    mu = jnp.mean(x, axis=-1, keepdims=True)
    xc = x - mu
    var = jnp.mean(xc * xc, axis=-1, keepdims=True)
    return xc * lax.rsqrt(var + LN_EPS) * g + b


LOG2E = math.log2(math.e)


def _silu(x):
    return x * (1.0 / (1.0 + jnp.exp2(x * -LOG2E)))


def _gelu(x):
    return 0.5 * x * (1.0 + jnp.tanh(math.sqrt(2.0 / math.pi) * (x + 0.044715 * (x * x * x))))


def _pack_halves(x):
    c = x.shape[1] // 2
    lo = lax.bitcast_convert_type(x[:, :c].astype(BF16).astype(F32), jnp.uint32)
    hi = lax.bitcast_convert_type(x[:, c:].astype(BF16).astype(F32), jnp.uint32)
    return lax.bitcast_convert_type((lo >> 16) | hi, jnp.int32)


def _unpack_halves(p):
    u = lax.bitcast_convert_type(p, jnp.uint32)
    lo = lax.bitcast_convert_type(u << 16, F32)
    hi = lax.bitcast_convert_type(u & jnp.uint32(0xFFFF0000), F32)
    return lo, hi


def _log_sigmoid(x):
    return jnp.minimum(x, 0.0) - jnp.log(1.0 + jnp.exp(-jnp.abs(x)))


def _even_kernel(x_ref, w_in_ref, gm_g_ref, gm_b_ref, gm_w_ref, gm_bias_ref,
                 conv_w_ref, conv_b_ref, wq_ref, wk_ref, wv_ref, wif_t_ref, bif_t_ref,
                 norm_w_ref, skip_ref, w_out_ref, ln_g_ref, ln_b_ref,
                 o_ref,
                 xm_buf, ct_ref, m_ref, *, ts, aw, bw):
    dh = bw // B_HEADS
    agd = aw // A_GROUPS
    nck = ts // CHUNK
    pad = 8
    j = pl.program_id(1)

    @pl.when(j == 0)
    def _():
        xm_buf[0:pad, :] = jnp.zeros((pad, bw), F32)
        ct_ref[...] = jnp.zeros_like(ct_ref)
        m_ref[...] = jnp.zeros_like(m_ref)

    row = lax.broadcasted_iota(jnp.int32, (CHUNK, CHUNK), 0)
    col = lax.broadcasted_iota(jnp.int32, (CHUNK, CHUNK), 1)
    causal = col <= row
    diag = col == row
    triu = jnp.where(row <= col, 1.0, 0.0).astype(BF16)
    ones_blk = jnp.ones((CHUNK, LANES), BF16)

    x = x_ref[0]
    proj = _dot(x.astype(BF16), w_in_ref[...])
    a_u = _gelu(proj[:, :aw])
    a_v = _gelu(proj[:, aw:2 * aw])
    xm = proj[:, 2 * aw:2 * aw + bw]
    z = proj[:, 2 * aw + bw:]

    vn = _ln(a_v, gm_g_ref[...], gm_b_ref[...]).astype(BF16)
    ya_chunks = []
    for c in range(nck):
        cols = []
        for g in range(A_GROUPS):
            v_cg = vn[c * CHUNK:(c + 1) * CHUNK, g * agd:(g + 1) * agd]
            cols.append(_dot(gm_w_ref[g], v_cg))
        ya_chunks.append(jnp.concatenate(cols, axis=1) + gm_bias_ref[...])
    y_a = a_u * jnp.concatenate(ya_chunks, axis=0)

    xm_buf[pad:pad + ts, :] = xm
    conv = conv_b_ref[...] + conv_w_ref[B_CONV - 1:B_CONV, :] * xm
    for k in range(B_CONV - 1):
        sh = B_CONV - 1 - k
        conv = conv + conv_w_ref[k:k + 1, :] * xm_buf[pad - sh:pad - sh + ts, :]
    xm_buf[pad - (B_CONV - 1):pad, :] = xm_buf[pad + ts - (B_CONV - 1):pad + ts, :]
    xc = _silu(conv)
    xc_b = xc.astype(BF16)
    q = _dot_head_pairs(xc_b, wq_ref)
    k_ = _dot_head_pairs(xc_b, wk_ref)
    v = _dot_head_pairs(xm.astype(BF16), wv_ref)
    gate_in = jnp.concatenate([q, k_, v], axis=1).astype(BF16)
    gates_t = _dot_nt(wif_t_ref[...], gate_in) + bif_t_ref[...]
    ig_all = gates_t[:B_HEADS, :]
    lf_all = _log_sigmoid(gates_t[B_HEADS:, :])
    q_b = q.astype(BF16)
    k_b = (k_ * dh ** -0.5).astype(BF16)
    v_b = v.astype(BF16)
    gate_z = _silu(z)

    lf_c = [lf_all[:, c * CHUNK:(c + 1) * CHUNK] for c in range(nck)]
    b_all = _split_dot(jnp.concatenate(lf_c, axis=0), triu)
    m_prev = m_ref[...]
    a_c, m_c, decay_c, wg_c = [], [], [], []
    for c in range(nck):
        b_r = b_all[c * B_HEADS:(c + 1) * B_HEADS]
        a_r = ig_all[:, c * CHUNK:(c + 1) * CHUNK] - b_r
        bl = b_r[:, CHUNK - 1:CHUNK]
        g_r = bl + a_r
        m_new = jnp.maximum(bl + m_prev, jnp.max(g_r, axis=1, keepdims=True))
        a_c.append(a_r)
        m_c.append(m_prev)
        decay_c.append(jnp.exp(bl + m_prev - m_new))
        wg_c.append(jnp.exp(g_r - m_new))
        m_prev = m_new
    m_ref[...] = m_prev

    units = [(c, h) for c in range(nck) for h in range(B_HEADS)]
    lmat = jnp.concatenate([jnp.where(causal, lf_c[c][h:h + 1, :], 0.0) for c, h in units], axis=0)
    dgm = jnp.concatenate([jnp.where(diag, wg_c[c][h:h + 1, :], 0.0) for c, h in units], axis=0)
    b_t_all = _split_dot(lmat, ones_blk)
    wg_t_all = _dot(dgm.astype(BF16), ones_blk)

    caugs = [ct_ref[h] for h in range(B_HEADS)]
    h_chunks = []
    for c in range(nck):
        sl = slice(c * CHUNK, (c + 1) * CHUNK)
        heads = []
        for h in range(B_HEADS):
            u = c * B_HEADS + h
            us = slice(u * CHUNK, (u + 1) * CHUNK)
            hs = slice(h * dh, (h + 1) * dh)
            qh, kh = q_b[sl, hs], k_b[sl, hs]
            vaug = jnp.concatenate([v_b[sl, hs], ones_blk], axis=1)
            m_row = m_c[c][h:h + 1, :]
            amat = jnp.where(causal, a_c[c][h:h + 1, :], NEG)
            mx = jnp.maximum(jnp.max(amat, axis=1, keepdims=True), m_row)
            w_intra = jnp.exp(amat - mx)
            w_state = jnp.exp(m_row - mx)
            s = _dot_nt(qh, kh) * w_intra
            kw = (kh.astype(F32) * wg_t_all[us]).astype(BF16)
            both = _dot(jnp.concatenate([s.astype(BF16), kw.T], axis=0), vaug)
            caug = caugs[h]
            naug = both[:CHUNK] + jnp.concatenate([w_state, w_state], axis=1) * _dot(qh, caug.astype(BF16))
            num, nq = naug[:, :dh], naug[:, dh:]
            hv = num / jnp.maximum(jnp.abs(nq), jnp.exp(-(b_t_all[us] + mx)))
            decay = decay_c[c][h:h + 1, :]
            caugs[h] = jnp.concatenate([decay, decay], axis=1) * caug + both[CHUNK:]
            hc = hv - jnp.mean(hv, axis=1, keepdims=True)
            hn = hc * lax.rsqrt(jnp.mean(hc * hc, axis=1, keepdims=True) + LN_EPS)
            heads.append(hn)
        h_chunks.append(jnp.concatenate(heads, axis=1))
    for h in range(B_HEADS):
        ct_ref[h] = caugs[h]
    hn_all = jnp.concatenate(h_chunks, axis=0) if nck > 1 else h_chunks[0]
    y_b = (hn_all * norm_w_ref[...] + skip_ref[...] * xc) * gate_z

    mix = jnp.concatenate([y_a, y_b], axis=1).astype(BF16)
    y = _dot(mix, w_out_ref[...])
    o_ref[0] = _ln(DN_ALPHA * x + y, ln_g_ref[...], ln_b_ref[...])


def _block_diag_pairs(w):
    hh, d, _ = w.shape
    wp = w.reshape(hh // 2, 2, d, d)
    eye = jnp.eye(2, dtype=w.dtype)
    return jnp.einsum('pade,ab->padbe', wp, eye).reshape(hh // 2, 2 * d, 2 * d)


def _dot_head_pairs(x, w_ref):
    npair, w2, _ = w_ref.shape
    return jnp.concatenate([_dot(x[:, p * w2:(p + 1) * w2], w_ref[p]) for p in range(npair)], axis=1)


def even_mixer_layer(x, w_in, gm_ln_g, gm_ln_b, gm_ws, gm_bs, conv_w, conv_b, wq, wk, wv, w_if,
                     b_if, norm_w, skip, w_out, ln_g, ln_b, *, ts):
    bsz, seq, d = x.shape
    aw = gm_ln_g.shape[0]
    bw = conv_b.shape[0]
    agd = aw // A_GROUPS
    causal = jnp.tril(jnp.ones((CHUNK, CHUNK), dtype=bool))
    gm_w = jnp.where(causal[None], gm_ws, 0.0).astype(BF16)
    gm_bias = jnp.repeat(gm_bs.T, agd, axis=1)
    row = lambda a: a.reshape(1, -1)
    args = (x, w_in.astype(BF16), row(gm_ln_g), row(gm_ln_b), gm_w, gm_bias,
            conv_w, row(conv_b), _block_diag_pairs(wq).astype(BF16), _block_diag_pairs(wk).astype(BF16),
            _block_diag_pairs(wv).astype(BF16), w_if.T.astype(BF16),
            b_if.reshape(-1, 1), row(norm_w), row(skip), w_out.astype(BF16),
            row(ln_g), row(ln_b))
    in_specs = [pl.BlockSpec((1, ts, d), lambda b, j: (b, j, 0))] + [_full(a.shape) for a in args[1:]]
    dh = bw // B_HEADS
    assert dh == LANES and CHUNK == LANES
    return pl.pallas_call(
        functools.partial(_even_kernel, ts=ts, aw=aw, bw=bw),
        grid=(bsz, seq // ts),
        in_specs=in_specs,
        out_specs=pl.BlockSpec((1, ts, d), lambda b, j: (b, j, 0)),
        out_shape=jax.ShapeDtypeStruct((bsz, seq, d), F32),
        scratch_shapes=[pltpu.VMEM((8 + ts, bw), F32),
                        pltpu.VMEM((B_HEADS, dh, dh + LANES), F32),
                        pltpu.VMEM((B_HEADS, LANES), F32)],
        compiler_params=_cparams(("arbitrary", "arbitrary")),
        name="even_mixer",
    )(*args)


def _memfold_kernel(mem_ref, wkv_ref, wq_ref, wo_ref, wqk_ref, vo_ref, *, d):
    dh = d // X_HEADS
    m_len = mem_ref.shape[1]
    kv = _dot(mem_ref[0].astype(BF16), wkv_ref[...])
    k = (kv[:, :d] * (dh ** -0.5 * LOG2E)).astype(BF16)
    v = kv[:, d:].astype(BF16)
    for h in range(X_HEADS):
        hs = slice(h * dh, (h + 1) * dh)
        ms = slice(h * m_len, (h + 1) * m_len)
        wqk_ref[0, :, ms] = _dot_nt(wq_ref[:, hs], k[:, hs]).astype(BF16)
        vo_ref[0, ms, :] = _dot(v[:, hs], wo_ref[hs, :]).astype(BF16)


def memory_fold(mem, wkv, wq, wo):
    bsz, m_len, d = mem.shape
    hm = X_HEADS * m_len
    args = (mem, wkv.astype(BF16), wq.astype(BF16), wo.astype(BF16))
    return pl.pallas_call(
        functools.partial(_memfold_kernel, d=d),
        grid=(bsz,),
        in_specs=[pl.BlockSpec((1, m_len, d), lambda b: (b, 0, 0))] + [_full(a.shape) for a in args[1:]],
        out_specs=[pl.BlockSpec((1, d, hm), lambda b: (b, 0, 0)),
                   pl.BlockSpec((1, hm, d), lambda b: (b, 0, 0))],
        out_shape=[jax.ShapeDtypeStruct((bsz, d, hm), BF16),
                   jax.ShapeDtypeStruct((bsz, hm, d), BF16)],
        compiler_params=_cparams(("arbitrary",)),
        name="memory_fold",
    )(*args)


ROUTE_W = 128
XATTN_TS = 1024
DEST_ROWS = 8


def _xattn_kernel(x_ref, wqk_ref, vo_ref, ln_g_ref, ln_b_ref, wr_ref, br_ref,
                  o_ref, xp_ref, route_ref, cnt_ref):
    m_len = wqk_ref.shape[2] // X_HEADS

    @pl.when(pl.program_id(1) == 0)
    def _():
        cnt_ref[...] = jnp.zeros_like(cnt_ref)

    x = x_ref[0]
    ts = x.shape[0]
    xb = x.astype(BF16)
    probs = []
    for h in range(X_HEADS):
        s = _dot(xb, wqk_ref[0, :, h * m_len:(h + 1) * m_len])
        p = jnp.exp2(s - jnp.max(s, axis=1, keepdims=True))
        probs.append((p / jnp.sum(p, axis=1, keepdims=True)).astype(BF16))
    y = _dot(jnp.concatenate(probs, axis=1), vo_ref[0])
    x2 = _ln(DN_ALPHA * x + y, ln_g_ref[...], ln_b_ref[...])
    o_ref[0] = x2
    xp_ref[0] = _pack_halves(x2)

    x_hi = x2.astype(BF16)
    x_lo = (x2 - x_hi.astype(F32)).astype(BF16)
    logits = (_dot(x_hi, wr_ref[0]) + _dot(x_lo, wr_ref[0]) + _dot(x_hi, wr_ref[1])) + br_ref[...]
    lane = lax.broadcasted_iota(jnp.int32, (ts, ROUTE_W), 1)
    is_g = lane < N_GROUPS
    lg = jnp.where(is_g, logits, NEG)
    mg = jnp.max(lg, axis=1, keepdims=True)
    gi = jnp.min(jnp.where(jnp.logical_and(is_g, lg == mg), lane, ROUTE_W), axis=1, keepdims=True)
    gate_g = 1.0 / jnp.sum(jnp.where(is_g, jnp.exp(lg - mg), 0.0), axis=1, keepdims=True)
    lo = N_GROUPS + gi * EXPERTS_PER_GROUP
    in_grp = jnp.logical_and(lane >= lo, lane < lo + EXPERTS_PER_GROUP)
    le = jnp.where(in_grp, logits, NEG)
    v1 = jnp.max(le, axis=1, keepdims=True)
    i1 = jnp.min(jnp.where(jnp.logical_and(in_grp, le == v1), lane, ROUTE_W), axis=1, keepdims=True)
    le2 = jnp.where(lane == i1, NEG, le)
    v2 = jnp.max(le2, axis=1, keepdims=True)
    i2 = jnp.min(jnp.where(jnp.logical_and(in_grp, le2 == v2), lane, ROUTE_W), axis=1, keepdims=True)
    e21 = jnp.exp(v2 - v1)
    p1 = 1.0 / (1.0 + e21)
    p2 = e21 * p1
    e1 = (i1 - N_GROUPS).astype(F32)
    e2 = (i2 - N_GROUPS).astype(F32)
    rec = jnp.where(lane == 0, e1, 0.0)
    rec = jnp.where(lane == 1, e2, rec)
    rec = jnp.where(lane == 2, gate_g * p1, rec)
    rec = jnp.where(lane == 3, gate_g * p2, rec)
    route_ref[0] = rec
    sel = jnp.logical_or(lane == i1 - N_GROUPS, lane == i2 - N_GROUPS)
    cnt_ref[0] += jnp.sum(jnp.where(sel, 1.0, 0.0), axis=0, keepdims=True)


def xattn_router_layer(x, wqk, vo, ln_g, ln_b, w_rg, b_rg, w_re, b_re, *, ts):
    bsz, seq, d = x.shape
    hm = wqk.shape[2]
    wr = jnp.zeros((d, ROUTE_W), F32).at[:, :N_GROUPS].set(w_rg).at[:, N_GROUPS:N_GROUPS + N_EXPERTS].set(w_re)
    br = jnp.zeros((1, ROUTE_W), F32).at[0, :N_GROUPS].set(b_rg).at[0, N_GROUPS:N_GROUPS + N_EXPERTS].set(b_re)
    wr_hi = wr.astype(BF16)
    wr_lo = (wr - wr_hi.astype(F32)).astype(BF16)
    wr = jnp.stack([wr_hi, wr_lo])
    args = (x, wqk, vo, ln_g.reshape(1, -1), ln_b.reshape(1, -1), wr, br)
    in_specs = [pl.BlockSpec((1, ts, d), lambda b, j: (b, j, 0)),
                pl.BlockSpec((1, d, hm), lambda b, j: (b, 0, 0)),
                pl.BlockSpec((1, hm, d), lambda b, j: (b, 0, 0))] + [_full(a.shape) for a in args[3:]]
    return pl.pallas_call(
        _xattn_kernel,
        grid=(bsz, seq // ts),
        in_specs=in_specs,
        out_specs=[pl.BlockSpec((1, ts, d), lambda b, j: (b, j, 0)),
                   pl.BlockSpec((1, ts, d // 2), lambda b, j: (b, j, 0)),
                   pl.BlockSpec((1, ts, ROUTE_W), lambda b, j: (b, j, 0)),
                   pl.BlockSpec((1, 1, ROUTE_W), lambda b, j: (b, 0, 0))],
        out_shape=[jax.ShapeDtypeStruct((bsz, seq, d), F32),
                   jax.ShapeDtypeStruct((bsz, seq, d // 2), jnp.int32),
                   jax.ShapeDtypeStruct((bsz, seq, ROUTE_W), F32),
                   jax.ShapeDtypeStruct((bsz, 1, ROUTE_W), F32)],
        compiler_params=_cparams(("arbitrary", "arbitrary")),
        name="xattn_router",
    )(*args)


def _moe_output(x2, g0, g1, rec, ln_g, ln_b):
    a_lo, a_hi = _unpack_halves(g0)
    b_lo, b_hi = _unpack_halves(g1)
    w0, w1 = rec[:, 2:3], rec[:, 3:4]
    y = jnp.concatenate([w0 * a_lo + w1 * b_lo, w0 * a_hi + w1 * b_hi], axis=1)
    return _ln(DN_ALPHA * x2 + y, ln_g, ln_b)


def _swa_kernel(x_ref, g0_ref, g1_ref, route_ref, pln_g_ref, pln_b_ref,
                wqkv_ref, bqkv_ref, cos_ref, sin_ref, sink_ref, wo_ref, ln_g_ref, ln_b_ref,
                *rest, ts, cq, ckv):
    o_ref, kprev, vprev = rest[-3:]
    j = pl.program_id(1)
    nb = ts // CHUNK
    dh = C_HEAD_DIM
    grp = (cq // dh) // C_KV_HEADS

    @pl.when(j == 0)
    def _():
        kprev[...] = jnp.zeros_like(kprev)
        vprev[...] = jnp.zeros_like(vprev)

    x = _moe_output(x_ref[0], g0_ref[0, 0], g1_ref[0, 0], route_ref[0], pln_g_ref[...], pln_b_ref[...])
    qkv = _dot(x.astype(BF16), wqkv_ref[...]) + bqkv_ref[...]
    cos = cos_ref[...]
    sin = sin_ref[...]
    lane = lax.broadcasted_iota(jnp.int32, (ts, LANES), 1)
    first_half = (lane % dh) < (dh // 2)

    def rope(t):
        outs = []
        for c in range(t.shape[1] // LANES):
            tc = t[:, c * LANES:(c + 1) * LANES]
            rot = jnp.where(first_half, pltpu.roll(tc, LANES - dh // 2, 1), pltpu.roll(tc, dh // 2, 1))
            outs.append(tc * cos + rot * sin)
        return jnp.concatenate(outs, axis=1)

    q = rope(qkv[:, :cq]).astype(BF16)
    k = rope(qkv[:, cq:cq + 2 * ckv]).astype(BF16)
    v = qkv[:, cq + 2 * ckv:].astype(BF16)

    r_i = lax.broadcasted_iota(jnp.int32, (CHUNK, 2 * CHUNK), 0)
    c_i = lax.broadcasted_iota(jnp.int32, (CHUNK, 2 * CHUNK), 1)
    band = jnp.logical_and(c_i > r_i, c_i <= r_i + CHUNK)
    sink_col = c_i == 0
    lane_k = lax.broadcasted_iota(jnp.int32, (2 * CHUNK, LANES), 1)
    key_row = lax.broadcasted_iota(jnp.int32, (2 * CHUNK, LANES), 0)
    lane_q = lax.broadcasted_iota(jnp.int32, (CHUNK, LANES), 1)
    ones_blk = jnp.ones((2 * CHUNK, LANES), BF16)
    zero_b = jnp.zeros((), BF16)
    blocks = []
    for c in range(nb):
        sl = slice(c * CHUNK, (c + 1) * CHUNK)
        if c == 0:
            kb = jnp.concatenate([kprev[...].astype(BF16), k[sl]], axis=0)
            vb = jnp.concatenate([vprev[...].astype(BF16), v[sl]], axis=0)
            first_key = jnp.where(j > 0, 0, CHUNK)
            valid = jnp.logical_and(band, c_i >= first_key)
        else:
            kb = k[(c - 1) * CHUNK:(c + 1) * CHUNK]
            vb = v[(c - 1) * CHUNK:(c + 1) * CHUNK]
            valid = band
        tiles = []
        for h in range(C_KV_HEADS):
            kd = kb[:, h * LANES:(h + 1) * LANES]
            vd = vb[:, h * LANES:(h + 1) * LANES]
            k_lo = jnp.where(lane_k < dh, kd, zero_b)
            k_hi = jnp.where(lane_k >= dh, kd, zero_b)
            vz = jnp.where(key_row == 0, zero_b, vd)
            q2 = jnp.concatenate([q[sl, (2 * h) * LANES:(2 * h + 1) * LANES],
                                  q[sl, (2 * h + 1) * LANES:(2 * h + 2) * LANES]], axis=0)
            s_lo = _dot_nt(q2, k_lo)
            s_hi = _dot_nt(q2, k_hi)
            parts = []
            for qk, g in ((s_lo[:CHUNK], 0), (s_lo[CHUNK:], 2), (s_hi[:CHUNK], 1), (s_hi[CHUNK:], 3)):
                sink = sink_ref[h * grp + g]
                parts.append(jnp.where(valid, qk, jnp.where(sink_col, sink, NEG)))
            s = jnp.concatenate(parts, axis=0)
            p = jnp.exp2(s - jnp.max(s, axis=1, keepdims=True)).astype(BF16)
            den = _dot(p, ones_blk)
            o2 = _dot(p, vz) / den
            tiles.append(jnp.where(lane_q < dh, o2[:CHUNK], o2[2 * CHUNK:3 * CHUNK]))
            tiles.append(jnp.where(lane_q < dh, o2[CHUNK:2 * CHUNK], o2[3 * CHUNK:]))
        blocks.append(jnp.concatenate(tiles, axis=1))
    kprev[...] = k[(nb - 1) * CHUNK:].astype(F32)
    vprev[...] = v[(nb - 1) * CHUNK:].astype(F32)
    att = (jnp.concatenate(blocks, axis=0) if nb > 1 else blocks[0]).astype(BF16)
    y = _dot(att, wo_ref[...])
    o_ref[0] = _ln(DN_ALPHA * x + y, ln_g_ref[...], ln_b_ref[...])


def swa_mixer_layer(pending, w_qkv, b_qkv, sinks, w_o, ln_g, ln_b, *, ts):
    x, gs, route, pln_g, pln_b = pending
    bsz, seq, d = x.shape
    cq = w_o.shape[0]
    ckv = (w_qkv.shape[1] - cq) // 2
    dh = C_HEAD_DIM
    inv = ROPE_THETA ** (-jnp.arange(0, dh, 2, dtype=F32) / dh)
    ang = jnp.arange(seq, dtype=F32)[:, None] * inv[None, :]
    reps = LANES // (dh // 2)
    sign = jnp.tile(jnp.concatenate([-jnp.ones((dh // 2,), F32), jnp.ones((dh // 2,), F32)]), LANES // dh)
    cos_t = jnp.tile(jnp.cos(ang), (1, reps))
    sin_t = jnp.tile(jnp.sin(ang), (1, reps)) * sign[None, :]
    assert 2 * dh == LANES

    def dup_heads(t):
        th = t.reshape(t.shape[:-1] + (ckv // dh, dh))
        return jnp.concatenate([th, th], axis=-1).reshape(t.shape[:-1] + (2 * ckv,))

    qs = dh ** -0.5 * LOG2E
    w_all = jnp.concatenate([w_qkv[:, :cq] * qs, dup_heads(w_qkv[:, cq:cq + ckv]),
                             dup_heads(w_qkv[:, cq + ckv:])], axis=1)
    b_all = jnp.concatenate([b_qkv[:cq] * qs, dup_heads(b_qkv[cq:cq + ckv]), dup_heads(b_qkv[cq + ckv:])])
    shared = (pln_g.reshape(1, -1), pln_b.reshape(1, -1),
              w_all.astype(BF16), b_all.reshape(1, -1), cos_t, sin_t, sinks.astype(F32) * LOG2E,
              w_o.astype(BF16), ln_g.reshape(1, -1), ln_b.reshape(1, -1))
    bp = bsz // len(gs)
    out = None
    for p, g in enumerate(gs):
        b0 = p * bp
        g = g.reshape(2, bp, seq, d // 2)
        args = [x, g, g, route, *shared]
        in_specs = [pl.BlockSpec((1, ts, d), lambda b, j: (b0 + b, j, 0)),
                    pl.BlockSpec((1, 1, ts, d // 2), lambda b, j: (0, b, j, 0)),
                    pl.BlockSpec((1, 1, ts, d // 2), lambda b, j: (1, b, j, 0)),
                    pl.BlockSpec((1, ts, ROUTE_W), lambda b, j: (b0 + b, j, 0)),
                    _full((1, d)), _full((1, d)),
                    _full(shared[2].shape), _full(shared[3].shape),
                    pl.BlockSpec((ts, LANES), lambda b, j: (j, 0)),
                    pl.BlockSpec((ts, LANES), lambda b, j: (j, 0)),
                    pl.BlockSpec(memory_space=pltpu.SMEM),
                    _full(shared[7].shape), _full(shared[8].shape), _full(shared[9].shape)]
        aliases = {}
        if out is not None:
            args.append(out)
            in_specs.append(pl.BlockSpec(memory_space=pl.ANY))
            aliases = {len(args) - 1: 0}
        out = pl.pallas_call(
            functools.partial(_swa_kernel, ts=ts, cq=cq, ckv=ckv),
            grid=(bp, seq // ts),
            in_specs=in_specs,
            out_specs=pl.BlockSpec((1, ts, d), lambda b, j: (b0 + b, j, 0)),
            out_shape=jax.ShapeDtypeStruct((bsz, seq, d), F32),
            input_output_aliases=aliases,
            scratch_shapes=[pltpu.VMEM((CHUNK, 2 * ckv), F32), pltpu.VMEM((CHUNK, 2 * ckv), F32)],
            compiler_params=_cparams(("arbitrary", "arbitrary")),
            name="swa_mixer",
        )(*args)
    return out


def _slot_kernel(route_ref, pstart_ref, dest_ref, carry_ref, *, tb):
    @pl.when(pl.program_id(0) == 0)
    def _():
        carry_ref[...] = pstart_ref[...]

    rec = route_ref[...]
    lane = lax.broadcasted_iota(jnp.int32, (tb, ROUTE_W), 1)
    e0 = rec[:, 0:1].astype(jnp.int32)
    e1 = rec[:, 1:2].astype(jnp.int32)
    oh0 = lane == e0
    oh1 = lane == e1
    ohs = jnp.where(jnp.logical_or(oh0, oh1), 1.0, 0.0)
    r = lax.broadcasted_iota(jnp.int32, (tb, tb), 0)
    c = lax.broadcasted_iota(jnp.int32, (tb, tb), 1)
    before = jnp.where(c < r, 1.0, 0.0).astype(BF16)
    prefix = _dot(before, ohs.astype(BF16)) + carry_ref[...]
    d0 = jnp.sum(jnp.where(oh0, prefix, 0.0), axis=1, keepdims=True)
    d1 = jnp.sum(jnp.where(oh1, prefix, 0.0), axis=1, keepdims=True)
    dest = jnp.where(lane == 0, d0, jnp.where(lane == 1, d1, 0.0))
    dest_ref[...] = dest.T[:DEST_ROWS].astype(jnp.int32)
    carry_ref[...] += jnp.sum(ohs, axis=0, keepdims=True)


def moe_slots(route, pstart, *, tb):
    n = route.shape[0]
    return pl.pallas_call(
        functools.partial(_slot_kernel, tb=tb),
        grid=(n // tb,),
        in_specs=[pl.BlockSpec((tb, ROUTE_W), lambda i: (i, 0)), _full((1, ROUTE_W))],
        out_specs=pl.BlockSpec((DEST_ROWS, tb), lambda i: (0, i)),
        out_shape=jax.ShapeDtypeStruct((DEST_ROWS, n), jnp.int32),
        scratch_shapes=[pltpu.VMEM((1, ROUTE_W), F32)],
        compiler_params=_cparams(("arbitrary",)),
        name="moe_slots",
    )(route, pstart)


def _ffn_kernel(blk_exp_ref, new_exp_ref, nblk_ref, xs_ref, w1_ref, w3_ref, w2_ref, ys_ref,
                w1_b, w3_b, w2_b):
    i = pl.program_id(0)
    used = i < nblk_ref[0]

    @pl.when(jnp.logical_and(used, new_exp_ref[i] == 1))
    def _():
        w1_b[...] = w1_ref[0, 0].astype(BF16)
        w3_b[...] = w3_ref[0, 0].astype(BF16)
        w2_b[...] = w2_ref[0, 0].astype(BF16)

    @pl.when(used)
    def _():
        x_lo, x_hi = _unpack_halves(xs_ref[...])
        x_lo, x_hi = x_lo.astype(BF16), x_hi.astype(BF16)
        dl = x_lo.shape[1]
        h1 = _dot(x_lo, w1_b[:dl, :]) + _dot(x_hi, w1_b[dl:, :])
        h3 = _dot(x_lo, w3_b[:dl, :]) + _dot(x_hi, w3_b[dl:, :])
        h = (_silu(h1) * h3).astype(BF16)
        ys_ref[...] = _pack_halves(_dot(h, w2_b[...]))

    @pl.when(jnp.logical_not(used))
    def _():
        ys_ref[...] = jnp.zeros_like(ys_ref)


def moe_ffn(xs, blk_exp, new_exp, nblk, w1, w3, w2, *, layer, bm):
    n_pad, dp = xs.shape
    d = 2 * dp
    de = w1.shape[3]
    n_blk = n_pad // bm

    def x_map(i, be, ne, nb):
        return (jnp.minimum(i, nb[0] - 1), 0)

    def w_map(i, be, ne, nb):
        return (layer, be[i], 0, 0)

    return pl.pallas_call(
        _ffn_kernel,
        grid_spec=pltpu.PrefetchScalarGridSpec(
            num_scalar_prefetch=3,
            grid=(n_blk,),
            in_specs=[pl.BlockSpec((bm, dp), x_map),
                      pl.BlockSpec((1, 1, d, de), w_map),
                      pl.BlockSpec((1, 1, d, de), w_map),
                      pl.BlockSpec((1, 1, de, d), w_map)],
            out_specs=pl.BlockSpec((bm, dp), lambda i, be, ne, nb: (i, 0)),
            scratch_shapes=[pltpu.VMEM((d, de), BF16), pltpu.VMEM((d, de), BF16),
                            pltpu.VMEM((de, d), BF16)]),
        out_shape=jax.ShapeDtypeStruct((n_pad, dp), jnp.int32),
        compiler_params=_cparams(("arbitrary",)),
        name="moe_ffn",
    )(blk_exp, new_exp, nblk, xs, w1, w3, w2)


def _combine_kernel(x_ref, g0_ref, g1_ref, route_ref, ln_g_ref, ln_b_ref, *rest):
    o_ref = rest[-1]
    o_ref[...] = _moe_output(x_ref[...], g0_ref[0], g1_ref[0], route_ref[...], ln_g_ref[...], ln_b_ref[...])


def moe_combine(x, g, route, ln_g, ln_b, *, row0, tb, prev=None):
    n, d = x.shape
    m = g.shape[1]
    blk0 = row0 // tb
    row_spec = pl.BlockSpec((tb, d), lambda i: (blk0 + i, 0))
    args = [x, g, g, route, ln_g.reshape(1, -1), ln_b.reshape(1, -1)]
    in_specs = [row_spec,
                pl.BlockSpec((1, tb, d // 2), lambda i: (0, i, 0)),
                pl.BlockSpec((1, tb, d // 2), lambda i: (1, i, 0)),
                pl.BlockSpec((tb, ROUTE_W), lambda i: (blk0 + i, 0)),
                _full((1, d)), _full((1, d))]
    aliases = {}
    if prev is not None:
        args.append(prev)
        in_specs.append(pl.BlockSpec(memory_space=pl.ANY))
        aliases = {len(args) - 1: 0}
    return pl.pallas_call(
        _combine_kernel,
        grid=(m // tb,),
        in_specs=in_specs,
        out_specs=row_spec,
        out_shape=jax.ShapeDtypeStruct((n, d), F32),
        input_output_aliases=aliases,
        compiler_params=_cparams(("arbitrary",)),
        name="moe_combine",
    )(*args)


MOE_BM = 1024
COMBINE_PARTS = 4
MIXER_PARTS = 2
MIXER_TS = 1024


def hierarchical_moe_layer(x2, xp, route, counts, w1, w3, w2, ln_g, ln_b, *, layer, defer_combine):
    bsz, seq, d = x2.shape
    n = bsz * seq
    bm = MOE_BM
    rt = route.reshape(n, ROUTE_W)
    tb = min(512, n)
    n_blk = (2 * n) // bm + N_EXPERTS
    cnt = counts.sum(axis=0)[0, :N_EXPERTS].astype(jnp.int32)
    pcnt = (cnt + bm - 1) // bm * bm
    pends = jnp.cumsum(pcnt)
    pstart = pends - pcnt
    nblk = (pends[-1] // bm).astype(jnp.int32).reshape(1)
    blk_row = jnp.arange(n_blk, dtype=jnp.int32) * bm
    blk_exp = jnp.minimum(jnp.sum((pends[None, :] <= blk_row[:, None]).astype(jnp.int32), axis=1),
                          N_EXPERTS - 1)
    last_exp = blk_exp[jnp.maximum(nblk[0] - 1, 0)]
    blk_exp = jnp.where(jnp.arange(n_blk) < nblk[0], blk_exp, last_exp)
    new_exp = jnp.concatenate([jnp.ones((1,), jnp.int32),
                               (blk_exp[1:] != blk_exp[:-1]).astype(jnp.int32)])
    pstart_rec = jnp.zeros((1, ROUTE_W), F32).at[0, :N_EXPERTS].set(pstart.astype(F32))
    dest = moe_slots(rt, pstart_rec, tb=tb)
    xs = moe_dispatch(xp.reshape(n, d // 2), dest[0], dest[1], n_blk * bm)
    ys = moe_ffn(xs, blk_exp, new_exp, nblk, w1, w3, w2, layer=layer, bm=bm)
    if defer_combine:
        bparts = MIXER_PARTS if bsz % MIXER_PARTS == 0 else 1
        mb = n // bparts
        gs = [moe_gather(ys, dest[0, p * mb:(p + 1) * mb], dest[1, p * mb:(p + 1) * mb]) for p in range(bparts)]
        return x2, gs, route, ln_g, ln_b
    parts = COMBINE_PARTS if n % (COMBINE_PARTS * tb * 8) == 0 else 1
    m = n // parts
    out = None
    for p in range(parts):
        g = moe_gather(ys, dest[0, p * m:(p + 1) * m], dest[1, p * m:(p + 1) * m])
        out = moe_combine(x2.reshape(n, d), g, rt, ln_g, ln_b, row0=p * m, tb=tb, prev=out)
    return out.reshape(bsz, seq, d)


SC_ROWS = 128


def _sc_mesh():
    return plsc.VectorSubcoreMesh(core_axis_name="c", subcore_axis_name="s")


def moe_dispatch(xf, dest0, dest1, n_pad):
    n, d = xf.shape
    info = plsc.get_sparse_core_info()
    nw = info.num_cores * info.num_subcores
    per_w = n // nw
    r = min(SC_ROWS, per_w)

    def body(x_hbm, d0_hbm, d1_hbm, xs_hbm, i0_v, i1_v, rows_v, sem):
        wid = lax.axis_index("s") * info.num_cores + lax.axis_index("c")

        @pl.loop(0, per_w // r)
        def _(c):
            base = pl.multiple_of(wid * per_w + c * r, 8)
            pltpu.sync_copy(d0_hbm.at[pl.ds(base, r)], i0_v)
            pltpu.sync_copy(d1_hbm.at[pl.ds(base, r)], i1_v)
            pltpu.sync_copy(x_hbm.at[pl.ds(base, r)], rows_v)
            pltpu.async_copy(rows_v, xs_hbm.at[i0_v], sem).wait()
            pltpu.async_copy(rows_v, xs_hbm.at[i1_v], sem).wait()

    return pl.kernel(
        body, out_type=jax.ShapeDtypeStruct((n_pad, d), xf.dtype), mesh=_sc_mesh(),
        scratch_types=[pltpu.VMEM((r,), jnp.int32), pltpu.VMEM((r,), jnp.int32),
                       pltpu.VMEM((r, d), xf.dtype), pltpu.SemaphoreType.DMA],
        name="moe_dispatch",
    )(xf, dest0, dest1)


def moe_gather(ys, dest0, dest1):
    n = dest0.shape[0]
    d = ys.shape[1]
    info = plsc.get_sparse_core_info()
    nw = info.num_cores * info.num_subcores
    per_w = n // nw
    r = min(SC_ROWS, per_w)

    def body(ys_hbm, d0_hbm, d1_hbm, g_hbm, i_v, rows_v, sem):
        wid = lax.axis_index("s") * info.num_cores + lax.axis_index("c")

        @pl.loop(0, per_w // r)
        def _(c):
            base = pl.multiple_of(wid * per_w + c * r, 8)
            for k, d_hbm in enumerate((d0_hbm, d1_hbm)):
                pltpu.sync_copy(d_hbm.at[pl.ds(base, r)], i_v)
                pltpu.async_copy(ys_hbm.at[i_v], rows_v, sem).wait()
                pltpu.sync_copy(rows_v, g_hbm.at[k, pl.ds(base, r)])

    return pl.kernel(
        body, out_type=jax.ShapeDtypeStruct((2, n, d), ys.dtype), mesh=_sc_mesh(),
        scratch_types=[pltpu.VMEM((r,), jnp.int32), pltpu.VMEM((r, d), ys.dtype),
                       pltpu.SemaphoreType.DMA],
        name="moe_gather",
    )(ys, dest0, dest1)


def kernel(x, mem, ln_g, ln_b, ev_w_in, ev_gm_ln_g, ev_gm_ln_b, ev_gm_ws, ev_gm_bs, ev_conv_w, ev_conv_b, ev_wq, ev_wk, ev_wv, ev_w_if, ev_b_if, ev_norm_w, ev_skip, ev_w_out, od_w_qkv, od_b_qkv, od_sinks, od_w_o, xa_wq, xa_wkv, xa_wo, moe_w_rg, moe_b_rg, moe_w_re, moe_b_re, moe_w1, moe_w3, moe_w2):
    bsz, seq, d = x.shape
    depth = ln_g.shape[0]
    assert depth == DEPTH
    ts = min(MIXER_TS, seq)
    for l in range(depth):
        if l % 2 == 0:
            e = l // 2
            x = even_mixer_layer(x, ev_w_in[e], ev_gm_ln_g[e], ev_gm_ln_b[e], ev_gm_ws[e], ev_gm_bs[e],
                                 ev_conv_w[e], ev_conv_b[e], ev_wq[e], ev_wk[e], ev_wv[e], ev_w_if[e],
                                 ev_b_if[e], ev_norm_w[e], ev_skip[e], ev_w_out[e],
                                 ln_g[l, 0], ln_b[l, 0], ts=ts)
        else:
            o = l // 2
            x = swa_mixer_layer(x, od_w_qkv[o], od_b_qkv[o], od_sinks[o], od_w_o[o],
                                ln_g[l, 0], ln_b[l, 0], ts=ts)
        wqk, vo = memory_fold(mem, xa_wkv[l], xa_wq[l], xa_wo[l])
        x, xp, route, counts = xattn_router_layer(x, wqk, vo, ln_g[l, 1], ln_b[l, 1],
                                                  moe_w_rg[l], moe_b_rg[l], moe_w_re[l], moe_b_re[l],
                                                  ts=min(XATTN_TS, seq))
        x = hierarchical_moe_layer(x, xp, route, counts, moe_w1, moe_w3, moe_w2,
                                   ln_g[l, 2], ln_b[l, 2], layer=l,
                                   defer_combine=(l + 1 < depth and (l + 1) % 2 == 1))
    return x
```

```python
import functools
import math

import jax
import jax.numpy as jnp
from jax import lax
from jax.experimental import pallas as pl
from jax.experimental.pallas import tpu as pltpu
from jax.experimental.pallas import tpu_sc as plsc

F32 = jnp.float32
BF16 = jnp.bfloat16

A_GROUPS = 4
CHUNK = 128
B_HEADS = 4
B_CONV = 4
C_HEAD_DIM = 64
C_KV_HEADS = 4
X_HEADS = 4
N_GROUPS = 4
EXPERTS_PER_GROUP = 8
N_EXPERTS = N_GROUPS * EXPERTS_PER_GROUP
ROPE_THETA = 10000.0
LN_EPS = 1e-5
DEPTH = 2
DN_ALPHA = (2 * DEPTH) ** 0.25

LANES = 128
VMEM_LIMIT = 48 * 1024 * 1024
NEG = -1e30


def _cparams(sem):
    return pltpu.CompilerParams(dimension_semantics=sem, vmem_limit_bytes=VMEM_LIMIT)


def _full(shape):
    nd = len(shape)
    return pl.BlockSpec(shape, lambda *_: (0,) * nd)


def _dot(a, b):
    return jnp.dot(a, b, preferred_element_type=F32)


def _dot_nt(a, b):
    return lax.dot_general(a, b, (((1,), (1,)), ((), ())), preferred_element_type=F32)


def _split_dot(a, b_bf16):
    hi = a.astype(BF16)
    lo = (a - hi.astype(F32)).astype(BF16)
    return _dot(hi, b_bf16) + _dot(lo, b_bf16)


def _ln(x, g, b):
    mu = jnp.mean(x, axis=-1, keepdims=True)
    xc = x - mu
    var = jnp.mean(xc * xc, axis=-1, keepdims=True)
    return xc * lax.rsqrt(var + LN_EPS) * g + b


LOG2E = math.log2(math.e)


def _silu(x):
    return x * (1.0 / (1.0 + jnp.exp2(x * -LOG2E)))


def _gelu(x):
    return 0.5 * x * (1.0 + jnp.tanh(math.sqrt(2.0 / math.pi) * (x + 0.044715 * (x * x * x))))


def _pack_halves(x):
    c = x.shape[1] // 2
    lo = lax.bitcast_convert_type(x[:, :c].astype(BF16).astype(F32), jnp.uint32)
    hi = lax.bitcast_convert_type(x[:, c:].astype(BF16).astype(F32), jnp.uint32)
    return lax.bitcast_convert_type((lo >> 16) | hi, jnp.int32)


def _unpack_halves(p):
    u = lax.bitcast_convert_type(p, jnp.uint32)
    lo = lax.bitcast_convert_type(u << 16, F32)
    hi = lax.bitcast_convert_type(u & jnp.uint32(0xFFFF0000), F32)
    return lo, hi


def _log_sigmoid(x):
    return jnp.minimum(x, 0.0) - jnp.log(1.0 + jnp.exp(-jnp.abs(x)))


def _even_kernel(x_ref, w_in_ref, gm_g_ref, gm_b_ref, gm_w_ref, gm_bias_ref,
                 conv_w_ref, conv_b_ref, wq_ref, wk_ref, wv_ref, wif_t_ref, bif_t_ref,
                 norm_w_ref, skip_ref, w_out_ref, ln_g_ref, ln_b_ref,
                 o_ref,
                 xm_buf, ct_ref, m_ref, *, ts, aw, bw):
    dh = bw // B_HEADS
    agd = aw // A_GROUPS
    nck = ts // CHUNK
    pad = 8
    j = pl.program_id(1)

    @pl.when(j == 0)
    def _():
        xm_buf[0:pad, :] = jnp.zeros((pad, bw), F32)
        ct_ref[...] = jnp.zeros_like(ct_ref)
        m_ref[...] = jnp.zeros_like(m_ref)

    row = lax.broadcasted_iota(jnp.int32, (CHUNK, CHUNK), 0)
    col = lax.broadcasted_iota(jnp.int32, (CHUNK, CHUNK), 1)
    causal = col <= row
    diag = col == row
    triu = jnp.where(row <= col, 1.0, 0.0).astype(BF16)
    ones_blk = jnp.ones((CHUNK, LANES), BF16)

    x = x_ref[0]
    proj = _dot(x.astype(BF16), w_in_ref[...])
    a_u = _gelu(proj[:, :aw])
    a_v = _gelu(proj[:, aw:2 * aw])
    xm = proj[:, 2 * aw:2 * aw + bw]
    z = proj[:, 2 * aw + bw:]

    vn = _ln(a_v, gm_g_ref[...], gm_b_ref[...]).astype(BF16)
    ya_chunks = []
    for c in range(nck):
        cols = []
        for g in range(A_GROUPS):
            v_cg = vn[c * CHUNK:(c + 1) * CHUNK, g * agd:(g + 1) * agd]
            cols.append(_dot(gm_w_ref[g], v_cg))
        ya_chunks.append(jnp.concatenate(cols, axis=1) + gm_bias_ref[...])
    y_a = a_u * jnp.concatenate(ya_chunks, axis=0)

    xm_buf[pad:pad + ts, :] = xm
    conv = conv_b_ref[...] + conv_w_ref[B_CONV - 1:B_CONV, :] * xm
    for k in range(B_CONV - 1):
        sh = B_CONV - 1 - k
        conv = conv + conv_w_ref[k:k + 1, :] * xm_buf[pad - sh:pad - sh + ts, :]
    xm_buf[pad - (B_CONV - 1):pad, :] = xm_buf[pad + ts - (B_CONV - 1):pad + ts, :]
    xc = _silu(conv)
    xc_b = xc.astype(BF16)
    q = _dot_head_pairs(xc_b, wq_ref)
    k_ = _dot_head_pairs(xc_b, wk_ref)
    v = _dot_head_pairs(xm.astype(BF16), wv_ref)
    gate_in = jnp.concatenate([q, k_, v], axis=1).astype(BF16)
    gates_t = _dot_nt(wif_t_ref[...], gate_in) + bif_t_ref[...]
    ig_all = gates_t[:B_HEADS, :]
    lf_all = _log_sigmoid(gates_t[B_HEADS:, :])
    q_b = q.astype(BF16)
    k_b = (k_ * dh ** -0.5).astype(BF16)
    v_b = v.astype(BF16)
    gate_z = _silu(z)

    lf_c = [lf_all[:, c * CHUNK:(c + 1) * CHUNK] for c in range(nck)]
    b_all = _split_dot(jnp.concatenate(lf_c, axis=0), triu)
    m_prev = m_ref[...]
    a_c, m_c, decay_c, wg_c = [], [], [], []
    for c in range(nck):
        b_r = b_all[c * B_HEADS:(c + 1) * B_HEADS]
        a_r = ig_all[:, c * CHUNK:(c + 1) * CHUNK] - b_r
        bl = b_r[:, CHUNK - 1:CHUNK]
        g_r = bl + a_r
        m_new = jnp.maximum(bl + m_prev, jnp.max(g_r, axis=1, keepdims=True))
        a_c.append(a_r)
        m_c.append(m_prev)
        decay_c.append(jnp.exp(bl + m_prev - m_new))
        wg_c.append(jnp.exp(g_r - m_new))
        m_prev = m_new
    m_ref[...] = m_prev

    units = [(c, h) for c in range(nck) for h in range(B_HEADS)]
    lmat = jnp.concatenate([jnp.where(causal, lf_c[c][h:h + 1, :], 0.0) for c, h in units], axis=0)
    dgm = jnp.concatenate([jnp.where(diag, wg_c[c][h:h + 1, :], 0.0) for c, h in units], axis=0)
    b_t_all = _split_dot(lmat, ones_blk)
    wg_t_all = _dot(dgm.astype(BF16), ones_blk)

    caugs = [ct_ref[h] for h in range(B_HEADS)]
    h_chunks = []
    for c in range(nck):
        sl = slice(c * CHUNK, (c + 1) * CHUNK)
        heads = []
        for h in range(B_HEADS):
            u = c * B_HEADS + h
            us = slice(u * CHUNK, (u + 1) * CHUNK)
            hs = slice(h * dh, (h + 1) * dh)
            qh, kh = q_b[sl, hs], k_b[sl, hs]
            vaug = jnp.concatenate([v_b[sl, hs], ones_blk], axis=1)
            m_row = m_c[c][h:h + 1, :]
            amat = jnp.where(causal, a_c[c][h:h + 1, :], NEG)
            mx = jnp.maximum(jnp.max(amat, axis=1, keepdims=True), m_row)
            w_intra = jnp.exp(amat - mx)
            w_state = jnp.exp(m_row - mx)
            s = _dot_nt(qh, kh) * w_intra
            kw = (kh.astype(F32) * wg_t_all[us]).astype(BF16)
            both = _dot(jnp.concatenate([s.astype(BF16), kw.T], axis=0), vaug)
            caug = caugs[h]
            naug = both[:CHUNK] + jnp.concatenate([w_state, w_state], axis=1) * _dot(qh, caug.astype(BF16))
            num, nq = naug[:, :dh], naug[:, dh:]
            hv = num / jnp.maximum(jnp.abs(nq), jnp.exp(-(b_t_all[us] + mx)))
            decay = decay_c[c][h:h + 1, :]
            caugs[h] = jnp.concatenate([decay, decay], axis=1) * caug + both[CHUNK:]
            hc = hv - jnp.mean(hv, axis=1, keepdims=True)
            hn = hc * lax.rsqrt(jnp.mean(hc * hc, axis=1, keepdims=True) + LN_EPS)
            heads.append(hn)
        h_chunks.append(jnp.concatenate(heads, axis=1))
    for h in range(B_HEADS):
        ct_ref[h] = caugs[h]
    hn_all = jnp.concatenate(h_chunks, axis=0) if nck > 1 else h_chunks[0]
    y_b = (hn_all * norm_w_ref[...] + skip_ref[...] * xc) * gate_z

    mix = jnp.concatenate([y_a, y_b], axis=1).astype(BF16)
    y = _dot(mix, w_out_ref[...])
    o_ref[0] = _ln(DN_ALPHA * x + y, ln_g_ref[...], ln_b_ref[...])


def _block_diag_pairs(w):
    hh, d, _ = w.shape
    wp = w.reshape(hh // 2, 2, d, d)
    eye = jnp.eye(2, dtype=w.dtype)
    return jnp.einsum('pade,ab->padbe', wp, eye).reshape(hh // 2, 2 * d, 2 * d)


def _dot_head_pairs(x, w_ref):
    npair, w2, _ = w_ref.shape
    return jnp.concatenate([_dot(x[:, p * w2:(p + 1) * w2], w_ref[p]) for p in range(npair)], axis=1)


def even_mixer_layer(x, w_in, gm_ln_g, gm_ln_b, gm_ws, gm_bs, conv_w, conv_b, wq, wk, wv, w_if,
                     b_if, norm_w, skip, w_out, ln_g, ln_b, *, ts):
    bsz, seq, d = x.shape
    aw = gm_ln_g.shape[0]
    bw = conv_b.shape[0]
    agd = aw // A_GROUPS
    causal = jnp.tril(jnp.ones((CHUNK, CHUNK), dtype=bool))
    gm_w = jnp.where(causal[None], gm_ws, 0.0).astype(BF16)
    gm_bias = jnp.repeat(gm_bs.T, agd, axis=1)
    row = lambda a: a.reshape(1, -1)
    args = (x, w_in.astype(BF16), row(gm_ln_g), row(gm_ln_b), gm_w, gm_bias,
            conv_w, row(conv_b), _block_diag_pairs(wq).astype(BF16), _block_diag_pairs(wk).astype(BF16),
            _block_diag_pairs(wv).astype(BF16), w_if.T.astype(BF16),
            b_if.reshape(-1, 1), row(norm_w), row(skip), w_out.astype(BF16),
            row(ln_g), row(ln_b))
    in_specs = [pl.BlockSpec((1, ts, d), lambda b, j: (b, j, 0))] + [_full(a.shape) for a in args[1:]]
    dh = bw // B_HEADS
    assert dh == LANES and CHUNK == LANES
    return pl.pallas_call(
        functools.partial(_even_kernel, ts=ts, aw=aw, bw=bw),
        grid=(bsz, seq // ts),
        in_specs=in_specs,
        out_specs=pl.BlockSpec((1, ts, d), lambda b, j: (b, j, 0)),
        out_shape=jax.ShapeDtypeStruct((bsz, seq, d), F32),
        scratch_shapes=[pltpu.VMEM((8 + ts, bw), F32),
                        pltpu.VMEM((B_HEADS, dh, dh + LANES), F32),
                        pltpu.VMEM((B_HEADS, LANES), F32)],
        compiler_params=_cparams(("arbitrary", "arbitrary")),
        name="even_mixer",
    )(*args)


def _memfold_kernel(mem_ref, wkv_ref, wq_ref, wo_ref, wqk_ref, vo_ref, *, d):
    dh = d // X_HEADS
    m_len = mem_ref.shape[1]
    kv = _dot(mem_ref[0].astype(BF16), wkv_ref[...])
    k = (kv[:, :d] * (dh ** -0.5 * LOG2E)).astype(BF16)
    v = kv[:, d:].astype(BF16)
    for h in range(X_HEADS):
        hs = slice(h * dh, (h + 1) * dh)
        ms = slice(h * m_len, (h + 1) * m_len)
        wqk_ref[0, :, ms] = _dot_nt(wq_ref[:, hs], k[:, hs]).astype(BF16)
        vo_ref[0, ms, :] = _dot(v[:, hs], wo_ref[hs, :]).astype(BF16)


def memory_fold(mem, wkv, wq, wo):
    bsz, m_len, d = mem.shape
    hm = X_HEADS * m_len
    args = (mem, wkv.astype(BF16), wq.astype(BF16), wo.astype(BF16))
    return pl.pallas_call(
        functools.partial(_memfold_kernel, d=d),
        grid=(bsz,),
        in_specs=[pl.BlockSpec((1, m_len, d), lambda b: (b, 0, 0))] + [_full(a.shape) for a in args[1:]],
        out_specs=[pl.BlockSpec((1, d, hm), lambda b: (b, 0, 0)),
                   pl.BlockSpec((1, hm, d), lambda b: (b, 0, 0))],
        out_shape=[jax.ShapeDtypeStruct((bsz, d, hm), BF16),
                   jax.ShapeDtypeStruct((bsz, hm, d), BF16)],
        compiler_params=_cparams(("arbitrary",)),
        name="memory_fold",
    )(*args)


ROUTE_W = 128
XATTN_TS = 1024
DEST_ROWS = 8


def _xattn_kernel(x_ref, wqk_ref, vo_ref, ln_g_ref, ln_b_ref, wr_ref, br_ref,
                  o_ref, xp_ref, route_ref, cnt_ref):
    m_len = wqk_ref.shape[2] // X_HEADS

    @pl.when(pl.program_id(1) == 0)
    def _():
        cnt_ref[...] = jnp.zeros_like(cnt_ref)

    x = x_ref[0]
    ts = x.shape[0]
    xb = x.astype(BF16)
    probs = []
    for h in range(X_HEADS):
        s = _dot(xb, wqk_ref[0, :, h * m_len:(h + 1) * m_len])
        p = jnp.exp2(s - jnp.max(s, axis=1, keepdims=True))
        probs.append((p / jnp.sum(p, axis=1, keepdims=True)).astype(BF16))
    y = _dot(jnp.concatenate(probs, axis=1), vo_ref[0])
    x2 = _ln(DN_ALPHA * x + y, ln_g_ref[...], ln_b_ref[...])
    o_ref[0] = x2
    xp_ref[0] = _pack_halves(x2)

    x_hi = x2.astype(BF16)
    x_lo = (x2 - x_hi.astype(F32)).astype(BF16)
    logits = (_dot(x_hi, wr_ref[0]) + _dot(x_lo, wr_ref[0]) + _dot(x_hi, wr_ref[1])) + br_ref[...]
    lane = lax.broadcasted_iota(jnp.int32, (ts, ROUTE_W), 1)
    is_g = lane < N_GROUPS
    lg = jnp.where(is_g, logits, NEG)
    mg = jnp.max(lg, axis=1, keepdims=True)
    gi = jnp.min(jnp.where(jnp.logical_and(is_g, lg == mg), lane, ROUTE_W), axis=1, keepdims=True)
    gate_g = 1.0 / jnp.sum(jnp.where(is_g, jnp.exp(lg - mg), 0.0), axis=1, keepdims=True)
    lo = N_GROUPS + gi * EXPERTS_PER_GROUP
    in_grp = jnp.logical_and(lane >= lo, lane < lo + EXPERTS_PER_GROUP)
    le = jnp.where(in_grp, logits, NEG)
    v1 = jnp.max(le, axis=1, keepdims=True)
    i1 = jnp.min(jnp.where(jnp.logical_and(in_grp, le == v1), lane, ROUTE_W), axis=1, keepdims=True)
    le2 = jnp.where(lane == i1, NEG, le)
    v2 = jnp.max(le2, axis=1, keepdims=True)
    i2 = jnp.min(jnp.where(jnp.logical_and(in_grp, le2 == v2), lane, ROUTE_W), axis=1, keepdims=True)
    e21 = jnp.exp(v2 - v1)
    p1 = 1.0 / (1.0 + e21)
    p2 = e21 * p1
    e1 = (i1 - N_GROUPS).astype(F32)
    e2 = (i2 - N_GROUPS).astype(F32)
    rec = jnp.where(lane == 0, e1, 0.0)
    rec = jnp.where(lane == 1, e2, rec)
    rec = jnp.where(lane == 2, gate_g * p1, rec)
    rec = jnp.where(lane == 3, gate_g * p2, rec)
    route_ref[0] = rec
    sel = jnp.logical_or(lane == i1 - N_GROUPS, lane == i2 - N_GROUPS)
    cnt_ref[0] += jnp.sum(jnp.where(sel, 1.0, 0.0), axis=0, keepdims=True)


def xattn_router_layer(x, wqk, vo, ln_g, ln_b, w_rg, b_rg, w_re, b_re, *, ts):
    bsz, seq, d = x.shape
    hm = wqk.shape[2]
    wr = jnp.zeros((d, ROUTE_W), F32).at[:, :N_GROUPS].set(w_rg).at[:, N_GROUPS:N_GROUPS + N_EXPERTS].set(w_re)
    br = jnp.zeros((1, ROUTE_W), F32).at[0, :N_GROUPS].set(b_rg).at[0, N_GROUPS:N_GROUPS + N_EXPERTS].set(b_re)
    wr_hi = wr.astype(BF16)
    wr_lo = (wr - wr_hi.astype(F32)).astype(BF16)
    wr = jnp.stack([wr_hi, wr_lo])
    args = (x, wqk, vo, ln_g.reshape(1, -1), ln_b.reshape(1, -1), wr, br)
    in_specs = [pl.BlockSpec((1, ts, d), lambda b, j: (b, j, 0)),
                pl.BlockSpec((1, d, hm), lambda b, j: (b, 0, 0)),
                pl.BlockSpec((1, hm, d), lambda b, j: (b, 0, 0))] + [_full(a.shape) for a in args[3:]]
    return pl.pallas_call(
        _xattn_kernel,
        grid=(bsz, seq // ts),
        in_specs=in_specs,
        out_specs=[pl.BlockSpec((1, ts, d), lambda b, j: (b, j, 0)),
                   pl.BlockSpec((1, ts, d // 2), lambda b, j: (b, j, 0)),
                   pl.BlockSpec((1, ts, ROUTE_W), lambda b, j: (b, j, 0)),
                   pl.BlockSpec((1, 1, ROUTE_W), lambda b, j: (b, 0, 0))],
        out_shape=[jax.ShapeDtypeStruct((bsz, seq, d), F32),
                   jax.ShapeDtypeStruct((bsz, seq, d // 2), jnp.int32),
                   jax.ShapeDtypeStruct((bsz, seq, ROUTE_W), F32),
                   jax.ShapeDtypeStruct((bsz, 1, ROUTE_W), F32)],
        compiler_params=_cparams(("arbitrary", "arbitrary")),
        name="xattn_router",
    )(*args)


def _moe_output(x2, g0, g1, rec, ln_g, ln_b):
    a_lo, a_hi = _unpack_halves(g0)
    b_lo, b_hi = _unpack_halves(g1)
    w0, w1 = rec[:, 2:3], rec[:, 3:4]
    y = jnp.concatenate([w0 * a_lo + w1 * b_lo, w0 * a_hi + w1 * b_hi], axis=1)
    return _ln(DN_ALPHA * x2 + y, ln_g, ln_b)


def _swa_kernel(x_ref, g0_ref, g1_ref, route_ref, pln_g_ref, pln_b_ref,
                wqkv_ref, bqkv_ref, cos_ref, sin_ref, sink_ref, wo_ref, ln_g_ref, ln_b_ref,
                *rest, ts, cq, ckv):
    o_ref, kprev, vprev = rest[-3:]
    j = pl.program_id(1)
    nb = ts // CHUNK
    dh = C_HEAD_DIM
    grp = (cq // dh) // C_KV_HEADS

    @pl.when(j == 0)
    def _():
        kprev[...] = jnp.zeros_like(kprev)
        vprev[...] = jnp.zeros_like(vprev)

    x = _moe_output(x_ref[0], g0_ref[0, 0], g1_ref[0, 0], route_ref[0], pln_g_ref[...], pln_b_ref[...])
    qkv = _dot(x.astype(BF16), wqkv_ref[...]) + bqkv_ref[...]
    cos = cos_ref[...]
    sin = sin_ref[...]
    lane = lax.broadcasted_iota(jnp.int32, (ts, LANES), 1)
    first_half = (lane % dh) < (dh // 2)

    def rope(t):
        outs = []
        for c in range(t.shape[1] // LANES):
            tc = t[:, c * LANES:(c + 1) * LANES]
            rot = jnp.where(first_half, pltpu.roll(tc, LANES - dh // 2, 1), pltpu.roll(tc, dh // 2, 1))
            outs.append(tc * cos + rot * sin)
        return jnp.concatenate(outs, axis=1)

    q = rope(qkv[:, :cq]).astype(BF16)
    k = rope(qkv[:, cq:cq + 2 * ckv]).astype(BF16)
    v = qkv[:, cq + 2 * ckv:].astype(BF16)

    r_i = lax.broadcasted_iota(jnp.int32, (CHUNK, 2 * CHUNK), 0)
    c_i = lax.broadcasted_iota(jnp.int32, (CHUNK, 2 * CHUNK), 1)
    band = jnp.logical_and(c_i > r_i, c_i <= r_i + CHUNK)
    sink_col = c_i == 0
    lane_k = lax.broadcasted_iota(jnp.int32, (2 * CHUNK, LANES), 1)
    key_row = lax.broadcasted_iota(jnp.int32, (2 * CHUNK, LANES), 0)
    lane_q = lax.broadcasted_iota(jnp.int32, (CHUNK, LANES), 1)
    ones_blk = jnp.ones((2 * CHUNK, LANES), BF16)
    zero_b = jnp.zeros((), BF16)
    blocks = []
    for c in range(nb):
        sl = slice(c * CHUNK, (c + 1) * CHUNK)
        if c == 0:
            kb = jnp.concatenate([kprev[...].astype(BF16), k[sl]], axis=0)
            vb = jnp.concatenate([vprev[...].astype(BF16), v[sl]], axis=0)
            first_key = jnp.where(j > 0, 0, CHUNK)
            valid = jnp.logical_and(band, c_i >= first_key)
        else:
            kb = k[(c - 1) * CHUNK:(c + 1) * CHUNK]
            vb = v[(c - 1) * CHUNK:(c + 1) * CHUNK]
            valid = band
        tiles = []
        for h in range(C_KV_HEADS):
            kd = kb[:, h * LANES:(h + 1) * LANES]
            vd = vb[:, h * LANES:(h + 1) * LANES]
            k_lo = jnp.where(lane_k < dh, kd, zero_b)
            k_hi = jnp.where(lane_k >= dh, kd, zero_b)
            vz = jnp.where(key_row == 0, zero_b, vd)
            q2 = jnp.concatenate([q[sl, (2 * h) * LANES:(2 * h + 1) * LANES],
                                  q[sl, (2 * h + 1) * LANES:(2 * h + 2) * LANES]], axis=0)
            s_lo = _dot_nt(q2, k_lo)
            s_hi = _dot_nt(q2, k_hi)
            parts = []
            for qk, g in ((s_lo[:CHUNK], 0), (s_lo[CHUNK:], 2), (s_hi[:CHUNK], 1), (s_hi[CHUNK:], 3)):
                sink = sink_ref[h * grp + g]
                parts.append(jnp.where(valid, qk, jnp.where(sink_col, sink, NEG)))
            s = jnp.concatenate(parts, axis=0)
            p = jnp.exp2(s - jnp.max(s, axis=1, keepdims=True)).astype(BF16)
            den = _dot(p, ones_blk)
            o2 = _dot(p, vz) / den
            tiles.append(jnp.where(lane_q < dh, o2[:CHUNK], o2[2 * CHUNK:3 * CHUNK]))
            tiles.append(jnp.where(lane_q < dh, o2[CHUNK:2 * CHUNK], o2[3 * CHUNK:]))
        blocks.append(jnp.concatenate(tiles, axis=1))
    kprev[...] = k[(nb - 1) * CHUNK:].astype(F32)
    vprev[...] = v[(nb - 1) * CHUNK:].astype(F32)
    att = (jnp.concatenate(blocks, axis=0) if nb > 1 else blocks[0]).astype(BF16)
    y = _dot(att, wo_ref[...])
    o_ref[0] = _ln(DN_ALPHA * x + y, ln_g_ref[...], ln_b_ref[...])


def swa_mixer_layer(pending, w_qkv, b_qkv, sinks, w_o, ln_g, ln_b, *, ts):
    x, gs, route, pln_g, pln_b = pending
    bsz, seq, d = x.shape
    cq = w_o.shape[0]
    ckv = (w_qkv.shape[1] - cq) // 2
    dh = C_HEAD_DIM
    inv = ROPE_THETA ** (-jnp.arange(0, dh, 2, dtype=F32) / dh)
    ang = jnp.arange(seq, dtype=F32)[:, None] * inv[None, :]
    reps = LANES // (dh // 2)
    sign = jnp.tile(jnp.concatenate([-jnp.ones((dh // 2,), F32), jnp.ones((dh // 2,), F32)]), LANES // dh)
    cos_t = jnp.tile(jnp.cos(ang), (1, reps))
    sin_t = jnp.tile(jnp.sin(ang), (1, reps)) * sign[None, :]
    assert 2 * dh == LANES

    def dup_heads(t):
        th = t.reshape(t.shape[:-1] + (ckv // dh, dh))
        return jnp.concatenate([th, th], axis=-1).reshape(t.shape[:-1] + (2 * ckv,))

    qs = dh ** -0.5 * LOG2E
    w_all = jnp.concatenate([w_qkv[:, :cq] * qs, dup_heads(w_qkv[:, cq:cq + ckv]),
                             dup_heads(w_qkv[:, cq + ckv:])], axis=1)
    b_all = jnp.concatenate([b_qkv[:cq] * qs, dup_heads(b_qkv[cq:cq + ckv]), dup_heads(b_qkv[cq + ckv:])])
    shared = (pln_g.reshape(1, -1), pln_b.reshape(1, -1),
              w_all.astype(BF16), b_all.reshape(1, -1), cos_t, sin_t, sinks.astype(F32) * LOG2E,
              w_o.astype(BF16), ln_g.reshape(1, -1), ln_b.reshape(1, -1))
    bp = bsz // len(gs)
    out = None
    for p, g in enumerate(gs):
        b0 = p * bp
        g = g.reshape(2, bp, seq, d // 2)
        args = [x, g, g, route, *shared]
        in_specs = [pl.BlockSpec((1, ts, d), lambda b, j: (b0 + b, j, 0)),
                    pl.BlockSpec((1, 1, ts, d // 2), lambda b, j: (0, b, j, 0)),
                    pl.BlockSpec((1, 1, ts, d // 2), lambda b, j: (1, b, j, 0)),
                    pl.BlockSpec((1, ts, ROUTE_W), lambda b, j: (b0 + b, j, 0)),
                    _full((1, d)), _full((1, d)),
                    _full(shared[2].shape), _full(shared[3].shape),
                    pl.BlockSpec((ts, LANES), lambda b, j: (j, 0)),
                    pl.BlockSpec((ts, LANES), lambda b, j: (j, 0)),
                    pl.BlockSpec(memory_space=pltpu.SMEM),
                    _full(shared[7].shape), _full(shared[8].shape), _full(shared[9].shape)]
        aliases = {}
        if out is not None:
            args.append(out)
            in_specs.append(pl.BlockSpec(memory_space=pl.ANY))
            aliases = {len(args) - 1: 0}
        out = pl.pallas_call(
            functools.partial(_swa_kernel, ts=ts, cq=cq, ckv=ckv),
            grid=(bp, seq // ts),
            in_specs=in_specs,
            out_specs=pl.BlockSpec((1, ts, d), lambda b, j: (b0 + b, j, 0)),
            out_shape=jax.ShapeDtypeStruct((bsz, seq, d), F32),
            input_output_aliases=aliases,
            scratch_shapes=[pltpu.VMEM((CHUNK, 2 * ckv), F32), pltpu.VMEM((CHUNK, 2 * ckv), F32)],
            compiler_params=_cparams(("arbitrary", "arbitrary")),
            name="swa_mixer",
        )(*args)
    return out


def _slot_kernel(route_ref, pstart_ref, dest_ref, carry_ref, *, tb):
    @pl.when(pl.program_id(0) == 0)
    def _():
        carry_ref[...] = pstart_ref[...]

    rec = route_ref[...]
    lane = lax.broadcasted_iota(jnp.int32, (tb, ROUTE_W), 1)
    e0 = rec[:, 0:1].astype(jnp.int32)
    e1 = rec[:, 1:2].astype(jnp.int32)
    oh0 = lane == e0
    oh1 = lane == e1
    ohs = jnp.where(jnp.logical_or(oh0, oh1), 1.0, 0.0)
    r = lax.broadcasted_iota(jnp.int32, (tb, tb), 0)
    c = lax.broadcasted_iota(jnp.int32, (tb, tb), 1)
    before = jnp.where(c < r, 1.0, 0.0).astype(BF16)
    prefix = _dot(before, ohs.astype(BF16)) + carry_ref[...]
    d0 = jnp.sum(jnp.where(oh0, prefix, 0.0), axis=1, keepdims=True)
    d1 = jnp.sum(jnp.where(oh1, prefix, 0.0), axis=1, keepdims=True)
    dest = jnp.where(lane == 0, d0, jnp.where(lane == 1, d1, 0.0))
    dest_ref[...] = dest.T[:DEST_ROWS].astype(jnp.int32)
    carry_ref[...] += jnp.sum(ohs, axis=0, keepdims=True)


def moe_slots(route, pstart, *, tb):
    n = route.shape[0]
    return pl.pallas_call(
        functools.partial(_slot_kernel, tb=tb),
        grid=(n // tb,),
        in_specs=[pl.BlockSpec((tb, ROUTE_W), lambda i: (i, 0)), _full((1, ROUTE_W))],
        out_specs=pl.BlockSpec((DEST_ROWS, tb), lambda i: (0, i)),
        out_shape=jax.ShapeDtypeStruct((DEST_ROWS, n), jnp.int32),
        scratch_shapes=[pltpu.VMEM((1, ROUTE_W), F32)],
        compiler_params=_cparams(("arbitrary",)),
        name="moe_slots",
    )(route, pstart)


def _ffn_kernel(blk_exp_ref, new_exp_ref, nblk_ref, xs_ref, w1_ref, w3_ref, w2_ref, ys_ref,
                w1_b, w3_b, w2_b):
    i = pl.program_id(0)
    used = i < nblk_ref[0]

    @pl.when(jnp.logical_and(used, new_exp_ref[i] == 1))
    def _():
        w1_b[...] = w1_ref[0, 0].astype(BF16)
        w3_b[...] = w3_ref[0, 0].astype(BF16)
        w2_b[...] = w2_ref[0, 0].astype(BF16)

    @pl.when(used)
    def _():
        x_lo, x_hi = _unpack_halves(xs_ref[...])
        x_lo, x_hi = x_lo.astype(BF16), x_hi.astype(BF16)
        dl = x_lo.shape[1]
        h1 = _dot(x_lo, w1_b[:dl, :]) + _dot(x_hi, w1_b[dl:, :])
        h3 = _dot(x_lo, w3_b[:dl, :]) + _dot(x_hi, w3_b[dl:, :])
        h = (_silu(h1) * h3).astype(BF16)
        ys_ref[...] = _pack_halves(_dot(h, w2_b[...]))

    @pl.when(jnp.logical_not(used))
    def _():
        ys_ref[...] = jnp.zeros_like(ys_ref)


def moe_ffn(xs, blk_exp, new_exp, nblk, w1, w3, w2, *, layer, bm):
    n_pad, dp = xs.shape
    d = 2 * dp
    de = w1.shape[3]
    n_blk = n_pad // bm

    def x_map(i, be, ne, nb):
        return (jnp.minimum(i, nb[0] - 1), 0)

    def w_map(i, be, ne, nb):
        return (layer, be[i], 0, 0)

    return pl.pallas_call(
        _ffn_kernel,
        grid_spec=pltpu.PrefetchScalarGridSpec(
            num_scalar_prefetch=3,
            grid=(n_blk,),
            in_specs=[pl.BlockSpec((bm, dp), x_map),
                      pl.BlockSpec((1, 1, d, de), w_map),
                      pl.BlockSpec((1, 1, d, de), w_map),
                      pl.BlockSpec((1, 1, de, d), w_map)],
            out_specs=pl.BlockSpec((bm, dp), lambda i, be, ne, nb: (i, 0)),
            scratch_shapes=[pltpu.VMEM((d, de), BF16), pltpu.VMEM((d, de), BF16),
                            pltpu.VMEM((de, d), BF16)]),
        out_shape=jax.ShapeDtypeStruct((n_pad, dp), jnp.int32),
        compiler_params=_cparams(("arbitrary",)),
        name="moe_ffn",
    )(blk_exp, new_exp, nblk, xs, w1, w3, w2)


def _combine_kernel(x_ref, g0_ref, g1_ref, route_ref, ln_g_ref, ln_b_ref, *rest):
    o_ref = rest[-1]
    o_ref[...] = _moe_output(x_ref[...], g0_ref[0], g1_ref[0], route_ref[...], ln_g_ref[...], ln_b_ref[...])


def moe_combine(x, g, route, ln_g, ln_b, *, row0, tb, prev=None):
    n, d = x.shape
    m = g.shape[1]
    blk0 = row0 // tb
    row_spec = pl.BlockSpec((tb, d), lambda i: (blk0 + i, 0))
    args = [x, g, g, route, ln_g.reshape(1, -1), ln_b.reshape(1, -1)]
    in_specs = [row_spec,
                pl.BlockSpec((1, tb, d // 2), lambda i: (0, i, 0)),
                pl.BlockSpec((1, tb, d // 2), lambda i: (1, i, 0)),
                pl.BlockSpec((tb, ROUTE_W), lambda i: (blk0 + i, 0)),
                _full((1, d)), _full((1, d))]
    aliases = {}
    if prev is not None:
        args.append(prev)
        in_specs.append(pl.BlockSpec(memory_space=pl.ANY))
        aliases = {len(args) - 1: 0}
    return pl.pallas_call(
        _combine_kernel,
        grid=(m // tb,),
        in_specs=in_specs,
        out_specs=row_spec,
        out_shape=jax.ShapeDtypeStruct((n, d), F32),
        input_output_aliases=aliases,
        compiler_params=_cparams(("arbitrary",)),
        name="moe_combine",
    )(*args)


MOE_BM = 1024
COMBINE_PARTS = 8
MIXER_PARTS = 2
MIXER_TS = 1024


def hierarchical_moe_layer(x2, xp, route, counts, w1, w3, w2, ln_g, ln_b, *, layer, defer_combine):
    bsz, seq, d = x2.shape
    n = bsz * seq
    bm = MOE_BM
    rt = route.reshape(n, ROUTE_W)
    tb = min(512, n)
    n_blk = (2 * n) // bm + N_EXPERTS
    cnt = counts.sum(axis=0)[0, :N_EXPERTS].astype(jnp.int32)
    pcnt = (cnt + bm - 1) // bm * bm
    pends = jnp.cumsum(pcnt)
    pstart = pends - pcnt
    nblk = (pends[-1] // bm).astype(jnp.int32).reshape(1)
    blk_row = jnp.arange(n_blk, dtype=jnp.int32) * bm
    blk_exp = jnp.minimum(jnp.sum((pends[None, :] <= blk_row[:, None]).astype(jnp.int32), axis=1),
                          N_EXPERTS - 1)
    last_exp = blk_exp[jnp.maximum(nblk[0] - 1, 0)]
    blk_exp = jnp.where(jnp.arange(n_blk) < nblk[0], blk_exp, last_exp)
    new_exp = jnp.concatenate([jnp.ones((1,), jnp.int32),
                               (blk_exp[1:] != blk_exp[:-1]).astype(jnp.int32)])
    pstart_rec = jnp.zeros((1, ROUTE_W), F32).at[0, :N_EXPERTS].set(pstart.astype(F32))
    dest = moe_slots(rt, pstart_rec, tb=tb)
    xs = moe_dispatch(xp.reshape(n, d // 2), dest[0], dest[1], n_blk * bm)
    ys = moe_ffn(xs, blk_exp, new_exp, nblk, w1, w3, w2, layer=layer, bm=bm)
    if defer_combine:
        bparts = MIXER_PARTS if bsz % MIXER_PARTS == 0 else 1
        mb = n // bparts
        gs = [moe_gather(ys, dest[0, p * mb:(p + 1) * mb], dest[1, p * mb:(p + 1) * mb]) for p in range(bparts)]
        return x2, gs, route, ln_g, ln_b
    parts = COMBINE_PARTS if n % (COMBINE_PARTS * tb * 8) == 0 else 1
    m = n // parts
    out = None
    for p in range(parts):
        g = moe_gather(ys, dest[0, p * m:(p + 1) * m], dest[1, p * m:(p + 1) * m])
        out = moe_combine(x2.reshape(n, d), g, rt, ln_g, ln_b, row0=p * m, tb=tb, prev=out)
    return out.reshape(bsz, seq, d)


SC_ROWS = 128


def _sc_mesh():
    return plsc.VectorSubcoreMesh(core_axis_name="c", subcore_axis_name="s")


def moe_dispatch(xf, dest0, dest1, n_pad):
    n, d = xf.shape
    info = plsc.get_sparse_core_info()
    nw = info.num_cores * info.num_subcores
    per_w = n // nw
    r = min(SC_ROWS, per_w)

    def body(x_hbm, d0_hbm, d1_hbm, xs_hbm, i0_v, i1_v, rows_v, sem):
        wid = lax.axis_index("s") * info.num_cores + lax.axis_index("c")

        @pl.loop(0, per_w // r)
        def _(c):
            base = pl.multiple_of(wid * per_w + c * r, 8)
            pltpu.sync_copy(d0_hbm.at[pl.ds(base, r)], i0_v)
            pltpu.sync_copy(d1_hbm.at[pl.ds(base, r)], i1_v)
            pltpu.sync_copy(x_hbm.at[pl.ds(base, r)], rows_v)
            pltpu.async_copy(rows_v, xs_hbm.at[i0_v], sem).wait()
            pltpu.async_copy(rows_v, xs_hbm.at[i1_v], sem).wait()

    return pl.kernel(
        body, out_type=jax.ShapeDtypeStruct((n_pad, d), xf.dtype), mesh=_sc_mesh(),
        scratch_types=[pltpu.VMEM((r,), jnp.int32), pltpu.VMEM((r,), jnp.int32),
                       pltpu.VMEM((r, d), xf.dtype), pltpu.SemaphoreType.DMA],
        name="moe_dispatch",
    )(xf, dest0, dest1)


def moe_gather(ys, dest0, dest1):
    n = dest0.shape[0]
    d = ys.shape[1]
    info = plsc.get_sparse_core_info()
    nw = info.num_cores * info.num_subcores
    per_w = n // nw
    r = min(SC_ROWS, per_w)

    def body(ys_hbm, d0_hbm, d1_hbm, g_hbm, i_v, rows_v, sem):
        wid = lax.axis_index("s") * info.num_cores + lax.axis_index("c")

        @pl.loop(0, per_w // r)
        def _(c):
            base = pl.multiple_of(wid * per_w + c * r, 8)
            for k, d_hbm in enumerate((d0_hbm, d1_hbm)):
                pltpu.sync_copy(d_hbm.at[pl.ds(base, r)], i_v)
                pltpu.async_copy(ys_hbm.at[i_v], rows_v, sem).wait()
                pltpu.sync_copy(rows_v, g_hbm.at[k, pl.ds(base, r)])

    return pl.kernel(
        body, out_type=jax.ShapeDtypeStruct((2, n, d), ys.dtype), mesh=_sc_mesh(),
        scratch_types=[pltpu.VMEM((r,), jnp.int32), pltpu.VMEM((r, d), ys.dtype),
                       pltpu.SemaphoreType.DMA],
        name="moe_gather",
    )(ys, dest0, dest1)


def kernel(x, mem, ln_g, ln_b, ev_w_in, ev_gm_ln_g, ev_gm_ln_b, ev_gm_ws, ev_gm_bs, ev_conv_w, ev_conv_b, ev_wq, ev_wk, ev_wv, ev_w_if, ev_b_if, ev_norm_w, ev_skip, ev_w_out, od_w_qkv, od_b_qkv, od_sinks, od_w_o, xa_wq, xa_wkv, xa_wo, moe_w_rg, moe_b_rg, moe_w_re, moe_b_re, moe_w1, moe_w3, moe_w2):
    bsz, seq, d = x.shape
    depth = ln_g.shape[0]
    assert depth == DEPTH
    ts = min(MIXER_TS, seq)
    for l in range(depth):
        if l % 2 == 0:
            e = l // 2
            x = even_mixer_layer(x, ev_w_in[e], ev_gm_ln_g[e], ev_gm_ln_b[e], ev_gm_ws[e], ev_gm_bs[e],
                                 ev_conv_w[e], ev_conv_b[e], ev_wq[e], ev_wk[e], ev_wv[e], ev_w_if[e],
                                 ev_b_if[e], ev_norm_w[e], ev_skip[e], ev_w_out[e],
                                 ln_g[l, 0], ln_b[l, 0], ts=ts)
        else:
            o = l // 2
            x = swa_mixer_layer(x, od_w_qkv[o], od_b_qkv[o], od_sinks[o], od_w_o[o],
                                ln_g[l, 0], ln_b[l, 0], ts=ts)
        wqk, vo = memory_fold(mem, xa_wkv[l], xa_wq[l], xa_wo[l])
        x, xp, route, counts = xattn_router_layer(x, wqk, vo, ln_g[l, 1], ln_b[l, 1],
                                                  moe_w_rg[l], moe_b_rg[l], moe_w_re[l], moe_b_re[l],
                                                  ts=min(XATTN_TS, seq))
        x = hierarchical_moe_layer(x, xp, route, counts, moe_w1, moe_w3, moe_w2,
                                   ln_g[l, 2], ln_b[l, 2], layer=l,
                                   defer_combine=(l + 1 < depth and (l + 1) % 2 == 1))
    return x
```

```python
import functools
import math

import jax
import jax.numpy as jnp
from jax import lax
from jax.experimental import pallas as pl
from jax.experimental.pallas import tpu as pltpu
from jax.experimental.pallas import tpu_sc as plsc

F32 = jnp.float32
BF16 = jnp.bfloat16

A_GROUPS = 4
CHUNK = 128
B_HEADS = 4
B_CONV = 4
C_HEAD_DIM = 64
C_KV_HEADS = 4
X_HEADS = 4
N_GROUPS = 4
EXPERTS_PER_GROUP = 8
N_EXPERTS = N_GROUPS * EXPERTS_PER_GROUP
ROPE_THETA = 10000.0
LN_EPS = 1e-5
DEPTH = 2
DN_ALPHA = (2 * DEPTH) ** 0.25

LANES = 128
VMEM_LIMIT = 48 * 1024 * 1024
NEG = -1e30


def _cparams(sem):
    return pltpu.CompilerParams(dimension_semantics=sem, vmem_limit_bytes=VMEM_LIMIT)


def _full(shape):
    nd = len(shape)
    return pl.BlockSpec(shape, lambda *_: (0,) * nd)


def _dot(a, b):
    return jnp.dot(a, b, preferred_element_type=F32)


def _dot_nt(a, b):
    return lax.dot_general(a, b, (((1,), (1,)), ((), ())), preferred_element_type=F32)


def _split_dot(a, b_bf16):
    hi = a.astype(BF16)
    lo = (a - hi.astype(F32)).astype(BF16)
    return _dot(hi, b_bf16) + _dot(lo, b_bf16)


def _ln(x, g, b):
    mu = jnp.mean(x, axis=-1, keepdims=True)
    xc = x - mu
    var = jnp.mean(xc * xc, axis=-1, keepdims=True)
    return xc * lax.rsqrt(var + LN_EPS) * g + b


LOG2E = math.log2(math.e)


def _silu(x):
    return x * (1.0 / (1.0 + jnp.exp2(x * -LOG2E)))


def _gelu(x):
    return 0.5 * x * (1.0 + jnp.tanh(math.sqrt(2.0 / math.pi) * (x + 0.044715 * (x * x * x))))


def _pack_halves(x):
    c = x.shape[1] // 2
    lo = lax.bitcast_convert_type(x[:, :c].astype(BF16).astype(F32), jnp.uint32)
    hi = lax.bitcast_convert_type(x[:, c:].astype(BF16).astype(F32), jnp.uint32)
    return lax.bitcast_convert_type((lo >> 16) | hi, jnp.int32)


def _unpack_halves(p):
    u = lax.bitcast_convert_type(p, jnp.uint32)
    lo = lax.bitcast_convert_type(u << 16, F32)
    hi = lax.bitcast_convert_type(u & jnp.uint32(0xFFFF0000), F32)
    return lo, hi


def _log_sigmoid(x):
    return jnp.minimum(x, 0.0) - jnp.log(1.0 + jnp.exp(-jnp.abs(x)))


def _even_kernel(x_ref, w_in_ref, gm_g_ref, gm_b_ref, gm_w_ref, gm_bias_ref,
                 conv_w_ref, conv_b_ref, wq_ref, wk_ref, wv_ref, wif_t_ref, bif_t_ref,
                 norm_w_ref, skip_ref, w_out_ref, ln_g_ref, ln_b_ref,
                 o_ref,
                 xm_buf, ct_ref, m_ref, *, ts, aw, bw):
    dh = bw // B_HEADS
    agd = aw // A_GROUPS
    nck = ts // CHUNK
    pad = 8
    j = pl.program_id(1)

    @pl.when(j == 0)
    def _():
        xm_buf[0:pad, :] = jnp.zeros((pad, bw), F32)
        ct_ref[...] = jnp.zeros_like(ct_ref)
        m_ref[...] = jnp.zeros_like(m_ref)

    row = lax.broadcasted_iota(jnp.int32, (CHUNK, CHUNK), 0)
    col = lax.broadcasted_iota(jnp.int32, (CHUNK, CHUNK), 1)
    causal = col <= row
    diag = col == row
    triu = jnp.where(row <= col, 1.0, 0.0).astype(BF16)
    ones_blk = jnp.ones((CHUNK, LANES), BF16)

    x = x_ref[0]
    proj = _dot(x.astype(BF16), w_in_ref[...])
    a_u = _gelu(proj[:, :aw])
    a_v = _gelu(proj[:, aw:2 * aw])
    xm = proj[:, 2 * aw:2 * aw + bw]
    z = proj[:, 2 * aw + bw:]

    vn = _ln(a_v, gm_g_ref[...], gm_b_ref[...]).astype(BF16)
    ya_chunks = []
    for c in range(nck):
        cols = []
        for g in range(A_GROUPS):
            v_cg = vn[c * CHUNK:(c + 1) * CHUNK, g * agd:(g + 1) * agd]
            cols.append(_dot(gm_w_ref[g], v_cg))
        ya_chunks.append(jnp.concatenate(cols, axis=1) + gm_bias_ref[...])
    y_a = a_u * jnp.concatenate(ya_chunks, axis=0)

    xm_buf[pad:pad + ts, :] = xm
    conv = conv_b_ref[...] + conv_w_ref[B_CONV - 1:B_CONV, :] * xm
    for k in range(B_CONV - 1):
        sh = B_CONV - 1 - k
        conv = conv + conv_w_ref[k:k + 1, :] * xm_buf[pad - sh:pad - sh + ts, :]
    xm_buf[pad - (B_CONV - 1):pad, :] = xm_buf[pad + ts - (B_CONV - 1):pad + ts, :]
    xc = _silu(conv)
    xc_b = xc.astype(BF16)
    q = _dot_head_pairs(xc_b, wq_ref)
    k_ = _dot_head_pairs(xc_b, wk_ref)
    v = _dot_head_pairs(xm.astype(BF16), wv_ref)
    gate_in = jnp.concatenate([q, k_, v], axis=1).astype(BF16)
    gates_t = _dot_nt(wif_t_ref[...], gate_in) + bif_t_ref[...]
    ig_all = gates_t[:B_HEADS, :]
    lf_all = _log_sigmoid(gates_t[B_HEADS:, :])
    q_b = q.astype(BF16)
    k_b = (k_ * dh ** -0.5).astype(BF16)
    v_b = v.astype(BF16)
    gate_z = _silu(z)

    lf_c = [lf_all[:, c * CHUNK:(c + 1) * CHUNK] for c in range(nck)]
    b_all = _split_dot(jnp.concatenate(lf_c, axis=0), triu)
    m_prev = m_ref[...]
    a_c, m_c, decay_c, wg_c = [], [], [], []
    for c in range(nck):
        b_r = b_all[c * B_HEADS:(c + 1) * B_HEADS]
        a_r = ig_all[:, c * CHUNK:(c + 1) * CHUNK] - b_r
        bl = b_r[:, CHUNK - 1:CHUNK]
        g_r = bl + a_r
        m_new = jnp.maximum(bl + m_prev, jnp.max(g_r, axis=1, keepdims=True))
        a_c.append(a_r)
        m_c.append(m_prev)
        decay_c.append(jnp.exp(bl + m_prev - m_new))
        wg_c.append(jnp.exp(g_r - m_new))
        m_prev = m_new
    m_ref[...] = m_prev

    units = [(c, h) for c in range(nck) for h in range(B_HEADS)]
    lmat = jnp.concatenate([jnp.where(causal, lf_c[c][h:h + 1, :], 0.0) for c, h in units], axis=0)
    dgm = jnp.concatenate([jnp.where(diag, wg_c[c][h:h + 1, :], 0.0) for c, h in units], axis=0)
    b_t_all = _split_dot(lmat, ones_blk)
    wg_t_all = _dot(dgm.astype(BF16), ones_blk)

    caugs = [ct_ref[h] for h in range(B_HEADS)]
    h_chunks = []
    for c in range(nck):
        sl = slice(c * CHUNK, (c + 1) * CHUNK)
        heads = []
        for h in range(B_HEADS):
            u = c * B_HEADS + h
            us = slice(u * CHUNK, (u + 1) * CHUNK)
            hs = slice(h * dh, (h + 1) * dh)
            qh, kh = q_b[sl, hs], k_b[sl, hs]
            vaug = jnp.concatenate([v_b[sl, hs], ones_blk], axis=1)
            m_row = m_c[c][h:h + 1, :]
            amat = jnp.where(causal, a_c[c][h:h + 1, :], NEG)
            mx = jnp.maximum(jnp.max(amat, axis=1, keepdims=True), m_row)
            w_intra = jnp.exp(amat - mx)
            w_state = jnp.exp(m_row - mx)
            s = _dot_nt(qh, kh) * w_intra
            kw = (kh.astype(F32) * wg_t_all[us]).astype(BF16)
            both = _dot(jnp.concatenate([s.astype(BF16), kw.T], axis=0), vaug)
            caug = caugs[h]
            naug = both[:CHUNK] + jnp.concatenate([w_state, w_state], axis=1) * _dot(qh, caug.astype(BF16))
            num, nq = naug[:, :dh], naug[:, dh:]
            hv = num / jnp.maximum(jnp.abs(nq), jnp.exp(-(b_t_all[us] + mx)))
            decay = decay_c[c][h:h + 1, :]
            caugs[h] = jnp.concatenate([decay, decay], axis=1) * caug + both[CHUNK:]
            hc = hv - jnp.mean(hv, axis=1, keepdims=True)
            hn = hc * lax.rsqrt(jnp.mean(hc * hc, axis=1, keepdims=True) + LN_EPS)
            heads.append(hn)
        h_chunks.append(jnp.concatenate(heads, axis=1))
    for h in range(B_HEADS):
        ct_ref[h] = caugs[h]
    hn_all = jnp.concatenate(h_chunks, axis=0) if nck > 1 else h_chunks[0]
    y_b = (hn_all * norm_w_ref[...] + skip_ref[...] * xc) * gate_z

    mix = jnp.concatenate([y_a, y_b], axis=1).astype(BF16)
    y = _dot(mix, w_out_ref[...])
    o_ref[0] = _ln(DN_ALPHA * x + y, ln_g_ref[...], ln_b_ref[...])


def _block_diag_pairs(w):
    hh, d, _ = w.shape
    wp = w.reshape(hh // 2, 2, d, d)
    eye = jnp.eye(2, dtype=w.dtype)
    return jnp.einsum('pade,ab->padbe', wp, eye).reshape(hh // 2, 2 * d, 2 * d)


def _dot_head_pairs(x, w_ref):
    npair, w2, _ = w_ref.shape
    return jnp.concatenate([_dot(x[:, p * w2:(p + 1) * w2], w_ref[p]) for p in range(npair)], axis=1)


def even_mixer_layer(x, w_in, gm_ln_g, gm_ln_b, gm_ws, gm_bs, conv_w, conv_b, wq, wk, wv, w_if,
                     b_if, norm_w, skip, w_out, ln_g, ln_b, *, ts):
    bsz, seq, d = x.shape
    aw = gm_ln_g.shape[0]
    bw = conv_b.shape[0]
    agd = aw // A_GROUPS
    causal = jnp.tril(jnp.ones((CHUNK, CHUNK), dtype=bool))
    gm_w = jnp.where(causal[None], gm_ws, 0.0).astype(BF16)
    gm_bias = jnp.repeat(gm_bs.T, agd, axis=1)
    row = lambda a: a.reshape(1, -1)
    args = (x, w_in.astype(BF16), row(gm_ln_g), row(gm_ln_b), gm_w, gm_bias,
            conv_w, row(conv_b), _block_diag_pairs(wq).astype(BF16), _block_diag_pairs(wk).astype(BF16),
            _block_diag_pairs(wv).astype(BF16), w_if.T.astype(BF16),
            b_if.reshape(-1, 1), row(norm_w), row(skip), w_out.astype(BF16),
            row(ln_g), row(ln_b))
    in_specs = [pl.BlockSpec((1, ts, d), lambda b, j: (b, j, 0))] + [_full(a.shape) for a in args[1:]]
    dh = bw // B_HEADS
    assert dh == LANES and CHUNK == LANES
    return pl.pallas_call(
        functools.partial(_even_kernel, ts=ts, aw=aw, bw=bw),
        grid=(bsz, seq // ts),
        in_specs=in_specs,
        out_specs=pl.BlockSpec((1, ts, d), lambda b, j: (b, j, 0)),
        out_shape=jax.ShapeDtypeStruct((bsz, seq, d), F32),
        scratch_shapes=[pltpu.VMEM((8 + ts, bw), F32),
                        pltpu.VMEM((B_HEADS, dh, dh + LANES), F32),
                        pltpu.VMEM((B_HEADS, LANES), F32)],
        compiler_params=_cparams(("arbitrary", "arbitrary")),
        name="even_mixer",
    )(*args)


def _memfold_kernel(mem_ref, wkv_ref, wq_ref, wo_ref, wqk_ref, vo_ref, *, d):
    dh = d // X_HEADS
    m_len = mem_ref.shape[1]
    kv = _dot(mem_ref[0].astype(BF16), wkv_ref[...])
    k = (kv[:, :d] * (dh ** -0.5 * LOG2E)).astype(BF16)
    v = kv[:, d:].astype(BF16)
    for h in range(X_HEADS):
        hs = slice(h * dh, (h + 1) * dh)
        ms = slice(h * m_len, (h + 1) * m_len)
        wqk_ref[0, :, ms] = _dot_nt(wq_ref[:, hs], k[:, hs]).astype(BF16)
        vo_ref[0, ms, :] = _dot(v[:, hs], wo_ref[hs, :]).astype(BF16)


def memory_fold(mem, wkv, wq, wo):
    bsz, m_len, d = mem.shape
    hm = X_HEADS * m_len
    args = (mem, wkv.astype(BF16), wq.astype(BF16), wo.astype(BF16))
    return pl.pallas_call(
        functools.partial(_memfold_kernel, d=d),
        grid=(bsz,),
        in_specs=[pl.BlockSpec((1, m_len, d), lambda b: (b, 0, 0))] + [_full(a.shape) for a in args[1:]],
        out_specs=[pl.BlockSpec((1, d, hm), lambda b: (b, 0, 0)),
                   pl.BlockSpec((1, hm, d), lambda b: (b, 0, 0))],
        out_shape=[jax.ShapeDtypeStruct((bsz, d, hm), BF16),
                   jax.ShapeDtypeStruct((bsz, hm, d), BF16)],
        compiler_params=_cparams(("arbitrary",)),
        name="memory_fold",
    )(*args)


ROUTE_W = 128
XATTN_TS = 1024
DEST_ROWS = 8


def _xattn_kernel(x_ref, wqk_ref, vo_ref, ln_g_ref, ln_b_ref, wr_ref, br_ref,
                  o_ref, xp_ref, route_ref, cnt_ref):
    m_len = wqk_ref.shape[2] // X_HEADS

    @pl.when(pl.program_id(1) == 0)
    def _():
        cnt_ref[...] = jnp.zeros_like(cnt_ref)

    x = x_ref[0]
    ts = x.shape[0]
    xb = x.astype(BF16)
    probs = []
    for h in range(X_HEADS):
        s = _dot(xb, wqk_ref[0, :, h * m_len:(h + 1) * m_len])
        p = jnp.exp2(s - jnp.max(s, axis=1, keepdims=True))
        probs.append((p / jnp.sum(p, axis=1, keepdims=True)).astype(BF16))
    y = _dot(jnp.concatenate(probs, axis=1), vo_ref[0])
    x2 = _ln(DN_ALPHA * x + y, ln_g_ref[...], ln_b_ref[...])
    o_ref[0] = x2
    xp_ref[0] = _pack_halves(x2)

    x_hi = x2.astype(BF16)
    x_lo = (x2 - x_hi.astype(F32)).astype(BF16)
    logits = (_dot(x_hi, wr_ref[0]) + _dot(x_lo, wr_ref[0]) + _dot(x_hi, wr_ref[1])) + br_ref[...]
    lane = lax.broadcasted_iota(jnp.int32, (ts, ROUTE_W), 1)
    is_g = lane < N_GROUPS
    lg = jnp.where(is_g, logits, NEG)
    mg = jnp.max(lg, axis=1, keepdims=True)
    gi = jnp.min(jnp.where(jnp.logical_and(is_g, lg == mg), lane, ROUTE_W), axis=1, keepdims=True)
    gate_g = 1.0 / jnp.sum(jnp.where(is_g, jnp.exp(lg - mg), 0.0), axis=1, keepdims=True)
    lo = N_GROUPS + gi * EXPERTS_PER_GROUP
    in_grp = jnp.logical_and(lane >= lo, lane < lo + EXPERTS_PER_GROUP)
    le = jnp.where(in_grp, logits, NEG)
    v1 = jnp.max(le, axis=1, keepdims=True)
    i1 = jnp.min(jnp.where(jnp.logical_and(in_grp, le == v1), lane, ROUTE_W), axis=1, keepdims=True)
    le2 = jnp.where(lane == i1, NEG, le)
    v2 = jnp.max(le2, axis=1, keepdims=True)
    i2 = jnp.min(jnp.where(jnp.logical_and(in_grp, le2 == v2), lane, ROUTE_W), axis=1, keepdims=True)
    e21 = jnp.exp(v2 - v1)
    p1 = 1.0 / (1.0 + e21)
    p2 = e21 * p1
    e1 = (i1 - N_GROUPS).astype(F32)
    e2 = (i2 - N_GROUPS).astype(F32)
    rec = jnp.where(lane == 0, e1, 0.0)
    rec = jnp.where(lane == 1, e2, rec)
    rec = jnp.where(lane == 2, gate_g * p1, rec)
    rec = jnp.where(lane == 3, gate_g * p2, rec)
    route_ref[0] = rec
    sel = jnp.logical_or(lane == i1 - N_GROUPS, lane == i2 - N_GROUPS)
    cnt_ref[0] += jnp.sum(jnp.where(sel, 1.0, 0.0), axis=0, keepdims=True)


def xattn_router_layer(x, wqk, vo, ln_g, ln_b, w_rg, b_rg, w_re, b_re, *, ts):
    bsz, seq, d = x.shape
    hm = wqk.shape[2]
    wr = jnp.zeros((d, ROUTE_W), F32).at[:, :N_GROUPS].set(w_rg).at[:, N_GROUPS:N_GROUPS + N_EXPERTS].set(w_re)
    br = jnp.zeros((1, ROUTE_W), F32).at[0, :N_GROUPS].set(b_rg).at[0, N_GROUPS:N_GROUPS + N_EXPERTS].set(b_re)
    wr_hi = wr.astype(BF16)
    wr_lo = (wr - wr_hi.astype(F32)).astype(BF16)
    wr = jnp.stack([wr_hi, wr_lo])
    args = (x, wqk, vo, ln_g.reshape(1, -1), ln_b.reshape(1, -1), wr, br)
    in_specs = [pl.BlockSpec((1, ts, d), lambda b, j: (b, j, 0)),
                pl.BlockSpec((1, d, hm), lambda b, j: (b, 0, 0)),
                pl.BlockSpec((1, hm, d), lambda b, j: (b, 0, 0))] + [_full(a.shape) for a in args[3:]]
    return pl.pallas_call(
        _xattn_kernel,
        grid=(bsz, seq // ts),
        in_specs=in_specs,
        out_specs=[pl.BlockSpec((1, ts, d), lambda b, j: (b, j, 0)),
                   pl.BlockSpec((1, ts, d // 2), lambda b, j: (b, j, 0)),
                   pl.BlockSpec((1, ts, ROUTE_W), lambda b, j: (b, j, 0)),
                   pl.BlockSpec((1, 1, ROUTE_W), lambda b, j: (b, 0, 0))],
        out_shape=[jax.ShapeDtypeStruct((bsz, seq, d), F32),
                   jax.ShapeDtypeStruct((bsz, seq, d // 2), jnp.int32),
                   jax.ShapeDtypeStruct((bsz, seq, ROUTE_W), F32),
                   jax.ShapeDtypeStruct((bsz, 1, ROUTE_W), F32)],
        compiler_params=_cparams(("arbitrary", "arbitrary")),
        name="xattn_router",
    )(*args)


def _moe_output(x2, g0, g1, rec, ln_g, ln_b):
    a_lo, a_hi = _unpack_halves(g0)
    b_lo, b_hi = _unpack_halves(g1)
    w0, w1 = rec[:, 2:3], rec[:, 3:4]
    y = jnp.concatenate([w0 * a_lo + w1 * b_lo, w0 * a_hi + w1 * b_hi], axis=1)
    return _ln(DN_ALPHA * x2 + y, ln_g, ln_b)


def _swa_kernel(x_ref, g0_ref, g1_ref, route_ref, pln_g_ref, pln_b_ref,
                wqkv_ref, bqkv_ref, cos_ref, sin_ref, sink_ref, wo_ref, ln_g_ref, ln_b_ref,
                *rest, ts, cq, ckv):
    o_ref, kprev, vprev = rest[-3:]
    j = pl.program_id(1)
    nb = ts // CHUNK
    dh = C_HEAD_DIM
    grp = (cq // dh) // C_KV_HEADS

    @pl.when(j == 0)
    def _():
        kprev[...] = jnp.zeros_like(kprev)
        vprev[...] = jnp.zeros_like(vprev)

    x = _moe_output(x_ref[0], g0_ref[0, 0], g1_ref[0, 0], route_ref[0], pln_g_ref[...], pln_b_ref[...])
    qkv = _dot(x.astype(BF16), wqkv_ref[...]) + bqkv_ref[...]
    cos = cos_ref[...]
    sin = sin_ref[...]
    lane = lax.broadcasted_iota(jnp.int32, (ts, LANES), 1)
    first_half = (lane % dh) < (dh // 2)

    def rope(t):
        outs = []
        for c in range(t.shape[1] // LANES):
            tc = t[:, c * LANES:(c + 1) * LANES]
            rot = jnp.where(first_half, pltpu.roll(tc, LANES - dh // 2, 1), pltpu.roll(tc, dh // 2, 1))
            outs.append(tc * cos + rot * sin)
        return jnp.concatenate(outs, axis=1)

    q = rope(qkv[:, :cq]).astype(BF16)
    k = rope(qkv[:, cq:cq + 2 * ckv]).astype(BF16)
    v = qkv[:, cq + 2 * ckv:].astype(BF16)

    r_i = lax.broadcasted_iota(jnp.int32, (CHUNK, 2 * CHUNK), 0)
    c_i = lax.broadcasted_iota(jnp.int32, (CHUNK, 2 * CHUNK), 1)
    band = jnp.logical_and(c_i > r_i, c_i <= r_i + CHUNK)
    sink_col = c_i == 0
    lane_k = lax.broadcasted_iota(jnp.int32, (2 * CHUNK, LANES), 1)
    key_row = lax.broadcasted_iota(jnp.int32, (2 * CHUNK, LANES), 0)
    lane_q = lax.broadcasted_iota(jnp.int32, (CHUNK, LANES), 1)
    ones_blk = jnp.ones((2 * CHUNK, LANES), BF16)
    zero_b = jnp.zeros((), BF16)
    blocks = []
    for c in range(nb):
        sl = slice(c * CHUNK, (c + 1) * CHUNK)
        if c == 0:
            kb = jnp.concatenate([kprev[...].astype(BF16), k[sl]], axis=0)
            vb = jnp.concatenate([vprev[...].astype(BF16), v[sl]], axis=0)
            first_key = jnp.where(j > 0, 0, CHUNK)
            valid = jnp.logical_and(band, c_i >= first_key)
        else:
            kb = k[(c - 1) * CHUNK:(c + 1) * CHUNK]
            vb = v[(c - 1) * CHUNK:(c + 1) * CHUNK]
            valid = band
        tiles = []
        for h in range(C_KV_HEADS):
            kd = kb[:, h * LANES:(h + 1) * LANES]
            vd = vb[:, h * LANES:(h + 1) * LANES]
            k_lo = jnp.where(lane_k < dh, kd, zero_b)
            k_hi = jnp.where(lane_k >= dh, kd, zero_b)
            vz = jnp.where(key_row == 0, zero_b, vd)
            q2 = jnp.concatenate([q[sl, (2 * h) * LANES:(2 * h + 1) * LANES],
                                  q[sl, (2 * h + 1) * LANES:(2 * h + 2) * LANES]], axis=0)
            s_lo = _dot_nt(q2, k_lo)
            s_hi = _dot_nt(q2, k_hi)
            parts = []
            for qk, g in ((s_lo[:CHUNK], 0), (s_lo[CHUNK:], 2), (s_hi[:CHUNK], 1), (s_hi[CHUNK:], 3)):
                sink = sink_ref[h * grp + g]
                parts.append(jnp.where(valid, qk, jnp.where(sink_col, sink, NEG)))
            s = jnp.concatenate(parts, axis=0)
            p = jnp.exp2(s - jnp.max(s, axis=1, keepdims=True)).astype(BF16)
            den = _dot(p, ones_blk)
            o2 = _dot(p, vz) / den
            tiles.append(jnp.where(lane_q < dh, o2[:CHUNK], o2[2 * CHUNK:3 * CHUNK]))
            tiles.append(jnp.where(lane_q < dh, o2[CHUNK:2 * CHUNK], o2[3 * CHUNK:]))
        blocks.append(jnp.concatenate(tiles, axis=1))
    kprev[...] = k[(nb - 1) * CHUNK:].astype(F32)
    vprev[...] = v[(nb - 1) * CHUNK:].astype(F32)
    att = (jnp.concatenate(blocks, axis=0) if nb > 1 else blocks[0]).astype(BF16)
    y = _dot(att, wo_ref[...])
    o_ref[0] = _ln(DN_ALPHA * x + y, ln_g_ref[...], ln_b_ref[...])


def swa_mixer_layer(pending, w_qkv, b_qkv, sinks, w_o, ln_g, ln_b, *, ts):
    x, gs, route, pln_g, pln_b, _ = pending
    bsz, seq, d = x.shape
    cq = w_o.shape[0]
    ckv = (w_qkv.shape[1] - cq) // 2
    dh = C_HEAD_DIM
    inv = ROPE_THETA ** (-jnp.arange(0, dh, 2, dtype=F32) / dh)
    ang = jnp.arange(seq, dtype=F32)[:, None] * inv[None, :]
    reps = LANES // (dh // 2)
    sign = jnp.tile(jnp.concatenate([-jnp.ones((dh // 2,), F32), jnp.ones((dh // 2,), F32)]), LANES // dh)
    cos_t = jnp.tile(jnp.cos(ang), (1, reps))
    sin_t = jnp.tile(jnp.sin(ang), (1, reps)) * sign[None, :]
    assert 2 * dh == LANES

    def dup_heads(t):
        th = t.reshape(t.shape[:-1] + (ckv // dh, dh))
        return jnp.concatenate([th, th], axis=-1).reshape(t.shape[:-1] + (2 * ckv,))

    qs = dh ** -0.5 * LOG2E
    w_all = jnp.concatenate([w_qkv[:, :cq] * qs, dup_heads(w_qkv[:, cq:cq + ckv]),
                             dup_heads(w_qkv[:, cq + ckv:])], axis=1)
    b_all = jnp.concatenate([b_qkv[:cq] * qs, dup_heads(b_qkv[cq:cq + ckv]), dup_heads(b_qkv[cq + ckv:])])
    shared = (pln_g.reshape(1, -1), pln_b.reshape(1, -1),
              w_all.astype(BF16), b_all.reshape(1, -1), cos_t, sin_t, sinks.astype(F32) * LOG2E,
              w_o.astype(BF16), ln_g.reshape(1, -1), ln_b.reshape(1, -1))
    bp = bsz // len(gs)
    out = None
    for p, g in enumerate(gs):
        b0 = p * bp
        g = g.reshape(2, bp, seq, d // 2)
        args = [x, g, g, route, *shared]
        in_specs = [pl.BlockSpec((1, ts, d), lambda b, j: (b0 + b, j, 0)),
                    pl.BlockSpec((1, 1, ts, d // 2), lambda b, j: (0, b, j, 0)),
                    pl.BlockSpec((1, 1, ts, d // 2), lambda b, j: (1, b, j, 0)),
                    pl.BlockSpec((1, ts, ROUTE_W), lambda b, j: (b0 + b, j, 0)),
                    _full((1, d)), _full((1, d)),
                    _full(shared[2].shape), _full(shared[3].shape),
                    pl.BlockSpec((ts, LANES), lambda b, j: (j, 0)),
                    pl.BlockSpec((ts, LANES), lambda b, j: (j, 0)),
                    pl.BlockSpec(memory_space=pltpu.SMEM),
                    _full(shared[7].shape), _full(shared[8].shape), _full(shared[9].shape)]
        aliases = {}
        if out is not None:
            args.append(out)
            in_specs.append(pl.BlockSpec(memory_space=pl.ANY))
            aliases = {len(args) - 1: 0}
        out = pl.pallas_call(
            functools.partial(_swa_kernel, ts=ts, cq=cq, ckv=ckv),
            grid=(bp, seq // ts),
            in_specs=in_specs,
            out_specs=pl.BlockSpec((1, ts, d), lambda b, j: (b0 + b, j, 0)),
            out_shape=jax.ShapeDtypeStruct((bsz, seq, d), F32),
            input_output_aliases=aliases,
            scratch_shapes=[pltpu.VMEM((CHUNK, 2 * ckv), F32), pltpu.VMEM((CHUNK, 2 * ckv), F32)],
            compiler_params=_cparams(("arbitrary", "arbitrary")),
            name="swa_mixer",
        )(*args)
    return out


def _slot_kernel(route_ref, pstart_ref, dest_ref, carry_ref, *, tb):
    @pl.when(pl.program_id(0) == 0)
    def _():
        carry_ref[...] = pstart_ref[...]

    rec = route_ref[...]
    lane = lax.broadcasted_iota(jnp.int32, (tb, ROUTE_W), 1)
    e0 = rec[:, 0:1].astype(jnp.int32)
    e1 = rec[:, 1:2].astype(jnp.int32)
    oh0 = lane == e0
    oh1 = lane == e1
    ohs = jnp.where(jnp.logical_or(oh0, oh1), 1.0, 0.0)
    r = lax.broadcasted_iota(jnp.int32, (tb, tb), 0)
    c = lax.broadcasted_iota(jnp.int32, (tb, tb), 1)
    before = jnp.where(c < r, 1.0, 0.0).astype(BF16)
    prefix = _dot(before, ohs.astype(BF16)) + carry_ref[...]
    d0 = jnp.sum(jnp.where(oh0, prefix, 0.0), axis=1, keepdims=True)
    d1 = jnp.sum(jnp.where(oh1, prefix, 0.0), axis=1, keepdims=True)
    dest = jnp.where(lane == 0, d0, jnp.where(lane == 1, d1, 0.0))
    dest_ref[...] = dest.T[:DEST_ROWS].astype(jnp.int32)
    carry_ref[...] += jnp.sum(ohs, axis=0, keepdims=True)


def moe_slots(route, pstart, *, tb):
    n = route.shape[0]
    return pl.pallas_call(
        functools.partial(_slot_kernel, tb=tb),
        grid=(n // tb,),
        in_specs=[pl.BlockSpec((tb, ROUTE_W), lambda i: (i, 0)), _full((1, ROUTE_W))],
        out_specs=pl.BlockSpec((DEST_ROWS, tb), lambda i: (0, i)),
        out_shape=jax.ShapeDtypeStruct((DEST_ROWS, n), jnp.int32),
        scratch_shapes=[pltpu.VMEM((1, ROUTE_W), F32)],
        compiler_params=_cparams(("arbitrary",)),
        name="moe_slots",
    )(route, pstart)


def _ffn_kernel(blk_exp_ref, new_exp_ref, nblk_ref, xs_ref, w1_ref, w3_ref, w2_ref, ys_ref,
                w1_b, w3_b, w2_b):
    i = pl.program_id(0)
    used = i < nblk_ref[0]

    @pl.when(jnp.logical_and(used, new_exp_ref[i] == 1))
    def _():
        w1_b[...] = w1_ref[0, 0].astype(BF16)
        w3_b[...] = w3_ref[0, 0].astype(BF16)
        w2_b[...] = w2_ref[0, 0].astype(BF16)

    @pl.when(used)
    def _():
        x_lo, x_hi = _unpack_halves(xs_ref[...])
        x_lo, x_hi = x_lo.astype(BF16), x_hi.astype(BF16)
        dl = x_lo.shape[1]
        h1 = _dot(x_lo, w1_b[:dl, :]) + _dot(x_hi, w1_b[dl:, :])
        h3 = _dot(x_lo, w3_b[:dl, :]) + _dot(x_hi, w3_b[dl:, :])
        h = (_silu(h1) * h3).astype(BF16)
        ys_ref[...] = _pack_halves(_dot(h, w2_b[...]))

    @pl.when(jnp.logical_not(used))
    def _():
        ys_ref[...] = jnp.zeros_like(ys_ref)


def moe_ffn(xs, blk_exp, new_exp, nblk, w1, w3, w2, *, layer, bm):
    n_pad, dp = xs.shape
    d = 2 * dp
    de = w1.shape[3]
    n_blk = n_pad // bm

    def x_map(i, be, ne, nb):
        return (jnp.minimum(i, nb[0] - 1), 0)

    def w_map(i, be, ne, nb):
        return (layer, be[i], 0, 0)

    return pl.pallas_call(
        _ffn_kernel,
        grid_spec=pltpu.PrefetchScalarGridSpec(
            num_scalar_prefetch=3,
            grid=(n_blk,),
            in_specs=[pl.BlockSpec((bm, dp), x_map),
                      pl.BlockSpec((1, 1, d, de), w_map),
                      pl.BlockSpec((1, 1, d, de), w_map),
                      pl.BlockSpec((1, 1, de, d), w_map)],
            out_specs=pl.BlockSpec((bm, dp), lambda i, be, ne, nb: (i, 0)),
            scratch_shapes=[pltpu.VMEM((d, de), BF16), pltpu.VMEM((d, de), BF16),
                            pltpu.VMEM((de, d), BF16)]),
        out_shape=jax.ShapeDtypeStruct((n_pad, dp), jnp.int32),
        compiler_params=_cparams(("arbitrary",)),
        name="moe_ffn",
    )(blk_exp, new_exp, nblk, xs, w1, w3, w2)


def _combine_kernel(x_ref, g0_ref, g1_ref, route_ref, ln_g_ref, ln_b_ref, *rest):
    o_ref = rest[-1]
    o_ref[...] = _moe_output(x_ref[...], g0_ref[0], g1_ref[0], route_ref[...], ln_g_ref[...], ln_b_ref[...])


def moe_combine(x, g, route, ln_g, ln_b, *, row0, tb, prev=None):
    n, d = x.shape
    m = g.shape[1]
    blk0 = row0 // tb
    row_spec = pl.BlockSpec((tb, d), lambda i: (blk0 + i, 0))
    args = [x, g, g, route, ln_g.reshape(1, -1), ln_b.reshape(1, -1)]
    in_specs = [row_spec,
                pl.BlockSpec((1, tb, d // 2), lambda i: (0, i, 0)),
                pl.BlockSpec((1, tb, d // 2), lambda i: (1, i, 0)),
                pl.BlockSpec((tb, ROUTE_W), lambda i: (blk0 + i, 0)),
                _full((1, d)), _full((1, d))]
    aliases = {}
    if prev is not None:
        args.append(prev)
        in_specs.append(pl.BlockSpec(memory_space=pl.ANY))
        aliases = {len(args) - 1: 0}
    return pl.pallas_call(
        _combine_kernel,
        grid=(m // tb,),
        in_specs=in_specs,
        out_specs=row_spec,
        out_shape=jax.ShapeDtypeStruct((n, d), F32),
        input_output_aliases=aliases,
        compiler_params=_cparams(("arbitrary",)),
        name="moe_combine",
    )(*args)


MOE_BM = 1024
COMBINE_PARTS = 4
MIXER_PARTS = 2
MIXER_TS = 1024


def hierarchical_moe_layer(x2, xp, route, counts, w1, w3, w2, ln_g, ln_b, *, layer, defer_combine):
    bsz, seq, d = x2.shape
    n = bsz * seq
    bm = MOE_BM
    rt = route.reshape(n, ROUTE_W)
    tb = min(512, n)
    n_blk = (2 * n) // bm + N_EXPERTS
    cnt = counts.sum(axis=0)[0, :N_EXPERTS].astype(jnp.int32)
    pcnt = (cnt + bm - 1) // bm * bm
    pends = jnp.cumsum(pcnt)
    pstart = pends - pcnt
    nblk = (pends[-1] // bm).astype(jnp.int32).reshape(1)
    blk_row = jnp.arange(n_blk, dtype=jnp.int32) * bm
    blk_exp = jnp.minimum(jnp.sum((pends[None, :] <= blk_row[:, None]).astype(jnp.int32), axis=1),
                          N_EXPERTS - 1)
    last_exp = blk_exp[jnp.maximum(nblk[0] - 1, 0)]
    blk_exp = jnp.where(jnp.arange(n_blk) < nblk[0], blk_exp, last_exp)
    new_exp = jnp.concatenate([jnp.ones((1,), jnp.int32),
                               (blk_exp[1:] != blk_exp[:-1]).astype(jnp.int32)])
    pstart_rec = jnp.zeros((1, ROUTE_W), F32).at[0, :N_EXPERTS].set(pstart.astype(F32))
    dest = moe_slots(rt, pstart_rec, tb=tb)
    xs = moe_dispatch(xp.reshape(n, d // 2), dest[0], dest[1], n_blk * bm)
    ys = moe_ffn(xs, blk_exp, new_exp, nblk, w1, w3, w2, layer=layer, bm=bm)
    if defer_combine:
        bparts = MIXER_PARTS if bsz % MIXER_PARTS == 0 else 1
        mb = n // bparts
        gs = [moe_gather(ys, dest[0, p * mb:(p + 1) * mb], dest[1, p * mb:(p + 1) * mb]) for p in range(bparts)]
        tie = jnp.minimum(dest[0, 0], 0).astype(F32)
        return x2, gs, route, ln_g, ln_b, tie
    parts = COMBINE_PARTS if n % (COMBINE_PARTS * tb * 8) == 0 else 1
    m = n // parts
    out = None
    for p in range(parts):
        g = moe_gather(ys, dest[0, p * m:(p + 1) * m], dest[1, p * m:(p + 1) * m])
        out = moe_combine(x2.reshape(n, d), g, rt, ln_g, ln_b, row0=p * m, tb=tb, prev=out)
    return out.reshape(bsz, seq, d)


SC_ROWS = 128


def _sc_mesh():
    return plsc.VectorSubcoreMesh(core_axis_name="c", subcore_axis_name="s")


def moe_dispatch(xf, dest0, dest1, n_pad):
    n, d = xf.shape
    info = plsc.get_sparse_core_info()
    nw = info.num_cores * info.num_subcores
    per_w = n // nw
    r = min(SC_ROWS, per_w)

    def body(x_hbm, d0_hbm, d1_hbm, xs_hbm, i0_v, i1_v, rows_v, sem):
        wid = lax.axis_index("s") * info.num_cores + lax.axis_index("c")

        @pl.loop(0, per_w // r)
        def _(c):
            base = pl.multiple_of(wid * per_w + c * r, 8)
            pltpu.sync_copy(d0_hbm.at[pl.ds(base, r)], i0_v)
            pltpu.sync_copy(d1_hbm.at[pl.ds(base, r)], i1_v)
            pltpu.sync_copy(x_hbm.at[pl.ds(base, r)], rows_v)
            pltpu.async_copy(rows_v, xs_hbm.at[i0_v], sem).wait()
            pltpu.async_copy(rows_v, xs_hbm.at[i1_v], sem).wait()

    return pl.kernel(
        body, out_type=jax.ShapeDtypeStruct((n_pad, d), xf.dtype), mesh=_sc_mesh(),
        scratch_types=[pltpu.VMEM((r,), jnp.int32), pltpu.VMEM((r,), jnp.int32),
                       pltpu.VMEM((r, d), xf.dtype), pltpu.SemaphoreType.DMA],
        name="moe_dispatch",
    )(xf, dest0, dest1)


def moe_gather(ys, dest0, dest1):
    n = dest0.shape[0]
    d = ys.shape[1]
    info = plsc.get_sparse_core_info()
    nw = info.num_cores * info.num_subcores
    per_w = n // nw
    r = min(SC_ROWS, per_w)

    def body(ys_hbm, d0_hbm, d1_hbm, g_hbm, i_v, rows_v, sem):
        wid = lax.axis_index("s") * info.num_cores + lax.axis_index("c")

        @pl.loop(0, per_w // r)
        def _(c):
            base = pl.multiple_of(wid * per_w + c * r, 8)
            for k, d_hbm in enumerate((d0_hbm, d1_hbm)):
                pltpu.sync_copy(d_hbm.at[pl.ds(base, r)], i_v)
                pltpu.async_copy(ys_hbm.at[i_v], rows_v, sem).wait()
                pltpu.sync_copy(rows_v, g_hbm.at[k, pl.ds(base, r)])

    return pl.kernel(
        body, out_type=jax.ShapeDtypeStruct((2, n, d), ys.dtype), mesh=_sc_mesh(),
        scratch_types=[pltpu.VMEM((r,), jnp.int32), pltpu.VMEM((r, d), ys.dtype),
                       pltpu.SemaphoreType.DMA],
        name="moe_gather",
    )(ys, dest0, dest1)


def kernel(x, mem, ln_g, ln_b, ev_w_in, ev_gm_ln_g, ev_gm_ln_b, ev_gm_ws, ev_gm_bs, ev_conv_w, ev_conv_b, ev_wq, ev_wk, ev_wv, ev_w_if, ev_b_if, ev_norm_w, ev_skip, ev_w_out, od_w_qkv, od_b_qkv, od_sinks, od_w_o, xa_wq, xa_wkv, xa_wo, moe_w_rg, moe_b_rg, moe_w_re, moe_b_re, moe_w1, moe_w3, moe_w2):
    bsz, seq, d = x.shape
    depth = ln_g.shape[0]
    assert depth == DEPTH
    ts = min(MIXER_TS, seq)
    mem_l = mem
    for l in range(depth):
        if l % 2 == 0:
            e = l // 2
            x = even_mixer_layer(x, ev_w_in[e], ev_gm_ln_g[e], ev_gm_ln_b[e], ev_gm_ws[e], ev_gm_bs[e],
                                 ev_conv_w[e], ev_conv_b[e], ev_wq[e], ev_wk[e], ev_wv[e], ev_w_if[e],
                                 ev_b_if[e], ev_norm_w[e], ev_skip[e], ev_w_out[e],
                                 ln_g[l, 0], ln_b[l, 0], ts=ts)
        else:
            o = l // 2
            mem_l = mem + x[-1]
            x = swa_mixer_layer(x, od_w_qkv[o], od_b_qkv[o], od_sinks[o], od_w_o[o],
                                ln_g[l, 0], ln_b[l, 0], ts=ts)
        wqk, vo = memory_fold(mem_l, xa_wkv[l], xa_wq[l], xa_wo[l])
        x, xp, route, counts = xattn_router_layer(x, wqk, vo, ln_g[l, 1], ln_b[l, 1],
                                                  moe_w_rg[l], moe_b_rg[l], moe_w_re[l], moe_b_re[l],
                                                  ts=min(XATTN_TS, seq))
        x = hierarchical_moe_layer(x, xp, route, counts, moe_w1, moe_w3, moe_w2,
                                   ln_g[l, 2], ln_b[l, 2], layer=l,
                                   defer_combine=(l + 1 < depth and (l + 1) % 2 == 1))
    return x
```

```python
import functools
import math

import jax
import jax.numpy as jnp
from jax import lax
from jax.experimental import pallas as pl
from jax.experimental.pallas import tpu as pltpu
from jax.experimental.pallas import tpu_sc as plsc

F32 = jnp.float32
BF16 = jnp.bfloat16

A_GROUPS = 4
CHUNK = 128
B_HEADS = 4
B_CONV = 4
C_HEAD_DIM = 64
C_KV_HEADS = 4
X_HEADS = 4
N_GROUPS = 4
EXPERTS_PER_GROUP = 8
N_EXPERTS = N_GROUPS * EXPERTS_PER_GROUP
ROPE_THETA = 10000.0
LN_EPS = 1e-5
DEPTH = 2
DN_ALPHA = (2 * DEPTH) ** 0.25

LANES = 128
VMEM_LIMIT = 48 * 1024 * 1024
NEG = -1e30


def _cparams(sem):
    return pltpu.CompilerParams(dimension_semantics=sem, vmem_limit_bytes=VMEM_LIMIT)


def _full(shape):
    nd = len(shape)
    return pl.BlockSpec(shape, lambda *_: (0,) * nd)


def _dot(a, b):
    return jnp.dot(a, b, preferred_element_type=F32)


def _dot_nt(a, b):
    return lax.dot_general(a, b, (((1,), (1,)), ((), ())), preferred_element_type=F32)


def _split_dot(a, b_bf16):
    hi = a.astype(BF16)
    lo = (a - hi.astype(F32)).astype(BF16)
    return _dot(hi, b_bf16) + _dot(lo, b_bf16)


def _ln(x, g, b):
    mu = jnp.mean(x, axis=-1, keepdims=True)
    xc = x - mu
    var = jnp.mean(xc * xc, axis=-1, keepdims=True)
    return xc * lax.rsqrt(var + LN_EPS) * g + b


LOG2E = math.log2(math.e)


def _silu(x):
    return x * (1.0 / (1.0 + jnp.exp2(x * -LOG2E)))


def _gelu(x):
    return 0.5 * x * (1.0 + jnp.tanh(math.sqrt(2.0 / math.pi) * (x + 0.044715 * (x * x * x))))


def _pack_halves(x):
    c = x.shape[1] // 2
    lo = lax.bitcast_convert_type(x[:, :c].astype(BF16).astype(F32), jnp.uint32)
    hi = lax.bitcast_convert_type(x[:, c:].astype(BF16).astype(F32), jnp.uint32)
    return lax.bitcast_convert_type((lo >> 16) | hi, jnp.int32)


def _unpack_halves(p):
    u = lax.bitcast_convert_type(p, jnp.uint32)
    lo = lax.bitcast_convert_type(u << 16, F32)
    hi = lax.bitcast_convert_type(u & jnp.uint32(0xFFFF0000), F32)
    return lo, hi


def _log_sigmoid(x):
    return jnp.minimum(x, 0.0) - jnp.log(1.0 + jnp.exp(-jnp.abs(x)))


def _even_kernel(x_ref, w_in_ref, gm_g_ref, gm_b_ref, gm_w_ref, gm_bias_ref,
                 conv_w_ref, conv_b_ref, wq_ref, wk_ref, wv_ref, wif_t_ref, bif_t_ref,
                 norm_w_ref, skip_ref, w_out_ref, ln_g_ref, ln_b_ref,
                 o_ref,
                 xm_buf, ct_ref, m_ref, *, ts, aw, bw):
    dh = bw // B_HEADS
    agd = aw // A_GROUPS
    nck = ts // CHUNK
    pad = 8
    j = pl.program_id(1)

    @pl.when(j == 0)
    def _():
        xm_buf[0:pad, :] = jnp.zeros((pad, bw), F32)
        ct_ref[...] = jnp.zeros_like(ct_ref)
        m_ref[...] = jnp.zeros_like(m_ref)

    row = lax.broadcasted_iota(jnp.int32, (CHUNK, CHUNK), 0)
    col = lax.broadcasted_iota(jnp.int32, (CHUNK, CHUNK), 1)
    causal = col <= row
    diag = col == row
    triu = jnp.where(row <= col, 1.0, 0.0).astype(BF16)
    ones_blk = jnp.ones((CHUNK, LANES), BF16)

    x = x_ref[0]
    proj = _dot(x.astype(BF16), w_in_ref[...])
    a_u = _gelu(proj[:, :aw])
    a_v = _gelu(proj[:, aw:2 * aw])
    xm = proj[:, 2 * aw:2 * aw + bw]
    z = proj[:, 2 * aw + bw:]

    vn = _ln(a_v, gm_g_ref[...], gm_b_ref[...]).astype(BF16)
    ya_chunks = []
    for c in range(nck):
        cols = []
        for g in range(A_GROUPS):
            v_cg = vn[c * CHUNK:(c + 1) * CHUNK, g * agd:(g + 1) * agd]
            cols.append(_dot(gm_w_ref[g], v_cg))
        ya_chunks.append(jnp.concatenate(cols, axis=1) + gm_bias_ref[...])
    y_a = a_u * jnp.concatenate(ya_chunks, axis=0)

    xm_buf[pad:pad + ts, :] = xm
    conv = conv_b_ref[...] + conv_w_ref[B_CONV - 1:B_CONV, :] * xm
    for k in range(B_CONV - 1):
        sh = B_CONV - 1 - k
        conv = conv + conv_w_ref[k:k + 1, :] * xm_buf[pad - sh:pad - sh + ts, :]
    xm_buf[pad - (B_CONV - 1):pad, :] = xm_buf[pad + ts - (B_CONV - 1):pad + ts, :]
    xc = _silu(conv)
    xc_b = xc.astype(BF16)
    q = _dot_head_pairs(xc_b, wq_ref)
    k_ = _dot_head_pairs(xc_b, wk_ref)
    v = _dot_head_pairs(xm.astype(BF16), wv_ref)
    gate_in = jnp.concatenate([q, k_, v], axis=1).astype(BF16)
    gates_t = _dot_nt(wif_t_ref[...], gate_in) + bif_t_ref[...]
    ig_all = gates_t[:B_HEADS, :]
    lf_all = _log_sigmoid(gates_t[B_HEADS:, :])
    q_b = q.astype(BF16)
    k_b = (k_ * dh ** -0.5).astype(BF16)
    v_b = v.astype(BF16)
    gate_z = _silu(z)

    lf_c = [lf_all[:, c * CHUNK:(c + 1) * CHUNK] for c in range(nck)]
    b_all = _split_dot(jnp.concatenate(lf_c, axis=0), triu)
    m_prev = m_ref[...]
    a_c, m_c, decay_c, wg_c = [], [], [], []
    for c in range(nck):
        b_r = b_all[c * B_HEADS:(c + 1) * B_HEADS]
        a_r = ig_all[:, c * CHUNK:(c + 1) * CHUNK] - b_r
        bl = b_r[:, CHUNK - 1:CHUNK]
        g_r = bl + a_r
        m_new = jnp.maximum(bl + m_prev, jnp.max(g_r, axis=1, keepdims=True))
        a_c.append(a_r)
        m_c.append(m_prev)
        decay_c.append(jnp.exp(bl + m_prev - m_new))
        wg_c.append(jnp.exp(g_r - m_new))
        m_prev = m_new
    m_ref[...] = m_prev

    units = [(c, h) for c in range(nck) for h in range(B_HEADS)]
    lmat = jnp.concatenate([jnp.where(causal, lf_c[c][h:h + 1, :], 0.0) for c, h in units], axis=0)
    dgm = jnp.concatenate([jnp.where(diag, wg_c[c][h:h + 1, :], 0.0) for c, h in units], axis=0)
    b_t_all = _split_dot(lmat, ones_blk)
    wg_t_all = _dot(dgm.astype(BF16), ones_blk)

    caugs = [ct_ref[h] for h in range(B_HEADS)]
    h_chunks = []
    for c in range(nck):
        sl = slice(c * CHUNK, (c + 1) * CHUNK)
        heads = []
        for h in range(B_HEADS):
            u = c * B_HEADS + h
            us = slice(u * CHUNK, (u + 1) * CHUNK)
            hs = slice(h * dh, (h + 1) * dh)
            qh, kh = q_b[sl, hs], k_b[sl, hs]
            vaug = jnp.concatenate([v_b[sl, hs], ones_blk], axis=1)
            m_row = m_c[c][h:h + 1, :]
            amat = jnp.where(causal, a_c[c][h:h + 1, :], NEG)
            mx = jnp.maximum(jnp.max(amat, axis=1, keepdims=True), m_row)
            w_intra = jnp.exp(amat - mx)
            w_state = jnp.exp(m_row - mx)
            s = _dot_nt(qh, kh) * w_intra
            kw = (kh.astype(F32) * wg_t_all[us]).astype(BF16)
            both = _dot(jnp.concatenate([s.astype(BF16), kw.T], axis=0), vaug)
            caug = caugs[h]
            naug = both[:CHUNK] + jnp.concatenate([w_state, w_state], axis=1) * _dot(qh, caug.astype(BF16))
            num, nq = naug[:, :dh], naug[:, dh:]
            hv = num / jnp.maximum(jnp.abs(nq), jnp.exp(-(b_t_all[us] + mx)))
            decay = decay_c[c][h:h + 1, :]
            caugs[h] = jnp.concatenate([decay, decay], axis=1) * caug + both[CHUNK:]
            hc = hv - jnp.mean(hv, axis=1, keepdims=True)
            hn = hc * lax.rsqrt(jnp.mean(hc * hc, axis=1, keepdims=True) + LN_EPS)
            heads.append(hn)
        h_chunks.append(jnp.concatenate(heads, axis=1))
    for h in range(B_HEADS):
        ct_ref[h] = caugs[h]
    hn_all = jnp.concatenate(h_chunks, axis=0) if nck > 1 else h_chunks[0]
    y_b = (hn_all * norm_w_ref[...] + skip_ref[...] * xc) * gate_z

    mix = jnp.concatenate([y_a, y_b], axis=1).astype(BF16)
    y = _dot(mix, w_out_ref[...])
    o_ref[0] = _ln(DN_ALPHA * x + y, ln_g_ref[...], ln_b_ref[...])


def _block_diag_pairs(w):
    hh, d, _ = w.shape
    wp = w.reshape(hh // 2, 2, d, d)
    eye = jnp.eye(2, dtype=w.dtype)
    return jnp.einsum('pade,ab->padbe', wp, eye).reshape(hh // 2, 2 * d, 2 * d)


def _dot_head_pairs(x, w_ref):
    npair, w2, _ = w_ref.shape
    return jnp.concatenate([_dot(x[:, p * w2:(p + 1) * w2], w_ref[p]) for p in range(npair)], axis=1)


def even_mixer_layer(x, w_in, gm_ln_g, gm_ln_b, gm_ws, gm_bs, conv_w, conv_b, wq, wk, wv, w_if,
                     b_if, norm_w, skip, w_out, ln_g, ln_b, *, ts):
    bsz, seq, d = x.shape
    aw = gm_ln_g.shape[0]
    bw = conv_b.shape[0]
    agd = aw // A_GROUPS
    causal = jnp.tril(jnp.ones((CHUNK, CHUNK), dtype=bool))
    gm_w = jnp.where(causal[None], gm_ws, 0.0).astype(BF16)
    gm_bias = jnp.repeat(gm_bs.T, agd, axis=1)
    row = lambda a: a.reshape(1, -1)
    args = (x, w_in.astype(BF16), row(gm_ln_g), row(gm_ln_b), gm_w, gm_bias,
            conv_w, row(conv_b), _block_diag_pairs(wq).astype(BF16), _block_diag_pairs(wk).astype(BF16),
            _block_diag_pairs(wv).astype(BF16), w_if.T.astype(BF16),
            b_if.reshape(-1, 1), row(norm_w), row(skip), w_out.astype(BF16),
            row(ln_g), row(ln_b))
    in_specs = [pl.BlockSpec((1, ts, d), lambda b, j: (b, j, 0))] + [_full(a.shape) for a in args[1:]]
    dh = bw // B_HEADS
    assert dh == LANES and CHUNK == LANES
    return pl.pallas_call(
        functools.partial(_even_kernel, ts=ts, aw=aw, bw=bw),
        grid=(bsz, seq // ts),
        in_specs=in_specs,
        out_specs=pl.BlockSpec((1, ts, d), lambda b, j: (b, j, 0)),
        out_shape=jax.ShapeDtypeStruct((bsz, seq, d), F32),
        scratch_shapes=[pltpu.VMEM((8 + ts, bw), F32),
                        pltpu.VMEM((B_HEADS, dh, dh + LANES), F32),
                        pltpu.VMEM((B_HEADS, LANES), F32)],
        compiler_params=_cparams(("arbitrary", "arbitrary")),
        name="even_mixer",
    )(*args)


def _memfold_kernel(mem_ref, wkv_ref, wq_ref, wo_ref, wqk_ref, vo_ref, *, d):
    dh = d // X_HEADS
    m_len = mem_ref.shape[1]
    kv = _dot(mem_ref[0].astype(BF16), wkv_ref[...])
    k = (kv[:, :d] * (dh ** -0.5 * LOG2E)).astype(BF16)
    v = kv[:, d:].astype(BF16)
    for h in range(X_HEADS):
        hs = slice(h * dh, (h + 1) * dh)
        ms = slice(h * m_len, (h + 1) * m_len)
        wqk_ref[0, :, ms] = _dot_nt(wq_ref[:, hs], k[:, hs]).astype(BF16)
        vo_ref[0, ms, :] = _dot(v[:, hs], wo_ref[hs, :]).astype(BF16)


def memory_fold(mem, wkv, wq, wo):
    bsz, m_len, d = mem.shape
    hm = X_HEADS * m_len
    args = (mem, wkv.astype(BF16), wq.astype(BF16), wo.astype(BF16))
    return pl.pallas_call(
        functools.partial(_memfold_kernel, d=d),
        grid=(bsz,),
        in_specs=[pl.BlockSpec((1, m_len, d), lambda b: (b, 0, 0))] + [_full(a.shape) for a in args[1:]],
        out_specs=[pl.BlockSpec((1, d, hm), lambda b: (b, 0, 0)),
                   pl.BlockSpec((1, hm, d), lambda b: (b, 0, 0))],
        out_shape=[jax.ShapeDtypeStruct((bsz, d, hm), BF16),
                   jax.ShapeDtypeStruct((bsz, hm, d), BF16)],
        compiler_params=_cparams(("arbitrary",)),
        name="memory_fold",
    )(*args)


ROUTE_W = 128
XATTN_TS = 1024
DEST_ROWS = 8


def _xattn_kernel(x_ref, wqk_ref, vo_ref, ln_g_ref, ln_b_ref, wr_ref, br_ref,
                  o_ref, xp_ref, route_ref, cnt_ref):
    m_len = wqk_ref.shape[2] // X_HEADS

    @pl.when(pl.program_id(1) == 0)
    def _():
        cnt_ref[...] = jnp.zeros_like(cnt_ref)

    x = x_ref[0]
    ts = x.shape[0]
    xb = x.astype(BF16)
    probs = []
    for h in range(X_HEADS):
        s = _dot(xb, wqk_ref[0, :, h * m_len:(h + 1) * m_len])
        p = jnp.exp2(s - jnp.max(s, axis=1, keepdims=True))
        probs.append((p / jnp.sum(p, axis=1, keepdims=True)).astype(BF16))
    y = _dot(jnp.concatenate(probs, axis=1), vo_ref[0])
    x2 = _ln(DN_ALPHA * x + y, ln_g_ref[...], ln_b_ref[...])
    o_ref[0] = x2
    xp_ref[0] = _pack_halves(x2)

    x_hi = x2.astype(BF16)
    x_lo = (x2 - x_hi.astype(F32)).astype(BF16)
    logits = (_dot(x_hi, wr_ref[0]) + _dot(x_lo, wr_ref[0]) + _dot(x_hi, wr_ref[1])) + br_ref[...]
    lane = lax.broadcasted_iota(jnp.int32, (ts, ROUTE_W), 1)
    is_g = lane < N_GROUPS
    lg = jnp.where(is_g, logits, NEG)
    mg = jnp.max(lg, axis=1, keepdims=True)
    gi = jnp.min(jnp.where(jnp.logical_and(is_g, lg == mg), lane, ROUTE_W), axis=1, keepdims=True)
    gate_g = 1.0 / jnp.sum(jnp.where(is_g, jnp.exp(lg - mg), 0.0), axis=1, keepdims=True)
    lo = N_GROUPS + gi * EXPERTS_PER_GROUP
    in_grp = jnp.logical_and(lane >= lo, lane < lo + EXPERTS_PER_GROUP)
    le = jnp.where(in_grp, logits, NEG)
    v1 = jnp.max(le, axis=1, keepdims=True)
    i1 = jnp.min(jnp.where(jnp.logical_and(in_grp, le == v1), lane, ROUTE_W), axis=1, keepdims=True)
    le2 = jnp.where(lane == i1, NEG, le)
    v2 = jnp.max(le2, axis=1, keepdims=True)
    i2 = jnp.min(jnp.where(jnp.logical_and(in_grp, le2 == v2), lane, ROUTE_W), axis=1, keepdims=True)
    e21 = jnp.exp(v2 - v1)
    p1 = 1.0 / (1.0 + e21)
    p2 = e21 * p1
    e1 = (i1 - N_GROUPS).astype(F32)
    e2 = (i2 - N_GROUPS).astype(F32)
    rec = jnp.where(lane == 0, e1, 0.0)
    rec = jnp.where(lane == 1, e2, rec)
    rec = jnp.where(lane == 2, gate_g * p1, rec)
    rec = jnp.where(lane == 3, gate_g * p2, rec)
    route_ref[0] = rec
    sel = jnp.logical_or(lane == i1 - N_GROUPS, lane == i2 - N_GROUPS)
    cnt_ref[0] += jnp.sum(jnp.where(sel, 1.0, 0.0), axis=0, keepdims=True)


def xattn_router_layer(x, wqk, vo, ln_g, ln_b, w_rg, b_rg, w_re, b_re, *, ts):
    bsz, seq, d = x.shape
    hm = wqk.shape[2]
    wr = jnp.zeros((d, ROUTE_W), F32).at[:, :N_GROUPS].set(w_rg).at[:, N_GROUPS:N_GROUPS + N_EXPERTS].set(w_re)
    br = jnp.zeros((1, ROUTE_W), F32).at[0, :N_GROUPS].set(b_rg).at[0, N_GROUPS:N_GROUPS + N_EXPERTS].set(b_re)
    wr_hi = wr.astype(BF16)
    wr_lo = (wr - wr_hi.astype(F32)).astype(BF16)
    wr = jnp.stack([wr_hi, wr_lo])
    args = (x, wqk, vo, ln_g.reshape(1, -1), ln_b.reshape(1, -1), wr, br)
    in_specs = [pl.BlockSpec((1, ts, d), lambda b, j: (b, j, 0)),
                pl.BlockSpec((1, d, hm), lambda b, j: (b, 0, 0)),
                pl.BlockSpec((1, hm, d), lambda b, j: (b, 0, 0))] + [_full(a.shape) for a in args[3:]]
    return pl.pallas_call(
        _xattn_kernel,
        grid=(bsz, seq // ts),
        in_specs=in_specs,
        out_specs=[pl.BlockSpec((1, ts, d), lambda b, j: (b, j, 0)),
                   pl.BlockSpec((1, ts, d // 2), lambda b, j: (b, j, 0)),
                   pl.BlockSpec((1, ts, ROUTE_W), lambda b, j: (b, j, 0)),
                   pl.BlockSpec((1, 1, ROUTE_W), lambda b, j: (b, 0, 0))],
        out_shape=[jax.ShapeDtypeStruct((bsz, seq, d), F32),
                   jax.ShapeDtypeStruct((bsz, seq, d // 2), jnp.int32),
                   jax.ShapeDtypeStruct((bsz, seq, ROUTE_W), F32),
                   jax.ShapeDtypeStruct((bsz, 1, ROUTE_W), F32)],
        compiler_params=_cparams(("arbitrary", "arbitrary")),
        name="xattn_router",
    )(*args)


def _moe_output(x2, g0, g1, rec, ln_g, ln_b):
    a_lo, a_hi = _unpack_halves(g0)
    b_lo, b_hi = _unpack_halves(g1)
    w0, w1 = rec[:, 2:3], rec[:, 3:4]
    y = jnp.concatenate([w0 * a_lo + w1 * b_lo, w0 * a_hi + w1 * b_hi], axis=1)
    return _ln(DN_ALPHA * x2 + y, ln_g, ln_b)


def _swa_kernel(x_ref, g0_ref, g1_ref, route_ref, pln_g_ref, pln_b_ref,
                wqkv_ref, bqkv_ref, cos_ref, sin_ref, sink_ref, wo_ref, ln_g_ref, ln_b_ref,
                *rest, ts, cq, ckv):
    o_ref, kprev, vprev = rest[-3:]
    j = pl.program_id(1)
    nb = ts // CHUNK
    dh = C_HEAD_DIM
    grp = (cq // dh) // C_KV_HEADS

    @pl.when(j == 0)
    def _():
        kprev[...] = jnp.zeros_like(kprev)
        vprev[...] = jnp.zeros_like(vprev)

    x = _moe_output(x_ref[0], g0_ref[0, 0], g1_ref[0, 0], route_ref[0], pln_g_ref[...], pln_b_ref[...])
    qkv = _dot(x.astype(BF16), wqkv_ref[...]) + bqkv_ref[...]
    cos = cos_ref[...]
    sin = sin_ref[...]
    lane = lax.broadcasted_iota(jnp.int32, (ts, LANES), 1)
    first_half = (lane % dh) < (dh // 2)

    def rope(t):
        outs = []
        for c in range(t.shape[1] // LANES):
            tc = t[:, c * LANES:(c + 1) * LANES]
            rot = jnp.where(first_half, pltpu.roll(tc, LANES - dh // 2, 1), pltpu.roll(tc, dh // 2, 1))
            outs.append(tc * cos + rot * sin)
        return jnp.concatenate(outs, axis=1)

    q = rope(qkv[:, :cq]).astype(BF16)
    k = rope(qkv[:, cq:cq + 2 * ckv]).astype(BF16)
    v = qkv[:, cq + 2 * ckv:].astype(BF16)

    r_i = lax.broadcasted_iota(jnp.int32, (CHUNK, 2 * CHUNK), 0)
    c_i = lax.broadcasted_iota(jnp.int32, (CHUNK, 2 * CHUNK), 1)
    band = jnp.logical_and(c_i > r_i, c_i <= r_i + CHUNK)
    sink_col = c_i == 0
    lane_k = lax.broadcasted_iota(jnp.int32, (2 * CHUNK, LANES), 1)
    key_row = lax.broadcasted_iota(jnp.int32, (2 * CHUNK, LANES), 0)
    lane_q = lax.broadcasted_iota(jnp.int32, (CHUNK, LANES), 1)
    ones_blk = jnp.ones((2 * CHUNK, LANES), BF16)
    zero_b = jnp.zeros((), BF16)
    blocks = []
    for c in range(nb):
        sl = slice(c * CHUNK, (c + 1) * CHUNK)
        if c == 0:
            kb = jnp.concatenate([kprev[...].astype(BF16), k[sl]], axis=0)
            vb = jnp.concatenate([vprev[...].astype(BF16), v[sl]], axis=0)
            first_key = jnp.where(j > 0, 0, CHUNK)
            valid = jnp.logical_and(band, c_i >= first_key)
        else:
            kb = k[(c - 1) * CHUNK:(c + 1) * CHUNK]
            vb = v[(c - 1) * CHUNK:(c + 1) * CHUNK]
            valid = band
        tiles = []
        for h in range(C_KV_HEADS):
            kd = kb[:, h * LANES:(h + 1) * LANES]
            vd = vb[:, h * LANES:(h + 1) * LANES]
            k_lo = jnp.where(lane_k < dh, kd, zero_b)
            k_hi = jnp.where(lane_k >= dh, kd, zero_b)
            vz = jnp.where(key_row == 0, zero_b, vd)
            q2 = jnp.concatenate([q[sl, (2 * h) * LANES:(2 * h + 1) * LANES],
                                  q[sl, (2 * h + 1) * LANES:(2 * h + 2) * LANES]], axis=0)
            s_lo = _dot_nt(q2, k_lo)
            s_hi = _dot_nt(q2, k_hi)
            parts = []
            for qk, g in ((s_lo[:CHUNK], 0), (s_lo[CHUNK:], 2), (s_hi[:CHUNK], 1), (s_hi[CHUNK:], 3)):
                sink = sink_ref[h * grp + g]
                parts.append(jnp.where(valid, qk, jnp.where(sink_col, sink, NEG)))
            s = jnp.concatenate(parts, axis=0)
            p = jnp.exp2(s - jnp.max(s, axis=1, keepdims=True)).astype(BF16)
            den = _dot(p, ones_blk)
            o2 = _dot(p, vz) / den
            tiles.append(jnp.where(lane_q < dh, o2[:CHUNK], o2[2 * CHUNK:3 * CHUNK]))
            tiles.append(jnp.where(lane_q < dh, o2[CHUNK:2 * CHUNK], o2[3 * CHUNK:]))
        blocks.append(jnp.concatenate(tiles, axis=1))
    kprev[...] = k[(nb - 1) * CHUNK:].astype(F32)
    vprev[...] = v[(nb - 1) * CHUNK:].astype(F32)
    att = (jnp.concatenate(blocks, axis=0) if nb > 1 else blocks[0]).astype(BF16)
    y = _dot(att, wo_ref[...])
    o_ref[0] = _ln(DN_ALPHA * x + y, ln_g_ref[...], ln_b_ref[...])


def swa_mixer_layer(pending, w_qkv, b_qkv, sinks, w_o, ln_g, ln_b, *, ts):
    x, gs, route, pln_g, pln_b = pending
    bsz, seq, d = x.shape
    cq = w_o.shape[0]
    ckv = (w_qkv.shape[1] - cq) // 2
    dh = C_HEAD_DIM
    inv = ROPE_THETA ** (-jnp.arange(0, dh, 2, dtype=F32) / dh)
    ang = jnp.arange(seq, dtype=F32)[:, None] * inv[None, :]
    reps = LANES // (dh // 2)
    sign = jnp.tile(jnp.concatenate([-jnp.ones((dh // 2,), F32), jnp.ones((dh // 2,), F32)]), LANES // dh)
    cos_t = jnp.tile(jnp.cos(ang), (1, reps))
    sin_t = jnp.tile(jnp.sin(ang), (1, reps)) * sign[None, :]
    assert 2 * dh == LANES

    def dup_heads(t):
        th = t.reshape(t.shape[:-1] + (ckv // dh, dh))
        return jnp.concatenate([th, th], axis=-1).reshape(t.shape[:-1] + (2 * ckv,))

    qs = dh ** -0.5 * LOG2E
    w_all = jnp.concatenate([w_qkv[:, :cq] * qs, dup_heads(w_qkv[:, cq:cq + ckv]),
                             dup_heads(w_qkv[:, cq + ckv:])], axis=1)
    b_all = jnp.concatenate([b_qkv[:cq] * qs, dup_heads(b_qkv[cq:cq + ckv]), dup_heads(b_qkv[cq + ckv:])])
    shared = (pln_g.reshape(1, -1), pln_b.reshape(1, -1),
              w_all.astype(BF16), b_all.reshape(1, -1), cos_t, sin_t, sinks.astype(F32) * LOG2E,
              w_o.astype(BF16), ln_g.reshape(1, -1), ln_b.reshape(1, -1))
    bp = bsz // len(gs)
    out = None
    for p, g in enumerate(gs):
        b0 = p * bp
        g = g.reshape(2, bp, seq, d // 2)
        args = [x, g, g, route, *shared]
        in_specs = [pl.BlockSpec((1, ts, d), lambda b, j: (b0 + b, j, 0)),
                    pl.BlockSpec((1, 1, ts, d // 2), lambda b, j: (0, b, j, 0)),
                    pl.BlockSpec((1, 1, ts, d // 2), lambda b, j: (1, b, j, 0)),
                    pl.BlockSpec((1, ts, ROUTE_W), lambda b, j: (b0 + b, j, 0)),
                    _full((1, d)), _full((1, d)),
                    _full(shared[2].shape), _full(shared[3].shape),
                    pl.BlockSpec((ts, LANES), lambda b, j: (j, 0)),
                    pl.BlockSpec((ts, LANES), lambda b, j: (j, 0)),
                    pl.BlockSpec(memory_space=pltpu.SMEM),
                    _full(shared[7].shape), _full(shared[8].shape), _full(shared[9].shape)]
        aliases = {}
        if out is not None:
            args.append(out)
            in_specs.append(pl.BlockSpec(memory_space=pl.ANY))
            aliases = {len(args) - 1: 0}
        out = pl.pallas_call(
            functools.partial(_swa_kernel, ts=ts, cq=cq, ckv=ckv),
            grid=(bp, seq // ts),
            in_specs=in_specs,
            out_specs=pl.BlockSpec((1, ts, d), lambda b, j: (b0 + b, j, 0)),
            out_shape=jax.ShapeDtypeStruct((bsz, seq, d), F32),
            input_output_aliases=aliases,
            scratch_shapes=[pltpu.VMEM((CHUNK, 2 * ckv), F32), pltpu.VMEM((CHUNK, 2 * ckv), F32)],
            compiler_params=_cparams(("arbitrary", "arbitrary")),
            name="swa_mixer",
        )(*args)
    return out


def _slot_kernel(route_ref, pstart_ref, dest_ref, carry_ref, *, tb):
    @pl.when(pl.program_id(0) == 0)
    def _():
        carry_ref[...] = pstart_ref[...]

    rec = route_ref[...]
    lane = lax.broadcasted_iota(jnp.int32, (tb, ROUTE_W), 1)
    e0 = rec[:, 0:1].astype(jnp.int32)
    e1 = rec[:, 1:2].astype(jnp.int32)
    oh0 = lane == e0
    oh1 = lane == e1
    ohs = jnp.where(jnp.logical_or(oh0, oh1), 1.0, 0.0)
    r = lax.broadcasted_iota(jnp.int32, (tb, tb), 0)
    c = lax.broadcasted_iota(jnp.int32, (tb, tb), 1)
    before = jnp.where(c < r, 1.0, 0.0).astype(BF16)
    prefix = _dot(before, ohs.astype(BF16)) + carry_ref[...]
    d0 = jnp.sum(jnp.where(oh0, prefix, 0.0), axis=1, keepdims=True)
    d1 = jnp.sum(jnp.where(oh1, prefix, 0.0), axis=1, keepdims=True)
    dest = jnp.where(lane == 0, d0, jnp.where(lane == 1, d1, 0.0))
    dest_ref[...] = dest.T[:DEST_ROWS].astype(jnp.int32)
    carry_ref[...] += jnp.sum(ohs, axis=0, keepdims=True)


def moe_slots(route, pstart, *, tb):
    n = route.shape[0]
    return pl.pallas_call(
        functools.partial(_slot_kernel, tb=tb),
        grid=(n // tb,),
        in_specs=[pl.BlockSpec((tb, ROUTE_W), lambda i: (i, 0)), _full((1, ROUTE_W))],
        out_specs=pl.BlockSpec((DEST_ROWS, tb), lambda i: (0, i)),
        out_shape=jax.ShapeDtypeStruct((DEST_ROWS, n), jnp.int32),
        scratch_shapes=[pltpu.VMEM((1, ROUTE_W), F32)],
        compiler_params=_cparams(("arbitrary",)),
        name="moe_slots",
    )(route, pstart)


def _ffn_kernel(blk_exp_ref, new_exp_ref, nblk_ref, xs_ref, w1_ref, w3_ref, w2_ref, ys_ref,
                w1_b, w3_b, w2_b):
    i = pl.program_id(0)
    used = i < nblk_ref[0]

    @pl.when(jnp.logical_and(used, new_exp_ref[i] == 1))
    def _():
        w1_b[...] = w1_ref[0, 0].astype(BF16)
        w3_b[...] = w3_ref[0, 0].astype(BF16)
        w2_b[...] = w2_ref[0, 0].astype(BF16)

    @pl.when(used)
    def _():
        x_lo, x_hi = _unpack_halves(xs_ref[...])
        x_lo, x_hi = x_lo.astype(BF16), x_hi.astype(BF16)
        dl = x_lo.shape[1]
        h1 = _dot(x_lo, w1_b[:dl, :]) + _dot(x_hi, w1_b[dl:, :])
        h3 = _dot(x_lo, w3_b[:dl, :]) + _dot(x_hi, w3_b[dl:, :])
        h = (_silu(h1) * h3).astype(BF16)
        ys_ref[...] = _pack_halves(_dot(h, w2_b[...]))

    @pl.when(jnp.logical_not(used))
    def _():
        ys_ref[...] = jnp.zeros_like(ys_ref)


def moe_ffn(xs, blk_exp, new_exp, nblk, w1, w3, w2, *, layer, bm):
    n_pad, dp = xs.shape
    d = 2 * dp
    de = w1.shape[3]
    n_blk = n_pad // bm

    def x_map(i, be, ne, nb):
        return (jnp.minimum(i, nb[0] - 1), 0)

    def w_map(i, be, ne, nb):
        return (layer, be[i], 0, 0)

    return pl.pallas_call(
        _ffn_kernel,
        grid_spec=pltpu.PrefetchScalarGridSpec(
            num_scalar_prefetch=3,
            grid=(n_blk,),
            in_specs=[pl.BlockSpec((bm, dp), x_map),
                      pl.BlockSpec((1, 1, d, de), w_map),
                      pl.BlockSpec((1, 1, d, de), w_map),
                      pl.BlockSpec((1, 1, de, d), w_map)],
            out_specs=pl.BlockSpec((bm, dp), lambda i, be, ne, nb: (i, 0)),
            scratch_shapes=[pltpu.VMEM((d, de), BF16), pltpu.VMEM((d, de), BF16),
                            pltpu.VMEM((de, d), BF16)]),
        out_shape=jax.ShapeDtypeStruct((n_pad, dp), jnp.int32),
        compiler_params=_cparams(("arbitrary",)),
        name="moe_ffn",
    )(blk_exp, new_exp, nblk, xs, w1, w3, w2)


def _combine_kernel(x_ref, g0_ref, g1_ref, route_ref, ln_g_ref, ln_b_ref, *rest):
    o_ref = rest[-1]
    o_ref[...] = _moe_output(x_ref[...], g0_ref[0], g1_ref[0], route_ref[...], ln_g_ref[...], ln_b_ref[...])


def moe_combine(x, g, route, ln_g, ln_b, *, row0, tb, prev=None):
    n, d = x.shape
    m = g.shape[1]
    blk0 = row0 // tb
    row_spec = pl.BlockSpec((tb, d), lambda i: (blk0 + i, 0))
    args = [x, g, g, route, ln_g.reshape(1, -1), ln_b.reshape(1, -1)]
    in_specs = [row_spec,
                pl.BlockSpec((1, tb, d // 2), lambda i: (0, i, 0)),
                pl.BlockSpec((1, tb, d // 2), lambda i: (1, i, 0)),
                pl.BlockSpec((tb, ROUTE_W), lambda i: (blk0 + i, 0)),
                _full((1, d)), _full((1, d))]
    aliases = {}
    if prev is not None:
        args.append(prev)
        in_specs.append(pl.BlockSpec(memory_space=pl.ANY))
        aliases = {len(args) - 1: 0}
    return pl.pallas_call(
        _combine_kernel,
        grid=(m // tb,),
        in_specs=in_specs,
        out_specs=row_spec,
        out_shape=jax.ShapeDtypeStruct((n, d), F32),
        input_output_aliases=aliases,
        compiler_params=_cparams(("arbitrary",)),
        name="moe_combine",
    )(*args)


MOE_BM = 1024
COMBINE_PARTS = 4
MIXER_PARTS = 4
MIXER_TS = 1024


def hierarchical_moe_layer(x2, xp, route, counts, w1, w3, w2, ln_g, ln_b, *, layer, defer_combine):
    bsz, seq, d = x2.shape
    n = bsz * seq
    bm = MOE_BM
    rt = route.reshape(n, ROUTE_W)
    tb = min(512, n)
    n_blk = (2 * n) // bm + N_EXPERTS
    cnt = counts.sum(axis=0)[0, :N_EXPERTS].astype(jnp.int32)
    pcnt = (cnt + bm - 1) // bm * bm
    pends = jnp.cumsum(pcnt)
    pstart = pends - pcnt
    nblk = (pends[-1] // bm).astype(jnp.int32).reshape(1)
    blk_row = jnp.arange(n_blk, dtype=jnp.int32) * bm
    blk_exp = jnp.minimum(jnp.sum((pends[None, :] <= blk_row[:, None]).astype(jnp.int32), axis=1),
                          N_EXPERTS - 1)
    last_exp = blk_exp[jnp.maximum(nblk[0] - 1, 0)]
    blk_exp = jnp.where(jnp.arange(n_blk) < nblk[0], blk_exp, last_exp)
    new_exp = jnp.concatenate([jnp.ones((1,), jnp.int32),
                               (blk_exp[1:] != blk_exp[:-1]).astype(jnp.int32)])
    pstart_rec = jnp.zeros((1, ROUTE_W), F32).at[0, :N_EXPERTS].set(pstart.astype(F32))
    dest = moe_slots(rt, pstart_rec, tb=tb)
    xs = moe_dispatch(xp.reshape(n, d // 2), dest[0], dest[1], n_blk * bm)
    ys = moe_ffn(xs, blk_exp, new_exp, nblk, w1, w3, w2, layer=layer, bm=bm)
    if defer_combine:
        bparts = MIXER_PARTS if bsz % MIXER_PARTS == 0 else 1
        mb = n // bparts
        gs = [moe_gather(ys, dest[0, p * mb:(p + 1) * mb], dest[1, p * mb:(p + 1) * mb]) for p in range(bparts)]
        return x2, gs, route, ln_g, ln_b
    parts = COMBINE_PARTS if n % (COMBINE_PARTS * tb * 8) == 0 else 1
    m = n // parts
    out = None
    for p in range(parts):
        g = moe_gather(ys, dest[0, p * m:(p + 1) * m], dest[1, p * m:(p + 1) * m])
        out = moe_combine(x2.reshape(n, d), g, rt, ln_g, ln_b, row0=p * m, tb=tb, prev=out)
    return out.reshape(bsz, seq, d)


SC_ROWS = 128


def _sc_mesh():
    return plsc.VectorSubcoreMesh(core_axis_name="c", subcore_axis_name="s")


def moe_dispatch(xf, dest0, dest1, n_pad):
    n, d = xf.shape
    info = plsc.get_sparse_core_info()
    nw = info.num_cores * info.num_subcores
    per_w = n // nw
    r = min(SC_ROWS, per_w)

    def body(x_hbm, d0_hbm, d1_hbm, xs_hbm, i0_v, i1_v, rows_v, sem):
        wid = lax.axis_index("s") * info.num_cores + lax.axis_index("c")

        @pl.loop(0, per_w // r)
        def _(c):
            base = pl.multiple_of(wid * per_w + c * r, 8)
            pltpu.sync_copy(d0_hbm.at[pl.ds(base, r)], i0_v)
            pltpu.sync_copy(d1_hbm.at[pl.ds(base, r)], i1_v)
            pltpu.sync_copy(x_hbm.at[pl.ds(base, r)], rows_v)
            pltpu.async_copy(rows_v, xs_hbm.at[i0_v], sem).wait()
            pltpu.async_copy(rows_v, xs_hbm.at[i1_v], sem).wait()

    return pl.kernel(
        body, out_type=jax.ShapeDtypeStruct((n_pad, d), xf.dtype), mesh=_sc_mesh(),
        scratch_types=[pltpu.VMEM((r,), jnp.int32), pltpu.VMEM((r,), jnp.int32),
                       pltpu.VMEM((r, d), xf.dtype), pltpu.SemaphoreType.DMA],
        name="moe_dispatch",
    )(xf, dest0, dest1)


def moe_gather(ys, dest0, dest1):
    n = dest0.shape[0]
    d = ys.shape[1]
    info = plsc.get_sparse_core_info()
    nw = info.num_cores * info.num_subcores
    per_w = n // nw
    r = min(SC_ROWS, per_w)

    def body(ys_hbm, d0_hbm, d1_hbm, g_hbm, i_v, rows_v, sem):
        wid = lax.axis_index("s") * info.num_cores + lax.axis_index("c")

        @pl.loop(0, per_w // r)
        def _(c):
            base = pl.multiple_of(wid * per_w + c * r, 8)
            for k, d_hbm in enumerate((d0_hbm, d1_hbm)):
                pltpu.sync_copy(d_hbm.at[pl.ds(base, r)], i_v)
                pltpu.async_copy(ys_hbm.at[i_v], rows_v, sem).wait()
                pltpu.sync_copy(rows_v, g_hbm.at[k, pl.ds(base, r)])

    return pl.kernel(
        body, out_type=jax.ShapeDtypeStruct((2, n, d), ys.dtype), mesh=_sc_mesh(),
        scratch_types=[pltpu.VMEM((r,), jnp.int32), pltpu.VMEM((r, d), ys.dtype),
                       pltpu.SemaphoreType.DMA],
        name="moe_gather",
    )(ys, dest0, dest1)


def kernel(x, mem, ln_g, ln_b, ev_w_in, ev_gm_ln_g, ev_gm_ln_b, ev_gm_ws, ev_gm_bs, ev_conv_w, ev_conv_b, ev_wq, ev_wk, ev_wv, ev_w_if, ev_b_if, ev_norm_w, ev_skip, ev_w_out, od_w_qkv, od_b_qkv, od_sinks, od_w_o, xa_wq, xa_wkv, xa_wo, moe_w_rg, moe_b_rg, moe_w_re, moe_b_re, moe_w1, moe_w3, moe_w2):
    bsz, seq, d = x.shape
    depth = ln_g.shape[0]
    assert depth == DEPTH
    ts = min(MIXER_TS, seq)
    for l in range(depth):
        if l % 2 == 0:
            e = l // 2
            x = even_mixer_layer(x, ev_w_in[e], ev_gm_ln_g[e], ev_gm_ln_b[e], ev_gm_ws[e], ev_gm_bs[e],
                                 ev_conv_w[e], ev_conv_b[e], ev_wq[e], ev_wk[e], ev_wv[e], ev_w_if[e],
                                 ev_b_if[e], ev_norm_w[e], ev_skip[e], ev_w_out[e],
                                 ln_g[l, 0], ln_b[l, 0], ts=ts)
        else:
            o = l // 2
            x = swa_mixer_layer(x, od_w_qkv[o], od_b_qkv[o], od_sinks[o], od_w_o[o],
                                ln_g[l, 0], ln_b[l, 0], ts=ts)
        wqk, vo = memory_fold(mem, xa_wkv[l], xa_wq[l], xa_wo[l])
        x, xp, route, counts = xattn_router_layer(x, wqk, vo, ln_g[l, 1], ln_b[l, 1],
                                                  moe_w_rg[l], moe_b_rg[l], moe_w_re[l], moe_b_re[l],
                                                  ts=min(XATTN_TS, seq))
        x = hierarchical_moe_layer(x, xp, route, counts, moe_w1, moe_w3, moe_w2,
                                   ln_g[l, 2], ln_b[l, 2], layer=l,
                                   defer_combine=(l + 1 < depth and (l + 1) % 2 == 1))
    return x
```

```python
import functools
import math

import jax
import jax.numpy as jnp
from jax import lax
from jax.experimental import pallas as pl
from jax.experimental.pallas import tpu as pltpu
from jax.experimental.pallas import tpu_sc as plsc

F32 = jnp.float32
BF16 = jnp.bfloat16

A_GROUPS = 4
CHUNK = 128
B_HEADS = 4
B_CONV = 4
C_HEAD_DIM = 64
C_KV_HEADS = 4
X_HEADS = 4
N_GROUPS = 4
EXPERTS_PER_GROUP = 8
N_EXPERTS = N_GROUPS * EXPERTS_PER_GROUP
ROPE_THETA = 10000.0
LN_EPS = 1e-5
DEPTH = 2
DN_ALPHA = (2 * DEPTH) ** 0.25

LANES = 128
VMEM_LIMIT = 48 * 1024 * 1024
NEG = -1e30


def _cparams(sem):
    return pltpu.CompilerParams(dimension_semantics=sem, vmem_limit_bytes=VMEM_LIMIT)


def _full(shape):
    nd = len(shape)
    return pl.BlockSpec(shape, lambda *_: (0,) * nd)


def _dot(a, b):
    return jnp.dot(a, b, preferred_element_type=F32)


def _dot_nt(a, b):
    return lax.dot_general(a, b, (((1,), (1,)), ((), ())), preferred_element_type=F32)


def _split_dot(a, b_bf16):
    hi = a.astype(BF16)
    lo = (a - hi.astype(F32)).astype(BF16)
    return _dot(hi, b_bf16) + _dot(lo, b_bf16)


def _ln(x, g, b):
    mu = jnp.mean(x, axis=-1, keepdims=True)
    xc = x - mu
    var = jnp.mean(xc * xc, axis=-1, keepdims=True)
    return xc * lax.rsqrt(var + LN_EPS) * g + b


LOG2E = math.log2(math.e)


def _silu(x):
    return x * (1.0 / (1.0 + jnp.exp2(x * -LOG2E)))


def _gelu(x):
    return 0.5 * x * (1.0 + jnp.tanh(math.sqrt(2.0 / math.pi) * (x + 0.044715 * (x * x * x))))


def _pack_halves(x):
    c = x.shape[1] // 2
    lo = lax.bitcast_convert_type(x[:, :c].astype(BF16).astype(F32), jnp.uint32)
    hi = lax.bitcast_convert_type(x[:, c:].astype(BF16).astype(F32), jnp.uint32)
    return lax.bitcast_convert_type((lo >> 16) | hi, jnp.int32)


def _unpack_halves(p):
    u = lax.bitcast_convert_type(p, jnp.uint32)
    lo = lax.bitcast_convert_type(u << 16, F32)
    hi = lax.bitcast_convert_type(u & jnp.uint32(0xFFFF0000), F32)
    return lo, hi


def _log_sigmoid(x):
    return jnp.minimum(x, 0.0) - jnp.log(1.0 + jnp.exp(-jnp.abs(x)))


def _even_kernel(x_ref, w_in_ref, gm_g_ref, gm_b_ref, gm_w_ref, gm_bias_ref,
                 conv_w_ref, conv_b_ref, wq_ref, wk_ref, wv_ref, wif_t_ref, bif_t_ref,
                 norm_w_ref, skip_ref, w_out_ref, ln_g_ref, ln_b_ref,
                 o_ref,
                 xm_buf, ct_ref, m_ref, *, ts, aw, bw):
    dh = bw // B_HEADS
    agd = aw // A_GROUPS
    nck = ts // CHUNK
    pad = 8
    j = pl.program_id(1)

    @pl.when(j == 0)
    def _():
        xm_buf[0:pad, :] = jnp.zeros((pad, bw), F32)
        ct_ref[...] = jnp.zeros_like(ct_ref)
        m_ref[...] = jnp.zeros_like(m_ref)

    row = lax.broadcasted_iota(jnp.int32, (CHUNK, CHUNK), 0)
    col = lax.broadcasted_iota(jnp.int32, (CHUNK, CHUNK), 1)
    causal = col <= row
    diag = col == row
    triu = jnp.where(row <= col, 1.0, 0.0).astype(BF16)
    ones_blk = jnp.ones((CHUNK, LANES), BF16)

    x = x_ref[0]
    proj = _dot(x.astype(BF16), w_in_ref[...])
    a_u = _gelu(proj[:, :aw])
    a_v = _gelu(proj[:, aw:2 * aw])
    xm = proj[:, 2 * aw:2 * aw + bw]
    z = proj[:, 2 * aw + bw:]

    vn = _ln(a_v, gm_g_ref[...], gm_b_ref[...]).astype(BF16)
    ya_chunks = []
    for c in range(nck):
        cols = []
        for g in range(A_GROUPS):
            v_cg = vn[c * CHUNK:(c + 1) * CHUNK, g * agd:(g + 1) * agd]
            cols.append(_dot(gm_w_ref[g], v_cg))
        ya_chunks.append(jnp.concatenate(cols, axis=1) + gm_bias_ref[...])
    y_a = a_u * jnp.concatenate(ya_chunks, axis=0)

    xm_buf[pad:pad + ts, :] = xm
    conv = conv_b_ref[...] + conv_w_ref[B_CONV - 1:B_CONV, :] * xm
    for k in range(B_CONV - 1):
        sh = B_CONV - 1 - k
        conv = conv + conv_w_ref[k:k + 1, :] * xm_buf[pad - sh:pad - sh + ts, :]
    xm_buf[pad - (B_CONV - 1):pad, :] = xm_buf[pad + ts - (B_CONV - 1):pad + ts, :]
    xc = _silu(conv)
    xc_b = xc.astype(BF16)
    q = _dot_head_pairs(xc_b, wq_ref)
    k_ = _dot_head_pairs(xc_b, wk_ref)
    v = _dot_head_pairs(xm.astype(BF16), wv_ref)
    gate_in = jnp.concatenate([q, k_, v], axis=1).astype(BF16)
    gates_t = _dot_nt(wif_t_ref[...], gate_in) + bif_t_ref[...]
    ig_all = gates_t[:B_HEADS, :]
    lf_all = _log_sigmoid(gates_t[B_HEADS:, :])
    q_b = q.astype(BF16)
    k_b = (k_ * dh ** -0.5).astype(BF16)
    v_b = v.astype(BF16)
    gate_z = _silu(z)

    lf_c = [lf_all[:, c * CHUNK:(c + 1) * CHUNK] for c in range(nck)]
    b_all = _split_dot(jnp.concatenate(lf_c, axis=0), triu)
    m_prev = m_ref[...]
    a_c, m_c, decay_c, wg_c = [], [], [], []
    for c in range(nck):
        b_r = b_all[c * B_HEADS:(c + 1) * B_HEADS]
        a_r = ig_all[:, c * CHUNK:(c + 1) * CHUNK] - b_r
        bl = b_r[:, CHUNK - 1:CHUNK]
        g_r = bl + a_r
        m_new = jnp.maximum(bl + m_prev, jnp.max(g_r, axis=1, keepdims=True))
        a_c.append(a_r)
        m_c.append(m_prev)
        decay_c.append(jnp.exp(bl + m_prev - m_new))
        wg_c.append(jnp.exp(g_r - m_new))
        m_prev = m_new
    m_ref[...] = m_prev

    units = [(c, h) for c in range(nck) for h in range(B_HEADS)]
    lmat = jnp.concatenate([jnp.where(causal, lf_c[c][h:h + 1, :], 0.0) for c, h in units], axis=0)
    dgm = jnp.concatenate([jnp.where(diag, wg_c[c][h:h + 1, :], 0.0) for c, h in units], axis=0)
    b_t_all = _split_dot(lmat, ones_blk)
    wg_t_all = _dot(dgm.astype(BF16), ones_blk)

    caugs = [ct_ref[h] for h in range(B_HEADS)]
    h_chunks = []
    for c in range(nck):
        sl = slice(c * CHUNK, (c + 1) * CHUNK)
        heads = []
        for h in range(B_HEADS):
            u = c * B_HEADS + h
            us = slice(u * CHUNK, (u + 1) * CHUNK)
            hs = slice(h * dh, (h + 1) * dh)
            qh, kh = q_b[sl, hs], k_b[sl, hs]
            vaug = jnp.concatenate([v_b[sl, hs], ones_blk], axis=1)
            m_row = m_c[c][h:h + 1, :]
            amat = jnp.where(causal, a_c[c][h:h + 1, :], NEG)
            mx = jnp.maximum(jnp.max(amat, axis=1, keepdims=True), m_row)
            w_intra = jnp.exp(amat - mx)
            w_state = jnp.exp(m_row - mx)
            s = _dot_nt(qh, kh) * w_intra
            kw = (kh.astype(F32) * wg_t_all[us]).astype(BF16)
            both = _dot(jnp.concatenate([s.astype(BF16), kw.T], axis=0), vaug)
            caug = caugs[h]
            naug = both[:CHUNK] + jnp.concatenate([w_state, w_state], axis=1) * _dot(qh, caug.astype(BF16))
            num, nq = naug[:, :dh], naug[:, dh:]
            hv = num / jnp.maximum(jnp.abs(nq), jnp.exp(-(b_t_all[us] + mx)))
            decay = decay_c[c][h:h + 1, :]
            caugs[h] = jnp.concatenate([decay, decay], axis=1) * caug + both[CHUNK:]
            hc = hv - jnp.mean(hv, axis=1, keepdims=True)
            hn = hc * lax.rsqrt(jnp.mean(hc * hc, axis=1, keepdims=True) + LN_EPS)
            heads.append(hn)
        h_chunks.append(jnp.concatenate(heads, axis=1))
    for h in range(B_HEADS):
        ct_ref[h] = caugs[h]
    hn_all = jnp.concatenate(h_chunks, axis=0) if nck > 1 else h_chunks[0]
    y_b = (hn_all * norm_w_ref[...] + skip_ref[...] * xc) * gate_z

    mix = jnp.concatenate([y_a, y_b], axis=1).astype(BF16)
    y = _dot(mix, w_out_ref[...])
    o_ref[0] = _ln(DN_ALPHA * x + y, ln_g_ref[...], ln_b_ref[...])


def _block_diag_pairs(w):
    hh, d, _ = w.shape
    wp = w.reshape(hh // 2, 2, d, d)
    eye = jnp.eye(2, dtype=w.dtype)
    return jnp.einsum('pade,ab->padbe', wp, eye).reshape(hh // 2, 2 * d, 2 * d)


def _dot_head_pairs(x, w_ref):
    npair, w2, _ = w_ref.shape
    return jnp.concatenate([_dot(x[:, p * w2:(p + 1) * w2], w_ref[p]) for p in range(npair)], axis=1)


def even_mixer_layer(x, w_in, gm_ln_g, gm_ln_b, gm_ws, gm_bs, conv_w, conv_b, wq, wk, wv, w_if,
                     b_if, norm_w, skip, w_out, ln_g, ln_b, *, ts):
    bsz, seq, d = x.shape
    aw = gm_ln_g.shape[0]
    bw = conv_b.shape[0]
    agd = aw // A_GROUPS
    causal = jnp.tril(jnp.ones((CHUNK, CHUNK), dtype=bool))
    gm_w = jnp.where(causal[None], gm_ws, 0.0).astype(BF16)
    gm_bias = jnp.repeat(gm_bs.T, agd, axis=1)
    row = lambda a: a.reshape(1, -1)
    args = (x, w_in.astype(BF16), row(gm_ln_g), row(gm_ln_b), gm_w, gm_bias,
            conv_w, row(conv_b), _block_diag_pairs(wq).astype(BF16), _block_diag_pairs(wk).astype(BF16),
            _block_diag_pairs(wv).astype(BF16), w_if.T.astype(BF16),
            b_if.reshape(-1, 1), row(norm_w), row(skip), w_out.astype(BF16),
            row(ln_g), row(ln_b))
    in_specs = [pl.BlockSpec((1, ts, d), lambda b, j: (b, j, 0))] + [_full(a.shape) for a in args[1:]]
    dh = bw // B_HEADS
    assert dh == LANES and CHUNK == LANES
    return pl.pallas_call(
        functools.partial(_even_kernel, ts=ts, aw=aw, bw=bw),
        grid=(bsz, seq // ts),
        in_specs=in_specs,
        out_specs=pl.BlockSpec((1, ts, d), lambda b, j: (b, j, 0)),
        out_shape=jax.ShapeDtypeStruct((bsz, seq, d), F32),
        scratch_shapes=[pltpu.VMEM((8 + ts, bw), F32),
                        pltpu.VMEM((B_HEADS, dh, dh + LANES), F32),
                        pltpu.VMEM((B_HEADS, LANES), F32)],
        compiler_params=_cparams(("arbitrary", "arbitrary")),
        name="even_mixer",
    )(*args)


def _memfold_kernel(mem_ref, wkv_ref, wq_ref, wo_ref, wqk_ref, vo_ref, *, d):
    dh = d // X_HEADS
    m_len = mem_ref.shape[1]
    kv = _dot(mem_ref[0].astype(BF16), wkv_ref[...])
    k = (kv[:, :d] * (dh ** -0.5 * LOG2E)).astype(BF16)
    v = kv[:, d:].astype(BF16)
    for h in range(X_HEADS):
        hs = slice(h * dh, (h + 1) * dh)
        ms = slice(h * m_len, (h + 1) * m_len)
        wqk_ref[0, :, ms] = _dot_nt(wq_ref[:, hs], k[:, hs]).astype(BF16)
        vo_ref[0, ms, :] = _dot(v[:, hs], wo_ref[hs, :]).astype(BF16)


def memory_fold(mem, wkv, wq, wo):
    bsz, m_len, d = mem.shape
    hm = X_HEADS * m_len
    args = (mem, wkv.astype(BF16), wq.astype(BF16), wo.astype(BF16))
    return pl.pallas_call(
        functools.partial(_memfold_kernel, d=d),
        grid=(bsz,),
        in_specs=[pl.BlockSpec((1, m_len, d), lambda b: (b, 0, 0))] + [_full(a.shape) for a in args[1:]],
        out_specs=[pl.BlockSpec((1, d, hm), lambda b: (b, 0, 0)),
                   pl.BlockSpec((1, hm, d), lambda b: (b, 0, 0))],
        out_shape=[jax.ShapeDtypeStruct((bsz, d, hm), BF16),
                   jax.ShapeDtypeStruct((bsz, hm, d), BF16)],
        compiler_params=_cparams(("arbitrary",)),
        name="memory_fold",
    )(*args)


ROUTE_W = 128
XATTN_TS = 1024
DEST_ROWS = 8
SLOT_TB = 2048
SLOT_SUB = 512


def _xattn_kernel(x_ref, wqk_ref, vo_ref, ln_g_ref, ln_b_ref, wr_ref, br_ref,
                  o_ref, xp_ref, route_ref, cnt_ref):
    m_len = wqk_ref.shape[2] // X_HEADS

    @pl.when(pl.program_id(1) == 0)
    def _():
        cnt_ref[...] = jnp.zeros_like(cnt_ref)

    x = x_ref[0]
    ts = x.shape[0]
    xb = x.astype(BF16)
    probs = []
    for h in range(X_HEADS):
        s = _dot(xb, wqk_ref[0, :, h * m_len:(h + 1) * m_len])
        p = jnp.exp2(s - jnp.max(s, axis=1, keepdims=True))
        probs.append((p / jnp.sum(p, axis=1, keepdims=True)).astype(BF16))
    y = _dot(jnp.concatenate(probs, axis=1), vo_ref[0])
    x2 = _ln(DN_ALPHA * x + y, ln_g_ref[...], ln_b_ref[...])
    o_ref[0] = x2
    xp_ref[0] = _pack_halves(x2)

    x_hi = x2.astype(BF16)
    x_lo = (x2 - x_hi.astype(F32)).astype(BF16)
    logits = (_dot(x_hi, wr_ref[0]) + _dot(x_lo, wr_ref[0]) + _dot(x_hi, wr_ref[1])) + br_ref[...]
    lane = lax.broadcasted_iota(jnp.int32, (ts, ROUTE_W), 1)
    is_g = lane < N_GROUPS
    lg = jnp.where(is_g, logits, NEG)
    mg = jnp.max(lg, axis=1, keepdims=True)
    gi = jnp.min(jnp.where(jnp.logical_and(is_g, lg == mg), lane, ROUTE_W), axis=1, keepdims=True)
    gate_g = 1.0 / jnp.sum(jnp.where(is_g, jnp.exp(lg - mg), 0.0), axis=1, keepdims=True)
    lo = N_GROUPS + gi * EXPERTS_PER_GROUP
    in_grp = jnp.logical_and(lane >= lo, lane < lo + EXPERTS_PER_GROUP)
    le = jnp.where(in_grp, logits, NEG)
    v1 = jnp.max(le, axis=1, keepdims=True)
    i1 = jnp.min(jnp.where(jnp.logical_and(in_grp, le == v1), lane, ROUTE_W), axis=1, keepdims=True)
    le2 = jnp.where(lane == i1, NEG, le)
    v2 = jnp.max(le2, axis=1, keepdims=True)
    i2 = jnp.min(jnp.where(jnp.logical_and(in_grp, le2 == v2), lane, ROUTE_W), axis=1, keepdims=True)
    e21 = jnp.exp(v2 - v1)
    p1 = 1.0 / (1.0 + e21)
    p2 = e21 * p1
    e1 = (i1 - N_GROUPS).astype(F32)
    e2 = (i2 - N_GROUPS).astype(F32)
    rec = jnp.where(lane == 0, e1, 0.0)
    rec = jnp.where(lane == 1, e2, rec)
    rec = jnp.where(lane == 2, gate_g * p1, rec)
    rec = jnp.where(lane == 3, gate_g * p2, rec)
    route_ref[0] = rec
    sel = jnp.logical_or(lane == i1 - N_GROUPS, lane == i2 - N_GROUPS)
    cnt_ref[0] += jnp.sum(jnp.where(sel, 1.0, 0.0), axis=0, keepdims=True)


def xattn_router_layer(x, wqk, vo, ln_g, ln_b, w_rg, b_rg, w_re, b_re, *, ts):
    bsz, seq, d = x.shape
    hm = wqk.shape[2]
    wr = jnp.zeros((d, ROUTE_W), F32).at[:, :N_GROUPS].set(w_rg).at[:, N_GROUPS:N_GROUPS + N_EXPERTS].set(w_re)
    br = jnp.zeros((1, ROUTE_W), F32).at[0, :N_GROUPS].set(b_rg).at[0, N_GROUPS:N_GROUPS + N_EXPERTS].set(b_re)
    wr_hi = wr.astype(BF16)
    wr_lo = (wr - wr_hi.astype(F32)).astype(BF16)
    wr = jnp.stack([wr_hi, wr_lo])
    args = (x, wqk, vo, ln_g.reshape(1, -1), ln_b.reshape(1, -1), wr, br)
    in_specs = [pl.BlockSpec((1, ts, d), lambda b, j: (b, j, 0)),
                pl.BlockSpec((1, d, hm), lambda b, j: (b, 0, 0)),
                pl.BlockSpec((1, hm, d), lambda b, j: (b, 0, 0))] + [_full(a.shape) for a in args[3:]]
    return pl.pallas_call(
        _xattn_kernel,
        grid=(bsz, seq // ts),
        in_specs=in_specs,
        out_specs=[pl.BlockSpec((1, ts, d), lambda b, j: (b, j, 0)),
                   pl.BlockSpec((1, ts, d // 2), lambda b, j: (b, j, 0)),
                   pl.BlockSpec((1, ts, ROUTE_W), lambda b, j: (b, j, 0)),
                   pl.BlockSpec((1, 1, ROUTE_W), lambda b, j: (b, 0, 0))],
        out_shape=[jax.ShapeDtypeStruct((bsz, seq, d), F32),
                   jax.ShapeDtypeStruct((bsz, seq, d // 2), jnp.int32),
                   jax.ShapeDtypeStruct((bsz, seq, ROUTE_W), F32),
                   jax.ShapeDtypeStruct((bsz, 1, ROUTE_W), F32)],
        compiler_params=_cparams(("arbitrary", "arbitrary")),
        name="xattn_router",
    )(*args)


def _moe_output(x2, g0, g1, rec, ln_g, ln_b):
    a_lo, a_hi = _unpack_halves(g0)
    b_lo, b_hi = _unpack_halves(g1)
    w0, w1 = rec[:, 2:3], rec[:, 3:4]
    y = jnp.concatenate([w0 * a_lo + w1 * b_lo, w0 * a_hi + w1 * b_hi], axis=1)
    return _ln(DN_ALPHA * x2 + y, ln_g, ln_b)


def _swa_kernel(x_ref, g0_ref, g1_ref, route_ref, pln_g_ref, pln_b_ref,
                wqkv_ref, bqkv_ref, cos_ref, sin_ref, sink_ref, wo_ref, ln_g_ref, ln_b_ref,
                *rest, ts, cq, ckv):
    o_ref, kprev, vprev = rest[-3:]
    j = pl.program_id(1)
    nb = ts // CHUNK
    dh = C_HEAD_DIM
    grp = (cq // dh) // C_KV_HEADS

    @pl.when(j == 0)
    def _():
        kprev[...] = jnp.zeros_like(kprev)
        vprev[...] = jnp.zeros_like(vprev)

    x = _moe_output(x_ref[0], g0_ref[0, 0], g1_ref[0, 0], route_ref[0], pln_g_ref[...], pln_b_ref[...])
    qkv = _dot(x.astype(BF16), wqkv_ref[...]) + bqkv_ref[...]
    cos = cos_ref[...]
    sin = sin_ref[...]
    lane = lax.broadcasted_iota(jnp.int32, (ts, LANES), 1)
    first_half = (lane % dh) < (dh // 2)

    def rope(t):
        outs = []
        for c in range(t.shape[1] // LANES):
            tc = t[:, c * LANES:(c + 1) * LANES]
            rot = jnp.where(first_half, pltpu.roll(tc, LANES - dh // 2, 1), pltpu.roll(tc, dh // 2, 1))
            outs.append(tc * cos + rot * sin)
        return jnp.concatenate(outs, axis=1)

    q = rope(qkv[:, :cq]).astype(BF16)
    k = rope(qkv[:, cq:cq + 2 * ckv]).astype(BF16)
    v = qkv[:, cq + 2 * ckv:].astype(BF16)

    r_i = lax.broadcasted_iota(jnp.int32, (CHUNK, 2 * CHUNK), 0)
    c_i = lax.broadcasted_iota(jnp.int32, (CHUNK, 2 * CHUNK), 1)
    band = jnp.logical_and(c_i > r_i, c_i <= r_i + CHUNK)
    sink_col = c_i == 0
    lane_k = lax.broadcasted_iota(jnp.int32, (2 * CHUNK, LANES), 1)
    key_row = lax.broadcasted_iota(jnp.int32, (2 * CHUNK, LANES), 0)
    lane_q = lax.broadcasted_iota(jnp.int32, (CHUNK, LANES), 1)
    ones_blk = jnp.ones((2 * CHUNK, LANES), BF16)
    zero_b = jnp.zeros((), BF16)
    blocks = []
    for c in range(nb):
        sl = slice(c * CHUNK, (c + 1) * CHUNK)
        if c == 0:
            kb = jnp.concatenate([kprev[...].astype(BF16), k[sl]], axis=0)
            vb = jnp.concatenate([vprev[...].astype(BF16), v[sl]], axis=0)
            first_key = jnp.where(j > 0, 0, CHUNK)
            valid = jnp.logical_and(band, c_i >= first_key)
        else:
            kb = k[(c - 1) * CHUNK:(c + 1) * CHUNK]
            vb = v[(c - 1) * CHUNK:(c + 1) * CHUNK]
            valid = band
        tiles = []
        for h in range(C_KV_HEADS):
            kd = kb[:, h * LANES:(h + 1) * LANES]
            vd = vb[:, h * LANES:(h + 1) * LANES]
            k_lo = jnp.where(lane_k < dh, kd, zero_b)
            k_hi = jnp.where(lane_k >= dh, kd, zero_b)
            vz = jnp.where(key_row == 0, zero_b, vd)
            q2 = jnp.concatenate([q[sl, (2 * h) * LANES:(2 * h + 1) * LANES],
                                  q[sl, (2 * h + 1) * LANES:(2 * h + 2) * LANES]], axis=0)
            s_lo = _dot_nt(q2, k_lo)
            s_hi = _dot_nt(q2, k_hi)
            parts = []
            for qk, g in ((s_lo[:CHUNK], 0), (s_lo[CHUNK:], 2), (s_hi[:CHUNK], 1), (s_hi[CHUNK:], 3)):
                sink = sink_ref[h * grp + g]
                parts.append(jnp.where(valid, qk, jnp.where(sink_col, sink, NEG)))
            s = jnp.concatenate(parts, axis=0)
            p = jnp.exp2(s - jnp.max(s, axis=1, keepdims=True)).astype(BF16)
            den = _dot(p, ones_blk)
            o2 = _dot(p, vz) / den
            tiles.append(jnp.where(lane_q < dh, o2[:CHUNK], o2[2 * CHUNK:3 * CHUNK]))
            tiles.append(jnp.where(lane_q < dh, o2[CHUNK:2 * CHUNK], o2[3 * CHUNK:]))
        blocks.append(jnp.concatenate(tiles, axis=1))
    kprev[...] = k[(nb - 1) * CHUNK:].astype(F32)
    vprev[...] = v[(nb - 1) * CHUNK:].astype(F32)
    att = (jnp.concatenate(blocks, axis=0) if nb > 1 else blocks[0]).astype(BF16)
    y = _dot(att, wo_ref[...])
    o_ref[0] = _ln(DN_ALPHA * x + y, ln_g_ref[...], ln_b_ref[...])


def swa_mixer_layer(pending, w_qkv, b_qkv, sinks, w_o, ln_g, ln_b, *, ts):
    x, gs, route, pln_g, pln_b = pending
    bsz, seq, d = x.shape
    cq = w_o.shape[0]
    ckv = (w_qkv.shape[1] - cq) // 2
    dh = C_HEAD_DIM
    inv = ROPE_THETA ** (-jnp.arange(0, dh, 2, dtype=F32) / dh)
    ang = jnp.arange(seq, dtype=F32)[:, None] * inv[None, :]
    reps = LANES // (dh // 2)
    sign = jnp.tile(jnp.concatenate([-jnp.ones((dh // 2,), F32), jnp.ones((dh // 2,), F32)]), LANES // dh)
    cos_t = jnp.tile(jnp.cos(ang), (1, reps))
    sin_t = jnp.tile(jnp.sin(ang), (1, reps)) * sign[None, :]
    assert 2 * dh == LANES

    def dup_heads(t):
        th = t.reshape(t.shape[:-1] + (ckv // dh, dh))
        return jnp.concatenate([th, th], axis=-1).reshape(t.shape[:-1] + (2 * ckv,))

    qs = dh ** -0.5 * LOG2E
    w_all = jnp.concatenate([w_qkv[:, :cq] * qs, dup_heads(w_qkv[:, cq:cq + ckv]),
                             dup_heads(w_qkv[:, cq + ckv:])], axis=1)
    b_all = jnp.concatenate([b_qkv[:cq] * qs, dup_heads(b_qkv[cq:cq + ckv]), dup_heads(b_qkv[cq + ckv:])])
    shared = (pln_g.reshape(1, -1), pln_b.reshape(1, -1),
              w_all.astype(BF16), b_all.reshape(1, -1), cos_t, sin_t, sinks.astype(F32) * LOG2E,
              w_o.astype(BF16), ln_g.reshape(1, -1), ln_b.reshape(1, -1))
    bp = bsz // len(gs)
    out = None
    for p, g in enumerate(gs):
        b0 = p * bp
        g = g.reshape(2, bp, seq, d // 2)
        args = [x, g, g, route, *shared]
        in_specs = [pl.BlockSpec((1, ts, d), lambda b, j: (b0 + b, j, 0)),
                    pl.BlockSpec((1, 1, ts, d // 2), lambda b, j: (0, b, j, 0)),
                    pl.BlockSpec((1, 1, ts, d // 2), lambda b, j: (1, b, j, 0)),
                    pl.BlockSpec((1, ts, ROUTE_W), lambda b, j: (b0 + b, j, 0)),
                    _full((1, d)), _full((1, d)),
                    _full(shared[2].shape), _full(shared[3].shape),
                    pl.BlockSpec((ts, LANES), lambda b, j: (j, 0)),
                    pl.BlockSpec((ts, LANES), lambda b, j: (j, 0)),
                    pl.BlockSpec(memory_space=pltpu.SMEM),
                    _full(shared[7].shape), _full(shared[8].shape), _full(shared[9].shape)]
        aliases = {}
        if out is not None:
            args.append(out)
            in_specs.append(pl.BlockSpec(memory_space=pl.ANY))
            aliases = {len(args) - 1: 0}
        out = pl.pallas_call(
            functools.partial(_swa_kernel, ts=ts, cq=cq, ckv=ckv),
            grid=(bp, seq // ts),
            in_specs=in_specs,
            out_specs=pl.BlockSpec((1, ts, d), lambda b, j: (b0 + b, j, 0)),
            out_shape=jax.ShapeDtypeStruct((bsz, seq, d), F32),
            input_output_aliases=aliases,
            scratch_shapes=[pltpu.VMEM((CHUNK, 2 * ckv), F32), pltpu.VMEM((CHUNK, 2 * ckv), F32)],
            compiler_params=_cparams(("arbitrary", "arbitrary")),
            name="swa_mixer",
        )(*args)
    return out


def _slot_kernel(route_ref, pstart_ref, dest_ref, carry_ref, *, tb):
    @pl.when(pl.program_id(0) == 0)
    def _():
        carry_ref[...] = pstart_ref[...]

    sub = min(SLOT_SUB, tb)
    lane = lax.broadcasted_iota(jnp.int32, (sub, ROUTE_W), 1)
    r = lax.broadcasted_iota(jnp.int32, (sub, sub), 0)
    c = lax.broadcasted_iota(jnp.int32, (sub, sub), 1)
    before = jnp.where(c < r, 1.0, 0.0).astype(BF16)
    carry = carry_ref[...]
    for k in range(tb // sub):
        rows = slice(k * sub, (k + 1) * sub)
        rec = route_ref[rows, :]
        e0 = rec[:, 0:1].astype(jnp.int32)
        e1 = rec[:, 1:2].astype(jnp.int32)
        oh0 = lane == e0
        oh1 = lane == e1
        ohs = jnp.where(jnp.logical_or(oh0, oh1), 1.0, 0.0)
        prefix = _dot(before, ohs.astype(BF16)) + carry
        d0 = jnp.sum(jnp.where(oh0, prefix, 0.0), axis=1, keepdims=True)
        d1 = jnp.sum(jnp.where(oh1, prefix, 0.0), axis=1, keepdims=True)
        dest = jnp.where(lane == 0, d0, jnp.where(lane == 1, d1, 0.0))
        dest_ref[:, rows] = dest.T[:DEST_ROWS].astype(jnp.int32)
        carry = carry + jnp.sum(ohs, axis=0, keepdims=True)
    carry_ref[...] = carry


def moe_slots(route, pstart, *, tb):
    n = route.shape[0]
    return pl.pallas_call(
        functools.partial(_slot_kernel, tb=tb),
        grid=(n // tb,),
        in_specs=[pl.BlockSpec((tb, ROUTE_W), lambda i: (i, 0)), _full((1, ROUTE_W))],
        out_specs=pl.BlockSpec((DEST_ROWS, tb), lambda i: (0, i)),
        out_shape=jax.ShapeDtypeStruct((DEST_ROWS, n), jnp.int32),
        scratch_shapes=[pltpu.VMEM((1, ROUTE_W), F32)],
        compiler_params=_cparams(("arbitrary",)),
        name="moe_slots",
    )(route, pstart)


def _ffn_kernel(blk_exp_ref, new_exp_ref, nblk_ref, xs_ref, w1_ref, w3_ref, w2_ref, ys_ref,
                w1_b, w3_b, w2_b):
    i = pl.program_id(0)
    used = i < nblk_ref[0]

    @pl.when(jnp.logical_and(used, new_exp_ref[i] == 1))
    def _():
        w1_b[...] = w1_ref[0, 0].astype(BF16)
        w3_b[...] = w3_ref[0, 0].astype(BF16)
        w2_b[...] = w2_ref[0, 0].astype(BF16)

    @pl.when(used)
    def _():
        x_lo, x_hi = _unpack_halves(xs_ref[...])
        x_lo, x_hi = x_lo.astype(BF16), x_hi.astype(BF16)
        dl = x_lo.shape[1]
        h1 = _dot(x_lo, w1_b[:dl, :]) + _dot(x_hi, w1_b[dl:, :])
        h3 = _dot(x_lo, w3_b[:dl, :]) + _dot(x_hi, w3_b[dl:, :])
        h = (_silu(h1) * h3).astype(BF16)
        ys_ref[...] = _pack_halves(_dot(h, w2_b[...]))

    @pl.when(jnp.logical_not(used))
    def _():
        ys_ref[...] = jnp.zeros_like(ys_ref)


def moe_ffn(xs, blk_exp, new_exp, nblk, w1, w3, w2, *, layer, bm):
    n_pad, dp = xs.shape
    d = 2 * dp
    de = w1.shape[3]
    n_blk = n_pad // bm

    def x_map(i, be, ne, nb):
        return (jnp.minimum(i, nb[0] - 1), 0)

    def w_map(i, be, ne, nb):
        return (layer, be[i], 0, 0)

    return pl.pallas_call(
        _ffn_kernel,
        grid_spec=pltpu.PrefetchScalarGridSpec(
            num_scalar_prefetch=3,
            grid=(n_blk,),
            in_specs=[pl.BlockSpec((bm, dp), x_map),
                      pl.BlockSpec((1, 1, d, de), w_map),
                      pl.BlockSpec((1, 1, d, de), w_map),
                      pl.BlockSpec((1, 1, de, d), w_map)],
            out_specs=pl.BlockSpec((bm, dp), lambda i, be, ne, nb: (i, 0)),
            scratch_shapes=[pltpu.VMEM((d, de), BF16), pltpu.VMEM((d, de), BF16),
                            pltpu.VMEM((de, d), BF16)]),
        out_shape=jax.ShapeDtypeStruct((n_pad, dp), jnp.int32),
        compiler_params=_cparams(("arbitrary",)),
        name="moe_ffn",
    )(blk_exp, new_exp, nblk, xs, w1, w3, w2)


def _combine_kernel(x_ref, g0_ref, g1_ref, route_ref, ln_g_ref, ln_b_ref, *rest):
    o_ref = rest[-1]
    o_ref[...] = _moe_output(x_ref[...], g0_ref[0], g1_ref[0], route_ref[...], ln_g_ref[...], ln_b_ref[...])


def moe_combine(x, g, route, ln_g, ln_b, *, row0, tb, prev=None):
    n, d = x.shape
    m = g.shape[1]
    blk0 = row0 // tb
    row_spec = pl.BlockSpec((tb, d), lambda i: (blk0 + i, 0))
    args = [x, g, g, route, ln_g.reshape(1, -1), ln_b.reshape(1, -1)]
    in_specs = [row_spec,
                pl.BlockSpec((1, tb, d // 2), lambda i: (0, i, 0)),
                pl.BlockSpec((1, tb, d // 2), lambda i: (1, i, 0)),
                pl.BlockSpec((tb, ROUTE_W), lambda i: (blk0 + i, 0)),
                _full((1, d)), _full((1, d))]
    aliases = {}
    if prev is not None:
        args.append(prev)
        in_specs.append(pl.BlockSpec(memory_space=pl.ANY))
        aliases = {len(args) - 1: 0}
    return pl.pallas_call(
        _combine_kernel,
        grid=(m // tb,),
        in_specs=in_specs,
        out_specs=row_spec,
        out_shape=jax.ShapeDtypeStruct((n, d), F32),
        input_output_aliases=aliases,
        compiler_params=_cparams(("arbitrary",)),
        name="moe_combine",
    )(*args)


MOE_BM = 1024
COMBINE_PARTS = 4
MIXER_PARTS = 2
MIXER_TS = 1024


def hierarchical_moe_layer(x2, xp, route, counts, w1, w3, w2, ln_g, ln_b, *, layer, defer_combine):
    bsz, seq, d = x2.shape
    n = bsz * seq
    bm = MOE_BM
    rt = route.reshape(n, ROUTE_W)
    tb = min(512, n)
    n_blk = (2 * n) // bm + N_EXPERTS
    cnt = counts.sum(axis=0)[0, :N_EXPERTS].astype(jnp.int32)
    pcnt = (cnt + bm - 1) // bm * bm
    pends = jnp.cumsum(pcnt)
    pstart = pends - pcnt
    nblk = (pends[-1] // bm).astype(jnp.int32).reshape(1)
    blk_row = jnp.arange(n_blk, dtype=jnp.int32) * bm
    blk_exp = jnp.minimum(jnp.sum((pends[None, :] <= blk_row[:, None]).astype(jnp.int32), axis=1),
                          N_EXPERTS - 1)
    last_exp = blk_exp[jnp.maximum(nblk[0] - 1, 0)]
    blk_exp = jnp.where(jnp.arange(n_blk) < nblk[0], blk_exp, last_exp)
    new_exp = jnp.concatenate([jnp.ones((1,), jnp.int32),
                               (blk_exp[1:] != blk_exp[:-1]).astype(jnp.int32)])
    pstart_rec = jnp.zeros((1, ROUTE_W), F32).at[0, :N_EXPERTS].set(pstart.astype(F32))
    dest = moe_slots(rt, pstart_rec, tb=min(SLOT_TB, n))
    xs = moe_dispatch(xp.reshape(n, d // 2), dest[0], dest[1], n_blk * bm)
    ys = moe_ffn(xs, blk_exp, new_exp, nblk, w1, w3, w2, layer=layer, bm=bm)
    if defer_combine:
        bparts = MIXER_PARTS if bsz % MIXER_PARTS == 0 else 1
        mb = n // bparts
        gs = [moe_gather(ys, dest[0, p * mb:(p + 1) * mb], dest[1, p * mb:(p + 1) * mb]) for p in range(bparts)]
        return x2, gs, route, ln_g, ln_b
    parts = COMBINE_PARTS if n % (COMBINE_PARTS * tb * 8) == 0 else 1
    m = n // parts
    out = None
    for p in range(parts):
        g = moe_gather(ys, dest[0, p * m:(p + 1) * m], dest[1, p * m:(p + 1) * m])
        out = moe_combine(x2.reshape(n, d), g, rt, ln_g, ln_b, row0=p * m, tb=tb, prev=out)
    return out.reshape(bsz, seq, d)


SC_ROWS = 128


def _sc_mesh():
    return plsc.VectorSubcoreMesh(core_axis_name="c", subcore_axis_name="s")


def moe_dispatch(xf, dest0, dest1, n_pad):
    n, d = xf.shape
    info = plsc.get_sparse_core_info()
    nw = info.num_cores * info.num_subcores
    per_w = n // nw
    r = min(SC_ROWS, per_w)

    def body(x_hbm, d0_hbm, d1_hbm, xs_hbm, i0_v, i1_v, rows_v, sem):
        wid = lax.axis_index("s") * info.num_cores + lax.axis_index("c")

        @pl.loop(0, per_w // r)
        def _(c):
            base = pl.multiple_of(wid * per_w + c * r, 8)
            pltpu.sync_copy(d0_hbm.at[pl.ds(base, r)], i0_v)
            pltpu.sync_copy(d1_hbm.at[pl.ds(base, r)], i1_v)
            pltpu.sync_copy(x_hbm.at[pl.ds(base, r)], rows_v)
            pltpu.async_copy(rows_v, xs_hbm.at[i0_v], sem).wait()
            pltpu.async_copy(rows_v, xs_hbm.at[i1_v], sem).wait()

    return pl.kernel(
        body, out_type=jax.ShapeDtypeStruct((n_pad, d), xf.dtype), mesh=_sc_mesh(),
        scratch_types=[pltpu.VMEM((r,), jnp.int32), pltpu.VMEM((r,), jnp.int32),
                       pltpu.VMEM((r, d), xf.dtype), pltpu.SemaphoreType.DMA],
        name="moe_dispatch",
    )(xf, dest0, dest1)


def moe_gather(ys, dest0, dest1):
    n = dest0.shape[0]
    d = ys.shape[1]
    info = plsc.get_sparse_core_info()
    nw = info.num_cores * info.num_subcores
    per_w = n // nw
    r = min(SC_ROWS, per_w)

    def body(ys_hbm, d0_hbm, d1_hbm, g_hbm, i_v, rows_v, sem):
        wid = lax.axis_index("s") * info.num_cores + lax.axis_index("c")

        @pl.loop(0, per_w // r)
        def _(c):
            base = pl.multiple_of(wid * per_w + c * r, 8)
            for k, d_hbm in enumerate((d0_hbm, d1_hbm)):
                pltpu.sync_copy(d_hbm.at[pl.ds(base, r)], i_v)
                pltpu.async_copy(ys_hbm.at[i_v], rows_v, sem).wait()
                pltpu.sync_copy(rows_v, g_hbm.at[k, pl.ds(base, r)])

    return pl.kernel(
        body, out_type=jax.ShapeDtypeStruct((2, n, d), ys.dtype), mesh=_sc_mesh(),
        scratch_types=[pltpu.VMEM((r,), jnp.int32), pltpu.VMEM((r, d), ys.dtype),
                       pltpu.SemaphoreType.DMA],
        name="moe_gather",
    )(ys, dest0, dest1)


def kernel(x, mem, ln_g, ln_b, ev_w_in, ev_gm_ln_g, ev_gm_ln_b, ev_gm_ws, ev_gm_bs, ev_conv_w, ev_conv_b, ev_wq, ev_wk, ev_wv, ev_w_if, ev_b_if, ev_norm_w, ev_skip, ev_w_out, od_w_qkv, od_b_qkv, od_sinks, od_w_o, xa_wq, xa_wkv, xa_wo, moe_w_rg, moe_b_rg, moe_w_re, moe_b_re, moe_w1, moe_w3, moe_w2):
    bsz, seq, d = x.shape
    depth = ln_g.shape[0]
    assert depth == DEPTH
    ts = min(MIXER_TS, seq)
    for l in range(depth):
        if l % 2 == 0:
            e = l // 2
            x = even_mixer_layer(x, ev_w_in[e], ev_gm_ln_g[e], ev_gm_ln_b[e], ev_gm_ws[e], ev_gm_bs[e],
                                 ev_conv_w[e], ev_conv_b[e], ev_wq[e], ev_wk[e], ev_wv[e], ev_w_if[e],
                                 ev_b_if[e], ev_norm_w[e], ev_skip[e], ev_w_out[e],
                                 ln_g[l, 0], ln_b[l, 0], ts=ts)
        else:
            o = l // 2
            x = swa_mixer_layer(x, od_w_qkv[o], od_b_qkv[o], od_sinks[o], od_w_o[o],
                                ln_g[l, 0], ln_b[l, 0], ts=ts)
        wqk, vo = memory_fold(mem, xa_wkv[l], xa_wq[l], xa_wo[l])
        x, xp, route, counts = xattn_router_layer(x, wqk, vo, ln_g[l, 1], ln_b[l, 1],
                                                  moe_w_rg[l], moe_b_rg[l], moe_w_re[l], moe_b_re[l],
                                                  ts=min(XATTN_TS, seq))
        x = hierarchical_moe_layer(x, xp, route, counts, moe_w1, moe_w3, moe_w2,
                                   ln_g[l, 2], ln_b[l, 2], layer=l,
                                   defer_combine=(l + 1 < depth and (l + 1) % 2 == 1))
    return x
```

```python
import functools
import math

import jax
import jax.numpy as jnp
from jax import lax
from jax.experimental import pallas as pl
from jax.experimental.pallas import tpu as pltpu
from jax.experimental.pallas import tpu_sc as plsc

F32 = jnp.float32
BF16 = jnp.bfloat16

A_GROUPS = 4
CHUNK = 128
B_HEADS = 4
B_CONV = 4
C_HEAD_DIM = 64
C_KV_HEADS = 4
X_HEADS = 4
N_GROUPS = 4
EXPERTS_PER_GROUP = 8
N_EXPERTS = N_GROUPS * EXPERTS_PER_GROUP
ROPE_THETA = 10000.0
LN_EPS = 1e-5
DEPTH = 2
DN_ALPHA = (2 * DEPTH) ** 0.25

LANES = 128
VMEM_LIMIT = 48 * 1024 * 1024
NEG = -1e30


def _cparams(sem):
    return pltpu.CompilerParams(dimension_semantics=sem, vmem_limit_bytes=VMEM_LIMIT)


def _full(shape):
    nd = len(shape)
    return pl.BlockSpec(shape, lambda *_: (0,) * nd)


def _dot(a, b):
    return jnp.dot(a, b, preferred_element_type=F32)


def _dot_nt(a, b):
    return lax.dot_general(a, b, (((1,), (1,)), ((), ())), preferred_element_type=F32)


def _split_dot(a, b_bf16):
    hi = a.astype(BF16)
    lo = (a - hi.astype(F32)).astype(BF16)
    return _dot(hi, b_bf16) + _dot(lo, b_bf16)


def _ln(x, g, b):
    mu = jnp.mean(x, axis=-1, keepdims=True)
    xc = x - mu
    var = jnp.mean(xc * xc, axis=-1, keepdims=True)
    return xc * lax.rsqrt(var + LN_EPS) * g + b


LOG2E = math.log2(math.e)


def _silu(x):
    return x * (1.0 / (1.0 + jnp.exp2(x * -LOG2E)))


def _gelu(x):
    return 0.5 * x * (1.0 + jnp.tanh(math.sqrt(2.0 / math.pi) * (x + 0.044715 * (x * x * x))))


def _pack_halves(x):
    c = x.shape[1] // 2
    lo = lax.bitcast_convert_type(x[:, :c].astype(BF16).astype(F32), jnp.uint32)
    hi = lax.bitcast_convert_type(x[:, c:].astype(BF16).astype(F32), jnp.uint32)
    return lax.bitcast_convert_type((lo >> 16) | hi, jnp.int32)


def _unpack_halves(p):
    u = lax.bitcast_convert_type(p, jnp.uint32)
    lo = lax.bitcast_convert_type(u << 16, F32)
    hi = lax.bitcast_convert_type(u & jnp.uint32(0xFFFF0000), F32)
    return lo, hi


def _log_sigmoid(x):
    return jnp.minimum(x, 0.0) - jnp.log(1.0 + jnp.exp(-jnp.abs(x)))


def _even_kernel(x_ref, w_in_ref, gm_g_ref, gm_b_ref, gm_w_ref, gm_bias_ref,
                 conv_w_ref, conv_b_ref, wq_ref, wk_ref, wv_ref, wif_t_ref, bif_t_ref,
                 norm_w_ref, skip_ref, w_out_ref, ln_g_ref, ln_b_ref,
                 o_ref,
                 xm_buf, ct_ref, m_ref, *, ts, aw, bw):
    dh = bw // B_HEADS
    agd = aw // A_GROUPS
    nck = ts // CHUNK
    pad = 8
    j = pl.program_id(1)

    @pl.when(j == 0)
    def _():
        xm_buf[0:pad, :] = jnp.zeros((pad, bw), F32)
        ct_ref[...] = jnp.zeros_like(ct_ref)
        m_ref[...] = jnp.zeros_like(m_ref)

    row = lax.broadcasted_iota(jnp.int32, (CHUNK, CHUNK), 0)
    col = lax.broadcasted_iota(jnp.int32, (CHUNK, CHUNK), 1)
    causal = col <= row
    diag = col == row
    triu = jnp.where(row <= col, 1.0, 0.0).astype(BF16)
    ones_blk = jnp.ones((CHUNK, LANES), BF16)

    x = x_ref[0]
    proj = _dot(x.astype(BF16), w_in_ref[...])
    a_u = _gelu(proj[:, :aw])
    a_v = _gelu(proj[:, aw:2 * aw])
    xm = proj[:, 2 * aw:2 * aw + bw]
    z = proj[:, 2 * aw + bw:]

    vn = _ln(a_v, gm_g_ref[...], gm_b_ref[...]).astype(BF16)
    ya_chunks = []
    for c in range(nck):
        cols = []
        for g in range(A_GROUPS):
            v_cg = vn[c * CHUNK:(c + 1) * CHUNK, g * agd:(g + 1) * agd]
            cols.append(_dot(gm_w_ref[g], v_cg))
        ya_chunks.append(jnp.concatenate(cols, axis=1) + gm_bias_ref[...])
    y_a = a_u * jnp.concatenate(ya_chunks, axis=0)

    xm_buf[pad:pad + ts, :] = xm
    conv = conv_b_ref[...] + conv_w_ref[B_CONV - 1:B_CONV, :] * xm
    for k in range(B_CONV - 1):
        sh = B_CONV - 1 - k
        conv = conv + conv_w_ref[k:k + 1, :] * xm_buf[pad - sh:pad - sh + ts, :]
    xm_buf[pad - (B_CONV - 1):pad, :] = xm_buf[pad + ts - (B_CONV - 1):pad + ts, :]
    xc = _silu(conv)
    xc_b = xc.astype(BF16)
    q = _dot_head_pairs(xc_b, wq_ref)
    k_ = _dot_head_pairs(xc_b, wk_ref)
    v = _dot_head_pairs(xm.astype(BF16), wv_ref)
    gate_in = jnp.concatenate([q, k_, v], axis=1).astype(BF16)
    gates_t = _dot_nt(wif_t_ref[...], gate_in) + bif_t_ref[...]
    ig_all = gates_t[:B_HEADS, :]
    lf_all = _log_sigmoid(gates_t[B_HEADS:, :])
    q_b = q.astype(BF16)
    k_b = (k_ * dh ** -0.5).astype(BF16)
    v_b = v.astype(BF16)
    gate_z = _silu(z)

    lf_c = [lf_all[:, c * CHUNK:(c + 1) * CHUNK] for c in range(nck)]
    b_all = _split_dot(jnp.concatenate(lf_c, axis=0), triu)
    m_prev = m_ref[...]
    a_c, m_c, decay_c, wg_c = [], [], [], []
    for c in range(nck):
        b_r = b_all[c * B_HEADS:(c + 1) * B_HEADS]
        a_r = ig_all[:, c * CHUNK:(c + 1) * CHUNK] - b_r
        bl = b_r[:, CHUNK - 1:CHUNK]
        g_r = bl + a_r
        m_new = jnp.maximum(bl + m_prev, jnp.max(g_r, axis=1, keepdims=True))
        a_c.append(a_r)
        m_c.append(m_prev)
        decay_c.append(jnp.exp(bl + m_prev - m_new))
        wg_c.append(jnp.exp(g_r - m_new))
        m_prev = m_new
    m_ref[...] = m_prev

    units = [(c, h) for c in range(nck) for h in range(B_HEADS)]
    lmat = jnp.concatenate([jnp.where(causal, lf_c[c][h:h + 1, :], 0.0) for c, h in units], axis=0)
    dgm = jnp.concatenate([jnp.where(diag, wg_c[c][h:h + 1, :], 0.0) for c, h in units], axis=0)
    b_t_all = _split_dot(lmat, ones_blk)
    wg_t_all = _dot(dgm.astype(BF16), ones_blk)

    caugs = [ct_ref[h] for h in range(B_HEADS)]
    h_chunks = []
    for c in range(nck):
        sl = slice(c * CHUNK, (c + 1) * CHUNK)
        heads = []
        for h in range(B_HEADS):
            u = c * B_HEADS + h
            us = slice(u * CHUNK, (u + 1) * CHUNK)
            hs = slice(h * dh, (h + 1) * dh)
            qh, kh = q_b[sl, hs], k_b[sl, hs]
            vaug = jnp.concatenate([v_b[sl, hs], ones_blk], axis=1)
            m_row = m_c[c][h:h + 1, :]
            amat = jnp.where(causal, a_c[c][h:h + 1, :], NEG)
            mx = jnp.maximum(jnp.max(amat, axis=1, keepdims=True), m_row)
            w_intra = jnp.exp(amat - mx)
            w_state = jnp.exp(m_row - mx)
            s = _dot_nt(qh, kh) * w_intra
            kw = (kh.astype(F32) * wg_t_all[us]).astype(BF16)
            both = _dot(jnp.concatenate([s.astype(BF16), kw.T], axis=0), vaug)
            caug = caugs[h]
            naug = both[:CHUNK] + jnp.concatenate([w_state, w_state], axis=1) * _dot(qh, caug.astype(BF16))
            num, nq = naug[:, :dh], naug[:, dh:]
            hv = num / jnp.maximum(jnp.abs(nq), jnp.exp(-(b_t_all[us] + mx)))
            decay = decay_c[c][h:h + 1, :]
            caugs[h] = jnp.concatenate([decay, decay], axis=1) * caug + both[CHUNK:]
            hc = hv - jnp.mean(hv, axis=1, keepdims=True)
            hn = hc * lax.rsqrt(jnp.mean(hc * hc, axis=1, keepdims=True) + LN_EPS)
            heads.append(hn)
        h_chunks.append(jnp.concatenate(heads, axis=1))
    for h in range(B_HEADS):
        ct_ref[h] = caugs[h]
    hn_all = jnp.concatenate(h_chunks, axis=0) if nck > 1 else h_chunks[0]
    y_b = (hn_all * norm_w_ref[...] + skip_ref[...] * xc) * gate_z

    mix = jnp.concatenate([y_a, y_b], axis=1).astype(BF16)
    y = _dot(mix, w_out_ref[...])
    o_ref[0] = _ln(DN_ALPHA * x + y, ln_g_ref[...], ln_b_ref[...])


def _block_diag_pairs(w):
    hh, d, _ = w.shape
    wp = w.reshape(hh // 2, 2, d, d)
    eye = jnp.eye(2, dtype=w.dtype)
    return jnp.einsum('pade,ab->padbe', wp, eye).reshape(hh // 2, 2 * d, 2 * d)


def _dot_head_pairs(x, w_ref):
    npair, w2, _ = w_ref.shape
    return jnp.concatenate([_dot(x[:, p * w2:(p + 1) * w2], w_ref[p]) for p in range(npair)], axis=1)


def even_mixer_layer(x, w_in, gm_ln_g, gm_ln_b, gm_ws, gm_bs, conv_w, conv_b, wq, wk, wv, w_if,
                     b_if, norm_w, skip, w_out, ln_g, ln_b, *, ts):
    bsz, seq, d = x.shape
    aw = gm_ln_g.shape[0]
    bw = conv_b.shape[0]
    agd = aw // A_GROUPS
    causal = jnp.tril(jnp.ones((CHUNK, CHUNK), dtype=bool))
    gm_w = jnp.where(causal[None], gm_ws, 0.0).astype(BF16)
    gm_bias = jnp.repeat(gm_bs.T, agd, axis=1)
    row = lambda a: a.reshape(1, -1)
    args = (x, w_in.astype(BF16), row(gm_ln_g), row(gm_ln_b), gm_w, gm_bias,
            conv_w, row(conv_b), _block_diag_pairs(wq).astype(BF16), _block_diag_pairs(wk).astype(BF16),
            _block_diag_pairs(wv).astype(BF16), w_if.T.astype(BF16),
            b_if.reshape(-1, 1), row(norm_w), row(skip), w_out.astype(BF16),
            row(ln_g), row(ln_b))
    in_specs = [pl.BlockSpec((1, ts, d), lambda b, j: (b, j, 0))] + [_full(a.shape) for a in args[1:]]
    dh = bw // B_HEADS
    assert dh == LANES and CHUNK == LANES
    return pl.pallas_call(
        functools.partial(_even_kernel, ts=ts, aw=aw, bw=bw),
        grid=(bsz, seq // ts),
        in_specs=in_specs,
        out_specs=pl.BlockSpec((1, ts, d), lambda b, j: (b, j, 0)),
        out_shape=jax.ShapeDtypeStruct((bsz, seq, d), F32),
        scratch_shapes=[pltpu.VMEM((8 + ts, bw), F32),
                        pltpu.VMEM((B_HEADS, dh, dh + LANES), F32),
                        pltpu.VMEM((B_HEADS, LANES), F32)],
        compiler_params=_cparams(("arbitrary", "arbitrary")),
        name="even_mixer",
    )(*args)


def _memfold_kernel(mem_ref, wkv_ref, wq_ref, wo_ref, wqk_ref, vo_ref, *, d):
    dh = d // X_HEADS
    m_len = mem_ref.shape[1]
    kv = _dot(mem_ref[0].astype(BF16), wkv_ref[...])
    k = (kv[:, :d] * (dh ** -0.5 * LOG2E)).astype(BF16)
    v = kv[:, d:].astype(BF16)
    for h in range(X_HEADS):
        hs = slice(h * dh, (h + 1) * dh)
        ms = slice(h * m_len, (h + 1) * m_len)
        wqk_ref[0, :, ms] = _dot_nt(wq_ref[:, hs], k[:, hs]).astype(BF16)
        vo_ref[0, ms, :] = _dot(v[:, hs], wo_ref[hs, :]).astype(BF16)


def memory_fold(mem, wkv, wq, wo):
    bsz, m_len, d = mem.shape
    hm = X_HEADS * m_len
    args = (mem, wkv.astype(BF16), wq.astype(BF16), wo.astype(BF16))
    return pl.pallas_call(
        functools.partial(_memfold_kernel, d=d),
        grid=(bsz,),
        in_specs=[pl.BlockSpec((1, m_len, d), lambda b: (b, 0, 0))] + [_full(a.shape) for a in args[1:]],
        out_specs=[pl.BlockSpec((1, d, hm), lambda b: (b, 0, 0)),
                   pl.BlockSpec((1, hm, d), lambda b: (b, 0, 0))],
        out_shape=[jax.ShapeDtypeStruct((bsz, d, hm), BF16),
                   jax.ShapeDtypeStruct((bsz, hm, d), BF16)],
        compiler_params=_cparams(("arbitrary",)),
        name="memory_fold",
    )(*args)


ROUTE_W = 128
XATTN_TS = 1024
DEST_ROWS = 8
SLOT_TB = 2048
SLOT_SUB = 512


def _xattn_kernel(x_ref, wqk_ref, vo_ref, ln_g_ref, ln_b_ref, wr_ref, br_ref,
                  o_ref, xp_ref, route_ref, cnt_ref):
    m_len = wqk_ref.shape[2] // X_HEADS

    @pl.when(pl.program_id(1) == 0)
    def _():
        cnt_ref[...] = jnp.zeros_like(cnt_ref)

    x = x_ref[0]
    ts = x.shape[0]
    xb = x.astype(BF16)
    probs = []
    for h in range(X_HEADS):
        s = _dot(xb, wqk_ref[0, :, h * m_len:(h + 1) * m_len])
        p = jnp.exp2(s - jnp.max(s, axis=1, keepdims=True))
        probs.append((p / jnp.sum(p, axis=1, keepdims=True)).astype(BF16))
    y = _dot(jnp.concatenate(probs, axis=1), vo_ref[0])
    x2 = _ln(DN_ALPHA * x + y, ln_g_ref[...], ln_b_ref[...])
    o_ref[0] = x2
    xp_ref[0] = _pack_halves(x2)

    x_hi = x2.astype(BF16)
    x_lo = (x2 - x_hi.astype(F32)).astype(BF16)
    logits = (_dot(x_hi, wr_ref[0]) + _dot(x_lo, wr_ref[0]) + _dot(x_hi, wr_ref[1])) + br_ref[...]
    lane = lax.broadcasted_iota(jnp.int32, (ts, ROUTE_W), 1)
    is_g = lane < N_GROUPS
    lg = jnp.where(is_g, logits, NEG)
    mg = jnp.max(lg, axis=1, keepdims=True)
    gi = jnp.min(jnp.where(jnp.logical_and(is_g, lg == mg), lane, ROUTE_W), axis=1, keepdims=True)
    gate_g = 1.0 / jnp.sum(jnp.where(is_g, jnp.exp(lg - mg), 0.0), axis=1, keepdims=True)
    lo = N_GROUPS + gi * EXPERTS_PER_GROUP
    in_grp = jnp.logical_and(lane >= lo, lane < lo + EXPERTS_PER_GROUP)
    le = jnp.where(in_grp, logits, NEG)
    v1 = jnp.max(le, axis=1, keepdims=True)
    i1 = jnp.min(jnp.where(jnp.logical_and(in_grp, le == v1), lane, ROUTE_W), axis=1, keepdims=True)
    le2 = jnp.where(lane == i1, NEG, le)
    v2 = jnp.max(le2, axis=1, keepdims=True)
    i2 = jnp.min(jnp.where(jnp.logical_and(in_grp, le2 == v2), lane, ROUTE_W), axis=1, keepdims=True)
    e21 = jnp.exp(v2 - v1)
    p1 = 1.0 / (1.0 + e21)
    p2 = e21 * p1
    e1 = (i1 - N_GROUPS).astype(F32)
    e2 = (i2 - N_GROUPS).astype(F32)
    rec = jnp.where(lane == 0, e1, 0.0)
    rec = jnp.where(lane == 1, e2, rec)
    rec = jnp.where(lane == 2, gate_g * p1, rec)
    rec = jnp.where(lane == 3, gate_g * p2, rec)
    route_ref[0] = rec
    sel = jnp.logical_or(lane == i1 - N_GROUPS, lane == i2 - N_GROUPS)
    cnt_ref[0] += jnp.sum(jnp.where(sel, 1.0, 0.0), axis=0, keepdims=True)


def xattn_router_layer(x, wqk, vo, ln_g, ln_b, w_rg, b_rg, w_re, b_re, *, ts):
    bsz, seq, d = x.shape
    hm = wqk.shape[2]
    wr = jnp.zeros((d, ROUTE_W), F32).at[:, :N_GROUPS].set(w_rg).at[:, N_GROUPS:N_GROUPS + N_EXPERTS].set(w_re)
    br = jnp.zeros((1, ROUTE_W), F32).at[0, :N_GROUPS].set(b_rg).at[0, N_GROUPS:N_GROUPS + N_EXPERTS].set(b_re)
    wr_hi = wr.astype(BF16)
    wr_lo = (wr - wr_hi.astype(F32)).astype(BF16)
    wr = jnp.stack([wr_hi, wr_lo])
    args = (x, wqk, vo, ln_g.reshape(1, -1), ln_b.reshape(1, -1), wr, br)
    in_specs = [pl.BlockSpec((1, ts, d), lambda b, j: (b, j, 0)),
                pl.BlockSpec((1, d, hm), lambda b, j: (b, 0, 0)),
                pl.BlockSpec((1, hm, d), lambda b, j: (b, 0, 0))] + [_full(a.shape) for a in args[3:]]
    return pl.pallas_call(
        _xattn_kernel,
        grid=(bsz, seq // ts),
        in_specs=in_specs,
        out_specs=[pl.BlockSpec((1, ts, d), lambda b, j: (b, j, 0)),
                   pl.BlockSpec((1, ts, d // 2), lambda b, j: (b, j, 0)),
                   pl.BlockSpec((1, ts, ROUTE_W), lambda b, j: (b, j, 0)),
                   pl.BlockSpec((1, 1, ROUTE_W), lambda b, j: (b, 0, 0))],
        out_shape=[jax.ShapeDtypeStruct((bsz, seq, d), F32),
                   jax.ShapeDtypeStruct((bsz, seq, d // 2), jnp.int32),
                   jax.ShapeDtypeStruct((bsz, seq, ROUTE_W), F32),
                   jax.ShapeDtypeStruct((bsz, 1, ROUTE_W), F32)],
        compiler_params=_cparams(("arbitrary", "arbitrary")),
        name="xattn_router",
    )(*args)


def _moe_output(x2, g0, g1, rec, ln_g, ln_b):
    a_lo, a_hi = _unpack_halves(g0)
    b_lo, b_hi = _unpack_halves(g1)
    w0, w1 = rec[:, 2:3], rec[:, 3:4]
    y = jnp.concatenate([w0 * a_lo + w1 * b_lo, w0 * a_hi + w1 * b_hi], axis=1)
    return _ln(DN_ALPHA * x2 + y, ln_g, ln_b)


def _swa_kernel(x_ref, g0_ref, g1_ref, route_ref, pln_g_ref, pln_b_ref,
                wqkv_ref, bqkv_ref, cos_ref, sin_ref, sink_ref, wo_ref, ln_g_ref, ln_b_ref,
                *rest, ts, cq, ckv):
    o_ref, kprev, vprev = rest[-3:]
    j = pl.program_id(1)
    nb = ts // CHUNK
    dh = C_HEAD_DIM
    grp = (cq // dh) // C_KV_HEADS

    @pl.when(j == 0)
    def _():
        kprev[...] = jnp.zeros_like(kprev)
        vprev[...] = jnp.zeros_like(vprev)

    x = _moe_output(x_ref[0], g0_ref[0, 0], g1_ref[0, 0], route_ref[0], pln_g_ref[...], pln_b_ref[...])
    qkv = _dot(x.astype(BF16), wqkv_ref[...]) + bqkv_ref[...]
    cos = cos_ref[...]
    sin = sin_ref[...]
    lane = lax.broadcasted_iota(jnp.int32, (ts, LANES), 1)
    first_half = (lane % dh) < (dh // 2)

    def rope(t):
        outs = []
        for c in range(t.shape[1] // LANES):
            tc = t[:, c * LANES:(c + 1) * LANES]
            rot = jnp.where(first_half, pltpu.roll(tc, LANES - dh // 2, 1), pltpu.roll(tc, dh // 2, 1))
            outs.append(tc * cos + rot * sin)
        return jnp.concatenate(outs, axis=1)

    q = rope(qkv[:, :cq]).astype(BF16)
    k = rope(qkv[:, cq:cq + 2 * ckv]).astype(BF16)
    v = qkv[:, cq + 2 * ckv:].astype(BF16)

    r_i = lax.broadcasted_iota(jnp.int32, (CHUNK, 2 * CHUNK), 0)
    c_i = lax.broadcasted_iota(jnp.int32, (CHUNK, 2 * CHUNK), 1)
    band = jnp.logical_and(c_i > r_i, c_i <= r_i + CHUNK)
    sink_col = c_i == 0
    lane_k = lax.broadcasted_iota(jnp.int32, (2 * CHUNK, LANES), 1)
    key_row = lax.broadcasted_iota(jnp.int32, (2 * CHUNK, LANES), 0)
    lane_q = lax.broadcasted_iota(jnp.int32, (CHUNK, LANES), 1)
    ones_blk = jnp.ones((2 * CHUNK, LANES), BF16)
    zero_b = jnp.zeros((), BF16)
    blocks = []
    for c in range(nb):
        sl = slice(c * CHUNK, (c + 1) * CHUNK)
        if c == 0:
            kb = jnp.concatenate([kprev[...].astype(BF16), k[sl]], axis=0)
            vb = jnp.concatenate([vprev[...].astype(BF16), v[sl]], axis=0)
            first_key = jnp.where(j > 0, 0, CHUNK)
            valid = jnp.logical_and(band, c_i >= first_key)
        else:
            kb = k[(c - 1) * CHUNK:(c + 1) * CHUNK]
            vb = v[(c - 1) * CHUNK:(c + 1) * CHUNK]
            valid = band
        tiles = []
        for h in range(C_KV_HEADS):
            kd = kb[:, h * LANES:(h + 1) * LANES]
            vd = vb[:, h * LANES:(h + 1) * LANES]
            k_lo = jnp.where(lane_k < dh, kd, zero_b)
            k_hi = jnp.where(lane_k >= dh, kd, zero_b)
            vz = jnp.where(key_row == 0, zero_b, vd)
            q2 = jnp.concatenate([q[sl, (2 * h) * LANES:(2 * h + 1) * LANES],
                                  q[sl, (2 * h + 1) * LANES:(2 * h + 2) * LANES]], axis=0)
            s_lo = _dot_nt(q2, k_lo)
            s_hi = _dot_nt(q2, k_hi)
            parts = []
            for qk, g in ((s_lo[:CHUNK], 0), (s_lo[CHUNK:], 2), (s_hi[:CHUNK], 1), (s_hi[CHUNK:], 3)):
                sink = sink_ref[h * grp + g]
                parts.append(jnp.where(valid, qk, jnp.where(sink_col, sink, NEG)))
            s = jnp.concatenate(parts, axis=0)
            p = jnp.exp2(s - jnp.max(s, axis=1, keepdims=True)).astype(BF16)
            den = _dot(p, ones_blk)
            o2 = _dot(p, vz) / den
            tiles.append(jnp.where(lane_q < dh, o2[:CHUNK], o2[2 * CHUNK:3 * CHUNK]))
            tiles.append(jnp.where(lane_q < dh, o2[CHUNK:2 * CHUNK], o2[3 * CHUNK:]))
        blocks.append(jnp.concatenate(tiles, axis=1))
    kprev[...] = k[(nb - 1) * CHUNK:].astype(F32)
    vprev[...] = v[(nb - 1) * CHUNK:].astype(F32)
    att = (jnp.concatenate(blocks, axis=0) if nb > 1 else blocks[0]).astype(BF16)
    y = _dot(att, wo_ref[...])
    o_ref[0] = _ln(DN_ALPHA * x + y, ln_g_ref[...], ln_b_ref[...])


def swa_mixer_layer(pending, w_qkv, b_qkv, sinks, w_o, ln_g, ln_b, *, ts):
    x, gs, route, pln_g, pln_b = pending
    bsz, seq, d = x.shape
    cq = w_o.shape[0]
    ckv = (w_qkv.shape[1] - cq) // 2
    dh = C_HEAD_DIM
    inv = ROPE_THETA ** (-jnp.arange(0, dh, 2, dtype=F32) / dh)
    ang = jnp.arange(seq, dtype=F32)[:, None] * inv[None, :]
    reps = LANES // (dh // 2)
    sign = jnp.tile(jnp.concatenate([-jnp.ones((dh // 2,), F32), jnp.ones((dh // 2,), F32)]), LANES // dh)
    cos_t = jnp.tile(jnp.cos(ang), (1, reps))
    sin_t = jnp.tile(jnp.sin(ang), (1, reps)) * sign[None, :]
    assert 2 * dh == LANES

    def dup_heads(t):
        th = t.reshape(t.shape[:-1] + (ckv // dh, dh))
        return jnp.concatenate([th, th], axis=-1).reshape(t.shape[:-1] + (2 * ckv,))

    qs = dh ** -0.5 * LOG2E
    w_all = jnp.concatenate([w_qkv[:, :cq] * qs, dup_heads(w_qkv[:, cq:cq + ckv]),
                             dup_heads(w_qkv[:, cq + ckv:])], axis=1)
    b_all = jnp.concatenate([b_qkv[:cq] * qs, dup_heads(b_qkv[cq:cq + ckv]), dup_heads(b_qkv[cq + ckv:])])
    shared = (pln_g.reshape(1, -1), pln_b.reshape(1, -1),
              w_all.astype(BF16), b_all.reshape(1, -1), cos_t, sin_t, sinks.astype(F32) * LOG2E,
              w_o.astype(BF16), ln_g.reshape(1, -1), ln_b.reshape(1, -1))
    bp = bsz // len(gs)
    out = None
    for p, g in enumerate(gs):
        b0 = p * bp
        g = g.reshape(2, bp, seq, d // 2)
        args = [x, g, g, route, *shared]
        in_specs = [pl.BlockSpec((1, ts, d), lambda b, j: (b0 + b, j, 0)),
                    pl.BlockSpec((1, 1, ts, d // 2), lambda b, j: (0, b, j, 0)),
                    pl.BlockSpec((1, 1, ts, d // 2), lambda b, j: (1, b, j, 0)),
                    pl.BlockSpec((1, ts, ROUTE_W), lambda b, j: (b0 + b, j, 0)),
                    _full((1, d)), _full((1, d)),
                    _full(shared[2].shape), _full(shared[3].shape),
                    pl.BlockSpec((ts, LANES), lambda b, j: (j, 0)),
                    pl.BlockSpec((ts, LANES), lambda b, j: (j, 0)),
                    pl.BlockSpec(memory_space=pltpu.SMEM),
                    _full(shared[7].shape), _full(shared[8].shape), _full(shared[9].shape)]
        aliases = {}
        if out is not None:
            args.append(out)
            in_specs.append(pl.BlockSpec(memory_space=pl.ANY))
            aliases = {len(args) - 1: 0}
        out = pl.pallas_call(
            functools.partial(_swa_kernel, ts=ts, cq=cq, ckv=ckv),
            grid=(bp, seq // ts),
            in_specs=in_specs,
            out_specs=pl.BlockSpec((1, ts, d), lambda b, j: (b0 + b, j, 0)),
            out_shape=jax.ShapeDtypeStruct((bsz, seq, d), F32),
            input_output_aliases=aliases,
            scratch_shapes=[pltpu.VMEM((CHUNK, 2 * ckv), F32), pltpu.VMEM((CHUNK, 2 * ckv), F32)],
            compiler_params=_cparams(("arbitrary", "arbitrary")),
            name="swa_mixer",
        )(*args)
    return out


def _slot_kernel(route_ref, pstart_ref, dest_ref, carry_ref, *, tb):
    @pl.when(pl.program_id(0) == 0)
    def _():
        carry_ref[...] = pstart_ref[...]

    sub = min(SLOT_SUB, tb)
    lane = lax.broadcasted_iota(jnp.int32, (sub, ROUTE_W), 1)
    r = lax.broadcasted_iota(jnp.int32, (sub, sub), 0)
    c = lax.broadcasted_iota(jnp.int32, (sub, sub), 1)
    before = jnp.where(c < r, 1.0, 0.0).astype(BF16)
    carry = carry_ref[...]
    for k in range(tb // sub):
        rows = slice(k * sub, (k + 1) * sub)
        rec = route_ref[rows, :]
        e0 = rec[:, 0:1].astype(jnp.int32)
        e1 = rec[:, 1:2].astype(jnp.int32)
        oh0 = lane == e0
        oh1 = lane == e1
        ohs = jnp.where(jnp.logical_or(oh0, oh1), 1.0, 0.0)
        prefix = _dot(before, ohs.astype(BF16)) + carry
        d0 = jnp.sum(jnp.where(oh0, prefix, 0.0), axis=1, keepdims=True)
        d1 = jnp.sum(jnp.where(oh1, prefix, 0.0), axis=1, keepdims=True)
        dest = jnp.where(lane == 0, d0, jnp.where(lane == 1, d1, 0.0))
        dest_ref[:, rows] = dest.T[:DEST_ROWS].astype(jnp.int32)
        carry = carry + jnp.sum(ohs, axis=0, keepdims=True)
    carry_ref[...] = carry


def moe_slots(route, pstart, *, tb):
    n = route.shape[0]
    return pl.pallas_call(
        functools.partial(_slot_kernel, tb=tb),
        grid=(n // tb,),
        in_specs=[pl.BlockSpec((tb, ROUTE_W), lambda i: (i, 0)), _full((1, ROUTE_W))],
        out_specs=pl.BlockSpec((DEST_ROWS, tb), lambda i: (0, i)),
        out_shape=jax.ShapeDtypeStruct((DEST_ROWS, n), jnp.int32),
        scratch_shapes=[pltpu.VMEM((1, ROUTE_W), F32)],
        compiler_params=_cparams(("arbitrary",)),
        name="moe_slots",
    )(route, pstart)


def _ffn_kernel(blk_exp_ref, new_exp_ref, nblk_ref, xs_ref, w1_ref, w3_ref, w2_ref, ys_ref,
                w1_b, w3_b, w2_b):
    i = pl.program_id(0)
    used = i < nblk_ref[0]

    @pl.when(jnp.logical_and(used, new_exp_ref[i] == 1))
    def _():
        w1_b[...] = w1_ref[0, 0].astype(BF16)
        w3_b[...] = w3_ref[0, 0].astype(BF16)
        w2_b[...] = w2_ref[0, 0].astype(BF16)

    @pl.when(used)
    def _():
        x_lo, x_hi = _unpack_halves(xs_ref[...])
        x_lo, x_hi = x_lo.astype(BF16), x_hi.astype(BF16)
        dl = x_lo.shape[1]
        h1 = _dot(x_lo, w1_b[:dl, :]) + _dot(x_hi, w1_b[dl:, :])
        h3 = _dot(x_lo, w3_b[:dl, :]) + _dot(x_hi, w3_b[dl:, :])
        h = (_silu(h1) * h3).astype(BF16)
        ys_ref[...] = _pack_halves(_dot(h, w2_b[...]))

    @pl.when(jnp.logical_not(used))
    def _():
        ys_ref[...] = jnp.zeros_like(ys_ref)


def moe_ffn(xs, blk_exp, new_exp, nblk, w1, w3, w2, *, layer, bm):
    n_pad, dp = xs.shape
    d = 2 * dp
    de = w1.shape[3]
    n_blk = n_pad // bm

    def x_map(i, be, ne, nb):
        return (jnp.minimum(i, nb[0] - 1), 0)

    def w_map(i, be, ne, nb):
        return (layer, be[i], 0, 0)

    return pl.pallas_call(
        _ffn_kernel,
        grid_spec=pltpu.PrefetchScalarGridSpec(
            num_scalar_prefetch=3,
            grid=(n_blk,),
            in_specs=[pl.BlockSpec((bm, dp), x_map),
                      pl.BlockSpec((1, 1, d, de), w_map),
                      pl.BlockSpec((1, 1, d, de), w_map),
                      pl.BlockSpec((1, 1, de, d), w_map)],
            out_specs=pl.BlockSpec((bm, dp), lambda i, be, ne, nb: (i, 0)),
            scratch_shapes=[pltpu.VMEM((d, de), BF16), pltpu.VMEM((d, de), BF16),
                            pltpu.VMEM((de, d), BF16)]),
        out_shape=jax.ShapeDtypeStruct((n_pad, dp), jnp.int32),
        compiler_params=_cparams(("arbitrary",)),
        name="moe_ffn",
    )(blk_exp, new_exp, nblk, xs, w1, w3, w2)


def _combine_kernel(x_ref, g0_ref, g1_ref, route_ref, ln_g_ref, ln_b_ref, *rest):
    o_ref = rest[-1]
    o_ref[...] = _moe_output(x_ref[...], g0_ref[0], g1_ref[0], route_ref[...], ln_g_ref[...], ln_b_ref[...])


def moe_combine(x, g, route, ln_g, ln_b, *, row0, tb, prev=None):
    n, d = x.shape
    m = g.shape[1]
    blk0 = row0 // tb
    row_spec = pl.BlockSpec((tb, d), lambda i: (blk0 + i, 0))
    args = [x, g, g, route, ln_g.reshape(1, -1), ln_b.reshape(1, -1)]
    in_specs = [row_spec,
                pl.BlockSpec((1, tb, d // 2), lambda i: (0, i, 0)),
                pl.BlockSpec((1, tb, d // 2), lambda i: (1, i, 0)),
                pl.BlockSpec((tb, ROUTE_W), lambda i: (blk0 + i, 0)),
                _full((1, d)), _full((1, d))]
    aliases = {}
    if prev is not None:
        args.append(prev)
        in_specs.append(pl.BlockSpec(memory_space=pl.ANY))
        aliases = {len(args) - 1: 0}
    return pl.pallas_call(
        _combine_kernel,
        grid=(m // tb,),
        in_specs=in_specs,
        out_specs=row_spec,
        out_shape=jax.ShapeDtypeStruct((n, d), F32),
        input_output_aliases=aliases,
        compiler_params=_cparams(("arbitrary",)),
        name="moe_combine",
    )(*args)


MOE_BM = 1024
COMBINE_PARTS = 4
COMBINE_TB = 1024
MIXER_PARTS = 2
MIXER_TS = 1024


def hierarchical_moe_layer(x2, xp, route, counts, w1, w3, w2, ln_g, ln_b, *, layer, defer_combine):
    bsz, seq, d = x2.shape
    n = bsz * seq
    bm = MOE_BM
    rt = route.reshape(n, ROUTE_W)
    tb = min(512, n)
    n_blk = (2 * n) // bm + N_EXPERTS
    cnt = counts.sum(axis=0)[0, :N_EXPERTS].astype(jnp.int32)
    pcnt = (cnt + bm - 1) // bm * bm
    pends = jnp.cumsum(pcnt)
    pstart = pends - pcnt
    nblk = (pends[-1] // bm).astype(jnp.int32).reshape(1)
    blk_row = jnp.arange(n_blk, dtype=jnp.int32) * bm
    blk_exp = jnp.minimum(jnp.sum((pends[None, :] <= blk_row[:, None]).astype(jnp.int32), axis=1),
                          N_EXPERTS - 1)
    last_exp = blk_exp[jnp.maximum(nblk[0] - 1, 0)]
    blk_exp = jnp.where(jnp.arange(n_blk) < nblk[0], blk_exp, last_exp)
    new_exp = jnp.concatenate([jnp.ones((1,), jnp.int32),
                               (blk_exp[1:] != blk_exp[:-1]).astype(jnp.int32)])
    pstart_rec = jnp.zeros((1, ROUTE_W), F32).at[0, :N_EXPERTS].set(pstart.astype(F32))
    dest = moe_slots(rt, pstart_rec, tb=min(SLOT_TB, n))
    xs = moe_dispatch(xp.reshape(n, d // 2), dest[0], dest[1], n_blk * bm)
    ys = moe_ffn(xs, blk_exp, new_exp, nblk, w1, w3, w2, layer=layer, bm=bm)
    if defer_combine:
        bparts = MIXER_PARTS if bsz % MIXER_PARTS == 0 else 1
        mb = n // bparts
        gs = [moe_gather(ys, dest[0, p * mb:(p + 1) * mb], dest[1, p * mb:(p + 1) * mb]) for p in range(bparts)]
        return x2, gs, route, ln_g, ln_b
    parts = COMBINE_PARTS if n % (COMBINE_PARTS * tb * 8) == 0 else 1
    m = n // parts
    out = None
    for p in range(parts):
        g = moe_gather(ys, dest[0, p * m:(p + 1) * m], dest[1, p * m:(p + 1) * m])
        out = moe_combine(x2.reshape(n, d), g, rt, ln_g, ln_b, row0=p * m, tb=min(COMBINE_TB, m), prev=out)
    return out.reshape(bsz, seq, d)


SC_ROWS = 128


def _sc_mesh():
    return plsc.VectorSubcoreMesh(core_axis_name="c", subcore_axis_name="s")


def moe_dispatch(xf, dest0, dest1, n_pad):
    n, d = xf.shape
    info = plsc.get_sparse_core_info()
    nw = info.num_cores * info.num_subcores
    per_w = n // nw
    r = min(SC_ROWS, per_w)

    def body(x_hbm, d0_hbm, d1_hbm, xs_hbm, i0_v, i1_v, rows_v, sem):
        wid = lax.axis_index("s") * info.num_cores + lax.axis_index("c")

        @pl.loop(0, per_w // r)
        def _(c):
            base = pl.multiple_of(wid * per_w + c * r, 8)
            pltpu.sync_copy(d0_hbm.at[pl.ds(base, r)], i0_v)
            pltpu.sync_copy(d1_hbm.at[pl.ds(base, r)], i1_v)
            pltpu.sync_copy(x_hbm.at[pl.ds(base, r)], rows_v)
            pltpu.async_copy(rows_v, xs_hbm.at[i0_v], sem).wait()
            pltpu.async_copy(rows_v, xs_hbm.at[i1_v], sem).wait()

    return pl.kernel(
        body, out_type=jax.ShapeDtypeStruct((n_pad, d), xf.dtype), mesh=_sc_mesh(),
        scratch_types=[pltpu.VMEM((r,), jnp.int32), pltpu.VMEM((r,), jnp.int32),
                       pltpu.VMEM((r, d), xf.dtype), pltpu.SemaphoreType.DMA],
        name="moe_dispatch",
    )(xf, dest0, dest1)


def moe_gather(ys, dest0, dest1):
    n = dest0.shape[0]
    d = ys.shape[1]
    info = plsc.get_sparse_core_info()
    nw = info.num_cores * info.num_subcores
    per_w = n // nw
    r = min(SC_ROWS, per_w)

    def body(ys_hbm, d0_hbm, d1_hbm, g_hbm, i_v, rows_v, sem):
        wid = lax.axis_index("s") * info.num_cores + lax.axis_index("c")

        @pl.loop(0, per_w // r)
        def _(c):
            base = pl.multiple_of(wid * per_w + c * r, 8)
            for k, d_hbm in enumerate((d0_hbm, d1_hbm)):
                pltpu.sync_copy(d_hbm.at[pl.ds(base, r)], i_v)
                pltpu.async_copy(ys_hbm.at[i_v], rows_v, sem).wait()
                pltpu.sync_copy(rows_v, g_hbm.at[k, pl.ds(base, r)])

    return pl.kernel(
        body, out_type=jax.ShapeDtypeStruct((2, n, d), ys.dtype), mesh=_sc_mesh(),
        scratch_types=[pltpu.VMEM((r,), jnp.int32), pltpu.VMEM((r, d), ys.dtype),
                       pltpu.SemaphoreType.DMA],
        name="moe_gather",
    )(ys, dest0, dest1)


def kernel(x, mem, ln_g, ln_b, ev_w_in, ev_gm_ln_g, ev_gm_ln_b, ev_gm_ws, ev_gm_bs, ev_conv_w, ev_conv_b, ev_wq, ev_wk, ev_wv, ev_w_if, ev_b_if, ev_norm_w, ev_skip, ev_w_out, od_w_qkv, od_b_qkv, od_sinks, od_w_o, xa_wq, xa_wkv, xa_wo, moe_w_rg, moe_b_rg, moe_w_re, moe_b_re, moe_w1, moe_w3, moe_w2):
    bsz, seq, d = x.shape
    depth = ln_g.shape[0]
    assert depth == DEPTH
    ts = min(MIXER_TS, seq)
    for l in range(depth):
        if l % 2 == 0:
            e = l // 2
            x = even_mixer_layer(x, ev_w_in[e], ev_gm_ln_g[e], ev_gm_ln_b[e], ev_gm_ws[e], ev_gm_bs[e],
                                 ev_conv_w[e], ev_conv_b[e], ev_wq[e], ev_wk[e], ev_wv[e], ev_w_if[e],
                                 ev_b_if[e], ev_norm_w[e], ev_skip[e], ev_w_out[e],
                                 ln_g[l, 0], ln_b[l, 0], ts=ts)
        else:
            o = l // 2
            x = swa_mixer_layer(x, od_w_qkv[o], od_b_qkv[o], od_sinks[o], od_w_o[o],
                                ln_g[l, 0], ln_b[l, 0], ts=ts)
        wqk, vo = memory_fold(mem, xa_wkv[l], xa_wq[l], xa_wo[l])
        x, xp, route, counts = xattn_router_layer(x, wqk, vo, ln_g[l, 1], ln_b[l, 1],
                                                  moe_w_rg[l], moe_b_rg[l], moe_w_re[l], moe_b_re[l],
                                                  ts=min(XATTN_TS, seq))
        x = hierarchical_moe_layer(x, xp, route, counts, moe_w1, moe_w3, moe_w2,
                                   ln_g[l, 2], ln_b[l, 2], layer=l,
                                   defer_combine=(l + 1 < depth and (l + 1) % 2 == 1))
    return x
```

```python
import functools
import math

import jax
import jax.numpy as jnp
from jax import lax
from jax.experimental import pallas as pl
from jax.experimental.pallas import tpu as pltpu
from jax.experimental.pallas import tpu_sc as plsc

F32 = jnp.float32
BF16 = jnp.bfloat16

A_GROUPS = 4
CHUNK = 128
B_HEADS = 4
B_CONV = 4
C_HEAD_DIM = 64
C_KV_HEADS = 4
X_HEADS = 4
N_GROUPS = 4
EXPERTS_PER_GROUP = 8
N_EXPERTS = N_GROUPS * EXPERTS_PER_GROUP
ROPE_THETA = 10000.0
LN_EPS = 1e-5
DEPTH = 2
DN_ALPHA = (2 * DEPTH) ** 0.25

LANES = 128
VMEM_LIMIT = 48 * 1024 * 1024
NEG = -1e30


def _cparams(sem):
    return pltpu.CompilerParams(dimension_semantics=sem, vmem_limit_bytes=VMEM_LIMIT)


def _full(shape):
    nd = len(shape)
    return pl.BlockSpec(shape, lambda *_: (0,) * nd)


def _dot(a, b):
    return jnp.dot(a, b, preferred_element_type=F32)


def _dot_nt(a, b):
    return lax.dot_general(a, b, (((1,), (1,)), ((), ())), preferred_element_type=F32)


def _split_dot(a, b_bf16):
    hi = a.astype(BF16)
    lo = (a - hi.astype(F32)).astype(BF16)
    return _dot(hi, b_bf16) + _dot(lo, b_bf16)


def _ln(x, g, b):
    mu = jnp.mean(x, axis=-1, keepdims=True)
    xc = x - mu
    var = jnp.mean(xc * xc, axis=-1, keepdims=True)
    return xc * lax.rsqrt(var + LN_EPS) * g + b


LOG2E = math.log2(math.e)


def _silu(x):
    return x * (1.0 / (1.0 + jnp.exp2(x * -LOG2E)))


def _gelu(x):
    return 0.5 * x * (1.0 + jnp.tanh(math.sqrt(2.0 / math.pi) * (x + 0.044715 * (x * x * x))))


def _pack_halves(x):
    c = x.shape[1] // 2
    lo = lax.bitcast_convert_type(x[:, :c].astype(BF16).astype(F32), jnp.uint32)
    hi = lax.bitcast_convert_type(x[:, c:].astype(BF16).astype(F32), jnp.uint32)
    return lax.bitcast_convert_type((lo >> 16) | hi, jnp.int32)


def _unpack_halves(p):
    u = lax.bitcast_convert_type(p, jnp.uint32)
    lo = lax.bitcast_convert_type(u << 16, F32)
    hi = lax.bitcast_convert_type(u & jnp.uint32(0xFFFF0000), F32)
    return lo, hi


def _log_sigmoid(x):
    return jnp.minimum(x, 0.0) - jnp.log(1.0 + jnp.exp(-jnp.abs(x)))


def _even_kernel(x_ref, w_in_ref, gm_g_ref, gm_b_ref, gm_w_ref, gm_bias_ref,
                 conv_w_ref, conv_b_ref, wq_ref, wk_ref, wv_ref, wif_t_ref, bif_t_ref,
                 norm_w_ref, skip_ref, w_out_ref, ln_g_ref, ln_b_ref,
                 o_ref,
                 xm_buf, ct_ref, m_ref, *, ts, aw, bw):
    dh = bw // B_HEADS
    agd = aw // A_GROUPS
    nck = ts // CHUNK
    pad = 8
    j = pl.program_id(1)

    @pl.when(j == 0)
    def _():
        xm_buf[0:pad, :] = jnp.zeros((pad, bw), F32)
        ct_ref[...] = jnp.zeros_like(ct_ref)
        m_ref[...] = jnp.zeros_like(m_ref)

    row = lax.broadcasted_iota(jnp.int32, (CHUNK, CHUNK), 0)
    col = lax.broadcasted_iota(jnp.int32, (CHUNK, CHUNK), 1)
    causal = col <= row
    diag = col == row
    triu = jnp.where(row <= col, 1.0, 0.0).astype(BF16)
    ones_blk = jnp.ones((CHUNK, LANES), BF16)

    x = x_ref[0]
    proj = _dot(x.astype(BF16), w_in_ref[...])
    a_u = _gelu(proj[:, :aw])
    a_v = _gelu(proj[:, aw:2 * aw])
    xm = proj[:, 2 * aw:2 * aw + bw]
    z = proj[:, 2 * aw + bw:]

    vn = _ln(a_v, gm_g_ref[...], gm_b_ref[...]).astype(BF16)
    ya_chunks = []
    for c in range(nck):
        cols = []
        for g in range(A_GROUPS):
            v_cg = vn[c * CHUNK:(c + 1) * CHUNK, g * agd:(g + 1) * agd]
            cols.append(_dot(gm_w_ref[g], v_cg))
        ya_chunks.append(jnp.concatenate(cols, axis=1) + gm_bias_ref[...])
    y_a = a_u * jnp.concatenate(ya_chunks, axis=0)

    xm_buf[pad:pad + ts, :] = xm
    conv = conv_b_ref[...] + conv_w_ref[B_CONV - 1:B_CONV, :] * xm
    for k in range(B_CONV - 1):
        sh = B_CONV - 1 - k
        conv = conv + conv_w_ref[k:k + 1, :] * xm_buf[pad - sh:pad - sh + ts, :]
    xm_buf[pad - (B_CONV - 1):pad, :] = xm_buf[pad + ts - (B_CONV - 1):pad + ts, :]
    xc = _silu(conv)
    xc_b = xc.astype(BF16)
    q = _dot_head_pairs(xc_b, wq_ref)
    k_ = _dot_head_pairs(xc_b, wk_ref)
    v = _dot_head_pairs(xm.astype(BF16), wv_ref)
    gate_in = jnp.concatenate([q, k_, v], axis=1).astype(BF16)
    gates_t = _dot_nt(wif_t_ref[...], gate_in) + bif_t_ref[...]
    ig_all = gates_t[:B_HEADS, :]
    lf_all = _log_sigmoid(gates_t[B_HEADS:, :])
    q_b = q.astype(BF16)
    k_b = (k_ * dh ** -0.5).astype(BF16)
    v_b = v.astype(BF16)
    gate_z = _silu(z)

    lf_c = [lf_all[:, c * CHUNK:(c + 1) * CHUNK] for c in range(nck)]
    b_all = _split_dot(jnp.concatenate(lf_c, axis=0), triu)
    m_prev = m_ref[...]
    a_c, m_c, decay_c, wg_c = [], [], [], []
    for c in range(nck):
        b_r = b_all[c * B_HEADS:(c + 1) * B_HEADS]
        a_r = ig_all[:, c * CHUNK:(c + 1) * CHUNK] - b_r
        bl = b_r[:, CHUNK - 1:CHUNK]
        g_r = bl + a_r
        m_new = jnp.maximum(bl + m_prev, jnp.max(g_r, axis=1, keepdims=True))
        a_c.append(a_r)
        m_c.append(m_prev)
        decay_c.append(jnp.exp(bl + m_prev - m_new))
        wg_c.append(jnp.exp(g_r - m_new))
        m_prev = m_new
    m_ref[...] = m_prev

    units = [(c, h) for c in range(nck) for h in range(B_HEADS)]
    lmat = jnp.concatenate([jnp.where(causal, lf_c[c][h:h + 1, :], 0.0) for c, h in units], axis=0)
    dgm = jnp.concatenate([jnp.where(diag, wg_c[c][h:h + 1, :], 0.0) for c, h in units], axis=0)
    b_t_all = _split_dot(lmat, ones_blk)
    wg_t_all = _dot(dgm.astype(BF16), ones_blk)

    caugs = [ct_ref[h] for h in range(B_HEADS)]
    h_chunks = []
    for c in range(nck):
        sl = slice(c * CHUNK, (c + 1) * CHUNK)
        heads = []
        for h in range(B_HEADS):
            u = c * B_HEADS + h
            us = slice(u * CHUNK, (u + 1) * CHUNK)
            hs = slice(h * dh, (h + 1) * dh)
            qh, kh = q_b[sl, hs], k_b[sl, hs]
            vaug = jnp.concatenate([v_b[sl, hs], ones_blk], axis=1)
            m_row = m_c[c][h:h + 1, :]
            amat = jnp.where(causal, a_c[c][h:h + 1, :], NEG)
            mx = jnp.maximum(jnp.max(amat, axis=1, keepdims=True), m_row)
            w_intra = jnp.exp(amat - mx)
            w_state = jnp.exp(m_row - mx)
            s = _dot_nt(qh, kh) * w_intra
            kw = (kh.astype(F32) * wg_t_all[us]).astype(BF16)
            both = _dot(jnp.concatenate([s.astype(BF16), kw.T], axis=0), vaug)
            caug = caugs[h]
            naug = both[:CHUNK] + jnp.concatenate([w_state, w_state], axis=1) * _dot(qh, caug.astype(BF16))
            num, nq = naug[:, :dh], naug[:, dh:]
            hv = num / jnp.maximum(jnp.abs(nq), jnp.exp(-(b_t_all[us] + mx)))
            decay = decay_c[c][h:h + 1, :]
            caugs[h] = jnp.concatenate([decay, decay], axis=1) * caug + both[CHUNK:]
            hc = hv - jnp.mean(hv, axis=1, keepdims=True)
            hn = hc * lax.rsqrt(jnp.mean(hc * hc, axis=1, keepdims=True) + LN_EPS)
            heads.append(hn)
        h_chunks.append(jnp.concatenate(heads, axis=1))
    for h in range(B_HEADS):
        ct_ref[h] = caugs[h]
    hn_all = jnp.concatenate(h_chunks, axis=0) if nck > 1 else h_chunks[0]
    y_b = (hn_all * norm_w_ref[...] + skip_ref[...] * xc) * gate_z

    mix = jnp.concatenate([y_a, y_b], axis=1).astype(BF16)
    y = _dot(mix, w_out_ref[...])
    o_ref[0] = _ln(DN_ALPHA * x + y, ln_g_ref[...], ln_b_ref[...])


def _block_diag_pairs(w):
    hh, d, _ = w.shape
    wp = w.reshape(hh // 2, 2, d, d)
    eye = jnp.eye(2, dtype=w.dtype)
    return jnp.einsum('pade,ab->padbe', wp, eye).reshape(hh // 2, 2 * d, 2 * d)


def _dot_head_pairs(x, w_ref):
    npair, w2, _ = w_ref.shape
    return jnp.concatenate([_dot(x[:, p * w2:(p + 1) * w2], w_ref[p]) for p in range(npair)], axis=1)


def even_mixer_layer(x, w_in, gm_ln_g, gm_ln_b, gm_ws, gm_bs, conv_w, conv_b, wq, wk, wv, w_if,
                     b_if, norm_w, skip, w_out, ln_g, ln_b, *, ts):
    bsz, seq, d = x.shape
    aw = gm_ln_g.shape[0]
    bw = conv_b.shape[0]
    agd = aw // A_GROUPS
    causal = jnp.tril(jnp.ones((CHUNK, CHUNK), dtype=bool))
    gm_w = jnp.where(causal[None], gm_ws, 0.0).astype(BF16)
    gm_bias = jnp.repeat(gm_bs.T, agd, axis=1)
    row = lambda a: a.reshape(1, -1)
    args = (x, w_in.astype(BF16), row(gm_ln_g), row(gm_ln_b), gm_w, gm_bias,
            conv_w, row(conv_b), _block_diag_pairs(wq).astype(BF16), _block_diag_pairs(wk).astype(BF16),
            _block_diag_pairs(wv).astype(BF16), w_if.T.astype(BF16),
            b_if.reshape(-1, 1), row(norm_w), row(skip), w_out.astype(BF16),
            row(ln_g), row(ln_b))
    in_specs = [pl.BlockSpec((1, ts, d), lambda b, j: (b, j, 0))] + [_full(a.shape) for a in args[1:]]
    dh = bw // B_HEADS
    assert dh == LANES and CHUNK == LANES
    return pl.pallas_call(
        functools.partial(_even_kernel, ts=ts, aw=aw, bw=bw),
        grid=(bsz, seq // ts),
        in_specs=in_specs,
        out_specs=pl.BlockSpec((1, ts, d), lambda b, j: (b, j, 0)),
        out_shape=jax.ShapeDtypeStruct((bsz, seq, d), F32),
        scratch_shapes=[pltpu.VMEM((8 + ts, bw), F32),
                        pltpu.VMEM((B_HEADS, dh, dh + LANES), F32),
                        pltpu.VMEM((B_HEADS, LANES), F32)],
        compiler_params=_cparams(("arbitrary", "arbitrary")),
        name="even_mixer",
    )(*args)


def _memfold_kernel(mem_ref, wkv_ref, wq_ref, wo_ref, wqk_ref, vo_ref, *, d):
    dh = d // X_HEADS
    m_len = mem_ref.shape[1]
    kv = _dot(mem_ref[0].astype(BF16), wkv_ref[...])
    k = (kv[:, :d] * (dh ** -0.5 * LOG2E)).astype(BF16)
    v = kv[:, d:].astype(BF16)
    for h in range(X_HEADS):
        hs = slice(h * dh, (h + 1) * dh)
        ms = slice(h * m_len, (h + 1) * m_len)
        wqk_ref[0, :, ms] = _dot_nt(wq_ref[:, hs], k[:, hs]).astype(BF16)
        vo_ref[0, ms, :] = _dot(v[:, hs], wo_ref[hs, :]).astype(BF16)


def memory_fold(mem, wkv, wq, wo):
    bsz, m_len, d = mem.shape
    hm = X_HEADS * m_len
    args = (mem, wkv.astype(BF16), wq.astype(BF16), wo.astype(BF16))
    return pl.pallas_call(
        functools.partial(_memfold_kernel, d=d),
        grid=(bsz,),
        in_specs=[pl.BlockSpec((1, m_len, d), lambda b: (b, 0, 0))] + [_full(a.shape) for a in args[1:]],
        out_specs=[pl.BlockSpec((1, d, hm), lambda b: (b, 0, 0)),
                   pl.BlockSpec((1, hm, d), lambda b: (b, 0, 0))],
        out_shape=[jax.ShapeDtypeStruct((bsz, d, hm), BF16),
                   jax.ShapeDtypeStruct((bsz, hm, d), BF16)],
        compiler_params=_cparams(("arbitrary",)),
        name="memory_fold",
    )(*args)


ROUTE_W = 128
XATTN_TS = 1024
DEST_ROWS = 8
SLOT_TB = 2048
SLOT_SUB = 512


def _xattn_kernel(x_ref, wqk_ref, vo_ref, ln_g_ref, ln_b_ref, wr_ref, br_ref,
                  o_ref, xp_ref, route_ref, cnt_ref):
    m_len = wqk_ref.shape[2] // X_HEADS

    @pl.when(pl.program_id(1) == 0)
    def _():
        cnt_ref[...] = jnp.zeros_like(cnt_ref)

    x = x_ref[0]
    ts = x.shape[0]
    xb = x.astype(BF16)
    probs = []
    for h in range(X_HEADS):
        s = _dot(xb, wqk_ref[0, :, h * m_len:(h + 1) * m_len])
        p = jnp.exp2(s - jnp.max(s, axis=1, keepdims=True))
        probs.append((p / jnp.sum(p, axis=1, keepdims=True)).astype(BF16))
    y = _dot(jnp.concatenate(probs, axis=1), vo_ref[0])
    x2 = _ln(DN_ALPHA * x + y, ln_g_ref[...], ln_b_ref[...])
    o_ref[0] = x2
    xp_ref[0] = _pack_halves(x2)

    x_hi = x2.astype(BF16)
    x_lo = (x2 - x_hi.astype(F32)).astype(BF16)
    logits = (_dot(x_hi, wr_ref[0]) + _dot(x_lo, wr_ref[0]) + _dot(x_hi, wr_ref[1])) + br_ref[...]
    lane = lax.broadcasted_iota(jnp.int32, (ts, ROUTE_W), 1)
    is_g = lane < N_GROUPS
    lg = jnp.where(is_g, logits, NEG)
    mg = jnp.max(lg, axis=1, keepdims=True)
    gi = jnp.min(jnp.where(jnp.logical_and(is_g, lg == mg), lane, ROUTE_W), axis=1, keepdims=True)
    gate_g = 1.0 / jnp.sum(jnp.where(is_g, jnp.exp(lg - mg), 0.0), axis=1, keepdims=True)
    lo = N_GROUPS + gi * EXPERTS_PER_GROUP
    in_grp = jnp.logical_and(lane >= lo, lane < lo + EXPERTS_PER_GROUP)
    le = jnp.where(in_grp, logits, NEG)
    v1 = jnp.max(le, axis=1, keepdims=True)
    i1 = jnp.min(jnp.where(jnp.logical_and(in_grp, le == v1), lane, ROUTE_W), axis=1, keepdims=True)
    le2 = jnp.where(lane == i1, NEG, le)
    v2 = jnp.max(le2, axis=1, keepdims=True)
    i2 = jnp.min(jnp.where(jnp.logical_and(in_grp, le2 == v2), lane, ROUTE_W), axis=1, keepdims=True)
    e21 = jnp.exp(v2 - v1)
    p1 = 1.0 / (1.0 + e21)
    p2 = e21 * p1
    e1 = (i1 - N_GROUPS).astype(F32)
    e2 = (i2 - N_GROUPS).astype(F32)
    rec = jnp.where(lane == 0, e1, 0.0)
    rec = jnp.where(lane == 1, e2, rec)
    rec = jnp.where(lane == 2, gate_g * p1, rec)
    rec = jnp.where(lane == 3, gate_g * p2, rec)
    route_ref[0] = rec
    sel = jnp.logical_or(lane == i1 - N_GROUPS, lane == i2 - N_GROUPS)
    cnt_ref[0] += jnp.sum(jnp.where(sel, 1.0, 0.0), axis=0, keepdims=True)


def xattn_router_layer(x, wqk, vo, ln_g, ln_b, w_rg, b_rg, w_re, b_re, *, ts):
    bsz, seq, d = x.shape
    hm = wqk.shape[2]
    wr = jnp.zeros((d, ROUTE_W), F32).at[:, :N_GROUPS].set(w_rg).at[:, N_GROUPS:N_GROUPS + N_EXPERTS].set(w_re)
    br = jnp.zeros((1, ROUTE_W), F32).at[0, :N_GROUPS].set(b_rg).at[0, N_GROUPS:N_GROUPS + N_EXPERTS].set(b_re)
    wr_hi = wr.astype(BF16)
    wr_lo = (wr - wr_hi.astype(F32)).astype(BF16)
    wr = jnp.stack([wr_hi, wr_lo])
    args = (x, wqk, vo, ln_g.reshape(1, -1), ln_b.reshape(1, -1), wr, br)
    in_specs = [pl.BlockSpec((1, ts, d), lambda b, j: (b, j, 0)),
                pl.BlockSpec((1, d, hm), lambda b, j: (b, 0, 0)),
                pl.BlockSpec((1, hm, d), lambda b, j: (b, 0, 0))] + [_full(a.shape) for a in args[3:]]
    return pl.pallas_call(
        _xattn_kernel,
        grid=(bsz, seq // ts),
        in_specs=in_specs,
        out_specs=[pl.BlockSpec((1, ts, d), lambda b, j: (b, j, 0)),
                   pl.BlockSpec((1, ts, d // 2), lambda b, j: (b, j, 0)),
                   pl.BlockSpec((1, ts, ROUTE_W), lambda b, j: (b, j, 0)),
                   pl.BlockSpec((1, 1, ROUTE_W), lambda b, j: (b, 0, 0))],
        out_shape=[jax.ShapeDtypeStruct((bsz, seq, d), F32),
                   jax.ShapeDtypeStruct((bsz, seq, d // 2), jnp.int32),
                   jax.ShapeDtypeStruct((bsz, seq, ROUTE_W), F32),
                   jax.ShapeDtypeStruct((bsz, 1, ROUTE_W), F32)],
        compiler_params=_cparams(("arbitrary", "arbitrary")),
        name="xattn_router",
    )(*args)


def _moe_output(x2, g0, g1, rec, ln_g, ln_b):
    a_lo, a_hi = _unpack_halves(g0)
    b_lo, b_hi = _unpack_halves(g1)
    w0, w1 = rec[:, 2:3], rec[:, 3:4]
    y = jnp.concatenate([w0 * a_lo + w1 * b_lo, w0 * a_hi + w1 * b_hi], axis=1)
    return _ln(DN_ALPHA * x2 + y, ln_g, ln_b)


def _swa_kernel(x_ref, g0_ref, g1_ref, route_ref, pln_g_ref, pln_b_ref,
                wqkv_ref, bqkv_ref, cos_ref, sin_ref, sink_ref, wo_ref, ln_g_ref, ln_b_ref,
                *rest, ts, cq, ckv):
    o_ref, kprev, vprev = rest[-3:]
    j = pl.program_id(1)
    nb = ts // CHUNK
    dh = C_HEAD_DIM
    grp = (cq // dh) // C_KV_HEADS

    @pl.when(j == 0)
    def _():
        kprev[...] = jnp.zeros_like(kprev)
        vprev[...] = jnp.zeros_like(vprev)

    x = _moe_output(x_ref[0], g0_ref[0, 0], g1_ref[0, 0], route_ref[0], pln_g_ref[...], pln_b_ref[...])
    qkv = _dot(x.astype(BF16), wqkv_ref[...]) + bqkv_ref[...]
    cos = cos_ref[...]
    sin = sin_ref[...]
    lane = lax.broadcasted_iota(jnp.int32, (ts, LANES), 1)
    first_half = (lane % dh) < (dh // 2)

    def rope(t):
        outs = []
        for c in range(t.shape[1] // LANES):
            tc = t[:, c * LANES:(c + 1) * LANES]
            rot = jnp.where(first_half, pltpu.roll(tc, LANES - dh // 2, 1), pltpu.roll(tc, dh // 2, 1))
            outs.append(tc * cos + rot * sin)
        return jnp.concatenate(outs, axis=1)

    q = rope(qkv[:, :cq]).astype(BF16)
    k = rope(qkv[:, cq:cq + 2 * ckv]).astype(BF16)
    v = qkv[:, cq + 2 * ckv:].astype(BF16)

    r_i = lax.broadcasted_iota(jnp.int32, (CHUNK, 2 * CHUNK), 0)
    c_i = lax.broadcasted_iota(jnp.int32, (CHUNK, 2 * CHUNK), 1)
    band = jnp.logical_and(c_i > r_i, c_i <= r_i + CHUNK)
    sink_col = c_i == 0
    lane_k = lax.broadcasted_iota(jnp.int32, (2 * CHUNK, LANES), 1)
    key_row = lax.broadcasted_iota(jnp.int32, (2 * CHUNK, LANES), 0)
    lane_q = lax.broadcasted_iota(jnp.int32, (CHUNK, LANES), 1)
    ones_blk = jnp.ones((2 * CHUNK, LANES), BF16)
    zero_b = jnp.zeros((), BF16)
    blocks = []
    for c in range(nb):
        sl = slice(c * CHUNK, (c + 1) * CHUNK)
        if c == 0:
            kb = jnp.concatenate([kprev[...].astype(BF16), k[sl]], axis=0)
            vb = jnp.concatenate([vprev[...].astype(BF16), v[sl]], axis=0)
            first_key = jnp.where(j > 0, 0, CHUNK)
            valid = jnp.logical_and(band, c_i >= first_key)
        else:
            kb = k[(c - 1) * CHUNK:(c + 1) * CHUNK]
            vb = v[(c - 1) * CHUNK:(c + 1) * CHUNK]
            valid = band
        tiles = []
        for h in range(C_KV_HEADS):
            kd = kb[:, h * LANES:(h + 1) * LANES]
            vd = vb[:, h * LANES:(h + 1) * LANES]
            k_lo = jnp.where(lane_k < dh, kd, zero_b)
            k_hi = jnp.where(lane_k >= dh, kd, zero_b)
            vz = jnp.where(key_row == 0, zero_b, vd)
            q2 = jnp.concatenate([q[sl, (2 * h) * LANES:(2 * h + 1) * LANES],
                                  q[sl, (2 * h + 1) * LANES:(2 * h + 2) * LANES]], axis=0)
            s_lo = _dot_nt(q2, k_lo)
            s_hi = _dot_nt(q2, k_hi)
            parts = []
            for qk, g in ((s_lo[:CHUNK], 0), (s_lo[CHUNK:], 2), (s_hi[:CHUNK], 1), (s_hi[CHUNK:], 3)):
                sink = sink_ref[h * grp + g]
                parts.append(jnp.where(valid, qk, jnp.where(sink_col, sink, NEG)))
            s = jnp.concatenate(parts, axis=0)
            p = jnp.exp2(s - jnp.max(s, axis=1, keepdims=True)).astype(BF16)
            den = _dot(p, ones_blk)
            o2 = _dot(p, vz) / den
            tiles.append(jnp.where(lane_q < dh, o2[:CHUNK], o2[2 * CHUNK:3 * CHUNK]))
            tiles.append(jnp.where(lane_q < dh, o2[CHUNK:2 * CHUNK], o2[3 * CHUNK:]))
        blocks.append(jnp.concatenate(tiles, axis=1))
    kprev[...] = k[(nb - 1) * CHUNK:].astype(F32)
    vprev[...] = v[(nb - 1) * CHUNK:].astype(F32)
    att = (jnp.concatenate(blocks, axis=0) if nb > 1 else blocks[0]).astype(BF16)
    y = _dot(att, wo_ref[...])
    o_ref[0] = _ln(DN_ALPHA * x + y, ln_g_ref[...], ln_b_ref[...])


def swa_prepare(w_qkv, b_qkv, sinks, w_o, seq):
    cq = w_o.shape[0]
    ckv = (w_qkv.shape[1] - cq) // 2
    dh = C_HEAD_DIM
    inv = ROPE_THETA ** (-jnp.arange(0, dh, 2, dtype=F32) / dh)
    ang = jnp.arange(seq, dtype=F32)[:, None] * inv[None, :]
    reps = LANES // (dh // 2)
    sign = jnp.tile(jnp.concatenate([-jnp.ones((dh // 2,), F32), jnp.ones((dh // 2,), F32)]), LANES // dh)
    cos_t = jnp.tile(jnp.cos(ang), (1, reps))
    sin_t = jnp.tile(jnp.sin(ang), (1, reps)) * sign[None, :]
    assert 2 * dh == LANES

    def dup_heads(t):
        th = t.reshape(t.shape[:-1] + (ckv // dh, dh))
        return jnp.concatenate([th, th], axis=-1).reshape(t.shape[:-1] + (2 * ckv,))

    qs = dh ** -0.5 * LOG2E
    w_all = jnp.concatenate([w_qkv[:, :cq] * qs, dup_heads(w_qkv[:, cq:cq + ckv]),
                             dup_heads(w_qkv[:, cq + ckv:])], axis=1)
    b_all = jnp.concatenate([b_qkv[:cq] * qs, dup_heads(b_qkv[cq:cq + ckv]), dup_heads(b_qkv[cq + ckv:])])
    return w_all.astype(BF16), b_all.reshape(1, -1), cos_t, sin_t, sinks.astype(F32) * LOG2E, w_o.astype(BF16)


def swa_mixer_layer(pending, prepared, ln_g, ln_b, *, ts):
    x, gs, route, pln_g, pln_b = pending
    bsz, seq, d = x.shape
    cq = prepared[5].shape[0]
    ckv = (prepared[0].shape[1] - cq) // 4
    shared = (pln_g.reshape(1, -1), pln_b.reshape(1, -1), *prepared, ln_g.reshape(1, -1), ln_b.reshape(1, -1))
    bp = bsz // len(gs)
    out = None
    for p, g in enumerate(gs):
        b0 = p * bp
        g = g.reshape(2, bp, seq, d // 2)
        args = [x, g, g, route, *shared]
        in_specs = [pl.BlockSpec((1, ts, d), lambda b, j: (b0 + b, j, 0)),
                    pl.BlockSpec((1, 1, ts, d // 2), lambda b, j: (0, b, j, 0)),
                    pl.BlockSpec((1, 1, ts, d // 2), lambda b, j: (1, b, j, 0)),
                    pl.BlockSpec((1, ts, ROUTE_W), lambda b, j: (b0 + b, j, 0)),
                    _full((1, d)), _full((1, d)),
                    _full(shared[2].shape), _full(shared[3].shape),
                    pl.BlockSpec((ts, LANES), lambda b, j: (j, 0)),
                    pl.BlockSpec((ts, LANES), lambda b, j: (j, 0)),
                    pl.BlockSpec(memory_space=pltpu.SMEM),
                    _full(shared[7].shape), _full(shared[8].shape), _full(shared[9].shape)]
        aliases = {}
        if out is not None:
            args.append(out)
            in_specs.append(pl.BlockSpec(memory_space=pl.ANY))
            aliases = {len(args) - 1: 0}
        out = pl.pallas_call(
            functools.partial(_swa_kernel, ts=ts, cq=cq, ckv=ckv),
            grid=(bp, seq // ts),
            in_specs=in_specs,
            out_specs=pl.BlockSpec((1, ts, d), lambda b, j: (b0 + b, j, 0)),
            out_shape=jax.ShapeDtypeStruct((bsz, seq, d), F32),
            input_output_aliases=aliases,
            scratch_shapes=[pltpu.VMEM((CHUNK, 2 * ckv), F32), pltpu.VMEM((CHUNK, 2 * ckv), F32)],
            compiler_params=_cparams(("arbitrary", "arbitrary")),
            name="swa_mixer",
        )(*args)
    return out


def _slot_kernel(route_ref, pstart_ref, dest_ref, carry_ref, *, tb):
    @pl.when(pl.program_id(0) == 0)
    def _():
        carry_ref[...] = pstart_ref[...]

    sub = min(SLOT_SUB, tb)
    lane = lax.broadcasted_iota(jnp.int32, (sub, ROUTE_W), 1)
    r = lax.broadcasted_iota(jnp.int32, (sub, sub), 0)
    c = lax.broadcasted_iota(jnp.int32, (sub, sub), 1)
    before = jnp.where(c < r, 1.0, 0.0).astype(BF16)
    carry = carry_ref[...]
    for k in range(tb // sub):
        rows = slice(k * sub, (k + 1) * sub)
        rec = route_ref[rows, :]
        e0 = rec[:, 0:1].astype(jnp.int32)
        e1 = rec[:, 1:2].astype(jnp.int32)
        oh0 = lane == e0
        oh1 = lane == e1
        ohs = jnp.where(jnp.logical_or(oh0, oh1), 1.0, 0.0)
        prefix = _dot(before, ohs.astype(BF16)) + carry
        d0 = jnp.sum(jnp.where(oh0, prefix, 0.0), axis=1, keepdims=True)
        d1 = jnp.sum(jnp.where(oh1, prefix, 0.0), axis=1, keepdims=True)
        dest = jnp.where(lane == 0, d0, jnp.where(lane == 1, d1, 0.0))
        dest_ref[:, rows] = dest.T[:DEST_ROWS].astype(jnp.int32)
        carry = carry + jnp.sum(ohs, axis=0, keepdims=True)
    carry_ref[...] = carry


def moe_slots(route, pstart, *, tb):
    n = route.shape[0]
    return pl.pallas_call(
        functools.partial(_slot_kernel, tb=tb),
        grid=(n // tb,),
        in_specs=[pl.BlockSpec((tb, ROUTE_W), lambda i: (i, 0)), _full((1, ROUTE_W))],
        out_specs=pl.BlockSpec((DEST_ROWS, tb), lambda i: (0, i)),
        out_shape=jax.ShapeDtypeStruct((DEST_ROWS, n), jnp.int32),
        scratch_shapes=[pltpu.VMEM((1, ROUTE_W), F32)],
        compiler_params=_cparams(("arbitrary",)),
        name="moe_slots",
    )(route, pstart)


def _ffn_kernel(blk_exp_ref, new_exp_ref, nblk_ref, xs_ref, w1_ref, w3_ref, w2_ref, ys_ref,
                w1_b, w3_b, w2_b):
    i = pl.program_id(0)
    used = i < nblk_ref[0]

    @pl.when(jnp.logical_and(used, new_exp_ref[i] == 1))
    def _():
        w1_b[...] = w1_ref[0, 0].astype(BF16)
        w3_b[...] = w3_ref[0, 0].astype(BF16)
        w2_b[...] = w2_ref[0, 0].astype(BF16)

    @pl.when(used)
    def _():
        x_lo, x_hi = _unpack_halves(xs_ref[...])
        x_lo, x_hi = x_lo.astype(BF16), x_hi.astype(BF16)
        dl = x_lo.shape[1]
        h1 = _dot(x_lo, w1_b[:dl, :]) + _dot(x_hi, w1_b[dl:, :])
        h3 = _dot(x_lo, w3_b[:dl, :]) + _dot(x_hi, w3_b[dl:, :])
        h = (_silu(h1) * h3).astype(BF16)
        ys_ref[...] = _pack_halves(_dot(h, w2_b[...]))

    @pl.when(jnp.logical_not(used))
    def _():
        ys_ref[...] = jnp.zeros_like(ys_ref)


def moe_ffn(xs, blk_exp, new_exp, nblk, w1, w3, w2, *, layer, bm):
    n_pad, dp = xs.shape
    d = 2 * dp
    de = w1.shape[3]
    n_blk = n_pad // bm

    def x_map(i, be, ne, nb):
        return (jnp.minimum(i, nb[0] - 1), 0)

    def w_map(i, be, ne, nb):
        return (layer, be[i], 0, 0)

    return pl.pallas_call(
        _ffn_kernel,
        grid_spec=pltpu.PrefetchScalarGridSpec(
            num_scalar_prefetch=3,
            grid=(n_blk,),
            in_specs=[pl.BlockSpec((bm, dp), x_map),
                      pl.BlockSpec((1, 1, d, de), w_map),
                      pl.BlockSpec((1, 1, d, de), w_map),
                      pl.BlockSpec((1, 1, de, d), w_map)],
            out_specs=pl.BlockSpec((bm, dp), lambda i, be, ne, nb: (i, 0)),
            scratch_shapes=[pltpu.VMEM((d, de), BF16), pltpu.VMEM((d, de), BF16),
                            pltpu.VMEM((de, d), BF16)]),
        out_shape=jax.ShapeDtypeStruct((n_pad, dp), jnp.int32),
        compiler_params=_cparams(("arbitrary",)),
        name="moe_ffn",
    )(blk_exp, new_exp, nblk, xs, w1, w3, w2)


def _combine_kernel(x_ref, g0_ref, g1_ref, route_ref, ln_g_ref, ln_b_ref, *rest):
    o_ref = rest[-1]
    o_ref[...] = _moe_output(x_ref[...], g0_ref[0], g1_ref[0], route_ref[...], ln_g_ref[...], ln_b_ref[...])


def moe_combine(x, g, route, ln_g, ln_b, *, row0, tb, prev=None):
    n, d = x.shape
    m = g.shape[1]
    blk0 = row0 // tb
    row_spec = pl.BlockSpec((tb, d), lambda i: (blk0 + i, 0))
    args = [x, g, g, route, ln_g.reshape(1, -1), ln_b.reshape(1, -1)]
    in_specs = [row_spec,
                pl.BlockSpec((1, tb, d // 2), lambda i: (0, i, 0)),
                pl.BlockSpec((1, tb, d // 2), lambda i: (1, i, 0)),
                pl.BlockSpec((tb, ROUTE_W), lambda i: (blk0 + i, 0)),
                _full((1, d)), _full((1, d))]
    aliases = {}
    if prev is not None:
        args.append(prev)
        in_specs.append(pl.BlockSpec(memory_space=pl.ANY))
        aliases = {len(args) - 1: 0}
    return pl.pallas_call(
        _combine_kernel,
        grid=(m // tb,),
        in_specs=in_specs,
        out_specs=row_spec,
        out_shape=jax.ShapeDtypeStruct((n, d), F32),
        input_output_aliases=aliases,
        compiler_params=_cparams(("arbitrary",)),
        name="moe_combine",
    )(*args)


MOE_BM = 1024
COMBINE_PARTS = 4
COMBINE_TB = 1024
MIXER_PARTS = 2
MIXER_TS = 1024


def hierarchical_moe_layer(x2, xp, route, counts, w1, w3, w2, ln_g, ln_b, *, layer, defer_combine,
                           side_work=None, side_inputs=None):
    bsz, seq, d = x2.shape
    n = bsz * seq
    bm = MOE_BM
    rt = route.reshape(n, ROUTE_W)
    tb = min(512, n)
    n_blk = (2 * n) // bm + N_EXPERTS
    cnt = counts.sum(axis=0)[0, :N_EXPERTS].astype(jnp.int32)
    pcnt = (cnt + bm - 1) // bm * bm
    pends = jnp.cumsum(pcnt)
    pstart = pends - pcnt
    nblk = (pends[-1] // bm).astype(jnp.int32).reshape(1)
    blk_row = jnp.arange(n_blk, dtype=jnp.int32) * bm
    blk_exp = jnp.minimum(jnp.sum((pends[None, :] <= blk_row[:, None]).astype(jnp.int32), axis=1),
                          N_EXPERTS - 1)
    last_exp = blk_exp[jnp.maximum(nblk[0] - 1, 0)]
    blk_exp = jnp.where(jnp.arange(n_blk) < nblk[0], blk_exp, last_exp)
    new_exp = jnp.concatenate([jnp.ones((1,), jnp.int32),
                               (blk_exp[1:] != blk_exp[:-1]).astype(jnp.int32)])
    pstart_rec = jnp.zeros((1, ROUTE_W), F32).at[0, :N_EXPERTS].set(pstart.astype(F32))
    dest = moe_slots(rt, pstart_rec, tb=min(SLOT_TB, n))
    xs = moe_dispatch(xp.reshape(n, d // 2), dest[0], dest[1], n_blk * bm)
    side = None
    if side_work is not None:
        side_inputs, _ = lax.optimization_barrier((side_inputs, dest))
        nblk, side = lax.optimization_barrier((nblk, side_work(*side_inputs)))
    ys = moe_ffn(xs, blk_exp, new_exp, nblk, w1, w3, w2, layer=layer, bm=bm)
    if defer_combine:
        bparts = MIXER_PARTS if bsz % MIXER_PARTS == 0 else 1
        mb = n // bparts
        gs = [moe_gather(ys, dest[0, p * mb:(p + 1) * mb], dest[1, p * mb:(p + 1) * mb]) for p in range(bparts)]
        return (x2, gs, route, ln_g, ln_b), side
    parts = COMBINE_PARTS if n % (COMBINE_PARTS * tb * 8) == 0 else 1
    m = n // parts
    out = None
    for p in range(parts):
        g = moe_gather(ys, dest[0, p * m:(p + 1) * m], dest[1, p * m:(p + 1) * m])
        out = moe_combine(x2.reshape(n, d), g, rt, ln_g, ln_b, row0=p * m, tb=min(COMBINE_TB, m), prev=out)
    return out.reshape(bsz, seq, d)


SC_ROWS = 128


def _sc_mesh():
    return plsc.VectorSubcoreMesh(core_axis_name="c", subcore_axis_name="s")


def moe_dispatch(xf, dest0, dest1, n_pad):
    n, d = xf.shape
    info = plsc.get_sparse_core_info()
    nw = info.num_cores * info.num_subcores
    per_w = n // nw
    r = min(SC_ROWS, per_w)

    def body(x_hbm, d0_hbm, d1_hbm, xs_hbm, i0_v, i1_v, rows_v, sem):
        wid = lax.axis_index("s") * info.num_cores + lax.axis_index("c")

        @pl.loop(0, per_w // r)
        def _(c):
            base = pl.multiple_of(wid * per_w + c * r, 8)
            pltpu.sync_copy(d0_hbm.at[pl.ds(base, r)], i0_v)
            pltpu.sync_copy(d1_hbm.at[pl.ds(base, r)], i1_v)
            pltpu.sync_copy(x_hbm.at[pl.ds(base, r)], rows_v)
            pltpu.async_copy(rows_v, xs_hbm.at[i0_v], sem).wait()
            pltpu.async_copy(rows_v, xs_hbm.at[i1_v], sem).wait()

    return pl.kernel(
        body, out_type=jax.ShapeDtypeStruct((n_pad, d), xf.dtype), mesh=_sc_mesh(),
        scratch_types=[pltpu.VMEM((r,), jnp.int32), pltpu.VMEM((r,), jnp.int32),
                       pltpu.VMEM((r, d), xf.dtype), pltpu.SemaphoreType.DMA],
        name="moe_dispatch",
    )(xf, dest0, dest1)


def moe_gather(ys, dest0, dest1):
    n = dest0.shape[0]
    d = ys.shape[1]
    info = plsc.get_sparse_core_info()
    nw = info.num_cores * info.num_subcores
    per_w = n // nw
    r = min(SC_ROWS, per_w)

    def body(ys_hbm, d0_hbm, d1_hbm, g_hbm, i_v, rows_v, sem):
        wid = lax.axis_index("s") * info.num_cores + lax.axis_index("c")

        @pl.loop(0, per_w // r)
        def _(c):
            base = pl.multiple_of(wid * per_w + c * r, 8)
            for k, d_hbm in enumerate((d0_hbm, d1_hbm)):
                pltpu.sync_copy(d_hbm.at[pl.ds(base, r)], i_v)
                pltpu.async_copy(ys_hbm.at[i_v], rows_v, sem).wait()
                pltpu.sync_copy(rows_v, g_hbm.at[k, pl.ds(base, r)])

    return pl.kernel(
        body, out_type=jax.ShapeDtypeStruct((2, n, d), ys.dtype), mesh=_sc_mesh(),
        scratch_types=[pltpu.VMEM((r,), jnp.int32), pltpu.VMEM((r, d), ys.dtype),
                       pltpu.SemaphoreType.DMA],
        name="moe_gather",
    )(ys, dest0, dest1)


def kernel(x, mem, ln_g, ln_b, ev_w_in, ev_gm_ln_g, ev_gm_ln_b, ev_gm_ws, ev_gm_bs, ev_conv_w, ev_conv_b, ev_wq, ev_wk, ev_wv, ev_w_if, ev_b_if, ev_norm_w, ev_skip, ev_w_out, od_w_qkv, od_b_qkv, od_sinks, od_w_o, xa_wq, xa_wkv, xa_wo, moe_w_rg, moe_b_rg, moe_w_re, moe_b_re, moe_w1, moe_w3, moe_w2):
    bsz, seq, d = x.shape
    depth = ln_g.shape[0]
    assert depth == DEPTH
    ts = min(MIXER_TS, seq)

    def weight_side(l):
        args = (mem, xa_wkv[l], xa_wq[l], xa_wo[l])
        if l % 2 == 0:
            return args, lambda m, wkv, wq, wo: (memory_fold(m, wkv, wq, wo), None)
        o = l // 2
        args += (od_w_qkv[o], od_b_qkv[o], od_sinks[o], od_w_o[o])
        return args, lambda m, wkv, wq, wo, *sw: (memory_fold(m, wkv, wq, wo), swa_prepare(*sw, seq))

    side = None
    for l in range(depth):
        if side is None:
            args, fn = weight_side(l)
            side = fn(*args)
        (wqk, vo), prepared = side
        if l % 2 == 0:
            e = l // 2
            x = even_mixer_layer(x, ev_w_in[e], ev_gm_ln_g[e], ev_gm_ln_b[e], ev_gm_ws[e], ev_gm_bs[e],
                                 ev_conv_w[e], ev_conv_b[e], ev_wq[e], ev_wk[e], ev_wv[e], ev_w_if[e],
                                 ev_b_if[e], ev_norm_w[e], ev_skip[e], ev_w_out[e],
                                 ln_g[l, 0], ln_b[l, 0], ts=ts)
        else:
            x = swa_mixer_layer(x, prepared, ln_g[l, 0], ln_b[l, 0], ts=ts)
        x, xp, route, counts = xattn_router_layer(x, wqk, vo, ln_g[l, 1], ln_b[l, 1],
                                                  moe_w_rg[l], moe_b_rg[l], moe_w_re[l], moe_b_re[l],
                                                  ts=min(XATTN_TS, seq))
        defer = l + 1 < depth and (l + 1) % 2 == 1
        side_inputs, side_work = weight_side(l + 1) if defer else (None, None)
        x = hierarchical_moe_layer(x, xp, route, counts, moe_w1, moe_w3, moe_w2,
                                   ln_g[l, 2], ln_b[l, 2], layer=l, defer_combine=defer,
                                   side_work=side_work, side_inputs=side_inputs)
        side = None
        if defer:
            x, side = x
    return x
```

```python
import functools
import math

import jax
import jax.numpy as jnp
from jax import lax
from jax.experimental import pallas as pl
from jax.experimental.pallas import tpu as pltpu
from jax.experimental.pallas import tpu_sc as plsc

F32 = jnp.float32
BF16 = jnp.bfloat16

A_GROUPS = 4
CHUNK = 128
B_HEADS = 4
B_CONV = 4
C_HEAD_DIM = 64
C_KV_HEADS = 4
X_HEADS = 4
N_GROUPS = 4
EXPERTS_PER_GROUP = 8
N_EXPERTS = N_GROUPS * EXPERTS_PER_GROUP
ROPE_THETA = 10000.0
LN_EPS = 1e-5
DEPTH = 2
DN_ALPHA = (2 * DEPTH) ** 0.25

LANES = 128
VMEM_LIMIT = 48 * 1024 * 1024
NEG = -1e30


def _cparams(sem):
    return pltpu.CompilerParams(dimension_semantics=sem, vmem_limit_bytes=VMEM_LIMIT)


def _full(shape):
    nd = len(shape)
    return pl.BlockSpec(shape, lambda *_: (0,) * nd)


def _dot(a, b):
    return jnp.dot(a, b, preferred_element_type=F32)


def _dot_nt(a, b):
    return lax.dot_general(a, b, (((1,), (1,)), ((), ())), preferred_element_type=F32)


def _split_dot(a, b_bf16):
    hi = a.astype(BF16)
    lo = (a - hi.astype(F32)).astype(BF16)
    return _dot(hi, b_bf16) + _dot(lo, b_bf16)


def _ln(x, g, b):
    mu = jnp.mean(x, axis=-1, keepdims=True)
    xc = x - mu
    var = jnp.mean(xc * xc, axis=-1, keepdims=True)
    return xc * lax.rsqrt(var + LN_EPS) * g + b


LOG2E = math.log2(math.e)


def _silu(x):
    return x * (1.0 / (1.0 + jnp.exp2(x * -LOG2E)))


def _gelu(x):
    return 0.5 * x * (1.0 + jnp.tanh(math.sqrt(2.0 / math.pi) * (x + 0.044715 * (x * x * x))))


def _pack_halves(x):
    c = x.shape[1] // 2
    lo = lax.bitcast_convert_type(x[:, :c].astype(BF16).astype(F32), jnp.uint32)
    hi = lax.bitcast_convert_type(x[:, c:].astype(BF16).astype(F32), jnp.uint32)
    return lax.bitcast_convert_type((lo >> 16) | hi, jnp.int32)


def _unpack_halves(p):
    u = lax.bitcast_convert_type(p, jnp.uint32)
    lo = lax.bitcast_convert_type(u << 16, F32)
    hi = lax.bitcast_convert_type(u & jnp.uint32(0xFFFF0000), F32)
    return lo, hi


def _log_sigmoid(x):
    return jnp.minimum(x, 0.0) - jnp.log(1.0 + jnp.exp(-jnp.abs(x)))


def _even_kernel(x_ref, w_in_ref, gm_g_ref, gm_b_ref, gm_w_ref, gm_bias_ref,
                 conv_w_ref, conv_b_ref, wq_ref, wk_ref, wv_ref, wif_t_ref, bif_t_ref,
                 norm_w_ref, skip_ref, w_out_ref, ln_g_ref, ln_b_ref,
                 o_ref,
                 xm_buf, ct_ref, m_ref, *, ts, aw, bw):
    dh = bw // B_HEADS
    agd = aw // A_GROUPS
    nck = ts // CHUNK
    pad = 8
    j = pl.program_id(1)

    @pl.when(j == 0)
    def _():
        xm_buf[0:pad, :] = jnp.zeros((pad, bw), F32)
        ct_ref[...] = jnp.zeros_like(ct_ref)
        m_ref[...] = jnp.zeros_like(m_ref)

    row = lax.broadcasted_iota(jnp.int32, (CHUNK, CHUNK), 0)
    col = lax.broadcasted_iota(jnp.int32, (CHUNK, CHUNK), 1)
    causal = col <= row
    diag = col == row
    triu = jnp.where(row <= col, 1.0, 0.0).astype(BF16)
    ones_blk = jnp.ones((CHUNK, LANES), BF16)

    x = x_ref[0]
    proj = _dot(x.astype(BF16), w_in_ref[...])
    a_u = _gelu(proj[:, :aw])
    a_v = _gelu(proj[:, aw:2 * aw])
    xm = proj[:, 2 * aw:2 * aw + bw]
    z = proj[:, 2 * aw + bw:]

    vn = _ln(a_v, gm_g_ref[...], gm_b_ref[...]).astype(BF16)
    ya_chunks = []
    for c in range(nck):
        cols = []
        for g in range(A_GROUPS):
            v_cg = vn[c * CHUNK:(c + 1) * CHUNK, g * agd:(g + 1) * agd]
            cols.append(_dot(gm_w_ref[g], v_cg))
        ya_chunks.append(jnp.concatenate(cols, axis=1) + gm_bias_ref[...])
    y_a = a_u * jnp.concatenate(ya_chunks, axis=0)

    xm_buf[pad:pad + ts, :] = xm
    conv = conv_b_ref[...] + conv_w_ref[B_CONV - 1:B_CONV, :] * xm
    for k in range(B_CONV - 1):
        sh = B_CONV - 1 - k
        conv = conv + conv_w_ref[k:k + 1, :] * xm_buf[pad - sh:pad - sh + ts, :]
    xm_buf[pad - (B_CONV - 1):pad, :] = xm_buf[pad + ts - (B_CONV - 1):pad + ts, :]
    xc = _silu(conv)
    xc_b = xc.astype(BF16)
    q = _dot_head_pairs(xc_b, wq_ref)
    k_ = _dot_head_pairs(xc_b, wk_ref)
    v = _dot_head_pairs(xm.astype(BF16), wv_ref)
    gate_in = jnp.concatenate([q, k_, v], axis=1).astype(BF16)
    gates_t = _dot_nt(wif_t_ref[...], gate_in) + bif_t_ref[...]
    ig_all = gates_t[:B_HEADS, :]
    lf_all = _log_sigmoid(gates_t[B_HEADS:, :])
    q_b = q.astype(BF16)
    k_b = (k_ * dh ** -0.5).astype(BF16)
    v_b = v.astype(BF16)
    gate_z = _silu(z)

    lf_c = [lf_all[:, c * CHUNK:(c + 1) * CHUNK] for c in range(nck)]
    b_all = _split_dot(jnp.concatenate(lf_c, axis=0), triu)
    m_prev = m_ref[...]
    a_c, m_c, decay_c, wg_c = [], [], [], []
    for c in range(nck):
        b_r = b_all[c * B_HEADS:(c + 1) * B_HEADS]
        a_r = ig_all[:, c * CHUNK:(c + 1) * CHUNK] - b_r
        bl = b_r[:, CHUNK - 1:CHUNK]
        g_r = bl + a_r
        m_new = jnp.maximum(bl + m_prev, jnp.max(g_r, axis=1, keepdims=True))
        a_c.append(a_r)
        m_c.append(m_prev)
        decay_c.append(jnp.exp(bl + m_prev - m_new))
        wg_c.append(jnp.exp(g_r - m_new))
        m_prev = m_new
    m_ref[...] = m_prev

    units = [(c, h) for c in range(nck) for h in range(B_HEADS)]
    lmat = jnp.concatenate([jnp.where(causal, lf_c[c][h:h + 1, :], 0.0) for c, h in units], axis=0)
    dgm = jnp.concatenate([jnp.where(diag, wg_c[c][h:h + 1, :], 0.0) for c, h in units], axis=0)
    b_t_all = _split_dot(lmat, ones_blk)
    wg_t_all = _dot(dgm.astype(BF16), ones_blk)

    caugs = [ct_ref[h] for h in range(B_HEADS)]
    h_chunks = []
    for c in range(nck):
        sl = slice(c * CHUNK, (c + 1) * CHUNK)
        heads = []
        for h in range(B_HEADS):
            u = c * B_HEADS + h
            us = slice(u * CHUNK, (u + 1) * CHUNK)
            hs = slice(h * dh, (h + 1) * dh)
            qh, kh = q_b[sl, hs], k_b[sl, hs]
            vaug = jnp.concatenate([v_b[sl, hs], ones_blk], axis=1)
            m_row = m_c[c][h:h + 1, :]
            amat = jnp.where(causal, a_c[c][h:h + 1, :], NEG)
            mx = jnp.maximum(jnp.max(amat, axis=1, keepdims=True), m_row)
            w_intra = jnp.exp(amat - mx)
            w_state = jnp.exp(m_row - mx)
            s = _dot_nt(qh, kh) * w_intra
            kw = (kh.astype(F32) * wg_t_all[us]).astype(BF16)
            both = _dot(jnp.concatenate([s.astype(BF16), kw.T], axis=0), vaug)
            caug = caugs[h]
            naug = both[:CHUNK] + jnp.concatenate([w_state, w_state], axis=1) * _dot(qh, caug.astype(BF16))
            num, nq = naug[:, :dh], naug[:, dh:]
            hv = num / jnp.maximum(jnp.abs(nq), jnp.exp(-(b_t_all[us] + mx)))
            decay = decay_c[c][h:h + 1, :]
            caugs[h] = jnp.concatenate([decay, decay], axis=1) * caug + both[CHUNK:]
            hc = hv - jnp.mean(hv, axis=1, keepdims=True)
            hn = hc * lax.rsqrt(jnp.mean(hc * hc, axis=1, keepdims=True) + LN_EPS)
            heads.append(hn)
        h_chunks.append(jnp.concatenate(heads, axis=1))
    for h in range(B_HEADS):
        ct_ref[h] = caugs[h]
    hn_all = jnp.concatenate(h_chunks, axis=0) if nck > 1 else h_chunks[0]
    y_b = (hn_all * norm_w_ref[...] + skip_ref[...] * xc) * gate_z

    mix = jnp.concatenate([y_a, y_b], axis=1).astype(BF16)
    y = _dot(mix, w_out_ref[...])
    o_ref[0] = _ln(DN_ALPHA * x + y, ln_g_ref[...], ln_b_ref[...])


def _block_diag_pairs(w):
    hh, d, _ = w.shape
    wp = w.reshape(hh // 2, 2, d, d)
    eye = jnp.eye(2, dtype=w.dtype)
    return jnp.einsum('pade,ab->padbe', wp, eye).reshape(hh // 2, 2 * d, 2 * d)


def _dot_head_pairs(x, w_ref):
    npair, w2, _ = w_ref.shape
    return jnp.concatenate([_dot(x[:, p * w2:(p + 1) * w2], w_ref[p]) for p in range(npair)], axis=1)


def even_mixer_layer(x, w_in, gm_ln_g, gm_ln_b, gm_ws, gm_bs, conv_w, conv_b, wq, wk, wv, w_if,
                     b_if, norm_w, skip, w_out, ln_g, ln_b, *, ts):
    bsz, seq, d = x.shape
    aw = gm_ln_g.shape[0]
    bw = conv_b.shape[0]
    agd = aw // A_GROUPS
    causal = jnp.tril(jnp.ones((CHUNK, CHUNK), dtype=bool))
    gm_w = jnp.where(causal[None], gm_ws, 0.0).astype(BF16)
    gm_bias = jnp.repeat(gm_bs.T, agd, axis=1)
    row = lambda a: a.reshape(1, -1)
    args = (x, w_in.astype(BF16), row(gm_ln_g), row(gm_ln_b), gm_w, gm_bias,
            conv_w, row(conv_b), _block_diag_pairs(wq).astype(BF16), _block_diag_pairs(wk).astype(BF16),
            _block_diag_pairs(wv).astype(BF16), w_if.T.astype(BF16),
            b_if.reshape(-1, 1), row(norm_w), row(skip), w_out.astype(BF16),
            row(ln_g), row(ln_b))
    in_specs = [pl.BlockSpec((1, ts, d), lambda b, j: (b, j, 0))] + [_full(a.shape) for a in args[1:]]
    dh = bw // B_HEADS
    assert dh == LANES and CHUNK == LANES
    return pl.pallas_call(
        functools.partial(_even_kernel, ts=ts, aw=aw, bw=bw),
        grid=(bsz, seq // ts),
        in_specs=in_specs,
        out_specs=pl.BlockSpec((1, ts, d), lambda b, j: (b, j, 0)),
        out_shape=jax.ShapeDtypeStruct((bsz, seq, d), F32),
        scratch_shapes=[pltpu.VMEM((8 + ts, bw), F32),
                        pltpu.VMEM((B_HEADS, dh, dh + LANES), F32),
                        pltpu.VMEM((B_HEADS, LANES), F32)],
        compiler_params=_cparams(("arbitrary", "arbitrary")),
        name="even_mixer",
    )(*args)


def _memfold_kernel(mem_ref, wkv_ref, wq_ref, wo_ref, wqk_ref, vo_ref, *, d):
    dh = d // X_HEADS
    m_len = mem_ref.shape[1]
    kv = _dot(mem_ref[0].astype(BF16), wkv_ref[...])
    k = (kv[:, :d] * (dh ** -0.5 * LOG2E)).astype(BF16)
    v = kv[:, d:].astype(BF16)
    for h in range(X_HEADS):
        hs = slice(h * dh, (h + 1) * dh)
        ms = slice(h * m_len, (h + 1) * m_len)
        wqk_ref[0, :, ms] = _dot_nt(wq_ref[:, hs], k[:, hs]).astype(BF16)
        vo_ref[0, ms, :] = _dot(v[:, hs], wo_ref[hs, :]).astype(BF16)


def memory_fold(mem, wkv, wq, wo):
    bsz, m_len, d = mem.shape
    hm = X_HEADS * m_len
    args = (mem, wkv.astype(BF16), wq.astype(BF16), wo.astype(BF16))
    return pl.pallas_call(
        functools.partial(_memfold_kernel, d=d),
        grid=(bsz,),
        in_specs=[pl.BlockSpec((1, m_len, d), lambda b: (b, 0, 0))] + [_full(a.shape) for a in args[1:]],
        out_specs=[pl.BlockSpec((1, d, hm), lambda b: (b, 0, 0)),
                   pl.BlockSpec((1, hm, d), lambda b: (b, 0, 0))],
        out_shape=[jax.ShapeDtypeStruct((bsz, d, hm), BF16),
                   jax.ShapeDtypeStruct((bsz, hm, d), BF16)],
        compiler_params=_cparams(("arbitrary",)),
        name="memory_fold",
    )(*args)


ROUTE_W = 128
XATTN_TS = 1024
DEST_ROWS = 8
SLOT_TB = 2048
SLOT_SUB = 512


def _xattn_kernel(x_ref, wqk_ref, vo_ref, ln_g_ref, ln_b_ref, wr_ref, br_ref,
                  o_ref, xp_ref, route_ref, cnt_ref):
    m_len = wqk_ref.shape[2] // X_HEADS

    @pl.when(pl.program_id(1) == 0)
    def _():
        cnt_ref[...] = jnp.zeros_like(cnt_ref)

    x = x_ref[0]
    ts = x.shape[0]
    xb = x.astype(BF16)
    probs = []
    for h in range(X_HEADS):
        s = _dot(xb, wqk_ref[0, :, h * m_len:(h + 1) * m_len])
        p = jnp.exp2(s - jnp.max(s, axis=1, keepdims=True))
        probs.append((p / jnp.sum(p, axis=1, keepdims=True)).astype(BF16))
    y = _dot(jnp.concatenate(probs, axis=1), vo_ref[0])
    x2 = _ln(DN_ALPHA * x + y, ln_g_ref[...], ln_b_ref[...])
    o_ref[0] = x2
    xp_ref[0] = _pack_halves(x2)

    x_hi = x2.astype(BF16)
    x_lo = (x2 - x_hi.astype(F32)).astype(BF16)
    logits = (_dot(x_hi, wr_ref[0]) + _dot(x_lo, wr_ref[0]) + _dot(x_hi, wr_ref[1])) + br_ref[...]
    lane = lax.broadcasted_iota(jnp.int32, (ts, ROUTE_W), 1)
    is_g = lane < N_GROUPS
    lg = jnp.where(is_g, logits, NEG)
    mg = jnp.max(lg, axis=1, keepdims=True)
    gi = jnp.min(jnp.where(jnp.logical_and(is_g, lg == mg), lane, ROUTE_W), axis=1, keepdims=True)
    gate_g = 1.0 / jnp.sum(jnp.where(is_g, jnp.exp(lg - mg), 0.0), axis=1, keepdims=True)
    lo = N_GROUPS + gi * EXPERTS_PER_GROUP
    in_grp = jnp.logical_and(lane >= lo, lane < lo + EXPERTS_PER_GROUP)
    le = jnp.where(in_grp, logits, NEG)
    v1 = jnp.max(le, axis=1, keepdims=True)
    i1 = jnp.min(jnp.where(jnp.logical_and(in_grp, le == v1), lane, ROUTE_W), axis=1, keepdims=True)
    le2 = jnp.where(lane == i1, NEG, le)
    v2 = jnp.max(le2, axis=1, keepdims=True)
    i2 = jnp.min(jnp.where(jnp.logical_and(in_grp, le2 == v2), lane, ROUTE_W), axis=1, keepdims=True)
    e21 = jnp.exp(v2 - v1)
    p1 = 1.0 / (1.0 + e21)
    p2 = e21 * p1
    e1 = (i1 - N_GROUPS).astype(F32)
    e2 = (i2 - N_GROUPS).astype(F32)
    rec = jnp.where(lane == 0, e1, 0.0)
    rec = jnp.where(lane == 1, e2, rec)
    rec = jnp.where(lane == 2, gate_g * p1, rec)
    rec = jnp.where(lane == 3, gate_g * p2, rec)
    route_ref[0] = rec
    sel = jnp.logical_or(lane == i1 - N_GROUPS, lane == i2 - N_GROUPS)
    cnt_ref[0] += jnp.sum(jnp.where(sel, 1.0, 0.0), axis=0, keepdims=True)


def xattn_router_layer(x, wqk, vo, ln_g, ln_b, w_rg, b_rg, w_re, b_re, *, ts):
    bsz, seq, d = x.shape
    hm = wqk.shape[2]
    wr = jnp.zeros((d, ROUTE_W), F32).at[:, :N_GROUPS].set(w_rg).at[:, N_GROUPS:N_GROUPS + N_EXPERTS].set(w_re)
    br = jnp.zeros((1, ROUTE_W), F32).at[0, :N_GROUPS].set(b_rg).at[0, N_GROUPS:N_GROUPS + N_EXPERTS].set(b_re)
    wr_hi = wr.astype(BF16)
    wr_lo = (wr - wr_hi.astype(F32)).astype(BF16)
    wr = jnp.stack([wr_hi, wr_lo])
    args = (x, wqk, vo, ln_g.reshape(1, -1), ln_b.reshape(1, -1), wr, br)
    in_specs = [pl.BlockSpec((1, ts, d), lambda b, j: (b, j, 0)),
                pl.BlockSpec((1, d, hm), lambda b, j: (b, 0, 0)),
                pl.BlockSpec((1, hm, d), lambda b, j: (b, 0, 0))] + [_full(a.shape) for a in args[3:]]
    return pl.pallas_call(
        _xattn_kernel,
        grid=(bsz, seq // ts),
        in_specs=in_specs,
        out_specs=[pl.BlockSpec((1, ts, d), lambda b, j: (b, j, 0)),
                   pl.BlockSpec((1, ts, d // 2), lambda b, j: (b, j, 0)),
                   pl.BlockSpec((1, ts, ROUTE_W), lambda b, j: (b, j, 0)),
                   pl.BlockSpec((1, 1, ROUTE_W), lambda b, j: (b, 0, 0))],
        out_shape=[jax.ShapeDtypeStruct((bsz, seq, d), F32),
                   jax.ShapeDtypeStruct((bsz, seq, d // 2), jnp.int32),
                   jax.ShapeDtypeStruct((bsz, seq, ROUTE_W), F32),
                   jax.ShapeDtypeStruct((bsz, 1, ROUTE_W), F32)],
        compiler_params=_cparams(("arbitrary", "arbitrary")),
        name="xattn_router",
    )(*args)


def _moe_output(x2, g0, g1, rec, ln_g, ln_b):
    a_lo, a_hi = _unpack_halves(g0)
    b_lo, b_hi = _unpack_halves(g1)
    w0, w1 = rec[:, 2:3], rec[:, 3:4]
    y = jnp.concatenate([w0 * a_lo + w1 * b_lo, w0 * a_hi + w1 * b_hi], axis=1)
    return _ln(DN_ALPHA * x2 + y, ln_g, ln_b)


def _swa_kernel(x_ref, g0_ref, g1_ref, route_ref, pln_g_ref, pln_b_ref,
                wqkv_ref, bqkv_ref, cos_ref, sin_ref, sink_ref, wo_ref, ln_g_ref, ln_b_ref,
                *rest, ts, cq, ckv):
    o_ref, kprev, vprev = rest[-3:]
    j = pl.program_id(1)
    nb = ts // CHUNK
    dh = C_HEAD_DIM
    grp = (cq // dh) // C_KV_HEADS

    @pl.when(j == 0)
    def _():
        kprev[...] = jnp.zeros_like(kprev)
        vprev[...] = jnp.zeros_like(vprev)

    x = _moe_output(x_ref[0], g0_ref[0, 0], g1_ref[0, 0], route_ref[0], pln_g_ref[...], pln_b_ref[...])
    qkv = _dot(x.astype(BF16), wqkv_ref[...]) + bqkv_ref[...]
    cos = cos_ref[...]
    sin = sin_ref[...]
    lane = lax.broadcasted_iota(jnp.int32, (ts, LANES), 1)
    first_half = (lane % dh) < (dh // 2)

    def rope(t):
        outs = []
        for c in range(t.shape[1] // LANES):
            tc = t[:, c * LANES:(c + 1) * LANES]
            rot = jnp.where(first_half, pltpu.roll(tc, LANES - dh // 2, 1), pltpu.roll(tc, dh // 2, 1))
            outs.append(tc * cos + rot * sin)
        return jnp.concatenate(outs, axis=1)

    q = rope(qkv[:, :cq]).astype(BF16)
    k = rope(qkv[:, cq:cq + 2 * ckv]).astype(BF16)
    v = qkv[:, cq + 2 * ckv:].astype(BF16)

    r_i = lax.broadcasted_iota(jnp.int32, (CHUNK, 2 * CHUNK), 0)
    c_i = lax.broadcasted_iota(jnp.int32, (CHUNK, 2 * CHUNK), 1)
    band = jnp.logical_and(c_i > r_i, c_i <= r_i + CHUNK)
    sink_col = c_i == 0
    lane_k = lax.broadcasted_iota(jnp.int32, (2 * CHUNK, LANES), 1)
    key_row = lax.broadcasted_iota(jnp.int32, (2 * CHUNK, LANES), 0)
    lane_q = lax.broadcasted_iota(jnp.int32, (CHUNK, LANES), 1)
    ones_blk = jnp.ones((2 * CHUNK, LANES), BF16)
    zero_b = jnp.zeros((), BF16)
    blocks = []
    for c in range(nb):
        sl = slice(c * CHUNK, (c + 1) * CHUNK)
        if c == 0:
            kb = jnp.concatenate([kprev[...].astype(BF16), k[sl]], axis=0)
            vb = jnp.concatenate([vprev[...].astype(BF16), v[sl]], axis=0)
            first_key = jnp.where(j > 0, 0, CHUNK)
            valid = jnp.logical_and(band, c_i >= first_key)
        else:
            kb = k[(c - 1) * CHUNK:(c + 1) * CHUNK]
            vb = v[(c - 1) * CHUNK:(c + 1) * CHUNK]
            valid = band
        tiles = []
        for h in range(C_KV_HEADS):
            kd = kb[:, h * LANES:(h + 1) * LANES]
            vd = vb[:, h * LANES:(h + 1) * LANES]
            k_lo = jnp.where(lane_k < dh, kd, zero_b)
            k_hi = jnp.where(lane_k >= dh, kd, zero_b)
            vz = jnp.where(key_row == 0, zero_b, vd)
            q2 = jnp.concatenate([q[sl, (2 * h) * LANES:(2 * h + 1) * LANES],
                                  q[sl, (2 * h + 1) * LANES:(2 * h + 2) * LANES]], axis=0)
            s_lo = _dot_nt(q2, k_lo)
            s_hi = _dot_nt(q2, k_hi)
            parts = []
            for qk, g in ((s_lo[:CHUNK], 0), (s_lo[CHUNK:], 2), (s_hi[:CHUNK], 1), (s_hi[CHUNK:], 3)):
                sink = sink_ref[h * grp + g]
                parts.append(jnp.where(valid, qk, jnp.where(sink_col, sink, NEG)))
            s = jnp.concatenate(parts, axis=0)
            p = jnp.exp2(s - jnp.max(s, axis=1, keepdims=True)).astype(BF16)
            den = _dot(p, ones_blk)
            o2 = _dot(p, vz) / den
            tiles.append(jnp.where(lane_q < dh, o2[:CHUNK], o2[2 * CHUNK:3 * CHUNK]))
            tiles.append(jnp.where(lane_q < dh, o2[CHUNK:2 * CHUNK], o2[3 * CHUNK:]))
        blocks.append(jnp.concatenate(tiles, axis=1))
    kprev[...] = k[(nb - 1) * CHUNK:].astype(F32)
    vprev[...] = v[(nb - 1) * CHUNK:].astype(F32)
    att = (jnp.concatenate(blocks, axis=0) if nb > 1 else blocks[0]).astype(BF16)
    y = _dot(att, wo_ref[...])
    o_ref[0] = _ln(DN_ALPHA * x + y, ln_g_ref[...], ln_b_ref[...])


def swa_prepare(w_qkv, b_qkv, sinks, w_o, seq):
    cq = w_o.shape[0]
    ckv = (w_qkv.shape[1] - cq) // 2
    dh = C_HEAD_DIM
    inv = ROPE_THETA ** (-jnp.arange(0, dh, 2, dtype=F32) / dh)
    ang = jnp.arange(seq, dtype=F32)[:, None] * inv[None, :]
    reps = LANES // (dh // 2)
    sign = jnp.tile(jnp.concatenate([-jnp.ones((dh // 2,), F32), jnp.ones((dh // 2,), F32)]), LANES // dh)
    cos_t = jnp.tile(jnp.cos(ang), (1, reps))
    sin_t = jnp.tile(jnp.sin(ang), (1, reps)) * sign[None, :]
    assert 2 * dh == LANES

    def dup_heads(t):
        th = t.reshape(t.shape[:-1] + (ckv // dh, dh))
        return jnp.concatenate([th, th], axis=-1).reshape(t.shape[:-1] + (2 * ckv,))

    qs = dh ** -0.5 * LOG2E
    w_all = jnp.concatenate([w_qkv[:, :cq] * qs, dup_heads(w_qkv[:, cq:cq + ckv]),
                             dup_heads(w_qkv[:, cq + ckv:])], axis=1)
    b_all = jnp.concatenate([b_qkv[:cq] * qs, dup_heads(b_qkv[cq:cq + ckv]), dup_heads(b_qkv[cq + ckv:])])
    return w_all.astype(BF16), b_all.reshape(1, -1), cos_t, sin_t, sinks.astype(F32) * LOG2E, w_o.astype(BF16)


def swa_mixer_layer(pending, prepared, ln_g, ln_b, *, ts):
    x, gs, route, pln_g, pln_b = pending
    bsz, seq, d = x.shape
    cq = prepared[5].shape[0]
    ckv = (prepared[0].shape[1] - cq) // 4
    shared = (pln_g.reshape(1, -1), pln_b.reshape(1, -1), *prepared, ln_g.reshape(1, -1), ln_b.reshape(1, -1))
    bp = bsz // len(gs)
    out = None
    for p, g in enumerate(gs):
        b0 = p * bp
        g = g.reshape(2, bp, seq, d // 2)
        args = [x, g, g, route, *shared]
        in_specs = [pl.BlockSpec((1, ts, d), lambda b, j: (b0 + b, j, 0)),
                    pl.BlockSpec((1, 1, ts, d // 2), lambda b, j: (0, b, j, 0)),
                    pl.BlockSpec((1, 1, ts, d // 2), lambda b, j: (1, b, j, 0)),
                    pl.BlockSpec((1, ts, ROUTE_W), lambda b, j: (b0 + b, j, 0)),
                    _full((1, d)), _full((1, d)),
                    _full(shared[2].shape), _full(shared[3].shape),
                    pl.BlockSpec((ts, LANES), lambda b, j: (j, 0)),
                    pl.BlockSpec((ts, LANES), lambda b, j: (j, 0)),
                    pl.BlockSpec(memory_space=pltpu.SMEM),
                    _full(shared[7].shape), _full(shared[8].shape), _full(shared[9].shape)]
        aliases = {}
        if out is not None:
            args.append(out)
            in_specs.append(pl.BlockSpec(memory_space=pl.ANY))
            aliases = {len(args) - 1: 0}
        out = pl.pallas_call(
            functools.partial(_swa_kernel, ts=ts, cq=cq, ckv=ckv),
            grid=(bp, seq // ts),
            in_specs=in_specs,
            out_specs=pl.BlockSpec((1, ts, d), lambda b, j: (b0 + b, j, 0)),
            out_shape=jax.ShapeDtypeStruct((bsz, seq, d), F32),
            input_output_aliases=aliases,
            scratch_shapes=[pltpu.VMEM((CHUNK, 2 * ckv), F32), pltpu.VMEM((CHUNK, 2 * ckv), F32)],
            compiler_params=_cparams(("arbitrary", "arbitrary")),
            name="swa_mixer",
        )(*args)
    return out


def _slot_kernel(route_ref, pstart_ref, dest_ref, carry_ref, *, tb):
    @pl.when(pl.program_id(0) == 0)
    def _():
        carry_ref[...] = pstart_ref[...]

    sub = min(SLOT_SUB, tb)
    lane = lax.broadcasted_iota(jnp.int32, (sub, ROUTE_W), 1)
    r = lax.broadcasted_iota(jnp.int32, (sub, sub), 0)
    c = lax.broadcasted_iota(jnp.int32, (sub, sub), 1)
    before = jnp.where(c < r, 1.0, 0.0).astype(BF16)
    carry = carry_ref[...]
    for k in range(tb // sub):
        rows = slice(k * sub, (k + 1) * sub)
        rec = route_ref[rows, :]
        e0 = rec[:, 0:1].astype(jnp.int32)
        e1 = rec[:, 1:2].astype(jnp.int32)
        oh0 = lane == e0
        oh1 = lane == e1
        ohs = jnp.where(jnp.logical_or(oh0, oh1), 1.0, 0.0)
        prefix = _dot(before, ohs.astype(BF16)) + carry
        d0 = jnp.sum(jnp.where(oh0, prefix, 0.0), axis=1, keepdims=True)
        d1 = jnp.sum(jnp.where(oh1, prefix, 0.0), axis=1, keepdims=True)
        dest = jnp.where(lane == 0, d0, jnp.where(lane == 1, d1, 0.0))
        dest_ref[:, rows] = dest.T[:DEST_ROWS].astype(jnp.int32)
        carry = carry + jnp.sum(ohs, axis=0, keepdims=True)
    carry_ref[...] = carry


def moe_slots(route, pstart, *, tb):
    n = route.shape[0]
    return pl.pallas_call(
        functools.partial(_slot_kernel, tb=tb),
        grid=(n // tb,),
        in_specs=[pl.BlockSpec((tb, ROUTE_W), lambda i: (i, 0)), _full((1, ROUTE_W))],
        out_specs=pl.BlockSpec((DEST_ROWS, tb), lambda i: (0, i)),
        out_shape=jax.ShapeDtypeStruct((DEST_ROWS, n), jnp.int32),
        scratch_shapes=[pltpu.VMEM((1, ROUTE_W), F32)],
        compiler_params=_cparams(("arbitrary",)),
        name="moe_slots",
    )(route, pstart)


def _ffn_kernel(blk_exp_ref, new_exp_ref, nblk_ref, xs_ref, w1_ref, w3_ref, w2_ref, ys_ref,
                w1_b, w3_b, w2_b):
    i = pl.program_id(0)
    used = i < nblk_ref[0]

    @pl.when(jnp.logical_and(used, new_exp_ref[i] == 1))
    def _():
        w1_b[...] = w1_ref[0, 0].astype(BF16)
        w3_b[...] = w3_ref[0, 0].astype(BF16)
        w2_b[...] = w2_ref[0, 0].astype(BF16)

    @pl.when(used)
    def _():
        x_lo, x_hi = _unpack_halves(xs_ref[...])
        x_lo, x_hi = x_lo.astype(BF16), x_hi.astype(BF16)
        dl = x_lo.shape[1]
        h1 = _dot(x_lo, w1_b[:dl, :]) + _dot(x_hi, w1_b[dl:, :])
        h3 = _dot(x_lo, w3_b[:dl, :]) + _dot(x_hi, w3_b[dl:, :])
        h = (_silu(h1) * h3).astype(BF16)
        ys_ref[...] = _pack_halves(_dot(h, w2_b[...]))

    @pl.when(jnp.logical_not(used))
    def _():
        ys_ref[...] = jnp.zeros_like(ys_ref)


def moe_ffn(xs, blk_exp, new_exp, nblk, w1, w3, w2, *, layer, bm):
    n_pad, dp = xs.shape
    d = 2 * dp
    de = w1.shape[3]
    n_blk = n_pad // bm

    def x_map(i, be, ne, nb):
        return (jnp.minimum(i, nb[0] - 1), 0)

    def w_map(i, be, ne, nb):
        return (layer, be[i], 0, 0)

    return pl.pallas_call(
        _ffn_kernel,
        grid_spec=pltpu.PrefetchScalarGridSpec(
            num_scalar_prefetch=3,
            grid=(n_blk,),
            in_specs=[pl.BlockSpec((bm, dp), x_map),
                      pl.BlockSpec((1, 1, d, de), w_map),
                      pl.BlockSpec((1, 1, d, de), w_map),
                      pl.BlockSpec((1, 1, de, d), w_map)],
            out_specs=pl.BlockSpec((bm, dp), lambda i, be, ne, nb: (i, 0)),
            scratch_shapes=[pltpu.VMEM((d, de), BF16), pltpu.VMEM((d, de), BF16),
                            pltpu.VMEM((de, d), BF16)]),
        out_shape=jax.ShapeDtypeStruct((n_pad, dp), jnp.int32),
        compiler_params=_cparams(("arbitrary",)),
        name="moe_ffn",
    )(blk_exp, new_exp, nblk, xs, w1, w3, w2)


def _combine_kernel(x_ref, g0_ref, g1_ref, route_ref, ln_g_ref, ln_b_ref, *rest):
    o_ref = rest[-1]
    o_ref[...] = _moe_output(x_ref[...], g0_ref[0], g1_ref[0], route_ref[...], ln_g_ref[...], ln_b_ref[...])


def moe_combine(x, g, route, ln_g, ln_b, *, row0, tb, prev=None):
    n, d = x.shape
    m = g.shape[1]
    blk0 = row0 // tb
    row_spec = pl.BlockSpec((tb, d), lambda i: (blk0 + i, 0))
    args = [x, g, g, route, ln_g.reshape(1, -1), ln_b.reshape(1, -1)]
    in_specs = [row_spec,
                pl.BlockSpec((1, tb, d // 2), lambda i: (0, i, 0)),
                pl.BlockSpec((1, tb, d // 2), lambda i: (1, i, 0)),
                pl.BlockSpec((tb, ROUTE_W), lambda i: (blk0 + i, 0)),
                _full((1, d)), _full((1, d))]
    aliases = {}
    if prev is not None:
        args.append(prev)
        in_specs.append(pl.BlockSpec(memory_space=pl.ANY))
        aliases = {len(args) - 1: 0}
    return pl.pallas_call(
        _combine_kernel,
        grid=(m // tb,),
        in_specs=in_specs,
        out_specs=row_spec,
        out_shape=jax.ShapeDtypeStruct((n, d), F32),
        input_output_aliases=aliases,
        compiler_params=_cparams(("arbitrary",)),
        name="moe_combine",
    )(*args)


MOE_BM = 1024
COMBINE_PARTS = 4
COMBINE_TB = 1024
MIXER_PARTS = 2
MIXER_TS = 1024


def hierarchical_moe_layer(x2, xp, route, counts, w1, w3, w2, ln_g, ln_b, *, layer, defer_combine,
                           side_work=None, side_inputs=None):
    bsz, seq, d = x2.shape
    n = bsz * seq
    bm = MOE_BM
    rt = route.reshape(n, ROUTE_W)
    tb = min(512, n)
    n_blk = (2 * n) // bm + N_EXPERTS
    cnt = counts.sum(axis=0)[0, :N_EXPERTS].astype(jnp.int32)
    pcnt = (cnt + bm - 1) // bm * bm
    pends = jnp.cumsum(pcnt)
    pstart = pends - pcnt
    nblk = (pends[-1] // bm).astype(jnp.int32).reshape(1)
    blk_row = jnp.arange(n_blk, dtype=jnp.int32) * bm
    blk_exp = jnp.minimum(jnp.sum((pends[None, :] <= blk_row[:, None]).astype(jnp.int32), axis=1),
                          N_EXPERTS - 1)
    last_exp = blk_exp[jnp.maximum(nblk[0] - 1, 0)]
    blk_exp = jnp.where(jnp.arange(n_blk) < nblk[0], blk_exp, last_exp)
    new_exp = jnp.concatenate([jnp.ones((1,), jnp.int32),
                               (blk_exp[1:] != blk_exp[:-1]).astype(jnp.int32)])
    pstart_rec = jnp.zeros((1, ROUTE_W), F32).at[0, :N_EXPERTS].set(pstart.astype(F32))
    dest = moe_slots(rt, pstart_rec, tb=min(SLOT_TB, n))
    xs = moe_dispatch(xp.reshape(n, d // 2), dest[0], dest[1], n_blk * bm)
    side = None
    if side_work is not None:
        side_inputs, _ = lax.optimization_barrier((side_inputs, dest))
        nblk, side = lax.optimization_barrier((nblk, side_work(*side_inputs)))
    ys = moe_ffn(xs, blk_exp, new_exp, nblk, w1, w3, w2, layer=layer, bm=bm)
    if defer_combine:
        bparts = MIXER_PARTS if bsz % MIXER_PARTS == 0 else 1
        mb = n // bparts
        gs = [moe_gather(ys, dest[0, p * mb:(p + 1) * mb], dest[1, p * mb:(p + 1) * mb]) for p in range(bparts)]
        return (x2, gs, route, ln_g, ln_b), side
    parts = COMBINE_PARTS if n % (COMBINE_PARTS * tb * 8) == 0 else 1
    m = n // parts
    out = None
    for p in range(parts):
        g = moe_gather(ys, dest[0, p * m:(p + 1) * m], dest[1, p * m:(p + 1) * m])
        out = moe_combine(x2.reshape(n, d), g, rt, ln_g, ln_b, row0=p * m, tb=min(COMBINE_TB, m), prev=out)
    return out.reshape(bsz, seq, d)


SC_ROWS = 128


def _sc_mesh():
    return plsc.VectorSubcoreMesh(core_axis_name="c", subcore_axis_name="s")


def moe_dispatch(xf, dest0, dest1, n_pad):
    n, d = xf.shape
    info = plsc.get_sparse_core_info()
    nw = info.num_cores * info.num_subcores
    per_w = n // nw
    r = min(SC_ROWS, per_w)

    def body(x_hbm, d0_hbm, d1_hbm, xs_hbm, i0_v, i1_v, rows_v, sem):
        wid = lax.axis_index("s") * info.num_cores + lax.axis_index("c")

        @pl.loop(0, per_w // r)
        def _(c):
            base = pl.multiple_of(wid * per_w + c * r, 8)
            pltpu.sync_copy(d0_hbm.at[pl.ds(base, r)], i0_v)
            pltpu.sync_copy(d1_hbm.at[pl.ds(base, r)], i1_v)
            pltpu.sync_copy(x_hbm.at[pl.ds(base, r)], rows_v)
            pltpu.async_copy(rows_v, xs_hbm.at[i0_v], sem).wait()
            pltpu.async_copy(rows_v, xs_hbm.at[i1_v], sem).wait()

    return pl.kernel(
        body, out_type=jax.ShapeDtypeStruct((n_pad, d), xf.dtype), mesh=_sc_mesh(),
        scratch_types=[pltpu.VMEM((r,), jnp.int32), pltpu.VMEM((r,), jnp.int32),
                       pltpu.VMEM((r, d), xf.dtype), pltpu.SemaphoreType.DMA],
        name="moe_dispatch",
    )(xf, dest0, dest1)


def moe_gather(ys, dest0, dest1):
    n = dest0.shape[0]
    d = ys.shape[1]
    info = plsc.get_sparse_core_info()
    nw = info.num_cores * info.num_subcores
    per_w = n // nw
    r = min(SC_ROWS, per_w)

    def body(ys_hbm, d0_hbm, d1_hbm, g_hbm, i_v, rows_v, sem):
        wid = lax.axis_index("s") * info.num_cores + lax.axis_index("c")

        @pl.loop(0, per_w // r)
        def _(c):
            base = pl.multiple_of(wid * per_w + c * r, 8)
            for k, d_hbm in enumerate((d0_hbm, d1_hbm)):
                pltpu.sync_copy(d_hbm.at[pl.ds(base, r)], i_v)
                pltpu.async_copy(ys_hbm.at[i_v], rows_v, sem).wait()
                pltpu.sync_copy(rows_v, g_hbm.at[k, pl.ds(base, r)])

    return pl.kernel(
        body, out_type=jax.ShapeDtypeStruct((2, n, d), ys.dtype), mesh=_sc_mesh(),
        scratch_types=[pltpu.VMEM((r,), jnp.int32), pltpu.VMEM((r, d), ys.dtype),
                       pltpu.SemaphoreType.DMA],
        name="moe_gather",
    )(ys, dest0, dest1)


def kernel(x, mem, ln_g, ln_b, ev_w_in, ev_gm_ln_g, ev_gm_ln_b, ev_gm_ws, ev_gm_bs, ev_conv_w, ev_conv_b, ev_wq, ev_wk, ev_wv, ev_w_if, ev_b_if, ev_norm_w, ev_skip, ev_w_out, od_w_qkv, od_b_qkv, od_sinks, od_w_o, xa_wq, xa_wkv, xa_wo, moe_w_rg, moe_b_rg, moe_w_re, moe_b_re, moe_w1, moe_w3, moe_w2):
    bsz, seq, d = x.shape
    depth = ln_g.shape[0]
    assert depth == DEPTH
    ts = min(MIXER_TS, seq)

    folded = None
    for l in range(depth):
        wqk, vo = folded if folded is not None else memory_fold(mem, xa_wkv[l], xa_wq[l], xa_wo[l])
        if l % 2 == 0:
            e = l // 2
            x = even_mixer_layer(x, ev_w_in[e], ev_gm_ln_g[e], ev_gm_ln_b[e], ev_gm_ws[e], ev_gm_bs[e],
                                 ev_conv_w[e], ev_conv_b[e], ev_wq[e], ev_wk[e], ev_wv[e], ev_w_if[e],
                                 ev_b_if[e], ev_norm_w[e], ev_skip[e], ev_w_out[e],
                                 ln_g[l, 0], ln_b[l, 0], ts=ts)
        else:
            o = l // 2
            prepared = swa_prepare(od_w_qkv[o], od_b_qkv[o], od_sinks[o], od_w_o[o], seq)
            x = swa_mixer_layer(x, prepared, ln_g[l, 0], ln_b[l, 0], ts=ts)
        x, xp, route, counts = xattn_router_layer(x, wqk, vo, ln_g[l, 1], ln_b[l, 1],
                                                  moe_w_rg[l], moe_b_rg[l], moe_w_re[l], moe_b_re[l],
                                                  ts=min(XATTN_TS, seq))
        defer = l + 1 < depth and (l + 1) % 2 == 1
        side_inputs = (mem, xa_wkv[l + 1], xa_wq[l + 1], xa_wo[l + 1]) if defer else None
        x = hierarchical_moe_layer(x, xp, route, counts, moe_w1, moe_w3, moe_w2,
                                   ln_g[l, 2], ln_b[l, 2], layer=l, defer_combine=defer,
                                   side_work=memory_fold if defer else None, side_inputs=side_inputs)
        folded = None
        if defer:
            x, folded = x
    return x
```

```python
import functools
import math

import jax
import jax.numpy as jnp
from jax import lax
from jax.experimental import pallas as pl
from jax.experimental.pallas import tpu as pltpu
from jax.experimental.pallas import tpu_sc as plsc

F32 = jnp.float32
BF16 = jnp.bfloat16

A_GROUPS = 4
CHUNK = 128
B_HEADS = 4
B_CONV = 4
C_HEAD_DIM = 64
C_KV_HEADS = 4
X_HEADS = 4
N_GROUPS = 4
EXPERTS_PER_GROUP = 8
N_EXPERTS = N_GROUPS * EXPERTS_PER_GROUP
ROPE_THETA = 10000.0
LN_EPS = 1e-5
DEPTH = 2
DN_ALPHA = (2 * DEPTH) ** 0.25

LANES = 128
VMEM_LIMIT = 48 * 1024 * 1024
NEG = -1e30


def _cparams(sem):
    return pltpu.CompilerParams(dimension_semantics=sem, vmem_limit_bytes=VMEM_LIMIT)


def _full(shape):
    nd = len(shape)
    return pl.BlockSpec(shape, lambda *_: (0,) * nd)


def _dot(a, b):
    return jnp.dot(a, b, preferred_element_type=F32)


def _dot_nt(a, b):
    return lax.dot_general(a, b, (((1,), (1,)), ((), ())), preferred_element_type=F32)


def _split_dot(a, b_bf16):
    hi = a.astype(BF16)
    lo = (a - hi.astype(F32)).astype(BF16)
    return _dot(hi, b_bf16) + _dot(lo, b_bf16)


def _ln(x, g, b):
    mu = jnp.mean(x, axis=-1, keepdims=True)
    xc = x - mu
    var = jnp.mean(xc * xc, axis=-1, keepdims=True)
    return xc * lax.rsqrt(var + LN_EPS) * g + b


LOG2E = math.log2(math.e)


def _silu(x):
    return x * (1.0 / (1.0 + jnp.exp2(x * -LOG2E)))


def _gelu(x):
    return 0.5 * x * (1.0 + jnp.tanh(math.sqrt(2.0 / math.pi) * (x + 0.044715 * (x * x * x))))


def _pack_halves(x):
    c = x.shape[1] // 2
    lo = lax.bitcast_convert_type(x[:, :c].astype(BF16).astype(F32), jnp.uint32)
    hi = lax.bitcast_convert_type(x[:, c:].astype(BF16).astype(F32), jnp.uint32)
    return lax.bitcast_convert_type((lo >> 16) | hi, jnp.int32)


def _unpack_halves(p):
    u = lax.bitcast_convert_type(p, jnp.uint32)
    lo = lax.bitcast_convert_type(u << 16, F32)
    hi = lax.bitcast_convert_type(u & jnp.uint32(0xFFFF0000), F32)
    return lo, hi


def _log_sigmoid(x):
    return jnp.minimum(x, 0.0) - jnp.log(1.0 + jnp.exp(-jnp.abs(x)))


def _even_kernel(x_ref, w_in_ref, gm_g_ref, gm_b_ref, gm_w_ref, gm_bias_ref,
                 conv_w_ref, conv_b_ref, wq_ref, wk_ref, wv_ref, wif_t_ref, bif_t_ref,
                 norm_w_ref, skip_ref, w_out_ref, ln_g_ref, ln_b_ref,
                 o_ref,
                 xm_buf, ct_ref, m_ref, *, ts, aw, bw):
    dh = bw // B_HEADS
    agd = aw // A_GROUPS
    nck = ts // CHUNK
    pad = 8
    j = pl.program_id(1)

    @pl.when(j == 0)
    def _():
        xm_buf[0:pad, :] = jnp.zeros((pad, bw), F32)
        ct_ref[...] = jnp.zeros_like(ct_ref)
        m_ref[...] = jnp.zeros_like(m_ref)

    row = lax.broadcasted_iota(jnp.int32, (CHUNK, CHUNK), 0)
    col = lax.broadcasted_iota(jnp.int32, (CHUNK, CHUNK), 1)
    causal = col <= row
    diag = col == row
    triu = jnp.where(row <= col, 1.0, 0.0).astype(BF16)
    ones_blk = jnp.ones((CHUNK, LANES), BF16)

    x = x_ref[0]
    proj = _dot(x.astype(BF16), w_in_ref[...])
    a_u = _gelu(proj[:, :aw])
    a_v = _gelu(proj[:, aw:2 * aw])
    xm = proj[:, 2 * aw:2 * aw + bw]
    z = proj[:, 2 * aw + bw:]

    vn = _ln(a_v, gm_g_ref[...], gm_b_ref[...]).astype(BF16)
    ya_chunks = []
    for c in range(nck):
        cols = []
        for g in range(A_GROUPS):
            v_cg = vn[c * CHUNK:(c + 1) * CHUNK, g * agd:(g + 1) * agd]
            cols.append(_dot(gm_w_ref[g], v_cg))
        ya_chunks.append(jnp.concatenate(cols, axis=1) + gm_bias_ref[...])
    y_a = a_u * jnp.concatenate(ya_chunks, axis=0)

    xm_buf[pad:pad + ts, :] = xm
    conv = conv_b_ref[...] + conv_w_ref[B_CONV - 1:B_CONV, :] * xm
    for k in range(B_CONV - 1):
        sh = B_CONV - 1 - k
        conv = conv + conv_w_ref[k:k + 1, :] * xm_buf[pad - sh:pad - sh + ts, :]
    xm_buf[pad - (B_CONV - 1):pad, :] = xm_buf[pad + ts - (B_CONV - 1):pad + ts, :]
    xc = _silu(conv)
    xc_b = xc.astype(BF16)
    q = _dot_head_pairs(xc_b, wq_ref)
    k_ = _dot_head_pairs(xc_b, wk_ref)
    v = _dot_head_pairs(xm.astype(BF16), wv_ref)
    gate_in = jnp.concatenate([q, k_, v], axis=1).astype(BF16)
    gates_t = _dot_nt(wif_t_ref[...], gate_in) + bif_t_ref[...]
    ig_all = gates_t[:B_HEADS, :]
    lf_all = _log_sigmoid(gates_t[B_HEADS:, :])
    q_b = q.astype(BF16)
    k_b = (k_ * dh ** -0.5).astype(BF16)
    v_b = v.astype(BF16)
    gate_z = _silu(z)

    lf_c = [lf_all[:, c * CHUNK:(c + 1) * CHUNK] for c in range(nck)]
    b_all = _split_dot(jnp.concatenate(lf_c, axis=0), triu)
    m_prev = m_ref[...]
    a_c, m_c, decay_c, wg_c = [], [], [], []
    for c in range(nck):
        b_r = b_all[c * B_HEADS:(c + 1) * B_HEADS]
        a_r = ig_all[:, c * CHUNK:(c + 1) * CHUNK] - b_r
        bl = b_r[:, CHUNK - 1:CHUNK]
        g_r = bl + a_r
        m_new = jnp.maximum(bl + m_prev, jnp.max(g_r, axis=1, keepdims=True))
        a_c.append(a_r)
        m_c.append(m_prev)
        decay_c.append(jnp.exp(bl + m_prev - m_new))
        wg_c.append(jnp.exp(g_r - m_new))
        m_prev = m_new
    m_ref[...] = m_prev

    units = [(c, h) for c in range(nck) for h in range(B_HEADS)]
    lmat = jnp.concatenate([jnp.where(causal, lf_c[c][h:h + 1, :], 0.0) for c, h in units], axis=0)
    dgm = jnp.concatenate([jnp.where(diag, wg_c[c][h:h + 1, :], 0.0) for c, h in units], axis=0)
    b_t_all = _split_dot(lmat, ones_blk)
    wg_t_all = _dot(dgm.astype(BF16), ones_blk)

    caugs = [ct_ref[h] for h in range(B_HEADS)]
    h_chunks = []
    for c in range(nck):
        sl = slice(c * CHUNK, (c + 1) * CHUNK)
        heads = []
        for h in range(B_HEADS):
            u = c * B_HEADS + h
            us = slice(u * CHUNK, (u + 1) * CHUNK)
            hs = slice(h * dh, (h + 1) * dh)
            qh, kh = q_b[sl, hs], k_b[sl, hs]
            vaug = jnp.concatenate([v_b[sl, hs], ones_blk], axis=1)
            m_row = m_c[c][h:h + 1, :]
            amat = jnp.where(causal, a_c[c][h:h + 1, :], NEG)
            mx = jnp.maximum(jnp.max(amat, axis=1, keepdims=True), m_row)
            w_intra = jnp.exp(amat - mx)
            w_state = jnp.exp(m_row - mx)
            s = _dot_nt(qh, kh) * w_intra
            kw = (kh.astype(F32) * wg_t_all[us]).astype(BF16)
            both = _dot(jnp.concatenate([s.astype(BF16), kw.T], axis=0), vaug)
            caug = caugs[h]
            naug = both[:CHUNK] + jnp.concatenate([w_state, w_state], axis=1) * _dot(qh, caug.astype(BF16))
            num, nq = naug[:, :dh], naug[:, dh:]
            hv = num / jnp.maximum(jnp.abs(nq), jnp.exp(-(b_t_all[us] + mx)))
            decay = decay_c[c][h:h + 1, :]
            caugs[h] = jnp.concatenate([decay, decay], axis=1) * caug + both[CHUNK:]
            hc = hv - jnp.mean(hv, axis=1, keepdims=True)
            hn = hc * lax.rsqrt(jnp.mean(hc * hc, axis=1, keepdims=True) + LN_EPS)
            heads.append(hn)
        h_chunks.append(jnp.concatenate(heads, axis=1))
    for h in range(B_HEADS):
        ct_ref[h] = caugs[h]
    hn_all = jnp.concatenate(h_chunks, axis=0) if nck > 1 else h_chunks[0]
    y_b = (hn_all * norm_w_ref[...] + skip_ref[...] * xc) * gate_z

    mix = jnp.concatenate([y_a, y_b], axis=1).astype(BF16)
    y = _dot(mix, w_out_ref[...])
    o_ref[0] = _ln(DN_ALPHA * x + y, ln_g_ref[...], ln_b_ref[...])


def _block_diag_pairs(w):
    hh, d, _ = w.shape
    wp = w.reshape(hh // 2, 2, d, d)
    eye = jnp.eye(2, dtype=w.dtype)
    return jnp.einsum('pade,ab->padbe', wp, eye).reshape(hh // 2, 2 * d, 2 * d)


def _dot_head_pairs(x, w_ref):
    npair, w2, _ = w_ref.shape
    return jnp.concatenate([_dot(x[:, p * w2:(p + 1) * w2], w_ref[p]) for p in range(npair)], axis=1)


def even_mixer_layer(x, w_in, gm_ln_g, gm_ln_b, gm_ws, gm_bs, conv_w, conv_b, wq, wk, wv, w_if,
                     b_if, norm_w, skip, w_out, ln_g, ln_b, *, ts):
    bsz, seq, d = x.shape
    aw = gm_ln_g.shape[0]
    bw = conv_b.shape[0]
    agd = aw // A_GROUPS
    causal = jnp.tril(jnp.ones((CHUNK, CHUNK), dtype=bool))
    gm_w = jnp.where(causal[None], gm_ws, 0.0).astype(BF16)
    gm_bias = jnp.repeat(gm_bs.T, agd, axis=1)
    row = lambda a: a.reshape(1, -1)
    args = (x, w_in.astype(BF16), row(gm_ln_g), row(gm_ln_b), gm_w, gm_bias,
            conv_w, row(conv_b), _block_diag_pairs(wq).astype(BF16), _block_diag_pairs(wk).astype(BF16),
            _block_diag_pairs(wv).astype(BF16), w_if.T.astype(BF16),
            b_if.reshape(-1, 1), row(norm_w), row(skip), w_out.astype(BF16),
            row(ln_g), row(ln_b))
    in_specs = [pl.BlockSpec((1, ts, d), lambda b, j: (b, j, 0))] + [_full(a.shape) for a in args[1:]]
    dh = bw // B_HEADS
    assert dh == LANES and CHUNK == LANES
    return pl.pallas_call(
        functools.partial(_even_kernel, ts=ts, aw=aw, bw=bw),
        grid=(bsz, seq // ts),
        in_specs=in_specs,
        out_specs=pl.BlockSpec((1, ts, d), lambda b, j: (b, j, 0)),
        out_shape=jax.ShapeDtypeStruct((bsz, seq, d), F32),
        scratch_shapes=[pltpu.VMEM((8 + ts, bw), F32),
                        pltpu.VMEM((B_HEADS, dh, dh + LANES), F32),
                        pltpu.VMEM((B_HEADS, LANES), F32)],
        compiler_params=_cparams(("arbitrary", "arbitrary")),
        name="even_mixer",
    )(*args)


def _memfold_kernel(mem_ref, wkv_ref, wq_ref, wo_ref, wqk_ref, vo_ref, *, d):
    dh = d // X_HEADS
    m_len = mem_ref.shape[1]
    kv = _dot(mem_ref[0].astype(BF16), wkv_ref[...])
    k = (kv[:, :d] * (dh ** -0.5 * LOG2E)).astype(BF16)
    v = kv[:, d:].astype(BF16)
    for h in range(X_HEADS):
        hs = slice(h * dh, (h + 1) * dh)
        ms = slice(h * m_len, (h + 1) * m_len)
        wqk_ref[0, :, ms] = _dot_nt(wq_ref[:, hs], k[:, hs]).astype(BF16)
        vo_ref[0, ms, :] = _dot(v[:, hs], wo_ref[hs, :]).astype(BF16)


def memory_fold(mem, wkv, wq, wo):
    bsz, m_len, d = mem.shape
    hm = X_HEADS * m_len
    args = (mem, wkv.astype(BF16), wq.astype(BF16), wo.astype(BF16))
    return pl.pallas_call(
        functools.partial(_memfold_kernel, d=d),
        grid=(bsz,),
        in_specs=[pl.BlockSpec((1, m_len, d), lambda b: (b, 0, 0))] + [_full(a.shape) for a in args[1:]],
        out_specs=[pl.BlockSpec((1, d, hm), lambda b: (b, 0, 0)),
                   pl.BlockSpec((1, hm, d), lambda b: (b, 0, 0))],
        out_shape=[jax.ShapeDtypeStruct((bsz, d, hm), BF16),
                   jax.ShapeDtypeStruct((bsz, hm, d), BF16)],
        compiler_params=_cparams(("arbitrary",)),
        name="memory_fold",
    )(*args)


ROUTE_W = 128
XATTN_TS = 1024
DEST_ROWS = 8
SLOT_TB = 2048
SLOT_SUB = 512


def _xattn_kernel(x_ref, wqk_ref, vo_ref, ln_g_ref, ln_b_ref, wr_ref, br_ref,
                  o_ref, xp_ref, route_ref, cnt_ref):
    m_len = wqk_ref.shape[2] // X_HEADS

    @pl.when(pl.program_id(1) == 0)
    def _():
        cnt_ref[...] = jnp.zeros_like(cnt_ref)

    x = x_ref[0]
    ts = x.shape[0]
    xb = x.astype(BF16)
    probs = []
    for h in range(X_HEADS):
        s = _dot(xb, wqk_ref[0, :, h * m_len:(h + 1) * m_len])
        p = jnp.exp2(s - jnp.max(s, axis=1, keepdims=True))
        probs.append((p / jnp.sum(p, axis=1, keepdims=True)).astype(BF16))
    y = _dot(jnp.concatenate(probs, axis=1), vo_ref[0])
    x2 = _ln(DN_ALPHA * x + y, ln_g_ref[...], ln_b_ref[...])
    o_ref[0] = x2
    xp_ref[0] = _pack_halves(x2)

    x_hi = x2.astype(BF16)
    x_lo = (x2 - x_hi.astype(F32)).astype(BF16)
    logits = (_dot(x_hi, wr_ref[0]) + _dot(x_lo, wr_ref[0]) + _dot(x_hi, wr_ref[1])) + br_ref[...]
    lane = lax.broadcasted_iota(jnp.int32, (ts, ROUTE_W), 1)
    is_g = lane < N_GROUPS
    lg = jnp.where(is_g, logits, NEG)
    mg = jnp.max(lg, axis=1, keepdims=True)
    gi = jnp.min(jnp.where(jnp.logical_and(is_g, lg == mg), lane, ROUTE_W), axis=1, keepdims=True)
    gate_g = 1.0 / jnp.sum(jnp.where(is_g, jnp.exp(lg - mg), 0.0), axis=1, keepdims=True)
    lo = N_GROUPS + gi * EXPERTS_PER_GROUP
    in_grp = jnp.logical_and(lane >= lo, lane < lo + EXPERTS_PER_GROUP)
    le = jnp.where(in_grp, logits, NEG)
    v1 = jnp.max(le, axis=1, keepdims=True)
    i1 = jnp.min(jnp.where(jnp.logical_and(in_grp, le == v1), lane, ROUTE_W), axis=1, keepdims=True)
    le2 = jnp.where(lane == i1, NEG, le)
    v2 = jnp.max(le2, axis=1, keepdims=True)
    i2 = jnp.min(jnp.where(jnp.logical_and(in_grp, le2 == v2), lane, ROUTE_W), axis=1, keepdims=True)
    e21 = jnp.exp(v2 - v1)
    p1 = 1.0 / (1.0 + e21)
    p2 = e21 * p1
    e1 = (i1 - N_GROUPS).astype(F32)
    e2 = (i2 - N_GROUPS).astype(F32)
    rec = jnp.where(lane == 0, e1, 0.0)
    rec = jnp.where(lane == 1, e2, rec)
    rec = jnp.where(lane == 2, gate_g * p1, rec)
    rec = jnp.where(lane == 3, gate_g * p2, rec)
    route_ref[0] = rec
    sel = jnp.logical_or(lane == i1 - N_GROUPS, lane == i2 - N_GROUPS)
    cnt_ref[0] += jnp.sum(jnp.where(sel, 1.0, 0.0), axis=0, keepdims=True)


def xattn_router_layer(x, wqk, vo, ln_g, ln_b, w_rg, b_rg, w_re, b_re, *, ts):
    bsz, seq, d = x.shape
    hm = wqk.shape[2]
    wr = jnp.zeros((d, ROUTE_W), F32).at[:, :N_GROUPS].set(w_rg).at[:, N_GROUPS:N_GROUPS + N_EXPERTS].set(w_re)
    br = jnp.zeros((1, ROUTE_W), F32).at[0, :N_GROUPS].set(b_rg).at[0, N_GROUPS:N_GROUPS + N_EXPERTS].set(b_re)
    wr_hi = wr.astype(BF16)
    wr_lo = (wr - wr_hi.astype(F32)).astype(BF16)
    wr = jnp.stack([wr_hi, wr_lo])
    args = (x, wqk, vo, ln_g.reshape(1, -1), ln_b.reshape(1, -1), wr, br)
    in_specs = [pl.BlockSpec((1, ts, d), lambda b, j: (b, j, 0)),
                pl.BlockSpec((1, d, hm), lambda b, j: (b, 0, 0)),
                pl.BlockSpec((1, hm, d), lambda b, j: (b, 0, 0))] + [_full(a.shape) for a in args[3:]]
    return pl.pallas_call(
        _xattn_kernel,
        grid=(bsz, seq // ts),
        in_specs=in_specs,
        out_specs=[pl.BlockSpec((1, ts, d), lambda b, j: (b, j, 0)),
                   pl.BlockSpec((1, ts, d // 2), lambda b, j: (b, j, 0)),
                   pl.BlockSpec((1, ts, ROUTE_W), lambda b, j: (b, j, 0)),
                   pl.BlockSpec((1, 1, ROUTE_W), lambda b, j: (b, 0, 0))],
        out_shape=[jax.ShapeDtypeStruct((bsz, seq, d), F32),
                   jax.ShapeDtypeStruct((bsz, seq, d // 2), jnp.int32),
                   jax.ShapeDtypeStruct((bsz, seq, ROUTE_W), F32),
                   jax.ShapeDtypeStruct((bsz, 1, ROUTE_W), F32)],
        compiler_params=_cparams(("arbitrary", "arbitrary")),
        name="xattn_router",
    )(*args)


def _moe_output(x2, g0, g1, rec, ln_g, ln_b):
    a_lo, a_hi = _unpack_halves(g0)
    b_lo, b_hi = _unpack_halves(g1)
    w0, w1 = rec[:, 2:3], rec[:, 3:4]
    y = jnp.concatenate([w0 * a_lo + w1 * b_lo, w0 * a_hi + w1 * b_hi], axis=1)
    return _ln(DN_ALPHA * x2 + y, ln_g, ln_b)


def _swa_kernel(x_ref, g0_ref, g1_ref, route_ref, pln_g_ref, pln_b_ref,
                wqkv_ref, bqkv_ref, cos_ref, sin_ref, sink_ref, wo_ref, ln_g_ref, ln_b_ref,
                *rest, ts, cq, ckv):
    o_ref, kprev, vprev = rest[-3:]
    j = pl.program_id(1)
    nb = ts // CHUNK
    dh = C_HEAD_DIM
    grp = (cq // dh) // C_KV_HEADS

    @pl.when(j == 0)
    def _():
        kprev[...] = jnp.zeros_like(kprev)
        vprev[...] = jnp.zeros_like(vprev)

    x = _moe_output(x_ref[0], g0_ref[0, 0], g1_ref[0, 0], route_ref[0], pln_g_ref[...], pln_b_ref[...])
    qkv = _dot(x.astype(BF16), wqkv_ref[...]) + bqkv_ref[...]
    cos = cos_ref[...]
    sin = sin_ref[...]
    lane = lax.broadcasted_iota(jnp.int32, (ts, LANES), 1)
    first_half = (lane % dh) < (dh // 2)

    def rope(t):
        outs = []
        for c in range(t.shape[1] // LANES):
            tc = t[:, c * LANES:(c + 1) * LANES]
            rot = jnp.where(first_half, pltpu.roll(tc, LANES - dh // 2, 1), pltpu.roll(tc, dh // 2, 1))
            outs.append(tc * cos + rot * sin)
        return jnp.concatenate(outs, axis=1)

    q = rope(qkv[:, :cq]).astype(BF16)
    k = rope(qkv[:, cq:cq + 2 * ckv]).astype(BF16)
    v = qkv[:, cq + 2 * ckv:].astype(BF16)

    r_i = lax.broadcasted_iota(jnp.int32, (CHUNK, 2 * CHUNK), 0)
    c_i = lax.broadcasted_iota(jnp.int32, (CHUNK, 2 * CHUNK), 1)
    band = jnp.logical_and(c_i > r_i, c_i <= r_i + CHUNK)
    sink_col = c_i == 0
    lane_k = lax.broadcasted_iota(jnp.int32, (2 * CHUNK, LANES), 1)
    key_row = lax.broadcasted_iota(jnp.int32, (2 * CHUNK, LANES), 0)
    lane_q = lax.broadcasted_iota(jnp.int32, (CHUNK, LANES), 1)
    ones_blk = jnp.ones((2 * CHUNK, LANES), BF16)
    zero_b = jnp.zeros((), BF16)
    blocks = []
    for c in range(nb):
        sl = slice(c * CHUNK, (c + 1) * CHUNK)
        if c == 0:
            kb = jnp.concatenate([kprev[...].astype(BF16), k[sl]], axis=0)
            vb = jnp.concatenate([vprev[...].astype(BF16), v[sl]], axis=0)
            first_key = jnp.where(j > 0, 0, CHUNK)
            valid = jnp.logical_and(band, c_i >= first_key)
        else:
            kb = k[(c - 1) * CHUNK:(c + 1) * CHUNK]
            vb = v[(c - 1) * CHUNK:(c + 1) * CHUNK]
            valid = band
        tiles = []
        for h in range(C_KV_HEADS):
            kd = kb[:, h * LANES:(h + 1) * LANES]
            vd = vb[:, h * LANES:(h + 1) * LANES]
            k_lo = jnp.where(lane_k < dh, kd, zero_b)
            k_hi = jnp.where(lane_k >= dh, kd, zero_b)
            vz = jnp.where(key_row == 0, zero_b, vd)
            q2 = jnp.concatenate([q[sl, (2 * h) * LANES:(2 * h + 1) * LANES],
                                  q[sl, (2 * h + 1) * LANES:(2 * h + 2) * LANES]], axis=0)
            s_lo = _dot_nt(q2, k_lo)
            s_hi = _dot_nt(q2, k_hi)
            parts = []
            for qk, g in ((s_lo[:CHUNK], 0), (s_lo[CHUNK:], 2), (s_hi[:CHUNK], 1), (s_hi[CHUNK:], 3)):
                sink = sink_ref[h * grp + g]
                parts.append(jnp.where(valid, qk, jnp.where(sink_col, sink, NEG)))
            s = jnp.concatenate(parts, axis=0)
            p = jnp.exp2(s - jnp.max(s, axis=1, keepdims=True)).astype(BF16)
            den = _dot(p, ones_blk)
            o2 = _dot(p, vz) / den
            tiles.append(jnp.where(lane_q < dh, o2[:CHUNK], o2[2 * CHUNK:3 * CHUNK]))
            tiles.append(jnp.where(lane_q < dh, o2[CHUNK:2 * CHUNK], o2[3 * CHUNK:]))
        blocks.append(jnp.concatenate(tiles, axis=1))
    kprev[...] = k[(nb - 1) * CHUNK:].astype(F32)
    vprev[...] = v[(nb - 1) * CHUNK:].astype(F32)
    att = (jnp.concatenate(blocks, axis=0) if nb > 1 else blocks[0]).astype(BF16)
    y = _dot(att, wo_ref[...])
    o_ref[0] = _ln(DN_ALPHA * x + y, ln_g_ref[...], ln_b_ref[...])


def swa_prepare(w_qkv, b_qkv, sinks, w_o, seq):
    cq = w_o.shape[0]
    ckv = (w_qkv.shape[1] - cq) // 2
    dh = C_HEAD_DIM
    inv = ROPE_THETA ** (-jnp.arange(0, dh, 2, dtype=F32) / dh)
    ang = jnp.arange(seq, dtype=F32)[:, None] * inv[None, :]
    reps = LANES // (dh // 2)
    sign = jnp.tile(jnp.concatenate([-jnp.ones((dh // 2,), F32), jnp.ones((dh // 2,), F32)]), LANES // dh)
    cos_t = jnp.tile(jnp.cos(ang), (1, reps))
    sin_t = jnp.tile(jnp.sin(ang), (1, reps)) * sign[None, :]
    assert 2 * dh == LANES

    def dup_heads(t):
        th = t.reshape(t.shape[:-1] + (ckv // dh, dh))
        return jnp.concatenate([th, th], axis=-1).reshape(t.shape[:-1] + (2 * ckv,))

    qs = dh ** -0.5 * LOG2E
    w_all = jnp.concatenate([w_qkv[:, :cq] * qs, dup_heads(w_qkv[:, cq:cq + ckv]),
                             dup_heads(w_qkv[:, cq + ckv:])], axis=1)
    b_all = jnp.concatenate([b_qkv[:cq] * qs, dup_heads(b_qkv[cq:cq + ckv]), dup_heads(b_qkv[cq + ckv:])])
    return w_all.astype(BF16), b_all.reshape(1, -1), cos_t, sin_t, sinks.astype(F32) * LOG2E, w_o.astype(BF16)


def swa_mixer_layer(pending, prepared, ln_g, ln_b, *, ts):
    x, gs, route, pln_g, pln_b = pending
    bsz, seq, d = x.shape
    cq = prepared[5].shape[0]
    ckv = (prepared[0].shape[1] - cq) // 4
    shared = (pln_g.reshape(1, -1), pln_b.reshape(1, -1), *prepared, ln_g.reshape(1, -1), ln_b.reshape(1, -1))
    out = None
    b_next = 0
    for g in gs:
        bp = g.shape[1] // seq
        b0, b_next = b_next, b_next + bp
        g = g.reshape(2, bp, seq, d // 2)
        args = [x, g, g, route, *shared]
        in_specs = [pl.BlockSpec((1, ts, d), lambda b, j: (b0 + b, j, 0)),
                    pl.BlockSpec((1, 1, ts, d // 2), lambda b, j: (0, b, j, 0)),
                    pl.BlockSpec((1, 1, ts, d // 2), lambda b, j: (1, b, j, 0)),
                    pl.BlockSpec((1, ts, ROUTE_W), lambda b, j: (b0 + b, j, 0)),
                    _full((1, d)), _full((1, d)),
                    _full(shared[2].shape), _full(shared[3].shape),
                    pl.BlockSpec((ts, LANES), lambda b, j: (j, 0)),
                    pl.BlockSpec((ts, LANES), lambda b, j: (j, 0)),
                    pl.BlockSpec(memory_space=pltpu.SMEM),
                    _full(shared[7].shape), _full(shared[8].shape), _full(shared[9].shape)]
        aliases = {}
        if out is not None:
            args.append(out)
            in_specs.append(pl.BlockSpec(memory_space=pl.ANY))
            aliases = {len(args) - 1: 0}
        out = pl.pallas_call(
            functools.partial(_swa_kernel, ts=ts, cq=cq, ckv=ckv),
            grid=(bp, seq // ts),
            in_specs=in_specs,
            out_specs=pl.BlockSpec((1, ts, d), lambda b, j: (b0 + b, j, 0)),
            out_shape=jax.ShapeDtypeStruct((bsz, seq, d), F32),
            input_output_aliases=aliases,
            scratch_shapes=[pltpu.VMEM((CHUNK, 2 * ckv), F32), pltpu.VMEM((CHUNK, 2 * ckv), F32)],
            compiler_params=_cparams(("arbitrary", "arbitrary")),
            name="swa_mixer",
        )(*args)
    return out


def _slot_kernel(route_ref, pstart_ref, dest_ref, carry_ref, *, tb):
    @pl.when(pl.program_id(0) == 0)
    def _():
        carry_ref[...] = pstart_ref[...]

    sub = min(SLOT_SUB, tb)
    lane = lax.broadcasted_iota(jnp.int32, (sub, ROUTE_W), 1)
    r = lax.broadcasted_iota(jnp.int32, (sub, sub), 0)
    c = lax.broadcasted_iota(jnp.int32, (sub, sub), 1)
    before = jnp.where(c < r, 1.0, 0.0).astype(BF16)
    carry = carry_ref[...]
    for k in range(tb // sub):
        rows = slice(k * sub, (k + 1) * sub)
        rec = route_ref[rows, :]
        e0 = rec[:, 0:1].astype(jnp.int32)
        e1 = rec[:, 1:2].astype(jnp.int32)
        oh0 = lane == e0
        oh1 = lane == e1
        ohs = jnp.where(jnp.logical_or(oh0, oh1), 1.0, 0.0)
        prefix = _dot(before, ohs.astype(BF16)) + carry
        d0 = jnp.sum(jnp.where(oh0, prefix, 0.0), axis=1, keepdims=True)
        d1 = jnp.sum(jnp.where(oh1, prefix, 0.0), axis=1, keepdims=True)
        dest = jnp.where(lane == 0, d0, jnp.where(lane == 1, d1, 0.0))
        dest_ref[:, rows] = dest.T[:DEST_ROWS].astype(jnp.int32)
        carry = carry + jnp.sum(ohs, axis=0, keepdims=True)
    carry_ref[...] = carry


def moe_slots(route, pstart, *, tb):
    n = route.shape[0]
    return pl.pallas_call(
        functools.partial(_slot_kernel, tb=tb),
        grid=(n // tb,),
        in_specs=[pl.BlockSpec((tb, ROUTE_W), lambda i: (i, 0)), _full((1, ROUTE_W))],
        out_specs=pl.BlockSpec((DEST_ROWS, tb), lambda i: (0, i)),
        out_shape=jax.ShapeDtypeStruct((DEST_ROWS, n), jnp.int32),
        scratch_shapes=[pltpu.VMEM((1, ROUTE_W), F32)],
        compiler_params=_cparams(("arbitrary",)),
        name="moe_slots",
    )(route, pstart)


def _ffn_kernel(blk_exp_ref, new_exp_ref, nblk_ref, xs_ref, w1_ref, w3_ref, w2_ref, ys_ref,
                w1_b, w3_b, w2_b):
    i = pl.program_id(0)
    used = i < nblk_ref[0]

    @pl.when(jnp.logical_and(used, new_exp_ref[i] == 1))
    def _():
        w1_b[...] = w1_ref[0, 0].astype(BF16)
        w3_b[...] = w3_ref[0, 0].astype(BF16)
        w2_b[...] = w2_ref[0, 0].astype(BF16)

    @pl.when(used)
    def _():
        x_lo, x_hi = _unpack_halves(xs_ref[...])
        x_lo, x_hi = x_lo.astype(BF16), x_hi.astype(BF16)
        dl = x_lo.shape[1]
        h1 = _dot(x_lo, w1_b[:dl, :]) + _dot(x_hi, w1_b[dl:, :])
        h3 = _dot(x_lo, w3_b[:dl, :]) + _dot(x_hi, w3_b[dl:, :])
        h = (_silu(h1) * h3).astype(BF16)
        ys_ref[...] = _pack_halves(_dot(h, w2_b[...]))

    @pl.when(jnp.logical_not(used))
    def _():
        ys_ref[...] = jnp.zeros_like(ys_ref)


def moe_ffn(xs, blk_exp, new_exp, nblk, w1, w3, w2, *, layer, bm):
    n_pad, dp = xs.shape
    d = 2 * dp
    de = w1.shape[3]
    n_blk = n_pad // bm

    def x_map(i, be, ne, nb):
        return (jnp.minimum(i, nb[0] - 1), 0)

    def w_map(i, be, ne, nb):
        return (layer, be[i], 0, 0)

    return pl.pallas_call(
        _ffn_kernel,
        grid_spec=pltpu.PrefetchScalarGridSpec(
            num_scalar_prefetch=3,
            grid=(n_blk,),
            in_specs=[pl.BlockSpec((bm, dp), x_map),
                      pl.BlockSpec((1, 1, d, de), w_map),
                      pl.BlockSpec((1, 1, d, de), w_map),
                      pl.BlockSpec((1, 1, de, d), w_map)],
            out_specs=pl.BlockSpec((bm, dp), lambda i, be, ne, nb: (i, 0)),
            scratch_shapes=[pltpu.VMEM((d, de), BF16), pltpu.VMEM((d, de), BF16),
                            pltpu.VMEM((de, d), BF16)]),
        out_shape=jax.ShapeDtypeStruct((n_pad, dp), jnp.int32),
        compiler_params=_cparams(("arbitrary",)),
        name="moe_ffn",
    )(blk_exp, new_exp, nblk, xs, w1, w3, w2)


def _combine_kernel(x_ref, g0_ref, g1_ref, route_ref, ln_g_ref, ln_b_ref, *rest):
    o_ref = rest[-1]
    o_ref[...] = _moe_output(x_ref[...], g0_ref[0], g1_ref[0], route_ref[...], ln_g_ref[...], ln_b_ref[...])


def moe_combine(x, g, route, ln_g, ln_b, *, row0, tb, prev=None):
    n, d = x.shape
    m = g.shape[1]
    blk0 = row0 // tb
    row_spec = pl.BlockSpec((tb, d), lambda i: (blk0 + i, 0))
    args = [x, g, g, route, ln_g.reshape(1, -1), ln_b.reshape(1, -1)]
    in_specs = [row_spec,
                pl.BlockSpec((1, tb, d // 2), lambda i: (0, i, 0)),
                pl.BlockSpec((1, tb, d // 2), lambda i: (1, i, 0)),
                pl.BlockSpec((tb, ROUTE_W), lambda i: (blk0 + i, 0)),
                _full((1, d)), _full((1, d))]
    aliases = {}
    if prev is not None:
        args.append(prev)
        in_specs.append(pl.BlockSpec(memory_space=pl.ANY))
        aliases = {len(args) - 1: 0}
    return pl.pallas_call(
        _combine_kernel,
        grid=(m // tb,),
        in_specs=in_specs,
        out_specs=row_spec,
        out_shape=jax.ShapeDtypeStruct((n, d), F32),
        input_output_aliases=aliases,
        compiler_params=_cparams(("arbitrary",)),
        name="moe_combine",
    )(*args)


MOE_BM = 1024
COMBINE_PARTS = 4
COMBINE_TB = 1024
MIXER_FIRST_DIV = 4
MIXER_TS = 1024


def hierarchical_moe_layer(x2, xp, route, counts, w1, w3, w2, ln_g, ln_b, *, layer, defer_combine,
                           side_work=None, side_inputs=None):
    bsz, seq, d = x2.shape
    n = bsz * seq
    bm = MOE_BM
    rt = route.reshape(n, ROUTE_W)
    tb = min(512, n)
    n_blk = (2 * n) // bm + N_EXPERTS
    cnt = counts.sum(axis=0)[0, :N_EXPERTS].astype(jnp.int32)
    pcnt = (cnt + bm - 1) // bm * bm
    pends = jnp.cumsum(pcnt)
    pstart = pends - pcnt
    nblk = (pends[-1] // bm).astype(jnp.int32).reshape(1)
    blk_row = jnp.arange(n_blk, dtype=jnp.int32) * bm
    blk_exp = jnp.minimum(jnp.sum((pends[None, :] <= blk_row[:, None]).astype(jnp.int32), axis=1),
                          N_EXPERTS - 1)
    last_exp = blk_exp[jnp.maximum(nblk[0] - 1, 0)]
    blk_exp = jnp.where(jnp.arange(n_blk) < nblk[0], blk_exp, last_exp)
    new_exp = jnp.concatenate([jnp.ones((1,), jnp.int32),
                               (blk_exp[1:] != blk_exp[:-1]).astype(jnp.int32)])
    pstart_rec = jnp.zeros((1, ROUTE_W), F32).at[0, :N_EXPERTS].set(pstart.astype(F32))
    dest = moe_slots(rt, pstart_rec, tb=min(SLOT_TB, n))
    xs = moe_dispatch(xp.reshape(n, d // 2), dest[0], dest[1], n_blk * bm)
    side = None
    if side_work is not None:
        side_inputs, _ = lax.optimization_barrier((side_inputs, dest))
        nblk, side = lax.optimization_barrier((nblk, side_work(*side_inputs)))
    ys = moe_ffn(xs, blk_exp, new_exp, nblk, w1, w3, w2, layer=layer, bm=bm)
    if defer_combine:
        b_first = bsz // MIXER_FIRST_DIV
        cuts = [0, b_first * seq, n] if 0 < b_first < bsz else [0, n]
        gs = [moe_gather(ys, dest[0, lo:hi], dest[1, lo:hi]) for lo, hi in zip(cuts[:-1], cuts[1:])]
        return (x2, gs, route, ln_g, ln_b), side
    parts = COMBINE_PARTS if n % (COMBINE_PARTS * tb * 8) == 0 else 1
    m = n // parts
    out = None
    for p in range(parts):
        g = moe_gather(ys, dest[0, p * m:(p + 1) * m], dest[1, p * m:(p + 1) * m])
        out = moe_combine(x2.reshape(n, d), g, rt, ln_g, ln_b, row0=p * m, tb=min(COMBINE_TB, m), prev=out)
    return out.reshape(bsz, seq, d)


SC_ROWS = 128


def _sc_mesh():
    return plsc.VectorSubcoreMesh(core_axis_name="c", subcore_axis_name="s")


def moe_dispatch(xf, dest0, dest1, n_pad):
    n, d = xf.shape
    info = plsc.get_sparse_core_info()
    nw = info.num_cores * info.num_subcores
    per_w = n // nw
    r = min(SC_ROWS, per_w)

    def body(x_hbm, d0_hbm, d1_hbm, xs_hbm, i0_v, i1_v, rows_v, sem):
        wid = lax.axis_index("s") * info.num_cores + lax.axis_index("c")

        @pl.loop(0, per_w // r)
        def _(c):
            base = pl.multiple_of(wid * per_w + c * r, 8)
            pltpu.sync_copy(d0_hbm.at[pl.ds(base, r)], i0_v)
            pltpu.sync_copy(d1_hbm.at[pl.ds(base, r)], i1_v)
            pltpu.sync_copy(x_hbm.at[pl.ds(base, r)], rows_v)
            pltpu.async_copy(rows_v, xs_hbm.at[i0_v], sem).wait()
            pltpu.async_copy(rows_v, xs_hbm.at[i1_v], sem).wait()

    return pl.kernel(
        body, out_type=jax.ShapeDtypeStruct((n_pad, d), xf.dtype), mesh=_sc_mesh(),
        scratch_types=[pltpu.VMEM((r,), jnp.int32), pltpu.VMEM((r,), jnp.int32),
                       pltpu.VMEM((r, d), xf.dtype), pltpu.SemaphoreType.DMA],
        name="moe_dispatch",
    )(xf, dest0, dest1)


def moe_gather(ys, dest0, dest1):
    n = dest0.shape[0]
    d = ys.shape[1]
    info = plsc.get_sparse_core_info()
    nw = info.num_cores * info.num_subcores
    per_w = n // nw
    r = min(SC_ROWS, per_w)

    def body(ys_hbm, d0_hbm, d1_hbm, g_hbm, i_v, rows_v, sem):
        wid = lax.axis_index("s") * info.num_cores + lax.axis_index("c")

        @pl.loop(0, per_w // r)
        def _(c):
            base = pl.multiple_of(wid * per_w + c * r, 8)
            for k, d_hbm in enumerate((d0_hbm, d1_hbm)):
                pltpu.sync_copy(d_hbm.at[pl.ds(base, r)], i_v)
                pltpu.async_copy(ys_hbm.at[i_v], rows_v, sem).wait()
                pltpu.sync_copy(rows_v, g_hbm.at[k, pl.ds(base, r)])

    return pl.kernel(
        body, out_type=jax.ShapeDtypeStruct((2, n, d), ys.dtype), mesh=_sc_mesh(),
        scratch_types=[pltpu.VMEM((r,), jnp.int32), pltpu.VMEM((r, d), ys.dtype),
                       pltpu.SemaphoreType.DMA],
        name="moe_gather",
    )(ys, dest0, dest1)


def kernel(x, mem, ln_g, ln_b, ev_w_in, ev_gm_ln_g, ev_gm_ln_b, ev_gm_ws, ev_gm_bs, ev_conv_w, ev_conv_b, ev_wq, ev_wk, ev_wv, ev_w_if, ev_b_if, ev_norm_w, ev_skip, ev_w_out, od_w_qkv, od_b_qkv, od_sinks, od_w_o, xa_wq, xa_wkv, xa_wo, moe_w_rg, moe_b_rg, moe_w_re, moe_b_re, moe_w1, moe_w3, moe_w2):
    bsz, seq, d = x.shape
    depth = ln_g.shape[0]
    assert depth == DEPTH
    ts = min(MIXER_TS, seq)

    folded = None
    for l in range(depth):
        wqk, vo = folded if folded is not None else memory_fold(mem, xa_wkv[l], xa_wq[l], xa_wo[l])
        if l % 2 == 0:
            e = l // 2
            x = even_mixer_layer(x, ev_w_in[e], ev_gm_ln_g[e], ev_gm_ln_b[e], ev_gm_ws[e], ev_gm_bs[e],
                                 ev_conv_w[e], ev_conv_b[e], ev_wq[e], ev_wk[e], ev_wv[e], ev_w_if[e],
                                 ev_b_if[e], ev_norm_w[e], ev_skip[e], ev_w_out[e],
                                 ln_g[l, 0], ln_b[l, 0], ts=ts)
        else:
            o = l // 2
            prepared = swa_prepare(od_w_qkv[o], od_b_qkv[o], od_sinks[o], od_w_o[o], seq)
            x = swa_mixer_layer(x, prepared, ln_g[l, 0], ln_b[l, 0], ts=ts)
        x, xp, route, counts = xattn_router_layer(x, wqk, vo, ln_g[l, 1], ln_b[l, 1],
                                                  moe_w_rg[l], moe_b_rg[l], moe_w_re[l], moe_b_re[l],
                                                  ts=min(XATTN_TS, seq))
        defer = l + 1 < depth and (l + 1) % 2 == 1
        side_inputs = (mem, xa_wkv[l + 1], xa_wq[l + 1], xa_wo[l + 1]) if defer else None
        x = hierarchical_moe_layer(x, xp, route, counts, moe_w1, moe_w3, moe_w2,
                                   ln_g[l, 2], ln_b[l, 2], layer=l, defer_combine=defer,
                                   side_work=memory_fold if defer else None, side_inputs=side_inputs)
        folded = None
        if defer:
            x, folded = x
    return x
```

```python
import functools
import math

import jax
import jax.numpy as jnp
from jax import lax
from jax.experimental import pallas as pl
from jax.experimental.pallas import tpu as pltpu
from jax.experimental.pallas import tpu_sc as plsc

F32 = jnp.float32
BF16 = jnp.bfloat16

A_GROUPS = 4
CHUNK = 128
B_HEADS = 4
B_CONV = 4
C_HEAD_DIM = 64
C_KV_HEADS = 4
X_HEADS = 4
N_GROUPS = 4
EXPERTS_PER_GROUP = 8
N_EXPERTS = N_GROUPS * EXPERTS_PER_GROUP
ROPE_THETA = 10000.0
LN_EPS = 1e-5
DEPTH = 2
DN_ALPHA = (2 * DEPTH) ** 0.25

LANES = 128
VMEM_LIMIT = 48 * 1024 * 1024
NEG = -1e30


def _cparams(sem):
    return pltpu.CompilerParams(dimension_semantics=sem, vmem_limit_bytes=VMEM_LIMIT)


def _full(shape):
    nd = len(shape)
    return pl.BlockSpec(shape, lambda *_: (0,) * nd)


def _dot(a, b):
    return jnp.dot(a, b, preferred_element_type=F32)


def _dot_nt(a, b):
    return lax.dot_general(a, b, (((1,), (1,)), ((), ())), preferred_element_type=F32)


def _split_dot(a, b_bf16):
    hi = a.astype(BF16)
    lo = (a - hi.astype(F32)).astype(BF16)
    return _dot(hi, b_bf16) + _dot(lo, b_bf16)


def _ln(x, g, b):
    mu = jnp.mean(x, axis=-1, keepdims=True)
    xc = x - mu
    var = jnp.mean(xc * xc, axis=-1, keepdims=True)
    return xc * lax.rsqrt(var + LN_EPS) * g + b


LOG2E = math.log2(math.e)


def _silu(x):
    return x * (1.0 / (1.0 + jnp.exp2(x * -LOG2E)))


def _gelu(x):
    return 0.5 * x * (1.0 + jnp.tanh(math.sqrt(2.0 / math.pi) * (x + 0.044715 * (x * x * x))))


def _pack_halves(x):
    c = x.shape[1] // 2
    lo = lax.bitcast_convert_type(x[:, :c].astype(BF16).astype(F32), jnp.uint32)
    hi = lax.bitcast_convert_type(x[:, c:].astype(BF16).astype(F32), jnp.uint32)
    return lax.bitcast_convert_type((lo >> 16) | hi, jnp.int32)


def _unpack_halves(p):
    u = lax.bitcast_convert_type(p, jnp.uint32)
    lo = lax.bitcast_convert_type(u << 16, F32)
    hi = lax.bitcast_convert_type(u & jnp.uint32(0xFFFF0000), F32)
    return lo, hi


def _log_sigmoid(x):
    return jnp.minimum(x, 0.0) - jnp.log(1.0 + jnp.exp(-jnp.abs(x)))


def _even_kernel(x_ref, w_in_ref, gm_g_ref, gm_b_ref, gm_w_ref, gm_bias_ref,
                 conv_w_ref, conv_b_ref, wq_ref, wk_ref, wv_ref, wif_t_ref, bif_t_ref,
                 norm_w_ref, skip_ref, w_out_ref, ln_g_ref, ln_b_ref,
                 o_ref,
                 xm_buf, ct_ref, m_ref, *, ts, aw, bw):
    dh = bw // B_HEADS
    agd = aw // A_GROUPS
    nck = ts // CHUNK
    pad = 8
    j = pl.program_id(1)

    @pl.when(j == 0)
    def _():
        xm_buf[0:pad, :] = jnp.zeros((pad, bw), F32)
        ct_ref[...] = jnp.zeros_like(ct_ref)
        m_ref[...] = jnp.zeros_like(m_ref)

    row = lax.broadcasted_iota(jnp.int32, (CHUNK, CHUNK), 0)
    col = lax.broadcasted_iota(jnp.int32, (CHUNK, CHUNK), 1)
    causal = col <= row
    diag = col == row
    triu = jnp.where(row <= col, 1.0, 0.0).astype(BF16)
    ones_blk = jnp.ones((CHUNK, LANES), BF16)

    x = x_ref[0]
    proj = _dot(x.astype(BF16), w_in_ref[...])
    a_u = _gelu(proj[:, :aw])
    a_v = _gelu(proj[:, aw:2 * aw])
    xm = proj[:, 2 * aw:2 * aw + bw]
    z = proj[:, 2 * aw + bw:]

    vn = _ln(a_v, gm_g_ref[...], gm_b_ref[...]).astype(BF16)
    ya_chunks = []
    for c in range(nck):
        cols = []
        for g in range(A_GROUPS):
            v_cg = vn[c * CHUNK:(c + 1) * CHUNK, g * agd:(g + 1) * agd]
            cols.append(_dot(gm_w_ref[g], v_cg))
        ya_chunks.append(jnp.concatenate(cols, axis=1) + gm_bias_ref[...])
    y_a = a_u * jnp.concatenate(ya_chunks, axis=0)

    xm_buf[pad:pad + ts, :] = xm
    conv = conv_b_ref[...] + conv_w_ref[B_CONV - 1:B_CONV, :] * xm
    for k in range(B_CONV - 1):
        sh = B_CONV - 1 - k
        conv = conv + conv_w_ref[k:k + 1, :] * xm_buf[pad - sh:pad - sh + ts, :]
    xm_buf[pad - (B_CONV - 1):pad, :] = xm_buf[pad + ts - (B_CONV - 1):pad + ts, :]
    xc = _silu(conv)
    xc_b = xc.astype(BF16)
    q = _dot_head_pairs(xc_b, wq_ref)
    k_ = _dot_head_pairs(xc_b, wk_ref)
    v = _dot_head_pairs(xm.astype(BF16), wv_ref)
    gate_in = jnp.concatenate([q, k_, v], axis=1).astype(BF16)
    gates_t = _dot_nt(wif_t_ref[...], gate_in) + bif_t_ref[...]
    ig_all = gates_t[:B_HEADS, :]
    lf_all = _log_sigmoid(gates_t[B_HEADS:, :])
    q_b = q.astype(BF16)
    k_b = (k_ * dh ** -0.5).astype(BF16)
    v_b = v.astype(BF16)
    gate_z = _silu(z)

    lf_c = [lf_all[:, c * CHUNK:(c + 1) * CHUNK] for c in range(nck)]
    b_all = _split_dot(jnp.concatenate(lf_c, axis=0), triu)
    m_prev = m_ref[...]
    a_c, m_c, decay_c, wg_c = [], [], [], []
    for c in range(nck):
        b_r = b_all[c * B_HEADS:(c + 1) * B_HEADS]
        a_r = ig_all[:, c * CHUNK:(c + 1) * CHUNK] - b_r
        bl = b_r[:, CHUNK - 1:CHUNK]
        g_r = bl + a_r
        m_new = jnp.maximum(bl + m_prev, jnp.max(g_r, axis=1, keepdims=True))
        a_c.append(a_r)
        m_c.append(m_prev)
        decay_c.append(jnp.exp(bl + m_prev - m_new))
        wg_c.append(jnp.exp(g_r - m_new))
        m_prev = m_new
    m_ref[...] = m_prev

    units = [(c, h) for c in range(nck) for h in range(B_HEADS)]
    lmat = jnp.concatenate([jnp.where(causal, lf_c[c][h:h + 1, :], 0.0) for c, h in units], axis=0)
    dgm = jnp.concatenate([jnp.where(diag, wg_c[c][h:h + 1, :], 0.0) for c, h in units], axis=0)
    b_t_all = _split_dot(lmat, ones_blk)
    wg_t_all = _dot(dgm.astype(BF16), ones_blk)

    caugs = [ct_ref[h] for h in range(B_HEADS)]
    h_chunks = []
    for c in range(nck):
        sl = slice(c * CHUNK, (c + 1) * CHUNK)
        heads = []
        for h in range(B_HEADS):
            u = c * B_HEADS + h
            us = slice(u * CHUNK, (u + 1) * CHUNK)
            hs = slice(h * dh, (h + 1) * dh)
            qh, kh = q_b[sl, hs], k_b[sl, hs]
            vaug = jnp.concatenate([v_b[sl, hs], ones_blk], axis=1)
            m_row = m_c[c][h:h + 1, :]
            amat = jnp.where(causal, a_c[c][h:h + 1, :], NEG)
            mx = jnp.maximum(jnp.max(amat, axis=1, keepdims=True), m_row)
            w_intra = jnp.exp(amat - mx)
            w_state = jnp.exp(m_row - mx)
            s = _dot_nt(qh, kh) * w_intra
            kw = (kh.astype(F32) * wg_t_all[us]).astype(BF16)
            both = _dot(jnp.concatenate([s.astype(BF16), kw.T], axis=0), vaug)
            caug = caugs[h]
            naug = both[:CHUNK] + jnp.concatenate([w_state, w_state], axis=1) * _dot(qh, caug.astype(BF16))
            num, nq = naug[:, :dh], naug[:, dh:]
            hv = num / jnp.maximum(jnp.abs(nq), jnp.exp(-(b_t_all[us] + mx)))
            decay = decay_c[c][h:h + 1, :]
            caugs[h] = jnp.concatenate([decay, decay], axis=1) * caug + both[CHUNK:]
            hc = hv - jnp.mean(hv, axis=1, keepdims=True)
            hn = hc * lax.rsqrt(jnp.mean(hc * hc, axis=1, keepdims=True) + LN_EPS)
            heads.append(hn)
        h_chunks.append(jnp.concatenate(heads, axis=1))
    for h in range(B_HEADS):
        ct_ref[h] = caugs[h]
    hn_all = jnp.concatenate(h_chunks, axis=0) if nck > 1 else h_chunks[0]
    y_b = (hn_all * norm_w_ref[...] + skip_ref[...] * xc) * gate_z

    mix = jnp.concatenate([y_a, y_b], axis=1).astype(BF16)
    y = _dot(mix, w_out_ref[...])
    o_ref[0] = _ln(DN_ALPHA * x + y, ln_g_ref[...], ln_b_ref[...])


def _block_diag_pairs(w):
    hh, d, _ = w.shape
    wp = w.reshape(hh // 2, 2, d, d)
    eye = jnp.eye(2, dtype=w.dtype)
    return jnp.einsum('pade,ab->padbe', wp, eye).reshape(hh // 2, 2 * d, 2 * d)


def _dot_head_pairs(x, w_ref):
    npair, w2, _ = w_ref.shape
    return jnp.concatenate([_dot(x[:, p * w2:(p + 1) * w2], w_ref[p]) for p in range(npair)], axis=1)


def even_mixer_layer(x, w_in, gm_ln_g, gm_ln_b, gm_ws, gm_bs, conv_w, conv_b, wq, wk, wv, w_if,
                     b_if, norm_w, skip, w_out, ln_g, ln_b, *, ts):
    bsz, seq, d = x.shape
    aw = gm_ln_g.shape[0]
    bw = conv_b.shape[0]
    agd = aw // A_GROUPS
    causal = jnp.tril(jnp.ones((CHUNK, CHUNK), dtype=bool))
    gm_w = jnp.where(causal[None], gm_ws, 0.0).astype(BF16)
    gm_bias = jnp.repeat(gm_bs.T, agd, axis=1)
    row = lambda a: a.reshape(1, -1)
    args = (x, w_in.astype(BF16), row(gm_ln_g), row(gm_ln_b), gm_w, gm_bias,
            conv_w, row(conv_b), _block_diag_pairs(wq).astype(BF16), _block_diag_pairs(wk).astype(BF16),
            _block_diag_pairs(wv).astype(BF16), w_if.T.astype(BF16),
            b_if.reshape(-1, 1), row(norm_w), row(skip), w_out.astype(BF16),
            row(ln_g), row(ln_b))
    in_specs = [pl.BlockSpec((1, ts, d), lambda b, j: (b, j, 0))] + [_full(a.shape) for a in args[1:]]
    dh = bw // B_HEADS
    assert dh == LANES and CHUNK == LANES
    return pl.pallas_call(
        functools.partial(_even_kernel, ts=ts, aw=aw, bw=bw),
        grid=(bsz, seq // ts),
        in_specs=in_specs,
        out_specs=pl.BlockSpec((1, ts, d), lambda b, j: (b, j, 0)),
        out_shape=jax.ShapeDtypeStruct((bsz, seq, d), F32),
        scratch_shapes=[pltpu.VMEM((8 + ts, bw), F32),
                        pltpu.VMEM((B_HEADS, dh, dh + LANES), F32),
                        pltpu.VMEM((B_HEADS, LANES), F32)],
        compiler_params=_cparams(("arbitrary", "arbitrary")),
        name="even_mixer",
    )(*args)


def _memfold_kernel(mem_ref, wkv_ref, wq_ref, wo_ref, wqk_ref, vo_ref, *, d):
    dh = d // X_HEADS
    m_len = mem_ref.shape[1]
    kv = _dot(mem_ref[0].astype(BF16), wkv_ref[...])
    k = (kv[:, :d] * (dh ** -0.5 * LOG2E)).astype(BF16)
    v = kv[:, d:].astype(BF16)
    for h in range(X_HEADS):
        hs = slice(h * dh, (h + 1) * dh)
        ms = slice(h * m_len, (h + 1) * m_len)
        wqk_ref[0, :, ms] = _dot_nt(wq_ref[:, hs], k[:, hs]).astype(BF16)
        vo_ref[0, ms, :] = _dot(v[:, hs], wo_ref[hs, :]).astype(BF16)


def memory_fold(mem, wkv, wq, wo):
    bsz, m_len, d = mem.shape
    hm = X_HEADS * m_len
    args = (mem, wkv.astype(BF16), wq.astype(BF16), wo.astype(BF16))
    return pl.pallas_call(
        functools.partial(_memfold_kernel, d=d),
        grid=(bsz,),
        in_specs=[pl.BlockSpec((1, m_len, d), lambda b: (b, 0, 0))] + [_full(a.shape) for a in args[1:]],
        out_specs=[pl.BlockSpec((1, d, hm), lambda b: (b, 0, 0)),
                   pl.BlockSpec((1, hm, d), lambda b: (b, 0, 0))],
        out_shape=[jax.ShapeDtypeStruct((bsz, d, hm), BF16),
                   jax.ShapeDtypeStruct((bsz, hm, d), BF16)],
        compiler_params=_cparams(("arbitrary",)),
        name="memory_fold",
    )(*args)


ROUTE_W = 128
XATTN_TS = 1024
DEST_ROWS = 8
SLOT_TB = 2048
SLOT_SUB = 512


def _xattn_kernel(x_ref, wqk_ref, vo_ref, ln_g_ref, ln_b_ref, wr_ref, br_ref,
                  o_ref, xp_ref, route_ref, cnt_ref):
    m_len = wqk_ref.shape[2] // X_HEADS

    @pl.when(pl.program_id(1) == 0)
    def _():
        cnt_ref[...] = jnp.zeros_like(cnt_ref)

    x = x_ref[0]
    ts = x.shape[0]
    xb = x.astype(BF16)
    probs = []
    for h in range(X_HEADS):
        s = _dot(xb, wqk_ref[0, :, h * m_len:(h + 1) * m_len])
        p = jnp.exp2(s - jnp.max(s, axis=1, keepdims=True))
        probs.append((p / jnp.sum(p, axis=1, keepdims=True)).astype(BF16))
    y = _dot(jnp.concatenate(probs, axis=1), vo_ref[0])
    x2 = _ln(DN_ALPHA * x + y, ln_g_ref[...], ln_b_ref[...])
    o_ref[0] = x2
    xp_ref[0] = _pack_halves(x2)

    x_hi = x2.astype(BF16)
    x_lo = (x2 - x_hi.astype(F32)).astype(BF16)
    logits = (_dot(x_hi, wr_ref[0]) + _dot(x_lo, wr_ref[0]) + _dot(x_hi, wr_ref[1])) + br_ref[...]
    lane = lax.broadcasted_iota(jnp.int32, (ts, ROUTE_W), 1)
    is_g = lane < N_GROUPS
    lg = jnp.where(is_g, logits, NEG)
    mg = jnp.max(lg, axis=1, keepdims=True)
    gi = jnp.min(jnp.where(jnp.logical_and(is_g, lg == mg), lane, ROUTE_W), axis=1, keepdims=True)
    gate_g = 1.0 / jnp.sum(jnp.where(is_g, jnp.exp(lg - mg), 0.0), axis=1, keepdims=True)
    lo = N_GROUPS + gi * EXPERTS_PER_GROUP
    in_grp = jnp.logical_and(lane >= lo, lane < lo + EXPERTS_PER_GROUP)
    le = jnp.where(in_grp, logits, NEG)
    v1 = jnp.max(le, axis=1, keepdims=True)
    i1 = jnp.min(jnp.where(jnp.logical_and(in_grp, le == v1), lane, ROUTE_W), axis=1, keepdims=True)
    le2 = jnp.where(lane == i1, NEG, le)
    v2 = jnp.max(le2, axis=1, keepdims=True)
    i2 = jnp.min(jnp.where(jnp.logical_and(in_grp, le2 == v2), lane, ROUTE_W), axis=1, keepdims=True)
    e21 = jnp.exp(v2 - v1)
    p1 = 1.0 / (1.0 + e21)
    p2 = e21 * p1
    e1 = (i1 - N_GROUPS).astype(F32)
    e2 = (i2 - N_GROUPS).astype(F32)
    rec = jnp.where(lane == 0, e1, 0.0)
    rec = jnp.where(lane == 1, e2, rec)
    rec = jnp.where(lane == 2, gate_g * p1, rec)
    rec = jnp.where(lane == 3, gate_g * p2, rec)
    route_ref[0] = rec
    sel = jnp.logical_or(lane == i1 - N_GROUPS, lane == i2 - N_GROUPS)
    cnt_ref[0] += jnp.sum(jnp.where(sel, 1.0, 0.0), axis=0, keepdims=True)


def xattn_router_layer(x, wqk, vo, ln_g, ln_b, w_rg, b_rg, w_re, b_re, *, ts):
    bsz, seq, d = x.shape
    hm = wqk.shape[2]
    wr = jnp.zeros((d, ROUTE_W), F32).at[:, :N_GROUPS].set(w_rg).at[:, N_GROUPS:N_GROUPS + N_EXPERTS].set(w_re)
    br = jnp.zeros((1, ROUTE_W), F32).at[0, :N_GROUPS].set(b_rg).at[0, N_GROUPS:N_GROUPS + N_EXPERTS].set(b_re)
    wr_hi = wr.astype(BF16)
    wr_lo = (wr - wr_hi.astype(F32)).astype(BF16)
    wr = jnp.stack([wr_hi, wr_lo])
    args = (x, wqk, vo, ln_g.reshape(1, -1), ln_b.reshape(1, -1), wr, br)
    in_specs = [pl.BlockSpec((1, ts, d), lambda b, j: (b, j, 0)),
                pl.BlockSpec((1, d, hm), lambda b, j: (b, 0, 0)),
                pl.BlockSpec((1, hm, d), lambda b, j: (b, 0, 0))] + [_full(a.shape) for a in args[3:]]
    return pl.pallas_call(
        _xattn_kernel,
        grid=(bsz, seq // ts),
        in_specs=in_specs,
        out_specs=[pl.BlockSpec((1, ts, d), lambda b, j: (b, j, 0)),
                   pl.BlockSpec((1, ts, d // 2), lambda b, j: (b, j, 0)),
                   pl.BlockSpec((1, ts, ROUTE_W), lambda b, j: (b, j, 0)),
                   pl.BlockSpec((1, 1, ROUTE_W), lambda b, j: (b, 0, 0))],
        out_shape=[jax.ShapeDtypeStruct((bsz, seq, d), F32),
                   jax.ShapeDtypeStruct((bsz, seq, d // 2), jnp.int32),
                   jax.ShapeDtypeStruct((bsz, seq, ROUTE_W), F32),
                   jax.ShapeDtypeStruct((bsz, 1, ROUTE_W), F32)],
        compiler_params=_cparams(("arbitrary", "arbitrary")),
        name="xattn_router",
    )(*args)


def _moe_output(x2, g0, g1, rec, ln_g, ln_b):
    a_lo, a_hi = _unpack_halves(g0)
    b_lo, b_hi = _unpack_halves(g1)
    w0, w1 = rec[:, 2:3], rec[:, 3:4]
    y = jnp.concatenate([w0 * a_lo + w1 * b_lo, w0 * a_hi + w1 * b_hi], axis=1)
    return _ln(DN_ALPHA * x2 + y, ln_g, ln_b)


def _swa_kernel(x_ref, g0_ref, g1_ref, route_ref, pln_g_ref, pln_b_ref,
                wqkv_ref, bqkv_ref, cos_ref, sin_ref, sink_ref, wo_ref, ln_g_ref, ln_b_ref,
                *rest, ts, cq, ckv):
    o_ref, kprev, vprev = rest[-3:]
    j = pl.program_id(1)
    nb = ts // CHUNK
    dh = C_HEAD_DIM
    grp = (cq // dh) // C_KV_HEADS

    @pl.when(j == 0)
    def _():
        kprev[...] = jnp.zeros_like(kprev)
        vprev[...] = jnp.zeros_like(vprev)

    x = _moe_output(x_ref[0], g0_ref[0, 0], g1_ref[0, 0], route_ref[0], pln_g_ref[...], pln_b_ref[...])
    qkv = _dot(x.astype(BF16), wqkv_ref[...]) + bqkv_ref[...]
    cos = cos_ref[...]
    sin = sin_ref[...]
    lane = lax.broadcasted_iota(jnp.int32, (ts, LANES), 1)
    first_half = (lane % dh) < (dh // 2)

    def rope(t):
        outs = []
        for c in range(t.shape[1] // LANES):
            tc = t[:, c * LANES:(c + 1) * LANES]
            rot = jnp.where(first_half, pltpu.roll(tc, LANES - dh // 2, 1), pltpu.roll(tc, dh // 2, 1))
            outs.append(tc * cos + rot * sin)
        return jnp.concatenate(outs, axis=1)

    q = rope(qkv[:, :cq]).astype(BF16)
    k = rope(qkv[:, cq:cq + 2 * ckv]).astype(BF16)
    v = qkv[:, cq + 2 * ckv:].astype(BF16)

    r_i = lax.broadcasted_iota(jnp.int32, (CHUNK, 2 * CHUNK), 0)
    c_i = lax.broadcasted_iota(jnp.int32, (CHUNK, 2 * CHUNK), 1)
    band = jnp.logical_and(c_i > r_i, c_i <= r_i + CHUNK)
    sink_col = c_i == 0
    lane_k = lax.broadcasted_iota(jnp.int32, (2 * CHUNK, LANES), 1)
    key_row = lax.broadcasted_iota(jnp.int32, (2 * CHUNK, LANES), 0)
    lane_q = lax.broadcasted_iota(jnp.int32, (CHUNK, LANES), 1)
    ones_blk = jnp.ones((2 * CHUNK, LANES), BF16)
    zero_b = jnp.zeros((), BF16)
    blocks = []
    for c in range(nb):
        sl = slice(c * CHUNK, (c + 1) * CHUNK)
        if c == 0:
            kb = jnp.concatenate([kprev[...].astype(BF16), k[sl]], axis=0)
            vb = jnp.concatenate([vprev[...].astype(BF16), v[sl]], axis=0)
            first_key = jnp.where(j > 0, 0, CHUNK)
            valid = jnp.logical_and(band, c_i >= first_key)
        else:
            kb = k[(c - 1) * CHUNK:(c + 1) * CHUNK]
            vb = v[(c - 1) * CHUNK:(c + 1) * CHUNK]
            valid = band
        tiles = []
        for h in range(C_KV_HEADS):
            kd = kb[:, h * LANES:(h + 1) * LANES]
            vd = vb[:, h * LANES:(h + 1) * LANES]
            k_lo = jnp.where(lane_k < dh, kd, zero_b)
            k_hi = jnp.where(lane_k >= dh, kd, zero_b)
            vz = jnp.where(key_row == 0, zero_b, vd)
            q2 = jnp.concatenate([q[sl, (2 * h) * LANES:(2 * h + 1) * LANES],
                                  q[sl, (2 * h + 1) * LANES:(2 * h + 2) * LANES]], axis=0)
            s_lo = _dot_nt(q2, k_lo)
            s_hi = _dot_nt(q2, k_hi)
            parts = []
            for qk, g in ((s_lo[:CHUNK], 0), (s_lo[CHUNK:], 2), (s_hi[:CHUNK], 1), (s_hi[CHUNK:], 3)):
                sink = sink_ref[h * grp + g]
                parts.append(jnp.where(valid, qk, jnp.where(sink_col, sink, NEG)))
            s = jnp.concatenate(parts, axis=0)
            p = jnp.exp2(s - jnp.max(s, axis=1, keepdims=True)).astype(BF16)
            den = _dot(p, ones_blk)
            o2 = _dot(p, vz) / den
            tiles.append(jnp.where(lane_q < dh, o2[:CHUNK], o2[2 * CHUNK:3 * CHUNK]))
            tiles.append(jnp.where(lane_q < dh, o2[CHUNK:2 * CHUNK], o2[3 * CHUNK:]))
        blocks.append(jnp.concatenate(tiles, axis=1))
    kprev[...] = k[(nb - 1) * CHUNK:].astype(F32)
    vprev[...] = v[(nb - 1) * CHUNK:].astype(F32)
    att = (jnp.concatenate(blocks, axis=0) if nb > 1 else blocks[0]).astype(BF16)
    y = _dot(att, wo_ref[...])
    o_ref[0] = _ln(DN_ALPHA * x + y, ln_g_ref[...], ln_b_ref[...])


def swa_prepare(w_qkv, b_qkv, sinks, w_o, seq):
    cq = w_o.shape[0]
    ckv = (w_qkv.shape[1] - cq) // 2
    dh = C_HEAD_DIM
    inv = ROPE_THETA ** (-jnp.arange(0, dh, 2, dtype=F32) / dh)
    ang = jnp.arange(seq, dtype=F32)[:, None] * inv[None, :]
    reps = LANES // (dh // 2)
    sign = jnp.tile(jnp.concatenate([-jnp.ones((dh // 2,), F32), jnp.ones((dh // 2,), F32)]), LANES // dh)
    cos_t = jnp.tile(jnp.cos(ang), (1, reps))
    sin_t = jnp.tile(jnp.sin(ang), (1, reps)) * sign[None, :]
    assert 2 * dh == LANES

    def dup_heads(t):
        th = t.reshape(t.shape[:-1] + (ckv // dh, dh))
        return jnp.concatenate([th, th], axis=-1).reshape(t.shape[:-1] + (2 * ckv,))

    qs = dh ** -0.5 * LOG2E
    w_all = jnp.concatenate([w_qkv[:, :cq] * qs, dup_heads(w_qkv[:, cq:cq + ckv]),
                             dup_heads(w_qkv[:, cq + ckv:])], axis=1)
    b_all = jnp.concatenate([b_qkv[:cq] * qs, dup_heads(b_qkv[cq:cq + ckv]), dup_heads(b_qkv[cq + ckv:])])
    return w_all.astype(BF16), b_all.reshape(1, -1), cos_t, sin_t, sinks.astype(F32) * LOG2E, w_o.astype(BF16)


def swa_mixer_layer(pending, prepared, ln_g, ln_b, *, ts):
    x, gs, route, pln_g, pln_b = pending
    bsz, seq, d = x.shape
    cq = prepared[5].shape[0]
    ckv = (prepared[0].shape[1] - cq) // 4
    shared = (pln_g.reshape(1, -1), pln_b.reshape(1, -1), *prepared, ln_g.reshape(1, -1), ln_b.reshape(1, -1))
    out = None
    b_next = 0
    for g in gs:
        bp = g.shape[1] // seq
        b0, b_next = b_next, b_next + bp
        g = g.reshape(2, bp, seq, d // 2)
        args = [x, g, g, route, *shared]
        in_specs = [pl.BlockSpec((1, ts, d), lambda b, j: (b0 + b, j, 0)),
                    pl.BlockSpec((1, 1, ts, d // 2), lambda b, j: (0, b, j, 0)),
                    pl.BlockSpec((1, 1, ts, d // 2), lambda b, j: (1, b, j, 0)),
                    pl.BlockSpec((1, ts, ROUTE_W), lambda b, j: (b0 + b, j, 0)),
                    _full((1, d)), _full((1, d)),
                    _full(shared[2].shape), _full(shared[3].shape),
                    pl.BlockSpec((ts, LANES), lambda b, j: (j, 0)),
                    pl.BlockSpec((ts, LANES), lambda b, j: (j, 0)),
                    pl.BlockSpec(memory_space=pltpu.SMEM),
                    _full(shared[7].shape), _full(shared[8].shape), _full(shared[9].shape)]
        aliases = {}
        if out is not None:
            args.append(out)
            in_specs.append(pl.BlockSpec(memory_space=pl.ANY))
            aliases = {len(args) - 1: 0}
        out = pl.pallas_call(
            functools.partial(_swa_kernel, ts=ts, cq=cq, ckv=ckv),
            grid=(bp, seq // ts),
            in_specs=in_specs,
            out_specs=pl.BlockSpec((1, ts, d), lambda b, j: (b0 + b, j, 0)),
            out_shape=jax.ShapeDtypeStruct((bsz, seq, d), F32),
            input_output_aliases=aliases,
            scratch_shapes=[pltpu.VMEM((CHUNK, 2 * ckv), F32), pltpu.VMEM((CHUNK, 2 * ckv), F32)],
            compiler_params=_cparams(("arbitrary", "arbitrary")),
            name="swa_mixer",
        )(*args)
    return out


def _slot_kernel(route_ref, pstart_ref, dest_ref, carry_ref, *, tb):
    @pl.when(pl.program_id(0) == 0)
    def _():
        carry_ref[...] = pstart_ref[...]

    sub = min(SLOT_SUB, tb)
    lane = lax.broadcasted_iota(jnp.int32, (sub, ROUTE_W), 1)
    r = lax.broadcasted_iota(jnp.int32, (sub, sub), 0)
    c = lax.broadcasted_iota(jnp.int32, (sub, sub), 1)
    before = jnp.where(c < r, 1.0, 0.0).astype(BF16)
    carry = carry_ref[...]
    for k in range(tb // sub):
        rows = slice(k * sub, (k + 1) * sub)
        rec = route_ref[rows, :]
        e0 = rec[:, 0:1].astype(jnp.int32)
        e1 = rec[:, 1:2].astype(jnp.int32)
        oh0 = lane == e0
        oh1 = lane == e1
        ohs = jnp.where(jnp.logical_or(oh0, oh1), 1.0, 0.0)
        prefix = _dot(before, ohs.astype(BF16)) + carry
        d0 = jnp.sum(jnp.where(oh0, prefix, 0.0), axis=1, keepdims=True)
        d1 = jnp.sum(jnp.where(oh1, prefix, 0.0), axis=1, keepdims=True)
        dest = jnp.where(lane == 0, d0, jnp.where(lane == 1, d1, 0.0))
        dest_ref[:, rows] = dest.T[:DEST_ROWS].astype(jnp.int32)
        carry = carry + jnp.sum(ohs, axis=0, keepdims=True)
    carry_ref[...] = carry


def moe_slots(route, pstart, *, tb):
    n = route.shape[0]
    return pl.pallas_call(
        functools.partial(_slot_kernel, tb=tb),
        grid=(n // tb,),
        in_specs=[pl.BlockSpec((tb, ROUTE_W), lambda i: (i, 0)), _full((1, ROUTE_W))],
        out_specs=pl.BlockSpec((DEST_ROWS, tb), lambda i: (0, i)),
        out_shape=jax.ShapeDtypeStruct((DEST_ROWS, n), jnp.int32),
        scratch_shapes=[pltpu.VMEM((1, ROUTE_W), F32)],
        compiler_params=_cparams(("arbitrary",)),
        name="moe_slots",
    )(route, pstart)


def _ffn_kernel(blk_exp_ref, new_exp_ref, nblk_ref, xs_ref, w1_ref, w3_ref, w2_ref, ys_ref,
                w1_b, w3_b, w2_b):
    i = pl.program_id(0)
    used = i < nblk_ref[0]

    @pl.when(jnp.logical_and(used, new_exp_ref[i] == 1))
    def _():
        w1_b[...] = w1_ref[0, 0].astype(BF16)
        w3_b[...] = w3_ref[0, 0].astype(BF16)
        w2_b[...] = w2_ref[0, 0].astype(BF16)

    @pl.when(used)
    def _():
        x_lo, x_hi = _unpack_halves(xs_ref[...])
        x_lo, x_hi = x_lo.astype(BF16), x_hi.astype(BF16)
        dl = x_lo.shape[1]
        h1 = _dot(x_lo, w1_b[:dl, :]) + _dot(x_hi, w1_b[dl:, :])
        h3 = _dot(x_lo, w3_b[:dl, :]) + _dot(x_hi, w3_b[dl:, :])
        h = (_silu(h1) * h3).astype(BF16)
        ys_ref[...] = _pack_halves(_dot(h, w2_b[...]))

    @pl.when(jnp.logical_not(used))
    def _():
        ys_ref[...] = jnp.zeros_like(ys_ref)


def moe_ffn(xs, blk_exp, new_exp, nblk, w1, w3, w2, *, layer, bm):
    n_pad, dp = xs.shape
    d = 2 * dp
    de = w1.shape[3]
    n_blk = n_pad // bm

    def x_map(i, be, ne, nb):
        return (jnp.minimum(i, nb[0] - 1), 0)

    def w_map(i, be, ne, nb):
        return (layer, be[i], 0, 0)

    return pl.pallas_call(
        _ffn_kernel,
        grid_spec=pltpu.PrefetchScalarGridSpec(
            num_scalar_prefetch=3,
            grid=(n_blk,),
            in_specs=[pl.BlockSpec((bm, dp), x_map),
                      pl.BlockSpec((1, 1, d, de), w_map),
                      pl.BlockSpec((1, 1, d, de), w_map),
                      pl.BlockSpec((1, 1, de, d), w_map)],
            out_specs=pl.BlockSpec((bm, dp), lambda i, be, ne, nb: (i, 0)),
            scratch_shapes=[pltpu.VMEM((d, de), BF16), pltpu.VMEM((d, de), BF16),
                            pltpu.VMEM((de, d), BF16)]),
        out_shape=jax.ShapeDtypeStruct((n_pad, dp), jnp.int32),
        compiler_params=_cparams(("arbitrary",)),
        name="moe_ffn",
    )(blk_exp, new_exp, nblk, xs, w1, w3, w2)


def _combine_kernel(x_ref, g0_ref, g1_ref, route_ref, ln_g_ref, ln_b_ref, *rest):
    o_ref = rest[-1]
    o_ref[...] = _moe_output(x_ref[...], g0_ref[0], g1_ref[0], route_ref[...], ln_g_ref[...], ln_b_ref[...])


def moe_combine(x, g, route, ln_g, ln_b, *, row0, tb, prev=None):
    n, d = x.shape
    m = g.shape[1]
    blk0 = row0 // tb
    row_spec = pl.BlockSpec((tb, d), lambda i: (blk0 + i, 0))
    args = [x, g, g, route, ln_g.reshape(1, -1), ln_b.reshape(1, -1)]
    in_specs = [row_spec,
                pl.BlockSpec((1, tb, d // 2), lambda i: (0, i, 0)),
                pl.BlockSpec((1, tb, d // 2), lambda i: (1, i, 0)),
                pl.BlockSpec((tb, ROUTE_W), lambda i: (blk0 + i, 0)),
                _full((1, d)), _full((1, d))]
    aliases = {}
    if prev is not None:
        args.append(prev)
        in_specs.append(pl.BlockSpec(memory_space=pl.ANY))
        aliases = {len(args) - 1: 0}
    return pl.pallas_call(
        _combine_kernel,
        grid=(m // tb,),
        in_specs=in_specs,
        out_specs=row_spec,
        out_shape=jax.ShapeDtypeStruct((n, d), F32),
        input_output_aliases=aliases,
        compiler_params=_cparams(("arbitrary",)),
        name="moe_combine",
    )(*args)


MOE_BM = 1024
COMBINE_PARTS = 4
COMBINE_TB = 1024
MIXER_FIRST_DIV = 4
MIXER_TS = 1024


def hierarchical_moe_layer(x2, xp, route, counts, w1, w3, w2, ln_g, ln_b, *, layer, defer_combine,
                           side_work=None, side_inputs=None):
    bsz, seq, d = x2.shape
    n = bsz * seq
    bm = MOE_BM
    rt = route.reshape(n, ROUTE_W)
    tb = min(512, n)
    n_blk = (2 * n) // bm + N_EXPERTS
    cnt = counts.sum(axis=0)[0, :N_EXPERTS].astype(jnp.int32)
    pcnt = (cnt + bm - 1) // bm * bm
    pends = jnp.cumsum(pcnt)
    pstart = pends - pcnt
    nblk = (pends[-1] // bm).astype(jnp.int32).reshape(1)
    blk_row = jnp.arange(n_blk, dtype=jnp.int32) * bm
    blk_exp = jnp.minimum(jnp.sum((pends[None, :] <= blk_row[:, None]).astype(jnp.int32), axis=1),
                          N_EXPERTS - 1)
    last_exp = blk_exp[jnp.maximum(nblk[0] - 1, 0)]
    blk_exp = jnp.where(jnp.arange(n_blk) < nblk[0], blk_exp, last_exp)
    new_exp = jnp.concatenate([jnp.ones((1,), jnp.int32),
                               (blk_exp[1:] != blk_exp[:-1]).astype(jnp.int32)])
    pstart_rec = jnp.zeros((1, ROUTE_W), F32).at[0, :N_EXPERTS].set(pstart.astype(F32))
    dest = moe_slots(rt, pstart_rec, tb=min(SLOT_TB, n))
    xs = moe_dispatch(xp.reshape(n, d // 2), dest[0], dest[1], n_blk * bm)
    side = None
    if side_work is not None:
        side_inputs, _ = lax.optimization_barrier((side_inputs, dest))
        nblk, side = lax.optimization_barrier((nblk, side_work(*side_inputs)))
    ys = moe_ffn(xs, blk_exp, new_exp, nblk, w1, w3, w2, layer=layer, bm=bm)
    if defer_combine:
        b_first = bsz // MIXER_FIRST_DIV
        cuts = [0, b_first * seq, n] if 0 < b_first < bsz else [0, n]
        gs = [moe_gather(ys, dest[0, lo:hi], dest[1, lo:hi]) for lo, hi in zip(cuts[:-1], cuts[1:])]
        return (x2, gs, route, ln_g, ln_b), side
    parts = COMBINE_PARTS if n % (2 * COMBINE_PARTS * tb * 8) == 0 else 1
    m = n // parts
    cuts = [0] + [p * m + m // 2 for p in range(parts - 1)] + [n]
    out = None
    for lo, hi in zip(cuts[:-1], cuts[1:]):
        g = moe_gather(ys, dest[0, lo:hi], dest[1, lo:hi])
        out = moe_combine(x2.reshape(n, d), g, rt, ln_g, ln_b, row0=lo, tb=min(COMBINE_TB, m // 2), prev=out)
    return out.reshape(bsz, seq, d)


SC_ROWS = 128


def _sc_mesh():
    return plsc.VectorSubcoreMesh(core_axis_name="c", subcore_axis_name="s")


def moe_dispatch(xf, dest0, dest1, n_pad):
    n, d = xf.shape
    info = plsc.get_sparse_core_info()
    nw = info.num_cores * info.num_subcores
    per_w = n // nw
    r = min(SC_ROWS, per_w)

    def body(x_hbm, d0_hbm, d1_hbm, xs_hbm, i0_v, i1_v, rows_v, sem):
        wid = lax.axis_index("s") * info.num_cores + lax.axis_index("c")

        @pl.loop(0, per_w // r)
        def _(c):
            base = pl.multiple_of(wid * per_w + c * r, 8)
            pltpu.sync_copy(d0_hbm.at[pl.ds(base, r)], i0_v)
            pltpu.sync_copy(d1_hbm.at[pl.ds(base, r)], i1_v)
            pltpu.sync_copy(x_hbm.at[pl.ds(base, r)], rows_v)
            pltpu.async_copy(rows_v, xs_hbm.at[i0_v], sem).wait()
            pltpu.async_copy(rows_v, xs_hbm.at[i1_v], sem).wait()

    return pl.kernel(
        body, out_type=jax.ShapeDtypeStruct((n_pad, d), xf.dtype), mesh=_sc_mesh(),
        scratch_types=[pltpu.VMEM((r,), jnp.int32), pltpu.VMEM((r,), jnp.int32),
                       pltpu.VMEM((r, d), xf.dtype), pltpu.SemaphoreType.DMA],
        name="moe_dispatch",
    )(xf, dest0, dest1)


def moe_gather(ys, dest0, dest1):
    n = dest0.shape[0]
    d = ys.shape[1]
    info = plsc.get_sparse_core_info()
    nw = info.num_cores * info.num_subcores
    per_w = n // nw
    r = min(SC_ROWS, per_w)

    def body(ys_hbm, d0_hbm, d1_hbm, g_hbm, i_v, rows_v, sem):
        wid = lax.axis_index("s") * info.num_cores + lax.axis_index("c")

        @pl.loop(0, per_w // r)
        def _(c):
            base = pl.multiple_of(wid * per_w + c * r, 8)
            for k, d_hbm in enumerate((d0_hbm, d1_hbm)):
                pltpu.sync_copy(d_hbm.at[pl.ds(base, r)], i_v)
                pltpu.async_copy(ys_hbm.at[i_v], rows_v, sem).wait()
                pltpu.sync_copy(rows_v, g_hbm.at[k, pl.ds(base, r)])

    return pl.kernel(
        body, out_type=jax.ShapeDtypeStruct((2, n, d), ys.dtype), mesh=_sc_mesh(),
        scratch_types=[pltpu.VMEM((r,), jnp.int32), pltpu.VMEM((r, d), ys.dtype),
                       pltpu.SemaphoreType.DMA],
        name="moe_gather",
    )(ys, dest0, dest1)


def kernel(x, mem, ln_g, ln_b, ev_w_in, ev_gm_ln_g, ev_gm_ln_b, ev_gm_ws, ev_gm_bs, ev_conv_w, ev_conv_b, ev_wq, ev_wk, ev_wv, ev_w_if, ev_b_if, ev_norm_w, ev_skip, ev_w_out, od_w_qkv, od_b_qkv, od_sinks, od_w_o, xa_wq, xa_wkv, xa_wo, moe_w_rg, moe_b_rg, moe_w_re, moe_b_re, moe_w1, moe_w3, moe_w2):
    bsz, seq, d = x.shape
    depth = ln_g.shape[0]
    assert depth == DEPTH
    ts = min(MIXER_TS, seq)

    folded = None
    for l in range(depth):
        wqk, vo = folded if folded is not None else memory_fold(mem, xa_wkv[l], xa_wq[l], xa_wo[l])
        if l % 2 == 0:
            e = l // 2
            x = even_mixer_layer(x, ev_w_in[e], ev_gm_ln_g[e], ev_gm_ln_b[e], ev_gm_ws[e], ev_gm_bs[e],
                                 ev_conv_w[e], ev_conv_b[e], ev_wq[e], ev_wk[e], ev_wv[e], ev_w_if[e],
                                 ev_b_if[e], ev_norm_w[e], ev_skip[e], ev_w_out[e],
                                 ln_g[l, 0], ln_b[l, 0], ts=ts)
        else:
            o = l // 2
            prepared = swa_prepare(od_w_qkv[o], od_b_qkv[o], od_sinks[o], od_w_o[o], seq)
            x = swa_mixer_layer(x, prepared, ln_g[l, 0], ln_b[l, 0], ts=ts)
        x, xp, route, counts = xattn_router_layer(x, wqk, vo, ln_g[l, 1], ln_b[l, 1],
                                                  moe_w_rg[l], moe_b_rg[l], moe_w_re[l], moe_b_re[l],
                                                  ts=min(XATTN_TS, seq))
        defer = l + 1 < depth and (l + 1) % 2 == 1
        side_inputs = (mem, xa_wkv[l + 1], xa_wq[l + 1], xa_wo[l + 1]) if defer else None
        x = hierarchical_moe_layer(x, xp, route, counts, moe_w1, moe_w3, moe_w2,
                                   ln_g[l, 2], ln_b[l, 2], layer=l, defer_combine=defer,
                                   side_work=memory_fold if defer else None, side_inputs=side_inputs)
        folded = None
        if defer:
            x, folded = x
    return x
```
